```python
import jax, jax.numpy as jnp
from jax import lax
import numpy as np

D_MODEL = 1024
BATCH = 8
SEQ = 4096
DEPTH = 1

CHUNK = 64
HEAD_DIM = 64
ATTN_WIDTH = D_MODEL // 2
CONV_WIDTH = D_MODEL - ATTN_WIDTH
N_ATTN_HEADS = ATTN_WIDTH // HEAD_DIM
N_CONV_GROUPS = CONV_WIDTH // HEAD_DIM
CONV_KERNEL = 3
Q_BLOCK = 128
D_FF = -(-8 * D_MODEL // (3 * 256)) * 256
IN_WIDTH = 3 * ATTN_WIDTH + N_ATTN_HEADS + 3 * CONV_WIDTH
EPS = 1e-6
FORGET_BIAS_INIT = 3.0

kernel_name = "hymba_fox_shortconv_sandwich_block"


def rms_norm(x, g):
    xf = x.astype(jnp.float32)
    y = xf * lax.rsqrt(jnp.mean(xf * xf, axis=-1, keepdims=True) + EPS)
    return (y * g.astype(jnp.float32)).astype(x.dtype)


def group_rms_norm(y, g, n_groups):
    b, s, w = y.shape
    yf = y.astype(jnp.float32).reshape(b, s, n_groups, w // n_groups)
    yf = yf * lax.rsqrt(jnp.mean(yf * yf, axis=-1, keepdims=True) + EPS)
    return (yf.reshape(b, s, w) * g.astype(jnp.float32)).astype(y.dtype)


def forgetting_attention(q, k, v, log_f):
    b, s, h, dh = q.shape
    nb = s // Q_BLOCK
    c = jnp.transpose(jnp.cumsum(log_f.astype(jnp.float32), axis=1), (0, 2, 1))
    qf = jnp.transpose(q, (0, 2, 1, 3)).astype(jnp.float32) * (dh ** -0.5)
    kf = jnp.transpose(k, (0, 2, 1, 3)).astype(jnp.float32)
    vf = jnp.transpose(v, (0, 2, 1, 3)).astype(jnp.float32)
    qb = qf.reshape(b, h, nb, Q_BLOCK, dh).transpose(2, 0, 1, 3, 4)
    cqb = c.reshape(b, h, nb, Q_BLOCK).transpose(2, 0, 1, 3)
    key_pos = jnp.arange(s)

    def one_block(args):
        i, q_blk, cq_blk = args
        q_pos = i * Q_BLOCK + jnp.arange(Q_BLOCK)
        sc = jnp.einsum('bhqd,bhkd->bhqk', q_blk, kf) + cq_blk[..., :, None] - c[:, :, None, :]
        sc = jnp.where(key_pos[None, :] <= q_pos[:, None], sc, -jnp.inf)
        p = jax.nn.softmax(sc, axis=-1)
        return jnp.einsum('bhqk,bhkd->bhqd', p, vf)

    out = lax.map(one_block, (jnp.arange(nb), qb, cqb))
    out = out.transpose(1, 0, 3, 2, 4).reshape(b, s, h * dh)
    return out.astype(q.dtype)


def causal_depthwise_conv(u, w):
    kw = w.shape[0]
    s = u.shape[1]
    up = jnp.pad(u, ((0, 0), (kw - 1, 0), (0, 0)))
    out = up[:, 0:s, :] * w[0]
    for j in range(1, kw):
        out = out + up[:, j:j + s, :] * w[j]
    return out


def _fwd_setup_inputs(seed: int = 0) -> dict:
    key = jax.random.key(seed)
    ks = jax.random.split(key, 14)
    f32 = jnp.float32

    def gain(k):
        return 1.0 + 0.05 * jax.random.normal(k, (DEPTH, D_MODEL), f32)

    return {
        "x": jax.random.normal(ks[0], (BATCH, SEQ, D_MODEL), f32),
        "w_in": jax.random.normal(ks[1], (DEPTH, D_MODEL, IN_WIDTH), f32) * D_MODEL ** -0.5,
        "b_forget": FORGET_BIAS_INIT + 0.5 * jax.random.normal(ks[2], (DEPTH, N_ATTN_HEADS), f32),
        "conv_w": jax.random.normal(ks[3], (DEPTH, CONV_KERNEL, CONV_WIDTH), f32) * CONV_KERNEL ** -0.5,
        "g_attn_out": 1.0 + 0.05 * jax.random.normal(ks[4], (DEPTH, ATTN_WIDTH), f32),
        "g_conv_out": 1.0 + 0.05 * jax.random.normal(ks[5], (DEPTH, CONV_WIDTH), f32),
        "w_out": jax.random.normal(ks[6], (DEPTH, D_MODEL, D_MODEL), f32) * D_MODEL ** -0.5,
        "g_mix_pre": gain(ks[7]),
        "g_mix_post": gain(ks[8]),
        "w_gate_up": jax.random.normal(ks[9], (DEPTH, D_MODEL, 2 * D_FF), f32) * D_MODEL ** -0.5,
        "w_down": jax.random.normal(ks[10], (DEPTH, D_FF, D_MODEL), f32) * D_FF ** -0.5,
        "g_ffn_pre": gain(ks[11]),
        "g_ffn_post": gain(ks[12]),
    }


def _fwd_reference(x, w_in, b_forget, conv_w, g_attn_out, g_conv_out, w_out,
              g_mix_pre, g_mix_post, w_gate_up, w_down, g_ffn_pre, g_ffn_post):
    b, s, _ = x.shape
    splits = [ATTN_WIDTH, 2 * ATTN_WIDTH, 3 * ATTN_WIDTH,
              3 * ATTN_WIDTH + N_ATTN_HEADS,
              3 * ATTN_WIDTH + N_ATTN_HEADS + CONV_WIDTH,
              3 * ATTN_WIDTH + N_ATTN_HEADS + 2 * CONV_WIDTH]
    for l in range(DEPTH):
        h = rms_norm(x, g_mix_pre[l])
        proj = jnp.einsum('bsd,de->bse', h, w_in[l])
        q, k, v, f_logit, gate_b, gate_c, u = jnp.split(proj, splits, axis=-1)
        log_f = jax.nn.log_sigmoid(f_logit.astype(jnp.float32) + b_forget[l].astype(jnp.float32))
        attn = forgetting_attention(q.reshape(b, s, N_ATTN_HEADS, HEAD_DIM),
                                    k.reshape(b, s, N_ATTN_HEADS, HEAD_DIM),
                                    v.reshape(b, s, N_ATTN_HEADS, HEAD_DIM), log_f)
        conv = gate_b * causal_depthwise_conv(gate_c * u, conv_w[l])
        merged = jnp.concatenate([group_rms_norm(attn, g_attn_out[l], N_ATTN_HEADS),
                                  group_rms_norm(conv, g_conv_out[l], N_CONV_GROUPS)], axis=-1)
        y = jnp.einsum('bse,ed->bsd', merged, w_out[l])
        x = x + rms_norm(y, g_mix_post[l])
        h = rms_norm(x, g_ffn_pre[l])
        gu = jnp.einsum('bsd,df->bsf', h, w_gate_up[l])
        g, up = jnp.split(gu, [D_FF], axis=-1)
        ff = jnp.einsum('bsf,fd->bsd', jax.nn.silu(g) * up, w_down[l])
        x = x + rms_norm(ff, g_ffn_post[l])
    return x


import jax as _jax
import jax.numpy as _jnp

TWIN_FORMAT = 'train_step'
FWD_PARAMS = ['x', 'w_in', 'b_forget', 'conv_w', 'g_attn_out', 'g_conv_out', 'w_out', 'g_mix_pre', 'g_mix_post', 'w_gate_up', 'w_down', 'g_ffn_pre', 'g_ffn_post']
TWIN_WEIGHTS = ['w_in', 'b_forget', 'conv_w', 'g_attn_out', 'g_conv_out', 'w_out', 'g_mix_pre', 'g_mix_post', 'w_gate_up', 'w_down', 'g_ffn_pre', 'g_ffn_post']
TWIN_DIFF_INPUT = 'x'
TWIN_INPUTS = ['x', 'w_in', 'b_forget', 'conv_w', 'g_attn_out', 'g_conv_out', 'w_out', 'g_mix_pre', 'g_mix_post', 'w_gate_up', 'w_down', 'g_ffn_pre', 'g_ffn_post', 'loss_target', 'm_w_in', 'm_b_forget', 'm_conv_w', 'm_g_attn_out', 'm_g_conv_out', 'm_w_out', 'm_g_mix_pre', 'm_g_mix_post', 'm_w_gate_up', 'm_w_down', 'm_g_ffn_pre', 'm_g_ffn_post', 'v_w_in', 'v_b_forget', 'v_conv_w', 'v_g_attn_out', 'v_g_conv_out', 'v_w_out', 'v_g_mix_pre', 'v_g_mix_post', 'v_w_gate_up', 'v_w_down', 'v_g_ffn_pre', 'v_g_ffn_post']
TWIN_OUTPUTS = ['loss', 'grad_x', 'grad_w_in', 'grad_b_forget', 'grad_conv_w', 'grad_g_attn_out', 'grad_g_conv_out', 'grad_w_out', 'grad_g_mix_pre', 'grad_g_mix_post', 'grad_w_gate_up', 'grad_w_down', 'grad_g_ffn_pre', 'grad_g_ffn_post', 'delta_w_in', 'delta_b_forget', 'delta_conv_w', 'delta_g_attn_out', 'delta_g_conv_out', 'delta_w_out', 'delta_g_mix_pre', 'delta_g_mix_post', 'delta_w_gate_up', 'delta_w_down', 'delta_g_ffn_pre', 'delta_g_ffn_post', 'new_m_w_in', 'new_m_b_forget', 'new_m_conv_w', 'new_m_g_attn_out', 'new_m_g_conv_out', 'new_m_w_out', 'new_m_g_mix_pre', 'new_m_g_mix_post', 'new_m_w_gate_up', 'new_m_w_down', 'new_m_g_ffn_pre', 'new_m_g_ffn_post', 'new_v_w_in', 'new_v_b_forget', 'new_v_conv_w', 'new_v_g_attn_out', 'new_v_g_conv_out', 'new_v_w_out', 'new_v_g_mix_pre', 'new_v_g_mix_post', 'new_v_w_gate_up', 'new_v_w_down', 'new_v_g_ffn_pre', 'new_v_g_ffn_post']
TWIN_LEAF_KINDS = {'loss': 'loss', 'grad_x': 'grad_x', 'grad_w_in': 'grad_w', 'grad_b_forget': 'grad_w', 'grad_conv_w': 'grad_w', 'grad_g_attn_out': 'grad_w', 'grad_g_conv_out': 'grad_w', 'grad_w_out': 'grad_w', 'grad_g_mix_pre': 'grad_w', 'grad_g_mix_post': 'grad_w', 'grad_w_gate_up': 'grad_w', 'grad_w_down': 'grad_w', 'grad_g_ffn_pre': 'grad_w', 'grad_g_ffn_post': 'grad_w', 'delta_w_in': 'delta_w', 'delta_b_forget': 'delta_w', 'delta_conv_w': 'delta_w', 'delta_g_attn_out': 'delta_w', 'delta_g_conv_out': 'delta_w', 'delta_w_out': 'delta_w', 'delta_g_mix_pre': 'delta_w', 'delta_g_mix_post': 'delta_w', 'delta_w_gate_up': 'delta_w', 'delta_w_down': 'delta_w', 'delta_g_ffn_pre': 'delta_w', 'delta_g_ffn_post': 'delta_w', 'new_m_w_in': 'new_m', 'new_m_b_forget': 'new_m', 'new_m_conv_w': 'new_m', 'new_m_g_attn_out': 'new_m', 'new_m_g_conv_out': 'new_m', 'new_m_w_out': 'new_m', 'new_m_g_mix_pre': 'new_m', 'new_m_g_mix_post': 'new_m', 'new_m_w_gate_up': 'new_m', 'new_m_w_down': 'new_m', 'new_m_g_ffn_pre': 'new_m', 'new_m_g_ffn_post': 'new_m', 'new_v_w_in': 'new_v', 'new_v_b_forget': 'new_v', 'new_v_conv_w': 'new_v', 'new_v_g_attn_out': 'new_v', 'new_v_g_conv_out': 'new_v', 'new_v_w_out': 'new_v', 'new_v_g_mix_pre': 'new_v', 'new_v_g_mix_post': 'new_v', 'new_v_w_gate_up': 'new_v', 'new_v_w_down': 'new_v', 'new_v_g_ffn_pre': 'new_v', 'new_v_g_ffn_post': 'new_v'}


def _forward(args):
    return _fwd_reference(*[args[k] for k in FWD_PARAMS])


def _output_shape():
    out = _jax.eval_shape(lambda: _forward(_fwd_setup_inputs(0)))
    return out.shape, out.dtype

N_MICROBATCH = 1
ADAM_LR = 0.001
ADAM_B1 = 0.9
ADAM_B2 = 0.999
ADAM_EPS = 1e-08
ADAM_WD = 0.01
ADAM_STEP = 10
PER_EXAMPLE_BATCH_AXIS = {'x': 0, 'loss_target': 0}
SHARED_INPUTS = []
_WEIGHT_DTYPES = {'w_in': _jnp.float32, 'b_forget': _jnp.float32, 'conv_w': _jnp.float32, 'g_attn_out': _jnp.float32, 'g_conv_out': _jnp.float32, 'w_out': _jnp.float32, 'g_mix_pre': _jnp.float32, 'g_mix_post': _jnp.float32, 'w_gate_up': _jnp.float32, 'w_down': _jnp.float32, 'g_ffn_pre': _jnp.float32, 'g_ffn_post': _jnp.float32}
MOMENT_SCALE = {'w_in': 4.441086e-01, 'b_forget': 2.198117e+00, 'conv_w': 5.044575e-01, 'g_attn_out': 4.991807e-01, 'g_conv_out': 5.710366e-01, 'w_out': 5.113420e-01, 'g_mix_pre': 8.151519e-01, 'g_mix_post': 3.190748e+01, 'w_gate_up': 2.331189e-01, 'w_down': 4.516025e-01, 'g_ffn_pre': 5.299588e-01, 'g_ffn_post': 3.200535e+01}


def _to_microbatches(a, axis):
    t = _jnp.moveaxis(a, axis, 0)
    t = t.reshape((N_MICROBATCH, t.shape[0] // N_MICROBATCH) + t.shape[1:])
    return _jnp.moveaxis(t, 1, axis + 1)


def setup_inputs(seed: int = 0) -> dict:
    inp = _fwd_setup_inputs(seed)
    key = _jax.random.fold_in(_jax.random.key(seed), 7919)
    shape, _ = _output_shape()
    out = dict(inp)
    out["loss_target"] = _jax.random.normal(_jax.random.fold_in(key, 0), shape, _jnp.float32)
    for i, name in enumerate(TWIN_WEIGHTS):
        w = inp[name].astype(_jnp.float32)
        if MOMENT_SCALE is None:
            s = _jnp.sqrt(_jnp.mean(_jnp.square(w)) + 1e-30)
        else:
            s = MOMENT_SCALE[name]
        km, kv = _jax.random.split(_jax.random.fold_in(key, i + 1))
        out[name] = w
        out["m_" + name] = s * _jax.random.normal(km, w.shape, _jnp.float32)
        out["v_" + name] = (s * s) * _jax.random.uniform(kv, w.shape, _jnp.float32, 0.5, 1.5)
    if N_MICROBATCH > 1:
        for name, axis in PER_EXAMPLE_BATCH_AXIS.items():
            out[name] = _to_microbatches(out[name], axis)
    return {'x': out['x'], 'w_in': out['w_in'], 'b_forget': out['b_forget'], 'conv_w': out['conv_w'], 'g_attn_out': out['g_attn_out'], 'g_conv_out': out['g_conv_out'], 'w_out': out['w_out'], 'g_mix_pre': out['g_mix_pre'], 'g_mix_post': out['g_mix_post'], 'w_gate_up': out['w_gate_up'], 'w_down': out['w_down'], 'g_ffn_pre': out['g_ffn_pre'], 'g_ffn_post': out['g_ffn_post'], 'loss_target': out['loss_target'], 'm_w_in': out['m_w_in'], 'm_b_forget': out['m_b_forget'], 'm_conv_w': out['m_conv_w'], 'm_g_attn_out': out['m_g_attn_out'], 'm_g_conv_out': out['m_g_conv_out'], 'm_w_out': out['m_w_out'], 'm_g_mix_pre': out['m_g_mix_pre'], 'm_g_mix_post': out['m_g_mix_post'], 'm_w_gate_up': out['m_w_gate_up'], 'm_w_down': out['m_w_down'], 'm_g_ffn_pre': out['m_g_ffn_pre'], 'm_g_ffn_post': out['m_g_ffn_post'], 'v_w_in': out['v_w_in'], 'v_b_forget': out['v_b_forget'], 'v_conv_w': out['v_conv_w'], 'v_g_attn_out': out['v_g_attn_out'], 'v_g_conv_out': out['v_g_conv_out'], 'v_w_out': out['v_w_out'], 'v_g_mix_pre': out['v_g_mix_pre'], 'v_g_mix_post': out['v_g_mix_post'], 'v_w_gate_up': out['v_w_gate_up'], 'v_w_down': out['v_w_down'], 'v_g_ffn_pre': out['v_g_ffn_pre'], 'v_g_ffn_post': out['v_g_ffn_post']}


def _loss(weights, diff, rest, loss_target):
    with _jax.named_scope("forward"):
        args = {**rest, TWIN_DIFF_INPUT: diff, **{k: w.astype(_WEIGHT_DTYPES[k]) for k, w in weights.items()}}
        y = _forward(args)
    with _jax.named_scope("loss_head"):
        err = _jnp.square(y.astype(_jnp.float32) - loss_target)
        return 0.5 * _jnp.sum(_jnp.mean(err, axis=-1)) if err.ndim else 0.5 * err


def _adamw(w, g, m, v):
    m = ADAM_B1 * m + (1.0 - ADAM_B1) * g
    v = ADAM_B2 * v + (1.0 - ADAM_B2) * _jnp.square(g)
    m_hat = m / (1.0 - ADAM_B1 ** ADAM_STEP)
    v_hat = v / (1.0 - ADAM_B2 ** ADAM_STEP)
    delta = -ADAM_LR * (m_hat / (_jnp.sqrt(v_hat) + ADAM_EPS) + ADAM_WD * w)
    return delta, m, v


def reference(x, w_in, b_forget, conv_w, g_attn_out, g_conv_out, w_out, g_mix_pre, g_mix_post, w_gate_up, w_down, g_ffn_pre, g_ffn_post, loss_target, m_w_in, m_b_forget, m_conv_w, m_g_attn_out, m_g_conv_out, m_w_out, m_g_mix_pre, m_g_mix_post, m_w_gate_up, m_w_down, m_g_ffn_pre, m_g_ffn_post, v_w_in, v_b_forget, v_conv_w, v_g_attn_out, v_g_conv_out, v_w_out, v_g_mix_pre, v_g_mix_post, v_w_gate_up, v_w_down, v_g_ffn_pre, v_g_ffn_post):
    given = dict(x=x, w_in=w_in, b_forget=b_forget, conv_w=conv_w, g_attn_out=g_attn_out, g_conv_out=g_conv_out, w_out=w_out, g_mix_pre=g_mix_pre, g_mix_post=g_mix_post, w_gate_up=w_gate_up, w_down=w_down, g_ffn_pre=g_ffn_pre, g_ffn_post=g_ffn_post, loss_target=loss_target, m_w_in=m_w_in, m_b_forget=m_b_forget, m_conv_w=m_conv_w, m_g_attn_out=m_g_attn_out, m_g_conv_out=m_g_conv_out, m_w_out=m_w_out, m_g_mix_pre=m_g_mix_pre, m_g_mix_post=m_g_mix_post, m_w_gate_up=m_w_gate_up, m_w_down=m_w_down, m_g_ffn_pre=m_g_ffn_pre, m_g_ffn_post=m_g_ffn_post, v_w_in=v_w_in, v_b_forget=v_b_forget, v_conv_w=v_conv_w, v_g_attn_out=v_g_attn_out, v_g_conv_out=v_g_conv_out, v_w_out=v_w_out, v_g_mix_pre=v_g_mix_pre, v_g_mix_post=v_g_mix_post, v_w_gate_up=v_w_gate_up, v_w_down=v_w_down, v_g_ffn_pre=v_g_ffn_pre, v_g_ffn_post=v_g_ffn_post)
    weights = {n: given[n] for n in TWIN_WEIGHTS}
    shared = {n: given[n] for n in SHARED_INPUTS}
    per_example = {n: given[n] for n in ['x']}
    grad_fn = _jax.value_and_grad(_loss, argnums=(0, 1))

    def one_microbatch(ex, loss_target):
        ex = dict(ex)
        diff = ex.pop(TWIN_DIFF_INPUT)
        return grad_fn(weights, diff, {**shared, **ex}, loss_target)

    if N_MICROBATCH == 1:
        loss, (grad_w, grad_x) = one_microbatch(per_example, given["loss_target"])
    else:
        def body(carry, xs):
            loss_sum, grad_sum = carry
            l_k, (gw_k, gx_k) = one_microbatch(xs[0], xs[1])
            with _jax.named_scope("update"):
                return (loss_sum + l_k, _jax.tree.map(_jnp.add, grad_sum, gw_k)), gx_k

        init = (_jnp.zeros((), _jnp.float32), _jax.tree.map(_jnp.zeros_like, weights))
        (loss, grad_w), grad_x = _jax.lax.scan(body, init, (per_example, given["loss_target"]))
    with _jax.named_scope("update"):
        delta_w, new_m, new_v = {}, {}, {}
        for n in TWIN_WEIGHTS:
            delta_w[n], new_m[n], new_v[n] = _adamw(weights[n], grad_w[n], given["m_" + n], given["v_" + n])
    return (loss, grad_x, *[grad_w[n] for n in TWIN_WEIGHTS], *[delta_w[n] for n in TWIN_WEIGHTS],
            *[new_m[n] for n in TWIN_WEIGHTS], *[new_v[n] for n in TWIN_WEIGHTS])
```

```python
import functools

import jax
import jax.numpy as jnp
import numpy as np
from jax import lax
from jax.experimental import pallas as pl
from jax.experimental.pallas import tpu as pltpu

F32 = jnp.float32
BF16 = jnp.bfloat16
MXU_DTYPE = jnp.bfloat16

D_MODEL = 1024
HEAD_DIM = 64
N_HEADS = 8
ATTN_W = 512
CONV_W = 512
D_FF = 2816
FF_PIECE = 1408
EPS = 1e-6
Q_SCALE = HEAD_DIM ** -0.5

OFF_F = 1536
OFF_B = 1664
OFF_C = 2176
OFF_U = 2688
IN_PAD = 3200
IN_W = 3080
N_CHIPS = 4

ADAM_LR = 0.001
ADAM_B1 = 0.9
ADAM_B2 = 0.999
ADAM_EPS = 1e-08
ADAM_WD = 0.01
ADAM_STEP = 10

VMEM_LIMIT_V7X = 56 * 1024 * 1024
MESH_ID = pl.DeviceIdType.MESH


def _params(sem=None, vmem=VMEM_LIMIT_V7X):
    kw = {"vmem_limit_bytes": vmem}
    if sem is not None:
        kw["dimension_semantics"] = sem
    return pltpu.CompilerParams(**kw)


def _dot(a, b):
    return jnp.dot(a, b, preferred_element_type=F32)


def _dot_nt(a, b):
    return lax.dot_general(a, b, (((1,), (1,)), ((), ())), preferred_element_type=F32)


def _dot_exact(x, ones, parts):
    if ones.dtype == F32:
        return _dot(x, ones)
    acc = None
    rem = x
    for _ in range(parts):
        piece = rem.astype(BF16)
        rem = rem - piece.astype(F32)
        term = _dot(piece, ones)
        acc = term if acc is None else acc + term
    return acc


def _rms(v):
    return lax.rsqrt(jnp.mean(v * v, axis=-1, keepdims=True) + EPS)


def _tok(tm, w):
    return pl.BlockSpec((tm, w), lambda i: (i, 0))


def _whole(shape, single=False):
    nd = len(shape)
    if single:
        return pl.BlockSpec(shape, lambda i: (0,) * nd, pipeline_mode=pl.Buffered(1))
    return pl.BlockSpec(shape, lambda i: (0,) * nd)


def _inproj_fwd(x, g_pre, w_pad, tm):
    s = x.shape[0]

    def body(x_ref, g_ref, w_ref, h_ref, q_ref, k_ref, v_ref, f_ref, b_ref, c_ref, u_ref):
        xv = x_ref[...]
        h = ((xv * _rms(xv)) * g_ref[...]).astype(MXU_DTYPE)
        h_ref[...] = h

        def proj(lo, hi):
            return _dot(h, w_ref[:, lo:hi])

        q_ref[...] = (proj(0, 512) * Q_SCALE).astype(MXU_DTYPE)
        k_ref[...] = proj(512, 1024).astype(MXU_DTYPE)
        v_ref[...] = proj(1024, OFF_F).astype(MXU_DTYPE)
        f_ref[...] = proj(OFF_F, OFF_B)
        b_ref[...] = proj(OFF_B, OFF_C)
        c_ref[...] = proj(OFF_C, OFF_U)
        u_ref[...] = proj(OFF_U, IN_PAD)

    sd = jax.ShapeDtypeStruct
    return pl.pallas_call(
        body, name="inproj_fwd", grid=(s // tm,),
        in_specs=[_tok(tm, D_MODEL), _whole((1, D_MODEL)), _whole((D_MODEL, IN_PAD), single=True)],
        out_specs=[_tok(tm, D_MODEL), _tok(tm, 512), _tok(tm, 512), _tok(tm, 512), _tok(tm, 128),
                   _tok(tm, 512), _tok(tm, 512), _tok(tm, 512)],
        out_shape=[sd((s, D_MODEL), MXU_DTYPE), sd((s, 512), MXU_DTYPE), sd((s, 512), MXU_DTYPE),
                   sd((s, 512), MXU_DTYPE), sd((s, 128), F32), sd((s, 512), F32), sd((s, 512), F32),
                   sd((s, 512), F32)],
        compiler_params=_params(("arbitrary",)),
    )(x, g_pre, w_pad)


def _tri(n, upper):
    r = lax.broadcasted_iota(jnp.int32, (n, n), 0)
    c = lax.broadcasted_iota(jnp.int32, (n, n), 1)
    return ((r <= c) if upper else (r >= c)).astype(MXU_DTYPE)


def _forget_fwd(z_t, b_col):
    rows, s = z_t.shape
    nb = s // 128

    def body(z_ref, b_ref, c_ref):
        upper = _tri(128, True)

        def blk(n, carry):
            off = pl.multiple_of(n * 128, 128)
            lf = jax.nn.log_sigmoid(z_ref[:, pl.ds(off, 128)] + b_ref[...])
            c_ref[:, pl.ds(off, 128)] = _dot_exact(lf, upper, 3) + carry
            return carry + jnp.sum(lf, axis=1, keepdims=True)

        lax.fori_loop(0, nb, blk, jnp.zeros((rows, 1), F32))

    return pl.pallas_call(body, name="forget_fwd", out_shape=jax.ShapeDtypeStruct((rows, s), F32),
                          compiler_params=_params())(z_t, b_col)


def _attn_fwd(qs, k, v_t, c_row, c_col, t):
    h, s, _ = qs.shape
    n = s // t
    pairs = [(i, j) for i in range(n) for j in range(i + 1)]
    it = jnp.asarray(np.array([p[0] for p in pairs], np.int32))
    jt = jnp.asarray(np.array([p[1] for p in pairs], np.int32))

    def body(it_ref, jt_ref, q_ref, k_ref, vt_ref, cq_ref, ck_ref, o_ref, lse_ref, m_sc, l_sc, acc_sc):
        p = pl.program_id(1)
        i = it_ref[p]
        j = jt_ref[p]

        @pl.when(j == 0)
        def _():
            m_sc[...] = jnp.full_like(m_sc, -1e30)
            l_sc[...] = jnp.zeros_like(l_sc)
            acc_sc[...] = jnp.zeros_like(acc_sc)

        def step(diagonal):
            st = _dot_nt(k_ref[0], q_ref[0]) + (cq_ref[0] - ck_ref[0])
            if diagonal:
                kpos = lax.broadcasted_iota(jnp.int32, (t, t), 0)
                qpos = lax.broadcasted_iota(jnp.int32, (t, t), 1)
                st = jnp.where(kpos <= qpos, st, -1e30)
            m_prev = m_sc[...]
            m_new = jnp.maximum(m_prev, jnp.max(st, axis=0, keepdims=True))
            alpha = jnp.exp(m_prev - m_new)
            pt = jnp.exp(st - m_new)
            l_sc[...] = alpha * l_sc[...] + jnp.sum(pt, axis=0, keepdims=True)
            acc_sc[...] = acc_sc[...] * alpha + _dot(vt_ref[0], pt.astype(MXU_DTYPE))
            m_sc[...] = m_new

        @pl.when(j < i)
        def _():
            step(False)

        @pl.when(j == i)
        def _():
            step(True)
            o_ref[0] = acc_sc[...] / l_sc[...]
            lse_ref[0] = m_sc[...] + jnp.log(l_sc[...])

    gs = pltpu.PrefetchScalarGridSpec(
        num_scalar_prefetch=2, grid=(h, len(pairs)),
        in_specs=[pl.BlockSpec((1, t, HEAD_DIM), lambda hh, p, it_, jt_: (hh, it_[p], 0)),
                  pl.BlockSpec((1, t, HEAD_DIM), lambda hh, p, it_, jt_: (hh, jt_[p], 0)),
                  pl.BlockSpec((1, HEAD_DIM, t), lambda hh, p, it_, jt_: (hh, 0, jt_[p])),
                  pl.BlockSpec((1, 1, t), lambda hh, p, it_, jt_: (hh, 0, it_[p])),
                  pl.BlockSpec((1, t, 1), lambda hh, p, it_, jt_: (hh, jt_[p], 0))],
        out_specs=[pl.BlockSpec((1, HEAD_DIM, t), lambda hh, p, it_, jt_: (hh, 0, it_[p])),
                   pl.BlockSpec((1, 1, t), lambda hh, p, it_, jt_: (hh, 0, it_[p]))],
        scratch_shapes=[pltpu.VMEM((1, t), F32), pltpu.VMEM((1, t), F32), pltpu.VMEM((HEAD_DIM, t), F32)])
    return pl.pallas_call(
        body, name="attn_fwd", grid_spec=gs,
        out_shape=[jax.ShapeDtypeStruct((h, HEAD_DIM, s), F32), jax.ShapeDtypeStruct((h, 1, s), F32)],
        compiler_params=_params(("arbitrary", "arbitrary")),
    )(it, jt, qs, k, v_t, c_row, c_col)


def _shift_down(cur, prev_ref, first):
    row = lax.broadcasted_iota(jnp.int32, cur.shape, 0)
    p7 = jnp.where(first, 0.0, prev_ref[0][7:8, :] * prev_ref[1][7:8, :])
    p6 = jnp.where(first, 0.0, prev_ref[0][6:7, :] * prev_ref[1][6:7, :])
    s1 = jnp.where(row == 0, p7, pltpu.roll(cur, 1, 0))
    s2 = jnp.where(row == 0, p6, jnp.where(row == 1, p7, pltpu.roll(cur, 2, 0)))
    return s1, s2


def _group_ms(v, gmat):
    return _dot_exact(v, gmat, 2) * (1.0 / HEAD_DIM)


def _mixer_fwd(x, o_attn, gate_b, gate_c, u, conv_w, g_attn, g_conv, w_out, g_post, gmat, tm):
    s = x.shape[0]

    def body(x_ref, o_ref, b_ref, c_ref, u_ref, cp_ref, up_ref, cw_ref, ga_ref, gc_ref, wo_ref, gp_ref, gm_ref,
             x2_ref, mg_ref, y_ref, z_ref):
        i = pl.program_id(0)
        cu = c_ref[...] * u_ref[...]
        cu1, cu2 = _shift_down(cu, (cp_ref, up_ref), i == 0)
        z = cw_ref[0:1, :] * cu2 + cw_ref[1:2, :] * cu1 + cw_ref[2:3, :] * cu
        z_ref[...] = z
        cv = b_ref[...] * z
        ov = o_ref[...]
        gm = gm_ref[...]
        ma = ((ov * lax.rsqrt(_group_ms(ov * ov, gm) + EPS)) * ga_ref[...]).astype(MXU_DTYPE)
        mc = ((cv * lax.rsqrt(_group_ms(cv * cv, gm) + EPS)) * gc_ref[...]).astype(MXU_DTYPE)
        mg_ref[:, 0:ATTN_W] = ma
        mg_ref[:, ATTN_W:D_MODEL] = mc
        y = _dot(ma, wo_ref[0:ATTN_W, :]) + _dot(mc, wo_ref[ATTN_W:D_MODEL, :])
        y_ref[...] = y
        x2_ref[...] = x_ref[...] + (y * _rms(y)) * gp_ref[...]

    halo = pl.BlockSpec((8, 512), lambda i: (jnp.maximum(i * (tm // 8) - 1, 0), 0))
    sd = jax.ShapeDtypeStruct
    return pl.pallas_call(
        body, name="mixer_fwd", grid=(s // tm,),
        in_specs=[_tok(tm, D_MODEL), _tok(tm, 512), _tok(tm, 512), _tok(tm, 512), _tok(tm, 512), halo, halo,
                  _whole((3, 512)), _whole((1, 512)), _whole((1, 512)), _whole((D_MODEL, D_MODEL), single=True),
                  _whole((1, D_MODEL)), _whole((512, 512))],
        out_specs=[_tok(tm, D_MODEL), _tok(tm, D_MODEL), _tok(tm, D_MODEL), _tok(tm, 512)],
        out_shape=[sd((s, D_MODEL), F32), sd((s, D_MODEL), MXU_DTYPE), sd((s, D_MODEL), F32), sd((s, 512), F32)],
        compiler_params=_params(("arbitrary",)),
    )(x, o_attn, gate_b, gate_c, u, gate_c, u, conv_w, g_attn, g_conv, w_out, g_post, gmat)


def _ffn_fwd(x2, target, g_pre, w_gu, w_dn, g_post, tm):
    s = x2.shape[0]

    def body(x_ref, t_ref, gpre_ref, wgu_ref, wdn_ref, gpost_ref,
             h_ref, g_ref, up_ref, a_ref, ff_ref, dout_ref, loss_ref):
        xv = x_ref[...]
        h = ((xv * _rms(xv)) * gpre_ref[...]).astype(MXU_DTYPE)
        h_ref[...] = h
        ff = jnp.zeros((tm, D_MODEL), F32)
        for j in range(2):
            cols = slice(j * FF_PIECE, (j + 1) * FF_PIECE)
            g = _dot(h, wgu_ref[j])
            up = _dot(h, wgu_ref[2 + j])
            a = ((g * jax.nn.sigmoid(g)) * up).astype(MXU_DTYPE)
            g_ref[:, cols] = g
            up_ref[:, cols] = up
            a_ref[:, cols] = a
            ff = ff + _dot(a, wdn_ref[j])
        ff_ref[...] = ff
        err = (xv + (ff * _rms(ff)) * gpost_ref[...]) - t_ref[...]
        dout_ref[...] = err * (1.0 / D_MODEL)
        part = jnp.sum(jnp.mean(err * err, axis=-1, keepdims=True), axis=0, keepdims=True)

        @pl.when(pl.program_id(0) == 0)
        def _():
            loss_ref[...] = jnp.zeros_like(loss_ref)

        loss_ref[...] += part

    sd = jax.ShapeDtypeStruct
    return pl.pallas_call(
        body, name="ffn_fwd", grid=(s // tm,),
        in_specs=[_tok(tm, D_MODEL), _tok(tm, D_MODEL), _whole((1, D_MODEL)),
                  _whole((4, D_MODEL, FF_PIECE), single=True), _whole((2, FF_PIECE, D_MODEL), single=True),
                  _whole((1, D_MODEL))],
        out_specs=[_tok(tm, D_MODEL), _tok(tm, D_FF), _tok(tm, D_FF), _tok(tm, D_FF), _tok(tm, D_MODEL),
                   _tok(tm, D_MODEL), _whole((8, 128))],
        out_shape=[sd((s, D_MODEL), MXU_DTYPE), sd((s, D_FF), F32), sd((s, D_FF), F32), sd((s, D_FF), MXU_DTYPE),
                   sd((s, D_MODEL), F32), sd((s, D_MODEL), F32), sd((8, 128), F32)],
        compiler_params=_params(("arbitrary",)),
    )(x2, target, g_pre, w_gu, w_dn, g_post)


def _norm_bwd(dy, normed, rinv, gain):
    t = dy * gain
    return rinv * (t - normed * jnp.mean(t * normed, axis=-1, keepdims=True))


def _acc_rows(ref, first, val):
    @pl.when(first)
    def _():
        ref[...] = jnp.zeros_like(ref)

    ref[...] += jnp.sum(val, axis=0, keepdims=True)


def _ffn_bwd(dout, ff, x2, g, up, g_post, g_pre, w_gu, w_dn, tm):
    s = x2.shape[0]

    def body(do_ref, ff_ref, x_ref, g_ref, up_ref, gpost_ref, gpre_ref, wgu_ref, wdn_ref,
             dx_ref, dff_ref, dgu_ref, dgpost_ref, dgpre_ref):
        first = pl.program_id(0) == 0
        ffv = ff_ref[...]
        rf = _rms(ffv)
        n = ffv * rf
        do = do_ref[...]
        _acc_rows(dgpost_ref, first, do * n)
        dff = _norm_bwd(do, n, rf, gpost_ref[...]).astype(MXU_DTYPE)
        dff_ref[...] = dff
        dh = jnp.zeros((tm, D_MODEL), F32)
        for j in range(2):
            cols = slice(j * FF_PIECE, (j + 1) * FF_PIECE)
            da = _dot_nt(dff, wdn_ref[j])
            gv = g_ref[:, cols]
            sg = jax.nn.sigmoid(gv)
            dg = (da * up_ref[:, cols] * (sg * (1.0 + gv * (1.0 - sg)))).astype(MXU_DTYPE)
            du = (da * (gv * sg)).astype(MXU_DTYPE)
            dgu_ref[:, cols] = dg
            dgu_ref[:, D_FF + j * FF_PIECE:D_FF + (j + 1) * FF_PIECE] = du
            dh = dh + _dot_nt(dg, wgu_ref[j]) + _dot_nt(du, wgu_ref[2 + j])
        xv = x_ref[...]
        r2 = _rms(xv)
        nx = xv * r2
        _acc_rows(dgpre_ref, first, dh * nx)
        dx_ref[...] = do + _norm_bwd(dh, nx, r2, gpre_ref[...])

    sd = jax.ShapeDtypeStruct
    return pl.pallas_call(
        body, name="ffn_bwd", grid=(s // tm,),
        in_specs=[_tok(tm, D_MODEL), _tok(tm, D_MODEL), _tok(tm, D_MODEL), _tok(tm, D_FF), _tok(tm, D_FF),
                  _whole((1, D_MODEL)), _whole((1, D_MODEL)),
                  _whole((4, D_MODEL, FF_PIECE), single=True), _whole((2, FF_PIECE, D_MODEL), single=True)],
        out_specs=[_tok(tm, D_MODEL), _tok(tm, D_MODEL), _tok(tm, 2 * D_FF), _whole((1, D_MODEL)),
                   _whole((1, D_MODEL))],
        out_shape=[sd((s, D_MODEL), F32), sd((s, D_MODEL), MXU_DTYPE), sd((s, 2 * D_FF), MXU_DTYPE),
                   sd((1, D_MODEL), F32), sd((1, D_MODEL), F32)],
        compiler_params=_params(("arbitrary",)),
    )(dout, ff, x2, g, up, g_post, g_pre, w_gu, w_dn)


def _tn_matmul(a, b, tm, tn, tk, name):
    s, m = a.shape
    n = b.shape[1]

    def body(a_ref, b_ref, o_ref):
        @pl.when(pl.program_id(2) == 0)
        def _():
            o_ref[...] = jnp.zeros_like(o_ref)

        o_ref[...] += lax.dot_general(a_ref[...], b_ref[...], (((0,), (0,)), ((), ())), preferred_element_type=F32)

    return pl.pallas_call(
        body, name=name, grid=(m // tm, n // tn, s // tk),
        in_specs=[pl.BlockSpec((tk, tm), lambda i, j, kk: (kk, i)), pl.BlockSpec((tk, tn), lambda i, j, kk: (kk, j))],
        out_specs=pl.BlockSpec((tm, tn), lambda i, j, kk: (i, j)),
        out_shape=jax.ShapeDtypeStruct((m, n), F32),
        compiler_params=_params(("arbitrary", "arbitrary", "arbitrary")),
    )(a, b)


def _mixer_bwd(dx2, y, o_attn, gate_b, z, g_post, g_attn, g_conv, w_out, gmat, sel, tm):
    s = dx2.shape[0]

    def body(d_ref, y_ref, o_ref, b_ref, z_ref, gp_ref, ga_ref, gc_ref, wo_ref, gm_ref, sel_ref,
             dy_ref, do_ref, db_ref, dz_ref, delta_ref, dgp_ref, dga_ref, dgc_ref):
        first = pl.program_id(0) == 0
        yv = y_ref[...]
        ry = _rms(yv)
        ny = yv * ry
        d = d_ref[...]
        _acc_rows(dgp_ref, first, d * ny)
        dy = _norm_bwd(d, ny, ry, gp_ref[...]).astype(MXU_DTYPE)
        dy_ref[...] = dy
        dm = _dot_nt(dy, wo_ref[...])
        gm = gm_ref[...]

        def group_bwd(val, dmv, gain, dg_ref):
            rg = lax.rsqrt(_group_ms(val * val, gm) + EPS)
            nv = val * rg
            _acc_rows(dg_ref, first, dmv * nv)
            t = dmv * gain
            return rg * (t - nv * _group_ms(t * nv, gm))

        ov = o_ref[...]
        d_o = group_bwd(ov, dm[:, 0:ATTN_W], ga_ref[...], dga_ref)
        do_ref[...] = d_o
        delta_ref[...] = _dot_exact(d_o * ov, sel_ref[...], 2)
        zv = z_ref[...]
        bv = b_ref[...]
        d_cv = group_bwd(bv * zv, dm[:, ATTN_W:D_MODEL], gc_ref[...], dgc_ref)
        db_ref[...] = d_cv * zv
        dz_ref[...] = d_cv * bv

    sd = jax.ShapeDtypeStruct
    return pl.pallas_call(
        body, name="mixer_bwd", grid=(s // tm,),
        in_specs=[_tok(tm, D_MODEL), _tok(tm, D_MODEL), _tok(tm, 512), _tok(tm, 512), _tok(tm, 512),
                  _whole((1, D_MODEL)), _whole((1, 512)), _whole((1, 512)),
                  _whole((D_MODEL, D_MODEL), single=True), _whole((512, 512)), _whole((512, 128))],
        out_specs=[_tok(tm, D_MODEL), _tok(tm, 512), _tok(tm, 512), _tok(tm, 512), _tok(tm, 128),
                   _whole((1, D_MODEL)), _whole((1, 512)), _whole((1, 512))],
        out_shape=[sd((s, D_MODEL), MXU_DTYPE), sd((s, 512), F32), sd((s, 512), F32), sd((s, 512), F32),
                   sd((s, 128), F32), sd((1, D_MODEL), F32), sd((1, 512), F32), sd((1, 512), F32)],
        compiler_params=_params(("arbitrary",)),
    )(dx2, y, o_attn, gate_b, z, g_post, g_attn, g_conv, w_out, gmat, sel)


def _attn_bwd(qs, k, k_t, v, do, c_row, c_col, lse, delta, t):
    h, s, _ = qs.shape
    n = s // t
    pairs = [(i, j) for j in range(n) for i in range(j, n)]
    it = jnp.asarray(np.array([p[0] for p in pairs], np.int32))
    jt = jnp.asarray(np.array([p[1] for p in pairs], np.int32))

    def body(it_ref, jt_ref, q_ref, k_ref, kt_ref, v_ref, do_ref, cq_ref, ck_ref, lse_ref, dl_ref,
             dq_ref, dk_ref, dv_ref, dc_ref, dcq_ref, dk_sc, dv_sc, dc_sc):
        p = pl.program_id(1)
        i = it_ref[p]
        j = jt_ref[p]

        @pl.when(p == 0)
        def _():
            dq_ref[...] = jnp.zeros_like(dq_ref)
            dcq_ref[...] = jnp.zeros_like(dcq_ref)

        @pl.when(i == j)
        def _():
            dk_sc[...] = jnp.zeros_like(dk_sc)
            dv_sc[...] = jnp.zeros_like(dv_sc)
            dc_sc[...] = jnp.zeros_like(dc_sc)

        def step(diagonal):
            qv = q_ref[0]
            dov = do_ref[0]
            st = _dot_nt(k_ref[0], qv) + ((cq_ref[0] - lse_ref[0]) - ck_ref[0])
            pt = jnp.exp(st)
            if diagonal:
                kpos = lax.broadcasted_iota(jnp.int32, (t, t), 0)
                qpos = lax.broadcasted_iota(jnp.int32, (t, t), 1)
                pt = jnp.where(kpos <= qpos, pt, 0.0)
            dv_sc[...] += _dot(pt.astype(MXU_DTYPE), dov)
            dst = pt * (_dot_nt(v_ref[0], dov) - dl_ref[0])
            dc_sc[...] -= jnp.sum(dst, axis=1, keepdims=True)
            dcq_ref[0, i] += jnp.sum(dst, axis=0, keepdims=True)
            dsb = dst.astype(MXU_DTYPE)
            dk_sc[...] += _dot(dsb, qv)
            dq_ref[0, i] += _dot(kt_ref[0], dsb)

        @pl.when(i > j)
        def _():
            step(False)

        @pl.when(i == j)
        def _():
            step(True)

        @pl.when(i == n - 1)
        def _():
            dk_ref[0] = dk_sc[...]
            dv_ref[0] = dv_sc[...]
            dc_ref[0] = dc_sc[...]

    qi = lambda hh, p, it_, jt_: (hh, it_[p], 0)
    kj = lambda hh, p, it_, jt_: (hh, jt_[p], 0)
    row_i = lambda hh, p, it_, jt_: (hh, 0, it_[p])
    gs = pltpu.PrefetchScalarGridSpec(
        num_scalar_prefetch=2, grid=(h, len(pairs)),
        in_specs=[pl.BlockSpec((1, t, HEAD_DIM), qi), pl.BlockSpec((1, t, HEAD_DIM), kj),
                  pl.BlockSpec((1, HEAD_DIM, t), lambda hh, p, it_, jt_: (hh, 0, jt_[p])),
                  pl.BlockSpec((1, t, HEAD_DIM), kj), pl.BlockSpec((1, t, HEAD_DIM), qi),
                  pl.BlockSpec((1, 1, t), row_i), pl.BlockSpec((1, t, 1), kj),
                  pl.BlockSpec((1, 1, t), row_i), pl.BlockSpec((1, 1, t), row_i)],
        out_specs=[pl.BlockSpec((1, n, HEAD_DIM, t), lambda hh, p, it_, jt_: (hh, 0, 0, 0)),
                   pl.BlockSpec((1, t, HEAD_DIM), kj), pl.BlockSpec((1, t, HEAD_DIM), kj),
                   pl.BlockSpec((1, t, 1), kj),
                   pl.BlockSpec((1, n, 1, t), lambda hh, p, it_, jt_: (hh, 0, 0, 0))],
        scratch_shapes=[pltpu.VMEM((t, HEAD_DIM), F32), pltpu.VMEM((t, HEAD_DIM), F32), pltpu.VMEM((t, 1), F32)])
    sd = jax.ShapeDtypeStruct
    return pl.pallas_call(
        body, name="attn_bwd", grid_spec=gs,
        out_shape=[sd((h, n, HEAD_DIM, t), F32), sd((h, s, HEAD_DIM), F32), sd((h, s, HEAD_DIM), F32),
                   sd((h, s, 1), F32), sd((h, n, 1, t), F32)],
        compiler_params=_params(("arbitrary", "arbitrary")),
    )(it, jt, qs, k, k_t, v, do, c_row, c_col, lse, delta)


def _forget_bwd(dc_t, z_t, b_col):
    rows, s = z_t.shape
    nb = s // 128

    def body(dc_ref, z_ref, b_ref, dz_ref, db_ref):
        lower = _tri(128, False)

        def blk(m, carry):
            tail, dbias = carry
            off = pl.multiple_of((nb - 1 - m) * 128, 128)
            dc = dc_ref[:, pl.ds(off, 128)]
            dlf = _dot_exact(dc, lower, 3) + tail
            dz = dlf * jax.nn.sigmoid(-(z_ref[:, pl.ds(off, 128)] + b_ref[...]))
            dz_ref[:, pl.ds(off, 128)] = dz
            return tail + jnp.sum(dc, axis=1, keepdims=True), dbias + jnp.sum(dz, axis=1, keepdims=True)

        zero = jnp.zeros((rows, 1), F32)
        _, dbias = lax.fori_loop(0, nb, blk, (zero, zero))
        db_ref[...] = jnp.broadcast_to(dbias, db_ref.shape)

    return pl.pallas_call(
        body, name="forget_bwd",
        out_shape=[jax.ShapeDtypeStruct((rows, s), F32), jax.ShapeDtypeStruct((rows, 128), F32)],
        compiler_params=_params())(dc_t, z_t, b_col)


def _inproj_bwd(dz, gate_c, u, conv_w, dq, dk, dv, dzf, db, x, dx2, g_pre, w_pad, tm):
    s = x.shape[0]
    nt = s // tm

    def body(dz_ref, dzn_ref, c_ref, u_ref, cp_ref, up_ref, cw_ref, dq_ref, dk_ref, dv_ref, dzf_ref, db_ref,
             x_ref, dx2_ref, g_ref, w_ref, gx_ref, dp_ref, dg_ref, dcw_ref):
        i = pl.program_id(0)
        first = i == 0
        last = i == nt - 1
        dzv = dz_ref[...]
        row = lax.broadcasted_iota(jnp.int32, dzv.shape, 0)
        n0 = jnp.where(last, 0.0, dzn_ref[0:1, :])
        n1 = jnp.where(last, 0.0, dzn_ref[1:2, :])
        dz1 = jnp.where(row == tm - 1, n0, pltpu.roll(dzv, tm - 1, 0))
        dz2 = jnp.where(row == tm - 1, n1, jnp.where(row == tm - 2, n0, pltpu.roll(dzv, tm - 2, 0)))
        dcu = cw_ref[2:3, :] * dzv + cw_ref[1:2, :] * dz1 + cw_ref[0:1, :] * dz2
        cv = c_ref[...]
        uv = u_ref[...]
        cu = cv * uv
        cu1, cu2 = _shift_down(cu, (cp_ref, up_ref), first)

        @pl.when(first)
        def _():
            dcw_ref[...] = jnp.zeros_like(dcw_ref)

        dcw_ref[0:1, :] += jnp.sum(dzv * cu2, axis=0, keepdims=True)
        dcw_ref[1:2, :] += jnp.sum(dzv * cu1, axis=0, keepdims=True)
        dcw_ref[2:3, :] += jnp.sum(dzv * cu, axis=0, keepdims=True)

        dp_ref[:, 0:512] = (dq_ref[...] * Q_SCALE).astype(MXU_DTYPE)
        dp_ref[:, 512:1024] = dk_ref[...].astype(MXU_DTYPE)
        dp_ref[:, 1024:OFF_F] = dv_ref[...].astype(MXU_DTYPE)
        dp_ref[:, OFF_F:OFF_B] = dzf_ref[...].astype(MXU_DTYPE)
        dp_ref[:, OFF_B:OFF_C] = db_ref[...].astype(MXU_DTYPE)
        dp_ref[:, OFF_C:OFF_U] = (dcu * uv).astype(MXU_DTYPE)
        dp_ref[:, OFF_U:IN_PAD] = (dcu * cv).astype(MXU_DTYPE)
        dh = _dot_nt(dp_ref[...], w_ref[...])
        xv = x_ref[...]
        r1 = _rms(xv)
        nx = xv * r1
        _acc_rows(dg_ref, first, dh * nx)
        gx_ref[...] = dx2_ref[...] + _norm_bwd(dh, nx, r1, g_ref[...])

    prev = pl.BlockSpec((8, 512), lambda i: (jnp.maximum(i * (tm // 8) - 1, 0), 0))
    nxt = pl.BlockSpec((8, 512), lambda i: (jnp.minimum((i + 1) * (tm // 8), s // 8 - 1), 0))
    sd = jax.ShapeDtypeStruct
    return pl.pallas_call(
        body, name="inproj_bwd", grid=(nt,),
        in_specs=[_tok(tm, 512), nxt, _tok(tm, 512), _tok(tm, 512), prev, prev, _whole((3, 512)),
                  _tok(tm, 512), _tok(tm, 512), _tok(tm, 512), _tok(tm, 128), _tok(tm, 512),
                  _tok(tm, D_MODEL), _tok(tm, D_MODEL), _whole((1, D_MODEL)), _whole((D_MODEL, IN_PAD), single=True)],
        out_specs=[_tok(tm, D_MODEL), _tok(tm, IN_PAD), _whole((1, D_MODEL)), _whole((8, 512))],
        out_shape=[sd((s, D_MODEL), F32), sd((s, IN_PAD), MXU_DTYPE), sd((1, D_MODEL), F32), sd((8, 512), F32)],
        compiler_params=_params(("arbitrary",)),
    )(dz, dz, gate_c, u, gate_c, u, conv_w, dq, dk, dv, dzf, db, x, dx2, g_pre, w_pad)


def _heads(a):
    s = a.shape[0]
    return jnp.transpose(a.reshape(s, N_HEADS, HEAD_DIM), (1, 0, 2))


def _heads_t(a):
    s = a.shape[0]
    return jnp.transpose(a.reshape(s, N_HEADS, HEAD_DIM), (1, 2, 0))


def _unheads(a):
    s = a.shape[1]
    return jnp.transpose(a, (1, 0, 2)).reshape(s, N_HEADS * HEAD_DIM)


def _tile(s, want):
    return want if s % want == 0 else s


def _local_step(x, target, w_pad, b_forget, conv_w, g_attn, g_conv, w_out, g_mix_pre, g_mix_post,
                w_gu, w_dn, g_ffn_pre, g_ffn_post):
    s = x.shape[0]
    tm = _tile(s, 512)
    tf = _tile(s, 256)
    ta = _tile(s, 512)
    rows = 16
    gidx = np.arange(512) // HEAD_DIM
    gmat = jnp.asarray(gidx[:, None] == gidx[None, :], MXU_DTYPE)
    sel = jnp.asarray(gidx[:, None] == np.arange(128)[None, :], MXU_DTYPE)

    h1, qs, k, v, f, gate_b, gate_c, u = _inproj_fwd(x, g_mix_pre, w_pad, tm)
    z_t = jnp.pad(jnp.transpose(f[:, :N_HEADS]), ((0, rows - N_HEADS), (0, 0)))
    b_col = jnp.pad(jnp.transpose(b_forget), ((0, rows - N_HEADS), (0, 0)))
    c_t = _forget_fwd(z_t, b_col)
    c_row = c_t[:N_HEADS].reshape(N_HEADS, 1, s)
    c_col = c_t[:N_HEADS].reshape(N_HEADS, s, 1)
    qh, kh, vh = _heads(qs), _heads(k), _heads(v)
    o_t, lse = _attn_fwd(qh, kh, _heads_t(v), c_row, c_col, ta)
    o_attn = jnp.transpose(o_t, (2, 0, 1)).reshape(s, ATTN_W)
    x2, merged, y, z = _mixer_fwd(x, o_attn, gate_b, gate_c, u, conv_w, g_attn, g_conv, w_out, g_mix_post, gmat, tm)
    h2, g, up, a, ff, dout, loss_acc = _ffn_fwd(x2, target, g_ffn_pre, w_gu, w_dn, g_ffn_post, tf)

    dx2, dff, dgu, dg_ffn_post, dg_ffn_pre = _ffn_bwd(dout, ff, x2, g, up, g_ffn_post, g_ffn_pre, w_gu, w_dn, tf)
    tkk = _tile(s, 512)
    dw_dn = _tn_matmul(a, dff, FF_PIECE, 512, tkk, "dw_down")
    dw_gu = _tn_matmul(h2, dgu, 1024, 512, tkk, "dw_gate_up")
    dy, d_o, d_b, dz, delta8, dg_mix_post, dg_attn, dg_conv = _mixer_bwd(
        dx2, y, o_attn, gate_b, z, g_mix_post, g_attn, g_conv, w_out, gmat, sel, tm)
    dw_out = _tn_matmul(merged, dy, 512, 512, tkk, "dw_out")
    delta = jnp.transpose(delta8[:, :N_HEADS]).reshape(N_HEADS, 1, s)
    dq4, dkh, dvh, dc, dcq = _attn_bwd(qh, kh, _heads_t(k), vh, _heads(d_o.astype(MXU_DTYPE)), c_row, c_col, lse, delta, ta)
    dc_t = jnp.pad(dc.reshape(N_HEADS, s) + dcq.reshape(N_HEADS, s), ((0, rows - N_HEADS), (0, 0)))
    dz_t, db_f = _forget_bwd(dc_t, z_t, b_col)
    dzf = jnp.pad(jnp.transpose(dz_t[:N_HEADS]), ((0, 0), (0, 128 - N_HEADS)))
    dq = jnp.transpose(dq4, (1, 3, 0, 2)).reshape(s, ATTN_W)
    grad_x, dproj, dg_mix_pre, dcw = _inproj_bwd(dz, gate_c, u, conv_w, dq, _unheads(dkh), _unheads(dvh), dzf, d_b,
                                                 x, dx2, g_mix_pre, w_pad, tm)
    dw_pad = _tn_matmul(h1, dproj, 1024, 640, tkk, "dw_in")
    dw_in = jnp.concatenate([dw_pad[:, :OFF_F + N_HEADS], dw_pad[:, OFF_B:]], axis=1)
    grads = dict(w_in=dw_in, b_forget=db_f[:N_HEADS, 0].reshape(1, N_HEADS), conv_w=dcw[:3], g_attn_out=dg_attn,
                 g_conv_out=dg_conv, w_out=dw_out, g_mix_pre=dg_mix_pre, g_mix_post=dg_mix_post,
                 w_gate_up=dw_gu, w_down=dw_dn, g_ffn_pre=dg_ffn_pre, g_ffn_post=dg_ffn_post)
    return loss_acc[0, 0], grad_x, grads


BIG = ("w_in", "w_out", "w_gate_up", "w_down")
ANY = pl.BlockSpec(memory_space=pl.ANY)


def _place():
    x, y, c = lax.axis_index("x"), lax.axis_index("y"), lax.axis_index("c")
    others = [(1 - x, y), (x, 1 - y), (1 - x, 1 - y)]
    return x, y, c, 2 * x + y, others, [2 * px + py for px, py in others]


def _remote(src, dst, send, recv, dev):
    return pltpu.make_async_remote_copy(src_ref=src, dst_ref=dst, send_sem=send, recv_sem=recv,
                                        device_id=dev, device_id_type=MESH_ID)


def _gather_weights(shards, conv_w):
    n = len(shards)

    def body(*refs):
        sh, cw, outs, cwo = refs[:n], refs[n], refs[n + 1:2 * n + 1], refs[2 * n + 1]
        send, recv, loc = refs[2 * n + 2:]
        x, y, c, me, others, chips = _place()
        sib = (x, y, 1 - c)
        local = [pltpu.make_async_copy(sh[w], outs[w].at[me], loc.at[w]) for w in range(n)]
        local.append(pltpu.make_async_copy(cw, cwo.at[me], loc.at[n]))
        for cp in local:
            cp.start()
        sends = []
        for w in range(n):
            for kk, (px, py) in enumerate(others):
                sends.append(_remote(sh[w].at[c], outs[w].at[me, c], send.at[w, kk], recv.at[w, kk], (px, py, c)))
        for kk, (px, py) in enumerate(others):
            sends.append(_remote(cw, cwo.at[me], send.at[n, kk], recv.at[n, kk], (px, py, c)))
        for cp in sends:
            cp.start()
        for w in range(n):
            for kk, (px, py) in enumerate(others):
                landed = outs[w].at[chips[kk], c]
                _remote(landed, landed, send.at[w, kk], recv.at[w, kk], (px, py, c)).wait_recv()
                fwd = _remote(landed, landed, send.at[w, 3 + kk], recv.at[w, 3 + kk], sib)
                fwd.start()
                sends.append(fwd)
        for kk, (px, py) in enumerate(others):
            _remote(cw, cwo.at[chips[kk]], send.at[n, kk], recv.at[n, kk], (px, py, c)).wait_recv()
        for w in range(n):
            for kk in range(3):
                passed = outs[w].at[chips[kk], 1 - c]
                _remote(passed, passed, send.at[w, 3 + kk], recv.at[w, 3 + kk], sib).wait_recv()
        for cp in sends:
            cp.wait_send()
        for cp in local:
            cp.wait()

    out_shape = [jax.ShapeDtypeStruct((N_CHIPS,) + s.shape, s.dtype) for s in shards]
    out_shape.append(jax.ShapeDtypeStruct((N_CHIPS,) + conv_w.shape, conv_w.dtype))
    return pl.pallas_call(
        body, name="gather_weights", in_specs=[ANY] * (n + 1), out_specs=[ANY] * (n + 1), out_shape=out_shape,
        scratch_shapes=[pltpu.SemaphoreType.DMA((n + 1, 6)), pltpu.SemaphoreType.DMA((n + 1, 6)),
                        pltpu.SemaphoreType.DMA((n + 1,))],
    )(*shards, conv_w)


def _pair_exchange(grads):
    n = len(grads)

    def body(*refs):
        g, a = refs[:n], refs[n:2 * n]
        send, recv = refs[2 * n:]
        x, y, c, _, _, _ = _place()
        sib = (x, y, 1 - c)
        copies = [_remote(g[w].at[p, 1 - c], a[w].at[p], send.at[w, p], recv.at[w, p], sib)
                  for w in range(n) for p in range(N_CHIPS)]
        for cp in copies:
            cp.start()
        for cp in copies:
            cp.wait()

    return pl.pallas_call(
        body, name="pair_exchange", in_specs=[ANY] * n, out_specs=[ANY] * n,
        out_shape=[jax.ShapeDtypeStruct((N_CHIPS,) + g.shape[2:], g.dtype) for g in grads],
        scratch_shapes=[pltpu.SemaphoreType.DMA((n, N_CHIPS)), pltpu.SemaphoreType.DMA((n, N_CHIPS))],
    )(*grads)


def _pair_sum(c_idx, g, a, name):
    _, _, half, cols = g.shape

    def body(c_ref, g_ref, a_ref, pf_ref, pb_ref):
        tot = g_ref[0, 0] + a_ref[0]
        pf_ref[0] = tot
        pb_ref[0] = tot.astype(BF16)

    gs = pltpu.PrefetchScalarGridSpec(
        num_scalar_prefetch=1, grid=(N_CHIPS,),
        in_specs=[pl.BlockSpec((1, 1, half, cols), lambda p, cr: (p, cr[0], 0, 0)),
                  pl.BlockSpec((1, half, cols), lambda p, cr: (p, 0, 0))],
        out_specs=[pl.BlockSpec((1, half, cols), lambda p, cr: (p, 0, 0)),
                   pl.BlockSpec((1, half, cols), lambda p, cr: (p, 0, 0))])
    return pl.pallas_call(
        body, name=name, grid_spec=gs,
        out_shape=[jax.ShapeDtypeStruct((N_CHIPS, half, cols), F32), jax.ShapeDtypeStruct((N_CHIPS, half, cols), BF16)],
        compiler_params=_params(("arbitrary",)),
    )(c_idx, g, a)


def _chip_exchange(parts, small):
    n = len(parts)

    def body(*refs):
        pb, sm, rcv, smg = refs[:n], refs[n], refs[n + 1:2 * n + 1], refs[2 * n + 1]
        send, recv, ssend, srecv, loc = refs[2 * n + 2:]
        x, y, c, _, others, chips = _place()
        mine = 4 * x + 2 * y + c
        own = pltpu.make_async_copy(sm, smg.at[mine], loc)
        own.start()
        copies = [_remote(pb[w].at[chips[kk]], rcv[w].at[kk], send.at[w, kk], recv.at[w, kk], (px, py, c))
                  for w in range(n) for kk, (px, py) in enumerate(others)]
        for r in range(1, 8):
            peer = (1 - x if r & 4 else x, 1 - y if r & 2 else y, 1 - c if r & 1 else c)
            copies.append(_remote(sm, smg.at[mine], ssend.at[r - 1], srecv.at[r - 1], peer))
        for cp in copies:
            cp.start()
        for w in range(n):
            for kk, (px, py) in enumerate(others):
                _remote(pb[w].at[chips[kk]], rcv[w].at[kk], send.at[w, kk], recv.at[w, kk], (px, py, c)).wait_recv()
        for r in range(1, 8):
            px, py, pc = (1 - x if r & 4 else x, 1 - y if r & 2 else y, 1 - c if r & 1 else c)
            _remote(sm, smg.at[4 * px + 2 * py + pc], ssend.at[r - 1], srecv.at[r - 1], (px, py, pc)).wait_recv()
        for cp in copies:
            cp.wait_send()
        own.wait()

    out_shape = [jax.ShapeDtypeStruct((3,) + p.shape[1:], p.dtype) for p in parts]
    out_shape.append(jax.ShapeDtypeStruct((8,) + small.shape, small.dtype))
    return pl.pallas_call(
        body, name="chip_exchange", in_specs=[ANY] * (n + 1), out_specs=[ANY] * (n + 1), out_shape=out_shape,
        scratch_shapes=[pltpu.SemaphoreType.DMA((n, 3)), pltpu.SemaphoreType.DMA((n, 3)),
                        pltpu.SemaphoreType.DMA((7,)), pltpu.SemaphoreType.DMA((7,)), pltpu.SemaphoreType.DMA(())],
    )(*parts, small)


def _chip_sum(me_idx, pf, rcv, name):
    _, half, cols = pf.shape

    def body(me_ref, pf_ref, r_ref, t_ref):
        t_ref[...] = ((pf_ref[0] + r_ref[0].astype(F32)) + r_ref[1].astype(F32)) + r_ref[2].astype(F32)

    gs = pltpu.PrefetchScalarGridSpec(
        num_scalar_prefetch=1, grid=(1,),
        in_specs=[pl.BlockSpec((1, half, cols), lambda i, mr: (mr[0], 0, 0)),
                  pl.BlockSpec((3, half, cols), lambda i, mr: (0, 0, 0))],
        out_specs=pl.BlockSpec((half, cols), lambda i, mr: (0, 0)))
    return pl.pallas_call(
        body, name=name, grid_spec=gs, out_shape=jax.ShapeDtypeStruct((half, cols), F32),
        compiler_params=_params(("arbitrary",)),
    )(me_idx, pf, rcv)


def _pair_share(totals):
    n = len(totals)

    def body(*refs):
        t, g = refs[:n], refs[n:2 * n]
        send, recv, loc = refs[2 * n:]
        x, y, c, _, _, _ = _place()
        sib = (x, y, 1 - c)
        local = [pltpu.make_async_copy(t[w], g[w].at[c], loc.at[w]) for w in range(n)]
        copies = [_remote(t[w], g[w].at[c], send.at[w], recv.at[w], sib) for w in range(n)]
        for cp in local + copies:
            cp.start()
        for w in range(n):
            _remote(t[w], g[w].at[1 - c], send.at[w], recv.at[w], sib).wait_recv()
        for cp in copies:
            cp.wait_send()
        for cp in local:
            cp.wait()

    return pl.pallas_call(
        body, name="pair_share", in_specs=[ANY] * n, out_specs=[ANY] * n,
        out_shape=[jax.ShapeDtypeStruct((2,) + t.shape, t.dtype) for t in totals],
        scratch_shapes=[pltpu.SemaphoreType.DMA((n,)), pltpu.SemaphoreType.DMA((n,)), pltpu.SemaphoreType.DMA((n,))],
    )(*totals)


def _adamw_math(w, g, m, v):
    m = ADAM_B1 * m + (1.0 - ADAM_B1) * g
    v = ADAM_B2 * v + (1.0 - ADAM_B2) * (g * g)
    m_hat = m / (1.0 - ADAM_B1 ** ADAM_STEP)
    v_hat = v / (1.0 - ADAM_B2 ** ADAM_STEP)
    delta = -ADAM_LR * (m_hat / (jnp.sqrt(v_hat) + ADAM_EPS) + ADAM_WD * w)
    return delta, m, v


def _adamw(w, g, m, v, tr, name):
    rows, cols = w.shape

    def body(w_ref, g_ref, m_ref, v_ref, d_ref, nm_ref, nv_ref):
        d_ref[...], nm_ref[...], nv_ref[...] = _adamw_math(w_ref[...], g_ref[...], m_ref[...], v_ref[...])

    spec = pl.BlockSpec((tr, cols), lambda i: (i, 0))
    return pl.pallas_call(
        body, name=name, grid=(rows // tr,), in_specs=[spec] * 4, out_specs=[spec] * 3,
        out_shape=[jax.ShapeDtypeStruct((rows, cols), F32)] * 3, compiler_params=_params(("arbitrary",)),
    )(w, g, m, v)


def _small_update(gathered, w, m, v):
    def body(gg_ref, w_ref, m_ref, v_ref, g_ref, d_ref, nm_ref, nv_ref):
        g = gg_ref[0]
        for dev in range(1, 8):
            g = g + gg_ref[dev]
        g_ref[...] = g
        d_ref[...], nm_ref[...], nv_ref[...] = _adamw_math(w_ref[...], g, m_ref[...], v_ref[...])

    return pl.pallas_call(body, name="small_update", out_shape=[jax.ShapeDtypeStruct(w.shape, F32)] * 4,
                          compiler_params=_params())(gathered, w, m, v)


SMALL = ("g_mix_pre", "g_mix_post", "g_ffn_pre", "g_ffn_post")


def _pack_small(t, conv_full):
    conv = jnp.pad(conv_full.reshape(1, 3 * CONV_W), ((0, 0), (0, 2048 - 3 * CONV_W))).reshape(2, 1024)
    return jnp.concatenate([t[n].reshape(1, 1024) for n in SMALL]
                           + [jnp.concatenate([t["g_attn_out"].reshape(1, 512), t["g_conv_out"].reshape(1, 512)], axis=1),
                              conv, jnp.pad(t["b_forget"].reshape(1, N_HEADS), ((0, 0), (0, 1024 - N_HEADS)))], axis=0)


def _unpack_small(p, me):
    out = {n: p[i].reshape(1, 1024) for i, n in enumerate(SMALL)}
    out["g_attn_out"] = p[4, :512].reshape(1, 512)
    out["g_conv_out"] = p[4, 512:].reshape(1, 512)
    conv = p[5:7].reshape(2048)[:3 * CONV_W].reshape(3, CONV_W)
    out["conv_w"] = lax.dynamic_slice_in_dim(conv, me * 128, 128, axis=1).reshape(1, 3, 128)
    out["b_forget"] = p[7, :N_HEADS].reshape(1, N_HEADS)
    return out


def _conv_in_place(shard, me):
    return lax.dynamic_update_slice_in_dim(jnp.zeros((3, CONV_W), F32), shard, me * 128, axis=1)


def kernel(x, w_in, b_forget, conv_w, g_attn_out, g_conv_out, w_out, g_mix_pre, g_mix_post, w_gate_up, w_down, g_ffn_pre, g_ffn_post, loss_target, m_w_in, m_b_forget, m_conv_w, m_g_attn_out, m_g_conv_out, m_w_out, m_g_mix_pre, m_g_mix_post, m_w_gate_up, m_w_down, m_g_ffn_pre, m_g_ffn_post, v_w_in, v_b_forget, v_conv_w, v_g_attn_out, v_g_conv_out, v_w_out, v_g_mix_pre, v_g_mix_post, v_w_gate_up, v_w_down, v_g_ffn_pre, v_g_ffn_post):
    w = dict(w_in=w_in, b_forget=b_forget, conv_w=conv_w, g_attn_out=g_attn_out, g_conv_out=g_conv_out, w_out=w_out,
             g_mix_pre=g_mix_pre, g_mix_post=g_mix_post, w_gate_up=w_gate_up, w_down=w_down, g_ffn_pre=g_ffn_pre,
             g_ffn_post=g_ffn_post)
    m = dict(w_in=m_w_in, b_forget=m_b_forget, conv_w=m_conv_w, g_attn_out=m_g_attn_out, g_conv_out=m_g_conv_out,
             w_out=m_w_out, g_mix_pre=m_g_mix_pre, g_mix_post=m_g_mix_post, w_gate_up=m_w_gate_up, w_down=m_w_down,
             g_ffn_pre=m_g_ffn_pre, g_ffn_post=m_g_ffn_post)
    v = dict(w_in=v_w_in, b_forget=v_b_forget, conv_w=v_conv_w, g_attn_out=v_g_attn_out, g_conv_out=v_g_conv_out,
             w_out=v_w_out, g_mix_pre=v_g_mix_pre, g_mix_post=v_g_mix_post, w_gate_up=v_w_gate_up, w_down=v_w_down,
             g_ffn_pre=v_g_ffn_pre, g_ffn_post=v_g_ffn_post)
    cx, cy, cc = lax.axis_index("x"), lax.axis_index("y"), lax.axis_index("c")
    me = 2 * cx + cy
    c_idx = cc.astype(jnp.int32).reshape(1)
    me_idx = me.astype(jnp.int32).reshape(1)
    s = x.shape[1]

    shards = []
    for n in BIG:
        a = w[n][0].astype(MXU_DTYPE)
        shards.append(a.reshape(2, a.shape[0] // 2, a.shape[1]))
    g_in, g_out, g_gu, g_dn, conv_all = _gather_weights(shards, conv_w[0])
    w_in_full = jnp.transpose(g_in.reshape(N_CHIPS, D_MODEL, IN_W // N_CHIPS), (1, 0, 2)).reshape(D_MODEL, IN_W)
    w_pad = jnp.concatenate([w_in_full[:, :OFF_F + N_HEADS], jnp.zeros((D_MODEL, OFF_B - OFF_F - N_HEADS), MXU_DTYPE),
                             w_in_full[:, OFF_F + N_HEADS:]], axis=1)
    conv_full = jnp.transpose(conv_all, (1, 0, 2)).reshape(3, CONV_W)

    loss_local, grad_x, grads = _local_step(
        x[0], loss_target[0], w_pad, b_forget, conv_full, g_attn_out, g_conv_out, g_out.reshape(D_MODEL, D_MODEL),
        g_mix_pre, g_mix_post, g_gu.reshape(N_CHIPS, D_MODEL, FF_PIECE), g_dn.reshape(2, FF_PIECE, D_MODEL),
        g_ffn_pre, g_ffn_post)
    loss = 0.5 * lax.psum(loss_local, ("x", "y", "c"))

    pieces = dict(
        w_in=jnp.transpose(grads["w_in"].reshape(D_MODEL, N_CHIPS, IN_W // N_CHIPS), (1, 0, 2)),
        w_out=grads["w_out"].reshape(N_CHIPS, D_MODEL // N_CHIPS, D_MODEL),
        w_gate_up=jnp.transpose(grads["w_gate_up"].reshape(D_MODEL, N_CHIPS, FF_PIECE), (1, 0, 2)),
        w_down=grads["w_down"].reshape(N_CHIPS, D_FF // N_CHIPS, D_MODEL))
    split = [pieces[n].reshape(N_CHIPS, 2, pieces[n].shape[1] // 2, pieces[n].shape[2]) for n in BIG]
    from_sibling = _pair_exchange(split)
    sums = [_pair_sum(c_idx, g, a, "pair_sum_" + n) for n, g, a in zip(BIG, split, from_sibling)]
    small_local = _pack_small(grads, grads["conv_w"])
    *arrived, small_all = _chip_exchange([sb[1] for sb in sums], small_local)
    totals = [_chip_sum(me_idx, sb[0], r, "chip_sum_" + n) for n, sb, r in zip(BIG, sums, arrived)]
    shared = _pair_share(totals)
    gsum = {n: sh.reshape(w[n].shape[1:]) for n, sh in zip(BIG, shared)}

    delta, new_m, new_v = {}, {}, {}
    for n in BIG:
        rows = gsum[n].shape[0]
        d, nm, nv = _adamw(w[n][0], gsum[n], m[n][0], v[n][0], rows // 4, "adamw_" + n)
        delta[n], new_m[n], new_v[n] = d[None], nm[None], nv[None]
        gsum[n] = gsum[n][None]
    packed = [_pack_small(t, _conv_in_place(t["conv_w"][0], me)) for t in (w, m, v)]
    g_small, d_small, m_small, v_small = _small_update(small_all, *packed)
    gsum.update(_unpack_small(g_small, me))
    delta.update(_unpack_small(d_small, me))
    new_m.update(_unpack_small(m_small, me))
    new_v.update(_unpack_small(v_small, me))

    order = ("w_in", "b_forget", "conv_w", "g_attn_out", "g_conv_out", "w_out", "g_mix_pre", "g_mix_post",
             "w_gate_up", "w_down", "g_ffn_pre", "g_ffn_post")
    return (loss, grad_x[None], *[gsum[n] for n in order], *[delta[n] for n in order],
            *[new_m[n] for n in order], *[new_v[n] for n in order])
```

```python
import functools

import jax
import jax.numpy as jnp
import numpy as np
from jax import lax
from jax.experimental import pallas as pl
from jax.experimental.pallas import tpu as pltpu

F32 = jnp.float32
BF16 = jnp.bfloat16
MXU_DTYPE = jnp.bfloat16

D_MODEL = 1024
HEAD_DIM = 64
N_HEADS = 8
ATTN_W = 512
CONV_W = 512
D_FF = 2816
FF_PIECE = 1408
EPS = 1e-6
Q_SCALE = HEAD_DIM ** -0.5

OFF_F = 1536
OFF_B = 1664
OFF_C = 2176
OFF_U = 2688
IN_PAD = 3200
IN_W = 3080
N_CHIPS = 4

ADAM_LR = 0.001
ADAM_B1 = 0.9
ADAM_B2 = 0.999
ADAM_EPS = 1e-08
ADAM_WD = 0.01
ADAM_STEP = 10

VMEM_LIMIT_V7X = 56 * 1024 * 1024
MESH_ID = pl.DeviceIdType.MESH


def _params(sem=None, vmem=VMEM_LIMIT_V7X):
    kw = {"vmem_limit_bytes": vmem}
    if sem is not None:
        kw["dimension_semantics"] = sem
    return pltpu.CompilerParams(**kw)


def _dot(a, b):
    return jnp.dot(a, b, preferred_element_type=F32)


def _dot_nt(a, b):
    return lax.dot_general(a, b, (((1,), (1,)), ((), ())), preferred_element_type=F32)


def _dot_exact(x, ones, parts):
    if ones.dtype == F32:
        return _dot(x, ones)
    acc = None
    rem = x
    for _ in range(parts):
        piece = rem.astype(BF16)
        rem = rem - piece.astype(F32)
        term = _dot(piece, ones)
        acc = term if acc is None else acc + term
    return acc


def _rms(v):
    return lax.rsqrt(jnp.mean(v * v, axis=-1, keepdims=True) + EPS)


def _tok(tm, w):
    return pl.BlockSpec((tm, w), lambda i: (i, 0))


def _whole(shape, single=False):
    nd = len(shape)
    if single:
        return pl.BlockSpec(shape, lambda i: (0,) * nd, pipeline_mode=pl.Buffered(1))
    return pl.BlockSpec(shape, lambda i: (0,) * nd)


def _inproj_fwd(x, g_pre, w_pad, tm):
    s = x.shape[0]

    def body(x_ref, g_ref, w_ref, h_ref, q_ref, k_ref, v_ref, f_ref, b_ref, c_ref, u_ref):
        xv = x_ref[...]
        h = ((xv * _rms(xv)) * g_ref[...]).astype(MXU_DTYPE)
        h_ref[...] = h

        def proj(lo, hi):
            return _dot(h, w_ref[:, lo:hi])

        q_ref[...] = (proj(0, 512) * Q_SCALE).astype(MXU_DTYPE)
        k_ref[...] = proj(512, 1024).astype(MXU_DTYPE)
        v_ref[...] = proj(1024, OFF_F).astype(MXU_DTYPE)
        f_ref[...] = proj(OFF_F, OFF_B)
        b_ref[...] = proj(OFF_B, OFF_C)
        c_ref[...] = proj(OFF_C, OFF_U)
        u_ref[...] = proj(OFF_U, IN_PAD)

    sd = jax.ShapeDtypeStruct
    return pl.pallas_call(
        body, name="inproj_fwd", grid=(s // tm,),
        in_specs=[_tok(tm, D_MODEL), _whole((1, D_MODEL)), _whole((D_MODEL, IN_PAD), single=True)],
        out_specs=[_tok(tm, D_MODEL), _tok(tm, 512), _tok(tm, 512), _tok(tm, 512), _tok(tm, 128),
                   _tok(tm, 512), _tok(tm, 512), _tok(tm, 512)],
        out_shape=[sd((s, D_MODEL), MXU_DTYPE), sd((s, 512), MXU_DTYPE), sd((s, 512), MXU_DTYPE),
                   sd((s, 512), MXU_DTYPE), sd((s, 128), F32), sd((s, 512), F32), sd((s, 512), F32),
                   sd((s, 512), F32)],
        compiler_params=_params(("arbitrary",)),
    )(x, g_pre, w_pad)


def _tri(n, upper):
    r = lax.broadcasted_iota(jnp.int32, (n, n), 0)
    c = lax.broadcasted_iota(jnp.int32, (n, n), 1)
    return ((r <= c) if upper else (r >= c)).astype(MXU_DTYPE)


def _forget_fwd(z_t, b_col):
    rows, s = z_t.shape
    nb = s // 128

    def body(z_ref, b_ref, c_ref):
        upper = _tri(128, True)

        def blk(n, carry):
            off = pl.multiple_of(n * 128, 128)
            lf = jax.nn.log_sigmoid(z_ref[:, pl.ds(off, 128)] + b_ref[...])
            c_ref[:, pl.ds(off, 128)] = _dot_exact(lf, upper, 3) + carry
            return carry + jnp.sum(lf, axis=1, keepdims=True)

        lax.fori_loop(0, nb, blk, jnp.zeros((rows, 1), F32))

    return pl.pallas_call(body, name="forget_fwd", out_shape=jax.ShapeDtypeStruct((rows, s), F32),
                          compiler_params=_params())(z_t, b_col)


def _attn_fwd(qs, k, v_t, c_row, c_col, t):
    h, s, _ = qs.shape
    n = s // t
    pairs = [(i, j) for i in range(n) for j in range(i + 1)]
    it = jnp.asarray(np.array([p[0] for p in pairs], np.int32))
    jt = jnp.asarray(np.array([p[1] for p in pairs], np.int32))

    def body(it_ref, jt_ref, q_ref, k_ref, vt_ref, cq_ref, ck_ref, o_ref, lse_ref, m_sc, l_sc, acc_sc):
        p = pl.program_id(1)
        i = it_ref[p]
        j = jt_ref[p]

        @pl.when(j == 0)
        def _():
            m_sc[...] = jnp.full_like(m_sc, -1e30)
            l_sc[...] = jnp.zeros_like(l_sc)
            acc_sc[...] = jnp.zeros_like(acc_sc)

        def step(diagonal):
            st = _dot_nt(k_ref[0], q_ref[0]) + (cq_ref[0] - ck_ref[0])
            if diagonal:
                kpos = lax.broadcasted_iota(jnp.int32, (t, t), 0)
                qpos = lax.broadcasted_iota(jnp.int32, (t, t), 1)
                st = jnp.where(kpos <= qpos, st, -1e30)
            m_prev = m_sc[...]
            m_new = jnp.maximum(m_prev, jnp.max(st, axis=0, keepdims=True))
            alpha = jnp.exp(m_prev - m_new)
            pt = jnp.exp(st - m_new)
            l_sc[...] = alpha * l_sc[...] + jnp.sum(pt, axis=0, keepdims=True)
            acc_sc[...] = acc_sc[...] * alpha + _dot(vt_ref[0], pt.astype(MXU_DTYPE))
            m_sc[...] = m_new

        @pl.when(j < i)
        def _():
            step(False)

        @pl.when(j == i)
        def _():
            step(True)
            o_ref[0] = acc_sc[...] / l_sc[...]
            lse_ref[0] = m_sc[...] + jnp.log(l_sc[...])

    gs = pltpu.PrefetchScalarGridSpec(
        num_scalar_prefetch=2, grid=(h, len(pairs)),
        in_specs=[pl.BlockSpec((1, t, HEAD_DIM), lambda hh, p, it_, jt_: (hh, it_[p], 0)),
                  pl.BlockSpec((1, t, HEAD_DIM), lambda hh, p, it_, jt_: (hh, jt_[p], 0)),
                  pl.BlockSpec((1, HEAD_DIM, t), lambda hh, p, it_, jt_: (hh, 0, jt_[p])),
                  pl.BlockSpec((1, 1, t), lambda hh, p, it_, jt_: (hh, 0, it_[p])),
                  pl.BlockSpec((1, t, 1), lambda hh, p, it_, jt_: (hh, jt_[p], 0))],
        out_specs=[pl.BlockSpec((1, HEAD_DIM, t), lambda hh, p, it_, jt_: (hh, 0, it_[p])),
                   pl.BlockSpec((1, 1, t), lambda hh, p, it_, jt_: (hh, 0, it_[p]))],
        scratch_shapes=[pltpu.VMEM((1, t), F32), pltpu.VMEM((1, t), F32), pltpu.VMEM((HEAD_DIM, t), F32)])
    return pl.pallas_call(
        body, name="attn_fwd", grid_spec=gs,
        out_shape=[jax.ShapeDtypeStruct((h, HEAD_DIM, s), F32), jax.ShapeDtypeStruct((h, 1, s), F32)],
        compiler_params=_params(("arbitrary", "arbitrary")),
    )(it, jt, qs, k, v_t, c_row, c_col)


def _shift_down(cur, prev_ref, first):
    row = lax.broadcasted_iota(jnp.int32, cur.shape, 0)
    p7 = jnp.where(first, 0.0, prev_ref[0][7:8, :] * prev_ref[1][7:8, :])
    p6 = jnp.where(first, 0.0, prev_ref[0][6:7, :] * prev_ref[1][6:7, :])
    s1 = jnp.where(row == 0, p7, pltpu.roll(cur, 1, 0))
    s2 = jnp.where(row == 0, p6, jnp.where(row == 1, p7, pltpu.roll(cur, 2, 0)))
    return s1, s2


def _group_ms(v, gmat):
    return _dot_exact(v, gmat, 2) * (1.0 / HEAD_DIM)


def _mixer_fwd(x, o_attn, gate_b, gate_c, u, conv_w, g_attn, g_conv, w_out, g_post, gmat, tm):
    s = x.shape[0]

    def body(x_ref, o_ref, b_ref, c_ref, u_ref, cp_ref, up_ref, cw_ref, ga_ref, gc_ref, wo_ref, gp_ref, gm_ref,
             x2_ref, mg_ref, y_ref, z_ref):
        i = pl.program_id(0)
        cu = c_ref[...] * u_ref[...]
        cu1, cu2 = _shift_down(cu, (cp_ref, up_ref), i == 0)
        z = cw_ref[0:1, :] * cu2 + cw_ref[1:2, :] * cu1 + cw_ref[2:3, :] * cu
        z_ref[...] = z
        cv = b_ref[...] * z
        ov = o_ref[...]
        gm = gm_ref[...]
        ma = ((ov * lax.rsqrt(_group_ms(ov * ov, gm) + EPS)) * ga_ref[...]).astype(MXU_DTYPE)
        mc = ((cv * lax.rsqrt(_group_ms(cv * cv, gm) + EPS)) * gc_ref[...]).astype(MXU_DTYPE)
        mg_ref[:, 0:ATTN_W] = ma
        mg_ref[:, ATTN_W:D_MODEL] = mc
        y = _dot(ma, wo_ref[0:ATTN_W, :]) + _dot(mc, wo_ref[ATTN_W:D_MODEL, :])
        y_ref[...] = y
        x2_ref[...] = x_ref[...] + (y * _rms(y)) * gp_ref[...]

    halo = pl.BlockSpec((8, 512), lambda i: (jnp.maximum(i * (tm // 8) - 1, 0), 0))
    sd = jax.ShapeDtypeStruct
    return pl.pallas_call(
        body, name="mixer_fwd", grid=(s // tm,),
        in_specs=[_tok(tm, D_MODEL), _tok(tm, 512), _tok(tm, 512), _tok(tm, 512), _tok(tm, 512), halo, halo,
                  _whole((3, 512)), _whole((1, 512)), _whole((1, 512)), _whole((D_MODEL, D_MODEL), single=True),
                  _whole((1, D_MODEL)), _whole((512, 512))],
        out_specs=[_tok(tm, D_MODEL), _tok(tm, D_MODEL), _tok(tm, D_MODEL), _tok(tm, 512)],
        out_shape=[sd((s, D_MODEL), F32), sd((s, D_MODEL), MXU_DTYPE), sd((s, D_MODEL), F32), sd((s, 512), F32)],
        compiler_params=_params(("arbitrary",)),
    )(x, o_attn, gate_b, gate_c, u, gate_c, u, conv_w, g_attn, g_conv, w_out, g_post, gmat)


def _ffn_fwd(x2, target, g_pre, w_gu, w_dn, g_post, tm):
    s = x2.shape[0]

    def body(x_ref, t_ref, gpre_ref, wgu_ref, wdn_ref, gpost_ref,
             h_ref, g_ref, up_ref, a_ref, ff_ref, dout_ref, loss_ref):
        xv = x_ref[...]
        h = ((xv * _rms(xv)) * gpre_ref[...]).astype(MXU_DTYPE)
        h_ref[...] = h
        ff = jnp.zeros((tm, D_MODEL), F32)
        for j in range(2):
            cols = slice(j * FF_PIECE, (j + 1) * FF_PIECE)
            g = _dot(h, wgu_ref[j])
            up = _dot(h, wgu_ref[2 + j])
            a = ((g * jax.nn.sigmoid(g)) * up).astype(MXU_DTYPE)
            g_ref[:, cols] = g
            up_ref[:, cols] = up
            a_ref[:, cols] = a
            ff = ff + _dot(a, wdn_ref[j])
        ff_ref[...] = ff
        err = (xv + (ff * _rms(ff)) * gpost_ref[...]) - t_ref[...]
        dout_ref[...] = err * (1.0 / D_MODEL)
        part = jnp.sum(jnp.mean(err * err, axis=-1, keepdims=True), axis=0, keepdims=True)

        @pl.when(pl.program_id(0) == 0)
        def _():
            loss_ref[...] = jnp.zeros_like(loss_ref)

        loss_ref[...] += part

    sd = jax.ShapeDtypeStruct
    return pl.pallas_call(
        body, name="ffn_fwd", grid=(s // tm,),
        in_specs=[_tok(tm, D_MODEL), _tok(tm, D_MODEL), _whole((1, D_MODEL)),
                  _whole((4, D_MODEL, FF_PIECE), single=True), _whole((2, FF_PIECE, D_MODEL), single=True),
                  _whole((1, D_MODEL))],
        out_specs=[_tok(tm, D_MODEL), _tok(tm, D_FF), _tok(tm, D_FF), _tok(tm, D_FF), _tok(tm, D_MODEL),
                   _tok(tm, D_MODEL), _whole((8, 128))],
        out_shape=[sd((s, D_MODEL), MXU_DTYPE), sd((s, D_FF), F32), sd((s, D_FF), F32), sd((s, D_FF), MXU_DTYPE),
                   sd((s, D_MODEL), F32), sd((s, D_MODEL), F32), sd((8, 128), F32)],
        compiler_params=_params(("arbitrary",)),
    )(x2, target, g_pre, w_gu, w_dn, g_post)


def _norm_bwd(dy, normed, rinv, gain):
    t = dy * gain
    return rinv * (t - normed * jnp.mean(t * normed, axis=-1, keepdims=True))


def _acc_rows(ref, first, val):
    @pl.when(first)
    def _():
        ref[...] = jnp.zeros_like(ref)

    ref[...] += jnp.sum(val, axis=0, keepdims=True)


def _ffn_bwd(dout, ff, x2, g, up, g_post, g_pre, w_gu, w_dn, tm):
    s = x2.shape[0]

    def body(do_ref, ff_ref, x_ref, g_ref, up_ref, gpost_ref, gpre_ref, wgu_ref, wdn_ref,
             dx_ref, dff_ref, dgu_ref, dgpost_ref, dgpre_ref):
        first = pl.program_id(0) == 0
        ffv = ff_ref[...]
        rf = _rms(ffv)
        n = ffv * rf
        do = do_ref[...]
        _acc_rows(dgpost_ref, first, do * n)
        dff = _norm_bwd(do, n, rf, gpost_ref[...]).astype(MXU_DTYPE)
        dff_ref[...] = dff
        dh = jnp.zeros((tm, D_MODEL), F32)
        for j in range(2):
            cols = slice(j * FF_PIECE, (j + 1) * FF_PIECE)
            da = _dot_nt(dff, wdn_ref[j])
            gv = g_ref[:, cols]
            sg = jax.nn.sigmoid(gv)
            dg = (da * up_ref[:, cols] * (sg * (1.0 + gv * (1.0 - sg)))).astype(MXU_DTYPE)
            du = (da * (gv * sg)).astype(MXU_DTYPE)
            dgu_ref[:, cols] = dg
            dgu_ref[:, D_FF + j * FF_PIECE:D_FF + (j + 1) * FF_PIECE] = du
            dh = dh + _dot_nt(dg, wgu_ref[j]) + _dot_nt(du, wgu_ref[2 + j])
        xv = x_ref[...]
        r2 = _rms(xv)
        nx = xv * r2
        _acc_rows(dgpre_ref, first, dh * nx)
        dx_ref[...] = do + _norm_bwd(dh, nx, r2, gpre_ref[...])

    sd = jax.ShapeDtypeStruct
    return pl.pallas_call(
        body, name="ffn_bwd", grid=(s // tm,),
        in_specs=[_tok(tm, D_MODEL), _tok(tm, D_MODEL), _tok(tm, D_MODEL), _tok(tm, D_FF), _tok(tm, D_FF),
                  _whole((1, D_MODEL)), _whole((1, D_MODEL)),
                  _whole((4, D_MODEL, FF_PIECE), single=True), _whole((2, FF_PIECE, D_MODEL), single=True)],
        out_specs=[_tok(tm, D_MODEL), _tok(tm, D_MODEL), _tok(tm, 2 * D_FF), _whole((1, D_MODEL)),
                   _whole((1, D_MODEL))],
        out_shape=[sd((s, D_MODEL), F32), sd((s, D_MODEL), MXU_DTYPE), sd((s, 2 * D_FF), MXU_DTYPE),
                   sd((1, D_MODEL), F32), sd((1, D_MODEL), F32)],
        compiler_params=_params(("arbitrary",)),
    )(dout, ff, x2, g, up, g_post, g_pre, w_gu, w_dn)


def _tn_matmul(a, b, tm, tn, tk, name):
    s, m = a.shape
    n = b.shape[1]

    def body(a_ref, b_ref, o_ref):
        @pl.when(pl.program_id(2) == 0)
        def _():
            o_ref[...] = jnp.zeros_like(o_ref)

        o_ref[...] += lax.dot_general(a_ref[...], b_ref[...], (((0,), (0,)), ((), ())), preferred_element_type=F32)

    return pl.pallas_call(
        body, name=name, grid=(m // tm, n // tn, s // tk),
        in_specs=[pl.BlockSpec((tk, tm), lambda i, j, kk: (kk, i)), pl.BlockSpec((tk, tn), lambda i, j, kk: (kk, j))],
        out_specs=pl.BlockSpec((tm, tn), lambda i, j, kk: (i, j)),
        out_shape=jax.ShapeDtypeStruct((m, n), F32),
        compiler_params=_params(("arbitrary", "arbitrary", "arbitrary")),
    )(a, b)


def _mixer_bwd(dx2, y, o_attn, gate_b, z, g_post, g_attn, g_conv, w_out, gmat, sel, tm):
    s = dx2.shape[0]

    def body(d_ref, y_ref, o_ref, b_ref, z_ref, gp_ref, ga_ref, gc_ref, wo_ref, gm_ref, sel_ref,
             dy_ref, do_ref, db_ref, dz_ref, delta_ref, dgp_ref, dga_ref, dgc_ref):
        first = pl.program_id(0) == 0
        yv = y_ref[...]
        ry = _rms(yv)
        ny = yv * ry
        d = d_ref[...]
        _acc_rows(dgp_ref, first, d * ny)
        dy = _norm_bwd(d, ny, ry, gp_ref[...]).astype(MXU_DTYPE)
        dy_ref[...] = dy
        dm = _dot_nt(dy, wo_ref[...])
        gm = gm_ref[...]

        def group_bwd(val, dmv, gain, dg_ref):
            rg = lax.rsqrt(_group_ms(val * val, gm) + EPS)
            nv = val * rg
            _acc_rows(dg_ref, first, dmv * nv)
            t = dmv * gain
            return rg * (t - nv * _group_ms(t * nv, gm))

        ov = o_ref[...]
        d_o = group_bwd(ov, dm[:, 0:ATTN_W], ga_ref[...], dga_ref)
        do_ref[...] = d_o
        delta_ref[...] = _dot_exact(d_o * ov, sel_ref[...], 2)
        zv = z_ref[...]
        bv = b_ref[...]
        d_cv = group_bwd(bv * zv, dm[:, ATTN_W:D_MODEL], gc_ref[...], dgc_ref)
        db_ref[...] = d_cv * zv
        dz_ref[...] = d_cv * bv

    sd = jax.ShapeDtypeStruct
    return pl.pallas_call(
        body, name="mixer_bwd", grid=(s // tm,),
        in_specs=[_tok(tm, D_MODEL), _tok(tm, D_MODEL), _tok(tm, 512), _tok(tm, 512), _tok(tm, 512),
                  _whole((1, D_MODEL)), _whole((1, 512)), _whole((1, 512)),
                  _whole((D_MODEL, D_MODEL), single=True), _whole((512, 512)), _whole((512, 128))],
        out_specs=[_tok(tm, D_MODEL), _tok(tm, 512), _tok(tm, 512), _tok(tm, 512), _tok(tm, 128),
                   _whole((1, D_MODEL)), _whole((1, 512)), _whole((1, 512))],
        out_shape=[sd((s, D_MODEL), MXU_DTYPE), sd((s, 512), F32), sd((s, 512), F32), sd((s, 512), F32),
                   sd((s, 128), F32), sd((1, D_MODEL), F32), sd((1, 512), F32), sd((1, 512), F32)],
        compiler_params=_params(("arbitrary",)),
    )(dx2, y, o_attn, gate_b, z, g_post, g_attn, g_conv, w_out, gmat, sel)


def _attn_bwd(qs, k, k_t, v, do, c_row, c_col, lse, delta, t):
    h, s, _ = qs.shape
    n = s // t
    pairs = [(i, j) for j in range(n) for i in range(j, n)]
    it = jnp.asarray(np.array([p[0] for p in pairs], np.int32))
    jt = jnp.asarray(np.array([p[1] for p in pairs], np.int32))

    def body(it_ref, jt_ref, q_ref, k_ref, kt_ref, v_ref, do_ref, cq_ref, ck_ref, lse_ref, dl_ref,
             dq_ref, dk_ref, dv_ref, dc_ref, dcq_ref, dk_sc, dv_sc, dc_sc):
        p = pl.program_id(1)
        i = it_ref[p]
        j = jt_ref[p]

        @pl.when(p == 0)
        def _():
            dq_ref[...] = jnp.zeros_like(dq_ref)
            dcq_ref[...] = jnp.zeros_like(dcq_ref)

        @pl.when(i == j)
        def _():
            dk_sc[...] = jnp.zeros_like(dk_sc)
            dv_sc[...] = jnp.zeros_like(dv_sc)
            dc_sc[...] = jnp.zeros_like(dc_sc)

        def step(diagonal):
            qv = q_ref[0]
            dov = do_ref[0]
            st = _dot_nt(k_ref[0], qv) + ((cq_ref[0] - lse_ref[0]) - ck_ref[0])
            pt = jnp.exp(st)
            if diagonal:
                kpos = lax.broadcasted_iota(jnp.int32, (t, t), 0)
                qpos = lax.broadcasted_iota(jnp.int32, (t, t), 1)
                pt = jnp.where(kpos <= qpos, pt, 0.0)
            dv_sc[...] += _dot(pt.astype(MXU_DTYPE), dov)
            dst = pt * (_dot_nt(v_ref[0], dov) - dl_ref[0])
            dc_sc[...] -= jnp.sum(dst, axis=1, keepdims=True)
            dcq_ref[0, i] += jnp.sum(dst, axis=0, keepdims=True)
            dsb = dst.astype(MXU_DTYPE)
            dk_sc[...] += _dot(dsb, qv)
            dq_ref[0, i] += _dot(kt_ref[0], dsb)

        @pl.when(i > j)
        def _():
            step(False)

        @pl.when(i == j)
        def _():
            step(True)

        @pl.when(i == n - 1)
        def _():
            dk_ref[0] = dk_sc[...]
            dv_ref[0] = dv_sc[...]
            dc_ref[0] = dc_sc[...]

    qi = lambda hh, p, it_, jt_: (hh, it_[p], 0)
    kj = lambda hh, p, it_, jt_: (hh, jt_[p], 0)
    row_i = lambda hh, p, it_, jt_: (hh, 0, it_[p])
    gs = pltpu.PrefetchScalarGridSpec(
        num_scalar_prefetch=2, grid=(h, len(pairs)),
        in_specs=[pl.BlockSpec((1, t, HEAD_DIM), qi), pl.BlockSpec((1, t, HEAD_DIM), kj),
                  pl.BlockSpec((1, HEAD_DIM, t), lambda hh, p, it_, jt_: (hh, 0, jt_[p])),
                  pl.BlockSpec((1, t, HEAD_DIM), kj), pl.BlockSpec((1, t, HEAD_DIM), qi),
                  pl.BlockSpec((1, 1, t), row_i), pl.BlockSpec((1, t, 1), kj),
                  pl.BlockSpec((1, 1, t), row_i), pl.BlockSpec((1, 1, t), row_i)],
        out_specs=[pl.BlockSpec((1, n, HEAD_DIM, t), lambda hh, p, it_, jt_: (hh, 0, 0, 0)),
                   pl.BlockSpec((1, t, HEAD_DIM), kj), pl.BlockSpec((1, t, HEAD_DIM), kj),
                   pl.BlockSpec((1, t, 1), kj),
                   pl.BlockSpec((1, n, 1, t), lambda hh, p, it_, jt_: (hh, 0, 0, 0))],
        scratch_shapes=[pltpu.VMEM((t, HEAD_DIM), F32), pltpu.VMEM((t, HEAD_DIM), F32), pltpu.VMEM((t, 1), F32)])
    sd = jax.ShapeDtypeStruct
    return pl.pallas_call(
        body, name="attn_bwd", grid_spec=gs,
        out_shape=[sd((h, n, HEAD_DIM, t), F32), sd((h, s, HEAD_DIM), F32), sd((h, s, HEAD_DIM), F32),
                   sd((h, s, 1), F32), sd((h, n, 1, t), F32)],
        compiler_params=_params(("arbitrary", "arbitrary")),
    )(it, jt, qs, k, k_t, v, do, c_row, c_col, lse, delta)


def _forget_bwd(dc_t, z_t, b_col):
    rows, s = z_t.shape
    nb = s // 128

    def body(dc_ref, z_ref, b_ref, dz_ref, db_ref):
        lower = _tri(128, False)

        def blk(m, carry):
            tail, dbias = carry
            off = pl.multiple_of((nb - 1 - m) * 128, 128)
            dc = dc_ref[:, pl.ds(off, 128)]
            dlf = _dot_exact(dc, lower, 3) + tail
            dz = dlf * jax.nn.sigmoid(-(z_ref[:, pl.ds(off, 128)] + b_ref[...]))
            dz_ref[:, pl.ds(off, 128)] = dz
            return tail + jnp.sum(dc, axis=1, keepdims=True), dbias + jnp.sum(dz, axis=1, keepdims=True)

        zero = jnp.zeros((rows, 1), F32)
        _, dbias = lax.fori_loop(0, nb, blk, (zero, zero))
        db_ref[...] = jnp.broadcast_to(dbias, db_ref.shape)

    return pl.pallas_call(
        body, name="forget_bwd",
        out_shape=[jax.ShapeDtypeStruct((rows, s), F32), jax.ShapeDtypeStruct((rows, 128), F32)],
        compiler_params=_params())(dc_t, z_t, b_col)


def _inproj_bwd(dz, gate_c, u, conv_w, dq, dk, dv, dzf, db, x, dx2, g_pre, w_pad, tm):
    s = x.shape[0]
    nt = s // tm

    def body(dz_ref, dzn_ref, c_ref, u_ref, cp_ref, up_ref, cw_ref, dq_ref, dk_ref, dv_ref, dzf_ref, db_ref,
             x_ref, dx2_ref, g_ref, w_ref, gx_ref, dp_ref, dg_ref, dcw_ref):
        i = pl.program_id(0)
        first = i == 0
        last = i == nt - 1
        dzv = dz_ref[...]
        row = lax.broadcasted_iota(jnp.int32, dzv.shape, 0)
        n0 = jnp.where(last, 0.0, dzn_ref[0:1, :])
        n1 = jnp.where(last, 0.0, dzn_ref[1:2, :])
        dz1 = jnp.where(row == tm - 1, n0, pltpu.roll(dzv, tm - 1, 0))
        dz2 = jnp.where(row == tm - 1, n1, jnp.where(row == tm - 2, n0, pltpu.roll(dzv, tm - 2, 0)))
        dcu = cw_ref[2:3, :] * dzv + cw_ref[1:2, :] * dz1 + cw_ref[0:1, :] * dz2
        cv = c_ref[...]
        uv = u_ref[...]
        cu = cv * uv
        cu1, cu2 = _shift_down(cu, (cp_ref, up_ref), first)

        @pl.when(first)
        def _():
            dcw_ref[...] = jnp.zeros_like(dcw_ref)

        dcw_ref[0:1, :] += jnp.sum(dzv * cu2, axis=0, keepdims=True)
        dcw_ref[1:2, :] += jnp.sum(dzv * cu1, axis=0, keepdims=True)
        dcw_ref[2:3, :] += jnp.sum(dzv * cu, axis=0, keepdims=True)

        dp_ref[:, 0:512] = (dq_ref[...] * Q_SCALE).astype(MXU_DTYPE)
        dp_ref[:, 512:1024] = dk_ref[...].astype(MXU_DTYPE)
        dp_ref[:, 1024:OFF_F] = dv_ref[...].astype(MXU_DTYPE)
        dp_ref[:, OFF_F:OFF_B] = dzf_ref[...].astype(MXU_DTYPE)
        dp_ref[:, OFF_B:OFF_C] = db_ref[...].astype(MXU_DTYPE)
        dp_ref[:, OFF_C:OFF_U] = (dcu * uv).astype(MXU_DTYPE)
        dp_ref[:, OFF_U:IN_PAD] = (dcu * cv).astype(MXU_DTYPE)
        dh = _dot_nt(dp_ref[...], w_ref[...])
        xv = x_ref[...]
        r1 = _rms(xv)
        nx = xv * r1
        _acc_rows(dg_ref, first, dh * nx)
        gx_ref[...] = dx2_ref[...] + _norm_bwd(dh, nx, r1, g_ref[...])

    prev = pl.BlockSpec((8, 512), lambda i: (jnp.maximum(i * (tm // 8) - 1, 0), 0))
    nxt = pl.BlockSpec((8, 512), lambda i: (jnp.minimum((i + 1) * (tm // 8), s // 8 - 1), 0))
    sd = jax.ShapeDtypeStruct
    return pl.pallas_call(
        body, name="inproj_bwd", grid=(nt,),
        in_specs=[_tok(tm, 512), nxt, _tok(tm, 512), _tok(tm, 512), prev, prev, _whole((3, 512)),
                  _tok(tm, 512), _tok(tm, 512), _tok(tm, 512), _tok(tm, 128), _tok(tm, 512),
                  _tok(tm, D_MODEL), _tok(tm, D_MODEL), _whole((1, D_MODEL)), _whole((D_MODEL, IN_PAD), single=True)],
        out_specs=[_tok(tm, D_MODEL), _tok(tm, IN_PAD), _whole((1, D_MODEL)), _whole((8, 512))],
        out_shape=[sd((s, D_MODEL), F32), sd((s, IN_PAD), MXU_DTYPE), sd((1, D_MODEL), F32), sd((8, 512), F32)],
        compiler_params=_params(("arbitrary",)),
    )(dz, dz, gate_c, u, gate_c, u, conv_w, dq, dk, dv, dzf, db, x, dx2, g_pre, w_pad)


def _heads(a):
    s = a.shape[0]
    return jnp.transpose(a.reshape(s, N_HEADS, HEAD_DIM), (1, 0, 2))


def _heads_t(a):
    s = a.shape[0]
    return jnp.transpose(a.reshape(s, N_HEADS, HEAD_DIM), (1, 2, 0))


def _unheads(a):
    s = a.shape[1]
    return jnp.transpose(a, (1, 0, 2)).reshape(s, N_HEADS * HEAD_DIM)


def _tile(s, want):
    return want if s % want == 0 else s


def _local_step(x, target, w_pad, b_forget, conv_w, g_attn, g_conv, w_out, g_mix_pre, g_mix_post,
                w_gu, w_dn, g_ffn_pre, g_ffn_post):
    s = x.shape[0]
    tm = _tile(s, 512)
    tf = _tile(s, 256)
    ta = _tile(s, 512)
    rows = 16
    gidx = np.arange(512) // HEAD_DIM
    gmat = jnp.asarray(gidx[:, None] == gidx[None, :], MXU_DTYPE)
    sel = jnp.asarray(gidx[:, None] == np.arange(128)[None, :], MXU_DTYPE)

    h1, qs, k, v, f, gate_b, gate_c, u = _inproj_fwd(x, g_mix_pre, w_pad, tm)
    z_t = jnp.pad(jnp.transpose(f[:, :N_HEADS]), ((0, rows - N_HEADS), (0, 0)))
    b_col = jnp.pad(jnp.transpose(b_forget), ((0, rows - N_HEADS), (0, 0)))
    c_t = _forget_fwd(z_t, b_col)
    c_row = c_t[:N_HEADS].reshape(N_HEADS, 1, s)
    c_col = c_t[:N_HEADS].reshape(N_HEADS, s, 1)
    qh, kh, vh = _heads(qs), _heads(k), _heads(v)
    o_t, lse = _attn_fwd(qh, kh, _heads_t(v), c_row, c_col, ta)
    o_attn = jnp.transpose(o_t, (2, 0, 1)).reshape(s, ATTN_W)
    x2, merged, y, z = _mixer_fwd(x, o_attn, gate_b, gate_c, u, conv_w, g_attn, g_conv, w_out, g_mix_post, gmat, tm)
    h2, g, up, a, ff, dout, loss_acc = _ffn_fwd(x2, target, g_ffn_pre, w_gu, w_dn, g_ffn_post, tf)

    dx2, dff, dgu, dg_ffn_post, dg_ffn_pre = _ffn_bwd(dout, ff, x2, g, up, g_ffn_post, g_ffn_pre, w_gu, w_dn, tf)
    tkk = _tile(s, 512)
    dw_dn = _tn_matmul(a, dff, FF_PIECE, 512, tkk, "dw_down")
    dw_gu = _tn_matmul(h2, dgu, 1024, 512, tkk, "dw_gate_up")
    dy, d_o, d_b, dz, delta8, dg_mix_post, dg_attn, dg_conv = _mixer_bwd(
        dx2, y, o_attn, gate_b, z, g_mix_post, g_attn, g_conv, w_out, gmat, sel, tm)
    dw_out = _tn_matmul(merged, dy, 512, 512, tkk, "dw_out")
    delta = jnp.transpose(delta8[:, :N_HEADS]).reshape(N_HEADS, 1, s)
    dq4, dkh, dvh, dc, dcq = _attn_bwd(qh, kh, _heads_t(k), vh, _heads(d_o.astype(MXU_DTYPE)), c_row, c_col, lse, delta, ta)
    dc_t = jnp.pad(dc.reshape(N_HEADS, s) + dcq.reshape(N_HEADS, s), ((0, rows - N_HEADS), (0, 0)))
    dz_t, db_f = _forget_bwd(dc_t, z_t, b_col)
    dzf = jnp.pad(jnp.transpose(dz_t[:N_HEADS]), ((0, 0), (0, 128 - N_HEADS)))
    dq = jnp.transpose(dq4, (1, 3, 0, 2)).reshape(s, ATTN_W)
    grad_x, dproj, dg_mix_pre, dcw = _inproj_bwd(dz, gate_c, u, conv_w, dq, _unheads(dkh), _unheads(dvh), dzf, d_b,
                                                 x, dx2, g_mix_pre, w_pad, tm)
    dw_pad = _tn_matmul(h1, dproj, 1024, 640, tkk, "dw_in")
    dw_in = jnp.concatenate([dw_pad[:, :OFF_F + N_HEADS], dw_pad[:, OFF_B:]], axis=1)
    grads = dict(w_in=dw_in, b_forget=db_f[:N_HEADS, 0].reshape(1, N_HEADS), conv_w=dcw[:3], g_attn_out=dg_attn,
                 g_conv_out=dg_conv, w_out=dw_out, g_mix_pre=dg_mix_pre, g_mix_post=dg_mix_post,
                 w_gate_up=dw_gu, w_down=dw_dn, g_ffn_pre=dg_ffn_pre, g_ffn_post=dg_ffn_post)
    return loss_acc[0, 0], grad_x, grads


BIG = ("w_in", "w_out", "w_gate_up", "w_down")
ANY = pl.BlockSpec(memory_space=pl.ANY)


def _place():
    x, y, c = lax.axis_index("x"), lax.axis_index("y"), lax.axis_index("c")
    others = [(1 - x, y), (x, 1 - y), (1 - x, 1 - y)]
    return x, y, c, 2 * x + y, others, [2 * px + py for px, py in others]


def _remote(src, dst, send, recv, dev):
    return pltpu.make_async_remote_copy(src_ref=src, dst_ref=dst, send_sem=send, recv_sem=recv,
                                        device_id=dev, device_id_type=MESH_ID)


def _gather_weights(shards, conv_w):
    n = len(shards)

    def body(*refs):
        sh, cw, outs, cwo = refs[:n], refs[n], refs[n + 1:2 * n + 1], refs[2 * n + 1]
        send, recv = refs[2 * n + 2:]
        x, y, c, me, others, chips = _place()
        sib = (x, y, 1 - c)
        sends = []
        for w in range(n):
            for kk, (px, py) in enumerate(others):
                sends.append(_remote(sh[w].at[c], outs[w].at[me, c], send.at[w, kk], recv.at[w, kk], (px, py, c)))
        for kk, (px, py) in enumerate(others):
            sends.append(_remote(cw, cwo.at[me], send.at[n, kk], recv.at[n, kk], (px, py, c)))
        for cp in sends:
            cp.start()
        for w in range(n):
            for kk, (px, py) in enumerate(others):
                landed = outs[w].at[chips[kk], c]
                _remote(landed, landed, send.at[w, kk], recv.at[w, kk], (px, py, c)).wait_recv()
                fwd = _remote(landed, landed, send.at[w, 3 + kk], recv.at[w, 3 + kk], sib)
                fwd.start()
                sends.append(fwd)
        for kk, (px, py) in enumerate(others):
            _remote(cw, cwo.at[chips[kk]], send.at[n, kk], recv.at[n, kk], (px, py, c)).wait_recv()
        for w in range(n):
            for kk in range(3):
                passed = outs[w].at[chips[kk], 1 - c]
                _remote(passed, passed, send.at[w, 3 + kk], recv.at[w, 3 + kk], sib).wait_recv()
        for cp in sends:
            cp.wait_send()

    out_shape = [jax.ShapeDtypeStruct((N_CHIPS,) + s.shape, s.dtype) for s in shards]
    out_shape.append(jax.ShapeDtypeStruct((N_CHIPS,) + conv_w.shape, conv_w.dtype))
    got = pl.pallas_call(
        body, name="gather_weights", in_specs=[ANY] * (n + 1), out_specs=[ANY] * (n + 1), out_shape=out_shape,
        scratch_shapes=[pltpu.SemaphoreType.DMA((n + 1, 6)), pltpu.SemaphoreType.DMA((n + 1, 6))],
    )(*shards, conv_w)
    me = 2 * lax.axis_index("x") + lax.axis_index("y")
    return [lax.dynamic_update_index_in_dim(g, own, me, 0) for g, own in zip(got, list(shards) + [conv_w])]


def _pair_exchange(grads):
    n = len(grads)

    def body(*refs):
        g, a = refs[:n], refs[n:2 * n]
        send, recv = refs[2 * n:]
        x, y, c, _, _, _ = _place()
        sib = (x, y, 1 - c)
        copies = [_remote(g[w].at[p, 1 - c], a[w].at[p], send.at[w, p], recv.at[w, p], sib)
                  for w in range(n) for p in range(N_CHIPS)]
        for cp in copies:
            cp.start()
        for cp in copies:
            cp.wait()

    return pl.pallas_call(
        body, name="pair_exchange", in_specs=[ANY] * n, out_specs=[ANY] * n,
        out_shape=[jax.ShapeDtypeStruct((N_CHIPS,) + g.shape[2:], g.dtype) for g in grads],
        scratch_shapes=[pltpu.SemaphoreType.DMA((n, N_CHIPS)), pltpu.SemaphoreType.DMA((n, N_CHIPS))],
    )(*grads)


def _pair_sum(c_idx, g, a, name):
    _, _, half, cols = g.shape

    def body(c_ref, g_ref, a_ref, pf_ref, pb_ref):
        tot = g_ref[0, 0] + a_ref[0]
        pf_ref[0] = tot
        pb_ref[0] = tot.astype(BF16)

    gs = pltpu.PrefetchScalarGridSpec(
        num_scalar_prefetch=1, grid=(N_CHIPS,),
        in_specs=[pl.BlockSpec((1, 1, half, cols), lambda p, cr: (p, cr[0], 0, 0)),
                  pl.BlockSpec((1, half, cols), lambda p, cr: (p, 0, 0))],
        out_specs=[pl.BlockSpec((1, half, cols), lambda p, cr: (p, 0, 0)),
                   pl.BlockSpec((1, half, cols), lambda p, cr: (p, 0, 0))])
    return pl.pallas_call(
        body, name=name, grid_spec=gs,
        out_shape=[jax.ShapeDtypeStruct((N_CHIPS, half, cols), F32), jax.ShapeDtypeStruct((N_CHIPS, half, cols), BF16)],
        compiler_params=_params(("arbitrary",)),
    )(c_idx, g, a)


def _chip_exchange(parts, small):
    n = len(parts)

    def body(*refs):
        pb, sm, rcv, smg = refs[:n], refs[n], refs[n + 1:2 * n + 1], refs[2 * n + 1]
        send, recv, ssend, srecv, loc = refs[2 * n + 2:]
        x, y, c, _, others, chips = _place()
        mine = 4 * x + 2 * y + c
        own = pltpu.make_async_copy(sm, smg.at[mine], loc)
        own.start()
        copies = [_remote(pb[w].at[chips[kk]], rcv[w].at[kk], send.at[w, kk], recv.at[w, kk], (px, py, c))
                  for w in range(n) for kk, (px, py) in enumerate(others)]
        for r in range(1, 8):
            peer = (1 - x if r & 4 else x, 1 - y if r & 2 else y, 1 - c if r & 1 else c)
            copies.append(_remote(sm, smg.at[mine], ssend.at[r - 1], srecv.at[r - 1], peer))
        for cp in copies:
            cp.start()
        for w in range(n):
            for kk, (px, py) in enumerate(others):
                _remote(pb[w].at[chips[kk]], rcv[w].at[kk], send.at[w, kk], recv.at[w, kk], (px, py, c)).wait_recv()
        for r in range(1, 8):
            px, py, pc = (1 - x if r & 4 else x, 1 - y if r & 2 else y, 1 - c if r & 1 else c)
            _remote(sm, smg.at[4 * px + 2 * py + pc], ssend.at[r - 1], srecv.at[r - 1], (px, py, pc)).wait_recv()
        for cp in copies:
            cp.wait_send()
        own.wait()

    out_shape = [jax.ShapeDtypeStruct((3,) + p.shape[1:], p.dtype) for p in parts]
    out_shape.append(jax.ShapeDtypeStruct((8,) + small.shape, small.dtype))
    return pl.pallas_call(
        body, name="chip_exchange", in_specs=[ANY] * (n + 1), out_specs=[ANY] * (n + 1), out_shape=out_shape,
        scratch_shapes=[pltpu.SemaphoreType.DMA((n, 3)), pltpu.SemaphoreType.DMA((n, 3)),
                        pltpu.SemaphoreType.DMA((7,)), pltpu.SemaphoreType.DMA((7,)), pltpu.SemaphoreType.DMA(())],
    )(*parts, small)


def _chip_sum(me_idx, pf, rcv, name):
    _, half, cols = pf.shape

    def body(me_ref, pf_ref, r_ref, t_ref):
        t_ref[...] = ((pf_ref[0] + r_ref[0].astype(F32)) + r_ref[1].astype(F32)) + r_ref[2].astype(F32)

    gs = pltpu.PrefetchScalarGridSpec(
        num_scalar_prefetch=1, grid=(1,),
        in_specs=[pl.BlockSpec((1, half, cols), lambda i, mr: (mr[0], 0, 0)),
                  pl.BlockSpec((3, half, cols), lambda i, mr: (0, 0, 0))],
        out_specs=pl.BlockSpec((half, cols), lambda i, mr: (0, 0)))
    return pl.pallas_call(
        body, name=name, grid_spec=gs, out_shape=jax.ShapeDtypeStruct((half, cols), F32),
        compiler_params=_params(("arbitrary",)),
    )(me_idx, pf, rcv)


def _pair_share(totals):
    n = len(totals)

    def body(*refs):
        t, g = refs[:n], refs[n:2 * n]
        send, recv = refs[2 * n:]
        x, y, c, _, _, _ = _place()
        copies = [_remote(t[w], g[w], send.at[w], recv.at[w], (x, y, 1 - c)) for w in range(n)]
        for cp in copies:
            cp.start()
        for cp in copies:
            cp.wait()

    return pl.pallas_call(
        body, name="pair_share", in_specs=[ANY] * n, out_specs=[ANY] * n,
        out_shape=[jax.ShapeDtypeStruct(t.shape, t.dtype) for t in totals],
        scratch_shapes=[pltpu.SemaphoreType.DMA((n,)), pltpu.SemaphoreType.DMA((n,))],
    )(*totals)


def _adamw_math(w, g, m, v):
    m = ADAM_B1 * m + (1.0 - ADAM_B1) * g
    v = ADAM_B2 * v + (1.0 - ADAM_B2) * (g * g)
    m_hat = m / (1.0 - ADAM_B1 ** ADAM_STEP)
    v_hat = v / (1.0 - ADAM_B2 ** ADAM_STEP)
    delta = -ADAM_LR * (m_hat / (jnp.sqrt(v_hat) + ADAM_EPS) + ADAM_WD * w)
    return delta, m, v


def _adamw(c_idx, w, mine, theirs, m, v, nb, name):
    rows, cols = w.shape
    tr = rows // (2 * nb)

    def body(c_ref, w_ref, a_ref, b_ref, m_ref, v_ref, g_ref, d_ref, nm_ref, nv_ref):
        g = jnp.where(pl.program_id(0) == c_ref[0], a_ref[...], b_ref[...])
        g_ref[...] = g
        d_ref[...], nm_ref[...], nv_ref[...] = _adamw_math(w_ref[...], g, m_ref[...], v_ref[...])

    full = pl.BlockSpec((tr, cols), lambda hh, i, cr: (hh * nb + i, 0))
    half = pl.BlockSpec((tr, cols), lambda hh, i, cr: (i, 0))
    gs = pltpu.PrefetchScalarGridSpec(num_scalar_prefetch=1, grid=(2, nb), in_specs=[full, half, half, full, full],
                                      out_specs=[full] * 4)
    return pl.pallas_call(
        body, name=name, grid_spec=gs, out_shape=[jax.ShapeDtypeStruct((rows, cols), F32)] * 4,
        compiler_params=_params(("arbitrary", "arbitrary")),
    )(c_idx, w, mine, theirs, m, v)


def _small_update(gathered, w, m, v):
    def body(gg_ref, w_ref, m_ref, v_ref, g_ref, d_ref, nm_ref, nv_ref):
        g = gg_ref[0]
        for dev in range(1, 8):
            g = g + gg_ref[dev]
        g_ref[...] = g
        d_ref[...], nm_ref[...], nv_ref[...] = _adamw_math(w_ref[...], g, m_ref[...], v_ref[...])

    return pl.pallas_call(body, name="small_update", out_shape=[jax.ShapeDtypeStruct(w.shape, F32)] * 4,
                          compiler_params=_params())(gathered, w, m, v)


SMALL = ("g_mix_pre", "g_mix_post", "g_ffn_pre", "g_ffn_post")


def _pack_small(t, conv_full):
    conv = jnp.pad(conv_full.reshape(1, 3 * CONV_W), ((0, 0), (0, 2048 - 3 * CONV_W))).reshape(2, 1024)
    return jnp.concatenate([t[n].reshape(1, 1024) for n in SMALL]
                           + [jnp.concatenate([t["g_attn_out"].reshape(1, 512), t["g_conv_out"].reshape(1, 512)], axis=1),
                              conv, jnp.pad(t["b_forget"].reshape(1, N_HEADS), ((0, 0), (0, 1024 - N_HEADS)))], axis=0)


def _unpack_small(p, me):
    out = {n: p[i].reshape(1, 1024) for i, n in enumerate(SMALL)}
    out["g_attn_out"] = p[4, :512].reshape(1, 512)
    out["g_conv_out"] = p[4, 512:].reshape(1, 512)
    conv = p[5:7].reshape(2048)[:3 * CONV_W].reshape(3, CONV_W)
    out["conv_w"] = lax.dynamic_slice_in_dim(conv, me * 128, 128, axis=1).reshape(1, 3, 128)
    out["b_forget"] = p[7, :N_HEADS].reshape(1, N_HEADS)
    return out


def _conv_in_place(shard, me):
    return lax.dynamic_update_slice_in_dim(jnp.zeros((3, CONV_W), F32), shard, me * 128, axis=1)


def kernel(x, w_in, b_forget, conv_w, g_attn_out, g_conv_out, w_out, g_mix_pre, g_mix_post, w_gate_up, w_down, g_ffn_pre, g_ffn_post, loss_target, m_w_in, m_b_forget, m_conv_w, m_g_attn_out, m_g_conv_out, m_w_out, m_g_mix_pre, m_g_mix_post, m_w_gate_up, m_w_down, m_g_ffn_pre, m_g_ffn_post, v_w_in, v_b_forget, v_conv_w, v_g_attn_out, v_g_conv_out, v_w_out, v_g_mix_pre, v_g_mix_post, v_w_gate_up, v_w_down, v_g_ffn_pre, v_g_ffn_post):
    w = dict(w_in=w_in, b_forget=b_forget, conv_w=conv_w, g_attn_out=g_attn_out, g_conv_out=g_conv_out, w_out=w_out,
             g_mix_pre=g_mix_pre, g_mix_post=g_mix_post, w_gate_up=w_gate_up, w_down=w_down, g_ffn_pre=g_ffn_pre,
             g_ffn_post=g_ffn_post)
    m = dict(w_in=m_w_in, b_forget=m_b_forget, conv_w=m_conv_w, g_attn_out=m_g_attn_out, g_conv_out=m_g_conv_out,
             w_out=m_w_out, g_mix_pre=m_g_mix_pre, g_mix_post=m_g_mix_post, w_gate_up=m_w_gate_up, w_down=m_w_down,
             g_ffn_pre=m_g_ffn_pre, g_ffn_post=m_g_ffn_post)
    v = dict(w_in=v_w_in, b_forget=v_b_forget, conv_w=v_conv_w, g_attn_out=v_g_attn_out, g_conv_out=v_g_conv_out,
             w_out=v_w_out, g_mix_pre=v_g_mix_pre, g_mix_post=v_g_mix_post, w_gate_up=v_w_gate_up, w_down=v_w_down,
             g_ffn_pre=v_g_ffn_pre, g_ffn_post=v_g_ffn_post)
    cx, cy, cc = lax.axis_index("x"), lax.axis_index("y"), lax.axis_index("c")
    me = 2 * cx + cy
    c_idx = cc.astype(jnp.int32).reshape(1)
    me_idx = me.astype(jnp.int32).reshape(1)
    s = x.shape[1]

    shards = []
    for n in BIG:
        a = w[n][0].astype(MXU_DTYPE)
        shards.append(a.reshape(2, a.shape[0] // 2, a.shape[1]))
    g_in, g_out, g_gu, g_dn, conv_all = _gather_weights(shards, conv_w[0])
    w_in_full = jnp.transpose(g_in.reshape(N_CHIPS, D_MODEL, IN_W // N_CHIPS), (1, 0, 2)).reshape(D_MODEL, IN_W)
    w_pad = jnp.concatenate([w_in_full[:, :OFF_F + N_HEADS], jnp.zeros((D_MODEL, OFF_B - OFF_F - N_HEADS), MXU_DTYPE),
                             w_in_full[:, OFF_F + N_HEADS:]], axis=1)
    conv_full = jnp.transpose(conv_all, (1, 0, 2)).reshape(3, CONV_W)

    loss_local, grad_x, grads = _local_step(
        x[0], loss_target[0], w_pad, b_forget, conv_full, g_attn_out, g_conv_out, g_out.reshape(D_MODEL, D_MODEL),
        g_mix_pre, g_mix_post, g_gu.reshape(N_CHIPS, D_MODEL, FF_PIECE), g_dn.reshape(2, FF_PIECE, D_MODEL),
        g_ffn_pre, g_ffn_post)
    loss = 0.5 * lax.psum(loss_local, ("x", "y", "c"))

    pieces = dict(
        w_in=jnp.transpose(grads["w_in"].reshape(D_MODEL, N_CHIPS, IN_W // N_CHIPS), (1, 0, 2)),
        w_out=grads["w_out"].reshape(N_CHIPS, D_MODEL // N_CHIPS, D_MODEL),
        w_gate_up=jnp.transpose(grads["w_gate_up"].reshape(D_MODEL, N_CHIPS, FF_PIECE), (1, 0, 2)),
        w_down=grads["w_down"].reshape(N_CHIPS, D_FF // N_CHIPS, D_MODEL))
    split = [pieces[n].reshape(N_CHIPS, 2, pieces[n].shape[1] // 2, pieces[n].shape[2]) for n in BIG]
    from_sibling = _pair_exchange(split)
    sums = [_pair_sum(c_idx, g, a, "pair_sum_" + n) for n, g, a in zip(BIG, split, from_sibling)]
    small_local = _pack_small(grads, grads["conv_w"])
    *arrived, small_all = _chip_exchange([sb[1] for sb in sums], small_local)
    totals = [_chip_sum(me_idx, sb[0], r, "chip_sum_" + n) for n, sb, r in zip(BIG, sums, arrived)]
    shared = _pair_share(totals)

    gsum, delta, new_m, new_v = {}, {}, {}, {}
    for n, mine, theirs in zip(BIG, totals, shared):
        gs, d, nm, nv = _adamw(c_idx, w[n][0], mine, theirs, m[n][0], v[n][0], 2, "adamw_" + n)
        gsum[n], delta[n], new_m[n], new_v[n] = gs[None], d[None], nm[None], nv[None]
    packed = [_pack_small(t, _conv_in_place(t["conv_w"][0], me)) for t in (w, m, v)]
    g_small, d_small, m_small, v_small = _small_update(small_all, *packed)
    gsum.update(_unpack_small(g_small, me))
    delta.update(_unpack_small(d_small, me))
    new_m.update(_unpack_small(m_small, me))
    new_v.update(_unpack_small(v_small, me))

    order = ("w_in", "b_forget", "conv_w", "g_attn_out", "g_conv_out", "w_out", "g_mix_pre", "g_mix_post",
             "w_gate_up", "w_down", "g_ffn_pre", "g_ffn_post")
    return (loss, grad_x[None], *[gsum[n] for n in order], *[delta[n] for n in order],
            *[new_m[n] for n in order], *[new_v[n] for n in order])
```

```python
import functools

import jax
import jax.numpy as jnp
import numpy as np
from jax import lax
from jax.experimental import pallas as pl
from jax.experimental.pallas import tpu as pltpu

F32 = jnp.float32
BF16 = jnp.bfloat16
MXU_DTYPE = jnp.bfloat16

D_MODEL = 1024
HEAD_DIM = 64
N_HEADS = 8
ATTN_W = 512
CONV_W = 512
D_FF = 2816
FF_PIECE = 1408
EPS = 1e-6
Q_SCALE = HEAD_DIM ** -0.5

OFF_F = 1536
OFF_B = 1664
OFF_C = 2176
OFF_U = 2688
IN_PAD = 3200
IN_W = 3080
N_CHIPS = 4

ADAM_LR = 0.001
ADAM_B1 = 0.9
ADAM_B2 = 0.999
ADAM_EPS = 1e-08
ADAM_WD = 0.01
ADAM_STEP = 10

VMEM_LIMIT_V7X = 56 * 1024 * 1024
MESH_ID = pl.DeviceIdType.MESH


def _params(sem=None, vmem=VMEM_LIMIT_V7X):
    kw = {"vmem_limit_bytes": vmem}
    if sem is not None:
        kw["dimension_semantics"] = sem
    return pltpu.CompilerParams(**kw)


def _dot(a, b):
    return jnp.dot(a, b, preferred_element_type=F32)


def _dot_nt(a, b):
    return lax.dot_general(a, b, (((1,), (1,)), ((), ())), preferred_element_type=F32)


def _dot_exact(x, ones, parts):
    if ones.dtype == F32:
        return _dot(x, ones)
    acc = None
    rem = x
    for _ in range(parts):
        piece = rem.astype(BF16)
        rem = rem - piece.astype(F32)
        term = _dot(piece, ones)
        acc = term if acc is None else acc + term
    return acc


def _rms(v):
    return lax.rsqrt(jnp.mean(v * v, axis=-1, keepdims=True) + EPS)


def _tok(tm, w):
    return pl.BlockSpec((tm, w), lambda i: (i, 0))


def _whole(shape, single=False):
    nd = len(shape)
    if single:
        return pl.BlockSpec(shape, lambda i: (0,) * nd, pipeline_mode=pl.Buffered(1))
    return pl.BlockSpec(shape, lambda i: (0,) * nd)


def _inproj_fwd(x, g_pre, w_pad, tm):
    s = x.shape[0]

    def body(x_ref, g_ref, w_ref, h_ref, q_ref, k_ref, v_ref, f_ref, b_ref, c_ref, u_ref):
        xv = x_ref[...]
        h = ((xv * _rms(xv)) * g_ref[...]).astype(MXU_DTYPE)
        h_ref[...] = h

        def proj(lo, hi):
            return _dot(h, w_ref[:, lo:hi])

        q_ref[...] = (proj(0, 512) * Q_SCALE).astype(MXU_DTYPE)
        k_ref[...] = proj(512, 1024).astype(MXU_DTYPE)
        v_ref[...] = proj(1024, OFF_F).astype(MXU_DTYPE)
        f_ref[...] = proj(OFF_F, OFF_B)
        b_ref[...] = proj(OFF_B, OFF_C)
        c_ref[...] = proj(OFF_C, OFF_U)
        u_ref[...] = proj(OFF_U, IN_PAD)

    sd = jax.ShapeDtypeStruct
    return pl.pallas_call(
        body, name="inproj_fwd", grid=(s // tm,),
        in_specs=[_tok(tm, D_MODEL), _whole((1, D_MODEL)), _whole((D_MODEL, IN_PAD), single=True)],
        out_specs=[_tok(tm, D_MODEL), _tok(tm, 512), _tok(tm, 512), _tok(tm, 512), _tok(tm, 128),
                   _tok(tm, 512), _tok(tm, 512), _tok(tm, 512)],
        out_shape=[sd((s, D_MODEL), MXU_DTYPE), sd((s, 512), MXU_DTYPE), sd((s, 512), MXU_DTYPE),
                   sd((s, 512), MXU_DTYPE), sd((s, 128), F32), sd((s, 512), F32), sd((s, 512), F32),
                   sd((s, 512), F32)],
        compiler_params=_params(("arbitrary",)),
    )(x, g_pre, w_pad)


def _tri(n, upper):
    r = lax.broadcasted_iota(jnp.int32, (n, n), 0)
    c = lax.broadcasted_iota(jnp.int32, (n, n), 1)
    return ((r <= c) if upper else (r >= c)).astype(MXU_DTYPE)


def _forget_fwd(z_t, b_col):
    rows, s = z_t.shape
    nb = s // 128

    def body(z_ref, b_ref, c_ref):
        upper = _tri(128, True)

        def blk(n, carry):
            off = pl.multiple_of(n * 128, 128)
            lf = jax.nn.log_sigmoid(z_ref[:, pl.ds(off, 128)] + b_ref[...])
            c_ref[:, pl.ds(off, 128)] = _dot_exact(lf, upper, 3) + carry
            return carry + jnp.sum(lf, axis=1, keepdims=True)

        lax.fori_loop(0, nb, blk, jnp.zeros((rows, 1), F32))

    return pl.pallas_call(body, name="forget_fwd", out_shape=jax.ShapeDtypeStruct((rows, s), F32),
                          compiler_params=_params())(z_t, b_col)


def _attn_fwd(qs, k, v_t, c_row, c_col, t):
    h, s, _ = qs.shape
    n = s // t
    pairs = [(i, j) for i in range(n) for j in range(i + 1)]
    it = jnp.asarray(np.array([p[0] for p in pairs], np.int32))
    jt = jnp.asarray(np.array([p[1] for p in pairs], np.int32))

    def body(it_ref, jt_ref, q_ref, k_ref, vt_ref, cq_ref, ck_ref, o_ref, lse_ref, m_sc, l_sc, acc_sc):
        p = pl.program_id(0)
        i = it_ref[p]
        j = jt_ref[p]

        @pl.when(j == 0)
        def _():
            m_sc[...] = jnp.full_like(m_sc, -1e30)
            l_sc[...] = jnp.zeros_like(l_sc)
            acc_sc[...] = jnp.zeros_like(acc_sc)

        def step(hh, diagonal):
            st = _dot_nt(k_ref[hh], q_ref[hh]) + (cq_ref[hh] - ck_ref[hh])
            if diagonal:
                kpos = lax.broadcasted_iota(jnp.int32, (t, t), 0)
                qpos = lax.broadcasted_iota(jnp.int32, (t, t), 1)
                st = jnp.where(kpos <= qpos, st, -1e30)
            m_prev = m_sc[hh]
            m_new = jnp.maximum(m_prev, jnp.max(st, axis=0, keepdims=True))
            alpha = jnp.exp(m_prev - m_new)
            pt = jnp.exp(st - m_new)
            l_sc[hh] = alpha * l_sc[hh] + jnp.sum(pt, axis=0, keepdims=True)
            acc_sc[hh] = acc_sc[hh] * alpha + _dot(vt_ref[hh], pt.astype(MXU_DTYPE))
            m_sc[hh] = m_new

        @pl.when(j < i)
        def _():
            @pl.loop(0, h)
            def _(hh):
                step(hh, False)

        @pl.when(j == i)
        def _():
            @pl.loop(0, h)
            def _(hh):
                step(hh, True)
                o_ref[hh] = acc_sc[hh] / l_sc[hh]
                lse_ref[hh] = m_sc[hh] + jnp.log(l_sc[hh])

    gs = pltpu.PrefetchScalarGridSpec(
        num_scalar_prefetch=2, grid=(len(pairs),),
        in_specs=[pl.BlockSpec((h, t, HEAD_DIM), lambda p, it_, jt_: (0, it_[p], 0)),
                  pl.BlockSpec((h, t, HEAD_DIM), lambda p, it_, jt_: (0, jt_[p], 0)),
                  pl.BlockSpec((h, HEAD_DIM, t), lambda p, it_, jt_: (0, 0, jt_[p])),
                  pl.BlockSpec((h, 1, t), lambda p, it_, jt_: (0, 0, it_[p])),
                  pl.BlockSpec((h, t, 1), lambda p, it_, jt_: (0, jt_[p], 0))],
        out_specs=[pl.BlockSpec((h, HEAD_DIM, t), lambda p, it_, jt_: (0, 0, it_[p])),
                   pl.BlockSpec((h, 1, t), lambda p, it_, jt_: (0, 0, it_[p]))],
        scratch_shapes=[pltpu.VMEM((h, 1, t), F32), pltpu.VMEM((h, 1, t), F32), pltpu.VMEM((h, HEAD_DIM, t), F32)])
    return pl.pallas_call(
        body, name="attn_fwd", grid_spec=gs,
        out_shape=[jax.ShapeDtypeStruct((h, HEAD_DIM, s), F32), jax.ShapeDtypeStruct((h, 1, s), F32)],
        compiler_params=_params(("arbitrary",)),
    )(it, jt, qs, k, v_t, c_row, c_col)


def _shift_down(cur, prev_ref, first):
    row = lax.broadcasted_iota(jnp.int32, cur.shape, 0)
    p7 = jnp.where(first, 0.0, prev_ref[0][7:8, :] * prev_ref[1][7:8, :])
    p6 = jnp.where(first, 0.0, prev_ref[0][6:7, :] * prev_ref[1][6:7, :])
    s1 = jnp.where(row == 0, p7, pltpu.roll(cur, 1, 0))
    s2 = jnp.where(row == 0, p6, jnp.where(row == 1, p7, pltpu.roll(cur, 2, 0)))
    return s1, s2


def _group_ms(v, gmat):
    return _dot_exact(v, gmat, 2) * (1.0 / HEAD_DIM)


def _mixer_fwd(x, o_attn, gate_b, gate_c, u, conv_w, g_attn, g_conv, w_out, g_post, gmat, tm):
    s = x.shape[0]

    def body(x_ref, o_ref, b_ref, c_ref, u_ref, cp_ref, up_ref, cw_ref, ga_ref, gc_ref, wo_ref, gp_ref, gm_ref,
             x2_ref, mg_ref, y_ref, z_ref):
        i = pl.program_id(0)
        cu = c_ref[...] * u_ref[...]
        cu1, cu2 = _shift_down(cu, (cp_ref, up_ref), i == 0)
        z = cw_ref[0:1, :] * cu2 + cw_ref[1:2, :] * cu1 + cw_ref[2:3, :] * cu
        z_ref[...] = z
        cv = b_ref[...] * z
        ov = o_ref[...]
        gm = gm_ref[...]
        ma = ((ov * lax.rsqrt(_group_ms(ov * ov, gm) + EPS)) * ga_ref[...]).astype(MXU_DTYPE)
        mc = ((cv * lax.rsqrt(_group_ms(cv * cv, gm) + EPS)) * gc_ref[...]).astype(MXU_DTYPE)
        mg_ref[:, 0:ATTN_W] = ma
        mg_ref[:, ATTN_W:D_MODEL] = mc
        y = _dot(ma, wo_ref[0:ATTN_W, :]) + _dot(mc, wo_ref[ATTN_W:D_MODEL, :])
        y_ref[...] = y
        x2_ref[...] = x_ref[...] + (y * _rms(y)) * gp_ref[...]

    halo = pl.BlockSpec((8, 512), lambda i: (jnp.maximum(i * (tm // 8) - 1, 0), 0))
    sd = jax.ShapeDtypeStruct
    return pl.pallas_call(
        body, name="mixer_fwd", grid=(s // tm,),
        in_specs=[_tok(tm, D_MODEL), _tok(tm, 512), _tok(tm, 512), _tok(tm, 512), _tok(tm, 512), halo, halo,
                  _whole((3, 512)), _whole((1, 512)), _whole((1, 512)), _whole((D_MODEL, D_MODEL), single=True),
                  _whole((1, D_MODEL)), _whole((512, 512))],
        out_specs=[_tok(tm, D_MODEL), _tok(tm, D_MODEL), _tok(tm, D_MODEL), _tok(tm, 512)],
        out_shape=[sd((s, D_MODEL), F32), sd((s, D_MODEL), MXU_DTYPE), sd((s, D_MODEL), F32), sd((s, 512), F32)],
        compiler_params=_params(("arbitrary",)),
    )(x, o_attn, gate_b, gate_c, u, gate_c, u, conv_w, g_attn, g_conv, w_out, g_post, gmat)


def _ffn_fwd(x2, target, g_pre, w_gu, w_dn, g_post, tm):
    s = x2.shape[0]

    def body(x_ref, t_ref, gpre_ref, wgu_ref, wdn_ref, gpost_ref,
             h_ref, g_ref, up_ref, a_ref, ff_ref, dout_ref, loss_ref):
        xv = x_ref[...]
        h = ((xv * _rms(xv)) * gpre_ref[...]).astype(MXU_DTYPE)
        h_ref[...] = h
        ff = jnp.zeros((tm, D_MODEL), F32)
        for j in range(2):
            cols = slice(j * FF_PIECE, (j + 1) * FF_PIECE)
            g = _dot(h, wgu_ref[j])
            up = _dot(h, wgu_ref[2 + j])
            a = ((g * jax.nn.sigmoid(g)) * up).astype(MXU_DTYPE)
            g_ref[:, cols] = g
            up_ref[:, cols] = up
            a_ref[:, cols] = a
            ff = ff + _dot(a, wdn_ref[j])
        ff_ref[...] = ff
        err = (xv + (ff * _rms(ff)) * gpost_ref[...]) - t_ref[...]
        dout_ref[...] = err * (1.0 / D_MODEL)
        part = jnp.sum(jnp.mean(err * err, axis=-1, keepdims=True), axis=0, keepdims=True)

        @pl.when(pl.program_id(0) == 0)
        def _():
            loss_ref[...] = jnp.zeros_like(loss_ref)

        loss_ref[...] += part

    sd = jax.ShapeDtypeStruct
    return pl.pallas_call(
        body, name="ffn_fwd", grid=(s // tm,),
        in_specs=[_tok(tm, D_MODEL), _tok(tm, D_MODEL), _whole((1, D_MODEL)),
                  _whole((4, D_MODEL, FF_PIECE), single=True), _whole((2, FF_PIECE, D_MODEL), single=True),
                  _whole((1, D_MODEL))],
        out_specs=[_tok(tm, D_MODEL), _tok(tm, D_FF), _tok(tm, D_FF), _tok(tm, D_FF), _tok(tm, D_MODEL),
                   _tok(tm, D_MODEL), _whole((8, 128))],
        out_shape=[sd((s, D_MODEL), MXU_DTYPE), sd((s, D_FF), F32), sd((s, D_FF), F32), sd((s, D_FF), MXU_DTYPE),
                   sd((s, D_MODEL), F32), sd((s, D_MODEL), F32), sd((8, 128), F32)],
        compiler_params=_params(("arbitrary",)),
    )(x2, target, g_pre, w_gu, w_dn, g_post)


def _norm_bwd(dy, normed, rinv, gain):
    t = dy * gain
    return rinv * (t - normed * jnp.mean(t * normed, axis=-1, keepdims=True))


def _acc_rows(ref, first, val):
    @pl.when(first)
    def _():
        ref[...] = jnp.zeros_like(ref)

    ref[...] += jnp.sum(val, axis=0, keepdims=True)


def _ffn_bwd(dout, ff, x2, g, up, g_post, g_pre, w_gu, w_dn, tm):
    s = x2.shape[0]

    def body(do_ref, ff_ref, x_ref, g_ref, up_ref, gpost_ref, gpre_ref, wgu_ref, wdn_ref,
             dx_ref, dff_ref, dgu_ref, dgpost_ref, dgpre_ref):
        first = pl.program_id(0) == 0
        ffv = ff_ref[...]
        rf = _rms(ffv)
        n = ffv * rf
        do = do_ref[...]
        _acc_rows(dgpost_ref, first, do * n)
        dff = _norm_bwd(do, n, rf, gpost_ref[...]).astype(MXU_DTYPE)
        dff_ref[...] = dff
        dh = jnp.zeros((tm, D_MODEL), F32)
        for j in range(2):
            cols = slice(j * FF_PIECE, (j + 1) * FF_PIECE)
            da = _dot_nt(dff, wdn_ref[j])
            gv = g_ref[:, cols]
            sg = jax.nn.sigmoid(gv)
            dg = (da * up_ref[:, cols] * (sg * (1.0 + gv * (1.0 - sg)))).astype(MXU_DTYPE)
            du = (da * (gv * sg)).astype(MXU_DTYPE)
            dgu_ref[:, cols] = dg
            dgu_ref[:, D_FF + j * FF_PIECE:D_FF + (j + 1) * FF_PIECE] = du
            dh = dh + _dot_nt(dg, wgu_ref[j]) + _dot_nt(du, wgu_ref[2 + j])
        xv = x_ref[...]
        r2 = _rms(xv)
        nx = xv * r2
        _acc_rows(dgpre_ref, first, dh * nx)
        dx_ref[...] = do + _norm_bwd(dh, nx, r2, gpre_ref[...])

    sd = jax.ShapeDtypeStruct
    return pl.pallas_call(
        body, name="ffn_bwd", grid=(s // tm,),
        in_specs=[_tok(tm, D_MODEL), _tok(tm, D_MODEL), _tok(tm, D_MODEL), _tok(tm, D_FF), _tok(tm, D_FF),
                  _whole((1, D_MODEL)), _whole((1, D_MODEL)),
                  _whole((4, D_MODEL, FF_PIECE), single=True), _whole((2, FF_PIECE, D_MODEL), single=True)],
        out_specs=[_tok(tm, D_MODEL), _tok(tm, D_MODEL), _tok(tm, 2 * D_FF), _whole((1, D_MODEL)),
                   _whole((1, D_MODEL))],
        out_shape=[sd((s, D_MODEL), F32), sd((s, D_MODEL), MXU_DTYPE), sd((s, 2 * D_FF), MXU_DTYPE),
                   sd((1, D_MODEL), F32), sd((1, D_MODEL), F32)],
        compiler_params=_params(("arbitrary",)),
    )(dout, ff, x2, g, up, g_post, g_pre, w_gu, w_dn)


def _tn_matmul(a, b, tm, tn, tk, name):
    s, m = a.shape
    n = b.shape[1]

    def body(a_ref, b_ref, o_ref):
        @pl.when(pl.program_id(2) == 0)
        def _():
            o_ref[...] = jnp.zeros_like(o_ref)

        o_ref[...] += lax.dot_general(a_ref[...], b_ref[...], (((0,), (0,)), ((), ())), preferred_element_type=F32)

    return pl.pallas_call(
        body, name=name, grid=(m // tm, n // tn, s // tk),
        in_specs=[pl.BlockSpec((tk, tm), lambda i, j, kk: (kk, i)), pl.BlockSpec((tk, tn), lambda i, j, kk: (kk, j))],
        out_specs=pl.BlockSpec((tm, tn), lambda i, j, kk: (i, j)),
        out_shape=jax.ShapeDtypeStruct((m, n), F32),
        compiler_params=_params(("arbitrary", "arbitrary", "arbitrary")),
    )(a, b)


def _mixer_bwd(dx2, y, o_attn, gate_b, z, g_post, g_attn, g_conv, w_out, gmat, sel, tm):
    s = dx2.shape[0]

    def body(d_ref, y_ref, o_ref, b_ref, z_ref, gp_ref, ga_ref, gc_ref, wo_ref, gm_ref, sel_ref,
             dy_ref, do_ref, db_ref, dz_ref, delta_ref, dgp_ref, dga_ref, dgc_ref):
        first = pl.program_id(0) == 0
        yv = y_ref[...]
        ry = _rms(yv)
        ny = yv * ry
        d = d_ref[...]
        _acc_rows(dgp_ref, first, d * ny)
        dy = _norm_bwd(d, ny, ry, gp_ref[...]).astype(MXU_DTYPE)
        dy_ref[...] = dy
        dm = _dot_nt(dy, wo_ref[...])
        gm = gm_ref[...]

        def group_bwd(val, dmv, gain, dg_ref):
            rg = lax.rsqrt(_group_ms(val * val, gm) + EPS)
            nv = val * rg
            _acc_rows(dg_ref, first, dmv * nv)
            t = dmv * gain
            return rg * (t - nv * _group_ms(t * nv, gm))

        ov = o_ref[...]
        d_o = group_bwd(ov, dm[:, 0:ATTN_W], ga_ref[...], dga_ref)
        do_ref[...] = d_o
        delta_ref[...] = _dot_exact(d_o * ov, sel_ref[...], 2)
        zv = z_ref[...]
        bv = b_ref[...]
        d_cv = group_bwd(bv * zv, dm[:, ATTN_W:D_MODEL], gc_ref[...], dgc_ref)
        db_ref[...] = d_cv * zv
        dz_ref[...] = d_cv * bv

    sd = jax.ShapeDtypeStruct
    return pl.pallas_call(
        body, name="mixer_bwd", grid=(s // tm,),
        in_specs=[_tok(tm, D_MODEL), _tok(tm, D_MODEL), _tok(tm, 512), _tok(tm, 512), _tok(tm, 512),
                  _whole((1, D_MODEL)), _whole((1, 512)), _whole((1, 512)),
                  _whole((D_MODEL, D_MODEL), single=True), _whole((512, 512)), _whole((512, 128))],
        out_specs=[_tok(tm, D_MODEL), _tok(tm, 512), _tok(tm, 512), _tok(tm, 512), _tok(tm, 128),
                   _whole((1, D_MODEL)), _whole((1, 512)), _whole((1, 512))],
        out_shape=[sd((s, D_MODEL), MXU_DTYPE), sd((s, 512), F32), sd((s, 512), F32), sd((s, 512), F32),
                   sd((s, 128), F32), sd((1, D_MODEL), F32), sd((1, 512), F32), sd((1, 512), F32)],
        compiler_params=_params(("arbitrary",)),
    )(dx2, y, o_attn, gate_b, z, g_post, g_attn, g_conv, w_out, gmat, sel)


def _attn_bwd(qs, k, k_t, v, do, c_row, c_col, lse, delta, t):
    h, s, _ = qs.shape
    n = s // t
    pairs = [(i, j) for j in range(n) for i in range(j, n)]
    it = jnp.asarray(np.array([p[0] for p in pairs], np.int32))
    jt = jnp.asarray(np.array([p[1] for p in pairs], np.int32))

    def body(it_ref, jt_ref, q_ref, k_ref, kt_ref, v_ref, do_ref, cq_ref, ck_ref, lse_ref, dl_ref,
             dq_ref, dk_ref, dv_ref, dc_ref, dcq_ref, dk_sc, dv_sc, dc_sc):
        p = pl.program_id(0)
        i = it_ref[p]
        j = jt_ref[p]

        @pl.when(p == 0)
        def _():
            dq_ref[...] = jnp.zeros_like(dq_ref)
            dcq_ref[...] = jnp.zeros_like(dcq_ref)

        @pl.when(i == j)
        def _():
            dk_sc[...] = jnp.zeros_like(dk_sc)
            dv_sc[...] = jnp.zeros_like(dv_sc)
            dc_sc[...] = jnp.zeros_like(dc_sc)

        def step(hh, diagonal):
            qv = q_ref[hh]
            dov = do_ref[hh]
            st = _dot_nt(k_ref[hh], qv) + ((cq_ref[hh] - lse_ref[hh]) - ck_ref[hh])
            pt = jnp.exp(st)
            if diagonal:
                kpos = lax.broadcasted_iota(jnp.int32, (t, t), 0)
                qpos = lax.broadcasted_iota(jnp.int32, (t, t), 1)
                pt = jnp.where(kpos <= qpos, pt, 0.0)
            dv_sc[hh] += _dot(pt.astype(MXU_DTYPE), dov)
            dst = pt * (_dot_nt(v_ref[hh], dov) - dl_ref[hh])
            dc_sc[hh] -= jnp.sum(dst, axis=1, keepdims=True)
            dcq_ref[hh, i] += jnp.sum(dst, axis=0, keepdims=True)
            dsb = dst.astype(MXU_DTYPE)
            dk_sc[hh] += _dot(dsb, qv)
            dq_ref[hh, i] += _dot(kt_ref[hh], dsb)

        @pl.when(i > j)
        def _():
            @pl.loop(0, h)
            def _(hh):
                step(hh, False)

        @pl.when(i == j)
        def _():
            @pl.loop(0, h)
            def _(hh):
                step(hh, True)

        @pl.when(i == n - 1)
        def _():
            dk_ref[...] = dk_sc[...]
            dv_ref[...] = dv_sc[...]
            dc_ref[...] = dc_sc[...]

    qi = lambda p, it_, jt_: (0, it_[p], 0)
    kj = lambda p, it_, jt_: (0, jt_[p], 0)
    row_i = lambda p, it_, jt_: (0, 0, it_[p])
    gs = pltpu.PrefetchScalarGridSpec(
        num_scalar_prefetch=2, grid=(len(pairs),),
        in_specs=[pl.BlockSpec((h, t, HEAD_DIM), qi), pl.BlockSpec((h, t, HEAD_DIM), kj),
                  pl.BlockSpec((h, HEAD_DIM, t), lambda p, it_, jt_: (0, 0, jt_[p])),
                  pl.BlockSpec((h, t, HEAD_DIM), kj), pl.BlockSpec((h, t, HEAD_DIM), qi),
                  pl.BlockSpec((h, 1, t), row_i), pl.BlockSpec((h, t, 1), kj),
                  pl.BlockSpec((h, 1, t), row_i), pl.BlockSpec((h, 1, t), row_i)],
        out_specs=[pl.BlockSpec((h, n, HEAD_DIM, t), lambda p, it_, jt_: (0, 0, 0, 0)),
                   pl.BlockSpec((h, t, HEAD_DIM), kj), pl.BlockSpec((h, t, HEAD_DIM), kj),
                   pl.BlockSpec((h, t, 1), kj),
                   pl.BlockSpec((h, n, 1, t), lambda p, it_, jt_: (0, 0, 0, 0))],
        scratch_shapes=[pltpu.VMEM((h, t, HEAD_DIM), F32), pltpu.VMEM((h, t, HEAD_DIM), F32),
                        pltpu.VMEM((h, t, 1), F32)])
    sd = jax.ShapeDtypeStruct
    return pl.pallas_call(
        body, name="attn_bwd", grid_spec=gs,
        out_shape=[sd((h, n, HEAD_DIM, t), F32), sd((h, s, HEAD_DIM), F32), sd((h, s, HEAD_DIM), F32),
                   sd((h, s, 1), F32), sd((h, n, 1, t), F32)],
        compiler_params=_params(("arbitrary",)),
    )(it, jt, qs, k, k_t, v, do, c_row, c_col, lse, delta)


def _forget_bwd(dc_t, z_t, b_col):
    rows, s = z_t.shape
    nb = s // 128

    def body(dc_ref, z_ref, b_ref, dz_ref, db_ref):
        lower = _tri(128, False)

        def blk(m, carry):
            tail, dbias = carry
            off = pl.multiple_of((nb - 1 - m) * 128, 128)
            dc = dc_ref[:, pl.ds(off, 128)]
            dlf = _dot_exact(dc, lower, 3) + tail
            dz = dlf * jax.nn.sigmoid(-(z_ref[:, pl.ds(off, 128)] + b_ref[...]))
            dz_ref[:, pl.ds(off, 128)] = dz
            return tail + jnp.sum(dc, axis=1, keepdims=True), dbias + jnp.sum(dz, axis=1, keepdims=True)

        zero = jnp.zeros((rows, 1), F32)
        _, dbias = lax.fori_loop(0, nb, blk, (zero, zero))
        db_ref[...] = jnp.broadcast_to(dbias, db_ref.shape)

    return pl.pallas_call(
        body, name="forget_bwd",
        out_shape=[jax.ShapeDtypeStruct((rows, s), F32), jax.ShapeDtypeStruct((rows, 128), F32)],
        compiler_params=_params())(dc_t, z_t, b_col)


def _inproj_bwd(dz, gate_c, u, conv_w, dq, dk, dv, dzf, db, x, dx2, g_pre, w_pad, tm):
    s = x.shape[0]
    nt = s // tm

    def body(dz_ref, dzn_ref, c_ref, u_ref, cp_ref, up_ref, cw_ref, dq_ref, dk_ref, dv_ref, dzf_ref, db_ref,
             x_ref, dx2_ref, g_ref, w_ref, gx_ref, dp_ref, dg_ref, dcw_ref):
        i = pl.program_id(0)
        first = i == 0
        last = i == nt - 1
        dzv = dz_ref[...]
        row = lax.broadcasted_iota(jnp.int32, dzv.shape, 0)
        n0 = jnp.where(last, 0.0, dzn_ref[0:1, :])
        n1 = jnp.where(last, 0.0, dzn_ref[1:2, :])
        dz1 = jnp.where(row == tm - 1, n0, pltpu.roll(dzv, tm - 1, 0))
        dz2 = jnp.where(row == tm - 1, n1, jnp.where(row == tm - 2, n0, pltpu.roll(dzv, tm - 2, 0)))
        dcu = cw_ref[2:3, :] * dzv + cw_ref[1:2, :] * dz1 + cw_ref[0:1, :] * dz2
        cv = c_ref[...]
        uv = u_ref[...]
        cu = cv * uv
        cu1, cu2 = _shift_down(cu, (cp_ref, up_ref), first)

        @pl.when(first)
        def _():
            dcw_ref[...] = jnp.zeros_like(dcw_ref)

        dcw_ref[0:1, :] += jnp.sum(dzv * cu2, axis=0, keepdims=True)
        dcw_ref[1:2, :] += jnp.sum(dzv * cu1, axis=0, keepdims=True)
        dcw_ref[2:3, :] += jnp.sum(dzv * cu, axis=0, keepdims=True)

        dp_ref[:, 0:512] = (dq_ref[...] * Q_SCALE).astype(MXU_DTYPE)
        dp_ref[:, 512:1024] = dk_ref[...].astype(MXU_DTYPE)
        dp_ref[:, 1024:OFF_F] = dv_ref[...].astype(MXU_DTYPE)
        dp_ref[:, OFF_F:OFF_B] = dzf_ref[...].astype(MXU_DTYPE)
        dp_ref[:, OFF_B:OFF_C] = db_ref[...].astype(MXU_DTYPE)
        dp_ref[:, OFF_C:OFF_U] = (dcu * uv).astype(MXU_DTYPE)
        dp_ref[:, OFF_U:IN_PAD] = (dcu * cv).astype(MXU_DTYPE)
        dh = _dot_nt(dp_ref[...], w_ref[...])
        xv = x_ref[...]
        r1 = _rms(xv)
        nx = xv * r1
        _acc_rows(dg_ref, first, dh * nx)
        gx_ref[...] = dx2_ref[...] + _norm_bwd(dh, nx, r1, g_ref[...])

    prev = pl.BlockSpec((8, 512), lambda i: (jnp.maximum(i * (tm // 8) - 1, 0), 0))
    nxt = pl.BlockSpec((8, 512), lambda i: (jnp.minimum((i + 1) * (tm // 8), s // 8 - 1), 0))
    sd = jax.ShapeDtypeStruct
    return pl.pallas_call(
        body, name="inproj_bwd", grid=(nt,),
        in_specs=[_tok(tm, 512), nxt, _tok(tm, 512), _tok(tm, 512), prev, prev, _whole((3, 512)),
                  _tok(tm, 512), _tok(tm, 512), _tok(tm, 512), _tok(tm, 128), _tok(tm, 512),
                  _tok(tm, D_MODEL), _tok(tm, D_MODEL), _whole((1, D_MODEL)), _whole((D_MODEL, IN_PAD), single=True)],
        out_specs=[_tok(tm, D_MODEL), _tok(tm, IN_PAD), _whole((1, D_MODEL)), _whole((8, 512))],
        out_shape=[sd((s, D_MODEL), F32), sd((s, IN_PAD), MXU_DTYPE), sd((1, D_MODEL), F32), sd((8, 512), F32)],
        compiler_params=_params(("arbitrary",)),
    )(dz, dz, gate_c, u, gate_c, u, conv_w, dq, dk, dv, dzf, db, x, dx2, g_pre, w_pad)


def _heads(a):
    s = a.shape[0]
    return jnp.transpose(a.reshape(s, N_HEADS, HEAD_DIM), (1, 0, 2))


def _heads_t(a):
    s = a.shape[0]
    return jnp.transpose(a.reshape(s, N_HEADS, HEAD_DIM), (1, 2, 0))


def _unheads(a):
    s = a.shape[1]
    return jnp.transpose(a, (1, 0, 2)).reshape(s, N_HEADS * HEAD_DIM)


def _tile(s, want):
    return want if s % want == 0 else s


def _local_step(x, target, w_pad, b_forget, conv_w, g_attn, g_conv, w_out, g_mix_pre, g_mix_post,
                w_gu, w_dn, g_ffn_pre, g_ffn_post):
    s = x.shape[0]
    tm = _tile(s, 512)
    tf = _tile(s, 256)
    ta = _tile(s, 512)
    rows = 16
    gidx = np.arange(512) // HEAD_DIM
    gmat = jnp.asarray(gidx[:, None] == gidx[None, :], MXU_DTYPE)
    sel = jnp.asarray(gidx[:, None] == np.arange(128)[None, :], MXU_DTYPE)

    h1, qs, k, v, f, gate_b, gate_c, u = _inproj_fwd(x, g_mix_pre, w_pad, tm)
    z_t = jnp.pad(jnp.transpose(f[:, :N_HEADS]), ((0, rows - N_HEADS), (0, 0)))
    b_col = jnp.pad(jnp.transpose(b_forget), ((0, rows - N_HEADS), (0, 0)))
    c_t = _forget_fwd(z_t, b_col)
    c_row = c_t[:N_HEADS].reshape(N_HEADS, 1, s)
    c_col = c_t[:N_HEADS].reshape(N_HEADS, s, 1)
    qh, kh, vh = _heads(qs), _heads(k), _heads(v)
    o_t, lse = _attn_fwd(qh, kh, _heads_t(v), c_row, c_col, ta)
    o_attn = jnp.transpose(o_t, (2, 0, 1)).reshape(s, ATTN_W)
    x2, merged, y, z = _mixer_fwd(x, o_attn, gate_b, gate_c, u, conv_w, g_attn, g_conv, w_out, g_mix_post, gmat, tm)
    h2, g, up, a, ff, dout, loss_acc = _ffn_fwd(x2, target, g_ffn_pre, w_gu, w_dn, g_ffn_post, tf)

    dx2, dff, dgu, dg_ffn_post, dg_ffn_pre = _ffn_bwd(dout, ff, x2, g, up, g_ffn_post, g_ffn_pre, w_gu, w_dn, tf)
    tkk = _tile(s, 2048)
    dw_dn = _tn_matmul(a, dff, FF_PIECE, 512, tkk, "dw_down")
    dw_gu = _tn_matmul(h2, dgu, 1024, 512, tkk, "dw_gate_up")
    dy, d_o, d_b, dz, delta8, dg_mix_post, dg_attn, dg_conv = _mixer_bwd(
        dx2, y, o_attn, gate_b, z, g_mix_post, g_attn, g_conv, w_out, gmat, sel, tm)
    dw_out = _tn_matmul(merged, dy, 1024, 512, tkk, "dw_out")
    delta = jnp.transpose(delta8[:, :N_HEADS]).reshape(N_HEADS, 1, s)
    dq4, dkh, dvh, dc, dcq = _attn_bwd(qh, kh, _heads_t(k), vh, _heads(d_o.astype(MXU_DTYPE)), c_row, c_col, lse, delta, ta)
    dc_t = jnp.pad(dc.reshape(N_HEADS, s) + dcq.reshape(N_HEADS, s), ((0, rows - N_HEADS), (0, 0)))
    dz_t, db_f = _forget_bwd(dc_t, z_t, b_col)
    dzf = jnp.pad(jnp.transpose(dz_t[:N_HEADS]), ((0, 0), (0, 128 - N_HEADS)))
    dq = jnp.transpose(dq4, (1, 3, 0, 2)).reshape(s, ATTN_W)
    grad_x, dproj, dg_mix_pre, dcw = _inproj_bwd(dz, gate_c, u, conv_w, dq, _unheads(dkh), _unheads(dvh), dzf, d_b,
                                                 x, dx2, g_mix_pre, w_pad, tm)
    dw_pad = _tn_matmul(h1, dproj, 1024, 640, tkk, "dw_in")
    dw_in = jnp.concatenate([dw_pad[:, :OFF_F + N_HEADS], dw_pad[:, OFF_B:]], axis=1)
    grads = dict(w_in=dw_in, b_forget=db_f[:N_HEADS, 0].reshape(1, N_HEADS), conv_w=dcw[:3], g_attn_out=dg_attn,
                 g_conv_out=dg_conv, w_out=dw_out, g_mix_pre=dg_mix_pre, g_mix_post=dg_mix_post,
                 w_gate_up=dw_gu, w_down=dw_dn, g_ffn_pre=dg_ffn_pre, g_ffn_post=dg_ffn_post)
    return loss_acc[0, 0], grad_x, grads


BIG = ("w_in", "w_out", "w_gate_up", "w_down")
ANY = pl.BlockSpec(memory_space=pl.ANY)


def _place():
    x, y, c = lax.axis_index("x"), lax.axis_index("y"), lax.axis_index("c")
    others = [(1 - x, y), (x, 1 - y), (1 - x, 1 - y)]
    return x, y, c, 2 * x + y, others, [2 * px + py for px, py in others]


def _remote(src, dst, send, recv, dev):
    return pltpu.make_async_remote_copy(src_ref=src, dst_ref=dst, send_sem=send, recv_sem=recv,
                                        device_id=dev, device_id_type=MESH_ID)


def _gather_weights(shards, conv_w):
    n = len(shards)

    def body(*refs):
        sh, cw, outs, cwo = refs[:n], refs[n], refs[n + 1:2 * n + 1], refs[2 * n + 1]
        send, recv = refs[2 * n + 2:]
        x, y, c, me, others, chips = _place()
        sib = (x, y, 1 - c)
        sends = []
        for w in range(n):
            for kk, (px, py) in enumerate(others):
                sends.append(_remote(sh[w].at[c], outs[w].at[me, c], send.at[w, kk], recv.at[w, kk], (px, py, c)))
        for kk, (px, py) in enumerate(others):
            sends.append(_remote(cw, cwo.at[me], send.at[n, kk], recv.at[n, kk], (px, py, c)))
        for cp in sends:
            cp.start()
        for w in range(n):
            for kk, (px, py) in enumerate(others):
                landed = outs[w].at[chips[kk], c]
                _remote(landed, landed, send.at[w, kk], recv.at[w, kk], (px, py, c)).wait_recv()
                fwd = _remote(landed, landed, send.at[w, 3 + kk], recv.at[w, 3 + kk], sib)
                fwd.start()
                sends.append(fwd)
        for kk, (px, py) in enumerate(others):
            _remote(cw, cwo.at[chips[kk]], send.at[n, kk], recv.at[n, kk], (px, py, c)).wait_recv()
        for w in range(n):
            for kk in range(3):
                passed = outs[w].at[chips[kk], 1 - c]
                _remote(passed, passed, send.at[w, 3 + kk], recv.at[w, 3 + kk], sib).wait_recv()
        for cp in sends:
            cp.wait_send()

    out_shape = [jax.ShapeDtypeStruct((N_CHIPS,) + s.shape, s.dtype) for s in shards]
    out_shape.append(jax.ShapeDtypeStruct((N_CHIPS,) + conv_w.shape, conv_w.dtype))
    got = pl.pallas_call(
        body, name="gather_weights", in_specs=[ANY] * (n + 1), out_specs=[ANY] * (n + 1), out_shape=out_shape,
        scratch_shapes=[pltpu.SemaphoreType.DMA((n + 1, 6)), pltpu.SemaphoreType.DMA((n + 1, 6))],
    )(*shards, conv_w)
    me = 2 * lax.axis_index("x") + lax.axis_index("y")
    return [lax.dynamic_update_index_in_dim(g, own, me, 0) for g, own in zip(got, list(shards) + [conv_w])]


def _pair_exchange(grads):
    n = len(grads)

    def body(*refs):
        g, a = refs[:n], refs[n:2 * n]
        send, recv = refs[2 * n:]
        x, y, c, _, _, _ = _place()
        sib = (x, y, 1 - c)
        copies = [_remote(g[w].at[p, 1 - c], a[w].at[p], send.at[w, p], recv.at[w, p], sib)
                  for w in range(n) for p in range(N_CHIPS)]
        for cp in copies:
            cp.start()
        for cp in copies:
            cp.wait()

    return pl.pallas_call(
        body, name="pair_exchange", in_specs=[ANY] * n, out_specs=[ANY] * n,
        out_shape=[jax.ShapeDtypeStruct((N_CHIPS,) + g.shape[2:], g.dtype) for g in grads],
        scratch_shapes=[pltpu.SemaphoreType.DMA((n, N_CHIPS)), pltpu.SemaphoreType.DMA((n, N_CHIPS))],
    )(*grads)


def _pair_sum(c_idx, g, a, name):
    _, _, half, cols = g.shape

    def body(c_ref, g_ref, a_ref, pf_ref, pb_ref):
        tot = g_ref[0, 0] + a_ref[0]
        pf_ref[0] = tot
        pb_ref[0] = tot.astype(BF16)

    gs = pltpu.PrefetchScalarGridSpec(
        num_scalar_prefetch=1, grid=(N_CHIPS,),
        in_specs=[pl.BlockSpec((1, 1, half, cols), lambda p, cr: (p, cr[0], 0, 0)),
                  pl.BlockSpec((1, half, cols), lambda p, cr: (p, 0, 0))],
        out_specs=[pl.BlockSpec((1, half, cols), lambda p, cr: (p, 0, 0)),
                   pl.BlockSpec((1, half, cols), lambda p, cr: (p, 0, 0))])
    return pl.pallas_call(
        body, name=name, grid_spec=gs,
        out_shape=[jax.ShapeDtypeStruct((N_CHIPS, half, cols), F32), jax.ShapeDtypeStruct((N_CHIPS, half, cols), BF16)],
        compiler_params=_params(("arbitrary",)),
    )(c_idx, g, a)


def _chip_exchange(parts, small):
    n = len(parts)

    def body(*refs):
        pb, sm, rcv, smg = refs[:n], refs[n], refs[n + 1:2 * n + 1], refs[2 * n + 1]
        send, recv, ssend, srecv, loc = refs[2 * n + 2:]
        x, y, c, _, others, chips = _place()
        mine = 4 * x + 2 * y + c
        own = pltpu.make_async_copy(sm, smg.at[mine], loc)
        own.start()
        copies = [_remote(pb[w].at[chips[kk]], rcv[w].at[kk], send.at[w, kk], recv.at[w, kk], (px, py, c))
                  for w in range(n) for kk, (px, py) in enumerate(others)]
        for r in range(1, 8):
            peer = (1 - x if r & 4 else x, 1 - y if r & 2 else y, 1 - c if r & 1 else c)
            copies.append(_remote(sm, smg.at[mine], ssend.at[r - 1], srecv.at[r - 1], peer))
        for cp in copies:
            cp.start()
        for w in range(n):
            for kk, (px, py) in enumerate(others):
                _remote(pb[w].at[chips[kk]], rcv[w].at[kk], send.at[w, kk], recv.at[w, kk], (px, py, c)).wait_recv()
        for r in range(1, 8):
            px, py, pc = (1 - x if r & 4 else x, 1 - y if r & 2 else y, 1 - c if r & 1 else c)
            _remote(sm, smg.at[4 * px + 2 * py + pc], ssend.at[r - 1], srecv.at[r - 1], (px, py, pc)).wait_recv()
        for cp in copies:
            cp.wait_send()
        own.wait()

    out_shape = [jax.ShapeDtypeStruct((3,) + p.shape[1:], p.dtype) for p in parts]
    out_shape.append(jax.ShapeDtypeStruct((8,) + small.shape, small.dtype))
    return pl.pallas_call(
        body, name="chip_exchange", in_specs=[ANY] * (n + 1), out_specs=[ANY] * (n + 1), out_shape=out_shape,
        scratch_shapes=[pltpu.SemaphoreType.DMA((n, 3)), pltpu.SemaphoreType.DMA((n, 3)),
                        pltpu.SemaphoreType.DMA((7,)), pltpu.SemaphoreType.DMA((7,)), pltpu.SemaphoreType.DMA(())],
    )(*parts, small)


def _chip_sum(me_idx, pf, rcv, name):
    _, half, cols = pf.shape

    def body(me_ref, pf_ref, r_ref, t_ref):
        t_ref[...] = ((pf_ref[0] + r_ref[0].astype(F32)) + r_ref[1].astype(F32)) + r_ref[2].astype(F32)

    gs = pltpu.PrefetchScalarGridSpec(
        num_scalar_prefetch=1, grid=(1,),
        in_specs=[pl.BlockSpec((1, half, cols), lambda i, mr: (mr[0], 0, 0)),
                  pl.BlockSpec((3, half, cols), lambda i, mr: (0, 0, 0))],
        out_specs=pl.BlockSpec((half, cols), lambda i, mr: (0, 0)))
    return pl.pallas_call(
        body, name=name, grid_spec=gs, out_shape=jax.ShapeDtypeStruct((half, cols), F32),
        compiler_params=_params(("arbitrary",)),
    )(me_idx, pf, rcv)


def _pair_share(totals):
    n = len(totals)

    def body(*refs):
        t, g = refs[:n], refs[n:2 * n]
        send, recv = refs[2 * n:]
        x, y, c, _, _, _ = _place()
        copies = [_remote(t[w], g[w], send.at[w], recv.at[w], (x, y, 1 - c)) for w in range(n)]
        for cp in copies:
            cp.start()
        for cp in copies:
            cp.wait()

    return pl.pallas_call(
        body, name="pair_share", in_specs=[ANY] * n, out_specs=[ANY] * n,
        out_shape=[jax.ShapeDtypeStruct(t.shape, t.dtype) for t in totals],
        scratch_shapes=[pltpu.SemaphoreType.DMA((n,)), pltpu.SemaphoreType.DMA((n,))],
    )(*totals)


def _adamw_math(w, g, m, v):
    m = ADAM_B1 * m + (1.0 - ADAM_B1) * g
    v = ADAM_B2 * v + (1.0 - ADAM_B2) * (g * g)
    m_hat = m / (1.0 - ADAM_B1 ** ADAM_STEP)
    v_hat = v / (1.0 - ADAM_B2 ** ADAM_STEP)
    delta = -ADAM_LR * (m_hat / (jnp.sqrt(v_hat) + ADAM_EPS) + ADAM_WD * w)
    return delta, m, v


def _adamw(c_idx, w, mine, theirs, m, v, nb, name):
    rows, cols = w.shape
    tr = rows // (2 * nb)

    def body(c_ref, w_ref, a_ref, b_ref, m_ref, v_ref, g_ref, d_ref, nm_ref, nv_ref):
        g = jnp.where(pl.program_id(0) == c_ref[0], a_ref[...], b_ref[...])
        g_ref[...] = g
        d_ref[...], nm_ref[...], nv_ref[...] = _adamw_math(w_ref[...], g, m_ref[...], v_ref[...])

    full = pl.BlockSpec((tr, cols), lambda hh, i, cr: (hh * nb + i, 0))
    half = pl.BlockSpec((tr, cols), lambda hh, i, cr: (i, 0))
    gs = pltpu.PrefetchScalarGridSpec(num_scalar_prefetch=1, grid=(2, nb), in_specs=[full, half, half, full, full],
                                      out_specs=[full] * 4)
    return pl.pallas_call(
        body, name=name, grid_spec=gs, out_shape=[jax.ShapeDtypeStruct((rows, cols), F32)] * 4,
        compiler_params=_params(("arbitrary", "arbitrary")),
    )(c_idx, w, mine, theirs, m, v)


def _small_update(gathered, w, m, v):
    def body(gg_ref, w_ref, m_ref, v_ref, g_ref, d_ref, nm_ref, nv_ref):
        g = gg_ref[0]
        for dev in range(1, 8):
            g = g + gg_ref[dev]
        g_ref[...] = g
        d_ref[...], nm_ref[...], nv_ref[...] = _adamw_math(w_ref[...], g, m_ref[...], v_ref[...])

    return pl.pallas_call(body, name="small_update", out_shape=[jax.ShapeDtypeStruct(w.shape, F32)] * 4,
                          compiler_params=_params())(gathered, w, m, v)


SMALL = ("g_mix_pre", "g_mix_post", "g_ffn_pre", "g_ffn_post")


def _pack_small(t, conv_full):
    conv = jnp.pad(conv_full.reshape(1, 3 * CONV_W), ((0, 0), (0, 2048 - 3 * CONV_W))).reshape(2, 1024)
    return jnp.concatenate([t[n].reshape(1, 1024) for n in SMALL]
                           + [jnp.concatenate([t["g_attn_out"].reshape(1, 512), t["g_conv_out"].reshape(1, 512)], axis=1),
                              conv, jnp.pad(t["b_forget"].reshape(1, N_HEADS), ((0, 0), (0, 1024 - N_HEADS)))], axis=0)


def _unpack_small(p, me):
    out = {n: p[i].reshape(1, 1024) for i, n in enumerate(SMALL)}
    out["g_attn_out"] = p[4, :512].reshape(1, 512)
    out["g_conv_out"] = p[4, 512:].reshape(1, 512)
    conv = p[5:7].reshape(2048)[:3 * CONV_W].reshape(3, CONV_W)
    out["conv_w"] = lax.dynamic_slice_in_dim(conv, me * 128, 128, axis=1).reshape(1, 3, 128)
    out["b_forget"] = p[7, :N_HEADS].reshape(1, N_HEADS)
    return out


def _conv_in_place(shard, me):
    return lax.dynamic_update_slice_in_dim(jnp.zeros((3, CONV_W), F32), shard, me * 128, axis=1)


def kernel(x, w_in, b_forget, conv_w, g_attn_out, g_conv_out, w_out, g_mix_pre, g_mix_post, w_gate_up, w_down, g_ffn_pre, g_ffn_post, loss_target, m_w_in, m_b_forget, m_conv_w, m_g_attn_out, m_g_conv_out, m_w_out, m_g_mix_pre, m_g_mix_post, m_w_gate_up, m_w_down, m_g_ffn_pre, m_g_ffn_post, v_w_in, v_b_forget, v_conv_w, v_g_attn_out, v_g_conv_out, v_w_out, v_g_mix_pre, v_g_mix_post, v_w_gate_up, v_w_down, v_g_ffn_pre, v_g_ffn_post):
    w = dict(w_in=w_in, b_forget=b_forget, conv_w=conv_w, g_attn_out=g_attn_out, g_conv_out=g_conv_out, w_out=w_out,
             g_mix_pre=g_mix_pre, g_mix_post=g_mix_post, w_gate_up=w_gate_up, w_down=w_down, g_ffn_pre=g_ffn_pre,
             g_ffn_post=g_ffn_post)
    m = dict(w_in=m_w_in, b_forget=m_b_forget, conv_w=m_conv_w, g_attn_out=m_g_attn_out, g_conv_out=m_g_conv_out,
             w_out=m_w_out, g_mix_pre=m_g_mix_pre, g_mix_post=m_g_mix_post, w_gate_up=m_w_gate_up, w_down=m_w_down,
             g_ffn_pre=m_g_ffn_pre, g_ffn_post=m_g_ffn_post)
    v = dict(w_in=v_w_in, b_forget=v_b_forget, conv_w=v_conv_w, g_attn_out=v_g_attn_out, g_conv_out=v_g_conv_out,
             w_out=v_w_out, g_mix_pre=v_g_mix_pre, g_mix_post=v_g_mix_post, w_gate_up=v_w_gate_up, w_down=v_w_down,
             g_ffn_pre=v_g_ffn_pre, g_ffn_post=v_g_ffn_post)
    cx, cy, cc = lax.axis_index("x"), lax.axis_index("y"), lax.axis_index("c")
    me = 2 * cx + cy
    c_idx = cc.astype(jnp.int32).reshape(1)
    me_idx = me.astype(jnp.int32).reshape(1)
    s = x.shape[1]

    shards = []
    for n in BIG:
        a = w[n][0].astype(MXU_DTYPE)
        shards.append(a.reshape(2, a.shape[0] // 2, a.shape[1]))
    g_in, g_out, g_gu, g_dn, conv_all = _gather_weights(shards, conv_w[0])
    w_in_full = jnp.transpose(g_in.reshape(N_CHIPS, D_MODEL, IN_W // N_CHIPS), (1, 0, 2)).reshape(D_MODEL, IN_W)
    w_pad = jnp.concatenate([w_in_full[:, :OFF_F + N_HEADS], jnp.zeros((D_MODEL, OFF_B - OFF_F - N_HEADS), MXU_DTYPE),
                             w_in_full[:, OFF_F + N_HEADS:]], axis=1)
    conv_full = jnp.transpose(conv_all, (1, 0, 2)).reshape(3, CONV_W)

    loss_local, grad_x, grads = _local_step(
        x[0], loss_target[0], w_pad, b_forget, conv_full, g_attn_out, g_conv_out, g_out.reshape(D_MODEL, D_MODEL),
        g_mix_pre, g_mix_post, g_gu.reshape(N_CHIPS, D_MODEL, FF_PIECE), g_dn.reshape(2, FF_PIECE, D_MODEL),
        g_ffn_pre, g_ffn_post)
    loss = 0.5 * lax.psum(loss_local, ("x", "y", "c"))

    pieces = dict(
        w_in=jnp.transpose(grads["w_in"].reshape(D_MODEL, N_CHIPS, IN_W // N_CHIPS), (1, 0, 2)),
        w_out=grads["w_out"].reshape(N_CHIPS, D_MODEL // N_CHIPS, D_MODEL),
        w_gate_up=jnp.transpose(grads["w_gate_up"].reshape(D_MODEL, N_CHIPS, FF_PIECE), (1, 0, 2)),
        w_down=grads["w_down"].reshape(N_CHIPS, D_FF // N_CHIPS, D_MODEL))
    split = [pieces[n].reshape(N_CHIPS, 2, pieces[n].shape[1] // 2, pieces[n].shape[2]) for n in BIG]
    from_sibling = _pair_exchange(split)
    sums = [_pair_sum(c_idx, g, a, "pair_sum_" + n) for n, g, a in zip(BIG, split, from_sibling)]
    small_local = _pack_small(grads, grads["conv_w"])
    *arrived, small_all = _chip_exchange([sb[1] for sb in sums], small_local)
    totals = [_chip_sum(me_idx, sb[0], r, "chip_sum_" + n) for n, sb, r in zip(BIG, sums, arrived)]
    shared = _pair_share(totals)

    gsum, delta, new_m, new_v = {}, {}, {}, {}
    for n, mine, theirs in zip(BIG, totals, shared):
        gs, d, nm, nv = _adamw(c_idx, w[n][0], mine, theirs, m[n][0], v[n][0], 2, "adamw_" + n)
        gsum[n], delta[n], new_m[n], new_v[n] = gs[None], d[None], nm[None], nv[None]
    packed = [_pack_small(t, _conv_in_place(t["conv_w"][0], me)) for t in (w, m, v)]
    g_small, d_small, m_small, v_small = _small_update(small_all, *packed)
    gsum.update(_unpack_small(g_small, me))
    delta.update(_unpack_small(d_small, me))
    new_m.update(_unpack_small(m_small, me))
    new_v.update(_unpack_small(v_small, me))

    order = ("w_in", "b_forget", "conv_w", "g_attn_out", "g_conv_out", "w_out", "g_mix_pre", "g_mix_post",
             "w_gate_up", "w_down", "g_ffn_pre", "g_ffn_post")
    return (loss, grad_x[None], *[gsum[n] for n in order], *[delta[n] for n in order],
            *[new_m[n] for n in order], *[new_v[n] for n in order])
```

```python
import functools

import jax
import jax.numpy as jnp
import numpy as np
from jax import lax
from jax.experimental import pallas as pl
from jax.experimental.pallas import tpu as pltpu

F32 = jnp.float32
BF16 = jnp.bfloat16
MXU_DTYPE = jnp.bfloat16

D_MODEL = 1024
HEAD_DIM = 64
N_HEADS = 8
ATTN_W = 512
CONV_W = 512
D_FF = 2816
FF_PIECE = 1408
EPS = 1e-6
Q_SCALE = HEAD_DIM ** -0.5

OFF_F = 1536
OFF_B = 1664
OFF_C = 2176
OFF_U = 2688
IN_PAD = 3200
IN_W = 3080
N_CHIPS = 4

ADAM_LR = 0.001
ADAM_B1 = 0.9
ADAM_B2 = 0.999
ADAM_EPS = 1e-08
ADAM_WD = 0.01
ADAM_STEP = 10

VMEM_LIMIT_V7X = 56 * 1024 * 1024
MESH_ID = pl.DeviceIdType.MESH


def _params(sem=None, vmem=VMEM_LIMIT_V7X):
    kw = {"vmem_limit_bytes": vmem}
    if sem is not None:
        kw["dimension_semantics"] = sem
    return pltpu.CompilerParams(**kw)


def _dot(a, b):
    return jnp.dot(a, b, preferred_element_type=F32)


def _dot_nt(a, b):
    return lax.dot_general(a, b, (((1,), (1,)), ((), ())), preferred_element_type=F32)


def _dot_exact(x, ones, parts):
    if ones.dtype == F32:
        return _dot(x, ones)
    acc = None
    rem = x
    for _ in range(parts):
        piece = rem.astype(BF16)
        rem = rem - piece.astype(F32)
        term = _dot(piece, ones)
        acc = term if acc is None else acc + term
    return acc


def _rms(v):
    return lax.rsqrt(jnp.mean(v * v, axis=-1, keepdims=True) + EPS)


def _tok(tm, w):
    return pl.BlockSpec((tm, w), lambda i: (i, 0))


def _whole(shape, single=False):
    nd = len(shape)
    if single:
        return pl.BlockSpec(shape, lambda i: (0,) * nd, pipeline_mode=pl.Buffered(1))
    return pl.BlockSpec(shape, lambda i: (0,) * nd)


def _inproj_fwd(x, g_pre, w_pad, tm):
    s = x.shape[0]

    def body(x_ref, g_ref, w_ref, h_ref, q_ref, k_ref, v_ref, f_ref, b_ref, c_ref, u_ref):
        xv = x_ref[...]
        h = ((xv * _rms(xv)) * g_ref[...]).astype(MXU_DTYPE)
        h_ref[...] = h

        def proj(lo, hi):
            return _dot(h, w_ref[:, lo:hi])

        q_ref[...] = (proj(0, 512) * Q_SCALE).astype(MXU_DTYPE)
        k_ref[...] = proj(512, 1024).astype(MXU_DTYPE)
        v_ref[...] = proj(1024, OFF_F).astype(MXU_DTYPE)
        f_ref[...] = proj(OFF_F, OFF_B)
        b_ref[...] = proj(OFF_B, OFF_C)
        c_ref[...] = proj(OFF_C, OFF_U)
        u_ref[...] = proj(OFF_U, IN_PAD)

    sd = jax.ShapeDtypeStruct
    return pl.pallas_call(
        body, name="inproj_fwd", grid=(s // tm,),
        in_specs=[_tok(tm, D_MODEL), _whole((1, D_MODEL)), _whole((D_MODEL, IN_PAD), single=True)],
        out_specs=[_tok(tm, D_MODEL), _tok(tm, 512), _tok(tm, 512), _tok(tm, 512), _tok(tm, 128),
                   _tok(tm, 512), _tok(tm, 512), _tok(tm, 512)],
        out_shape=[sd((s, D_MODEL), MXU_DTYPE), sd((s, 512), MXU_DTYPE), sd((s, 512), MXU_DTYPE),
                   sd((s, 512), MXU_DTYPE), sd((s, 128), F32), sd((s, 512), F32), sd((s, 512), F32),
                   sd((s, 512), F32)],
        compiler_params=_params(("arbitrary",)),
    )(x, g_pre, w_pad)


def _tri(n, upper):
    r = lax.broadcasted_iota(jnp.int32, (n, n), 0)
    c = lax.broadcasted_iota(jnp.int32, (n, n), 1)
    return ((r <= c) if upper else (r >= c)).astype(MXU_DTYPE)


def _forget_fwd(z_t, b_col):
    rows, s = z_t.shape
    nb = s // 128

    def body(z_ref, b_ref, c_ref):
        upper = _tri(128, True)

        def blk(n, carry):
            off = pl.multiple_of(n * 128, 128)
            lf = jax.nn.log_sigmoid(z_ref[:, pl.ds(off, 128)] + b_ref[...])
            c_ref[:, pl.ds(off, 128)] = _dot_exact(lf, upper, 3) + carry
            return carry + jnp.sum(lf, axis=1, keepdims=True)

        lax.fori_loop(0, nb, blk, jnp.zeros((rows, 1), F32))

    return pl.pallas_call(body, name="forget_fwd", out_shape=jax.ShapeDtypeStruct((rows, s), F32),
                          compiler_params=_params())(z_t, b_col)


def _attn_fwd(qs, k, v_t, c_row, c_col, t, shards):
    h, s, _ = qs.shape
    n = s // t
    pairs = [(i, j) for i in range(n) for j in range(i + 1)]
    it = jnp.asarray(np.array([p[0] for p in pairs], np.int32))
    jt = jnp.asarray(np.array([p[1] for p in pairs], np.int32))
    nw = len(shards)
    last = len(pairs) - 1
    mid = (2 * len(pairs)) // 3

    def body(it_ref, jt_ref, q_ref, k_ref, vt_ref, cq_ref, ck_ref, *rest):
        sh, (o_ref, lse_ref), got = rest[:nw], rest[nw:nw + 2], rest[nw + 2:2 * nw + 2]
        m_sc, l_sc, acc_sc, send, recv = rest[2 * nw + 2:]
        p = pl.program_id(0)
        i = it_ref[p]
        j = jt_ref[p]
        gather_start, gather_forward, gather_finish = _gather_stages(sh, got, send, recv)
        pl.when(p == 0)(gather_start)
        if mid < last:
            pl.when(p == mid)(gather_forward)

        @pl.when(j == 0)
        def _():
            m_sc[...] = jnp.full_like(m_sc, -1e30)
            l_sc[...] = jnp.zeros_like(l_sc)
            acc_sc[...] = jnp.zeros_like(acc_sc)

        def step(hh, diagonal):
            st = _dot_nt(k_ref[hh], q_ref[hh]) + (cq_ref[hh] - ck_ref[hh])
            if diagonal:
                kpos = lax.broadcasted_iota(jnp.int32, (t, t), 0)
                qpos = lax.broadcasted_iota(jnp.int32, (t, t), 1)
                st = jnp.where(kpos <= qpos, st, -1e30)
            m_prev = m_sc[hh]
            m_new = jnp.maximum(m_prev, jnp.max(st, axis=0, keepdims=True))
            alpha = jnp.exp(m_prev - m_new)
            pt = jnp.exp(st - m_new)
            l_sc[hh] = alpha * l_sc[hh] + jnp.sum(pt, axis=0, keepdims=True)
            acc_sc[hh] = acc_sc[hh] * alpha + _dot(vt_ref[hh], pt.astype(MXU_DTYPE))
            m_sc[hh] = m_new

        @pl.when(j < i)
        def _():
            @pl.loop(0, h, step=2)
            def _(hh):
                step(hh, False)
                step(hh + 1, False)

        @pl.when(j == i)
        def _():
            @pl.loop(0, h, step=2)
            def _(hh):
                for h2 in (hh, hh + 1):
                    step(h2, True)
                    o_ref[h2] = acc_sc[h2] / l_sc[h2]
                    lse_ref[h2] = m_sc[h2] + jnp.log(l_sc[h2])

        @pl.when(p == last)
        def _():
            if mid >= last:
                gather_forward()
            gather_finish()

    gs = pltpu.PrefetchScalarGridSpec(
        num_scalar_prefetch=2, grid=(len(pairs),),
        in_specs=[pl.BlockSpec((h, t, HEAD_DIM), lambda p, it_, jt_: (0, it_[p], 0)),
                  pl.BlockSpec((h, t, HEAD_DIM), lambda p, it_, jt_: (0, jt_[p], 0)),
                  pl.BlockSpec((h, HEAD_DIM, t), lambda p, it_, jt_: (0, 0, jt_[p])),
                  pl.BlockSpec((h, 1, t), lambda p, it_, jt_: (0, 0, it_[p])),
                  pl.BlockSpec((h, t, 1), lambda p, it_, jt_: (0, jt_[p], 0))] + [ANY] * nw,
        out_specs=[pl.BlockSpec((h, HEAD_DIM, t), lambda p, it_, jt_: (0, 0, it_[p])),
                   pl.BlockSpec((h, 1, t), lambda p, it_, jt_: (0, 0, it_[p]))] + [ANY] * nw,
        scratch_shapes=[pltpu.VMEM((h, 1, t), F32), pltpu.VMEM((h, 1, t), F32), pltpu.VMEM((h, HEAD_DIM, t), F32),
                        pltpu.SemaphoreType.DMA((nw, 6)), pltpu.SemaphoreType.DMA((nw, 6))])
    o_t, lse, *got = pl.pallas_call(
        body, name="attn_fwd", grid_spec=gs,
        out_shape=[jax.ShapeDtypeStruct((h, HEAD_DIM, s), F32), jax.ShapeDtypeStruct((h, 1, s), F32)]
        + [jax.ShapeDtypeStruct((N_CHIPS,) + a.shape, a.dtype) for a in shards],
        compiler_params=_params(("arbitrary",)),
    )(it, jt, qs, k, v_t, c_row, c_col, *shards)
    me = 2 * lax.axis_index("x") + lax.axis_index("y")
    return o_t, lse, [lax.dynamic_update_index_in_dim(g, own, me, 0) for g, own in zip(got, shards)]


def _shift_down(cur, prev_ref, first):
    row = lax.broadcasted_iota(jnp.int32, cur.shape, 0)
    p7 = jnp.where(first, 0.0, prev_ref[0][7:8, :] * prev_ref[1][7:8, :])
    p6 = jnp.where(first, 0.0, prev_ref[0][6:7, :] * prev_ref[1][6:7, :])
    s1 = jnp.where(row == 0, p7, pltpu.roll(cur, 1, 0))
    s2 = jnp.where(row == 0, p6, jnp.where(row == 1, p7, pltpu.roll(cur, 2, 0)))
    return s1, s2


def _group_ms(v, gmat):
    return _dot_exact(v, gmat, 2) * (1.0 / HEAD_DIM)


def _mixer_fwd(x, o_attn, gate_b, gate_c, u, conv_w, g_attn, g_conv, w_out, g_post, gmat, tm):
    s = x.shape[0]

    def body(x_ref, o_ref, b_ref, c_ref, u_ref, cp_ref, up_ref, cw_ref, ga_ref, gc_ref, wo_ref, gp_ref, gm_ref,
             x2_ref, mg_ref, y_ref, z_ref):
        i = pl.program_id(0)
        cu = c_ref[...] * u_ref[...]
        cu1, cu2 = _shift_down(cu, (cp_ref, up_ref), i == 0)
        z = cw_ref[0:1, :] * cu2 + cw_ref[1:2, :] * cu1 + cw_ref[2:3, :] * cu
        z_ref[...] = z
        cv = b_ref[...] * z
        ov = o_ref[...]
        gm = gm_ref[...]
        ma = ((ov * lax.rsqrt(_group_ms(ov * ov, gm) + EPS)) * ga_ref[...]).astype(MXU_DTYPE)
        mc = ((cv * lax.rsqrt(_group_ms(cv * cv, gm) + EPS)) * gc_ref[...]).astype(MXU_DTYPE)
        mg_ref[:, 0:ATTN_W] = ma
        mg_ref[:, ATTN_W:D_MODEL] = mc
        y = _dot(ma, wo_ref[0:ATTN_W, :]) + _dot(mc, wo_ref[ATTN_W:D_MODEL, :])
        y_ref[...] = y
        x2_ref[...] = x_ref[...] + (y * _rms(y)) * gp_ref[...]

    halo = pl.BlockSpec((8, 512), lambda i: (jnp.maximum(i * (tm // 8) - 1, 0), 0))
    sd = jax.ShapeDtypeStruct
    return pl.pallas_call(
        body, name="mixer_fwd", grid=(s // tm,),
        in_specs=[_tok(tm, D_MODEL), _tok(tm, 512), _tok(tm, 512), _tok(tm, 512), _tok(tm, 512), halo, halo,
                  _whole((3, 512)), _whole((1, 512)), _whole((1, 512)), _whole((D_MODEL, D_MODEL), single=True),
                  _whole((1, D_MODEL)), _whole((512, 512))],
        out_specs=[_tok(tm, D_MODEL), _tok(tm, D_MODEL), _tok(tm, D_MODEL), _tok(tm, 512)],
        out_shape=[sd((s, D_MODEL), F32), sd((s, D_MODEL), MXU_DTYPE), sd((s, D_MODEL), F32), sd((s, 512), F32)],
        compiler_params=_params(("arbitrary",)),
    )(x, o_attn, gate_b, gate_c, u, gate_c, u, conv_w, g_attn, g_conv, w_out, g_post, gmat)


def _ffn_fwd(x2, target, g_pre, w_gu, w_dn, g_post, tm):
    s = x2.shape[0]

    def body(x_ref, t_ref, gpre_ref, wgu_ref, wdn_ref, gpost_ref,
             h_ref, g_ref, up_ref, a_ref, ff_ref, dout_ref, loss_ref):
        xv = x_ref[...]
        h = ((xv * _rms(xv)) * gpre_ref[...]).astype(MXU_DTYPE)
        h_ref[...] = h
        ff = jnp.zeros((tm, D_MODEL), F32)
        for j in range(2):
            cols = slice(j * FF_PIECE, (j + 1) * FF_PIECE)
            g = _dot(h, wgu_ref[j])
            up = _dot(h, wgu_ref[2 + j])
            a = ((g * jax.nn.sigmoid(g)) * up).astype(MXU_DTYPE)
            g_ref[:, cols] = g
            up_ref[:, cols] = up
            a_ref[:, cols] = a
            ff = ff + _dot(a, wdn_ref[j])
        ff_ref[...] = ff
        err = (xv + (ff * _rms(ff)) * gpost_ref[...]) - t_ref[...]
        dout_ref[...] = err * (1.0 / D_MODEL)
        part = jnp.sum(jnp.mean(err * err, axis=-1, keepdims=True), axis=0, keepdims=True)

        @pl.when(pl.program_id(0) == 0)
        def _():
            loss_ref[...] = jnp.zeros_like(loss_ref)

        loss_ref[...] += part

    sd = jax.ShapeDtypeStruct
    return pl.pallas_call(
        body, name="ffn_fwd", grid=(s // tm,),
        in_specs=[_tok(tm, D_MODEL), _tok(tm, D_MODEL), _whole((1, D_MODEL)),
                  _whole((4, D_MODEL, FF_PIECE), single=True), _whole((2, FF_PIECE, D_MODEL), single=True),
                  _whole((1, D_MODEL))],
        out_specs=[_tok(tm, D_MODEL), _tok(tm, D_FF), _tok(tm, D_FF), _tok(tm, D_FF), _tok(tm, D_MODEL),
                   _tok(tm, D_MODEL), _whole((8, 128))],
        out_shape=[sd((s, D_MODEL), MXU_DTYPE), sd((s, D_FF), F32), sd((s, D_FF), F32), sd((s, D_FF), MXU_DTYPE),
                   sd((s, D_MODEL), F32), sd((s, D_MODEL), F32), sd((8, 128), F32)],
        compiler_params=_params(("arbitrary",)),
    )(x2, target, g_pre, w_gu, w_dn, g_post)


def _norm_bwd(dy, normed, rinv, gain):
    t = dy * gain
    return rinv * (t - normed * jnp.mean(t * normed, axis=-1, keepdims=True))


def _acc_rows(ref, first, val):
    @pl.when(first)
    def _():
        ref[...] = jnp.zeros_like(ref)

    ref[...] += jnp.sum(val, axis=0, keepdims=True)


def _ffn_bwd(dout, ff, x2, g, up, g_post, g_pre, w_gu, w_dn, tm):
    s = x2.shape[0]

    def body(do_ref, ff_ref, x_ref, g_ref, up_ref, gpost_ref, gpre_ref, wgu_ref, wdn_ref,
             dx_ref, dff_ref, dgu_ref, dgpost_ref, dgpre_ref):
        first = pl.program_id(0) == 0
        ffv = ff_ref[...]
        rf = _rms(ffv)
        n = ffv * rf
        do = do_ref[...]
        _acc_rows(dgpost_ref, first, do * n)
        dff = _norm_bwd(do, n, rf, gpost_ref[...]).astype(MXU_DTYPE)
        dff_ref[...] = dff
        dh = jnp.zeros((tm, D_MODEL), F32)
        for j in range(2):
            cols = slice(j * FF_PIECE, (j + 1) * FF_PIECE)
            da = _dot_nt(dff, wdn_ref[j])
            gv = g_ref[:, cols]
            sg = jax.nn.sigmoid(gv)
            dg = (da * up_ref[:, cols] * (sg * (1.0 + gv * (1.0 - sg)))).astype(MXU_DTYPE)
            du = (da * (gv * sg)).astype(MXU_DTYPE)
            dgu_ref[:, cols] = dg
            dgu_ref[:, D_FF + j * FF_PIECE:D_FF + (j + 1) * FF_PIECE] = du
            dh = dh + _dot_nt(dg, wgu_ref[j]) + _dot_nt(du, wgu_ref[2 + j])
        xv = x_ref[...]
        r2 = _rms(xv)
        nx = xv * r2
        _acc_rows(dgpre_ref, first, dh * nx)
        dx_ref[...] = do + _norm_bwd(dh, nx, r2, gpre_ref[...])

    sd = jax.ShapeDtypeStruct
    return pl.pallas_call(
        body, name="ffn_bwd", grid=(s // tm,),
        in_specs=[_tok(tm, D_MODEL), _tok(tm, D_MODEL), _tok(tm, D_MODEL), _tok(tm, D_FF), _tok(tm, D_FF),
                  _whole((1, D_MODEL)), _whole((1, D_MODEL)),
                  _whole((4, D_MODEL, FF_PIECE), single=True), _whole((2, FF_PIECE, D_MODEL), single=True)],
        out_specs=[_tok(tm, D_MODEL), _tok(tm, D_MODEL), _tok(tm, 2 * D_FF), _whole((1, D_MODEL)),
                   _whole((1, D_MODEL))],
        out_shape=[sd((s, D_MODEL), F32), sd((s, D_MODEL), MXU_DTYPE), sd((s, 2 * D_FF), MXU_DTYPE),
                   sd((1, D_MODEL), F32), sd((1, D_MODEL), F32)],
        compiler_params=_params(("arbitrary",)),
    )(dout, ff, x2, g, up, g_post, g_pre, w_gu, w_dn)


def _tn_matmul(a, b, tm, tn, tk, name):
    s, m = a.shape
    n = b.shape[1]

    def body(a_ref, b_ref, o_ref):
        @pl.when(pl.program_id(2) == 0)
        def _():
            o_ref[...] = jnp.zeros_like(o_ref)

        o_ref[...] += lax.dot_general(a_ref[...], b_ref[...], (((0,), (0,)), ((), ())), preferred_element_type=F32)

    return pl.pallas_call(
        body, name=name, grid=(m // tm, n // tn, s // tk),
        in_specs=[pl.BlockSpec((tk, tm), lambda i, j, kk: (kk, i)), pl.BlockSpec((tk, tn), lambda i, j, kk: (kk, j))],
        out_specs=pl.BlockSpec((tm, tn), lambda i, j, kk: (i, j)),
        out_shape=jax.ShapeDtypeStruct((m, n), F32),
        compiler_params=_params(("arbitrary", "arbitrary", "arbitrary")),
    )(a, b)


def _mixer_bwd(dx2, y, o_attn, gate_b, z, g_post, g_attn, g_conv, w_out, gmat, sel, tm, ready, kinds):
    s = dx2.shape[0]
    nw = len(ready)
    nt = s // tm

    def body(d_ref, y_ref, o_ref, b_ref, z_ref, gp_ref, ga_ref, gc_ref, wo_ref, gm_ref, sel_ref, *rest):
        grads = rest[:nw]
        dy_ref, do_ref, db_ref, dz_ref, delta_ref, dgp_ref, dga_ref, dgc_ref = rest[nw:nw + 8]
        taken = rest[nw + 8:2 * nw + 8]
        send, recv = rest[2 * nw + 8:]
        first = pl.program_id(0) == 0
        pair_start, pair_finish = _pair_stages(grads, kinds, taken, send, recv)
        pl.when(first)(pair_start)
        yv = y_ref[...]
        ry = _rms(yv)
        ny = yv * ry
        d = d_ref[...]
        _acc_rows(dgp_ref, first, d * ny)
        dy = _norm_bwd(d, ny, ry, gp_ref[...]).astype(MXU_DTYPE)
        dy_ref[...] = dy
        dm = _dot_nt(dy, wo_ref[...])
        gm = gm_ref[...]

        def group_bwd(val, dmv, gain, dg_ref):
            rg = lax.rsqrt(_group_ms(val * val, gm) + EPS)
            nv = val * rg
            _acc_rows(dg_ref, first, dmv * nv)
            t = dmv * gain
            return rg * (t - nv * _group_ms(t * nv, gm))

        ov = o_ref[...]
        d_o = group_bwd(ov, dm[:, 0:ATTN_W], ga_ref[...], dga_ref)
        do_ref[...] = d_o
        delta_ref[...] = _dot_exact(d_o * ov, sel_ref[...], 2)
        zv = z_ref[...]
        bv = b_ref[...]
        d_cv = group_bwd(bv * zv, dm[:, ATTN_W:D_MODEL], gc_ref[...], dgc_ref)
        db_ref[...] = d_cv * zv
        dz_ref[...] = d_cv * bv
        pl.when(pl.program_id(0) == nt - 1)(pair_finish)

    sd = jax.ShapeDtypeStruct
    taken_shape = [sd((N_CHIPS, g.shape[-2], g.shape[-1] if kd == "rows" else g.shape[-1] // N_CHIPS), F32)
                   for g, kd in zip(ready, kinds)]
    out = pl.pallas_call(
        body, name="mixer_bwd", grid=(nt,),
        in_specs=[_tok(tm, D_MODEL), _tok(tm, D_MODEL), _tok(tm, 512), _tok(tm, 512), _tok(tm, 512),
                  _whole((1, D_MODEL)), _whole((1, 512)), _whole((1, 512)),
                  _whole((D_MODEL, D_MODEL), single=True), _whole((512, 512)), _whole((512, 128))] + [ANY] * nw,
        out_specs=[_tok(tm, D_MODEL), _tok(tm, 512), _tok(tm, 512), _tok(tm, 512), _tok(tm, 128),
                   _whole((1, D_MODEL)), _whole((1, 512)), _whole((1, 512))] + [ANY] * nw,
        out_shape=[sd((s, D_MODEL), MXU_DTYPE), sd((s, 512), F32), sd((s, 512), F32), sd((s, 512), F32),
                   sd((s, 128), F32), sd((1, D_MODEL), F32), sd((1, 512), F32), sd((1, 512), F32)] + taken_shape,
        scratch_shapes=[pltpu.SemaphoreType.DMA((nw, N_CHIPS)), pltpu.SemaphoreType.DMA((nw, N_CHIPS))],
        compiler_params=_params(("arbitrary",)),
    )(dx2, y, o_attn, gate_b, z, g_post, g_attn, g_conv, w_out, gmat, sel, *ready)
    return out[:8], out[8:]


def _attn_bwd(qs, k, k_t, v, do, c_row, c_col, lse, delta, t, parts):
    h, s, _ = qs.shape
    n = s // t
    pairs = [(i, j) for j in range(n) for i in range(j, n)]
    it = jnp.asarray(np.array([p[0] for p in pairs], np.int32))
    jt = jnp.asarray(np.array([p[1] for p in pairs], np.int32))

    nw = len(parts)

    def body(it_ref, jt_ref, q_ref, k_ref, kt_ref, v_ref, do_ref, cq_ref, ck_ref, lse_ref, dl_ref, *rest):
        pb = rest[:nw]
        dq_ref, dk_ref, dv_ref, dc_ref, dcq_ref = rest[nw:nw + 5]
        rcv = rest[nw + 5:2 * nw + 5]
        dk_sc, dv_sc, dc_sc, send, recv = rest[2 * nw + 5:]
        p = pl.program_id(0)
        i = it_ref[p]
        j = jt_ref[p]
        chip_start, chip_finish = _chip_stages(pb, rcv, send, recv)

        @pl.when(p == 0)
        def _():
            chip_start()
            dq_ref[...] = jnp.zeros_like(dq_ref)
            dcq_ref[...] = jnp.zeros_like(dcq_ref)

        @pl.when(i == j)
        def _():
            dk_sc[...] = jnp.zeros_like(dk_sc)
            dv_sc[...] = jnp.zeros_like(dv_sc)
            dc_sc[...] = jnp.zeros_like(dc_sc)

        def step(hh, diagonal):
            qv = q_ref[hh]
            dov = do_ref[hh]
            st = _dot_nt(k_ref[hh], qv) + ((cq_ref[hh] - lse_ref[hh]) - ck_ref[hh])
            pt = jnp.exp(st)
            if diagonal:
                kpos = lax.broadcasted_iota(jnp.int32, (t, t), 0)
                qpos = lax.broadcasted_iota(jnp.int32, (t, t), 1)
                pt = jnp.where(kpos <= qpos, pt, 0.0)
            dv_sc[hh] += _dot(pt.astype(MXU_DTYPE), dov)
            dst = pt * (_dot_nt(v_ref[hh], dov) - dl_ref[hh])
            dc_sc[hh] -= jnp.sum(dst, axis=1, keepdims=True)
            dcq_ref[hh, i] += jnp.sum(dst, axis=0, keepdims=True)
            dsb = dst.astype(MXU_DTYPE)
            dk_sc[hh] += _dot(dsb, qv)
            dq_ref[hh, i] += _dot(kt_ref[hh], dsb)

        @pl.when(i > j)
        def _():
            @pl.loop(0, h, step=2)
            def _(hh):
                step(hh, False)
                step(hh + 1, False)

        @pl.when(i == j)
        def _():
            @pl.loop(0, h, step=2)
            def _(hh):
                step(hh, True)
                step(hh + 1, True)

        @pl.when(i == n - 1)
        def _():
            dk_ref[...] = dk_sc[...]
            dv_ref[...] = dv_sc[...]
            dc_ref[...] = dc_sc[...]

        pl.when(p == len(pairs) - 1)(chip_finish)

    qi = lambda p, it_, jt_: (0, it_[p], 0)
    kj = lambda p, it_, jt_: (0, jt_[p], 0)
    row_i = lambda p, it_, jt_: (0, 0, it_[p])
    gs = pltpu.PrefetchScalarGridSpec(
        num_scalar_prefetch=2, grid=(len(pairs),),
        in_specs=[pl.BlockSpec((h, t, HEAD_DIM), qi), pl.BlockSpec((h, t, HEAD_DIM), kj),
                  pl.BlockSpec((h, HEAD_DIM, t), lambda p, it_, jt_: (0, 0, jt_[p])),
                  pl.BlockSpec((h, t, HEAD_DIM), kj), pl.BlockSpec((h, t, HEAD_DIM), qi),
                  pl.BlockSpec((h, 1, t), row_i), pl.BlockSpec((h, t, 1), kj),
                  pl.BlockSpec((h, 1, t), row_i), pl.BlockSpec((h, 1, t), row_i)] + [ANY] * nw,
        out_specs=[pl.BlockSpec((h, n, HEAD_DIM, t), lambda p, it_, jt_: (0, 0, 0, 0)),
                   pl.BlockSpec((h, t, HEAD_DIM), kj), pl.BlockSpec((h, t, HEAD_DIM), kj),
                   pl.BlockSpec((h, t, 1), kj),
                   pl.BlockSpec((h, n, 1, t), lambda p, it_, jt_: (0, 0, 0, 0))] + [ANY] * nw,
        scratch_shapes=[pltpu.VMEM((h, t, HEAD_DIM), F32), pltpu.VMEM((h, t, HEAD_DIM), F32),
                        pltpu.VMEM((h, t, 1), F32), pltpu.SemaphoreType.DMA((nw, 3)), pltpu.SemaphoreType.DMA((nw, 3))])
    sd = jax.ShapeDtypeStruct
    out = pl.pallas_call(
        body, name="attn_bwd", grid_spec=gs,
        out_shape=[sd((h, n, HEAD_DIM, t), F32), sd((h, s, HEAD_DIM), F32), sd((h, s, HEAD_DIM), F32),
                   sd((h, s, 1), F32), sd((h, n, 1, t), F32)] + [sd((3,) + a.shape[1:], a.dtype) for a in parts],
        compiler_params=_params(("arbitrary",)),
    )(it, jt, qs, k, k_t, v, do, c_row, c_col, lse, delta, *parts)
    return out[:5], out[5:]


def _forget_bwd(dc_t, z_t, b_col):
    rows, s = z_t.shape
    nb = s // 128

    def body(dc_ref, z_ref, b_ref, dz_ref, db_ref):
        lower = _tri(128, False)

        def blk(m, carry):
            tail, dbias = carry
            off = pl.multiple_of((nb - 1 - m) * 128, 128)
            dc = dc_ref[:, pl.ds(off, 128)]
            dlf = _dot_exact(dc, lower, 3) + tail
            dz = dlf * jax.nn.sigmoid(-(z_ref[:, pl.ds(off, 128)] + b_ref[...]))
            dz_ref[:, pl.ds(off, 128)] = dz
            return tail + jnp.sum(dc, axis=1, keepdims=True), dbias + jnp.sum(dz, axis=1, keepdims=True)

        zero = jnp.zeros((rows, 1), F32)
        _, dbias = lax.fori_loop(0, nb, blk, (zero, zero))
        db_ref[...] = jnp.broadcast_to(dbias, db_ref.shape)

    return pl.pallas_call(
        body, name="forget_bwd",
        out_shape=[jax.ShapeDtypeStruct((rows, s), F32), jax.ShapeDtypeStruct((rows, 128), F32)],
        compiler_params=_params())(dc_t, z_t, b_col)


def _inproj_bwd(dz, gate_c, u, conv_w, dq, dk, dv, dzf, db, x, dx2, g_pre, w_pad, tm):
    s = x.shape[0]
    nt = s // tm

    def body(dz_ref, dzn_ref, c_ref, u_ref, cp_ref, up_ref, cw_ref, dq_ref, dk_ref, dv_ref, dzf_ref, db_ref,
             x_ref, dx2_ref, g_ref, w_ref, gx_ref, dp_ref, dg_ref, dcw_ref):
        i = pl.program_id(0)
        first = i == 0
        last = i == nt - 1
        dzv = dz_ref[...]
        row = lax.broadcasted_iota(jnp.int32, dzv.shape, 0)
        n0 = jnp.where(last, 0.0, dzn_ref[0:1, :])
        n1 = jnp.where(last, 0.0, dzn_ref[1:2, :])
        dz1 = jnp.where(row == tm - 1, n0, pltpu.roll(dzv, tm - 1, 0))
        dz2 = jnp.where(row == tm - 1, n1, jnp.where(row == tm - 2, n0, pltpu.roll(dzv, tm - 2, 0)))
        dcu = cw_ref[2:3, :] * dzv + cw_ref[1:2, :] * dz1 + cw_ref[0:1, :] * dz2
        cv = c_ref[...]
        uv = u_ref[...]
        cu = cv * uv
        cu1, cu2 = _shift_down(cu, (cp_ref, up_ref), first)

        @pl.when(first)
        def _():
            dcw_ref[...] = jnp.zeros_like(dcw_ref)

        dcw_ref[0:1, :] += jnp.sum(dzv * cu2, axis=0, keepdims=True)
        dcw_ref[1:2, :] += jnp.sum(dzv * cu1, axis=0, keepdims=True)
        dcw_ref[2:3, :] += jnp.sum(dzv * cu, axis=0, keepdims=True)

        dp_ref[:, 0:512] = (dq_ref[...] * Q_SCALE).astype(MXU_DTYPE)
        dp_ref[:, 512:1024] = dk_ref[...].astype(MXU_DTYPE)
        dp_ref[:, 1024:OFF_F] = dv_ref[...].astype(MXU_DTYPE)
        dp_ref[:, OFF_F:OFF_B] = dzf_ref[...].astype(MXU_DTYPE)
        dp_ref[:, OFF_B:OFF_C] = db_ref[...].astype(MXU_DTYPE)
        dp_ref[:, OFF_C:OFF_U] = (dcu * uv).astype(MXU_DTYPE)
        dp_ref[:, OFF_U:IN_PAD] = (dcu * cv).astype(MXU_DTYPE)
        dh = _dot_nt(dp_ref[...], w_ref[...])
        xv = x_ref[...]
        r1 = _rms(xv)
        nx = xv * r1
        _acc_rows(dg_ref, first, dh * nx)
        gx_ref[...] = dx2_ref[...] + _norm_bwd(dh, nx, r1, g_ref[...])

    prev = pl.BlockSpec((8, 512), lambda i: (jnp.maximum(i * (tm // 8) - 1, 0), 0))
    nxt = pl.BlockSpec((8, 512), lambda i: (jnp.minimum((i + 1) * (tm // 8), s // 8 - 1), 0))
    sd = jax.ShapeDtypeStruct
    return pl.pallas_call(
        body, name="inproj_bwd", grid=(nt,),
        in_specs=[_tok(tm, 512), nxt, _tok(tm, 512), _tok(tm, 512), prev, prev, _whole((3, 512)),
                  _tok(tm, 512), _tok(tm, 512), _tok(tm, 512), _tok(tm, 128), _tok(tm, 512),
                  _tok(tm, D_MODEL), _tok(tm, D_MODEL), _whole((1, D_MODEL)), _whole((D_MODEL, IN_PAD), single=True)],
        out_specs=[_tok(tm, D_MODEL), _tok(tm, IN_PAD), _whole((1, D_MODEL)), _whole((8, 512))],
        out_shape=[sd((s, D_MODEL), F32), sd((s, IN_PAD), MXU_DTYPE), sd((1, D_MODEL), F32), sd((8, 512), F32)],
        compiler_params=_params(("arbitrary",)),
    )(dz, dz, gate_c, u, gate_c, u, conv_w, dq, dk, dv, dzf, db, x, dx2, g_pre, w_pad)


def _heads(a):
    s = a.shape[0]
    return jnp.transpose(a.reshape(s, N_HEADS, HEAD_DIM), (1, 0, 2))


def _heads_t(a):
    s = a.shape[0]
    return jnp.transpose(a.reshape(s, N_HEADS, HEAD_DIM), (1, 2, 0))


def _unheads(a):
    s = a.shape[1]
    return jnp.transpose(a, (1, 0, 2)).reshape(s, N_HEADS * HEAD_DIM)


def _tile(s, want):
    return want if s % want == 0 else s


def _halves(a):
    return a.reshape(2, a.shape[0] // 2, a.shape[1])


def _device_step(x, target, w, c_idx, me_idx):
    s = x.shape[0]
    tm = _tile(s, 512)
    tf = _tile(s, 256)
    ta = _tile(s, 512)
    tkk = _tile(s, 2048)
    rows = 16
    gidx = np.arange(512) // HEAD_DIM
    gmat = jnp.asarray(gidx[:, None] == gidx[None, :], MXU_DTYPE)
    sel = jnp.asarray(gidx[:, None] == np.arange(128)[None, :], MXU_DTYPE)
    g_mix_pre, g_mix_post, g_ffn_pre, g_ffn_post = w["g_mix_pre"], w["g_mix_post"], w["g_ffn_pre"], w["g_ffn_post"]
    g_attn, g_conv, b_forget = w["g_attn_out"], w["g_conv_out"], w["b_forget"]
    shard = {n: _halves(w[n][0].astype(MXU_DTYPE)) for n in BIG}

    g_in, conv_all = _gather_weights([shard["w_in"]], w["conv_w"][0])
    w_in_full = jnp.transpose(g_in.reshape(N_CHIPS, D_MODEL, IN_W // N_CHIPS), (1, 0, 2)).reshape(D_MODEL, IN_W)
    w_pad = jnp.concatenate([w_in_full[:, :OFF_F + N_HEADS], jnp.zeros((D_MODEL, OFF_B - OFF_F - N_HEADS), MXU_DTYPE),
                             w_in_full[:, OFF_F + N_HEADS:]], axis=1)
    conv_w = jnp.transpose(conv_all, (1, 0, 2)).reshape(3, CONV_W)

    h1, qs, k, v, f, gate_b, gate_c, u = _inproj_fwd(x, g_mix_pre, w_pad, tm)
    z_t = jnp.pad(jnp.transpose(f[:, :N_HEADS]), ((0, rows - N_HEADS), (0, 0)))
    b_col = jnp.pad(jnp.transpose(b_forget), ((0, rows - N_HEADS), (0, 0)))
    c_t = _forget_fwd(z_t, b_col)
    c_row = c_t[:N_HEADS].reshape(N_HEADS, 1, s)
    c_col = c_t[:N_HEADS].reshape(N_HEADS, s, 1)
    qh, kh, vh = _heads(qs), _heads(k), _heads(v)
    o_t, lse, (g_out, g_gu, g_dn) = _attn_fwd(qh, kh, _heads_t(v), c_row, c_col, ta,
                                               [shard["w_out"], shard["w_gate_up"], shard["w_down"]])
    w_out = g_out.reshape(D_MODEL, D_MODEL)
    w_gu = g_gu.reshape(N_CHIPS, D_MODEL, FF_PIECE)
    w_dn = g_dn.reshape(2, FF_PIECE, D_MODEL)
    o_attn = jnp.transpose(o_t, (2, 0, 1)).reshape(s, ATTN_W)
    x2, merged, y, z = _mixer_fwd(x, o_attn, gate_b, gate_c, u, conv_w, g_attn, g_conv, w_out, g_mix_post, gmat, tm)
    h2, g, up, a, ff, dout, loss_acc = _ffn_fwd(x2, target, g_ffn_pre, w_gu, w_dn, g_ffn_post, tf)

    dx2, dff, dgu, dg_ffn_post, dg_ffn_pre = _ffn_bwd(dout, ff, x2, g, up, g_ffn_post, g_ffn_pre, w_gu, w_dn, tf)
    dw_dn = _tn_matmul(a, dff, FF_PIECE, 512, tkk, "dw_down").reshape(N_CHIPS, 2, D_FF // (2 * N_CHIPS), D_MODEL)
    dw_gu = _tn_matmul(h2, dgu, 1024, 512, tkk, "dw_gate_up").reshape(2, D_MODEL // 2, 2 * D_FF)
    (dy, d_o, d_b, dz, delta8, dg_mix_post, dg_attn, dg_conv), (a_gu, a_dn) = _mixer_bwd(
        dx2, y, o_attn, gate_b, z, g_mix_post, g_attn, g_conv, w_out, gmat, sel, tm, [dw_gu, dw_dn], ["cols", "rows"])
    dw_out = _tn_matmul(merged, dy, 1024, 512, tkk, "dw_out")
    sum_gu = _pair_sum(c_idx, dw_gu, "cols", a_gu, "pair_sum_w_gate_up")
    sum_dn = _pair_sum(c_idx, dw_dn, "rows", a_dn, "pair_sum_w_down")
    delta = jnp.transpose(delta8[:, :N_HEADS]).reshape(N_HEADS, 1, s)
    (dq4, dkh, dvh, dc, dcq), (r_gu, r_dn) = _attn_bwd(
        qh, kh, _heads_t(k), vh, _heads(d_o.astype(MXU_DTYPE)), c_row, c_col, lse, delta, ta, [sum_gu[1], sum_dn[1]])
    dc_t = jnp.pad(dc.reshape(N_HEADS, s) + dcq.reshape(N_HEADS, s), ((0, rows - N_HEADS), (0, 0)))
    dz_t, db_f = _forget_bwd(dc_t, z_t, b_col)
    dzf = jnp.pad(jnp.transpose(dz_t[:N_HEADS]), ((0, 0), (0, 128 - N_HEADS)))
    dq = jnp.transpose(dq4, (1, 3, 0, 2)).reshape(s, ATTN_W)
    grad_x, dproj, dg_mix_pre, dcw = _inproj_bwd(dz, gate_c, u, conv_w, dq, _unheads(dkh), _unheads(dvh), dzf, d_b,
                                                 x, dx2, g_mix_pre, w_pad, tm)
    dw_pad = _tn_matmul(h1, dproj, 1024, 640, tkk, "dw_in")
    dw_in = jnp.concatenate([dw_pad[:, :OFF_F + N_HEADS], dw_pad[:, OFF_B:]], axis=1)
    dw_in = jnp.transpose(dw_in.reshape(D_MODEL, N_CHIPS, IN_W // N_CHIPS), (1, 0, 2))
    dw_in = dw_in.reshape(N_CHIPS, 2, D_MODEL // 2, IN_W // N_CHIPS)
    dw_out = dw_out.reshape(N_CHIPS, 2, D_MODEL // (2 * N_CHIPS), D_MODEL)

    a_in, a_out = _pair_exchange([dw_in, dw_out])
    sum_in = _pair_sum(c_idx, dw_in, "rows", a_in, "pair_sum_w_in")
    sum_out = _pair_sum(c_idx, dw_out, "rows", a_out, "pair_sum_w_out")
    small = dict(b_forget=db_f[:N_HEADS, 0], g_attn_out=dg_attn, g_conv_out=dg_conv, g_mix_pre=dg_mix_pre,
                 g_mix_post=dg_mix_post, g_ffn_pre=dg_ffn_pre, g_ffn_post=dg_ffn_post)
    r_in, r_out, small_all = _chip_exchange([sum_in[1], sum_out[1]], _pack_small(small, dcw[:3]))
    totals = [_chip_sum(me_idx, sb[0], r, "chip_sum_" + n)
              for n, sb, r in zip(BIG, (sum_in, sum_out, sum_gu, sum_dn), (r_in, r_out, r_gu, r_dn))]
    return loss_acc[0, 0], grad_x, totals, _pair_share(totals), small_all


BIG = ("w_in", "w_out", "w_gate_up", "w_down")
ANY = pl.BlockSpec(memory_space=pl.ANY)


def _place():
    x, y, c = lax.axis_index("x"), lax.axis_index("y"), lax.axis_index("c")
    others = [(1 - x, y), (x, 1 - y), (1 - x, 1 - y)]
    return x, y, c, 2 * x + y, others, [2 * px + py for px, py in others]


def _remote(src, dst, send, recv, dev):
    return pltpu.make_async_remote_copy(src_ref=src, dst_ref=dst, send_sem=send, recv_sem=recv,
                                        device_id=dev, device_id_type=MESH_ID)


def _gather_stages(sh, outs, send, recv):
    x, y, c, me, others, chips = _place()
    sib = (x, y, 1 - c)
    every = [(w, kk) for w in range(len(sh)) for kk in range(3)]

    def first(w, kk):
        return _remote(sh[w].at[c], outs[w].at[me, c], send.at[w, kk], recv.at[w, kk], (*others[kk], c))

    def landed(w, kk):
        r = outs[w].at[chips[kk], c]
        return _remote(r, r, send.at[w, kk], recv.at[w, kk], (*others[kk], c))

    def onward(w, kk, half):
        r = outs[w].at[chips[kk], half]
        return _remote(r, r, send.at[w, 3 + kk], recv.at[w, 3 + kk], sib)

    def start():
        for w, kk in every:
            first(w, kk).start()

    def forward():
        for w, kk in every:
            landed(w, kk).wait_recv()
            onward(w, kk, c).start()

    def finish():
        for w, kk in every:
            onward(w, kk, 1 - c).wait_recv()
        for w, kk in every:
            first(w, kk).wait_send()
            onward(w, kk, c).wait_send()

    return start, forward, finish


def _pair_piece(ref, kind, p, half):
    if kind == "rows":
        return ref.at[p, half]
    cols = ref.shape[2] // N_CHIPS
    return ref.at[half, :, pl.ds(p * cols, cols)]


def _pair_stages(g, kinds, a, send, recv):
    x, y, c, _, _, _ = _place()
    copies = [_remote(_pair_piece(g[w], kinds[w], p, 1 - c), a[w].at[p], send.at[w, p], recv.at[w, p], (x, y, 1 - c))
              for w in range(len(g)) for p in range(N_CHIPS)]

    def start():
        for cp in copies:
            cp.start()

    def finish():
        for cp in copies:
            cp.wait()

    return start, finish


def _chip_stages(pb, rcv, send, recv):
    x, y, c, _, others, chips = _place()
    copies = [_remote(pb[w].at[chips[kk]], rcv[w].at[kk], send.at[w, kk], recv.at[w, kk], (*others[kk], c))
              for w in range(len(pb)) for kk in range(3)]

    def start():
        for cp in copies:
            cp.start()

    def finish():
        for cp in copies:
            cp.wait()

    return start, finish


def _gather_weights(shards, conv_w):
    n = len(shards)

    def body(*refs):
        sh, cw, outs, cwo = refs[:n], refs[n], refs[n + 1:2 * n + 1], refs[2 * n + 1]
        send, recv = refs[2 * n + 2:]
        x, y, c, me, others, chips = _place()
        start, forward, finish = _gather_stages(sh, outs, send, recv)
        start()
        small = [_remote(cw, cwo.at[me], send.at[n, kk], recv.at[n, kk], (*others[kk], c)) for kk in range(3)]
        for cp in small:
            cp.start()
        forward()
        for kk in range(3):
            _remote(cw, cwo.at[chips[kk]], send.at[n, kk], recv.at[n, kk], (*others[kk], c)).wait_recv()
        finish()
        for cp in small:
            cp.wait_send()

    out_shape = [jax.ShapeDtypeStruct((N_CHIPS,) + s.shape, s.dtype) for s in shards]
    out_shape.append(jax.ShapeDtypeStruct((N_CHIPS,) + conv_w.shape, conv_w.dtype))
    got = pl.pallas_call(
        body, name="gather_weights", in_specs=[ANY] * (n + 1), out_specs=[ANY] * (n + 1), out_shape=out_shape,
        scratch_shapes=[pltpu.SemaphoreType.DMA((n + 1, 6)), pltpu.SemaphoreType.DMA((n + 1, 6))],
    )(*shards, conv_w)
    me = 2 * lax.axis_index("x") + lax.axis_index("y")
    return [lax.dynamic_update_index_in_dim(g, own, me, 0) for g, own in zip(got, list(shards) + [conv_w])]


def _pair_exchange(grads):
    n = len(grads)

    def body(*refs):
        g, a = refs[:n], refs[n:2 * n]
        send, recv = refs[2 * n:]
        x, y, c, _, _, _ = _place()
        sib = (x, y, 1 - c)
        copies = [_remote(g[w].at[p, 1 - c], a[w].at[p], send.at[w, p], recv.at[w, p], sib)
                  for w in range(n) for p in range(N_CHIPS)]
        for cp in copies:
            cp.start()
        for cp in copies:
            cp.wait()

    return pl.pallas_call(
        body, name="pair_exchange", in_specs=[ANY] * n, out_specs=[ANY] * n,
        out_shape=[jax.ShapeDtypeStruct((N_CHIPS,) + g.shape[2:], g.dtype) for g in grads],
        scratch_shapes=[pltpu.SemaphoreType.DMA((n, N_CHIPS)), pltpu.SemaphoreType.DMA((n, N_CHIPS))],
    )(*grads)


def _pair_sum(c_idx, g, kind, a, name):
    _, half, cols = a.shape
    if kind == "rows":
        mine = pl.BlockSpec((1, 1, half, cols), lambda p, cr: (p, cr[0], 0, 0))
    else:
        mine = pl.BlockSpec((1, half, cols), lambda p, cr: (cr[0], 0, p))

    def body(c_ref, g_ref, a_ref, pf_ref, pb_ref):
        tot = (g_ref[0, 0] if kind == "rows" else g_ref[0]) + a_ref[0]
        pf_ref[0] = tot
        pb_ref[0] = tot.astype(BF16)

    gs = pltpu.PrefetchScalarGridSpec(
        num_scalar_prefetch=1, grid=(N_CHIPS,),
        in_specs=[mine,
                  pl.BlockSpec((1, half, cols), lambda p, cr: (p, 0, 0))],
        out_specs=[pl.BlockSpec((1, half, cols), lambda p, cr: (p, 0, 0)),
                   pl.BlockSpec((1, half, cols), lambda p, cr: (p, 0, 0))])
    return pl.pallas_call(
        body, name=name, grid_spec=gs,
        out_shape=[jax.ShapeDtypeStruct((N_CHIPS, half, cols), F32), jax.ShapeDtypeStruct((N_CHIPS, half, cols), BF16)],
        compiler_params=_params(("arbitrary",)),
    )(c_idx, g, a)


def _chip_exchange(parts, small):
    n = len(parts)

    def body(*refs):
        pb, sm, rcv, smg = refs[:n], refs[n], refs[n + 1:2 * n + 1], refs[2 * n + 1]
        send, recv, ssend, srecv, loc = refs[2 * n + 2:]
        x, y, c, _, others, chips = _place()
        mine = 4 * x + 2 * y + c
        own = pltpu.make_async_copy(sm, smg.at[mine], loc)
        own.start()
        copies = [_remote(pb[w].at[chips[kk]], rcv[w].at[kk], send.at[w, kk], recv.at[w, kk], (px, py, c))
                  for w in range(n) for kk, (px, py) in enumerate(others)]
        for r in range(1, 8):
            peer = (1 - x if r & 4 else x, 1 - y if r & 2 else y, 1 - c if r & 1 else c)
            copies.append(_remote(sm, smg.at[mine], ssend.at[r - 1], srecv.at[r - 1], peer))
        for cp in copies:
            cp.start()
        for w in range(n):
            for kk, (px, py) in enumerate(others):
                _remote(pb[w].at[chips[kk]], rcv[w].at[kk], send.at[w, kk], recv.at[w, kk], (px, py, c)).wait_recv()
        for r in range(1, 8):
            px, py, pc = (1 - x if r & 4 else x, 1 - y if r & 2 else y, 1 - c if r & 1 else c)
            _remote(sm, smg.at[4 * px + 2 * py + pc], ssend.at[r - 1], srecv.at[r - 1], (px, py, pc)).wait_recv()
        for cp in copies:
            cp.wait_send()
        own.wait()

    out_shape = [jax.ShapeDtypeStruct((3,) + p.shape[1:], p.dtype) for p in parts]
    out_shape.append(jax.ShapeDtypeStruct((8,) + small.shape, small.dtype))
    return pl.pallas_call(
        body, name="chip_exchange", in_specs=[ANY] * (n + 1), out_specs=[ANY] * (n + 1), out_shape=out_shape,
        scratch_shapes=[pltpu.SemaphoreType.DMA((n, 3)), pltpu.SemaphoreType.DMA((n, 3)),
                        pltpu.SemaphoreType.DMA((7,)), pltpu.SemaphoreType.DMA((7,)), pltpu.SemaphoreType.DMA(())],
    )(*parts, small)


def _chip_sum(me_idx, pf, rcv, name):
    _, half, cols = pf.shape

    def body(me_ref, pf_ref, r_ref, t_ref):
        t_ref[...] = ((pf_ref[0] + r_ref[0].astype(F32)) + r_ref[1].astype(F32)) + r_ref[2].astype(F32)

    gs = pltpu.PrefetchScalarGridSpec(
        num_scalar_prefetch=1, grid=(1,),
        in_specs=[pl.BlockSpec((1, half, cols), lambda i, mr: (mr[0], 0, 0)),
                  pl.BlockSpec((3, half, cols), lambda i, mr: (0, 0, 0))],
        out_specs=pl.BlockSpec((half, cols), lambda i, mr: (0, 0)))
    return pl.pallas_call(
        body, name=name, grid_spec=gs, out_shape=jax.ShapeDtypeStruct((half, cols), F32),
        compiler_params=_params(("arbitrary",)),
    )(me_idx, pf, rcv)


def _pair_share(totals):
    n = len(totals)

    def body(*refs):
        t, g = refs[:n], refs[n:2 * n]
        send, recv = refs[2 * n:]
        x, y, c, _, _, _ = _place()
        copies = [_remote(t[w], g[w], send.at[w], recv.at[w], (x, y, 1 - c)) for w in range(n)]
        for cp in copies:
            cp.start()
        for cp in copies:
            cp.wait()

    return pl.pallas_call(
        body, name="pair_share", in_specs=[ANY] * n, out_specs=[ANY] * n,
        out_shape=[jax.ShapeDtypeStruct(t.shape, t.dtype) for t in totals],
        scratch_shapes=[pltpu.SemaphoreType.DMA((n,)), pltpu.SemaphoreType.DMA((n,))],
    )(*totals)


def _adamw_math(w, g, m, v):
    m = ADAM_B1 * m + (1.0 - ADAM_B1) * g
    v = ADAM_B2 * v + (1.0 - ADAM_B2) * (g * g)
    m_hat = m / (1.0 - ADAM_B1 ** ADAM_STEP)
    v_hat = v / (1.0 - ADAM_B2 ** ADAM_STEP)
    delta = -ADAM_LR * (m_hat / (jnp.sqrt(v_hat) + ADAM_EPS) + ADAM_WD * w)
    return delta, m, v


def _adamw(c_idx, w, mine, theirs, m, v, nb, name):
    rows, cols = w.shape
    tr = rows // (2 * nb)

    def body(c_ref, w_ref, a_ref, b_ref, m_ref, v_ref, g_ref, d_ref, nm_ref, nv_ref):
        g = jnp.where(pl.program_id(0) == c_ref[0], a_ref[...], b_ref[...])
        g_ref[...] = g
        d_ref[...], nm_ref[...], nv_ref[...] = _adamw_math(w_ref[...], g, m_ref[...], v_ref[...])

    full = pl.BlockSpec((tr, cols), lambda hh, i, cr: (hh * nb + i, 0))
    half = pl.BlockSpec((tr, cols), lambda hh, i, cr: (i, 0))
    gs = pltpu.PrefetchScalarGridSpec(num_scalar_prefetch=1, grid=(2, nb), in_specs=[full, half, half, full, full],
                                      out_specs=[full] * 4)
    return pl.pallas_call(
        body, name=name, grid_spec=gs, out_shape=[jax.ShapeDtypeStruct((rows, cols), F32)] * 4,
        compiler_params=_params(("arbitrary", "arbitrary")),
    )(c_idx, w, mine, theirs, m, v)


def _small_update(gathered, w, m, v):
    def body(gg_ref, w_ref, m_ref, v_ref, g_ref, d_ref, nm_ref, nv_ref):
        g = gg_ref[0]
        for dev in range(1, 8):
            g = g + gg_ref[dev]
        g_ref[...] = g
        d_ref[...], nm_ref[...], nv_ref[...] = _adamw_math(w_ref[...], g, m_ref[...], v_ref[...])

    return pl.pallas_call(body, name="small_update", out_shape=[jax.ShapeDtypeStruct(w.shape, F32)] * 4,
                          compiler_params=_params())(gathered, w, m, v)


SMALL = ("g_mix_pre", "g_mix_post", "g_ffn_pre", "g_ffn_post")


def _pack_small(t, conv_full):
    conv = jnp.pad(conv_full.reshape(1, 3 * CONV_W), ((0, 0), (0, 2048 - 3 * CONV_W))).reshape(2, 1024)
    return jnp.concatenate([t[n].reshape(1, 1024) for n in SMALL]
                           + [jnp.concatenate([t["g_attn_out"].reshape(1, 512), t["g_conv_out"].reshape(1, 512)], axis=1),
                              conv, jnp.pad(t["b_forget"].reshape(1, N_HEADS), ((0, 0), (0, 1024 - N_HEADS)))], axis=0)


def _unpack_small(p, me):
    out = {n: p[i].reshape(1, 1024) for i, n in enumerate(SMALL)}
    out["g_attn_out"] = p[4, :512].reshape(1, 512)
    out["g_conv_out"] = p[4, 512:].reshape(1, 512)
    conv = p[5:7].reshape(2048)[:3 * CONV_W].reshape(3, CONV_W)
    out["conv_w"] = lax.dynamic_slice_in_dim(conv, me * 128, 128, axis=1).reshape(1, 3, 128)
    out["b_forget"] = p[7, :N_HEADS].reshape(1, N_HEADS)
    return out


def _conv_in_place(shard, me):
    return lax.dynamic_update_slice_in_dim(jnp.zeros((3, CONV_W), F32), shard, me * 128, axis=1)


def kernel(x, w_in, b_forget, conv_w, g_attn_out, g_conv_out, w_out, g_mix_pre, g_mix_post, w_gate_up, w_down, g_ffn_pre, g_ffn_post, loss_target, m_w_in, m_b_forget, m_conv_w, m_g_attn_out, m_g_conv_out, m_w_out, m_g_mix_pre, m_g_mix_post, m_w_gate_up, m_w_down, m_g_ffn_pre, m_g_ffn_post, v_w_in, v_b_forget, v_conv_w, v_g_attn_out, v_g_conv_out, v_w_out, v_g_mix_pre, v_g_mix_post, v_w_gate_up, v_w_down, v_g_ffn_pre, v_g_ffn_post):
    w = dict(w_in=w_in, b_forget=b_forget, conv_w=conv_w, g_attn_out=g_attn_out, g_conv_out=g_conv_out, w_out=w_out,
             g_mix_pre=g_mix_pre, g_mix_post=g_mix_post, w_gate_up=w_gate_up, w_down=w_down, g_ffn_pre=g_ffn_pre,
             g_ffn_post=g_ffn_post)
    m = dict(w_in=m_w_in, b_forget=m_b_forget, conv_w=m_conv_w, g_attn_out=m_g_attn_out, g_conv_out=m_g_conv_out,
             w_out=m_w_out, g_mix_pre=m_g_mix_pre, g_mix_post=m_g_mix_post, w_gate_up=m_w_gate_up, w_down=m_w_down,
             g_ffn_pre=m_g_ffn_pre, g_ffn_post=m_g_ffn_post)
    v = dict(w_in=v_w_in, b_forget=v_b_forget, conv_w=v_conv_w, g_attn_out=v_g_attn_out, g_conv_out=v_g_conv_out,
             w_out=v_w_out, g_mix_pre=v_g_mix_pre, g_mix_post=v_g_mix_post, w_gate_up=v_w_gate_up, w_down=v_w_down,
             g_ffn_pre=v_g_ffn_pre, g_ffn_post=v_g_ffn_post)
    cx, cy, cc = lax.axis_index("x"), lax.axis_index("y"), lax.axis_index("c")
    me = 2 * cx + cy
    c_idx = cc.astype(jnp.int32).reshape(1)
    me_idx = me.astype(jnp.int32).reshape(1)

    loss_local, grad_x, totals, shared, small_all = _device_step(x[0], loss_target[0], w, c_idx, me_idx)
    loss = 0.5 * lax.psum(loss_local, ("x", "y", "c"))

    gsum, delta, new_m, new_v = {}, {}, {}, {}
    for n, mine, theirs in zip(BIG, totals, shared):
        gs, d, nm, nv = _adamw(c_idx, w[n][0], mine, theirs, m[n][0], v[n][0], 2, "adamw_" + n)
        gsum[n], delta[n], new_m[n], new_v[n] = gs[None], d[None], nm[None], nv[None]
    packed = [_pack_small(t, _conv_in_place(t["conv_w"][0], me)) for t in (w, m, v)]
    g_small, d_small, m_small, v_small = _small_update(small_all, *packed)
    gsum.update(_unpack_small(g_small, me))
    delta.update(_unpack_small(d_small, me))
    new_m.update(_unpack_small(m_small, me))
    new_v.update(_unpack_small(v_small, me))

    order = ("w_in", "b_forget", "conv_w", "g_attn_out", "g_conv_out", "w_out", "g_mix_pre", "g_mix_post",
             "w_gate_up", "w_down", "g_ffn_pre", "g_ffn_post")
    return (loss, grad_x[None], *[gsum[n] for n in order], *[delta[n] for n in order],
            *[new_m[n] for n in order], *[new_v[n] for n in order])
```

```python
import functools

import jax
import jax.numpy as jnp
import numpy as np
from jax import lax
from jax.experimental import pallas as pl
from jax.experimental.pallas import tpu as pltpu

F32 = jnp.float32
BF16 = jnp.bfloat16
MXU_DTYPE = jnp.bfloat16

D_MODEL = 1024
HEAD_DIM = 64
N_HEADS = 8
ATTN_W = 512
CONV_W = 512
D_FF = 2816
FF_PIECE = 1408
EPS = 1e-6
Q_SCALE = HEAD_DIM ** -0.5

OFF_F = 1536
OFF_B = 1664
OFF_C = 2176
OFF_U = 2688
IN_PAD = 3200
IN_W = 3080
N_CHIPS = 4

ADAM_LR = 0.001
ADAM_B1 = 0.9
ADAM_B2 = 0.999
ADAM_EPS = 1e-08
ADAM_WD = 0.01
ADAM_STEP = 10

VMEM_LIMIT_V7X = 56 * 1024 * 1024
MESH_ID = pl.DeviceIdType.MESH


def _params(sem=None, vmem=VMEM_LIMIT_V7X):
    kw = {"vmem_limit_bytes": vmem}
    if sem is not None:
        kw["dimension_semantics"] = sem
    return pltpu.CompilerParams(**kw)


def _dot(a, b):
    return jnp.dot(a, b, preferred_element_type=F32)


def _dot_nt(a, b):
    return lax.dot_general(a, b, (((1,), (1,)), ((), ())), preferred_element_type=F32)


def _dot_exact(x, ones, parts):
    if ones.dtype == F32:
        return _dot(x, ones)
    acc = None
    rem = x
    for _ in range(parts):
        piece = rem.astype(BF16)
        rem = rem - piece.astype(F32)
        term = _dot(piece, ones)
        acc = term if acc is None else acc + term
    return acc


def _rms(v):
    return lax.rsqrt(jnp.mean(v * v, axis=-1, keepdims=True) + EPS)


def _tok(tm, w):
    return pl.BlockSpec((tm, w), lambda i: (i, 0))


def _whole(shape, single=False):
    nd = len(shape)
    if single:
        return pl.BlockSpec(shape, lambda i: (0,) * nd, pipeline_mode=pl.Buffered(1))
    return pl.BlockSpec(shape, lambda i: (0,) * nd)


def _feat(rows, tm):
    return pl.BlockSpec((rows, tm), lambda i: (0, i))


def _inproj_fwd(x, g_pre, w_pad, w_kvz_t, tm):
    s = x.shape[0]

    def body(x_ref, g_ref, w_ref, wt_ref, h_ref, q_ref, k_ref, v_ref, kt_ref, vt_ref, zt_ref, b_ref, c_ref, u_ref):
        xv = x_ref[...]
        h = ((xv * _rms(xv)) * g_ref[...]).astype(MXU_DTYPE)
        h_ref[...] = h

        def proj(lo, hi):
            return _dot(h, w_ref[:, lo:hi])

        q_ref[...] = (proj(0, 512) * Q_SCALE).astype(MXU_DTYPE)
        kt = _dot_nt(wt_ref[0:512, :], h)
        vt = _dot_nt(wt_ref[512:1024, :], h)
        kt_ref[...] = kt.astype(MXU_DTYPE)
        vt_ref[...] = vt.astype(MXU_DTYPE)
        k_ref[...] = kt.T.astype(MXU_DTYPE)
        v_ref[...] = vt.T.astype(MXU_DTYPE)
        zt_ref[...] = _dot_nt(wt_ref[1024:1152, :], h)
        b_ref[...] = proj(OFF_B, OFF_C)
        c_ref[...] = proj(OFF_C, OFF_U)
        u_ref[...] = proj(OFF_U, IN_PAD)

    sd = jax.ShapeDtypeStruct
    return pl.pallas_call(
        body, name="inproj_fwd", grid=(s // tm,),
        in_specs=[_tok(tm, D_MODEL), _whole((1, D_MODEL)), _whole((D_MODEL, IN_PAD), single=True),
                  _whole((1152, D_MODEL), single=True)],
        out_specs=[_tok(tm, D_MODEL), _tok(tm, 512), _tok(tm, 512), _tok(tm, 512), _feat(512, tm), _feat(512, tm),
                   _feat(128, tm), _tok(tm, 512), _tok(tm, 512), _tok(tm, 512)],
        out_shape=[sd((s, D_MODEL), MXU_DTYPE), sd((s, 512), MXU_DTYPE), sd((s, 512), MXU_DTYPE),
                   sd((s, 512), MXU_DTYPE), sd((512, s), MXU_DTYPE), sd((512, s), MXU_DTYPE), sd((128, s), F32),
                   sd((s, 512), F32), sd((s, 512), F32), sd((s, 512), F32)],
        compiler_params=_params(("arbitrary",)),
    )(x, g_pre, w_pad, w_kvz_t)


def _tri(n, upper):
    r = lax.broadcasted_iota(jnp.int32, (n, n), 0)
    c = lax.broadcasted_iota(jnp.int32, (n, n), 1)
    return ((r <= c) if upper else (r >= c)).astype(MXU_DTYPE)


HEAD_ROWS = 16


def _rows_to_cols(v):
    return jnp.concatenate([v, jnp.zeros((128 - HEAD_ROWS, 128), F32)], axis=0).T


def _forget_fwd(z_t, b_col):
    s = z_t.shape[1]
    nb = s // 128

    def body(z_ref, b_ref, c_ref, cc_ref):
        upper = _tri(128, True)

        def blk(n, carry):
            off = pl.multiple_of(n * 128, 128)
            lf = jax.nn.log_sigmoid(z_ref[0:HEAD_ROWS, pl.ds(off, 128)] + b_ref[...])
            cs = _dot_exact(lf, upper, 3) + carry
            c_ref[:, pl.ds(off, 128)] = cs
            cc_ref[pl.ds(off, 128), :] = _rows_to_cols(cs)
            return carry + jnp.sum(lf, axis=1, keepdims=True)

        lax.fori_loop(0, nb, blk, jnp.zeros((HEAD_ROWS, 1), F32))

    return pl.pallas_call(body, name="forget_fwd",
                          out_shape=[jax.ShapeDtypeStruct((HEAD_ROWS, s), F32), jax.ShapeDtypeStruct((s, 128), F32)],
                          compiler_params=_params())(z_t, b_col)


def _pair_lanes(pp):
    return pl.ds(pl.multiple_of(pp * 128, 128), 128)


def _head_rows(h):
    return pl.ds(pl.multiple_of(h * HEAD_DIM, HEAD_DIM), HEAD_DIM)


def _only_head(block, hb):
    lane = lax.broadcasted_iota(jnp.int32, block.shape, 1)
    return jnp.where((lane >= HEAD_DIM) if hb else (lane < HEAD_DIM), block, jnp.zeros_like(block))


def _head_col(cols, h):
    lane = lax.broadcasted_iota(jnp.int32, cols.shape, 1)
    return jnp.sum(jnp.where(lane == h, cols, 0.0), axis=1, keepdims=True)


def _attn_fwd(qs, k, v_t, c_rows, c_cols, t, shards):
    s = qs.shape[0]
    n = s // t
    pairs = [(i, j) for i in range(n) for j in range(i + 1)]
    it = jnp.asarray(np.array([p[0] for p in pairs], np.int32))
    jt = jnp.asarray(np.array([p[1] for p in pairs], np.int32))
    nw = len(shards)
    last = len(pairs) - 1
    mid = (2 * len(pairs)) // 3

    def body(it_ref, jt_ref, q_ref, k_ref, vt_ref, cq_ref, ck_ref, *rest):
        sh, (o_ref, lse_ref), got = rest[:nw], rest[nw:nw + 2], rest[nw + 2:2 * nw + 2]
        m_sc, l_sc, acc_sc, send, recv = rest[2 * nw + 2:]
        p = pl.program_id(0)
        i = it_ref[p]
        j = jt_ref[p]
        gather_start, gather_forward, gather_finish = _gather_stages(sh, got, send, recv)
        pl.when(p == 0)(gather_start)
        if mid < last:
            pl.when(p == mid)(gather_forward)

        @pl.when(j == 0)
        def _():
            m_sc[...] = jnp.full_like(m_sc, -1e30)
            l_sc[...] = jnp.ones_like(l_sc)
            acc_sc[...] = jnp.zeros_like(acc_sc)

        def pair_step(pp, diagonal):
            lanes = _pair_lanes(pp)
            kp = k_ref[:, lanes]
            qp = q_ref[:, lanes]
            ck_all = ck_ref[...]
            for hb in range(2):
                h = 2 * pp + hb
                row = pl.ds(h, 1)
                st = _dot_nt(_only_head(kp, hb), qp) + (cq_ref[row, :] - _head_col(ck_all, h))
                if diagonal:
                    kpos = lax.broadcasted_iota(jnp.int32, (t, t), 0)
                    qpos = lax.broadcasted_iota(jnp.int32, (t, t), 1)
                    st = jnp.where(kpos <= qpos, st, -1e30)
                m_prev = m_sc[row, :]
                m_new = jnp.maximum(m_prev, jnp.max(st, axis=0, keepdims=True))
                alpha = jnp.exp(m_prev - m_new)
                pt = jnp.exp(st - m_new)
                l_sc[row, :] = alpha * l_sc[row, :] + jnp.sum(pt, axis=0, keepdims=True)
                rows = _head_rows(h)
                acc_sc[rows, :] = acc_sc[rows, :] * alpha + _dot(vt_ref[rows, :], pt.astype(MXU_DTYPE))
                m_sc[row, :] = m_new

        @pl.when(j < i)
        def _():
            @pl.loop(0, N_HEADS // 2)
            def _(pp):
                pair_step(pp, False)

        @pl.when(j == i)
        def _():
            @pl.loop(0, N_HEADS // 2)
            def _(pp):
                pair_step(pp, True)
                sub = lax.broadcasted_iota(jnp.int32, (2 * HEAD_DIM, t), 0)
                l_pair = jnp.where(sub < HEAD_DIM, l_sc[pl.ds(2 * pp, 1), :], l_sc[pl.ds(2 * pp + 1, 1), :])
                o_t = acc_sc[pl.ds(pl.multiple_of(pp * 2 * HEAD_DIM, 2 * HEAD_DIM), 2 * HEAD_DIM), :] / l_pair
                o_ref[:, _pair_lanes(pp)] = o_t.T

            lse_ref[...] = m_sc[...] + jnp.log(l_sc[...])

        @pl.when(p == last)
        def _():
            if mid >= last:
                gather_forward()
            gather_finish()

    gs = pltpu.PrefetchScalarGridSpec(
        num_scalar_prefetch=2, grid=(len(pairs),),
        in_specs=[pl.BlockSpec((t, ATTN_W), lambda p, it_, jt_: (it_[p], 0)),
                  pl.BlockSpec((t, ATTN_W), lambda p, it_, jt_: (jt_[p], 0)),
                  pl.BlockSpec((ATTN_W, t), lambda p, it_, jt_: (0, jt_[p])),
                  pl.BlockSpec((HEAD_ROWS, t), lambda p, it_, jt_: (0, it_[p])),
                  pl.BlockSpec((t, 128), lambda p, it_, jt_: (jt_[p], 0))] + [ANY] * nw,
        out_specs=[pl.BlockSpec((t, ATTN_W), lambda p, it_, jt_: (it_[p], 0)),
                   pl.BlockSpec((HEAD_ROWS, t), lambda p, it_, jt_: (0, it_[p]))] + [ANY] * nw,
        scratch_shapes=[pltpu.VMEM((HEAD_ROWS, t), F32), pltpu.VMEM((HEAD_ROWS, t), F32), pltpu.VMEM((ATTN_W, t), F32),
                        pltpu.SemaphoreType.DMA((nw, 6)), pltpu.SemaphoreType.DMA((nw, 6))])
    o, lse, *got = pl.pallas_call(
        body, name="attn_fwd", grid_spec=gs,
        out_shape=[jax.ShapeDtypeStruct((s, ATTN_W), F32), jax.ShapeDtypeStruct((HEAD_ROWS, s), F32)]
        + [jax.ShapeDtypeStruct((N_CHIPS,) + a.shape, a.dtype) for a in shards],
        compiler_params=_params(("arbitrary",)),
    )(it, jt, qs, k, v_t, c_rows, c_cols, *shards)
    me = 2 * lax.axis_index("x") + lax.axis_index("y")
    return o, lse, [lax.dynamic_update_index_in_dim(g, own, me, 0) for g, own in zip(got, shards)]


def _shift_down(cur, prev_ref, first):
    row = lax.broadcasted_iota(jnp.int32, cur.shape, 0)
    p7 = jnp.where(first, 0.0, prev_ref[0][7:8, :] * prev_ref[1][7:8, :])
    p6 = jnp.where(first, 0.0, prev_ref[0][6:7, :] * prev_ref[1][6:7, :])
    s1 = jnp.where(row == 0, p7, pltpu.roll(cur, 1, 0))
    s2 = jnp.where(row == 0, p6, jnp.where(row == 1, p7, pltpu.roll(cur, 2, 0)))
    return s1, s2


def _group_ms(v, gmat):
    return _dot_exact(v, gmat, 2) * (1.0 / HEAD_DIM)


def _mixer_fwd(x, o_attn, gate_b, gate_c, u, conv_w, g_attn, g_conv, w_out, g_post, gmat, tm):
    s = x.shape[0]

    def body(x_ref, o_ref, b_ref, c_ref, u_ref, cp_ref, up_ref, cw_ref, ga_ref, gc_ref, wo_ref, gp_ref, gm_ref,
             x2_ref, mg_ref, y_ref, z_ref):
        i = pl.program_id(0)
        cu = c_ref[...] * u_ref[...]
        cu1, cu2 = _shift_down(cu, (cp_ref, up_ref), i == 0)
        z = cw_ref[0:1, :] * cu2 + cw_ref[1:2, :] * cu1 + cw_ref[2:3, :] * cu
        z_ref[...] = z
        cv = b_ref[...] * z
        ov = o_ref[...]
        gm = gm_ref[...]
        ma = ((ov * lax.rsqrt(_group_ms(ov * ov, gm) + EPS)) * ga_ref[...]).astype(MXU_DTYPE)
        mc = ((cv * lax.rsqrt(_group_ms(cv * cv, gm) + EPS)) * gc_ref[...]).astype(MXU_DTYPE)
        mg_ref[:, 0:ATTN_W] = ma
        mg_ref[:, ATTN_W:D_MODEL] = mc
        y = _dot(ma, wo_ref[0:ATTN_W, :]) + _dot(mc, wo_ref[ATTN_W:D_MODEL, :])
        y_ref[...] = y
        x2_ref[...] = x_ref[...] + (y * _rms(y)) * gp_ref[...]

    halo = pl.BlockSpec((8, 512), lambda i: (jnp.maximum(i * (tm // 8) - 1, 0), 0))
    sd = jax.ShapeDtypeStruct
    return pl.pallas_call(
        body, name="mixer_fwd", grid=(s // tm,),
        in_specs=[_tok(tm, D_MODEL), _tok(tm, 512), _tok(tm, 512), _tok(tm, 512), _tok(tm, 512), halo, halo,
                  _whole((3, 512)), _whole((1, 512)), _whole((1, 512)), _whole((D_MODEL, D_MODEL), single=True),
                  _whole((1, D_MODEL)), _whole((512, 512))],
        out_specs=[_tok(tm, D_MODEL), _tok(tm, D_MODEL), _tok(tm, D_MODEL), _tok(tm, 512)],
        out_shape=[sd((s, D_MODEL), F32), sd((s, D_MODEL), MXU_DTYPE), sd((s, D_MODEL), F32), sd((s, 512), F32)],
        compiler_params=_params(("arbitrary",)),
    )(x, o_attn, gate_b, gate_c, u, gate_c, u, conv_w, g_attn, g_conv, w_out, g_post, gmat)


def _ffn_fwd(x2, target, g_pre, w_gu, w_dn, g_post, tm):
    s = x2.shape[0]

    def body(x_ref, t_ref, gpre_ref, wgu_ref, wdn_ref, gpost_ref,
             h_ref, g_ref, up_ref, a_ref, ff_ref, dout_ref, loss_ref):
        xv = x_ref[...]
        h = ((xv * _rms(xv)) * gpre_ref[...]).astype(MXU_DTYPE)
        h_ref[...] = h
        ff = jnp.zeros((tm, D_MODEL), F32)
        for j in range(2):
            cols = slice(j * FF_PIECE, (j + 1) * FF_PIECE)
            g = _dot(h, wgu_ref[j])
            up = _dot(h, wgu_ref[2 + j])
            a = ((g * jax.nn.sigmoid(g)) * up).astype(MXU_DTYPE)
            g_ref[:, cols] = g
            up_ref[:, cols] = up
            a_ref[:, cols] = a
            ff = ff + _dot(a, wdn_ref[j])
        ff_ref[...] = ff
        err = (xv + (ff * _rms(ff)) * gpost_ref[...]) - t_ref[...]
        dout_ref[...] = err * (1.0 / D_MODEL)
        part = jnp.sum(jnp.mean(err * err, axis=-1, keepdims=True), axis=0, keepdims=True)

        @pl.when(pl.program_id(0) == 0)
        def _():
            loss_ref[...] = jnp.zeros_like(loss_ref)

        loss_ref[...] += part

    sd = jax.ShapeDtypeStruct
    return pl.pallas_call(
        body, name="ffn_fwd", grid=(s // tm,),
        in_specs=[_tok(tm, D_MODEL), _tok(tm, D_MODEL), _whole((1, D_MODEL)),
                  _whole((4, D_MODEL, FF_PIECE), single=True), _whole((2, FF_PIECE, D_MODEL), single=True),
                  _whole((1, D_MODEL))],
        out_specs=[_tok(tm, D_MODEL), _tok(tm, D_FF), _tok(tm, D_FF), _tok(tm, D_FF), _tok(tm, D_MODEL),
                   _tok(tm, D_MODEL), _whole((8, 128))],
        out_shape=[sd((s, D_MODEL), MXU_DTYPE), sd((s, D_FF), F32), sd((s, D_FF), F32), sd((s, D_FF), MXU_DTYPE),
                   sd((s, D_MODEL), F32), sd((s, D_MODEL), F32), sd((8, 128), F32)],
        compiler_params=_params(("arbitrary",)),
    )(x2, target, g_pre, w_gu, w_dn, g_post)


def _norm_bwd(dy, normed, rinv, gain):
    t = dy * gain
    return rinv * (t - normed * jnp.mean(t * normed, axis=-1, keepdims=True))


def _acc_rows(ref, first, val):
    @pl.when(first)
    def _():
        ref[...] = jnp.zeros_like(ref)

    ref[...] += jnp.sum(val, axis=0, keepdims=True)


def _ffn_bwd(dout, ff, x2, g, up, g_post, g_pre, w_gu, w_dn, tm):
    s = x2.shape[0]

    def body(do_ref, ff_ref, x_ref, g_ref, up_ref, gpost_ref, gpre_ref, wgu_ref, wdn_ref,
             dx_ref, dff_ref, dgu_ref, dgpost_ref, dgpre_ref):
        first = pl.program_id(0) == 0
        ffv = ff_ref[...]
        rf = _rms(ffv)
        n = ffv * rf
        do = do_ref[...]
        _acc_rows(dgpost_ref, first, do * n)
        dff = _norm_bwd(do, n, rf, gpost_ref[...]).astype(MXU_DTYPE)
        dff_ref[...] = dff
        dh = jnp.zeros((tm, D_MODEL), F32)
        for j in range(2):
            cols = slice(j * FF_PIECE, (j + 1) * FF_PIECE)
            da = _dot_nt(dff, wdn_ref[j])
            gv = g_ref[:, cols]
            sg = jax.nn.sigmoid(gv)
            dg = (da * up_ref[:, cols] * (sg * (1.0 + gv * (1.0 - sg)))).astype(MXU_DTYPE)
            du = (da * (gv * sg)).astype(MXU_DTYPE)
            dgu_ref[:, cols] = dg
            dgu_ref[:, D_FF + j * FF_PIECE:D_FF + (j + 1) * FF_PIECE] = du
            dh = dh + _dot_nt(dg, wgu_ref[j]) + _dot_nt(du, wgu_ref[2 + j])
        xv = x_ref[...]
        r2 = _rms(xv)
        nx = xv * r2
        _acc_rows(dgpre_ref, first, dh * nx)
        dx_ref[...] = do + _norm_bwd(dh, nx, r2, gpre_ref[...])

    sd = jax.ShapeDtypeStruct
    return pl.pallas_call(
        body, name="ffn_bwd", grid=(s // tm,),
        in_specs=[_tok(tm, D_MODEL), _tok(tm, D_MODEL), _tok(tm, D_MODEL), _tok(tm, D_FF), _tok(tm, D_FF),
                  _whole((1, D_MODEL)), _whole((1, D_MODEL)),
                  _whole((4, D_MODEL, FF_PIECE), single=True), _whole((2, FF_PIECE, D_MODEL), single=True)],
        out_specs=[_tok(tm, D_MODEL), _tok(tm, D_MODEL), _tok(tm, 2 * D_FF), _whole((1, D_MODEL)),
                   _whole((1, D_MODEL))],
        out_shape=[sd((s, D_MODEL), F32), sd((s, D_MODEL), MXU_DTYPE), sd((s, 2 * D_FF), MXU_DTYPE),
                   sd((1, D_MODEL), F32), sd((1, D_MODEL), F32)],
        compiler_params=_params(("arbitrary",)),
    )(dout, ff, x2, g, up, g_post, g_pre, w_gu, w_dn)


def _tn_matmul(a, b, tm, tn, tk, name):
    s, m = a.shape
    n = b.shape[1]

    def body(a_ref, b_ref, o_ref):
        @pl.when(pl.program_id(2) == 0)
        def _():
            o_ref[...] = jnp.zeros_like(o_ref)

        o_ref[...] += lax.dot_general(a_ref[...], b_ref[...], (((0,), (0,)), ((), ())), preferred_element_type=F32)

    return pl.pallas_call(
        body, name=name, grid=(m // tm, n // tn, s // tk),
        in_specs=[pl.BlockSpec((tk, tm), lambda i, j, kk: (kk, i)), pl.BlockSpec((tk, tn), lambda i, j, kk: (kk, j))],
        out_specs=pl.BlockSpec((tm, tn), lambda i, j, kk: (i, j)),
        out_shape=jax.ShapeDtypeStruct((m, n), F32),
        compiler_params=_params(("arbitrary", "arbitrary", "arbitrary")),
    )(a, b)


def _mixer_bwd(dx2, y, o_attn, gate_b, z, g_post, g_attn, g_conv, w_out, gmat, sel, tm, ready, kinds):
    s = dx2.shape[0]
    nw = len(ready)
    nt = s // tm

    def body(d_ref, y_ref, o_ref, b_ref, z_ref, gp_ref, ga_ref, gc_ref, wo_ref, gm_ref, sel_ref, *rest):
        grads = rest[:nw]
        dy_ref, do_ref, db_ref, dz_ref, delta_ref, dgp_ref, dga_ref, dgc_ref = rest[nw:nw + 8]
        taken = rest[nw + 8:2 * nw + 8]
        send, recv = rest[2 * nw + 8:]
        first = pl.program_id(0) == 0
        pair_start, pair_finish = _pair_stages(grads, kinds, taken, send, recv)
        pl.when(first)(pair_start)
        yv = y_ref[...]
        ry = _rms(yv)
        ny = yv * ry
        d = d_ref[...]
        _acc_rows(dgp_ref, first, d * ny)
        dy = _norm_bwd(d, ny, ry, gp_ref[...]).astype(MXU_DTYPE)
        dy_ref[...] = dy
        dm = _dot_nt(dy, wo_ref[...])
        gm = gm_ref[...]

        def group_bwd(val, dmv, gain, dg_ref):
            rg = lax.rsqrt(_group_ms(val * val, gm) + EPS)
            nv = val * rg
            _acc_rows(dg_ref, first, dmv * nv)
            t = dmv * gain
            return rg * (t - nv * _group_ms(t * nv, gm))

        ov = o_ref[...]
        d_o = group_bwd(ov, dm[:, 0:ATTN_W], ga_ref[...], dga_ref)
        do_ref[...] = d_o.astype(MXU_DTYPE)
        delta_ref[...] = _dot_exact(d_o * ov, sel_ref[...], 2).T[0:HEAD_ROWS, :]
        zv = z_ref[...]
        bv = b_ref[...]
        d_cv = group_bwd(bv * zv, dm[:, ATTN_W:D_MODEL], gc_ref[...], dgc_ref)
        db_ref[...] = d_cv * zv
        dz_ref[...] = d_cv * bv
        pl.when(pl.program_id(0) == nt - 1)(pair_finish)

    sd = jax.ShapeDtypeStruct
    taken_shape = [sd((N_CHIPS, g.shape[-2], g.shape[-1] if kd == "rows" else g.shape[-1] // N_CHIPS), F32)
                   for g, kd in zip(ready, kinds)]
    out = pl.pallas_call(
        body, name="mixer_bwd", grid=(nt,),
        in_specs=[_tok(tm, D_MODEL), _tok(tm, D_MODEL), _tok(tm, 512), _tok(tm, 512), _tok(tm, 512),
                  _whole((1, D_MODEL)), _whole((1, 512)), _whole((1, 512)),
                  _whole((D_MODEL, D_MODEL), single=True), _whole((512, 512)), _whole((512, 128))] + [ANY] * nw,
        out_specs=[_tok(tm, D_MODEL), _tok(tm, 512), _tok(tm, 512), _tok(tm, 512), _feat(HEAD_ROWS, tm),
                   _whole((1, D_MODEL)), _whole((1, 512)), _whole((1, 512))] + [ANY] * nw,
        out_shape=[sd((s, D_MODEL), MXU_DTYPE), sd((s, 512), MXU_DTYPE), sd((s, 512), F32), sd((s, 512), F32),
                   sd((HEAD_ROWS, s), F32), sd((1, D_MODEL), F32), sd((1, 512), F32), sd((1, 512), F32)] + taken_shape,
        scratch_shapes=[pltpu.SemaphoreType.DMA((nw, N_CHIPS)), pltpu.SemaphoreType.DMA((nw, N_CHIPS))],
        compiler_params=_params(("arbitrary",)),
    )(dx2, y, o_attn, gate_b, z, g_post, g_attn, g_conv, w_out, gmat, sel, *ready)
    return out[:8], out[8:]


def _attn_bwd(qs, k, k_t, v, do, c_rows, c_cols, lse, delta, t, parts):
    s = qs.shape[0]
    n = s // t
    pairs = [(i, j) for j in range(n) for i in range(j, n)]
    it = jnp.asarray(np.array([p[0] for p in pairs], np.int32))
    jt = jnp.asarray(np.array([p[1] for p in pairs], np.int32))

    nw = len(parts)

    def body(it_ref, jt_ref, q_ref, k_ref, kt_ref, v_ref, do_ref, cq_ref, ck_ref, lse_ref, dl_ref, *rest):
        pb = rest[:nw]
        dq_ref, dk_ref, dv_ref, dc_ref, dcq_ref = rest[nw:nw + 5]
        rcv = rest[nw + 5:2 * nw + 5]
        dk_sc, dv_sc, dc_sc, send, recv = rest[2 * nw + 5:]
        p = pl.program_id(0)
        i = it_ref[p]
        j = jt_ref[p]
        chip_start, chip_finish = _chip_stages(pb, rcv, send, recv)

        @pl.when(p == 0)
        def _():
            chip_start()
            dq_ref[...] = jnp.zeros_like(dq_ref)
            dcq_ref[...] = jnp.zeros_like(dcq_ref)

        @pl.when(i == j)
        def _():
            dk_sc[...] = jnp.zeros_like(dk_sc)
            dv_sc[...] = jnp.zeros_like(dv_sc)
            dc_sc[...] = jnp.zeros_like(dc_sc)

        def pair_step(pp, diagonal):
            lanes = _pair_lanes(pp)
            qp = q_ref[:, lanes]
            kp = k_ref[:, lanes]
            vp = v_ref[:, lanes]
            dop = do_ref[:, lanes]
            ck_all = ck_ref[...]
            lane = lax.broadcasted_iota(jnp.int32, (t, 128), 1)
            for hb in range(2):
                h = 2 * pp + hb
                row = pl.ds(h, 1)
                bias = (cq_ref[row, :] - lse_ref[row, :]) - _head_col(ck_all, h)
                pt = jnp.exp(_dot_nt(_only_head(kp, hb), qp) + bias)
                if diagonal:
                    kpos = lax.broadcasted_iota(jnp.int32, (t, t), 0)
                    qpos = lax.broadcasted_iota(jnp.int32, (t, t), 1)
                    pt = jnp.where(kpos <= qpos, pt, 0.0)
                dv_sc[:, lanes] += _dot(pt.astype(MXU_DTYPE), _only_head(dop, hb))
                dst = pt * (_dot_nt(_only_head(vp, hb), dop) - dl_ref[row, :])
                dc_sc[...] -= jnp.where(lane == h, jnp.sum(dst, axis=1, keepdims=True), 0.0)
                dcq_ref[i, row, :] += jnp.sum(dst, axis=0, keepdims=True)
                dsb = dst.astype(MXU_DTYPE)
                dk_sc[:, lanes] += _dot(dsb, _only_head(qp, hb))
                rows = _head_rows(h)
                dq_ref[i, rows, :] += _dot(kt_ref[rows, :], dsb)

        @pl.when(i > j)
        def _():
            @pl.loop(0, N_HEADS // 2)
            def _(pp):
                pair_step(pp, False)

        @pl.when(i == j)
        def _():
            @pl.loop(0, N_HEADS // 2)
            def _(pp):
                pair_step(pp, True)

        @pl.when(i == n - 1)
        def _():
            dk_ref[...] = dk_sc[...]
            dv_ref[...] = dv_sc[...]
            dc_ref[...] = dc_sc[...]

        pl.when(p == len(pairs) - 1)(chip_finish)

    qi = lambda p, it_, jt_: (it_[p], 0)
    kj = lambda p, it_, jt_: (jt_[p], 0)
    row_i = lambda p, it_, jt_: (0, it_[p])
    gs = pltpu.PrefetchScalarGridSpec(
        num_scalar_prefetch=2, grid=(len(pairs),),
        in_specs=[pl.BlockSpec((t, ATTN_W), qi), pl.BlockSpec((t, ATTN_W), kj),
                  pl.BlockSpec((ATTN_W, t), lambda p, it_, jt_: (0, jt_[p])),
                  pl.BlockSpec((t, ATTN_W), kj), pl.BlockSpec((t, ATTN_W), qi),
                  pl.BlockSpec((HEAD_ROWS, t), row_i), pl.BlockSpec((t, 128), kj),
                  pl.BlockSpec((HEAD_ROWS, t), row_i), pl.BlockSpec((HEAD_ROWS, t), row_i)] + [ANY] * nw,
        out_specs=[pl.BlockSpec((n, ATTN_W, t), lambda p, it_, jt_: (0, 0, 0)),
                   pl.BlockSpec((t, ATTN_W), kj), pl.BlockSpec((t, ATTN_W), kj),
                   pl.BlockSpec((t, 128), kj),
                   pl.BlockSpec((n, HEAD_ROWS, t), lambda p, it_, jt_: (0, 0, 0))] + [ANY] * nw,
        scratch_shapes=[pltpu.VMEM((t, ATTN_W), F32), pltpu.VMEM((t, ATTN_W), F32),
                        pltpu.VMEM((t, 128), F32), pltpu.SemaphoreType.DMA((nw, 3)), pltpu.SemaphoreType.DMA((nw, 3))])
    sd = jax.ShapeDtypeStruct
    out = pl.pallas_call(
        body, name="attn_bwd", grid_spec=gs,
        out_shape=[sd((n, ATTN_W, t), F32), sd((s, ATTN_W), F32), sd((s, ATTN_W), F32),
                   sd((s, 128), F32), sd((n, HEAD_ROWS, t), F32)] + [sd((3,) + a.shape[1:], a.dtype) for a in parts],
        compiler_params=_params(("arbitrary",)),
    )(it, jt, qs, k, k_t, v, do, c_rows, c_cols, lse, delta, *parts)
    return out[:5], out[5:]


def _forget_bwd(dc_rows, dc_cols, z_t, b_col):
    s = z_t.shape[1]
    nb = s // 128

    def body(dr_ref, dcc_ref, z_ref, b_ref, dz_ref, db_ref):
        lower = _tri(128, False)
        real = lax.broadcasted_iota(jnp.int32, (HEAD_ROWS, 128), 0) < N_HEADS

        def blk(m, carry):
            tail, dbias = carry
            off = pl.multiple_of((nb - 1 - m) * 128, 128)
            dc = dr_ref[:, pl.ds(off, 128)] + dcc_ref[pl.ds(off, 128), :].T[0:HEAD_ROWS, :]
            dlf = _dot_exact(dc, lower, 3) + tail
            dz = dlf * jax.nn.sigmoid(-(z_ref[0:HEAD_ROWS, pl.ds(off, 128)] + b_ref[...]))
            dz = jnp.where(real, dz, 0.0)
            dz_ref[pl.ds(off, 128), :] = _rows_to_cols(dz)
            return tail + jnp.sum(dc, axis=1, keepdims=True), dbias + jnp.sum(dz, axis=1, keepdims=True)

        zero = jnp.zeros((HEAD_ROWS, 1), F32)
        _, dbias = lax.fori_loop(0, nb, blk, (zero, zero))
        db_ref[...] = jnp.broadcast_to(dbias, db_ref.shape)

    return pl.pallas_call(
        body, name="forget_bwd",
        out_shape=[jax.ShapeDtypeStruct((s, 128), F32), jax.ShapeDtypeStruct((HEAD_ROWS, 128), F32)],
        compiler_params=_params())(dc_rows, dc_cols, z_t, b_col)


def _inproj_bwd(dz, gate_c, u, conv_w, dq, dk, dv, dzf, db, x, dx2, g_pre, w_pad, tm):
    s = x.shape[0]
    nt = s // tm
    assert dq.shape == (nt, ATTN_W, tm)

    def body(dz_ref, dzn_ref, c_ref, u_ref, cp_ref, up_ref, cw_ref, dq_ref, dk_ref, dv_ref, dzf_ref, db_ref,
             x_ref, dx2_ref, g_ref, w_ref, gx_ref, dp_ref, dg_ref, dcw_ref):
        i = pl.program_id(0)
        first = i == 0
        last = i == nt - 1
        dzv = dz_ref[...]
        row = lax.broadcasted_iota(jnp.int32, dzv.shape, 0)
        n0 = jnp.where(last, 0.0, dzn_ref[0:1, :])
        n1 = jnp.where(last, 0.0, dzn_ref[1:2, :])
        dz1 = jnp.where(row == tm - 1, n0, pltpu.roll(dzv, tm - 1, 0))
        dz2 = jnp.where(row == tm - 1, n1, jnp.where(row == tm - 2, n0, pltpu.roll(dzv, tm - 2, 0)))
        dcu = cw_ref[2:3, :] * dzv + cw_ref[1:2, :] * dz1 + cw_ref[0:1, :] * dz2
        cv = c_ref[...]
        uv = u_ref[...]
        cu = cv * uv
        cu1, cu2 = _shift_down(cu, (cp_ref, up_ref), first)

        @pl.when(first)
        def _():
            dcw_ref[...] = jnp.zeros_like(dcw_ref)

        dcw_ref[0:1, :] += jnp.sum(dzv * cu2, axis=0, keepdims=True)
        dcw_ref[1:2, :] += jnp.sum(dzv * cu1, axis=0, keepdims=True)
        dcw_ref[2:3, :] += jnp.sum(dzv * cu, axis=0, keepdims=True)

        dp_ref[:, 0:512] = (dq_ref[0].T * Q_SCALE).astype(MXU_DTYPE)
        dp_ref[:, 512:1024] = dk_ref[...].astype(MXU_DTYPE)
        dp_ref[:, 1024:OFF_F] = dv_ref[...].astype(MXU_DTYPE)
        dp_ref[:, OFF_F:OFF_B] = dzf_ref[...].astype(MXU_DTYPE)
        dp_ref[:, OFF_B:OFF_C] = db_ref[...].astype(MXU_DTYPE)
        dp_ref[:, OFF_C:OFF_U] = (dcu * uv).astype(MXU_DTYPE)
        dp_ref[:, OFF_U:IN_PAD] = (dcu * cv).astype(MXU_DTYPE)
        dh = _dot_nt(dp_ref[...], w_ref[...])
        xv = x_ref[...]
        r1 = _rms(xv)
        nx = xv * r1
        _acc_rows(dg_ref, first, dh * nx)
        gx_ref[...] = dx2_ref[...] + _norm_bwd(dh, nx, r1, g_ref[...])

    prev = pl.BlockSpec((8, 512), lambda i: (jnp.maximum(i * (tm // 8) - 1, 0), 0))
    nxt = pl.BlockSpec((8, 512), lambda i: (jnp.minimum((i + 1) * (tm // 8), s // 8 - 1), 0))
    sd = jax.ShapeDtypeStruct
    return pl.pallas_call(
        body, name="inproj_bwd", grid=(nt,),
        in_specs=[_tok(tm, 512), nxt, _tok(tm, 512), _tok(tm, 512), prev, prev, _whole((3, 512)),
                  pl.BlockSpec((1, ATTN_W, tm), lambda i: (i, 0, 0)), _tok(tm, 512), _tok(tm, 512), _tok(tm, 128),
                  _tok(tm, 512),
                  _tok(tm, D_MODEL), _tok(tm, D_MODEL), _whole((1, D_MODEL)), _whole((D_MODEL, IN_PAD), single=True)],
        out_specs=[_tok(tm, D_MODEL), _tok(tm, IN_PAD), _whole((1, D_MODEL)), _whole((8, 512))],
        out_shape=[sd((s, D_MODEL), F32), sd((s, IN_PAD), MXU_DTYPE), sd((1, D_MODEL), F32), sd((8, 512), F32)],
        compiler_params=_params(("arbitrary",)),
    )(dz, dz, gate_c, u, gate_c, u, conv_w, dq, dk, dv, dzf, db, x, dx2, g_pre, w_pad)


def _tile(s, want):
    return want if s % want == 0 else s


def _halves(a):
    return a.reshape(2, a.shape[0] // 2, a.shape[1])


def _device_step(x, target, w, c_idx, me_idx):
    s = x.shape[0]
    tm = _tile(s, 512)
    tf = _tile(s, 256)
    ta = _tile(s, 512)
    tkk = _tile(s, 2048)
    rows = 16
    gidx = np.arange(512) // HEAD_DIM
    gmat = jnp.asarray(gidx[:, None] == gidx[None, :], MXU_DTYPE)
    sel = jnp.asarray(gidx[:, None] == np.arange(128)[None, :], MXU_DTYPE)
    g_mix_pre, g_mix_post, g_ffn_pre, g_ffn_post = w["g_mix_pre"], w["g_mix_post"], w["g_ffn_pre"], w["g_ffn_post"]
    g_attn, g_conv, b_forget = w["g_attn_out"], w["g_conv_out"], w["b_forget"]
    shard = {n: _halves(w[n][0].astype(MXU_DTYPE)) for n in BIG}

    g_in, conv_all = _gather_weights([shard["w_in"]], w["conv_w"][0])
    w_in_full = jnp.transpose(g_in.reshape(N_CHIPS, D_MODEL, IN_W // N_CHIPS), (1, 0, 2)).reshape(D_MODEL, IN_W)
    w_pad = jnp.concatenate([w_in_full[:, :OFF_F + N_HEADS], jnp.zeros((D_MODEL, OFF_B - OFF_F - N_HEADS), MXU_DTYPE),
                             w_in_full[:, OFF_F + N_HEADS:]], axis=1)
    conv_w = jnp.transpose(conv_all, (1, 0, 2)).reshape(3, CONV_W)

    w_kvz_t = jnp.transpose(w_pad[:, 512:OFF_B])
    h1, qs, k, v, k_t, v_t, z_t, gate_b, gate_c, u = _inproj_fwd(x, g_mix_pre, w_pad, w_kvz_t, tm)
    b_col = jnp.pad(jnp.transpose(b_forget), ((0, HEAD_ROWS - N_HEADS), (0, 0)))
    c_rows, c_cols = _forget_fwd(z_t, b_col)
    o_attn, lse, (g_out, g_gu, g_dn) = _attn_fwd(qs, k, v_t, c_rows, c_cols, ta,
                                                 [shard["w_out"], shard["w_gate_up"], shard["w_down"]])
    w_out = g_out.reshape(D_MODEL, D_MODEL)
    w_gu = g_gu.reshape(N_CHIPS, D_MODEL, FF_PIECE)
    w_dn = g_dn.reshape(2, FF_PIECE, D_MODEL)
    x2, merged, y, z = _mixer_fwd(x, o_attn, gate_b, gate_c, u, conv_w, g_attn, g_conv, w_out, g_mix_post, gmat, tm)
    h2, g, up, a, ff, dout, loss_acc = _ffn_fwd(x2, target, g_ffn_pre, w_gu, w_dn, g_ffn_post, tf)

    dx2, dff, dgu, dg_ffn_post, dg_ffn_pre = _ffn_bwd(dout, ff, x2, g, up, g_ffn_post, g_ffn_pre, w_gu, w_dn, tf)
    dw_dn = _tn_matmul(a, dff, FF_PIECE, 512, tkk, "dw_down").reshape(N_CHIPS, 2, D_FF // (2 * N_CHIPS), D_MODEL)
    dw_gu = _tn_matmul(h2, dgu, 1024, 512, tkk, "dw_gate_up").reshape(2, D_MODEL // 2, 2 * D_FF)
    (dy, d_o, d_b, dz, delta, dg_mix_post, dg_attn, dg_conv), (a_gu, a_dn) = _mixer_bwd(
        dx2, y, o_attn, gate_b, z, g_mix_post, g_attn, g_conv, w_out, gmat, sel, tm, [dw_gu, dw_dn], ["cols", "rows"])
    dw_out = _tn_matmul(merged, dy, 1024, 512, tkk, "dw_out")
    sum_gu = _pair_sum(c_idx, dw_gu, "cols", a_gu, "pair_sum_w_gate_up")
    sum_dn = _pair_sum(c_idx, dw_dn, "rows", a_dn, "pair_sum_w_down")
    (dq_t, dk, dv, dc_cols, dcq), (r_gu, r_dn) = _attn_bwd(
        qs, k, k_t, v, d_o, c_rows, c_cols, lse, delta, ta, [sum_gu[1], sum_dn[1]])
    dc_rows = jnp.transpose(dcq, (1, 0, 2)).reshape(HEAD_ROWS, s)
    dzf, db_f = _forget_bwd(dc_rows, dc_cols, z_t, b_col)
    grad_x, dproj, dg_mix_pre, dcw = _inproj_bwd(dz, gate_c, u, conv_w, dq_t, dk, dv, dzf, d_b,
                                                 x, dx2, g_mix_pre, w_pad, tm)
    dw_pad = _tn_matmul(h1, dproj, 1024, 640, tkk, "dw_in")
    dw_in = jnp.concatenate([dw_pad[:, :OFF_F + N_HEADS], dw_pad[:, OFF_B:]], axis=1)
    dw_in = jnp.transpose(dw_in.reshape(D_MODEL, N_CHIPS, IN_W // N_CHIPS), (1, 0, 2))
    dw_in = dw_in.reshape(N_CHIPS, 2, D_MODEL // 2, IN_W // N_CHIPS)
    dw_out = dw_out.reshape(N_CHIPS, 2, D_MODEL // (2 * N_CHIPS), D_MODEL)

    a_in, a_out = _pair_exchange([dw_in, dw_out])
    sum_in = _pair_sum(c_idx, dw_in, "rows", a_in, "pair_sum_w_in")
    sum_out = _pair_sum(c_idx, dw_out, "rows", a_out, "pair_sum_w_out")
    small = dict(b_forget=db_f[:N_HEADS, 0], g_attn_out=dg_attn, g_conv_out=dg_conv, g_mix_pre=dg_mix_pre,
                 g_mix_post=dg_mix_post, g_ffn_pre=dg_ffn_pre, g_ffn_post=dg_ffn_post)
    r_in, r_out, small_all = _chip_exchange([sum_in[1], sum_out[1]], _pack_small(small, dcw[:3]))
    totals = [_chip_sum(me_idx, sb[0], r, "chip_sum_" + n)
              for n, sb, r in zip(BIG, (sum_in, sum_out, sum_gu, sum_dn), (r_in, r_out, r_gu, r_dn))]
    return loss_acc[0, 0], grad_x, totals, _pair_share(totals), small_all


BIG = ("w_in", "w_out", "w_gate_up", "w_down")
ANY = pl.BlockSpec(memory_space=pl.ANY)


def _place():
    x, y, c = lax.axis_index("x"), lax.axis_index("y"), lax.axis_index("c")
    others = [(1 - x, y), (x, 1 - y), (1 - x, 1 - y)]
    return x, y, c, 2 * x + y, others, [2 * px + py for px, py in others]


def _remote(src, dst, send, recv, dev):
    return pltpu.make_async_remote_copy(src_ref=src, dst_ref=dst, send_sem=send, recv_sem=recv,
                                        device_id=dev, device_id_type=MESH_ID)


def _gather_stages(sh, outs, send, recv):
    x, y, c, me, others, chips = _place()
    sib = (x, y, 1 - c)
    every = [(w, kk) for w in range(len(sh)) for kk in range(3)]

    def first(w, kk):
        return _remote(sh[w].at[c], outs[w].at[me, c], send.at[w, kk], recv.at[w, kk], (*others[kk], c))

    def landed(w, kk):
        r = outs[w].at[chips[kk], c]
        return _remote(r, r, send.at[w, kk], recv.at[w, kk], (*others[kk], c))

    def onward(w, kk, half):
        r = outs[w].at[chips[kk], half]
        return _remote(r, r, send.at[w, 3 + kk], recv.at[w, 3 + kk], sib)

    def start():
        for w, kk in every:
            first(w, kk).start()

    def forward():
        for w, kk in every:
            landed(w, kk).wait_recv()
            onward(w, kk, c).start()

    def finish():
        for w, kk in every:
            onward(w, kk, 1 - c).wait_recv()
        for w, kk in every:
            first(w, kk).wait_send()
            onward(w, kk, c).wait_send()

    return start, forward, finish


def _pair_piece(ref, kind, p, half):
    if kind == "rows":
        return ref.at[p, half]
    cols = ref.shape[2] // N_CHIPS
    return ref.at[half, :, pl.ds(p * cols, cols)]


def _pair_stages(g, kinds, a, send, recv):
    x, y, c, _, _, _ = _place()
    copies = [_remote(_pair_piece(g[w], kinds[w], p, 1 - c), a[w].at[p], send.at[w, p], recv.at[w, p], (x, y, 1 - c))
              for w in range(len(g)) for p in range(N_CHIPS)]

    def start():
        for cp in copies:
            cp.start()

    def finish():
        for cp in copies:
            cp.wait()

    return start, finish


def _chip_stages(pb, rcv, send, recv):
    x, y, c, _, others, chips = _place()
    copies = [_remote(pb[w].at[chips[kk]], rcv[w].at[kk], send.at[w, kk], recv.at[w, kk], (*others[kk], c))
              for w in range(len(pb)) for kk in range(3)]

    def start():
        for cp in copies:
            cp.start()

    def finish():
        for cp in copies:
            cp.wait()

    return start, finish


def _gather_weights(shards, conv_w):
    n = len(shards)

    def body(*refs):
        sh, cw, outs, cwo = refs[:n], refs[n], refs[n + 1:2 * n + 1], refs[2 * n + 1]
        send, recv = refs[2 * n + 2:]
        x, y, c, me, others, chips = _place()
        start, forward, finish = _gather_stages(sh, outs, send, recv)
        start()
        small = [_remote(cw, cwo.at[me], send.at[n, kk], recv.at[n, kk], (*others[kk], c)) for kk in range(3)]
        for cp in small:
            cp.start()
        forward()
        for kk in range(3):
            _remote(cw, cwo.at[chips[kk]], send.at[n, kk], recv.at[n, kk], (*others[kk], c)).wait_recv()
        finish()
        for cp in small:
            cp.wait_send()

    out_shape = [jax.ShapeDtypeStruct((N_CHIPS,) + s.shape, s.dtype) for s in shards]
    out_shape.append(jax.ShapeDtypeStruct((N_CHIPS,) + conv_w.shape, conv_w.dtype))
    got = pl.pallas_call(
        body, name="gather_weights", in_specs=[ANY] * (n + 1), out_specs=[ANY] * (n + 1), out_shape=out_shape,
        scratch_shapes=[pltpu.SemaphoreType.DMA((n + 1, 6)), pltpu.SemaphoreType.DMA((n + 1, 6))],
    )(*shards, conv_w)
    me = 2 * lax.axis_index("x") + lax.axis_index("y")
    return [lax.dynamic_update_index_in_dim(g, own, me, 0) for g, own in zip(got, list(shards) + [conv_w])]


def _pair_exchange(grads):
    n = len(grads)

    def body(*refs):
        g, a = refs[:n], refs[n:2 * n]
        send, recv = refs[2 * n:]
        x, y, c, _, _, _ = _place()
        sib = (x, y, 1 - c)
        copies = [_remote(g[w].at[p, 1 - c], a[w].at[p], send.at[w, p], recv.at[w, p], sib)
                  for w in range(n) for p in range(N_CHIPS)]
        for cp in copies:
            cp.start()
        for cp in copies:
            cp.wait()

    return pl.pallas_call(
        body, name="pair_exchange", in_specs=[ANY] * n, out_specs=[ANY] * n,
        out_shape=[jax.ShapeDtypeStruct((N_CHIPS,) + g.shape[2:], g.dtype) for g in grads],
        scratch_shapes=[pltpu.SemaphoreType.DMA((n, N_CHIPS)), pltpu.SemaphoreType.DMA((n, N_CHIPS))],
    )(*grads)


def _pair_sum(c_idx, g, kind, a, name):
    _, half, cols = a.shape
    if kind == "rows":
        mine = pl.BlockSpec((1, 1, half, cols), lambda p, cr: (p, cr[0], 0, 0))
    else:
        mine = pl.BlockSpec((1, half, cols), lambda p, cr: (cr[0], 0, p))

    def body(c_ref, g_ref, a_ref, pf_ref, pb_ref):
        tot = (g_ref[0, 0] if kind == "rows" else g_ref[0]) + a_ref[0]
        pf_ref[0] = tot
        pb_ref[0] = tot.astype(BF16)

    gs = pltpu.PrefetchScalarGridSpec(
        num_scalar_prefetch=1, grid=(N_CHIPS,),
        in_specs=[mine,
                  pl.BlockSpec((1, half, cols), lambda p, cr: (p, 0, 0))],
        out_specs=[pl.BlockSpec((1, half, cols), lambda p, cr: (p, 0, 0)),
                   pl.BlockSpec((1, half, cols), lambda p, cr: (p, 0, 0))])
    return pl.pallas_call(
        body, name=name, grid_spec=gs,
        out_shape=[jax.ShapeDtypeStruct((N_CHIPS, half, cols), F32), jax.ShapeDtypeStruct((N_CHIPS, half, cols), BF16)],
        compiler_params=_params(("arbitrary",)),
    )(c_idx, g, a)


def _chip_exchange(parts, small):
    n = len(parts)

    def body(*refs):
        pb, sm, rcv, smg = refs[:n], refs[n], refs[n + 1:2 * n + 1], refs[2 * n + 1]
        send, recv, ssend, srecv, loc = refs[2 * n + 2:]
        x, y, c, _, others, chips = _place()
        mine = 4 * x + 2 * y + c
        own = pltpu.make_async_copy(sm, smg.at[mine], loc)
        own.start()
        copies = [_remote(pb[w].at[chips[kk]], rcv[w].at[kk], send.at[w, kk], recv.at[w, kk], (px, py, c))
                  for w in range(n) for kk, (px, py) in enumerate(others)]
        for r in range(1, 8):
            peer = (1 - x if r & 4 else x, 1 - y if r & 2 else y, 1 - c if r & 1 else c)
            copies.append(_remote(sm, smg.at[mine], ssend.at[r - 1], srecv.at[r - 1], peer))
        for cp in copies:
            cp.start()
        for w in range(n):
            for kk, (px, py) in enumerate(others):
                _remote(pb[w].at[chips[kk]], rcv[w].at[kk], send.at[w, kk], recv.at[w, kk], (px, py, c)).wait_recv()
        for r in range(1, 8):
            px, py, pc = (1 - x if r & 4 else x, 1 - y if r & 2 else y, 1 - c if r & 1 else c)
            _remote(sm, smg.at[4 * px + 2 * py + pc], ssend.at[r - 1], srecv.at[r - 1], (px, py, pc)).wait_recv()
        for cp in copies:
            cp.wait_send()
        own.wait()

    out_shape = [jax.ShapeDtypeStruct((3,) + p.shape[1:], p.dtype) for p in parts]
    out_shape.append(jax.ShapeDtypeStruct((8,) + small.shape, small.dtype))
    return pl.pallas_call(
        body, name="chip_exchange", in_specs=[ANY] * (n + 1), out_specs=[ANY] * (n + 1), out_shape=out_shape,
        scratch_shapes=[pltpu.SemaphoreType.DMA((n, 3)), pltpu.SemaphoreType.DMA((n, 3)),
                        pltpu.SemaphoreType.DMA((7,)), pltpu.SemaphoreType.DMA((7,)), pltpu.SemaphoreType.DMA(())],
    )(*parts, small)


def _chip_sum(me_idx, pf, rcv, name):
    _, half, cols = pf.shape

    def body(me_ref, pf_ref, r_ref, t_ref):
        t_ref[...] = ((pf_ref[0] + r_ref[0].astype(F32)) + r_ref[1].astype(F32)) + r_ref[2].astype(F32)

    gs = pltpu.PrefetchScalarGridSpec(
        num_scalar_prefetch=1, grid=(1,),
        in_specs=[pl.BlockSpec((1, half, cols), lambda i, mr: (mr[0], 0, 0)),
                  pl.BlockSpec((3, half, cols), lambda i, mr: (0, 0, 0))],
        out_specs=pl.BlockSpec((half, cols), lambda i, mr: (0, 0)))
    return pl.pallas_call(
        body, name=name, grid_spec=gs, out_shape=jax.ShapeDtypeStruct((half, cols), F32),
        compiler_params=_params(("arbitrary",)),
    )(me_idx, pf, rcv)


def _pair_share(totals):
    n = len(totals)

    def body(*refs):
        t, g = refs[:n], refs[n:2 * n]
        send, recv = refs[2 * n:]
        x, y, c, _, _, _ = _place()
        copies = [_remote(t[w], g[w], send.at[w], recv.at[w], (x, y, 1 - c)) for w in range(n)]
        for cp in copies:
            cp.start()
        for cp in copies:
            cp.wait()

    return pl.pallas_call(
        body, name="pair_share", in_specs=[ANY] * n, out_specs=[ANY] * n,
        out_shape=[jax.ShapeDtypeStruct(t.shape, t.dtype) for t in totals],
        scratch_shapes=[pltpu.SemaphoreType.DMA((n,)), pltpu.SemaphoreType.DMA((n,))],
    )(*totals)


def _adamw_math(w, g, m, v):
    m = ADAM_B1 * m + (1.0 - ADAM_B1) * g
    v = ADAM_B2 * v + (1.0 - ADAM_B2) * (g * g)
    m_hat = m / (1.0 - ADAM_B1 ** ADAM_STEP)
    v_hat = v / (1.0 - ADAM_B2 ** ADAM_STEP)
    delta = -ADAM_LR * (m_hat / (jnp.sqrt(v_hat) + ADAM_EPS) + ADAM_WD * w)
    return delta, m, v


def _adamw(c_idx, w, mine, theirs, m, v, nb, name):
    rows, cols = w.shape
    tr = rows // (2 * nb)

    def body(c_ref, w_ref, a_ref, b_ref, m_ref, v_ref, g_ref, d_ref, nm_ref, nv_ref):
        g = jnp.where(pl.program_id(0) == c_ref[0], a_ref[...], b_ref[...])
        g_ref[...] = g
        d_ref[...], nm_ref[...], nv_ref[...] = _adamw_math(w_ref[...], g, m_ref[...], v_ref[...])

    full = pl.BlockSpec((tr, cols), lambda hh, i, cr: (hh * nb + i, 0))
    half = pl.BlockSpec((tr, cols), lambda hh, i, cr: (i, 0))
    gs = pltpu.PrefetchScalarGridSpec(num_scalar_prefetch=1, grid=(2, nb), in_specs=[full, half, half, full, full],
                                      out_specs=[full] * 4)
    return pl.pallas_call(
        body, name=name, grid_spec=gs, out_shape=[jax.ShapeDtypeStruct((rows, cols), F32)] * 4,
        compiler_params=_params(("arbitrary", "arbitrary")),
    )(c_idx, w, mine, theirs, m, v)


def _small_update(gathered, w, m, v):
    def body(gg_ref, w_ref, m_ref, v_ref, g_ref, d_ref, nm_ref, nv_ref):
        g = gg_ref[0]
        for dev in range(1, 8):
            g = g + gg_ref[dev]
        g_ref[...] = g
        d_ref[...], nm_ref[...], nv_ref[...] = _adamw_math(w_ref[...], g, m_ref[...], v_ref[...])

    return pl.pallas_call(body, name="small_update", out_shape=[jax.ShapeDtypeStruct(w.shape, F32)] * 4,
                          compiler_params=_params())(gathered, w, m, v)


SMALL = ("g_mix_pre", "g_mix_post", "g_ffn_pre", "g_ffn_post")


def _pack_small(t, conv_full):
    conv = jnp.pad(conv_full.reshape(1, 3 * CONV_W), ((0, 0), (0, 2048 - 3 * CONV_W))).reshape(2, 1024)
    return jnp.concatenate([t[n].reshape(1, 1024) for n in SMALL]
                           + [jnp.concatenate([t["g_attn_out"].reshape(1, 512), t["g_conv_out"].reshape(1, 512)], axis=1),
                              conv, jnp.pad(t["b_forget"].reshape(1, N_HEADS), ((0, 0), (0, 1024 - N_HEADS)))], axis=0)


def _unpack_small(p, me):
    out = {n: p[i].reshape(1, 1024) for i, n in enumerate(SMALL)}
    out["g_attn_out"] = p[4, :512].reshape(1, 512)
    out["g_conv_out"] = p[4, 512:].reshape(1, 512)
    conv = p[5:7].reshape(2048)[:3 * CONV_W].reshape(3, CONV_W)
    out["conv_w"] = lax.dynamic_slice_in_dim(conv, me * 128, 128, axis=1).reshape(1, 3, 128)
    out["b_forget"] = p[7, :N_HEADS].reshape(1, N_HEADS)
    return out


def _conv_in_place(shard, me):
    return lax.dynamic_update_slice_in_dim(jnp.zeros((3, CONV_W), F32), shard, me * 128, axis=1)


def kernel(x, w_in, b_forget, conv_w, g_attn_out, g_conv_out, w_out, g_mix_pre, g_mix_post, w_gate_up, w_down, g_ffn_pre, g_ffn_post, loss_target, m_w_in, m_b_forget, m_conv_w, m_g_attn_out, m_g_conv_out, m_w_out, m_g_mix_pre, m_g_mix_post, m_w_gate_up, m_w_down, m_g_ffn_pre, m_g_ffn_post, v_w_in, v_b_forget, v_conv_w, v_g_attn_out, v_g_conv_out, v_w_out, v_g_mix_pre, v_g_mix_post, v_w_gate_up, v_w_down, v_g_ffn_pre, v_g_ffn_post):
    w = dict(w_in=w_in, b_forget=b_forget, conv_w=conv_w, g_attn_out=g_attn_out, g_conv_out=g_conv_out, w_out=w_out,
             g_mix_pre=g_mix_pre, g_mix_post=g_mix_post, w_gate_up=w_gate_up, w_down=w_down, g_ffn_pre=g_ffn_pre,
             g_ffn_post=g_ffn_post)
    m = dict(w_in=m_w_in, b_forget=m_b_forget, conv_w=m_conv_w, g_attn_out=m_g_attn_out, g_conv_out=m_g_conv_out,
             w_out=m_w_out, g_mix_pre=m_g_mix_pre, g_mix_post=m_g_mix_post, w_gate_up=m_w_gate_up, w_down=m_w_down,
             g_ffn_pre=m_g_ffn_pre, g_ffn_post=m_g_ffn_post)
    v = dict(w_in=v_w_in, b_forget=v_b_forget, conv_w=v_conv_w, g_attn_out=v_g_attn_out, g_conv_out=v_g_conv_out,
             w_out=v_w_out, g_mix_pre=v_g_mix_pre, g_mix_post=v_g_mix_post, w_gate_up=v_w_gate_up, w_down=v_w_down,
             g_ffn_pre=v_g_ffn_pre, g_ffn_post=v_g_ffn_post)
    cx, cy, cc = lax.axis_index("x"), lax.axis_index("y"), lax.axis_index("c")
    me = 2 * cx + cy
    c_idx = cc.astype(jnp.int32).reshape(1)
    me_idx = me.astype(jnp.int32).reshape(1)

    loss_local, grad_x, totals, shared, small_all = _device_step(x[0], loss_target[0], w, c_idx, me_idx)
    loss = 0.5 * lax.psum(loss_local, ("x", "y", "c"))

    gsum, delta, new_m, new_v = {}, {}, {}, {}
    for n, mine, theirs in zip(BIG, totals, shared):
        gs, d, nm, nv = _adamw(c_idx, w[n][0], mine, theirs, m[n][0], v[n][0], 2, "adamw_" + n)
        gsum[n], delta[n], new_m[n], new_v[n] = gs[None], d[None], nm[None], nv[None]
    packed = [_pack_small(t, _conv_in_place(t["conv_w"][0], me)) for t in (w, m, v)]
    g_small, d_small, m_small, v_small = _small_update(small_all, *packed)
    gsum.update(_unpack_small(g_small, me))
    delta.update(_unpack_small(d_small, me))
    new_m.update(_unpack_small(m_small, me))
    new_v.update(_unpack_small(v_small, me))

    order = ("w_in", "b_forget", "conv_w", "g_attn_out", "g_conv_out", "w_out", "g_mix_pre", "g_mix_post",
             "w_gate_up", "w_down", "g_ffn_pre", "g_ffn_post")
    return (loss, grad_x[None], *[gsum[n] for n in order], *[delta[n] for n in order],
            *[new_m[n] for n in order], *[new_v[n] for n in order])
```

```python
import functools

import jax
import jax.numpy as jnp
import numpy as np
from jax import lax
from jax.experimental import pallas as pl
from jax.experimental.pallas import tpu as pltpu

F32 = jnp.float32
BF16 = jnp.bfloat16
MXU_DTYPE = jnp.bfloat16

D_MODEL = 1024
HEAD_DIM = 64
N_HEADS = 8
ATTN_W = 512
CONV_W = 512
D_FF = 2816
FF_PIECE = 1408
EPS = 1e-6
Q_SCALE = HEAD_DIM ** -0.5

OFF_F = 1536
OFF_B = 1664
OFF_C = 2176
OFF_U = 2688
IN_PAD = 3200
IN_W = 3080
N_CHIPS = 4

ADAM_LR = 0.001
ADAM_B1 = 0.9
ADAM_B2 = 0.999
ADAM_EPS = 1e-08
ADAM_WD = 0.01
ADAM_STEP = 10

VMEM_LIMIT_V7X = 56 * 1024 * 1024
MESH_ID = pl.DeviceIdType.MESH


def _params(sem=None, vmem=VMEM_LIMIT_V7X):
    kw = {"vmem_limit_bytes": vmem}
    if sem is not None:
        kw["dimension_semantics"] = sem
    return pltpu.CompilerParams(**kw)


def _dot(a, b):
    return jnp.dot(a, b, preferred_element_type=F32)


def _dot_nt(a, b):
    return lax.dot_general(a, b, (((1,), (1,)), ((), ())), preferred_element_type=F32)


def _dot_exact(x, ones, parts):
    if ones.dtype == F32:
        return _dot(x, ones)
    acc = None
    rem = x
    for _ in range(parts):
        piece = rem.astype(BF16)
        rem = rem - piece.astype(F32)
        term = _dot(piece, ones)
        acc = term if acc is None else acc + term
    return acc


def _rms(v):
    return lax.rsqrt(jnp.mean(v * v, axis=-1, keepdims=True) + EPS)


def _tok(tm, w):
    return pl.BlockSpec((tm, w), lambda i: (i, 0))


def _whole(shape, single=False):
    nd = len(shape)
    if single:
        return pl.BlockSpec(shape, lambda i: (0,) * nd, pipeline_mode=pl.Buffered(1))
    return pl.BlockSpec(shape, lambda i: (0,) * nd)


def _feat(rows, tm):
    return pl.BlockSpec((rows, tm), lambda i: (0, i))


def _inproj_fwd(x, g_pre, w_pad, w_kvz_t, tm):
    s = x.shape[0]

    def body(x_ref, g_ref, w_ref, wt_ref, h_ref, q_ref, k_ref, v_ref, kt_ref, vt_ref, zt_ref, b_ref, c_ref, u_ref):
        xv = x_ref[...]
        h = ((xv * _rms(xv)) * g_ref[...]).astype(MXU_DTYPE)
        h_ref[...] = h

        def proj(lo, hi):
            return _dot(h, w_ref[:, lo:hi])

        q_ref[...] = (proj(0, 512) * Q_SCALE).astype(MXU_DTYPE)
        kt = _dot_nt(wt_ref[0:512, :], h)
        vt = _dot_nt(wt_ref[512:1024, :], h)
        kt_ref[...] = kt.astype(MXU_DTYPE)
        vt_ref[...] = vt.astype(MXU_DTYPE)
        k_ref[...] = kt.T.astype(MXU_DTYPE)
        v_ref[...] = vt.T.astype(MXU_DTYPE)
        zt_ref[...] = _dot_nt(wt_ref[1024:1152, :], h)
        b_ref[...] = proj(OFF_B, OFF_C)
        c_ref[...] = proj(OFF_C, OFF_U)
        u_ref[...] = proj(OFF_U, IN_PAD)

    sd = jax.ShapeDtypeStruct
    return pl.pallas_call(
        body, name="inproj_fwd", grid=(s // tm,),
        in_specs=[_tok(tm, D_MODEL), _whole((1, D_MODEL)), _whole((D_MODEL, IN_PAD), single=True),
                  _whole((1152, D_MODEL), single=True)],
        out_specs=[_tok(tm, D_MODEL), _tok(tm, 512), _tok(tm, 512), _tok(tm, 512), _feat(512, tm), _feat(512, tm),
                   _feat(128, tm), _tok(tm, 512), _tok(tm, 512), _tok(tm, 512)],
        out_shape=[sd((s, D_MODEL), MXU_DTYPE), sd((s, 512), MXU_DTYPE), sd((s, 512), MXU_DTYPE),
                   sd((s, 512), MXU_DTYPE), sd((512, s), MXU_DTYPE), sd((512, s), MXU_DTYPE), sd((128, s), F32),
                   sd((s, 512), F32), sd((s, 512), F32), sd((s, 512), F32)],
        compiler_params=_params(("arbitrary",)),
    )(x, g_pre, w_pad, w_kvz_t)


def _tri(n, upper):
    r = lax.broadcasted_iota(jnp.int32, (n, n), 0)
    c = lax.broadcasted_iota(jnp.int32, (n, n), 1)
    return ((r <= c) if upper else (r >= c)).astype(MXU_DTYPE)


HEAD_ROWS = 16


def _rows_to_cols(v):
    return jnp.concatenate([v, jnp.zeros((128 - HEAD_ROWS, 128), F32)], axis=0).T


def _forget_fwd(z_t, b_col):
    s = z_t.shape[1]
    nb = s // 128

    def body(z_ref, b_ref, c_ref, cc_ref):
        upper = _tri(128, True)

        carry = jnp.zeros((HEAD_ROWS, 1), F32)
        for n in range(nb):
            off = n * 128
            lf = jax.nn.log_sigmoid(z_ref[0:HEAD_ROWS, off:off + 128] + b_ref[...])
            cs = _dot_exact(lf, upper, 3) + carry
            c_ref[:, off:off + 128] = cs
            cc_ref[off:off + 128, :] = _rows_to_cols(cs)
            carry = carry + jnp.sum(lf, axis=1, keepdims=True)

    return pl.pallas_call(body, name="forget_fwd",
                          out_shape=[jax.ShapeDtypeStruct((HEAD_ROWS, s), F32), jax.ShapeDtypeStruct((s, 128), F32)],
                          compiler_params=_params())(z_t, b_col)


def _pair_lanes(pp):
    return pl.ds(pl.multiple_of(pp * 128, 128), 128)


def _head_rows(h):
    return pl.ds(pl.multiple_of(h * HEAD_DIM, HEAD_DIM), HEAD_DIM)


def _only_head(block, hb):
    lane = lax.broadcasted_iota(jnp.int32, block.shape, 1)
    return jnp.where((lane >= HEAD_DIM) if hb else (lane < HEAD_DIM), block, jnp.zeros_like(block))


def _head_col(cols, h):
    lane = lax.broadcasted_iota(jnp.int32, cols.shape, 1)
    return jnp.sum(jnp.where(lane == h, cols, 0.0), axis=1, keepdims=True)


def _attn_fwd(qs, k, v_t, c_rows, c_cols, t, shards):
    s = qs.shape[0]
    n = s // t
    pairs = [(i, j) for i in range(n) for j in range(i + 1)]
    it = jnp.asarray(np.array([p[0] for p in pairs], np.int32))
    jt = jnp.asarray(np.array([p[1] for p in pairs], np.int32))
    nw = len(shards)
    last = len(pairs) - 1
    mid = (2 * len(pairs)) // 3

    def body(it_ref, jt_ref, q_ref, k_ref, vt_ref, cq_ref, ck_ref, *rest):
        sh, (o_ref, lse_ref), got = rest[:nw], rest[nw:nw + 2], rest[nw + 2:2 * nw + 2]
        m_sc, l_sc, acc_sc, send, recv = rest[2 * nw + 2:]
        p = pl.program_id(0)
        i = it_ref[p]
        j = jt_ref[p]
        gather_start, gather_forward, gather_finish = _gather_stages(sh, got, send, recv)
        pl.when(p == 0)(gather_start)
        if mid < last:
            pl.when(p == mid)(gather_forward)

        @pl.when(j == 0)
        def _():
            m_sc[...] = jnp.full_like(m_sc, -1e30)
            l_sc[...] = jnp.ones_like(l_sc)
            acc_sc[...] = jnp.zeros_like(acc_sc)

        def pair_step(pp, diagonal):
            lanes = _pair_lanes(pp)
            kp = k_ref[:, lanes]
            qp = q_ref[:, lanes]
            ck_all = ck_ref[...]
            for hb in range(2):
                h = 2 * pp + hb
                row = pl.ds(h, 1)
                st = _dot_nt(_only_head(kp, hb), qp) + (cq_ref[row, :] - _head_col(ck_all, h))
                if diagonal:
                    kpos = lax.broadcasted_iota(jnp.int32, (t, t), 0)
                    qpos = lax.broadcasted_iota(jnp.int32, (t, t), 1)
                    st = jnp.where(kpos <= qpos, st, -1e30)
                m_prev = m_sc[row, :]
                m_new = jnp.maximum(m_prev, jnp.max(st, axis=0, keepdims=True))
                alpha = jnp.exp(m_prev - m_new)
                pt = jnp.exp(st - m_new)
                l_sc[row, :] = alpha * l_sc[row, :] + jnp.sum(pt, axis=0, keepdims=True)
                rows = _head_rows(h)
                acc_sc[rows, :] = acc_sc[rows, :] * alpha + _dot(vt_ref[rows, :], pt.astype(MXU_DTYPE))
                m_sc[row, :] = m_new

        @pl.when(j < i)
        def _():
            @pl.loop(0, N_HEADS // 2)
            def _(pp):
                pair_step(pp, False)

        @pl.when(j == i)
        def _():
            @pl.loop(0, N_HEADS // 2)
            def _(pp):
                pair_step(pp, True)
                sub = lax.broadcasted_iota(jnp.int32, (2 * HEAD_DIM, t), 0)
                l_pair = jnp.where(sub < HEAD_DIM, l_sc[pl.ds(2 * pp, 1), :], l_sc[pl.ds(2 * pp + 1, 1), :])
                o_t = acc_sc[pl.ds(pl.multiple_of(pp * 2 * HEAD_DIM, 2 * HEAD_DIM), 2 * HEAD_DIM), :] / l_pair
                o_ref[:, _pair_lanes(pp)] = o_t.T

            lse_ref[...] = m_sc[...] + jnp.log(l_sc[...])

        @pl.when(p == last)
        def _():
            if mid >= last:
                gather_forward()
            gather_finish()

    gs = pltpu.PrefetchScalarGridSpec(
        num_scalar_prefetch=2, grid=(len(pairs),),
        in_specs=[pl.BlockSpec((t, ATTN_W), lambda p, it_, jt_: (it_[p], 0)),
                  pl.BlockSpec((t, ATTN_W), lambda p, it_, jt_: (jt_[p], 0)),
                  pl.BlockSpec((ATTN_W, t), lambda p, it_, jt_: (0, jt_[p])),
                  pl.BlockSpec((HEAD_ROWS, t), lambda p, it_, jt_: (0, it_[p])),
                  pl.BlockSpec((t, 128), lambda p, it_, jt_: (jt_[p], 0))] + [ANY] * nw,
        out_specs=[pl.BlockSpec((t, ATTN_W), lambda p, it_, jt_: (it_[p], 0)),
                   pl.BlockSpec((HEAD_ROWS, t), lambda p, it_, jt_: (0, it_[p]))] + [ANY] * nw,
        scratch_shapes=[pltpu.VMEM((HEAD_ROWS, t), F32), pltpu.VMEM((HEAD_ROWS, t), F32), pltpu.VMEM((ATTN_W, t), F32),
                        pltpu.SemaphoreType.DMA((nw, 6)), pltpu.SemaphoreType.DMA((nw, 6))])
    o, lse, *got = pl.pallas_call(
        body, name="attn_fwd", grid_spec=gs,
        out_shape=[jax.ShapeDtypeStruct((s, ATTN_W), F32), jax.ShapeDtypeStruct((HEAD_ROWS, s), F32)]
        + [jax.ShapeDtypeStruct((N_CHIPS,) + a.shape, a.dtype) for a in shards],
        compiler_params=_params(("arbitrary",)),
    )(it, jt, qs, k, v_t, c_rows, c_cols, *shards)
    me = 2 * lax.axis_index("x") + lax.axis_index("y")
    return o, lse, [lax.dynamic_update_index_in_dim(g, own, me, 0) for g, own in zip(got, shards)]


def _shift_down(cur, prev_ref, first):
    row = lax.broadcasted_iota(jnp.int32, cur.shape, 0)
    p7 = jnp.where(first, 0.0, prev_ref[0][7:8, :] * prev_ref[1][7:8, :])
    p6 = jnp.where(first, 0.0, prev_ref[0][6:7, :] * prev_ref[1][6:7, :])
    s1 = jnp.where(row == 0, p7, pltpu.roll(cur, 1, 0))
    s2 = jnp.where(row == 0, p6, jnp.where(row == 1, p7, pltpu.roll(cur, 2, 0)))
    return s1, s2


def _group_ms(v, gmat):
    return _dot_exact(v, gmat, 2) * (1.0 / HEAD_DIM)


def _mixer_fwd(x, o_attn, gate_b, gate_c, u, conv_w, g_attn, g_conv, w_out, g_post, gmat, tm):
    s = x.shape[0]

    def body(x_ref, o_ref, b_ref, c_ref, u_ref, cp_ref, up_ref, cw_ref, ga_ref, gc_ref, wo_ref, gp_ref, gm_ref,
             x2_ref, mg_ref, y_ref, z_ref):
        i = pl.program_id(0)
        cu = c_ref[...] * u_ref[...]
        cu1, cu2 = _shift_down(cu, (cp_ref, up_ref), i == 0)
        z = cw_ref[0:1, :] * cu2 + cw_ref[1:2, :] * cu1 + cw_ref[2:3, :] * cu
        z_ref[...] = z
        cv = b_ref[...] * z
        ov = o_ref[...]
        gm = gm_ref[...]
        ma = ((ov * lax.rsqrt(_group_ms(ov * ov, gm) + EPS)) * ga_ref[...]).astype(MXU_DTYPE)
        mc = ((cv * lax.rsqrt(_group_ms(cv * cv, gm) + EPS)) * gc_ref[...]).astype(MXU_DTYPE)
        mg_ref[:, 0:ATTN_W] = ma
        mg_ref[:, ATTN_W:D_MODEL] = mc
        y = _dot(ma, wo_ref[0:ATTN_W, :]) + _dot(mc, wo_ref[ATTN_W:D_MODEL, :])
        y_ref[...] = y
        x2_ref[...] = x_ref[...] + (y * _rms(y)) * gp_ref[...]

    halo = pl.BlockSpec((8, 512), lambda i: (jnp.maximum(i * (tm // 8) - 1, 0), 0))
    sd = jax.ShapeDtypeStruct
    return pl.pallas_call(
        body, name="mixer_fwd", grid=(s // tm,),
        in_specs=[_tok(tm, D_MODEL), _tok(tm, 512), _tok(tm, 512), _tok(tm, 512), _tok(tm, 512), halo, halo,
                  _whole((3, 512)), _whole((1, 512)), _whole((1, 512)), _whole((D_MODEL, D_MODEL), single=True),
                  _whole((1, D_MODEL)), _whole((512, 512))],
        out_specs=[_tok(tm, D_MODEL), _tok(tm, D_MODEL), _tok(tm, D_MODEL), _tok(tm, 512)],
        out_shape=[sd((s, D_MODEL), F32), sd((s, D_MODEL), MXU_DTYPE), sd((s, D_MODEL), F32), sd((s, 512), F32)],
        compiler_params=_params(("arbitrary",)),
    )(x, o_attn, gate_b, gate_c, u, gate_c, u, conv_w, g_attn, g_conv, w_out, g_post, gmat)


def _ffn_fwd(x2, target, g_pre, w_gu, w_dn, g_post, tm):
    s = x2.shape[0]

    def body(x_ref, t_ref, gpre_ref, wgu_ref, wdn_ref, gpost_ref,
             h_ref, g_ref, up_ref, a_ref, ff_ref, dout_ref, loss_ref):
        xv = x_ref[...]
        h = ((xv * _rms(xv)) * gpre_ref[...]).astype(MXU_DTYPE)
        h_ref[...] = h
        ff = jnp.zeros((tm, D_MODEL), F32)
        for j in range(2):
            cols = slice(j * FF_PIECE, (j + 1) * FF_PIECE)
            g = _dot(h, wgu_ref[j])
            up = _dot(h, wgu_ref[2 + j])
            a = ((g * jax.nn.sigmoid(g)) * up).astype(MXU_DTYPE)
            g_ref[:, cols] = g
            up_ref[:, cols] = up
            a_ref[:, cols] = a
            ff = ff + _dot(a, wdn_ref[j])
        ff_ref[...] = ff
        err = (xv + (ff * _rms(ff)) * gpost_ref[...]) - t_ref[...]
        dout_ref[...] = err * (1.0 / D_MODEL)
        part = jnp.sum(jnp.mean(err * err, axis=-1, keepdims=True), axis=0, keepdims=True)

        @pl.when(pl.program_id(0) == 0)
        def _():
            loss_ref[...] = jnp.zeros_like(loss_ref)

        loss_ref[...] += part

    sd = jax.ShapeDtypeStruct
    return pl.pallas_call(
        body, name="ffn_fwd", grid=(s // tm,),
        in_specs=[_tok(tm, D_MODEL), _tok(tm, D_MODEL), _whole((1, D_MODEL)),
                  _whole((4, D_MODEL, FF_PIECE), single=True), _whole((2, FF_PIECE, D_MODEL), single=True),
                  _whole((1, D_MODEL))],
        out_specs=[_tok(tm, D_MODEL), _tok(tm, D_FF), _tok(tm, D_FF), _tok(tm, D_FF), _tok(tm, D_MODEL),
                   _tok(tm, D_MODEL), _whole((8, 128))],
        out_shape=[sd((s, D_MODEL), MXU_DTYPE), sd((s, D_FF), F32), sd((s, D_FF), F32), sd((s, D_FF), MXU_DTYPE),
                   sd((s, D_MODEL), F32), sd((s, D_MODEL), F32), sd((8, 128), F32)],
        compiler_params=_params(("arbitrary",)),
    )(x2, target, g_pre, w_gu, w_dn, g_post)


def _norm_bwd(dy, normed, rinv, gain):
    t = dy * gain
    return rinv * (t - normed * jnp.mean(t * normed, axis=-1, keepdims=True))


def _acc_rows(ref, first, val):
    @pl.when(first)
    def _():
        ref[...] = jnp.zeros_like(ref)

    ref[...] += jnp.sum(val, axis=0, keepdims=True)


def _ffn_bwd(dout, ff, x2, g, up, g_post, g_pre, w_gu, w_dn, tm):
    s = x2.shape[0]

    def body(do_ref, ff_ref, x_ref, g_ref, up_ref, gpost_ref, gpre_ref, wgu_ref, wdn_ref,
             dx_ref, dff_ref, dgu_ref, dgpost_ref, dgpre_ref):
        first = pl.program_id(0) == 0
        ffv = ff_ref[...]
        rf = _rms(ffv)
        n = ffv * rf
        do = do_ref[...]
        _acc_rows(dgpost_ref, first, do * n)
        dff = _norm_bwd(do, n, rf, gpost_ref[...]).astype(MXU_DTYPE)
        dff_ref[...] = dff
        dh = jnp.zeros((tm, D_MODEL), F32)
        for j in range(2):
            cols = slice(j * FF_PIECE, (j + 1) * FF_PIECE)
            da = _dot_nt(dff, wdn_ref[j])
            gv = g_ref[:, cols]
            sg = jax.nn.sigmoid(gv)
            dg = (da * up_ref[:, cols] * (sg * (1.0 + gv * (1.0 - sg)))).astype(MXU_DTYPE)
            du = (da * (gv * sg)).astype(MXU_DTYPE)
            dgu_ref[:, cols] = dg
            dgu_ref[:, D_FF + j * FF_PIECE:D_FF + (j + 1) * FF_PIECE] = du
            dh = dh + _dot_nt(dg, wgu_ref[j]) + _dot_nt(du, wgu_ref[2 + j])
        xv = x_ref[...]
        r2 = _rms(xv)
        nx = xv * r2
        _acc_rows(dgpre_ref, first, dh * nx)
        dx_ref[...] = do + _norm_bwd(dh, nx, r2, gpre_ref[...])

    sd = jax.ShapeDtypeStruct
    return pl.pallas_call(
        body, name="ffn_bwd", grid=(s // tm,),
        in_specs=[_tok(tm, D_MODEL), _tok(tm, D_MODEL), _tok(tm, D_MODEL), _tok(tm, D_FF), _tok(tm, D_FF),
                  _whole((1, D_MODEL)), _whole((1, D_MODEL)),
                  _whole((4, D_MODEL, FF_PIECE), single=True), _whole((2, FF_PIECE, D_MODEL), single=True)],
        out_specs=[_tok(tm, D_MODEL), _tok(tm, D_MODEL), _tok(tm, 2 * D_FF), _whole((1, D_MODEL)),
                   _whole((1, D_MODEL))],
        out_shape=[sd((s, D_MODEL), F32), sd((s, D_MODEL), MXU_DTYPE), sd((s, 2 * D_FF), MXU_DTYPE),
                   sd((1, D_MODEL), F32), sd((1, D_MODEL), F32)],
        compiler_params=_params(("arbitrary",)),
    )(dout, ff, x2, g, up, g_post, g_pre, w_gu, w_dn)


def _tn_matmul(a, b, tm, tn, tk, name):
    s, m = a.shape
    n = b.shape[1]

    def body(a_ref, b_ref, o_ref):
        @pl.when(pl.program_id(2) == 0)
        def _():
            o_ref[...] = jnp.zeros_like(o_ref)

        o_ref[...] += lax.dot_general(a_ref[...], b_ref[...], (((0,), (0,)), ((), ())), preferred_element_type=F32)

    return pl.pallas_call(
        body, name=name, grid=(m // tm, n // tn, s // tk),
        in_specs=[pl.BlockSpec((tk, tm), lambda i, j, kk: (kk, i)), pl.BlockSpec((tk, tn), lambda i, j, kk: (kk, j))],
        out_specs=pl.BlockSpec((tm, tn), lambda i, j, kk: (i, j)),
        out_shape=jax.ShapeDtypeStruct((m, n), F32),
        compiler_params=_params(("arbitrary", "arbitrary", "arbitrary")),
    )(a, b)


def _mixer_bwd(dx2, y, o_attn, gate_b, z, g_post, g_attn, g_conv, w_out, gmat, sel, tm, ready, kinds):
    s = dx2.shape[0]
    nw = len(ready)
    nt = s // tm

    def body(d_ref, y_ref, o_ref, b_ref, z_ref, gp_ref, ga_ref, gc_ref, wo_ref, gm_ref, sel_ref, *rest):
        grads = rest[:nw]
        dy_ref, do_ref, db_ref, dz_ref, delta_ref, dgp_ref, dga_ref, dgc_ref = rest[nw:nw + 8]
        taken = rest[nw + 8:2 * nw + 8]
        send, recv = rest[2 * nw + 8:]
        first = pl.program_id(0) == 0
        pair_start, pair_finish = _pair_stages(grads, kinds, taken, send, recv)
        pl.when(first)(pair_start)
        yv = y_ref[...]
        ry = _rms(yv)
        ny = yv * ry
        d = d_ref[...]
        _acc_rows(dgp_ref, first, d * ny)
        dy = _norm_bwd(d, ny, ry, gp_ref[...]).astype(MXU_DTYPE)
        dy_ref[...] = dy
        dm = _dot_nt(dy, wo_ref[...])
        gm = gm_ref[...]

        def group_bwd(val, dmv, gain, dg_ref):
            rg = lax.rsqrt(_group_ms(val * val, gm) + EPS)
            nv = val * rg
            _acc_rows(dg_ref, first, dmv * nv)
            t = dmv * gain
            return rg * (t - nv * _group_ms(t * nv, gm))

        ov = o_ref[...]
        d_o = group_bwd(ov, dm[:, 0:ATTN_W], ga_ref[...], dga_ref)
        do_ref[...] = d_o.astype(MXU_DTYPE)
        delta_ref[...] = _dot_exact(d_o * ov, sel_ref[...], 2).T[0:HEAD_ROWS, :]
        zv = z_ref[...]
        bv = b_ref[...]
        d_cv = group_bwd(bv * zv, dm[:, ATTN_W:D_MODEL], gc_ref[...], dgc_ref)
        db_ref[...] = d_cv * zv
        dz_ref[...] = d_cv * bv
        pl.when(pl.program_id(0) == nt - 1)(pair_finish)

    sd = jax.ShapeDtypeStruct
    taken_shape = [sd((N_CHIPS, g.shape[-2], g.shape[-1] if kd == "rows" else g.shape[-1] // N_CHIPS), F32)
                   for g, kd in zip(ready, kinds)]
    out = pl.pallas_call(
        body, name="mixer_bwd", grid=(nt,),
        in_specs=[_tok(tm, D_MODEL), _tok(tm, D_MODEL), _tok(tm, 512), _tok(tm, 512), _tok(tm, 512),
                  _whole((1, D_MODEL)), _whole((1, 512)), _whole((1, 512)),
                  _whole((D_MODEL, D_MODEL), single=True), _whole((512, 512)), _whole((512, 128))] + [ANY] * nw,
        out_specs=[_tok(tm, D_MODEL), _tok(tm, 512), _tok(tm, 512), _tok(tm, 512), _feat(HEAD_ROWS, tm),
                   _whole((1, D_MODEL)), _whole((1, 512)), _whole((1, 512))] + [ANY] * nw,
        out_shape=[sd((s, D_MODEL), MXU_DTYPE), sd((s, 512), MXU_DTYPE), sd((s, 512), F32), sd((s, 512), F32),
                   sd((HEAD_ROWS, s), F32), sd((1, D_MODEL), F32), sd((1, 512), F32), sd((1, 512), F32)] + taken_shape,
        scratch_shapes=[pltpu.SemaphoreType.DMA((nw, N_CHIPS)), pltpu.SemaphoreType.DMA((nw, N_CHIPS))],
        compiler_params=_params(("arbitrary",)),
    )(dx2, y, o_attn, gate_b, z, g_post, g_attn, g_conv, w_out, gmat, sel, *ready)
    return out[:8], out[8:]


def _attn_bwd(qs, k, k_t, v, do, c_rows, c_cols, lse, delta, t, parts):
    s = qs.shape[0]
    n = s // t
    pairs = [(i, j) for j in range(n) for i in range(j, n)]
    it = jnp.asarray(np.array([p[0] for p in pairs], np.int32))
    jt = jnp.asarray(np.array([p[1] for p in pairs], np.int32))

    nw = len(parts)

    def body(it_ref, jt_ref, q_ref, k_ref, kt_ref, v_ref, do_ref, cq_ref, ck_ref, lse_ref, dl_ref, *rest):
        pb = rest[:nw]
        dq_ref, dk_ref, dv_ref, dc_ref, dcq_ref = rest[nw:nw + 5]
        rcv = rest[nw + 5:2 * nw + 5]
        dk_sc, dv_sc, dc_sc, send, recv = rest[2 * nw + 5:]
        p = pl.program_id(0)
        i = it_ref[p]
        j = jt_ref[p]
        chip_start, chip_finish = _chip_stages(pb, rcv, send, recv)

        @pl.when(p == 0)
        def _():
            chip_start()
            dq_ref[...] = jnp.zeros_like(dq_ref)
            dcq_ref[...] = jnp.zeros_like(dcq_ref)

        @pl.when(i == j)
        def _():
            dk_sc[...] = jnp.zeros_like(dk_sc)
            dv_sc[...] = jnp.zeros_like(dv_sc)
            dc_sc[...] = jnp.zeros_like(dc_sc)

        def pair_step(pp, diagonal):
            lanes = _pair_lanes(pp)
            qp = q_ref[:, lanes]
            kp = k_ref[:, lanes]
            vp = v_ref[:, lanes]
            dop = do_ref[:, lanes]
            ck_all = ck_ref[...]
            lane = lax.broadcasted_iota(jnp.int32, (t, 128), 1)
            for hb in range(2):
                h = 2 * pp + hb
                row = pl.ds(h, 1)
                bias = (cq_ref[row, :] - lse_ref[row, :]) - _head_col(ck_all, h)
                pt = jnp.exp(_dot_nt(_only_head(kp, hb), qp) + bias)
                if diagonal:
                    kpos = lax.broadcasted_iota(jnp.int32, (t, t), 0)
                    qpos = lax.broadcasted_iota(jnp.int32, (t, t), 1)
                    pt = jnp.where(kpos <= qpos, pt, 0.0)
                dv_sc[:, lanes] += _dot(pt.astype(MXU_DTYPE), _only_head(dop, hb))
                dst = pt * (_dot_nt(_only_head(vp, hb), dop) - dl_ref[row, :])
                dc_sc[...] -= jnp.where(lane == h, jnp.sum(dst, axis=1, keepdims=True), 0.0)
                dcq_ref[i, row, :] += jnp.sum(dst, axis=0, keepdims=True)
                dsb = dst.astype(MXU_DTYPE)
                dk_sc[:, lanes] += _dot(dsb, _only_head(qp, hb))
                rows = _head_rows(h)
                dq_ref[i, rows, :] += _dot(kt_ref[rows, :], dsb)

        @pl.when(i > j)
        def _():
            @pl.loop(0, N_HEADS // 2)
            def _(pp):
                pair_step(pp, False)

        @pl.when(i == j)
        def _():
            @pl.loop(0, N_HEADS // 2)
            def _(pp):
                pair_step(pp, True)

        @pl.when(i == n - 1)
        def _():
            dk_ref[...] = dk_sc[...]
            dv_ref[...] = dv_sc[...]
            dc_ref[...] = dc_sc[...]

        pl.when(p == len(pairs) - 1)(chip_finish)

    qi = lambda p, it_, jt_: (it_[p], 0)
    kj = lambda p, it_, jt_: (jt_[p], 0)
    row_i = lambda p, it_, jt_: (0, it_[p])
    gs = pltpu.PrefetchScalarGridSpec(
        num_scalar_prefetch=2, grid=(len(pairs),),
        in_specs=[pl.BlockSpec((t, ATTN_W), qi), pl.BlockSpec((t, ATTN_W), kj),
                  pl.BlockSpec((ATTN_W, t), lambda p, it_, jt_: (0, jt_[p])),
                  pl.BlockSpec((t, ATTN_W), kj), pl.BlockSpec((t, ATTN_W), qi),
                  pl.BlockSpec((HEAD_ROWS, t), row_i), pl.BlockSpec((t, 128), kj),
                  pl.BlockSpec((HEAD_ROWS, t), row_i), pl.BlockSpec((HEAD_ROWS, t), row_i)] + [ANY] * nw,
        out_specs=[pl.BlockSpec((n, ATTN_W, t), lambda p, it_, jt_: (0, 0, 0)),
                   pl.BlockSpec((t, ATTN_W), kj), pl.BlockSpec((t, ATTN_W), kj),
                   pl.BlockSpec((t, 128), kj),
                   pl.BlockSpec((n, HEAD_ROWS, t), lambda p, it_, jt_: (0, 0, 0))] + [ANY] * nw,
        scratch_shapes=[pltpu.VMEM((t, ATTN_W), F32), pltpu.VMEM((t, ATTN_W), F32),
                        pltpu.VMEM((t, 128), F32), pltpu.SemaphoreType.DMA((nw, 3)), pltpu.SemaphoreType.DMA((nw, 3))])
    sd = jax.ShapeDtypeStruct
    out = pl.pallas_call(
        body, name="attn_bwd", grid_spec=gs,
        out_shape=[sd((n, ATTN_W, t), F32), sd((s, ATTN_W), F32), sd((s, ATTN_W), F32),
                   sd((s, 128), F32), sd((n, HEAD_ROWS, t), F32)] + [sd((3,) + a.shape[1:], a.dtype) for a in parts],
        compiler_params=_params(("arbitrary",)),
    )(it, jt, qs, k, k_t, v, do, c_rows, c_cols, lse, delta, *parts)
    return out[:5], out[5:]


def _forget_bwd(dc_rows, dc_cols, z_t, b_col):
    s = z_t.shape[1]
    nb = s // 128

    def body(dr_ref, dcc_ref, z_ref, b_ref, dz_ref, db_ref):
        lower = _tri(128, False)
        real = lax.broadcasted_iota(jnp.int32, (HEAD_ROWS, 128), 0) < N_HEADS

        tail = jnp.zeros((HEAD_ROWS, 1), F32)
        dbias = jnp.zeros((HEAD_ROWS, 1), F32)
        for m in range(nb):
            off = (nb - 1 - m) * 128
            dc = dr_ref[:, off:off + 128] + dcc_ref[off:off + 128, :].T[0:HEAD_ROWS, :]
            dlf = _dot_exact(dc, lower, 3) + tail
            dz = dlf * jax.nn.sigmoid(-(z_ref[0:HEAD_ROWS, off:off + 128] + b_ref[...]))
            dz = jnp.where(real, dz, 0.0)
            dz_ref[off:off + 128, :] = _rows_to_cols(dz)
            tail = tail + jnp.sum(dc, axis=1, keepdims=True)
            dbias = dbias + jnp.sum(dz, axis=1, keepdims=True)
        db_ref[...] = jnp.broadcast_to(dbias, db_ref.shape)

    return pl.pallas_call(
        body, name="forget_bwd",
        out_shape=[jax.ShapeDtypeStruct((s, 128), F32), jax.ShapeDtypeStruct((HEAD_ROWS, 128), F32)],
        compiler_params=_params())(dc_rows, dc_cols, z_t, b_col)


def _inproj_bwd(dz, gate_c, u, conv_w, dq, dk, dv, dzf, db, x, dx2, g_pre, w_pad, tm):
    s = x.shape[0]
    nt = s // tm
    assert dq.shape == (nt, ATTN_W, tm)

    def body(dz_ref, dzn_ref, c_ref, u_ref, cp_ref, up_ref, cw_ref, dq_ref, dk_ref, dv_ref, dzf_ref, db_ref,
             x_ref, dx2_ref, g_ref, w_ref, gx_ref, dp_ref, dg_ref, dcw_ref):
        i = pl.program_id(0)
        first = i == 0
        last = i == nt - 1
        dzv = dz_ref[...]
        row = lax.broadcasted_iota(jnp.int32, dzv.shape, 0)
        n0 = jnp.where(last, 0.0, dzn_ref[0:1, :])
        n1 = jnp.where(last, 0.0, dzn_ref[1:2, :])
        dz1 = jnp.where(row == tm - 1, n0, pltpu.roll(dzv, tm - 1, 0))
        dz2 = jnp.where(row == tm - 1, n1, jnp.where(row == tm - 2, n0, pltpu.roll(dzv, tm - 2, 0)))
        dcu = cw_ref[2:3, :] * dzv + cw_ref[1:2, :] * dz1 + cw_ref[0:1, :] * dz2
        cv = c_ref[...]
        uv = u_ref[...]
        cu = cv * uv
        cu1, cu2 = _shift_down(cu, (cp_ref, up_ref), first)

        @pl.when(first)
        def _():
            dcw_ref[...] = jnp.zeros_like(dcw_ref)

        dcw_ref[0:1, :] += jnp.sum(dzv * cu2, axis=0, keepdims=True)
        dcw_ref[1:2, :] += jnp.sum(dzv * cu1, axis=0, keepdims=True)
        dcw_ref[2:3, :] += jnp.sum(dzv * cu, axis=0, keepdims=True)

        dp_ref[:, 0:512] = (dq_ref[0].T * Q_SCALE).astype(MXU_DTYPE)
        dp_ref[:, 512:1024] = dk_ref[...].astype(MXU_DTYPE)
        dp_ref[:, 1024:OFF_F] = dv_ref[...].astype(MXU_DTYPE)
        dp_ref[:, OFF_F:OFF_B] = dzf_ref[...].astype(MXU_DTYPE)
        dp_ref[:, OFF_B:OFF_C] = db_ref[...].astype(MXU_DTYPE)
        dp_ref[:, OFF_C:OFF_U] = (dcu * uv).astype(MXU_DTYPE)
        dp_ref[:, OFF_U:IN_PAD] = (dcu * cv).astype(MXU_DTYPE)
        dh = _dot_nt(dp_ref[...], w_ref[...])
        xv = x_ref[...]
        r1 = _rms(xv)
        nx = xv * r1
        _acc_rows(dg_ref, first, dh * nx)
        gx_ref[...] = dx2_ref[...] + _norm_bwd(dh, nx, r1, g_ref[...])

    prev = pl.BlockSpec((8, 512), lambda i: (jnp.maximum(i * (tm // 8) - 1, 0), 0))
    nxt = pl.BlockSpec((8, 512), lambda i: (jnp.minimum((i + 1) * (tm // 8), s // 8 - 1), 0))
    sd = jax.ShapeDtypeStruct
    return pl.pallas_call(
        body, name="inproj_bwd", grid=(nt,),
        in_specs=[_tok(tm, 512), nxt, _tok(tm, 512), _tok(tm, 512), prev, prev, _whole((3, 512)),
                  pl.BlockSpec((1, ATTN_W, tm), lambda i: (i, 0, 0)), _tok(tm, 512), _tok(tm, 512), _tok(tm, 128),
                  _tok(tm, 512),
                  _tok(tm, D_MODEL), _tok(tm, D_MODEL), _whole((1, D_MODEL)), _whole((D_MODEL, IN_PAD), single=True)],
        out_specs=[_tok(tm, D_MODEL), _tok(tm, IN_PAD), _whole((1, D_MODEL)), _whole((8, 512))],
        out_shape=[sd((s, D_MODEL), F32), sd((s, IN_PAD), MXU_DTYPE), sd((1, D_MODEL), F32), sd((8, 512), F32)],
        compiler_params=_params(("arbitrary",)),
    )(dz, dz, gate_c, u, gate_c, u, conv_w, dq, dk, dv, dzf, db, x, dx2, g_pre, w_pad)


def _tile(s, want):
    return want if s % want == 0 else s


def _halves(a):
    return a.reshape(2, a.shape[0] // 2, a.shape[1])


def _device_step(x, target, w, c_idx, me_idx):
    s = x.shape[0]
    tm = _tile(s, 512)
    tf = _tile(s, 256)
    ta = _tile(s, 512)
    tkk = _tile(s, 2048)
    rows = 16
    gidx = np.arange(512) // HEAD_DIM
    gmat = jnp.asarray(gidx[:, None] == gidx[None, :], MXU_DTYPE)
    sel = jnp.asarray(gidx[:, None] == np.arange(128)[None, :], MXU_DTYPE)
    g_mix_pre, g_mix_post, g_ffn_pre, g_ffn_post = w["g_mix_pre"], w["g_mix_post"], w["g_ffn_pre"], w["g_ffn_post"]
    g_attn, g_conv, b_forget = w["g_attn_out"], w["g_conv_out"], w["b_forget"]
    shard = {n: _halves(w[n][0].astype(MXU_DTYPE)) for n in BIG}

    g_in, conv_all = _gather_weights([shard["w_in"]], w["conv_w"][0])
    w_in_full = jnp.transpose(g_in.reshape(N_CHIPS, D_MODEL, IN_W // N_CHIPS), (1, 0, 2)).reshape(D_MODEL, IN_W)
    w_pad = jnp.concatenate([w_in_full[:, :OFF_F + N_HEADS], jnp.zeros((D_MODEL, OFF_B - OFF_F - N_HEADS), MXU_DTYPE),
                             w_in_full[:, OFF_F + N_HEADS:]], axis=1)
    conv_w = jnp.transpose(conv_all, (1, 0, 2)).reshape(3, CONV_W)

    w_kvz_t = jnp.transpose(w_pad[:, 512:OFF_B])
    h1, qs, k, v, k_t, v_t, z_t, gate_b, gate_c, u = _inproj_fwd(x, g_mix_pre, w_pad, w_kvz_t, tm)
    b_col = jnp.pad(jnp.transpose(b_forget), ((0, HEAD_ROWS - N_HEADS), (0, 0)))
    c_rows, c_cols = _forget_fwd(z_t, b_col)
    o_attn, lse, (g_out, g_gu, g_dn) = _attn_fwd(qs, k, v_t, c_rows, c_cols, ta,
                                                 [shard["w_out"], shard["w_gate_up"], shard["w_down"]])
    w_out = g_out.reshape(D_MODEL, D_MODEL)
    w_gu = g_gu.reshape(N_CHIPS, D_MODEL, FF_PIECE)
    w_dn = g_dn.reshape(2, FF_PIECE, D_MODEL)
    x2, merged, y, z = _mixer_fwd(x, o_attn, gate_b, gate_c, u, conv_w, g_attn, g_conv, w_out, g_mix_post, gmat, tm)
    h2, g, up, a, ff, dout, loss_acc = _ffn_fwd(x2, target, g_ffn_pre, w_gu, w_dn, g_ffn_post, tf)

    dx2, dff, dgu, dg_ffn_post, dg_ffn_pre = _ffn_bwd(dout, ff, x2, g, up, g_ffn_post, g_ffn_pre, w_gu, w_dn, tf)
    dw_dn = _tn_matmul(a, dff, FF_PIECE, 512, tkk, "dw_down").reshape(N_CHIPS, 2, D_FF // (2 * N_CHIPS), D_MODEL)
    dw_gu = _tn_matmul(h2, dgu, 1024, 512, tkk, "dw_gate_up").reshape(2, D_MODEL // 2, 2 * D_FF)
    (dy, d_o, d_b, dz, delta, dg_mix_post, dg_attn, dg_conv), (a_gu, a_dn) = _mixer_bwd(
        dx2, y, o_attn, gate_b, z, g_mix_post, g_attn, g_conv, w_out, gmat, sel, tm, [dw_gu, dw_dn], ["cols", "rows"])
    dw_out = _tn_matmul(merged, dy, 1024, 512, tkk, "dw_out")
    sum_gu = _pair_sum(c_idx, dw_gu, "cols", a_gu, "pair_sum_w_gate_up")
    sum_dn = _pair_sum(c_idx, dw_dn, "rows", a_dn, "pair_sum_w_down")
    (dq_t, dk, dv, dc_cols, dcq), (r_gu, r_dn) = _attn_bwd(
        qs, k, k_t, v, d_o, c_rows, c_cols, lse, delta, ta, [sum_gu[1], sum_dn[1]])
    dc_rows = jnp.transpose(dcq, (1, 0, 2)).reshape(HEAD_ROWS, s)
    dzf, db_f = _forget_bwd(dc_rows, dc_cols, z_t, b_col)
    grad_x, dproj, dg_mix_pre, dcw = _inproj_bwd(dz, gate_c, u, conv_w, dq_t, dk, dv, dzf, d_b,
                                                 x, dx2, g_mix_pre, w_pad, tm)
    dw_pad = _tn_matmul(h1, dproj, 1024, 640, tkk, "dw_in")
    dw_in = jnp.concatenate([dw_pad[:, :OFF_F + N_HEADS], dw_pad[:, OFF_B:]], axis=1)
    dw_in = jnp.transpose(dw_in.reshape(D_MODEL, N_CHIPS, IN_W // N_CHIPS), (1, 0, 2))
    dw_in = dw_in.reshape(N_CHIPS, 2, D_MODEL // 2, IN_W // N_CHIPS)
    dw_out = dw_out.reshape(N_CHIPS, 2, D_MODEL // (2 * N_CHIPS), D_MODEL)

    a_in, a_out = _pair_exchange([dw_in, dw_out])
    sum_in = _pair_sum(c_idx, dw_in, "rows", a_in, "pair_sum_w_in")
    sum_out = _pair_sum(c_idx, dw_out, "rows", a_out, "pair_sum_w_out")
    small = dict(b_forget=db_f[:N_HEADS, 0], g_attn_out=dg_attn, g_conv_out=dg_conv, g_mix_pre=dg_mix_pre,
                 g_mix_post=dg_mix_post, g_ffn_pre=dg_ffn_pre, g_ffn_post=dg_ffn_post)
    r_in, r_out, small_all = _chip_exchange([sum_in[1], sum_out[1]], _pack_small(small, dcw[:3], loss_acc[0, 0]))
    totals = [_chip_sum(me_idx, sb[0], r, "chip_sum_" + n)
              for n, sb, r in zip(BIG, (sum_in, sum_out, sum_gu, sum_dn), (r_in, r_out, r_gu, r_dn))]
    return grad_x, totals, _pair_share(totals), small_all


BIG = ("w_in", "w_out", "w_gate_up", "w_down")
ANY = pl.BlockSpec(memory_space=pl.ANY)


def _place():
    x, y, c = lax.axis_index("x"), lax.axis_index("y"), lax.axis_index("c")
    others = [(1 - x, y), (x, 1 - y), (1 - x, 1 - y)]
    return x, y, c, 2 * x + y, others, [2 * px + py for px, py in others]


def _remote(src, dst, send, recv, dev):
    return pltpu.make_async_remote_copy(src_ref=src, dst_ref=dst, send_sem=send, recv_sem=recv,
                                        device_id=dev, device_id_type=MESH_ID)


def _gather_stages(sh, outs, send, recv):
    x, y, c, me, others, chips = _place()
    sib = (x, y, 1 - c)
    every = [(w, kk) for w in range(len(sh)) for kk in range(3)]

    def first(w, kk):
        return _remote(sh[w].at[c], outs[w].at[me, c], send.at[w, kk], recv.at[w, kk], (*others[kk], c))

    def landed(w, kk):
        r = outs[w].at[chips[kk], c]
        return _remote(r, r, send.at[w, kk], recv.at[w, kk], (*others[kk], c))

    def onward(w, kk, half):
        r = outs[w].at[chips[kk], half]
        return _remote(r, r, send.at[w, 3 + kk], recv.at[w, 3 + kk], sib)

    def start():
        for w, kk in every:
            first(w, kk).start()

    def forward():
        for w, kk in every:
            landed(w, kk).wait_recv()
            onward(w, kk, c).start()

    def finish():
        for w, kk in every:
            onward(w, kk, 1 - c).wait_recv()
        for w, kk in every:
            first(w, kk).wait_send()
            onward(w, kk, c).wait_send()

    return start, forward, finish


def _pair_piece(ref, kind, p, half):
    if kind == "rows":
        return ref.at[p, half]
    cols = ref.shape[2] // N_CHIPS
    return ref.at[half, :, pl.ds(p * cols, cols)]


def _pair_stages(g, kinds, a, send, recv):
    x, y, c, _, _, _ = _place()
    copies = [_remote(_pair_piece(g[w], kinds[w], p, 1 - c), a[w].at[p], send.at[w, p], recv.at[w, p], (x, y, 1 - c))
              for w in range(len(g)) for p in range(N_CHIPS)]

    def start():
        for cp in copies:
            cp.start()

    def finish():
        for cp in copies:
            cp.wait()

    return start, finish


def _chip_stages(pb, rcv, send, recv):
    x, y, c, _, others, chips = _place()
    copies = [_remote(pb[w].at[chips[kk]], rcv[w].at[kk], send.at[w, kk], recv.at[w, kk], (*others[kk], c))
              for w in range(len(pb)) for kk in range(3)]

    def start():
        for cp in copies:
            cp.start()

    def finish():
        for cp in copies:
            cp.wait()

    return start, finish


def _gather_weights(shards, conv_w):
    n = len(shards)

    def body(*refs):
        sh, cw, outs, cwo = refs[:n], refs[n], refs[n + 1:2 * n + 1], refs[2 * n + 1]
        send, recv = refs[2 * n + 2:]
        x, y, c, me, others, chips = _place()
        start, forward, finish = _gather_stages(sh, outs, send, recv)
        start()
        small = [_remote(cw, cwo.at[me], send.at[n, kk], recv.at[n, kk], (*others[kk], c)) for kk in range(3)]
        for cp in small:
            cp.start()
        forward()
        for kk in range(3):
            _remote(cw, cwo.at[chips[kk]], send.at[n, kk], recv.at[n, kk], (*others[kk], c)).wait_recv()
        finish()
        for cp in small:
            cp.wait_send()

    out_shape = [jax.ShapeDtypeStruct((N_CHIPS,) + s.shape, s.dtype) for s in shards]
    out_shape.append(jax.ShapeDtypeStruct((N_CHIPS,) + conv_w.shape, conv_w.dtype))
    got = pl.pallas_call(
        body, name="gather_weights", in_specs=[ANY] * (n + 1), out_specs=[ANY] * (n + 1), out_shape=out_shape,
        scratch_shapes=[pltpu.SemaphoreType.DMA((n + 1, 6)), pltpu.SemaphoreType.DMA((n + 1, 6))],
    )(*shards, conv_w)
    me = 2 * lax.axis_index("x") + lax.axis_index("y")
    return [lax.dynamic_update_index_in_dim(g, own, me, 0) for g, own in zip(got, list(shards) + [conv_w])]


def _pair_exchange(grads):
    n = len(grads)

    def body(*refs):
        g, a = refs[:n], refs[n:2 * n]
        send, recv = refs[2 * n:]
        x, y, c, _, _, _ = _place()
        sib = (x, y, 1 - c)
        copies = [_remote(g[w].at[p, 1 - c], a[w].at[p], send.at[w, p], recv.at[w, p], sib)
                  for w in range(n) for p in range(N_CHIPS)]
        for cp in copies:
            cp.start()
        for cp in copies:
            cp.wait()

    return pl.pallas_call(
        body, name="pair_exchange", in_specs=[ANY] * n, out_specs=[ANY] * n,
        out_shape=[jax.ShapeDtypeStruct((N_CHIPS,) + g.shape[2:], g.dtype) for g in grads],
        scratch_shapes=[pltpu.SemaphoreType.DMA((n, N_CHIPS)), pltpu.SemaphoreType.DMA((n, N_CHIPS))],
    )(*grads)


def _pair_sum(c_idx, g, kind, a, name):
    _, half, cols = a.shape
    if kind == "rows":
        mine = pl.BlockSpec((1, 1, half, cols), lambda p, cr: (p, cr[0], 0, 0))
    else:
        mine = pl.BlockSpec((1, half, cols), lambda p, cr: (cr[0], 0, p))

    def body(c_ref, g_ref, a_ref, pf_ref, pb_ref):
        tot = (g_ref[0, 0] if kind == "rows" else g_ref[0]) + a_ref[0]
        pf_ref[0] = tot
        pb_ref[0] = tot.astype(BF16)

    gs = pltpu.PrefetchScalarGridSpec(
        num_scalar_prefetch=1, grid=(N_CHIPS,),
        in_specs=[mine,
                  pl.BlockSpec((1, half, cols), lambda p, cr: (p, 0, 0))],
        out_specs=[pl.BlockSpec((1, half, cols), lambda p, cr: (p, 0, 0)),
                   pl.BlockSpec((1, half, cols), lambda p, cr: (p, 0, 0))])
    return pl.pallas_call(
        body, name=name, grid_spec=gs,
        out_shape=[jax.ShapeDtypeStruct((N_CHIPS, half, cols), F32), jax.ShapeDtypeStruct((N_CHIPS, half, cols), BF16)],
        compiler_params=_params(("arbitrary",)),
    )(c_idx, g, a)


def _chip_exchange(parts, small):
    n = len(parts)

    def body(*refs):
        pb, sm, rcv, smg = refs[:n], refs[n], refs[n + 1:2 * n + 1], refs[2 * n + 1]
        send, recv, ssend, srecv, loc = refs[2 * n + 2:]
        x, y, c, _, others, chips = _place()
        mine = 4 * x + 2 * y + c
        own = pltpu.make_async_copy(sm, smg.at[mine], loc)
        own.start()
        copies = [_remote(pb[w].at[chips[kk]], rcv[w].at[kk], send.at[w, kk], recv.at[w, kk], (px, py, c))
                  for w in range(n) for kk, (px, py) in enumerate(others)]
        for r in range(1, 8):
            peer = (1 - x if r & 4 else x, 1 - y if r & 2 else y, 1 - c if r & 1 else c)
            copies.append(_remote(sm, smg.at[mine], ssend.at[r - 1], srecv.at[r - 1], peer))
        for cp in copies:
            cp.start()
        for w in range(n):
            for kk, (px, py) in enumerate(others):
                _remote(pb[w].at[chips[kk]], rcv[w].at[kk], send.at[w, kk], recv.at[w, kk], (px, py, c)).wait_recv()
        for r in range(1, 8):
            px, py, pc = (1 - x if r & 4 else x, 1 - y if r & 2 else y, 1 - c if r & 1 else c)
            _remote(sm, smg.at[4 * px + 2 * py + pc], ssend.at[r - 1], srecv.at[r - 1], (px, py, pc)).wait_recv()
        for cp in copies:
            cp.wait_send()
        own.wait()

    out_shape = [jax.ShapeDtypeStruct((3,) + p.shape[1:], p.dtype) for p in parts]
    out_shape.append(jax.ShapeDtypeStruct((8,) + small.shape, small.dtype))
    return pl.pallas_call(
        body, name="chip_exchange", in_specs=[ANY] * (n + 1), out_specs=[ANY] * (n + 1), out_shape=out_shape,
        scratch_shapes=[pltpu.SemaphoreType.DMA((n, 3)), pltpu.SemaphoreType.DMA((n, 3)),
                        pltpu.SemaphoreType.DMA((7,)), pltpu.SemaphoreType.DMA((7,)), pltpu.SemaphoreType.DMA(())],
    )(*parts, small)


def _chip_sum(me_idx, pf, rcv, name):
    _, half, cols = pf.shape

    def body(me_ref, pf_ref, r_ref, t_ref):
        t_ref[...] = ((pf_ref[0] + r_ref[0].astype(F32)) + r_ref[1].astype(F32)) + r_ref[2].astype(F32)

    gs = pltpu.PrefetchScalarGridSpec(
        num_scalar_prefetch=1, grid=(1,),
        in_specs=[pl.BlockSpec((1, half, cols), lambda i, mr: (mr[0], 0, 0)),
                  pl.BlockSpec((3, half, cols), lambda i, mr: (0, 0, 0))],
        out_specs=pl.BlockSpec((half, cols), lambda i, mr: (0, 0)))
    return pl.pallas_call(
        body, name=name, grid_spec=gs, out_shape=jax.ShapeDtypeStruct((half, cols), F32),
        compiler_params=_params(("arbitrary",)),
    )(me_idx, pf, rcv)


def _pair_share(totals):
    n = len(totals)

    def body(*refs):
        t, g = refs[:n], refs[n:2 * n]
        send, recv = refs[2 * n:]
        x, y, c, _, _, _ = _place()
        copies = [_remote(t[w], g[w], send.at[w], recv.at[w], (x, y, 1 - c)) for w in range(n)]
        for cp in copies:
            cp.start()
        for cp in copies:
            cp.wait()

    return pl.pallas_call(
        body, name="pair_share", in_specs=[ANY] * n, out_specs=[ANY] * n,
        out_shape=[jax.ShapeDtypeStruct(t.shape, t.dtype) for t in totals],
        scratch_shapes=[pltpu.SemaphoreType.DMA((n,)), pltpu.SemaphoreType.DMA((n,))],
    )(*totals)


def _adamw_math(w, g, m, v):
    m = ADAM_B1 * m + (1.0 - ADAM_B1) * g
    v = ADAM_B2 * v + (1.0 - ADAM_B2) * (g * g)
    m_hat = m / (1.0 - ADAM_B1 ** ADAM_STEP)
    v_hat = v / (1.0 - ADAM_B2 ** ADAM_STEP)
    delta = -ADAM_LR * (m_hat / (jnp.sqrt(v_hat) + ADAM_EPS) + ADAM_WD * w)
    return delta, m, v


def _adamw(c_idx, w, mine, theirs, m, v, nb, name):
    rows, cols = w.shape
    tr = rows // (2 * nb)

    def body(c_ref, w_ref, a_ref, b_ref, m_ref, v_ref, g_ref, d_ref, nm_ref, nv_ref):
        g = jnp.where(pl.program_id(0) == c_ref[0], a_ref[...], b_ref[...])
        g_ref[...] = g
        d_ref[...], nm_ref[...], nv_ref[...] = _adamw_math(w_ref[...], g, m_ref[...], v_ref[...])

    full = pl.BlockSpec((tr, cols), lambda hh, i, cr: (hh * nb + i, 0))
    half = pl.BlockSpec((tr, cols), lambda hh, i, cr: (i, 0))
    gs = pltpu.PrefetchScalarGridSpec(num_scalar_prefetch=1, grid=(2, nb), in_specs=[full, half, half, full, full],
                                      out_specs=[full] * 4)
    return pl.pallas_call(
        body, name=name, grid_spec=gs, out_shape=[jax.ShapeDtypeStruct((rows, cols), F32)] * 4,
        compiler_params=_params(("arbitrary", "arbitrary")),
    )(c_idx, w, mine, theirs, m, v)


SMALL = ("g_mix_pre", "g_mix_post", "g_ffn_pre", "g_ffn_post")
SMALL_ALL = SMALL + ("g_attn_out", "g_conv_out", "conv_w", "b_forget")
SMALL_AT = {"g_mix_pre": (0, 0, 1024), "g_mix_post": (1, 0, 1024), "g_ffn_pre": (2, 0, 1024),
            "g_ffn_post": (3, 0, 1024), "g_attn_out": (4, 0, 512), "g_conv_out": (4, 512, 512),
            "b_forget": (7, 0, N_HEADS)}
CONV_AT = ((5, 0), (5, 512), (6, 0))
LOSS_AT = (6, 512)


def _pack_small(t, conv_full, loss_sum):
    conv = jnp.concatenate([conv_full.reshape(1, 3 * CONV_W), loss_sum.reshape(1, 1),
                            jnp.zeros((1, 2048 - 3 * CONV_W - 1), F32)], axis=1).reshape(2, 1024)
    return jnp.concatenate([t[n].reshape(1, 1024) for n in SMALL]
                           + [jnp.concatenate([t["g_attn_out"].reshape(1, 512), t["g_conv_out"].reshape(1, 512)], axis=1),
                              conv, jnp.pad(t["b_forget"].reshape(1, N_HEADS), ((0, 0), (0, 1024 - N_HEADS)))], axis=0)


def _small_update(me_idx, gathered, w, m, v):
    def body(me_ref, gg_ref, *refs):
        k = len(SMALL_ALL)
        w_refs, m_refs, v_refs = refs[:k], refs[k:2 * k], refs[2 * k:3 * k]
        loss_ref = refs[3 * k]
        outs = refs[3 * k + 1:3 * k + 1 + 4 * k]
        sums = refs[-1]
        g = gg_ref[0]
        for dev in range(1, 8):
            g = g + gg_ref[dev]
        sums[...] = g
        loss_ref[...] = sums[LOSS_AT[0]:LOSS_AT[0] + 1, LOSS_AT[1]:LOSS_AT[1] + 1]
        mine = pl.multiple_of(me_ref[0] * 128, 128)
        for idx, name in enumerate(SMALL_ALL):
            g_ref, d_ref, nm_ref, nv_ref = outs[4 * idx:4 * idx + 4]
            if name == "conv_w":
                for r, (row, lo) in enumerate(CONV_AT):
                    gr = sums[row:row + 1, pl.ds(lo + mine, 128)]
                    g_ref[0, r:r + 1, :] = gr
                    d_ref[0, r:r + 1, :], nm_ref[0, r:r + 1, :], nv_ref[0, r:r + 1, :] = _adamw_math(
                        w_refs[idx][0, r:r + 1, :], gr, m_refs[idx][0, r:r + 1, :], v_refs[idx][0, r:r + 1, :])
            else:
                row, lo, n = SMALL_AT[name]
                gr = sums[row:row + 1, lo:lo + n]
                g_ref[...] = gr
                d_ref[...], nm_ref[...], nv_ref[...] = _adamw_math(w_refs[idx][...], gr, m_refs[idx][...],
                                                                    v_refs[idx][...])

    def whole(a):
        nd = a.ndim
        return pl.BlockSpec(a.shape, lambda i, mr: (0,) * nd)

    ins = [t[n] for t in (w, m, v) for n in SMALL_ALL]
    out_shape = [jax.ShapeDtypeStruct((1, 1), F32)]
    for n in SMALL_ALL:
        out_shape += [jax.ShapeDtypeStruct(w[n].shape, F32)] * 4
    gs = pltpu.PrefetchScalarGridSpec(
        num_scalar_prefetch=1, grid=(1,), in_specs=[whole(gathered)] + [whole(a) for a in ins],
        out_specs=[whole(o) for o in out_shape], scratch_shapes=[pltpu.VMEM((8, 1024), F32)])
    out = pl.pallas_call(body, name="small_update", grid_spec=gs, out_shape=out_shape,
                         compiler_params=_params(("arbitrary",)))(me_idx, gathered, *ins)
    return out[0], {n: out[1 + 4 * i:5 + 4 * i] for i, n in enumerate(SMALL_ALL)}


def kernel(x, w_in, b_forget, conv_w, g_attn_out, g_conv_out, w_out, g_mix_pre, g_mix_post, w_gate_up, w_down, g_ffn_pre, g_ffn_post, loss_target, m_w_in, m_b_forget, m_conv_w, m_g_attn_out, m_g_conv_out, m_w_out, m_g_mix_pre, m_g_mix_post, m_w_gate_up, m_w_down, m_g_ffn_pre, m_g_ffn_post, v_w_in, v_b_forget, v_conv_w, v_g_attn_out, v_g_conv_out, v_w_out, v_g_mix_pre, v_g_mix_post, v_w_gate_up, v_w_down, v_g_ffn_pre, v_g_ffn_post):
    w = dict(w_in=w_in, b_forget=b_forget, conv_w=conv_w, g_attn_out=g_attn_out, g_conv_out=g_conv_out, w_out=w_out,
             g_mix_pre=g_mix_pre, g_mix_post=g_mix_post, w_gate_up=w_gate_up, w_down=w_down, g_ffn_pre=g_ffn_pre,
             g_ffn_post=g_ffn_post)
    m = dict(w_in=m_w_in, b_forget=m_b_forget, conv_w=m_conv_w, g_attn_out=m_g_attn_out, g_conv_out=m_g_conv_out,
             w_out=m_w_out, g_mix_pre=m_g_mix_pre, g_mix_post=m_g_mix_post, w_gate_up=m_w_gate_up, w_down=m_w_down,
             g_ffn_pre=m_g_ffn_pre, g_ffn_post=m_g_ffn_post)
    v = dict(w_in=v_w_in, b_forget=v_b_forget, conv_w=v_conv_w, g_attn_out=v_g_attn_out, g_conv_out=v_g_conv_out,
             w_out=v_w_out, g_mix_pre=v_g_mix_pre, g_mix_post=v_g_mix_post, w_gate_up=v_w_gate_up, w_down=v_w_down,
             g_ffn_pre=v_g_ffn_pre, g_ffn_post=v_g_ffn_post)
    cx, cy, cc = lax.axis_index("x"), lax.axis_index("y"), lax.axis_index("c")
    me = 2 * cx + cy
    c_idx = cc.astype(jnp.int32).reshape(1)
    me_idx = me.astype(jnp.int32).reshape(1)

    grad_x, totals, shared, small_all = _device_step(x[0], loss_target[0], w, c_idx, me_idx)

    gsum, delta, new_m, new_v = {}, {}, {}, {}
    for n, mine, theirs in zip(BIG, totals, shared):
        gs, d, nm, nv = _adamw(c_idx, w[n][0], mine, theirs, m[n][0], v[n][0], 2, "adamw_" + n)
        gsum[n], delta[n], new_m[n], new_v[n] = gs[None], d[None], nm[None], nv[None]
    loss_sum, small_new = _small_update(me_idx, small_all, w, m, v)
    for n in SMALL_ALL:
        gsum[n], delta[n], new_m[n], new_v[n] = small_new[n]
    loss = 0.5 * loss_sum[0, 0]

    order = ("w_in", "b_forget", "conv_w", "g_attn_out", "g_conv_out", "w_out", "g_mix_pre", "g_mix_post",
             "w_gate_up", "w_down", "g_ffn_pre", "g_ffn_post")
    return (loss, grad_x[None], *[gsum[n] for n in order], *[delta[n] for n in order],
            *[new_m[n] for n in order], *[new_v[n] for n in order])
```

```python
import functools

import jax
import jax.numpy as jnp
import numpy as np
from jax import lax
from jax.experimental import pallas as pl
from jax.experimental.pallas import tpu as pltpu

F32 = jnp.float32
BF16 = jnp.bfloat16
MXU_DTYPE = jnp.bfloat16

D_MODEL = 1024
HEAD_DIM = 64
N_HEADS = 8
ATTN_W = 512
CONV_W = 512
D_FF = 2816
FF_PIECE = 1408
EPS = 1e-6
Q_SCALE = HEAD_DIM ** -0.5

OFF_F = 1536
OFF_B = 1664
OFF_C = 2176
OFF_U = 2688
IN_PAD = 3200
IN_W = 3080
N_CHIPS = 4

ADAM_LR = 0.001
ADAM_B1 = 0.9
ADAM_B2 = 0.999
ADAM_EPS = 1e-08
ADAM_WD = 0.01
ADAM_STEP = 10

VMEM_LIMIT_V7X = 56 * 1024 * 1024
MESH_ID = pl.DeviceIdType.MESH


def _params(sem=None, vmem=VMEM_LIMIT_V7X):
    kw = {"vmem_limit_bytes": vmem}
    if sem is not None:
        kw["dimension_semantics"] = sem
    return pltpu.CompilerParams(**kw)


def _dot(a, b):
    return jnp.dot(a, b, preferred_element_type=F32)


def _dot_nt(a, b):
    return lax.dot_general(a, b, (((1,), (1,)), ((), ())), preferred_element_type=F32)


def _dot_exact(x, ones, parts):
    if ones.dtype == F32:
        return _dot(x, ones)
    acc = None
    rem = x
    for _ in range(parts):
        piece = rem.astype(BF16)
        rem = rem - piece.astype(F32)
        term = _dot(piece, ones)
        acc = term if acc is None else acc + term
    return acc


def _rms(v):
    return lax.rsqrt(jnp.mean(v * v, axis=-1, keepdims=True) + EPS)


def _tok(tm, w):
    return pl.BlockSpec((tm, w), lambda i: (i, 0))


def _whole(shape, single=False):
    nd = len(shape)
    if single:
        return pl.BlockSpec(shape, lambda i: (0,) * nd, pipeline_mode=pl.Buffered(1))
    return pl.BlockSpec(shape, lambda i: (0,) * nd)


def _feat(rows, tm):
    return pl.BlockSpec((rows, tm), lambda i: (0, i))


def _inproj_fwd(x, g_pre, w_t, tm):
    s = x.shape[0]

    def body(x_ref, g_ref, w_ref, h_ref, q_ref, k_ref, v_ref, kt_ref, vt_ref, zt_ref, b_ref, c_ref, u_ref):
        xv = x_ref[...]
        h = ((xv * _rms(xv)) * g_ref[...]).astype(MXU_DTYPE)
        h_ref[...] = h

        def proj(lo, hi):
            return _dot_nt(h, w_ref[lo:hi, :])

        q_ref[...] = (proj(0, 512) * Q_SCALE).astype(MXU_DTYPE)
        kt = _dot_nt(w_ref[512:1024, :], h)
        vt = _dot_nt(w_ref[1024:OFF_F, :], h)
        kt_ref[...] = kt.astype(MXU_DTYPE)
        vt_ref[...] = vt.astype(MXU_DTYPE)
        k_ref[...] = kt.T.astype(MXU_DTYPE)
        v_ref[...] = vt.T.astype(MXU_DTYPE)
        zt_ref[...] = _dot_nt(w_ref[OFF_F:OFF_B, :], h)
        b_ref[...] = proj(OFF_B, OFF_C)
        c_ref[...] = proj(OFF_C, OFF_U)
        u_ref[...] = proj(OFF_U, IN_PAD)

    sd = jax.ShapeDtypeStruct
    return pl.pallas_call(
        body, name="inproj_fwd", grid=(s // tm,),
        in_specs=[_tok(tm, D_MODEL), _whole((1, D_MODEL)), _whole((IN_PAD, D_MODEL), single=True)],
        out_specs=[_tok(tm, D_MODEL), _tok(tm, 512), _tok(tm, 512), _tok(tm, 512), _feat(512, tm), _feat(512, tm),
                   _feat(128, tm), _tok(tm, 512), _tok(tm, 512), _tok(tm, 512)],
        out_shape=[sd((s, D_MODEL), MXU_DTYPE), sd((s, 512), MXU_DTYPE), sd((s, 512), MXU_DTYPE),
                   sd((s, 512), MXU_DTYPE), sd((512, s), MXU_DTYPE), sd((512, s), MXU_DTYPE), sd((128, s), F32),
                   sd((s, 512), F32), sd((s, 512), F32), sd((s, 512), F32)],
        compiler_params=_params(("arbitrary",)),
    )(x, g_pre, w_t)


def _tri(n, upper):
    r = lax.broadcasted_iota(jnp.int32, (n, n), 0)
    c = lax.broadcasted_iota(jnp.int32, (n, n), 1)
    return ((r <= c) if upper else (r >= c)).astype(MXU_DTYPE)


HEAD_ROWS = 16


def _rows_to_cols(v):
    return jnp.concatenate([v, jnp.zeros((128 - HEAD_ROWS, 128), F32)], axis=0).T


def _forget_fwd(z_t, b_col):
    s = z_t.shape[1]
    nb = s // 128

    def body(z_ref, b_ref, c_ref, cc_ref):
        upper = _tri(128, True)

        carry = jnp.zeros((HEAD_ROWS, 1), F32)
        for n in range(nb):
            off = n * 128
            lf = jax.nn.log_sigmoid(z_ref[0:HEAD_ROWS, off:off + 128] + b_ref[...])
            cs = _dot_exact(lf, upper, 3) + carry
            c_ref[:, off:off + 128] = cs
            cc_ref[off:off + 128, :] = _rows_to_cols(cs)
            carry = carry + jnp.sum(lf, axis=1, keepdims=True)

    return pl.pallas_call(body, name="forget_fwd",
                          out_shape=[jax.ShapeDtypeStruct((HEAD_ROWS, s), F32), jax.ShapeDtypeStruct((s, 128), F32)],
                          compiler_params=_params())(z_t, b_col)


def _pair_lanes(pp):
    return pl.ds(pl.multiple_of(pp * 128, 128), 128)


def _head_rows(h):
    return pl.ds(pl.multiple_of(h * HEAD_DIM, HEAD_DIM), HEAD_DIM)


def _only_head(block, hb):
    lane = lax.broadcasted_iota(jnp.int32, block.shape, 1)
    return jnp.where((lane >= HEAD_DIM) if hb else (lane < HEAD_DIM), block, jnp.zeros_like(block))


def _head_col(cols, h):
    lane = lax.broadcasted_iota(jnp.int32, cols.shape, 1)
    return jnp.sum(jnp.where(lane == h, cols, 0.0), axis=1, keepdims=True)


def _attn_fwd(qs, k, v_t, c_rows, c_cols, t, shards):
    s = qs.shape[0]
    n = s // t
    pairs = [(i, j) for i in range(n) for j in range(i + 1)]
    it = jnp.asarray(np.array([p[0] for p in pairs], np.int32))
    jt = jnp.asarray(np.array([p[1] for p in pairs], np.int32))
    nw = len(shards)
    last = len(pairs) - 1
    mid = (2 * len(pairs)) // 3

    def body(it_ref, jt_ref, q_ref, k_ref, vt_ref, cq_ref, ck_ref, *rest):
        sh, (o_ref, lse_ref), got = rest[:nw], rest[nw:nw + 2], rest[nw + 2:2 * nw + 2]
        m_sc, l_sc, acc_sc, send, recv = rest[2 * nw + 2:]
        p = pl.program_id(0)
        i = it_ref[p]
        j = jt_ref[p]
        gather_start, gather_forward, gather_finish = _gather_stages(sh, got, send, recv)
        pl.when(p == 0)(gather_start)
        if mid < last:
            pl.when(p == mid)(gather_forward)

        @pl.when(j == 0)
        def _():
            m_sc[...] = jnp.full_like(m_sc, -1e30)
            l_sc[...] = jnp.ones_like(l_sc)
            acc_sc[...] = jnp.zeros_like(acc_sc)

        def pair_step(pp, diagonal):
            lanes = _pair_lanes(pp)
            kp = k_ref[:, lanes]
            qp = q_ref[:, lanes]
            ck_all = ck_ref[...]
            for hb in range(2):
                h = 2 * pp + hb
                row = pl.ds(h, 1)
                st = _dot_nt(_only_head(kp, hb), qp) + (cq_ref[row, :] - _head_col(ck_all, h))
                if diagonal:
                    kpos = lax.broadcasted_iota(jnp.int32, (t, t), 0)
                    qpos = lax.broadcasted_iota(jnp.int32, (t, t), 1)
                    st = jnp.where(kpos <= qpos, st, -1e30)
                m_prev = m_sc[row, :]
                m_new = jnp.maximum(m_prev, jnp.max(st, axis=0, keepdims=True))
                alpha = jnp.exp(m_prev - m_new)
                pt = jnp.exp(st - m_new)
                l_sc[row, :] = alpha * l_sc[row, :] + jnp.sum(pt, axis=0, keepdims=True)
                rows = _head_rows(h)
                acc_sc[rows, :] = acc_sc[rows, :] * alpha + _dot(vt_ref[rows, :], pt.astype(MXU_DTYPE))
                m_sc[row, :] = m_new

        @pl.when(j < i)
        def _():
            @pl.loop(0, N_HEADS // 2)
            def _(pp):
                pair_step(pp, False)

        @pl.when(j == i)
        def _():
            @pl.loop(0, N_HEADS // 2)
            def _(pp):
                pair_step(pp, True)
                sub = lax.broadcasted_iota(jnp.int32, (2 * HEAD_DIM, t), 0)
                l_pair = jnp.where(sub < HEAD_DIM, l_sc[pl.ds(2 * pp, 1), :], l_sc[pl.ds(2 * pp + 1, 1), :])
                o_t = acc_sc[pl.ds(pl.multiple_of(pp * 2 * HEAD_DIM, 2 * HEAD_DIM), 2 * HEAD_DIM), :] / l_pair
                o_ref[:, _pair_lanes(pp)] = o_t.T

            lse_ref[...] = m_sc[...] + jnp.log(l_sc[...])

        @pl.when(p == last)
        def _():
            if mid >= last:
                gather_forward()
            gather_finish()

    gs = pltpu.PrefetchScalarGridSpec(
        num_scalar_prefetch=2, grid=(len(pairs),),
        in_specs=[pl.BlockSpec((t, ATTN_W), lambda p, it_, jt_: (it_[p], 0)),
                  pl.BlockSpec((t, ATTN_W), lambda p, it_, jt_: (jt_[p], 0)),
                  pl.BlockSpec((ATTN_W, t), lambda p, it_, jt_: (0, jt_[p])),
                  pl.BlockSpec((HEAD_ROWS, t), lambda p, it_, jt_: (0, it_[p])),
                  pl.BlockSpec((t, 128), lambda p, it_, jt_: (jt_[p], 0))] + [ANY] * nw,
        out_specs=[pl.BlockSpec((t, ATTN_W), lambda p, it_, jt_: (it_[p], 0)),
                   pl.BlockSpec((HEAD_ROWS, t), lambda p, it_, jt_: (0, it_[p]))] + [ANY] * nw,
        scratch_shapes=[pltpu.VMEM((HEAD_ROWS, t), F32), pltpu.VMEM((HEAD_ROWS, t), F32), pltpu.VMEM((ATTN_W, t), F32),
                        pltpu.SemaphoreType.DMA((nw, 6)), pltpu.SemaphoreType.DMA((nw, 6))])
    o, lse, *got = pl.pallas_call(
        body, name="attn_fwd", grid_spec=gs,
        out_shape=[jax.ShapeDtypeStruct((s, ATTN_W), F32), jax.ShapeDtypeStruct((HEAD_ROWS, s), F32)]
        + [jax.ShapeDtypeStruct((N_CHIPS,) + a.shape, a.dtype) for a in shards],
        compiler_params=_params(("arbitrary",)),
    )(it, jt, qs, k, v_t, c_rows, c_cols, *shards)
    me = 2 * lax.axis_index("x") + lax.axis_index("y")
    return o, lse, [lax.dynamic_update_index_in_dim(g, own, me, 0) for g, own in zip(got, shards)]


def _shift_down(cur, prev_ref, first):
    row = lax.broadcasted_iota(jnp.int32, cur.shape, 0)
    p7 = jnp.where(first, 0.0, prev_ref[0][7:8, :] * prev_ref[1][7:8, :])
    p6 = jnp.where(first, 0.0, prev_ref[0][6:7, :] * prev_ref[1][6:7, :])
    s1 = jnp.where(row == 0, p7, pltpu.roll(cur, 1, 0))
    s2 = jnp.where(row == 0, p6, jnp.where(row == 1, p7, pltpu.roll(cur, 2, 0)))
    return s1, s2


def _group_ms(v, gmat):
    return _dot_exact(v, gmat, 2) * (1.0 / HEAD_DIM)


def _mixer_fwd(x, o_attn, gate_b, gate_c, u, conv_w, g_attn, g_conv, w_out, g_post, gmat, tm):
    s = x.shape[0]

    def body(x_ref, o_ref, b_ref, c_ref, u_ref, cp_ref, up_ref, cw_ref, ga_ref, gc_ref, wo_ref, gp_ref, gm_ref,
             x2_ref, mg_ref, y_ref, z_ref):
        i = pl.program_id(0)
        cu = c_ref[...] * u_ref[...]
        cu1, cu2 = _shift_down(cu, (cp_ref, up_ref), i == 0)
        z = cw_ref[0:1, :] * cu2 + cw_ref[1:2, :] * cu1 + cw_ref[2:3, :] * cu
        z_ref[...] = z
        cv = b_ref[...] * z
        ov = o_ref[...]
        gm = gm_ref[...]
        ma = ((ov * lax.rsqrt(_group_ms(ov * ov, gm) + EPS)) * ga_ref[...]).astype(MXU_DTYPE)
        mc = ((cv * lax.rsqrt(_group_ms(cv * cv, gm) + EPS)) * gc_ref[...]).astype(MXU_DTYPE)
        mg_ref[:, 0:ATTN_W] = ma
        mg_ref[:, ATTN_W:D_MODEL] = mc
        y = _dot(ma, wo_ref[0:ATTN_W, :]) + _dot(mc, wo_ref[ATTN_W:D_MODEL, :])
        y_ref[...] = y
        x2_ref[...] = x_ref[...] + (y * _rms(y)) * gp_ref[...]

    halo = pl.BlockSpec((8, 512), lambda i: (jnp.maximum(i * (tm // 8) - 1, 0), 0))
    sd = jax.ShapeDtypeStruct
    return pl.pallas_call(
        body, name="mixer_fwd", grid=(s // tm,),
        in_specs=[_tok(tm, D_MODEL), _tok(tm, 512), _tok(tm, 512), _tok(tm, 512), _tok(tm, 512), halo, halo,
                  _whole((3, 512)), _whole((1, 512)), _whole((1, 512)), _whole((D_MODEL, D_MODEL), single=True),
                  _whole((1, D_MODEL)), _whole((512, 512))],
        out_specs=[_tok(tm, D_MODEL), _tok(tm, D_MODEL), _tok(tm, D_MODEL), _tok(tm, 512)],
        out_shape=[sd((s, D_MODEL), F32), sd((s, D_MODEL), MXU_DTYPE), sd((s, D_MODEL), F32), sd((s, 512), F32)],
        compiler_params=_params(("arbitrary",)),
    )(x, o_attn, gate_b, gate_c, u, gate_c, u, conv_w, g_attn, g_conv, w_out, g_post, gmat)


def _ffn_fwd(x2, target, g_pre, w_gu, w_dn, g_post, tm):
    s = x2.shape[0]

    def body(x_ref, t_ref, gpre_ref, wgu_ref, wdn_ref, gpost_ref,
             h_ref, g_ref, up_ref, a_ref, ff_ref, dout_ref, loss_ref):
        xv = x_ref[...]
        h = ((xv * _rms(xv)) * gpre_ref[...]).astype(MXU_DTYPE)
        h_ref[...] = h
        ff = jnp.zeros((tm, D_MODEL), F32)
        for j in range(2):
            cols = slice(j * FF_PIECE, (j + 1) * FF_PIECE)
            g = _dot(h, wgu_ref[j])
            up = _dot(h, wgu_ref[2 + j])
            a = ((g * jax.nn.sigmoid(g)) * up).astype(MXU_DTYPE)
            g_ref[:, cols] = g
            up_ref[:, cols] = up
            a_ref[:, cols] = a
            ff = ff + _dot(a, wdn_ref[j])
        ff_ref[...] = ff
        err = (xv + (ff * _rms(ff)) * gpost_ref[...]) - t_ref[...]
        dout_ref[...] = err * (1.0 / D_MODEL)
        part = jnp.sum(jnp.mean(err * err, axis=-1, keepdims=True), axis=0, keepdims=True)

        @pl.when(pl.program_id(0) == 0)
        def _():
            loss_ref[...] = jnp.zeros_like(loss_ref)

        loss_ref[...] += part

    sd = jax.ShapeDtypeStruct
    return pl.pallas_call(
        body, name="ffn_fwd", grid=(s // tm,),
        in_specs=[_tok(tm, D_MODEL), _tok(tm, D_MODEL), _whole((1, D_MODEL)),
                  _whole((4, D_MODEL, FF_PIECE), single=True), _whole((2, FF_PIECE, D_MODEL), single=True),
                  _whole((1, D_MODEL))],
        out_specs=[_tok(tm, D_MODEL), _tok(tm, D_FF), _tok(tm, D_FF), _tok(tm, D_FF), _tok(tm, D_MODEL),
                   _tok(tm, D_MODEL), _whole((8, 128))],
        out_shape=[sd((s, D_MODEL), MXU_DTYPE), sd((s, D_FF), F32), sd((s, D_FF), F32), sd((s, D_FF), MXU_DTYPE),
                   sd((s, D_MODEL), F32), sd((s, D_MODEL), F32), sd((8, 128), F32)],
        compiler_params=_params(("arbitrary",)),
    )(x2, target, g_pre, w_gu, w_dn, g_post)


def _norm_bwd(dy, normed, rinv, gain):
    t = dy * gain
    return rinv * (t - normed * jnp.mean(t * normed, axis=-1, keepdims=True))


def _acc_rows(ref, first, val):
    @pl.when(first)
    def _():
        ref[...] = jnp.zeros_like(ref)

    ref[...] += jnp.sum(val, axis=0, keepdims=True)


def _ffn_bwd(dout, ff, x2, g, up, g_post, g_pre, w_gu, w_dn, tm):
    s = x2.shape[0]

    def body(do_ref, ff_ref, x_ref, g_ref, up_ref, gpost_ref, gpre_ref, wgu_ref, wdn_ref,
             dx_ref, dff_ref, dgu_ref, dgpost_ref, dgpre_ref):
        first = pl.program_id(0) == 0
        ffv = ff_ref[...]
        rf = _rms(ffv)
        n = ffv * rf
        do = do_ref[...]
        _acc_rows(dgpost_ref, first, do * n)
        dff = _norm_bwd(do, n, rf, gpost_ref[...]).astype(MXU_DTYPE)
        dff_ref[...] = dff
        dh = jnp.zeros((tm, D_MODEL), F32)
        for j in range(2):
            cols = slice(j * FF_PIECE, (j + 1) * FF_PIECE)
            da = _dot_nt(dff, wdn_ref[j])
            gv = g_ref[:, cols]
            sg = jax.nn.sigmoid(gv)
            dg = (da * up_ref[:, cols] * (sg * (1.0 + gv * (1.0 - sg)))).astype(MXU_DTYPE)
            du = (da * (gv * sg)).astype(MXU_DTYPE)
            dgu_ref[:, cols] = dg
            dgu_ref[:, D_FF + j * FF_PIECE:D_FF + (j + 1) * FF_PIECE] = du
            dh = dh + _dot_nt(dg, wgu_ref[j]) + _dot_nt(du, wgu_ref[2 + j])
        xv = x_ref[...]
        r2 = _rms(xv)
        nx = xv * r2
        _acc_rows(dgpre_ref, first, dh * nx)
        dx_ref[...] = do + _norm_bwd(dh, nx, r2, gpre_ref[...])

    sd = jax.ShapeDtypeStruct
    return pl.pallas_call(
        body, name="ffn_bwd", grid=(s // tm,),
        in_specs=[_tok(tm, D_MODEL), _tok(tm, D_MODEL), _tok(tm, D_MODEL), _tok(tm, D_FF), _tok(tm, D_FF),
                  _whole((1, D_MODEL)), _whole((1, D_MODEL)),
                  _whole((4, D_MODEL, FF_PIECE), single=True), _whole((2, FF_PIECE, D_MODEL), single=True)],
        out_specs=[_tok(tm, D_MODEL), _tok(tm, D_MODEL), _tok(tm, 2 * D_FF), _whole((1, D_MODEL)),
                   _whole((1, D_MODEL))],
        out_shape=[sd((s, D_MODEL), F32), sd((s, D_MODEL), MXU_DTYPE), sd((s, 2 * D_FF), MXU_DTYPE),
                   sd((1, D_MODEL), F32), sd((1, D_MODEL), F32)],
        compiler_params=_params(("arbitrary",)),
    )(dout, ff, x2, g, up, g_post, g_pre, w_gu, w_dn)


def _tn_matmul(a, b, tm, tn, tk, name):
    s, m = a.shape
    n = b.shape[1]

    def body(a_ref, b_ref, o_ref):
        @pl.when(pl.program_id(2) == 0)
        def _():
            o_ref[...] = jnp.zeros_like(o_ref)

        o_ref[...] += lax.dot_general(a_ref[...], b_ref[...], (((0,), (0,)), ((), ())), preferred_element_type=F32)

    return pl.pallas_call(
        body, name=name, grid=(m // tm, n // tn, s // tk),
        in_specs=[pl.BlockSpec((tk, tm), lambda i, j, kk: (kk, i)), pl.BlockSpec((tk, tn), lambda i, j, kk: (kk, j))],
        out_specs=pl.BlockSpec((tm, tn), lambda i, j, kk: (i, j)),
        out_shape=jax.ShapeDtypeStruct((m, n), F32),
        compiler_params=_params(("arbitrary", "arbitrary", "arbitrary")),
    )(a, b)


def _mixer_bwd(dx2, y, o_attn, gate_b, z, g_post, g_attn, g_conv, w_out, gmat, sel, tm, ready, kinds):
    s = dx2.shape[0]
    nw = len(ready)
    nt = s // tm

    def body(d_ref, y_ref, o_ref, b_ref, z_ref, gp_ref, ga_ref, gc_ref, wo_ref, gm_ref, sel_ref, *rest):
        grads = rest[:nw]
        dy_ref, do_ref, db_ref, dz_ref, delta_ref, dgp_ref, dga_ref, dgc_ref = rest[nw:nw + 8]
        taken = rest[nw + 8:2 * nw + 8]
        send, recv = rest[2 * nw + 8:]
        first = pl.program_id(0) == 0
        pair_start, pair_finish = _pair_stages(grads, kinds, taken, send, recv)
        pl.when(first)(pair_start)
        yv = y_ref[...]
        ry = _rms(yv)
        ny = yv * ry
        d = d_ref[...]
        _acc_rows(dgp_ref, first, d * ny)
        dy = _norm_bwd(d, ny, ry, gp_ref[...]).astype(MXU_DTYPE)
        dy_ref[...] = dy
        dm = _dot_nt(dy, wo_ref[...])
        gm = gm_ref[...]

        def group_bwd(val, dmv, gain, dg_ref):
            rg = lax.rsqrt(_group_ms(val * val, gm) + EPS)
            nv = val * rg
            _acc_rows(dg_ref, first, dmv * nv)
            t = dmv * gain
            return rg * (t - nv * _group_ms(t * nv, gm))

        ov = o_ref[...]
        d_o = group_bwd(ov, dm[:, 0:ATTN_W], ga_ref[...], dga_ref)
        do_ref[...] = d_o.astype(MXU_DTYPE)
        delta_ref[...] = _dot_exact(d_o * ov, sel_ref[...], 2).T[0:HEAD_ROWS, :]
        zv = z_ref[...]
        bv = b_ref[...]
        d_cv = group_bwd(bv * zv, dm[:, ATTN_W:D_MODEL], gc_ref[...], dgc_ref)
        db_ref[...] = d_cv * zv
        dz_ref[...] = d_cv * bv
        pl.when(pl.program_id(0) == nt - 1)(pair_finish)

    sd = jax.ShapeDtypeStruct
    taken_shape = [sd((N_CHIPS, g.shape[-2], g.shape[-1] if kd == "rows" else g.shape[-1] // N_CHIPS), F32)
                   for g, kd in zip(ready, kinds)]
    out = pl.pallas_call(
        body, name="mixer_bwd", grid=(nt,),
        in_specs=[_tok(tm, D_MODEL), _tok(tm, D_MODEL), _tok(tm, 512), _tok(tm, 512), _tok(tm, 512),
                  _whole((1, D_MODEL)), _whole((1, 512)), _whole((1, 512)),
                  _whole((D_MODEL, D_MODEL), single=True), _whole((512, 512)), _whole((512, 128))] + [ANY] * nw,
        out_specs=[_tok(tm, D_MODEL), _tok(tm, 512), _tok(tm, 512), _tok(tm, 512), _feat(HEAD_ROWS, tm),
                   _whole((1, D_MODEL)), _whole((1, 512)), _whole((1, 512))] + [ANY] * nw,
        out_shape=[sd((s, D_MODEL), MXU_DTYPE), sd((s, 512), MXU_DTYPE), sd((s, 512), F32), sd((s, 512), F32),
                   sd((HEAD_ROWS, s), F32), sd((1, D_MODEL), F32), sd((1, 512), F32), sd((1, 512), F32)] + taken_shape,
        scratch_shapes=[pltpu.SemaphoreType.DMA((nw, N_CHIPS)), pltpu.SemaphoreType.DMA((nw, N_CHIPS))],
        compiler_params=_params(("arbitrary",)),
    )(dx2, y, o_attn, gate_b, z, g_post, g_attn, g_conv, w_out, gmat, sel, *ready)
    return out[:8], out[8:]


def _attn_bwd(qs, k, k_t, v, do, c_rows, c_cols, lse, delta, t, parts):
    s = qs.shape[0]
    n = s // t
    pairs = [(i, j) for j in range(n) for i in range(j, n)]
    it = jnp.asarray(np.array([p[0] for p in pairs], np.int32))
    jt = jnp.asarray(np.array([p[1] for p in pairs], np.int32))

    nw = len(parts)

    def body(it_ref, jt_ref, q_ref, k_ref, kt_ref, v_ref, do_ref, cq_ref, ck_ref, lse_ref, dl_ref, *rest):
        pb = rest[:nw]
        dq_ref, dk_ref, dv_ref, dc_ref, dcq_ref = rest[nw:nw + 5]
        rcv = rest[nw + 5:2 * nw + 5]
        dk_sc, dv_sc, dc_sc, send, recv = rest[2 * nw + 5:]
        p = pl.program_id(0)
        i = it_ref[p]
        j = jt_ref[p]
        chip_start, chip_finish = _chip_stages(pb, rcv, send, recv)

        @pl.when(p == 0)
        def _():
            chip_start()
            dq_ref[...] = jnp.zeros_like(dq_ref)
            dcq_ref[...] = jnp.zeros_like(dcq_ref)

        @pl.when(i == j)
        def _():
            dk_sc[...] = jnp.zeros_like(dk_sc)
            dv_sc[...] = jnp.zeros_like(dv_sc)
            dc_sc[...] = jnp.zeros_like(dc_sc)

        def pair_step(pp, diagonal):
            lanes = _pair_lanes(pp)
            qp = q_ref[:, lanes]
            kp = k_ref[:, lanes]
            vp = v_ref[:, lanes]
            dop = do_ref[:, lanes]
            ck_all = ck_ref[...]
            lane = lax.broadcasted_iota(jnp.int32, (t, 128), 1)
            for hb in range(2):
                h = 2 * pp + hb
                row = pl.ds(h, 1)
                bias = (cq_ref[row, :] - lse_ref[row, :]) - _head_col(ck_all, h)
                pt = jnp.exp(_dot_nt(_only_head(kp, hb), qp) + bias)
                if diagonal:
                    kpos = lax.broadcasted_iota(jnp.int32, (t, t), 0)
                    qpos = lax.broadcasted_iota(jnp.int32, (t, t), 1)
                    pt = jnp.where(kpos <= qpos, pt, 0.0)
                dv_sc[:, lanes] += _dot(pt.astype(MXU_DTYPE), _only_head(dop, hb))
                dst = pt * (_dot_nt(_only_head(vp, hb), dop) - dl_ref[row, :])
                dc_sc[...] -= jnp.where(lane == h, jnp.sum(dst, axis=1, keepdims=True), 0.0)
                dcq_ref[i, row, :] += jnp.sum(dst, axis=0, keepdims=True)
                dsb = dst.astype(MXU_DTYPE)
                dk_sc[:, lanes] += _dot(dsb, _only_head(qp, hb))
                rows = _head_rows(h)
                dq_ref[i, rows, :] += _dot(kt_ref[rows, :], dsb)

        @pl.when(i > j)
        def _():
            @pl.loop(0, N_HEADS // 2)
            def _(pp):
                pair_step(pp, False)

        @pl.when(i == j)
        def _():
            @pl.loop(0, N_HEADS // 2)
            def _(pp):
                pair_step(pp, True)

        @pl.when(i == n - 1)
        def _():
            dk_ref[...] = dk_sc[...]
            dv_ref[...] = dv_sc[...]
            dc_ref[...] = dc_sc[...]

        pl.when(p == len(pairs) - 1)(chip_finish)

    qi = lambda p, it_, jt_: (it_[p], 0)
    kj = lambda p, it_, jt_: (jt_[p], 0)
    row_i = lambda p, it_, jt_: (0, it_[p])
    gs = pltpu.PrefetchScalarGridSpec(
        num_scalar_prefetch=2, grid=(len(pairs),),
        in_specs=[pl.BlockSpec((t, ATTN_W), qi), pl.BlockSpec((t, ATTN_W), kj),
                  pl.BlockSpec((ATTN_W, t), lambda p, it_, jt_: (0, jt_[p])),
                  pl.BlockSpec((t, ATTN_W), kj), pl.BlockSpec((t, ATTN_W), qi),
                  pl.BlockSpec((HEAD_ROWS, t), row_i), pl.BlockSpec((t, 128), kj),
                  pl.BlockSpec((HEAD_ROWS, t), row_i), pl.BlockSpec((HEAD_ROWS, t), row_i)] + [ANY] * nw,
        out_specs=[pl.BlockSpec((n, ATTN_W, t), lambda p, it_, jt_: (0, 0, 0)),
                   pl.BlockSpec((t, ATTN_W), kj), pl.BlockSpec((t, ATTN_W), kj),
                   pl.BlockSpec((t, 128), kj),
                   pl.BlockSpec((n, HEAD_ROWS, t), lambda p, it_, jt_: (0, 0, 0))] + [ANY] * nw,
        scratch_shapes=[pltpu.VMEM((t, ATTN_W), F32), pltpu.VMEM((t, ATTN_W), F32),
                        pltpu.VMEM((t, 128), F32), pltpu.SemaphoreType.DMA((nw, 3)), pltpu.SemaphoreType.DMA((nw, 3))])
    sd = jax.ShapeDtypeStruct
    out = pl.pallas_call(
        body, name="attn_bwd", grid_spec=gs,
        out_shape=[sd((n, ATTN_W, t), F32), sd((s, ATTN_W), F32), sd((s, ATTN_W), F32),
                   sd((s, 128), F32), sd((n, HEAD_ROWS, t), F32)] + [sd((3,) + a.shape[1:], a.dtype) for a in parts],
        compiler_params=_params(("arbitrary",)),
    )(it, jt, qs, k, k_t, v, do, c_rows, c_cols, lse, delta, *parts)
    return out[:5], out[5:]


def _forget_bwd(dc_rows, dc_cols, z_t, b_col):
    s = z_t.shape[1]
    nb = s // 128

    def body(dr_ref, dcc_ref, z_ref, b_ref, dz_ref, db_ref):
        lower = _tri(128, False)
        real = lax.broadcasted_iota(jnp.int32, (HEAD_ROWS, 128), 0) < N_HEADS

        tail = jnp.zeros((HEAD_ROWS, 1), F32)
        dbias = jnp.zeros((HEAD_ROWS, 1), F32)
        for m in range(nb):
            off = (nb - 1 - m) * 128
            dc = dr_ref[:, off:off + 128] + dcc_ref[off:off + 128, :].T[0:HEAD_ROWS, :]
            dlf = _dot_exact(dc, lower, 3) + tail
            dz = dlf * jax.nn.sigmoid(-(z_ref[0:HEAD_ROWS, off:off + 128] + b_ref[...]))
            dz = jnp.where(real, dz, 0.0)
            dz_ref[off:off + 128, :] = _rows_to_cols(dz)
            tail = tail + jnp.sum(dc, axis=1, keepdims=True)
            dbias = dbias + jnp.sum(dz, axis=1, keepdims=True)
        db_ref[...] = jnp.broadcast_to(dbias, db_ref.shape)

    return pl.pallas_call(
        body, name="forget_bwd",
        out_shape=[jax.ShapeDtypeStruct((s, 128), F32), jax.ShapeDtypeStruct((HEAD_ROWS, 128), F32)],
        compiler_params=_params())(dc_rows, dc_cols, z_t, b_col)


def _inproj_bwd(dz, gate_c, u, conv_w, dq, dk, dv, dzf, db, x, dx2, g_pre, w_t, tm):
    s = x.shape[0]
    nt = s // tm
    per = dq.shape[2] // tm
    assert dq.shape[2] % tm == 0 and dq.shape[:2] == (nt // per, ATTN_W)

    def body(dz_ref, dzn_ref, c_ref, u_ref, cp_ref, up_ref, cw_ref, dq_ref, dk_ref, dv_ref, dzf_ref, db_ref,
             x_ref, dx2_ref, g_ref, w_ref, gx_ref, dp_ref, dg_ref, dcw_ref):
        i = pl.program_id(0)
        first = i == 0
        last = i == nt - 1
        dzv = dz_ref[...]
        row = lax.broadcasted_iota(jnp.int32, dzv.shape, 0)
        n0 = jnp.where(last, 0.0, dzn_ref[0:1, :])
        n1 = jnp.where(last, 0.0, dzn_ref[1:2, :])
        dz1 = jnp.where(row == tm - 1, n0, pltpu.roll(dzv, tm - 1, 0))
        dz2 = jnp.where(row == tm - 1, n1, jnp.where(row == tm - 2, n0, pltpu.roll(dzv, tm - 2, 0)))
        dcu = cw_ref[2:3, :] * dzv + cw_ref[1:2, :] * dz1 + cw_ref[0:1, :] * dz2
        cv = c_ref[...]
        uv = u_ref[...]
        cu = cv * uv
        cu1, cu2 = _shift_down(cu, (cp_ref, up_ref), first)

        @pl.when(first)
        def _():
            dcw_ref[...] = jnp.zeros_like(dcw_ref)

        dcw_ref[0:1, :] += jnp.sum(dzv * cu2, axis=0, keepdims=True)
        dcw_ref[1:2, :] += jnp.sum(dzv * cu1, axis=0, keepdims=True)
        dcw_ref[2:3, :] += jnp.sum(dzv * cu, axis=0, keepdims=True)

        dp_ref[:, 0:512] = (dq_ref[0].T * Q_SCALE).astype(MXU_DTYPE)
        dp_ref[:, 512:1024] = dk_ref[...].astype(MXU_DTYPE)
        dp_ref[:, 1024:OFF_F] = dv_ref[...].astype(MXU_DTYPE)
        dp_ref[:, OFF_F:OFF_B] = dzf_ref[...].astype(MXU_DTYPE)
        dp_ref[:, OFF_B:OFF_C] = db_ref[...].astype(MXU_DTYPE)
        dp_ref[:, OFF_C:OFF_U] = (dcu * uv).astype(MXU_DTYPE)
        dp_ref[:, OFF_U:IN_PAD] = (dcu * cv).astype(MXU_DTYPE)
        dh = _dot(dp_ref[...], w_ref[...])
        xv = x_ref[...]
        r1 = _rms(xv)
        nx = xv * r1
        _acc_rows(dg_ref, first, dh * nx)
        gx_ref[...] = dx2_ref[...] + _norm_bwd(dh, nx, r1, g_ref[...])

    prev = pl.BlockSpec((8, 512), lambda i: (jnp.maximum(i * (tm // 8) - 1, 0), 0))
    nxt = pl.BlockSpec((8, 512), lambda i: (jnp.minimum((i + 1) * (tm // 8), s // 8 - 1), 0))
    sd = jax.ShapeDtypeStruct
    return pl.pallas_call(
        body, name="inproj_bwd", grid=(nt,),
        in_specs=[_tok(tm, 512), nxt, _tok(tm, 512), _tok(tm, 512), prev, prev, _whole((3, 512)),
                  pl.BlockSpec((1, ATTN_W, tm), lambda i: (i // per, 0, i % per)), _tok(tm, 512), _tok(tm, 512),
                  _tok(tm, 128),
                  _tok(tm, 512),
                  _tok(tm, D_MODEL), _tok(tm, D_MODEL), _whole((1, D_MODEL)), _whole((IN_PAD, D_MODEL), single=True)],
        out_specs=[_tok(tm, D_MODEL), _tok(tm, IN_PAD), _whole((1, D_MODEL)), _whole((8, 512))],
        out_shape=[sd((s, D_MODEL), F32), sd((s, IN_PAD), MXU_DTYPE), sd((1, D_MODEL), F32), sd((8, 512), F32)],
        compiler_params=_params(("arbitrary",)),
    )(dz, dz, gate_c, u, gate_c, u, conv_w, dq, dk, dv, dzf, db, x, dx2, g_pre, w_t)


def _tile(s, want):
    return want if s % want == 0 else s


def _halves(a):
    return a.reshape(2, a.shape[0] // 2, a.shape[1])


def _device_step(x, target, w, w_in_t, c_idx, me_idx):
    s = x.shape[0]
    tm = _tile(s, 512)
    tf = _tile(s, 256)
    ta = _tile(s, 512)
    tkk = _tile(s, 2048)
    gidx = np.arange(512) // HEAD_DIM
    gmat = jnp.asarray(gidx[:, None] == gidx[None, :], MXU_DTYPE)
    sel = jnp.asarray(gidx[:, None] == np.arange(128)[None, :], MXU_DTYPE)
    g_mix_pre, g_mix_post, g_ffn_pre, g_ffn_post = w["g_mix_pre"], w["g_mix_post"], w["g_ffn_pre"], w["g_ffn_post"]
    g_attn, g_conv, b_forget = w["g_attn_out"], w["g_conv_out"], w["b_forget"]
    shard = {n: _halves(w[n][0].astype(MXU_DTYPE)) for n in BIG[1:]}
    piece_rows = IN_W // N_CHIPS

    g_in, conv_all = _gather_weights([w_in_t.reshape(piece_rows, D_MODEL).astype(MXU_DTYPE)], w["conv_w"][0])
    w_rows = g_in.reshape(IN_W, D_MODEL)
    w_t = jnp.concatenate([w_rows[:OFF_F + N_HEADS], jnp.zeros((OFF_B - OFF_F - N_HEADS, D_MODEL), MXU_DTYPE),
                           w_rows[OFF_F + N_HEADS:]], axis=0)
    conv_w = jnp.transpose(conv_all, (1, 0, 2)).reshape(3, CONV_W)

    h1, qs, k, v, k_t, v_t, z_t, gate_b, gate_c, u = _inproj_fwd(x, g_mix_pre, w_t, tm)
    b_col = jnp.pad(jnp.transpose(b_forget), ((0, HEAD_ROWS - N_HEADS), (0, 0)))
    c_rows, c_cols = _forget_fwd(z_t, b_col)
    o_attn, lse, (g_out, g_gu, g_dn) = _attn_fwd(qs, k, v_t, c_rows, c_cols, ta,
                                                 [shard["w_out"], shard["w_gate_up"], shard["w_down"]])
    w_out = g_out.reshape(D_MODEL, D_MODEL)
    w_gu = g_gu.reshape(N_CHIPS, D_MODEL, FF_PIECE)
    w_dn = g_dn.reshape(2, FF_PIECE, D_MODEL)
    x2, merged, y, z = _mixer_fwd(x, o_attn, gate_b, gate_c, u, conv_w, g_attn, g_conv, w_out, g_mix_post, gmat, tm)
    h2, g, up, a, ff, dout, loss_acc = _ffn_fwd(x2, target, g_ffn_pre, w_gu, w_dn, g_ffn_post, tf)

    dx2, dff, dgu, dg_ffn_post, dg_ffn_pre = _ffn_bwd(dout, ff, x2, g, up, g_ffn_post, g_ffn_pre, w_gu, w_dn, tf)
    dw_dn = _tn_matmul(a, dff, FF_PIECE, 1024, tkk, "dw_down").reshape(N_CHIPS, 2, D_FF // (2 * N_CHIPS), D_MODEL)
    dw_gu = _tn_matmul(h2, dgu, 1024, FF_PIECE, tkk, "dw_gate_up").reshape(2, D_MODEL // 2, 2 * D_FF)
    (dy, d_o, d_b, dz, delta, dg_mix_post, dg_attn, dg_conv), (a_gu, a_dn) = _mixer_bwd(
        dx2, y, o_attn, gate_b, z, g_mix_post, g_attn, g_conv, w_out, gmat, sel, tm, [dw_gu, dw_dn], ["cols", "rows"])
    dw_out = _tn_matmul(merged, dy, 1024, 1024, tkk, "dw_out")
    sum_gu = _pair_sum(c_idx, dw_gu, "cols", a_gu, "pair_sum_w_gate_up")
    sum_dn = _pair_sum(c_idx, dw_dn, "rows", a_dn, "pair_sum_w_down")
    (dq_t, dk, dv, dc_cols, dcq), (r_gu, r_dn) = _attn_bwd(
        qs, k, k_t, v, d_o, c_rows, c_cols, lse, delta, ta, [sum_gu[1], sum_dn[1]])
    dc_rows = jnp.transpose(dcq, (1, 0, 2)).reshape(HEAD_ROWS, s)
    dzf, db_f = _forget_bwd(dc_rows, dc_cols, z_t, b_col)
    grad_x, dproj, dg_mix_pre, dcw = _inproj_bwd(dz, gate_c, u, conv_w, dq_t, dk, dv, dzf, d_b,
                                                 x, dx2, g_mix_pre, w_t, tm)
    dw_t = _tn_matmul(dproj, h1, 640, 1024, tkk, "dw_in")
    dw_in = jnp.concatenate([dw_t[:OFF_F + N_HEADS], dw_t[OFF_B:]], axis=0).reshape(N_CHIPS, piece_rows, D_MODEL)
    dw_out = dw_out.reshape(N_CHIPS, 2, D_MODEL // (2 * N_CHIPS), D_MODEL)

    a_in, a_out = _pair_exchange([dw_in, dw_out], ["lanes", "rows"])
    sum_in = _pair_sum(c_idx, dw_in, "lanes", a_in, "pair_sum_w_in")
    sum_out = _pair_sum(c_idx, dw_out, "rows", a_out, "pair_sum_w_out")
    small = dict(b_forget=db_f[:N_HEADS, 0], g_attn_out=dg_attn, g_conv_out=dg_conv, g_mix_pre=dg_mix_pre,
                 g_mix_post=dg_mix_post, g_ffn_pre=dg_ffn_pre, g_ffn_post=dg_ffn_post)
    r_in, r_out, small_all = _chip_exchange([sum_in[1], sum_out[1]], _pack_small(small, dcw[:3], loss_acc[0, 0]))
    totals = [_chip_sum(me_idx, sb[0], r, "chip_sum_" + n)
              for n, sb, r in zip(BIG, (sum_in, sum_out, sum_gu, sum_dn), (r_in, r_out, r_gu, r_dn))]
    return grad_x, totals, _pair_share(totals), small_all


BIG = ("w_in", "w_out", "w_gate_up", "w_down")
ANY = pl.BlockSpec(memory_space=pl.ANY)


def _place():
    x, y, c = lax.axis_index("x"), lax.axis_index("y"), lax.axis_index("c")
    others = [(1 - x, y), (x, 1 - y), (1 - x, 1 - y)]
    return x, y, c, 2 * x + y, others, [2 * px + py for px, py in others]


def _remote(src, dst, send, recv, dev):
    return pltpu.make_async_remote_copy(src_ref=src, dst_ref=dst, send_sem=send, recv_sem=recv,
                                        device_id=dev, device_id_type=MESH_ID)


def _gather_stages(sh, outs, send, recv):
    x, y, c, me, others, chips = _place()
    sib = (x, y, 1 - c)
    every = [(w, kk) for w in range(len(sh)) for kk in range(3)]

    def half_of(ref, half, piece=None):
        ref = ref if piece is None else ref.at[piece]
        if len(ref.shape) == 3:
            return ref.at[half]
        hc = ref.shape[1] // 2
        return ref.at[:, pl.ds(pl.multiple_of(half * hc, 128), hc)]

    def first(w, kk):
        return _remote(half_of(sh[w], c), half_of(outs[w], c, me), send.at[w, kk], recv.at[w, kk], (*others[kk], c))

    def landed(w, kk):
        r = half_of(outs[w], c, chips[kk])
        return _remote(r, r, send.at[w, kk], recv.at[w, kk], (*others[kk], c))

    def onward(w, kk, half):
        r = half_of(outs[w], half, chips[kk])
        return _remote(r, r, send.at[w, 3 + kk], recv.at[w, 3 + kk], sib)

    def start():
        for w, kk in every:
            first(w, kk).start()

    def forward():
        for w, kk in every:
            landed(w, kk).wait_recv()
            onward(w, kk, c).start()

    def finish():
        for w, kk in every:
            onward(w, kk, 1 - c).wait_recv()
        for w, kk in every:
            first(w, kk).wait_send()
            onward(w, kk, c).wait_send()

    return start, forward, finish


def _pair_piece(ref, kind, p, half):
    if kind == "rows":
        return ref.at[p, half]
    if kind == "lanes":
        hc = ref.shape[2] // 2
        return ref.at[p, :, pl.ds(pl.multiple_of(half * hc, 128), hc)]
    cols = ref.shape[2] // N_CHIPS
    return ref.at[half, :, pl.ds(p * cols, cols)]


def _pair_stages(g, kinds, a, send, recv):
    x, y, c, _, _, _ = _place()
    copies = [_remote(_pair_piece(g[w], kinds[w], p, 1 - c), a[w].at[p], send.at[w, p], recv.at[w, p], (x, y, 1 - c))
              for w in range(len(g)) for p in range(N_CHIPS)]

    def start():
        for cp in copies:
            cp.start()

    def finish():
        for cp in copies:
            cp.wait()

    return start, finish


def _chip_stages(pb, rcv, send, recv):
    x, y, c, _, others, chips = _place()
    copies = [_remote(pb[w].at[chips[kk]], rcv[w].at[kk], send.at[w, kk], recv.at[w, kk], (*others[kk], c))
              for w in range(len(pb)) for kk in range(3)]

    def start():
        for cp in copies:
            cp.start()

    def finish():
        for cp in copies:
            cp.wait()

    return start, finish


def _gather_weights(shards, conv_w):
    n = len(shards)

    def body(*refs):
        sh, cw, outs, cwo = refs[:n], refs[n], refs[n + 1:2 * n + 1], refs[2 * n + 1]
        send, recv = refs[2 * n + 2:]
        x, y, c, me, others, chips = _place()
        start, forward, finish = _gather_stages(sh, outs, send, recv)
        start()
        small = [_remote(cw, cwo.at[me], send.at[n, kk], recv.at[n, kk], (*others[kk], c)) for kk in range(3)]
        for cp in small:
            cp.start()
        forward()
        for kk in range(3):
            _remote(cw, cwo.at[chips[kk]], send.at[n, kk], recv.at[n, kk], (*others[kk], c)).wait_recv()
        finish()
        for cp in small:
            cp.wait_send()

    out_shape = [jax.ShapeDtypeStruct((N_CHIPS,) + s.shape, s.dtype) for s in shards]
    out_shape.append(jax.ShapeDtypeStruct((N_CHIPS,) + conv_w.shape, conv_w.dtype))
    got = pl.pallas_call(
        body, name="gather_weights", in_specs=[ANY] * (n + 1), out_specs=[ANY] * (n + 1), out_shape=out_shape,
        scratch_shapes=[pltpu.SemaphoreType.DMA((n + 1, 6)), pltpu.SemaphoreType.DMA((n + 1, 6))],
    )(*shards, conv_w)
    me = 2 * lax.axis_index("x") + lax.axis_index("y")
    return [lax.dynamic_update_index_in_dim(g, own, me, 0) for g, own in zip(got, list(shards) + [conv_w])]


def _taken_shape(g, kind):
    if kind == "rows":
        return (N_CHIPS,) + g.shape[2:]
    if kind == "lanes":
        return g.shape[:2] + (g.shape[2] // 2,)
    return (N_CHIPS, g.shape[1], g.shape[2] // N_CHIPS)


def _pair_exchange(grads, kinds):
    n = len(grads)

    def body(*refs):
        g, a = refs[:n], refs[n:2 * n]
        start, finish = _pair_stages(g, kinds, a, *refs[2 * n:])
        start()
        finish()

    return pl.pallas_call(
        body, name="pair_exchange", in_specs=[ANY] * n, out_specs=[ANY] * n,
        out_shape=[jax.ShapeDtypeStruct(_taken_shape(g, kd), g.dtype) for g, kd in zip(grads, kinds)],
        scratch_shapes=[pltpu.SemaphoreType.DMA((n, N_CHIPS)), pltpu.SemaphoreType.DMA((n, N_CHIPS))],
    )(*grads)


def _pair_sum(c_idx, g, kind, a, name):
    _, half, cols = a.shape
    if kind == "rows":
        mine = pl.BlockSpec((1, 1, half, cols), lambda p, cr: (p, cr[0], 0, 0))
    elif kind == "lanes":
        mine = pl.BlockSpec((1, half, cols), lambda p, cr: (p, 0, cr[0]))
    else:
        mine = pl.BlockSpec((1, half, cols), lambda p, cr: (cr[0], 0, p))

    def body(c_ref, g_ref, a_ref, pf_ref, pb_ref):
        tot = (g_ref[0, 0] if kind == "rows" else g_ref[0]) + a_ref[0]
        pf_ref[0] = tot
        pb_ref[0] = tot.astype(BF16)

    gs = pltpu.PrefetchScalarGridSpec(
        num_scalar_prefetch=1, grid=(N_CHIPS,),
        in_specs=[mine,
                  pl.BlockSpec((1, half, cols), lambda p, cr: (p, 0, 0))],
        out_specs=[pl.BlockSpec((1, half, cols), lambda p, cr: (p, 0, 0)),
                   pl.BlockSpec((1, half, cols), lambda p, cr: (p, 0, 0))])
    return pl.pallas_call(
        body, name=name, grid_spec=gs,
        out_shape=[jax.ShapeDtypeStruct((N_CHIPS, half, cols), F32), jax.ShapeDtypeStruct((N_CHIPS, half, cols), BF16)],
        compiler_params=_params(("arbitrary",)),
    )(c_idx, g, a)


def _chip_exchange(parts, small):
    n = len(parts)

    def body(*refs):
        pb, sm, rcv, smg = refs[:n], refs[n], refs[n + 1:2 * n + 1], refs[2 * n + 1]
        send, recv, ssend, srecv, loc = refs[2 * n + 2:]
        x, y, c, _, others, chips = _place()
        mine = 4 * x + 2 * y + c
        own = pltpu.make_async_copy(sm, smg.at[mine], loc)
        own.start()
        copies = [_remote(pb[w].at[chips[kk]], rcv[w].at[kk], send.at[w, kk], recv.at[w, kk], (px, py, c))
                  for w in range(n) for kk, (px, py) in enumerate(others)]
        for r in range(1, 8):
            peer = (1 - x if r & 4 else x, 1 - y if r & 2 else y, 1 - c if r & 1 else c)
            copies.append(_remote(sm, smg.at[mine], ssend.at[r - 1], srecv.at[r - 1], peer))
        for cp in copies:
            cp.start()
        for w in range(n):
            for kk, (px, py) in enumerate(others):
                _remote(pb[w].at[chips[kk]], rcv[w].at[kk], send.at[w, kk], recv.at[w, kk], (px, py, c)).wait_recv()
        for r in range(1, 8):
            px, py, pc = (1 - x if r & 4 else x, 1 - y if r & 2 else y, 1 - c if r & 1 else c)
            _remote(sm, smg.at[4 * px + 2 * py + pc], ssend.at[r - 1], srecv.at[r - 1], (px, py, pc)).wait_recv()
        for cp in copies:
            cp.wait_send()
        own.wait()

    out_shape = [jax.ShapeDtypeStruct((3,) + p.shape[1:], p.dtype) for p in parts]
    out_shape.append(jax.ShapeDtypeStruct((8,) + small.shape, small.dtype))
    return pl.pallas_call(
        body, name="chip_exchange", in_specs=[ANY] * (n + 1), out_specs=[ANY] * (n + 1), out_shape=out_shape,
        scratch_shapes=[pltpu.SemaphoreType.DMA((n, 3)), pltpu.SemaphoreType.DMA((n, 3)),
                        pltpu.SemaphoreType.DMA((7,)), pltpu.SemaphoreType.DMA((7,)), pltpu.SemaphoreType.DMA(())],
    )(*parts, small)


def _chip_sum(me_idx, pf, rcv, name):
    _, half, cols = pf.shape

    def body(me_ref, pf_ref, r_ref, t_ref):
        t_ref[...] = ((pf_ref[0] + r_ref[0].astype(F32)) + r_ref[1].astype(F32)) + r_ref[2].astype(F32)

    gs = pltpu.PrefetchScalarGridSpec(
        num_scalar_prefetch=1, grid=(1,),
        in_specs=[pl.BlockSpec((1, half, cols), lambda i, mr: (mr[0], 0, 0)),
                  pl.BlockSpec((3, half, cols), lambda i, mr: (0, 0, 0))],
        out_specs=pl.BlockSpec((half, cols), lambda i, mr: (0, 0)))
    return pl.pallas_call(
        body, name=name, grid_spec=gs, out_shape=jax.ShapeDtypeStruct((half, cols), F32),
        compiler_params=_params(("arbitrary",)),
    )(me_idx, pf, rcv)


def _pair_share(totals):
    n = len(totals)

    def body(*refs):
        t, g = refs[:n], refs[n:2 * n]
        send, recv = refs[2 * n:]
        x, y, c, _, _, _ = _place()
        copies = [_remote(t[w], g[w], send.at[w], recv.at[w], (x, y, 1 - c)) for w in range(n)]
        for cp in copies:
            cp.start()
        for cp in copies:
            cp.wait()

    return pl.pallas_call(
        body, name="pair_share", in_specs=[ANY] * n, out_specs=[ANY] * n,
        out_shape=[jax.ShapeDtypeStruct(t.shape, t.dtype) for t in totals],
        scratch_shapes=[pltpu.SemaphoreType.DMA((n,)), pltpu.SemaphoreType.DMA((n,))],
    )(*totals)


def _adamw_math(w, g, m, v):
    m = ADAM_B1 * m + (1.0 - ADAM_B1) * g
    v = ADAM_B2 * v + (1.0 - ADAM_B2) * (g * g)
    m_hat = m / (1.0 - ADAM_B1 ** ADAM_STEP)
    v_hat = v / (1.0 - ADAM_B2 ** ADAM_STEP)
    delta = -ADAM_LR * (m_hat / (jnp.sqrt(v_hat) + ADAM_EPS) + ADAM_WD * w)
    return delta, m, v


def _adamw(c_idx, w, mine, theirs, m, v, nb, name):
    rows, cols = w.shape
    tr = rows // (2 * nb)

    def body(c_ref, w_ref, a_ref, b_ref, m_ref, v_ref, g_ref, d_ref, nm_ref, nv_ref):
        g = jnp.where(pl.program_id(0) == c_ref[0], a_ref[...], b_ref[...])
        g_ref[...] = g
        d_ref[...], nm_ref[...], nv_ref[...] = _adamw_math(w_ref[...], g, m_ref[...], v_ref[...])

    full = pl.BlockSpec((tr, cols), lambda hh, i, cr: (hh * nb + i, 0))
    half = pl.BlockSpec((tr, cols), lambda hh, i, cr: (i, 0))
    gs = pltpu.PrefetchScalarGridSpec(num_scalar_prefetch=1, grid=(2, nb), in_specs=[full, half, half, full, full],
                                      out_specs=[full] * 4)
    return pl.pallas_call(
        body, name=name, grid_spec=gs, out_shape=[jax.ShapeDtypeStruct((rows, cols), F32)] * 4,
        compiler_params=_params(("arbitrary", "arbitrary")),
    )(c_idx, w, mine, theirs, m, v)


def _adamw_lanes(c_idx, w, mine, theirs, m, v, name):
    rows, _, cols = w.shape
    hc = cols // 2

    def body(c_ref, w_ref, a_ref, b_ref, m_ref, v_ref, g_ref, d_ref, nm_ref, nv_ref):
        g = jnp.where(pl.program_id(0) == c_ref[0], a_ref[...], b_ref[...])
        g_ref[:, 0, :] = g
        d_ref[:, 0, :], nm_ref[:, 0, :], nv_ref[:, 0, :] = _adamw_math(w_ref[:, 0, :], g, m_ref[:, 0, :], v_ref[:, 0, :])

    full = pl.BlockSpec((rows, 1, hc), lambda hh, cr: (0, 0, hh))
    half = pl.BlockSpec((rows, hc), lambda hh, cr: (0, 0))
    gs = pltpu.PrefetchScalarGridSpec(num_scalar_prefetch=1, grid=(2,), in_specs=[full, half, half, full, full],
                                      out_specs=[full] * 4)
    return pl.pallas_call(
        body, name=name, grid_spec=gs, out_shape=[jax.ShapeDtypeStruct((rows, 1, cols), F32)] * 4,
        compiler_params=_params(("arbitrary",)),
    )(c_idx, w, mine, theirs, m, v)


SMALL = ("g_mix_pre", "g_mix_post", "g_ffn_pre", "g_ffn_post")
SMALL_ALL = SMALL + ("g_attn_out", "g_conv_out", "conv_w", "b_forget")
SMALL_AT = {"g_mix_pre": (0, 0, 1024), "g_mix_post": (1, 0, 1024), "g_ffn_pre": (2, 0, 1024),
            "g_ffn_post": (3, 0, 1024), "g_attn_out": (4, 0, 512), "g_conv_out": (4, 512, 512),
            "b_forget": (7, 0, N_HEADS)}
CONV_AT = ((5, 0), (5, 512), (6, 0))
LOSS_AT = (6, 512)


def _pack_small(t, conv_full, loss_sum):
    conv = jnp.concatenate([conv_full.reshape(1, 3 * CONV_W), loss_sum.reshape(1, 1),
                            jnp.zeros((1, 2048 - 3 * CONV_W - 1), F32)], axis=1).reshape(2, 1024)
    return jnp.concatenate([t[n].reshape(1, 1024) for n in SMALL]
                           + [jnp.concatenate([t["g_attn_out"].reshape(1, 512), t["g_conv_out"].reshape(1, 512)], axis=1),
                              conv, jnp.pad(t["b_forget"].reshape(1, N_HEADS), ((0, 0), (0, 1024 - N_HEADS)))], axis=0)


def _small_update(me_idx, gathered, w, m, v):
    def body(me_ref, gg_ref, *refs):
        k = len(SMALL_ALL)
        w_refs, m_refs, v_refs = refs[:k], refs[k:2 * k], refs[2 * k:3 * k]
        loss_ref = refs[3 * k]
        outs = refs[3 * k + 1:3 * k + 1 + 4 * k]
        sums = refs[-1]
        g = gg_ref[0]
        for dev in range(1, 8):
            g = g + gg_ref[dev]
        sums[...] = g
        loss_ref[...] = sums[LOSS_AT[0]:LOSS_AT[0] + 1, LOSS_AT[1]:LOSS_AT[1] + 1]
        mine = pl.multiple_of(me_ref[0] * 128, 128)
        for idx, name in enumerate(SMALL_ALL):
            g_ref, d_ref, nm_ref, nv_ref = outs[4 * idx:4 * idx + 4]
            if name == "conv_w":
                for r, (row, lo) in enumerate(CONV_AT):
                    gr = sums[row:row + 1, pl.ds(lo + mine, 128)]
                    g_ref[0, r:r + 1, :] = gr
                    d_ref[0, r:r + 1, :], nm_ref[0, r:r + 1, :], nv_ref[0, r:r + 1, :] = _adamw_math(
                        w_refs[idx][0, r:r + 1, :], gr, m_refs[idx][0, r:r + 1, :], v_refs[idx][0, r:r + 1, :])
            else:
                row, lo, n = SMALL_AT[name]
                gr = sums[row:row + 1, lo:lo + n]
                g_ref[...] = gr
                d_ref[...], nm_ref[...], nv_ref[...] = _adamw_math(w_refs[idx][...], gr, m_refs[idx][...],
                                                                    v_refs[idx][...])

    def whole(a):
        nd = a.ndim
        return pl.BlockSpec(a.shape, lambda i, mr: (0,) * nd)

    ins = [t[n] for t in (w, m, v) for n in SMALL_ALL]
    out_shape = [jax.ShapeDtypeStruct((1, 1), F32)]
    for n in SMALL_ALL:
        out_shape += [jax.ShapeDtypeStruct(w[n].shape, F32)] * 4
    gs = pltpu.PrefetchScalarGridSpec(
        num_scalar_prefetch=1, grid=(1,), in_specs=[whole(gathered)] + [whole(a) for a in ins],
        out_specs=[whole(o) for o in out_shape], scratch_shapes=[pltpu.VMEM((8, 1024), F32)])
    out = pl.pallas_call(body, name="small_update", grid_spec=gs, out_shape=out_shape,
                         compiler_params=_params(("arbitrary",)))(me_idx, gathered, *ins)
    return out[0], {n: out[1 + 4 * i:5 + 4 * i] for i, n in enumerate(SMALL_ALL)}


def kernel(x, w_in, b_forget, conv_w, g_attn_out, g_conv_out, w_out, g_mix_pre, g_mix_post, w_gate_up, w_down, g_ffn_pre, g_ffn_post, loss_target, m_w_in, m_b_forget, m_conv_w, m_g_attn_out, m_g_conv_out, m_w_out, m_g_mix_pre, m_g_mix_post, m_w_gate_up, m_w_down, m_g_ffn_pre, m_g_ffn_post, v_w_in, v_b_forget, v_conv_w, v_g_attn_out, v_g_conv_out, v_w_out, v_g_mix_pre, v_g_mix_post, v_w_gate_up, v_w_down, v_g_ffn_pre, v_g_ffn_post):
    w = dict(w_in=w_in, b_forget=b_forget, conv_w=conv_w, g_attn_out=g_attn_out, g_conv_out=g_conv_out, w_out=w_out,
             g_mix_pre=g_mix_pre, g_mix_post=g_mix_post, w_gate_up=w_gate_up, w_down=w_down, g_ffn_pre=g_ffn_pre,
             g_ffn_post=g_ffn_post)
    m = dict(w_in=m_w_in, b_forget=m_b_forget, conv_w=m_conv_w, g_attn_out=m_g_attn_out, g_conv_out=m_g_conv_out,
             w_out=m_w_out, g_mix_pre=m_g_mix_pre, g_mix_post=m_g_mix_post, w_gate_up=m_w_gate_up, w_down=m_w_down,
             g_ffn_pre=m_g_ffn_pre, g_ffn_post=m_g_ffn_post)
    v = dict(w_in=v_w_in, b_forget=v_b_forget, conv_w=v_conv_w, g_attn_out=v_g_attn_out, g_conv_out=v_g_conv_out,
             w_out=v_w_out, g_mix_pre=v_g_mix_pre, g_mix_post=v_g_mix_post, w_gate_up=v_w_gate_up, w_down=v_w_down,
             g_ffn_pre=v_g_ffn_pre, g_ffn_post=v_g_ffn_post)
    cx, cy, cc = lax.axis_index("x"), lax.axis_index("y"), lax.axis_index("c")
    me = 2 * cx + cy
    c_idx = cc.astype(jnp.int32).reshape(1)
    me_idx = me.astype(jnp.int32).reshape(1)

    stored = lambda a: jnp.transpose(a, (2, 0, 1))
    grad_x, totals, shared, small_all = _device_step(x[0], loss_target[0], w, stored(w_in), c_idx, me_idx)

    gsum, delta, new_m, new_v = {}, {}, {}, {}
    for n, mine, theirs in zip(BIG, totals, shared):
        if n == "w_in":
            res = _adamw_lanes(c_idx, stored(w_in), mine, theirs, stored(m_w_in), stored(v_w_in), "adamw_w_in")
            gsum[n], delta[n], new_m[n], new_v[n] = [jnp.transpose(r, (1, 2, 0)) for r in res]
        else:
            gs, d, nm, nv = _adamw(c_idx, w[n][0], mine, theirs, m[n][0], v[n][0], 2, "adamw_" + n)
            gsum[n], delta[n], new_m[n], new_v[n] = gs[None], d[None], nm[None], nv[None]
    loss_sum, small_new = _small_update(me_idx, small_all, w, m, v)
    for n in SMALL_ALL:
        gsum[n], delta[n], new_m[n], new_v[n] = small_new[n]
    loss = 0.5 * loss_sum[0, 0]

    order = ("w_in", "b_forget", "conv_w", "g_attn_out", "g_conv_out", "w_out", "g_mix_pre", "g_mix_post",
             "w_gate_up", "w_down", "g_ffn_pre", "g_ffn_post")
    return (loss, grad_x[None], *[gsum[n] for n in order], *[delta[n] for n in order],
            *[new_m[n] for n in order], *[new_v[n] for n in order])
```

```python
import functools

import jax
import jax.numpy as jnp
import numpy as np
from jax import lax
from jax.experimental import pallas as pl
from jax.experimental.pallas import tpu as pltpu

F32 = jnp.float32
BF16 = jnp.bfloat16
MXU_DTYPE = jnp.bfloat16

D_MODEL = 1024
HEAD_DIM = 64
N_HEADS = 8
ATTN_W = 512
CONV_W = 512
D_FF = 2816
FF_PIECE = 1408
EPS = 1e-6
Q_SCALE = HEAD_DIM ** -0.5

OFF_F = 1536
OFF_B = 1664
OFF_C = 2176
OFF_U = 2688
IN_PAD = 3200
IN_W = 3080
N_CHIPS = 4

ADAM_LR = 0.001
ADAM_B1 = 0.9
ADAM_B2 = 0.999
ADAM_EPS = 1e-08
ADAM_WD = 0.01
ADAM_STEP = 10

VMEM_LIMIT_V7X = 56 * 1024 * 1024
MESH_ID = pl.DeviceIdType.MESH


def _params(sem=None, vmem=VMEM_LIMIT_V7X):
    kw = {"vmem_limit_bytes": vmem}
    if sem is not None:
        kw["dimension_semantics"] = sem
    return pltpu.CompilerParams(**kw)


def _dot(a, b):
    return jnp.dot(a, b, preferred_element_type=F32)


def _dot_nt(a, b):
    return lax.dot_general(a, b, (((1,), (1,)), ((), ())), preferred_element_type=F32)


def _dot_exact(x, ones, parts):
    if ones.dtype == F32:
        return _dot(x, ones)
    acc = None
    rem = x
    for _ in range(parts):
        piece = rem.astype(BF16)
        rem = rem - piece.astype(F32)
        term = _dot(piece, ones)
        acc = term if acc is None else acc + term
    return acc


def _rms(v):
    return lax.rsqrt(jnp.mean(v * v, axis=-1, keepdims=True) + EPS)


def _tok(tm, w):
    return pl.BlockSpec((tm, w), lambda i: (i, 0))


def _whole(shape, single=False):
    nd = len(shape)
    if single:
        return pl.BlockSpec(shape, lambda i: (0,) * nd, pipeline_mode=pl.Buffered(1))
    return pl.BlockSpec(shape, lambda i: (0,) * nd)


def _feat(rows, tm):
    return pl.BlockSpec((rows, tm), lambda i: (0, i))


def _inproj_fwd(x, g_pre, w_t, tm):
    s = x.shape[0]

    def body(x_ref, g_ref, w_ref, h_ref, q_ref, k_ref, v_ref, kt_ref, vt_ref, zt_ref, b_ref, c_ref, u_ref):
        xv = x_ref[...]
        h = ((xv * _rms(xv)) * g_ref[...]).astype(MXU_DTYPE)
        h_ref[...] = h

        def proj(lo, hi):
            return _dot_nt(h, w_ref[lo:hi, :])

        q_ref[...] = (proj(0, 512) * Q_SCALE).astype(MXU_DTYPE)
        kt = _dot_nt(w_ref[512:1024, :], h)
        vt = _dot_nt(w_ref[1024:OFF_F, :], h)
        kt_ref[...] = kt.astype(MXU_DTYPE)
        vt_ref[...] = vt.astype(MXU_DTYPE)
        k_ref[...] = kt.T.astype(MXU_DTYPE)
        v_ref[...] = vt.T.astype(MXU_DTYPE)
        zt_ref[...] = _dot_nt(w_ref[OFF_F:OFF_B, :], h)
        b_ref[...] = proj(OFF_B, OFF_C)
        c_ref[...] = proj(OFF_C, OFF_U)
        u_ref[...] = proj(OFF_U, IN_PAD)

    sd = jax.ShapeDtypeStruct
    return pl.pallas_call(
        body, name="inproj_fwd", grid=(s // tm,),
        in_specs=[_tok(tm, D_MODEL), _whole((1, D_MODEL)), _whole((IN_PAD, D_MODEL), single=True)],
        out_specs=[_tok(tm, D_MODEL), _tok(tm, 512), _tok(tm, 512), _tok(tm, 512), _feat(512, tm), _feat(512, tm),
                   _feat(128, tm), _tok(tm, 512), _tok(tm, 512), _tok(tm, 512)],
        out_shape=[sd((s, D_MODEL), MXU_DTYPE), sd((s, 512), MXU_DTYPE), sd((s, 512), MXU_DTYPE),
                   sd((s, 512), MXU_DTYPE), sd((512, s), MXU_DTYPE), sd((512, s), MXU_DTYPE), sd((128, s), F32),
                   sd((s, 512), F32), sd((s, 512), F32), sd((s, 512), F32)],
        compiler_params=_params(("arbitrary",)),
    )(x, g_pre, w_t)


def _tri(n, upper):
    r = lax.broadcasted_iota(jnp.int32, (n, n), 0)
    c = lax.broadcasted_iota(jnp.int32, (n, n), 1)
    return ((r <= c) if upper else (r >= c)).astype(MXU_DTYPE)


HEAD_ROWS = 16


def _rows_to_cols(v):
    return jnp.concatenate([v, jnp.zeros((128 - HEAD_ROWS, 128), F32)], axis=0).T


BIAS_PARTS = 3


def _bias_placement():
    place_q = np.zeros((BIAS_PARTS, 128, ATTN_W), np.float32)
    place_k = np.zeros((BIAS_PARTS, 128, ATTN_W), np.float32)
    ones_q = np.zeros((1, ATTN_W), np.float32)
    ones_k = np.zeros((1, ATTN_W), np.float32)
    for h in range(N_HEADS):
        base = 2 * HEAD_DIM * (h // 2) + HEAD_DIM * (1 - h % 2)
        for part in range(BIAS_PARTS):
            place_q[part, h, base + part] = 1.0
            place_k[part, h, base + BIAS_PARTS + part] = -1.0
        ones_q[0, base + BIAS_PARTS:base + 2 * BIAS_PARTS] = 1.0
        ones_k[0, base:base + BIAS_PARTS] = 1.0
    return (jnp.asarray(place_q, MXU_DTYPE), jnp.asarray(place_k, MXU_DTYPE), jnp.asarray(ones_q), jnp.asarray(ones_k))


def _forget_fwd(z_t, b_col):
    s = z_t.shape[1]
    nb = s // 128

    def body(z_ref, b_ref, pq_ref, pk_ref, oq_ref, ok_ref, c_ref, cc_ref, qa_ref, ka_ref):
        upper = _tri(128, True)

        carry = jnp.zeros((HEAD_ROWS, 1), F32)
        for n in range(nb):
            off = n * 128
            lf = jax.nn.log_sigmoid(z_ref[0:HEAD_ROWS, off:off + 128] + b_ref[...])
            cs = _dot_exact(lf, upper, 3) + carry
            c_ref[:, off:off + 128] = cs
            cols = _rows_to_cols(cs)
            cc_ref[off:off + 128, :] = cols
            qa = jnp.broadcast_to(oq_ref[...], (128, ATTN_W))
            ka = jnp.broadcast_to(ok_ref[...], (128, ATTN_W))
            rem = cols
            for part in range(BIAS_PARTS):
                piece = rem.astype(MXU_DTYPE)
                rem = rem - piece.astype(F32)
                qa = qa + _dot(piece, pq_ref[part])
                ka = ka + _dot(piece, pk_ref[part])
            qa_ref[off:off + 128, :] = qa.astype(MXU_DTYPE)
            ka_ref[off:off + 128, :] = ka.astype(MXU_DTYPE)
            carry = carry + jnp.sum(lf, axis=1, keepdims=True)

    sd = jax.ShapeDtypeStruct
    return pl.pallas_call(body, name="forget_fwd",
                          out_shape=[sd((HEAD_ROWS, s), F32), sd((s, 128), F32), sd((s, ATTN_W), MXU_DTYPE),
                                     sd((s, ATTN_W), MXU_DTYPE)],
                          compiler_params=_params())(z_t, b_col, *_bias_placement())


def _aligned(start, size):
    return pl.ds(start if isinstance(start, int) else pl.multiple_of(start, size), size)


def _pair_lanes(pp):
    return _aligned(pp * 2 * HEAD_DIM, 2 * HEAD_DIM)


def _head_rows(h):
    return _aligned(h * HEAD_DIM, HEAD_DIM)


def _only_head(block, hb):
    lane = lax.broadcasted_iota(jnp.int32, block.shape, 1)
    return jnp.where((lane >= HEAD_DIM) if hb else (lane < HEAD_DIM), block, jnp.zeros_like(block))


def _head_col(cols, h):
    lane = lax.broadcasted_iota(jnp.int32, cols.shape, 1)
    return jnp.sum(jnp.where(lane == h, cols, 0.0), axis=1, keepdims=True)


def _other_head(block, other, hb):
    lane = lax.broadcasted_iota(jnp.int32, block.shape, 1)
    return jnp.where((lane >= HEAD_DIM) if hb else (lane < HEAD_DIM), block, other)


def _attn_fwd(qs, k, v_t, q_bias, k_bias, t, shards):
    s = qs.shape[0]
    n = s // t
    pairs = [(i, j) for i in range(n) for j in range(i + 1)]
    it = jnp.asarray(np.array([p[0] for p in pairs], np.int32))
    jt = jnp.asarray(np.array([p[1] for p in pairs], np.int32))
    nw = len(shards)
    last = len(pairs) - 1
    mid = (2 * len(pairs)) // 3

    def body(it_ref, jt_ref, q_ref, k_ref, vt_ref, qb_ref, kb_ref, *rest):
        sh, (o_ref, lse_ref), got = rest[:nw], rest[nw:nw + 2], rest[nw + 2:2 * nw + 2]
        m_sc, l_sc, acc_sc, send, recv = rest[2 * nw + 2:]
        p = pl.program_id(0)
        i = it_ref[p]
        j = jt_ref[p]
        gather_start, gather_forward, gather_finish = _gather_stages(sh, got, send, recv)
        pl.when(p == 0)(gather_start)
        if mid < last:
            pl.when(p == mid)(gather_forward)

        @pl.when(j == 0)
        def _():
            m_sc[...] = jnp.full_like(m_sc, -1e30)
            l_sc[...] = jnp.ones_like(l_sc)
            acc_sc[...] = jnp.zeros_like(acc_sc)

        def pair_step(pp, diagonal):
            lanes = _pair_lanes(pp)
            kp = k_ref[:, lanes]
            qp = q_ref[:, lanes]
            kb = kb_ref[:, lanes]
            qb = qb_ref[:, lanes]
            for hb in range(2):
                h = 2 * pp + hb
                row = pl.ds(h, 1)
                st = _dot_nt(_other_head(kp, kb, hb), _other_head(qp, qb, hb))
                if diagonal:
                    kpos = lax.broadcasted_iota(jnp.int32, (t, t), 0)
                    qpos = lax.broadcasted_iota(jnp.int32, (t, t), 1)
                    st = jnp.where(kpos <= qpos, st, -1e30)
                m_prev = m_sc[row, :]
                m_new = jnp.maximum(m_prev, jnp.max(st, axis=0, keepdims=True))
                alpha = jnp.exp(m_prev - m_new)
                pt = jnp.exp(st - m_new)
                l_sc[row, :] = alpha * l_sc[row, :] + jnp.sum(pt, axis=0, keepdims=True)
                rows = _head_rows(h)
                acc_sc[rows, :] = acc_sc[rows, :] * alpha + _dot(vt_ref[rows, :], pt.astype(MXU_DTYPE))
                m_sc[row, :] = m_new

        @pl.when(j < i)
        def _():
            for pp in range(N_HEADS // 2):
                pair_step(pp, False)

        @pl.when(j == i)
        def _():
            for pp in range(N_HEADS // 2):
                pair_step(pp, True)
                sub = lax.broadcasted_iota(jnp.int32, (2 * HEAD_DIM, t), 0)
                l_pair = jnp.where(sub < HEAD_DIM, l_sc[pl.ds(2 * pp, 1), :], l_sc[pl.ds(2 * pp + 1, 1), :])
                o_t = acc_sc[_aligned(pp * 2 * HEAD_DIM, 2 * HEAD_DIM), :] / l_pair
                o_ref[:, _pair_lanes(pp)] = o_t.T

            lse_ref[...] = m_sc[...] + jnp.log(l_sc[...])

        @pl.when(p == last)
        def _():
            if mid >= last:
                gather_forward()
            gather_finish()

    gs = pltpu.PrefetchScalarGridSpec(
        num_scalar_prefetch=2, grid=(len(pairs),),
        in_specs=[pl.BlockSpec((t, ATTN_W), lambda p, it_, jt_: (it_[p], 0)),
                  pl.BlockSpec((t, ATTN_W), lambda p, it_, jt_: (jt_[p], 0)),
                  pl.BlockSpec((ATTN_W, t), lambda p, it_, jt_: (0, jt_[p])),
                  pl.BlockSpec((t, ATTN_W), lambda p, it_, jt_: (it_[p], 0)),
                  pl.BlockSpec((t, ATTN_W), lambda p, it_, jt_: (jt_[p], 0))] + [ANY] * nw,
        out_specs=[pl.BlockSpec((t, ATTN_W), lambda p, it_, jt_: (it_[p], 0)),
                   pl.BlockSpec((HEAD_ROWS, t), lambda p, it_, jt_: (0, it_[p]))] + [ANY] * nw,
        scratch_shapes=[pltpu.VMEM((HEAD_ROWS, t), F32), pltpu.VMEM((HEAD_ROWS, t), F32), pltpu.VMEM((ATTN_W, t), F32),
                        pltpu.SemaphoreType.DMA((nw, 6)), pltpu.SemaphoreType.DMA((nw, 6))])
    o, lse, *got = pl.pallas_call(
        body, name="attn_fwd", grid_spec=gs,
        out_shape=[jax.ShapeDtypeStruct((s, ATTN_W), F32), jax.ShapeDtypeStruct((HEAD_ROWS, s), F32)]
        + [jax.ShapeDtypeStruct((N_CHIPS,) + a.shape, a.dtype) for a in shards],
        compiler_params=_params(("arbitrary",)),
    )(it, jt, qs, k, v_t, q_bias, k_bias, *shards)
    me = 2 * lax.axis_index("x") + lax.axis_index("y")
    return o, lse, [lax.dynamic_update_index_in_dim(g, own, me, 0) for g, own in zip(got, shards)]


def _shift_down(cur, prev_ref, first):
    row = lax.broadcasted_iota(jnp.int32, cur.shape, 0)
    p7 = jnp.where(first, 0.0, prev_ref[0][7:8, :] * prev_ref[1][7:8, :])
    p6 = jnp.where(first, 0.0, prev_ref[0][6:7, :] * prev_ref[1][6:7, :])
    s1 = jnp.where(row == 0, p7, pltpu.roll(cur, 1, 0))
    s2 = jnp.where(row == 0, p6, jnp.where(row == 1, p7, pltpu.roll(cur, 2, 0)))
    return s1, s2


def _group_ms(v, gmat):
    return _dot_exact(v, gmat, 2) * (1.0 / HEAD_DIM)


def _mixer_fwd(x, o_attn, gate_b, gate_c, u, conv_w, g_attn, g_conv, w_out, g_post, gmat, tm):
    s = x.shape[0]

    def body(x_ref, o_ref, b_ref, c_ref, u_ref, cp_ref, up_ref, cw_ref, ga_ref, gc_ref, wo_ref, gp_ref, gm_ref,
             x2_ref, mg_ref, y_ref, z_ref):
        i = pl.program_id(0)
        cu = c_ref[...] * u_ref[...]
        cu1, cu2 = _shift_down(cu, (cp_ref, up_ref), i == 0)
        z = cw_ref[0:1, :] * cu2 + cw_ref[1:2, :] * cu1 + cw_ref[2:3, :] * cu
        z_ref[...] = z
        cv = b_ref[...] * z
        ov = o_ref[...]
        gm = gm_ref[...]
        ma = ((ov * lax.rsqrt(_group_ms(ov * ov, gm) + EPS)) * ga_ref[...]).astype(MXU_DTYPE)
        mc = ((cv * lax.rsqrt(_group_ms(cv * cv, gm) + EPS)) * gc_ref[...]).astype(MXU_DTYPE)
        mg_ref[:, 0:ATTN_W] = ma
        mg_ref[:, ATTN_W:D_MODEL] = mc
        y = _dot(ma, wo_ref[0:ATTN_W, :]) + _dot(mc, wo_ref[ATTN_W:D_MODEL, :])
        y_ref[...] = y
        x2_ref[...] = x_ref[...] + (y * _rms(y)) * gp_ref[...]

    halo = pl.BlockSpec((8, 512), lambda i: (jnp.maximum(i * (tm // 8) - 1, 0), 0))
    sd = jax.ShapeDtypeStruct
    return pl.pallas_call(
        body, name="mixer_fwd", grid=(s // tm,),
        in_specs=[_tok(tm, D_MODEL), _tok(tm, 512), _tok(tm, 512), _tok(tm, 512), _tok(tm, 512), halo, halo,
                  _whole((3, 512)), _whole((1, 512)), _whole((1, 512)), _whole((D_MODEL, D_MODEL), single=True),
                  _whole((1, D_MODEL)), _whole((512, 512))],
        out_specs=[_tok(tm, D_MODEL), _tok(tm, D_MODEL), _tok(tm, D_MODEL), _tok(tm, 512)],
        out_shape=[sd((s, D_MODEL), F32), sd((s, D_MODEL), MXU_DTYPE), sd((s, D_MODEL), F32), sd((s, 512), F32)],
        compiler_params=_params(("arbitrary",)),
    )(x, o_attn, gate_b, gate_c, u, gate_c, u, conv_w, g_attn, g_conv, w_out, g_post, gmat)


def _ffn_fwd(x2, target, g_pre, w_gu, w_dn, g_post, tm):
    s = x2.shape[0]

    def body(x_ref, t_ref, gpre_ref, wgu_ref, wdn_ref, gpost_ref,
             h_ref, g_ref, up_ref, a_ref, ff_ref, dout_ref, loss_ref):
        xv = x_ref[...]
        h = ((xv * _rms(xv)) * gpre_ref[...]).astype(MXU_DTYPE)
        h_ref[...] = h
        ff = jnp.zeros((tm, D_MODEL), F32)
        for j in range(2):
            cols = slice(j * FF_PIECE, (j + 1) * FF_PIECE)
            g = _dot(h, wgu_ref[j])
            up = _dot(h, wgu_ref[2 + j])
            a = ((g * jax.nn.sigmoid(g)) * up).astype(MXU_DTYPE)
            g_ref[:, cols] = g
            up_ref[:, cols] = up
            a_ref[:, cols] = a
            ff = ff + _dot(a, wdn_ref[j])
        ff_ref[...] = ff
        err = (xv + (ff * _rms(ff)) * gpost_ref[...]) - t_ref[...]
        dout_ref[...] = err * (1.0 / D_MODEL)
        part = jnp.sum(jnp.mean(err * err, axis=-1, keepdims=True), axis=0, keepdims=True)

        @pl.when(pl.program_id(0) == 0)
        def _():
            loss_ref[...] = jnp.zeros_like(loss_ref)

        loss_ref[...] += part

    sd = jax.ShapeDtypeStruct
    return pl.pallas_call(
        body, name="ffn_fwd", grid=(s // tm,),
        in_specs=[_tok(tm, D_MODEL), _tok(tm, D_MODEL), _whole((1, D_MODEL)),
                  _whole((4, D_MODEL, FF_PIECE), single=True), _whole((2, FF_PIECE, D_MODEL), single=True),
                  _whole((1, D_MODEL))],
        out_specs=[_tok(tm, D_MODEL), _tok(tm, D_FF), _tok(tm, D_FF), _tok(tm, D_FF), _tok(tm, D_MODEL),
                   _tok(tm, D_MODEL), _whole((8, 128))],
        out_shape=[sd((s, D_MODEL), MXU_DTYPE), sd((s, D_FF), F32), sd((s, D_FF), F32), sd((s, D_FF), MXU_DTYPE),
                   sd((s, D_MODEL), F32), sd((s, D_MODEL), F32), sd((8, 128), F32)],
        compiler_params=_params(("arbitrary",)),
    )(x2, target, g_pre, w_gu, w_dn, g_post)


def _norm_bwd(dy, normed, rinv, gain):
    t = dy * gain
    return rinv * (t - normed * jnp.mean(t * normed, axis=-1, keepdims=True))


def _acc_rows(ref, first, val):
    @pl.when(first)
    def _():
        ref[...] = jnp.zeros_like(ref)

    ref[...] += jnp.sum(val, axis=0, keepdims=True)


def _ffn_bwd(dout, ff, x2, g, up, g_post, g_pre, w_gu, w_dn, tm):
    s = x2.shape[0]

    def body(do_ref, ff_ref, x_ref, g_ref, up_ref, gpost_ref, gpre_ref, wgu_ref, wdn_ref,
             dx_ref, dff_ref, dgu_ref, dgpost_ref, dgpre_ref):
        first = pl.program_id(0) == 0
        ffv = ff_ref[...]
        rf = _rms(ffv)
        n = ffv * rf
        do = do_ref[...]
        _acc_rows(dgpost_ref, first, do * n)
        dff = _norm_bwd(do, n, rf, gpost_ref[...]).astype(MXU_DTYPE)
        dff_ref[...] = dff
        dh = jnp.zeros((tm, D_MODEL), F32)
        for j in range(2):
            cols = slice(j * FF_PIECE, (j + 1) * FF_PIECE)
            da = _dot_nt(dff, wdn_ref[j])
            gv = g_ref[:, cols]
            sg = jax.nn.sigmoid(gv)
            dg = (da * up_ref[:, cols] * (sg * (1.0 + gv * (1.0 - sg)))).astype(MXU_DTYPE)
            du = (da * (gv * sg)).astype(MXU_DTYPE)
            dgu_ref[:, cols] = dg
            dgu_ref[:, D_FF + j * FF_PIECE:D_FF + (j + 1) * FF_PIECE] = du
            dh = dh + _dot_nt(dg, wgu_ref[j]) + _dot_nt(du, wgu_ref[2 + j])
        xv = x_ref[...]
        r2 = _rms(xv)
        nx = xv * r2
        _acc_rows(dgpre_ref, first, dh * nx)
        dx_ref[...] = do + _norm_bwd(dh, nx, r2, gpre_ref[...])

    sd = jax.ShapeDtypeStruct
    return pl.pallas_call(
        body, name="ffn_bwd", grid=(s // tm,),
        in_specs=[_tok(tm, D_MODEL), _tok(tm, D_MODEL), _tok(tm, D_MODEL), _tok(tm, D_FF), _tok(tm, D_FF),
                  _whole((1, D_MODEL)), _whole((1, D_MODEL)),
                  _whole((4, D_MODEL, FF_PIECE), single=True), _whole((2, FF_PIECE, D_MODEL), single=True)],
        out_specs=[_tok(tm, D_MODEL), _tok(tm, D_MODEL), _tok(tm, 2 * D_FF), _whole((1, D_MODEL)),
                   _whole((1, D_MODEL))],
        out_shape=[sd((s, D_MODEL), F32), sd((s, D_MODEL), MXU_DTYPE), sd((s, 2 * D_FF), MXU_DTYPE),
                   sd((1, D_MODEL), F32), sd((1, D_MODEL), F32)],
        compiler_params=_params(("arbitrary",)),
    )(dout, ff, x2, g, up, g_post, g_pre, w_gu, w_dn)


def _tn_matmul(a, b, tm, tn, tk, name):
    s, m = a.shape
    n = b.shape[1]

    def body(a_ref, b_ref, o_ref):
        @pl.when(pl.program_id(2) == 0)
        def _():
            o_ref[...] = jnp.zeros_like(o_ref)

        o_ref[...] += lax.dot_general(a_ref[...], b_ref[...], (((0,), (0,)), ((), ())), preferred_element_type=F32)

    return pl.pallas_call(
        body, name=name, grid=(m // tm, n // tn, s // tk),
        in_specs=[pl.BlockSpec((tk, tm), lambda i, j, kk: (kk, i)), pl.BlockSpec((tk, tn), lambda i, j, kk: (kk, j))],
        out_specs=pl.BlockSpec((tm, tn), lambda i, j, kk: (i, j)),
        out_shape=jax.ShapeDtypeStruct((m, n), F32),
        compiler_params=_params(("arbitrary", "arbitrary", "arbitrary")),
    )(a, b)


def _mixer_bwd(dx2, y, o_attn, gate_b, z, g_post, g_attn, g_conv, w_out, gmat, sel, tm, ready, kinds):
    s = dx2.shape[0]
    nw = len(ready)
    nt = s // tm

    def body(d_ref, y_ref, o_ref, b_ref, z_ref, gp_ref, ga_ref, gc_ref, wo_ref, gm_ref, sel_ref, *rest):
        grads = rest[:nw]
        dy_ref, do_ref, db_ref, dz_ref, delta_ref, dgp_ref, dga_ref, dgc_ref = rest[nw:nw + 8]
        taken = rest[nw + 8:2 * nw + 8]
        send, recv = rest[2 * nw + 8:]
        first = pl.program_id(0) == 0
        pair_start, pair_finish = _pair_stages(grads, kinds, taken, send, recv)
        pl.when(first)(pair_start)
        yv = y_ref[...]
        ry = _rms(yv)
        ny = yv * ry
        d = d_ref[...]
        _acc_rows(dgp_ref, first, d * ny)
        dy = _norm_bwd(d, ny, ry, gp_ref[...]).astype(MXU_DTYPE)
        dy_ref[...] = dy
        dm = _dot_nt(dy, wo_ref[...])
        gm = gm_ref[...]

        def group_bwd(val, dmv, gain, dg_ref):
            rg = lax.rsqrt(_group_ms(val * val, gm) + EPS)
            nv = val * rg
            _acc_rows(dg_ref, first, dmv * nv)
            t = dmv * gain
            return rg * (t - nv * _group_ms(t * nv, gm))

        ov = o_ref[...]
        d_o = group_bwd(ov, dm[:, 0:ATTN_W], ga_ref[...], dga_ref)
        do_ref[...] = d_o.astype(MXU_DTYPE)
        delta_ref[...] = _dot_exact(d_o * ov, sel_ref[...], 2).T[0:HEAD_ROWS, :]
        zv = z_ref[...]
        bv = b_ref[...]
        d_cv = group_bwd(bv * zv, dm[:, ATTN_W:D_MODEL], gc_ref[...], dgc_ref)
        db_ref[...] = d_cv * zv
        dz_ref[...] = d_cv * bv
        pl.when(pl.program_id(0) == nt - 1)(pair_finish)

    sd = jax.ShapeDtypeStruct
    taken_shape = [sd((N_CHIPS, g.shape[-2], g.shape[-1] if kd == "rows" else g.shape[-1] // N_CHIPS), F32)
                   for g, kd in zip(ready, kinds)]
    out = pl.pallas_call(
        body, name="mixer_bwd", grid=(nt,),
        in_specs=[_tok(tm, D_MODEL), _tok(tm, D_MODEL), _tok(tm, 512), _tok(tm, 512), _tok(tm, 512),
                  _whole((1, D_MODEL)), _whole((1, 512)), _whole((1, 512)),
                  _whole((D_MODEL, D_MODEL), single=True), _whole((512, 512)), _whole((512, 128))] + [ANY] * nw,
        out_specs=[_tok(tm, D_MODEL), _tok(tm, 512), _tok(tm, 512), _tok(tm, 512), _feat(HEAD_ROWS, tm),
                   _whole((1, D_MODEL)), _whole((1, 512)), _whole((1, 512))] + [ANY] * nw,
        out_shape=[sd((s, D_MODEL), MXU_DTYPE), sd((s, 512), MXU_DTYPE), sd((s, 512), F32), sd((s, 512), F32),
                   sd((HEAD_ROWS, s), F32), sd((1, D_MODEL), F32), sd((1, 512), F32), sd((1, 512), F32)] + taken_shape,
        scratch_shapes=[pltpu.SemaphoreType.DMA((nw, N_CHIPS)), pltpu.SemaphoreType.DMA((nw, N_CHIPS))],
        compiler_params=_params(("arbitrary",)),
    )(dx2, y, o_attn, gate_b, z, g_post, g_attn, g_conv, w_out, gmat, sel, *ready)
    return out[:8], out[8:]


def _attn_bwd(qs, k, k_t, v, do, c_rows, c_cols, lse, delta, t, parts):
    s = qs.shape[0]
    n = s // t
    pairs = [(i, j) for j in range(n) for i in range(j, n)]
    it = jnp.asarray(np.array([p[0] for p in pairs], np.int32))
    jt = jnp.asarray(np.array([p[1] for p in pairs], np.int32))

    nw = len(parts)

    def body(it_ref, jt_ref, q_ref, k_ref, kt_ref, v_ref, do_ref, cq_ref, ck_ref, lse_ref, dl_ref, *rest):
        pb = rest[:nw]
        dq_ref, dk_ref, dv_ref, dc_ref, dcq_ref = rest[nw:nw + 5]
        rcv = rest[nw + 5:2 * nw + 5]
        dk_sc, dv_sc, dc_sc, send, recv = rest[2 * nw + 5:]
        p = pl.program_id(0)
        i = it_ref[p]
        j = jt_ref[p]
        chip_start, chip_finish = _chip_stages(pb, rcv, send, recv)

        @pl.when(p == 0)
        def _():
            chip_start()
            dq_ref[...] = jnp.zeros_like(dq_ref)
            dcq_ref[...] = jnp.zeros_like(dcq_ref)

        @pl.when(i == j)
        def _():
            dk_sc[...] = jnp.zeros_like(dk_sc)
            dv_sc[...] = jnp.zeros_like(dv_sc)
            dc_sc[...] = jnp.zeros_like(dc_sc)

        def pair_step(pp, diagonal):
            lanes = _pair_lanes(pp)
            qp = q_ref[:, lanes]
            kp = k_ref[:, lanes]
            vp = v_ref[:, lanes]
            dop = do_ref[:, lanes]
            ck_all = ck_ref[...]
            lane = lax.broadcasted_iota(jnp.int32, (t, 128), 1)
            for hb in range(2):
                h = 2 * pp + hb
                row = pl.ds(h, 1)
                bias = (cq_ref[row, :] - lse_ref[row, :]) - _head_col(ck_all, h)
                pt = jnp.exp(_dot_nt(_only_head(kp, hb), qp) + bias)
                if diagonal:
                    kpos = lax.broadcasted_iota(jnp.int32, (t, t), 0)
                    qpos = lax.broadcasted_iota(jnp.int32, (t, t), 1)
                    pt = jnp.where(kpos <= qpos, pt, 0.0)
                dv_sc[:, lanes] += _dot(pt.astype(MXU_DTYPE), _only_head(dop, hb))
                dst = pt * (_dot_nt(_only_head(vp, hb), dop) - dl_ref[row, :])
                dc_sc[...] -= jnp.where(lane == h, jnp.sum(dst, axis=1, keepdims=True), 0.0)
                dcq_ref[i, row, :] += jnp.sum(dst, axis=0, keepdims=True)
                dsb = dst.astype(MXU_DTYPE)
                dk_sc[:, lanes] += _dot(dsb, _only_head(qp, hb))
                rows = _head_rows(h)
                dq_ref[i, rows, :] += _dot(kt_ref[rows, :], dsb)

        @pl.when(i > j)
        def _():
            for pp in range(N_HEADS // 2):
                pair_step(pp, False)

        @pl.when(i == j)
        def _():
            for pp in range(N_HEADS // 2):
                pair_step(pp, True)

        @pl.when(i == n - 1)
        def _():
            dk_ref[...] = dk_sc[...]
            dv_ref[...] = dv_sc[...]
            dc_ref[...] = dc_sc[...]

        pl.when(p == len(pairs) - 1)(chip_finish)

    qi = lambda p, it_, jt_: (it_[p], 0)
    kj = lambda p, it_, jt_: (jt_[p], 0)
    row_i = lambda p, it_, jt_: (0, it_[p])
    gs = pltpu.PrefetchScalarGridSpec(
        num_scalar_prefetch=2, grid=(len(pairs),),
        in_specs=[pl.BlockSpec((t, ATTN_W), qi), pl.BlockSpec((t, ATTN_W), kj),
                  pl.BlockSpec((ATTN_W, t), lambda p, it_, jt_: (0, jt_[p])),
                  pl.BlockSpec((t, ATTN_W), kj), pl.BlockSpec((t, ATTN_W), qi),
                  pl.BlockSpec((HEAD_ROWS, t), row_i), pl.BlockSpec((t, 128), kj),
                  pl.BlockSpec((HEAD_ROWS, t), row_i), pl.BlockSpec((HEAD_ROWS, t), row_i)] + [ANY] * nw,
        out_specs=[pl.BlockSpec((n, ATTN_W, t), lambda p, it_, jt_: (0, 0, 0)),
                   pl.BlockSpec((t, ATTN_W), kj), pl.BlockSpec((t, ATTN_W), kj),
                   pl.BlockSpec((t, 128), kj),
                   pl.BlockSpec((n, HEAD_ROWS, t), lambda p, it_, jt_: (0, 0, 0))] + [ANY] * nw,
        scratch_shapes=[pltpu.VMEM((t, ATTN_W), F32), pltpu.VMEM((t, ATTN_W), F32),
                        pltpu.VMEM((t, 128), F32), pltpu.SemaphoreType.DMA((nw, 3)), pltpu.SemaphoreType.DMA((nw, 3))])
    sd = jax.ShapeDtypeStruct
    out = pl.pallas_call(
        body, name="attn_bwd", grid_spec=gs,
        out_shape=[sd((n, ATTN_W, t), F32), sd((s, ATTN_W), F32), sd((s, ATTN_W), F32),
                   sd((s, 128), F32), sd((n, HEAD_ROWS, t), F32)] + [sd((3,) + a.shape[1:], a.dtype) for a in parts],
        compiler_params=_params(("arbitrary",)),
    )(it, jt, qs, k, k_t, v, do, c_rows, c_cols, lse, delta, *parts)
    return out[:5], out[5:]


def _forget_bwd(dc_rows, dc_cols, z_t, b_col):
    s = z_t.shape[1]
    nb = s // 128

    def body(dr_ref, dcc_ref, z_ref, b_ref, dz_ref, db_ref):
        lower = _tri(128, False)
        real = lax.broadcasted_iota(jnp.int32, (HEAD_ROWS, 128), 0) < N_HEADS

        tail = jnp.zeros((HEAD_ROWS, 1), F32)
        dbias = jnp.zeros((HEAD_ROWS, 1), F32)
        for m in range(nb):
            off = (nb - 1 - m) * 128
            dc = dr_ref[:, off:off + 128] + dcc_ref[off:off + 128, :].T[0:HEAD_ROWS, :]
            dlf = _dot_exact(dc, lower, 3) + tail
            dz = dlf * jax.nn.sigmoid(-(z_ref[0:HEAD_ROWS, off:off + 128] + b_ref[...]))
            dz = jnp.where(real, dz, 0.0)
            dz_ref[off:off + 128, :] = _rows_to_cols(dz)
            tail = tail + jnp.sum(dc, axis=1, keepdims=True)
            dbias = dbias + jnp.sum(dz, axis=1, keepdims=True)
        db_ref[...] = jnp.broadcast_to(dbias, db_ref.shape)

    return pl.pallas_call(
        body, name="forget_bwd",
        out_shape=[jax.ShapeDtypeStruct((s, 128), F32), jax.ShapeDtypeStruct((HEAD_ROWS, 128), F32)],
        compiler_params=_params())(dc_rows, dc_cols, z_t, b_col)


def _inproj_bwd(dz, gate_c, u, conv_w, dq, dk, dv, dzf, db, x, dx2, g_pre, w_t, tm):
    s = x.shape[0]
    nt = s // tm
    per = dq.shape[2] // tm
    assert dq.shape[2] % tm == 0 and dq.shape[:2] == (nt // per, ATTN_W)

    def body(dz_ref, dzn_ref, c_ref, u_ref, cp_ref, up_ref, cw_ref, dq_ref, dk_ref, dv_ref, dzf_ref, db_ref,
             x_ref, dx2_ref, g_ref, w_ref, gx_ref, dp_ref, dg_ref, dcw_ref):
        i = pl.program_id(0)
        first = i == 0
        last = i == nt - 1
        dzv = dz_ref[...]
        row = lax.broadcasted_iota(jnp.int32, dzv.shape, 0)
        n0 = jnp.where(last, 0.0, dzn_ref[0:1, :])
        n1 = jnp.where(last, 0.0, dzn_ref[1:2, :])
        dz1 = jnp.where(row == tm - 1, n0, pltpu.roll(dzv, tm - 1, 0))
        dz2 = jnp.where(row == tm - 1, n1, jnp.where(row == tm - 2, n0, pltpu.roll(dzv, tm - 2, 0)))
        dcu = cw_ref[2:3, :] * dzv + cw_ref[1:2, :] * dz1 + cw_ref[0:1, :] * dz2
        cv = c_ref[...]
        uv = u_ref[...]
        cu = cv * uv
        cu1, cu2 = _shift_down(cu, (cp_ref, up_ref), first)

        @pl.when(first)
        def _():
            dcw_ref[...] = jnp.zeros_like(dcw_ref)

        dcw_ref[0:1, :] += jnp.sum(dzv * cu2, axis=0, keepdims=True)
        dcw_ref[1:2, :] += jnp.sum(dzv * cu1, axis=0, keepdims=True)
        dcw_ref[2:3, :] += jnp.sum(dzv * cu, axis=0, keepdims=True)

        dp_ref[:, 0:512] = (dq_ref[0].T * Q_SCALE).astype(MXU_DTYPE)
        dp_ref[:, 512:1024] = dk_ref[...].astype(MXU_DTYPE)
        dp_ref[:, 1024:OFF_F] = dv_ref[...].astype(MXU_DTYPE)
        dp_ref[:, OFF_F:OFF_B] = dzf_ref[...].astype(MXU_DTYPE)
        dp_ref[:, OFF_B:OFF_C] = db_ref[...].astype(MXU_DTYPE)
        dp_ref[:, OFF_C:OFF_U] = (dcu * uv).astype(MXU_DTYPE)
        dp_ref[:, OFF_U:IN_PAD] = (dcu * cv).astype(MXU_DTYPE)
        dh = _dot(dp_ref[...], w_ref[...])
        xv = x_ref[...]
        r1 = _rms(xv)
        nx = xv * r1
        _acc_rows(dg_ref, first, dh * nx)
        gx_ref[...] = dx2_ref[...] + _norm_bwd(dh, nx, r1, g_ref[...])

    prev = pl.BlockSpec((8, 512), lambda i: (jnp.maximum(i * (tm // 8) - 1, 0), 0))
    nxt = pl.BlockSpec((8, 512), lambda i: (jnp.minimum((i + 1) * (tm // 8), s // 8 - 1), 0))
    sd = jax.ShapeDtypeStruct
    return pl.pallas_call(
        body, name="inproj_bwd", grid=(nt,),
        in_specs=[_tok(tm, 512), nxt, _tok(tm, 512), _tok(tm, 512), prev, prev, _whole((3, 512)),
                  pl.BlockSpec((1, ATTN_W, tm), lambda i: (i // per, 0, i % per)), _tok(tm, 512), _tok(tm, 512),
                  _tok(tm, 128),
                  _tok(tm, 512),
                  _tok(tm, D_MODEL), _tok(tm, D_MODEL), _whole((1, D_MODEL)), _whole((IN_PAD, D_MODEL), single=True)],
        out_specs=[_tok(tm, D_MODEL), _tok(tm, IN_PAD), _whole((1, D_MODEL)), _whole((8, 512))],
        out_shape=[sd((s, D_MODEL), F32), sd((s, IN_PAD), MXU_DTYPE), sd((1, D_MODEL), F32), sd((8, 512), F32)],
        compiler_params=_params(("arbitrary",)),
    )(dz, dz, gate_c, u, gate_c, u, conv_w, dq, dk, dv, dzf, db, x, dx2, g_pre, w_t)


def _tile(s, want):
    return want if s % want == 0 else s


def _halves(a):
    return a.reshape(2, a.shape[0] // 2, a.shape[1])


def _device_step(x, target, w, w_in_t, c_idx, me_idx):
    s = x.shape[0]
    tm = _tile(s, 512)
    tf = _tile(s, 256)
    ta = _tile(s, 512)
    tkk = _tile(s, 2048)
    gidx = np.arange(512) // HEAD_DIM
    gmat = jnp.asarray(gidx[:, None] == gidx[None, :], MXU_DTYPE)
    sel = jnp.asarray(gidx[:, None] == np.arange(128)[None, :], MXU_DTYPE)
    g_mix_pre, g_mix_post, g_ffn_pre, g_ffn_post = w["g_mix_pre"], w["g_mix_post"], w["g_ffn_pre"], w["g_ffn_post"]
    g_attn, g_conv, b_forget = w["g_attn_out"], w["g_conv_out"], w["b_forget"]
    shard = {n: _halves(w[n][0].astype(MXU_DTYPE)) for n in BIG[1:]}
    piece_rows = IN_W // N_CHIPS

    g_in, conv_all = _gather_weights([w_in_t.reshape(piece_rows, D_MODEL).astype(MXU_DTYPE)], w["conv_w"][0])
    w_rows = g_in.reshape(IN_W, D_MODEL)
    w_t = jnp.concatenate([w_rows[:OFF_F + N_HEADS], jnp.zeros((OFF_B - OFF_F - N_HEADS, D_MODEL), MXU_DTYPE),
                           w_rows[OFF_F + N_HEADS:]], axis=0)
    conv_w = jnp.transpose(conv_all, (1, 0, 2)).reshape(3, CONV_W)

    h1, qs, k, v, k_t, v_t, z_t, gate_b, gate_c, u = _inproj_fwd(x, g_mix_pre, w_t, tm)
    b_col = jnp.pad(jnp.transpose(b_forget), ((0, HEAD_ROWS - N_HEADS), (0, 0)))
    c_rows, c_cols, q_bias, k_bias = _forget_fwd(z_t, b_col)
    o_attn, lse, (g_out, g_gu, g_dn) = _attn_fwd(qs, k, v_t, q_bias, k_bias, ta,
                                                 [shard["w_out"], shard["w_gate_up"], shard["w_down"]])
    w_out = g_out.reshape(D_MODEL, D_MODEL)
    w_gu = g_gu.reshape(N_CHIPS, D_MODEL, FF_PIECE)
    w_dn = g_dn.reshape(2, FF_PIECE, D_MODEL)
    x2, merged, y, z = _mixer_fwd(x, o_attn, gate_b, gate_c, u, conv_w, g_attn, g_conv, w_out, g_mix_post, gmat, tm)
    h2, g, up, a, ff, dout, loss_acc = _ffn_fwd(x2, target, g_ffn_pre, w_gu, w_dn, g_ffn_post, tf)

    dx2, dff, dgu, dg_ffn_post, dg_ffn_pre = _ffn_bwd(dout, ff, x2, g, up, g_ffn_post, g_ffn_pre, w_gu, w_dn, tf)
    dw_dn = _tn_matmul(a, dff, FF_PIECE, 1024, tkk, "dw_down").reshape(N_CHIPS, 2, D_FF // (2 * N_CHIPS), D_MODEL)
    dw_gu = _tn_matmul(h2, dgu, 1024, FF_PIECE, tkk, "dw_gate_up").reshape(2, D_MODEL // 2, 2 * D_FF)
    (dy, d_o, d_b, dz, delta, dg_mix_post, dg_attn, dg_conv), (a_gu, a_dn) = _mixer_bwd(
        dx2, y, o_attn, gate_b, z, g_mix_post, g_attn, g_conv, w_out, gmat, sel, tm, [dw_gu, dw_dn], ["cols", "rows"])
    dw_out = _tn_matmul(merged, dy, 1024, 1024, tkk, "dw_out")
    sum_gu = _pair_sum(c_idx, dw_gu, "cols", a_gu, "pair_sum_w_gate_up")
    sum_dn = _pair_sum(c_idx, dw_dn, "rows", a_dn, "pair_sum_w_down")
    (dq_t, dk, dv, dc_cols, dcq), (r_gu, r_dn) = _attn_bwd(
        qs, k, k_t, v, d_o, c_rows, c_cols, lse, delta, ta, [sum_gu[1], sum_dn[1]])
    dc_rows = jnp.transpose(dcq, (1, 0, 2)).reshape(HEAD_ROWS, s)
    dzf, db_f = _forget_bwd(dc_rows, dc_cols, z_t, b_col)
    grad_x, dproj, dg_mix_pre, dcw = _inproj_bwd(dz, gate_c, u, conv_w, dq_t, dk, dv, dzf, d_b,
                                                 x, dx2, g_mix_pre, w_t, tm)
    dw_t = _tn_matmul(dproj, h1, 640, 1024, tkk, "dw_in")
    dw_in = jnp.concatenate([dw_t[:OFF_F + N_HEADS], dw_t[OFF_B:]], axis=0).reshape(N_CHIPS, piece_rows, D_MODEL)
    dw_out = dw_out.reshape(N_CHIPS, 2, D_MODEL // (2 * N_CHIPS), D_MODEL)

    a_in, a_out = _pair_exchange([dw_in, dw_out], ["lanes", "rows"])
    sum_in = _pair_sum(c_idx, dw_in, "lanes", a_in, "pair_sum_w_in")
    sum_out = _pair_sum(c_idx, dw_out, "rows", a_out, "pair_sum_w_out")
    small = dict(b_forget=db_f[:N_HEADS, 0], g_attn_out=dg_attn, g_conv_out=dg_conv, g_mix_pre=dg_mix_pre,
                 g_mix_post=dg_mix_post, g_ffn_pre=dg_ffn_pre, g_ffn_post=dg_ffn_post)
    r_in, r_out, small_all = _chip_exchange([sum_in[1], sum_out[1]], _pack_small(small, dcw[:3], loss_acc[0, 0]))
    totals = [_chip_sum(me_idx, sb[0], r, "chip_sum_" + n)
              for n, sb, r in zip(BIG, (sum_in, sum_out, sum_gu, sum_dn), (r_in, r_out, r_gu, r_dn))]
    return grad_x, totals, _pair_share(totals), small_all


BIG = ("w_in", "w_out", "w_gate_up", "w_down")
ANY = pl.BlockSpec(memory_space=pl.ANY)


def _place():
    x, y, c = lax.axis_index("x"), lax.axis_index("y"), lax.axis_index("c")
    others = [(1 - x, y), (x, 1 - y), (1 - x, 1 - y)]
    return x, y, c, 2 * x + y, others, [2 * px + py for px, py in others]


def _remote(src, dst, send, recv, dev):
    return pltpu.make_async_remote_copy(src_ref=src, dst_ref=dst, send_sem=send, recv_sem=recv,
                                        device_id=dev, device_id_type=MESH_ID)


def _gather_stages(sh, outs, send, recv):
    x, y, c, me, others, chips = _place()
    sib = (x, y, 1 - c)
    every = [(w, kk) for w in range(len(sh)) for kk in range(3)]

    def half_of(ref, half, piece=None):
        ref = ref if piece is None else ref.at[piece]
        if len(ref.shape) == 3:
            return ref.at[half]
        hc = ref.shape[1] // 2
        return ref.at[:, pl.ds(pl.multiple_of(half * hc, 128), hc)]

    def first(w, kk):
        return _remote(half_of(sh[w], c), half_of(outs[w], c, me), send.at[w, kk], recv.at[w, kk], (*others[kk], c))

    def landed(w, kk):
        r = half_of(outs[w], c, chips[kk])
        return _remote(r, r, send.at[w, kk], recv.at[w, kk], (*others[kk], c))

    def onward(w, kk, half):
        r = half_of(outs[w], half, chips[kk])
        return _remote(r, r, send.at[w, 3 + kk], recv.at[w, 3 + kk], sib)

    def start():
        for w, kk in every:
            first(w, kk).start()

    def forward():
        for w, kk in every:
            landed(w, kk).wait_recv()
            onward(w, kk, c).start()

    def finish():
        for w, kk in every:
            onward(w, kk, 1 - c).wait_recv()
        for w, kk in every:
            first(w, kk).wait_send()
            onward(w, kk, c).wait_send()

    return start, forward, finish


def _pair_piece(ref, kind, p, half):
    if kind == "rows":
        return ref.at[p, half]
    if kind == "lanes":
        hc = ref.shape[2] // 2
        return ref.at[p, :, pl.ds(pl.multiple_of(half * hc, 128), hc)]
    cols = ref.shape[2] // N_CHIPS
    return ref.at[half, :, pl.ds(p * cols, cols)]


def _pair_stages(g, kinds, a, send, recv):
    x, y, c, _, _, _ = _place()
    copies = [_remote(_pair_piece(g[w], kinds[w], p, 1 - c), a[w].at[p], send.at[w, p], recv.at[w, p], (x, y, 1 - c))
              for w in range(len(g)) for p in range(N_CHIPS)]

    def start():
        for cp in copies:
            cp.start()

    def finish():
        for cp in copies:
            cp.wait()

    return start, finish


def _chip_stages(pb, rcv, send, recv):
    x, y, c, _, others, chips = _place()
    copies = [_remote(pb[w].at[chips[kk]], rcv[w].at[kk], send.at[w, kk], recv.at[w, kk], (*others[kk], c))
              for w in range(len(pb)) for kk in range(3)]

    def start():
        for cp in copies:
            cp.start()

    def finish():
        for cp in copies:
            cp.wait()

    return start, finish


def _gather_weights(shards, conv_w):
    n = len(shards)

    def body(*refs):
        sh, cw, outs, cwo = refs[:n], refs[n], refs[n + 1:2 * n + 1], refs[2 * n + 1]
        send, recv = refs[2 * n + 2:]
        x, y, c, me, others, chips = _place()
        start, forward, finish = _gather_stages(sh, outs, send, recv)
        start()
        small = [_remote(cw, cwo.at[me], send.at[n, kk], recv.at[n, kk], (*others[kk], c)) for kk in range(3)]
        for cp in small:
            cp.start()
        forward()
        for kk in range(3):
            _remote(cw, cwo.at[chips[kk]], send.at[n, kk], recv.at[n, kk], (*others[kk], c)).wait_recv()
        finish()
        for cp in small:
            cp.wait_send()

    out_shape = [jax.ShapeDtypeStruct((N_CHIPS,) + s.shape, s.dtype) for s in shards]
    out_shape.append(jax.ShapeDtypeStruct((N_CHIPS,) + conv_w.shape, conv_w.dtype))
    got = pl.pallas_call(
        body, name="gather_weights", in_specs=[ANY] * (n + 1), out_specs=[ANY] * (n + 1), out_shape=out_shape,
        scratch_shapes=[pltpu.SemaphoreType.DMA((n + 1, 6)), pltpu.SemaphoreType.DMA((n + 1, 6))],
    )(*shards, conv_w)
    me = 2 * lax.axis_index("x") + lax.axis_index("y")
    return [lax.dynamic_update_index_in_dim(g, own, me, 0) for g, own in zip(got, list(shards) + [conv_w])]


def _taken_shape(g, kind):
    if kind == "rows":
        return (N_CHIPS,) + g.shape[2:]
    if kind == "lanes":
        return g.shape[:2] + (g.shape[2] // 2,)
    return (N_CHIPS, g.shape[1], g.shape[2] // N_CHIPS)


def _pair_exchange(grads, kinds):
    n = len(grads)

    def body(*refs):
        g, a = refs[:n], refs[n:2 * n]
        start, finish = _pair_stages(g, kinds, a, *refs[2 * n:])
        start()
        finish()

    return pl.pallas_call(
        body, name="pair_exchange", in_specs=[ANY] * n, out_specs=[ANY] * n,
        out_shape=[jax.ShapeDtypeStruct(_taken_shape(g, kd), g.dtype) for g, kd in zip(grads, kinds)],
        scratch_shapes=[pltpu.SemaphoreType.DMA((n, N_CHIPS)), pltpu.SemaphoreType.DMA((n, N_CHIPS))],
    )(*grads)


def _pair_sum(c_idx, g, kind, a, name):
    _, half, cols = a.shape
    if kind == "rows":
        mine = pl.BlockSpec((1, 1, half, cols), lambda p, cr: (p, cr[0], 0, 0))
    elif kind == "lanes":
        mine = pl.BlockSpec((1, half, cols), lambda p, cr: (p, 0, cr[0]))
    else:
        mine = pl.BlockSpec((1, half, cols), lambda p, cr: (cr[0], 0, p))

    def body(c_ref, g_ref, a_ref, pf_ref, pb_ref):
        tot = (g_ref[0, 0] if kind == "rows" else g_ref[0]) + a_ref[0]
        pf_ref[0] = tot
        pb_ref[0] = tot.astype(BF16)

    gs = pltpu.PrefetchScalarGridSpec(
        num_scalar_prefetch=1, grid=(N_CHIPS,),
        in_specs=[mine,
                  pl.BlockSpec((1, half, cols), lambda p, cr: (p, 0, 0))],
        out_specs=[pl.BlockSpec((1, half, cols), lambda p, cr: (p, 0, 0)),
                   pl.BlockSpec((1, half, cols), lambda p, cr: (p, 0, 0))])
    return pl.pallas_call(
        body, name=name, grid_spec=gs,
        out_shape=[jax.ShapeDtypeStruct((N_CHIPS, half, cols), F32), jax.ShapeDtypeStruct((N_CHIPS, half, cols), BF16)],
        compiler_params=_params(("arbitrary",)),
    )(c_idx, g, a)


def _chip_exchange(parts, small):
    n = len(parts)

    def body(*refs):
        pb, sm, rcv, smg = refs[:n], refs[n], refs[n + 1:2 * n + 1], refs[2 * n + 1]
        send, recv, ssend, srecv, loc = refs[2 * n + 2:]
        x, y, c, _, others, chips = _place()
        mine = 4 * x + 2 * y + c
        own = pltpu.make_async_copy(sm, smg.at[mine], loc)
        own.start()
        copies = [_remote(pb[w].at[chips[kk]], rcv[w].at[kk], send.at[w, kk], recv.at[w, kk], (px, py, c))
                  for w in range(n) for kk, (px, py) in enumerate(others)]
        for r in range(1, 8):
            peer = (1 - x if r & 4 else x, 1 - y if r & 2 else y, 1 - c if r & 1 else c)
            copies.append(_remote(sm, smg.at[mine], ssend.at[r - 1], srecv.at[r - 1], peer))
        for cp in copies:
            cp.start()
        for w in range(n):
            for kk, (px, py) in enumerate(others):
                _remote(pb[w].at[chips[kk]], rcv[w].at[kk], send.at[w, kk], recv.at[w, kk], (px, py, c)).wait_recv()
        for r in range(1, 8):
            px, py, pc = (1 - x if r & 4 else x, 1 - y if r & 2 else y, 1 - c if r & 1 else c)
            _remote(sm, smg.at[4 * px + 2 * py + pc], ssend.at[r - 1], srecv.at[r - 1], (px, py, pc)).wait_recv()
        for cp in copies:
            cp.wait_send()
        own.wait()

    out_shape = [jax.ShapeDtypeStruct((3,) + p.shape[1:], p.dtype) for p in parts]
    out_shape.append(jax.ShapeDtypeStruct((8,) + small.shape, small.dtype))
    return pl.pallas_call(
        body, name="chip_exchange", in_specs=[ANY] * (n + 1), out_specs=[ANY] * (n + 1), out_shape=out_shape,
        scratch_shapes=[pltpu.SemaphoreType.DMA((n, 3)), pltpu.SemaphoreType.DMA((n, 3)),
                        pltpu.SemaphoreType.DMA((7,)), pltpu.SemaphoreType.DMA((7,)), pltpu.SemaphoreType.DMA(())],
    )(*parts, small)


def _chip_sum(me_idx, pf, rcv, name):
    _, half, cols = pf.shape

    def body(me_ref, pf_ref, r_ref, t_ref):
        t_ref[...] = ((pf_ref[0] + r_ref[0].astype(F32)) + r_ref[1].astype(F32)) + r_ref[2].astype(F32)

    gs = pltpu.PrefetchScalarGridSpec(
        num_scalar_prefetch=1, grid=(1,),
        in_specs=[pl.BlockSpec((1, half, cols), lambda i, mr: (mr[0], 0, 0)),
                  pl.BlockSpec((3, half, cols), lambda i, mr: (0, 0, 0))],
        out_specs=pl.BlockSpec((half, cols), lambda i, mr: (0, 0)))
    return pl.pallas_call(
        body, name=name, grid_spec=gs, out_shape=jax.ShapeDtypeStruct((half, cols), F32),
        compiler_params=_params(("arbitrary",)),
    )(me_idx, pf, rcv)


def _pair_share(totals):
    n = len(totals)

    def body(*refs):
        t, g = refs[:n], refs[n:2 * n]
        send, recv = refs[2 * n:]
        x, y, c, _, _, _ = _place()
        copies = [_remote(t[w], g[w], send.at[w], recv.at[w], (x, y, 1 - c)) for w in range(n)]
        for cp in copies:
            cp.start()
        for cp in copies:
            cp.wait()

    return pl.pallas_call(
        body, name="pair_share", in_specs=[ANY] * n, out_specs=[ANY] * n,
        out_shape=[jax.ShapeDtypeStruct(t.shape, t.dtype) for t in totals],
        scratch_shapes=[pltpu.SemaphoreType.DMA((n,)), pltpu.SemaphoreType.DMA((n,))],
    )(*totals)


def _adamw_math(w, g, m, v):
    m = ADAM_B1 * m + (1.0 - ADAM_B1) * g
    v = ADAM_B2 * v + (1.0 - ADAM_B2) * (g * g)
    m_hat = m / (1.0 - ADAM_B1 ** ADAM_STEP)
    v_hat = v / (1.0 - ADAM_B2 ** ADAM_STEP)
    delta = -ADAM_LR * (m_hat / (jnp.sqrt(v_hat) + ADAM_EPS) + ADAM_WD * w)
    return delta, m, v


def _adamw(c_idx, w, mine, theirs, m, v, nb, name):
    rows, cols = w.shape
    tr = rows // (2 * nb)

    def body(c_ref, w_ref, a_ref, b_ref, m_ref, v_ref, g_ref, d_ref, nm_ref, nv_ref):
        g = jnp.where(pl.program_id(0) == c_ref[0], a_ref[...], b_ref[...])
        g_ref[...] = g
        d_ref[...], nm_ref[...], nv_ref[...] = _adamw_math(w_ref[...], g, m_ref[...], v_ref[...])

    full = pl.BlockSpec((tr, cols), lambda hh, i, cr: (hh * nb + i, 0))
    half = pl.BlockSpec((tr, cols), lambda hh, i, cr: (i, 0))
    gs = pltpu.PrefetchScalarGridSpec(num_scalar_prefetch=1, grid=(2, nb), in_specs=[full, half, half, full, full],
                                      out_specs=[full] * 4)
    return pl.pallas_call(
        body, name=name, grid_spec=gs, out_shape=[jax.ShapeDtypeStruct((rows, cols), F32)] * 4,
        compiler_params=_params(("arbitrary", "arbitrary")),
    )(c_idx, w, mine, theirs, m, v)


def _adamw_lanes(c_idx, w, mine, theirs, m, v, name):
    rows, _, cols = w.shape
    hc = cols // 2

    def body(c_ref, w_ref, a_ref, b_ref, m_ref, v_ref, g_ref, d_ref, nm_ref, nv_ref):
        g = jnp.where(pl.program_id(0) == c_ref[0], a_ref[...], b_ref[...])
        g_ref[:, 0, :] = g
        d_ref[:, 0, :], nm_ref[:, 0, :], nv_ref[:, 0, :] = _adamw_math(w_ref[:, 0, :], g, m_ref[:, 0, :], v_ref[:, 0, :])

    full = pl.BlockSpec((rows, 1, hc), lambda hh, cr: (0, 0, hh))
    half = pl.BlockSpec((rows, hc), lambda hh, cr: (0, 0))
    gs = pltpu.PrefetchScalarGridSpec(num_scalar_prefetch=1, grid=(2,), in_specs=[full, half, half, full, full],
                                      out_specs=[full] * 4)
    return pl.pallas_call(
        body, name=name, grid_spec=gs, out_shape=[jax.ShapeDtypeStruct((rows, 1, cols), F32)] * 4,
        compiler_params=_params(("arbitrary",)),
    )(c_idx, w, mine, theirs, m, v)


SMALL = ("g_mix_pre", "g_mix_post", "g_ffn_pre", "g_ffn_post")
SMALL_ALL = SMALL + ("g_attn_out", "g_conv_out", "conv_w", "b_forget")
SMALL_AT = {"g_mix_pre": (0, 0, 1024), "g_mix_post": (1, 0, 1024), "g_ffn_pre": (2, 0, 1024),
            "g_ffn_post": (3, 0, 1024), "g_attn_out": (4, 0, 512), "g_conv_out": (4, 512, 512),
            "b_forget": (7, 0, N_HEADS)}
CONV_AT = ((5, 0), (5, 512), (6, 0))
LOSS_AT = (6, 512)


def _pack_small(t, conv_full, loss_sum):
    conv = jnp.concatenate([conv_full.reshape(1, 3 * CONV_W), loss_sum.reshape(1, 1),
                            jnp.zeros((1, 2048 - 3 * CONV_W - 1), F32)], axis=1).reshape(2, 1024)
    return jnp.concatenate([t[n].reshape(1, 1024) for n in SMALL]
                           + [jnp.concatenate([t["g_attn_out"].reshape(1, 512), t["g_conv_out"].reshape(1, 512)], axis=1),
                              conv, jnp.pad(t["b_forget"].reshape(1, N_HEADS), ((0, 0), (0, 1024 - N_HEADS)))], axis=0)


def _small_update(me_idx, gathered, w, m, v):
    def body(me_ref, gg_ref, *refs):
        k = len(SMALL_ALL)
        w_refs, m_refs, v_refs = refs[:k], refs[k:2 * k], refs[2 * k:3 * k]
        loss_ref = refs[3 * k]
        outs = refs[3 * k + 1:3 * k + 1 + 4 * k]
        sums = refs[-1]
        g = gg_ref[0]
        for dev in range(1, 8):
            g = g + gg_ref[dev]
        sums[...] = g
        loss_ref[...] = sums[LOSS_AT[0]:LOSS_AT[0] + 1, LOSS_AT[1]:LOSS_AT[1] + 1]
        mine = pl.multiple_of(me_ref[0] * 128, 128)
        for idx, name in enumerate(SMALL_ALL):
            g_ref, d_ref, nm_ref, nv_ref = outs[4 * idx:4 * idx + 4]
            if name == "conv_w":
                for r, (row, lo) in enumerate(CONV_AT):
                    gr = sums[row:row + 1, pl.ds(lo + mine, 128)]
                    g_ref[0, r:r + 1, :] = gr
                    d_ref[0, r:r + 1, :], nm_ref[0, r:r + 1, :], nv_ref[0, r:r + 1, :] = _adamw_math(
                        w_refs[idx][0, r:r + 1, :], gr, m_refs[idx][0, r:r + 1, :], v_refs[idx][0, r:r + 1, :])
            else:
                row, lo, n = SMALL_AT[name]
                gr = sums[row:row + 1, lo:lo + n]
                g_ref[...] = gr
                d_ref[...], nm_ref[...], nv_ref[...] = _adamw_math(w_refs[idx][...], gr, m_refs[idx][...],
                                                                    v_refs[idx][...])

    def whole(a):
        nd = a.ndim
        return pl.BlockSpec(a.shape, lambda i, mr: (0,) * nd)

    ins = [t[n] for t in (w, m, v) for n in SMALL_ALL]
    out_shape = [jax.ShapeDtypeStruct((1, 1), F32)]
    for n in SMALL_ALL:
        out_shape += [jax.ShapeDtypeStruct(w[n].shape, F32)] * 4
    gs = pltpu.PrefetchScalarGridSpec(
        num_scalar_prefetch=1, grid=(1,), in_specs=[whole(gathered)] + [whole(a) for a in ins],
        out_specs=[whole(o) for o in out_shape], scratch_shapes=[pltpu.VMEM((8, 1024), F32)])
    out = pl.pallas_call(body, name="small_update", grid_spec=gs, out_shape=out_shape,
                         compiler_params=_params(("arbitrary",)))(me_idx, gathered, *ins)
    return out[0], {n: out[1 + 4 * i:5 + 4 * i] for i, n in enumerate(SMALL_ALL)}


def kernel(x, w_in, b_forget, conv_w, g_attn_out, g_conv_out, w_out, g_mix_pre, g_mix_post, w_gate_up, w_down, g_ffn_pre, g_ffn_post, loss_target, m_w_in, m_b_forget, m_conv_w, m_g_attn_out, m_g_conv_out, m_w_out, m_g_mix_pre, m_g_mix_post, m_w_gate_up, m_w_down, m_g_ffn_pre, m_g_ffn_post, v_w_in, v_b_forget, v_conv_w, v_g_attn_out, v_g_conv_out, v_w_out, v_g_mix_pre, v_g_mix_post, v_w_gate_up, v_w_down, v_g_ffn_pre, v_g_ffn_post):
    w = dict(w_in=w_in, b_forget=b_forget, conv_w=conv_w, g_attn_out=g_attn_out, g_conv_out=g_conv_out, w_out=w_out,
             g_mix_pre=g_mix_pre, g_mix_post=g_mix_post, w_gate_up=w_gate_up, w_down=w_down, g_ffn_pre=g_ffn_pre,
             g_ffn_post=g_ffn_post)
    m = dict(w_in=m_w_in, b_forget=m_b_forget, conv_w=m_conv_w, g_attn_out=m_g_attn_out, g_conv_out=m_g_conv_out,
             w_out=m_w_out, g_mix_pre=m_g_mix_pre, g_mix_post=m_g_mix_post, w_gate_up=m_w_gate_up, w_down=m_w_down,
             g_ffn_pre=m_g_ffn_pre, g_ffn_post=m_g_ffn_post)
    v = dict(w_in=v_w_in, b_forget=v_b_forget, conv_w=v_conv_w, g_attn_out=v_g_attn_out, g_conv_out=v_g_conv_out,
             w_out=v_w_out, g_mix_pre=v_g_mix_pre, g_mix_post=v_g_mix_post, w_gate_up=v_w_gate_up, w_down=v_w_down,
             g_ffn_pre=v_g_ffn_pre, g_ffn_post=v_g_ffn_post)
    cx, cy, cc = lax.axis_index("x"), lax.axis_index("y"), lax.axis_index("c")
    me = 2 * cx + cy
    c_idx = cc.astype(jnp.int32).reshape(1)
    me_idx = me.astype(jnp.int32).reshape(1)

    stored = lambda a: jnp.transpose(a, (2, 0, 1))
    grad_x, totals, shared, small_all = _device_step(x[0], loss_target[0], w, stored(w_in), c_idx, me_idx)

    gsum, delta, new_m, new_v = {}, {}, {}, {}
    for n, mine, theirs in zip(BIG, totals, shared):
        if n == "w_in":
            res = _adamw_lanes(c_idx, stored(w_in), mine, theirs, stored(m_w_in), stored(v_w_in), "adamw_w_in")
            gsum[n], delta[n], new_m[n], new_v[n] = [jnp.transpose(r, (1, 2, 0)) for r in res]
        else:
            gs, d, nm, nv = _adamw(c_idx, w[n][0], mine, theirs, m[n][0], v[n][0], 2, "adamw_" + n)
            gsum[n], delta[n], new_m[n], new_v[n] = gs[None], d[None], nm[None], nv[None]
    loss_sum, small_new = _small_update(me_idx, small_all, w, m, v)
    for n in SMALL_ALL:
        gsum[n], delta[n], new_m[n], new_v[n] = small_new[n]
    loss = 0.5 * loss_sum[0, 0]

    order = ("w_in", "b_forget", "conv_w", "g_attn_out", "g_conv_out", "w_out", "g_mix_pre", "g_mix_post",
             "w_gate_up", "w_down", "g_ffn_pre", "g_ffn_post")
    return (loss, grad_x[None], *[gsum[n] for n in order], *[delta[n] for n in order],
            *[new_m[n] for n in order], *[new_v[n] for n in order])
```

```python
import functools

import jax
import jax.numpy as jnp
import numpy as np
from jax import lax
from jax.experimental import pallas as pl
from jax.experimental.pallas import tpu as pltpu

F32 = jnp.float32
BF16 = jnp.bfloat16
MXU_DTYPE = jnp.bfloat16

D_MODEL = 1024
HEAD_DIM = 64
N_HEADS = 8
ATTN_W = 512
CONV_W = 512
D_FF = 2816
FF_PIECE = 1408
EPS = 1e-6
Q_SCALE = HEAD_DIM ** -0.5

OFF_F = 1536
OFF_B = 1664
OFF_C = 2176
OFF_U = 2688
IN_PAD = 3200
IN_W = 3080
N_CHIPS = 4

ADAM_LR = 0.001
ADAM_B1 = 0.9
ADAM_B2 = 0.999
ADAM_EPS = 1e-08
ADAM_WD = 0.01
ADAM_STEP = 10

VMEM_LIMIT_V7X = 56 * 1024 * 1024
MESH_ID = pl.DeviceIdType.MESH


def _params(sem=None, vmem=VMEM_LIMIT_V7X):
    kw = {"vmem_limit_bytes": vmem}
    if sem is not None:
        kw["dimension_semantics"] = sem
    return pltpu.CompilerParams(**kw)


def _dot(a, b):
    return jnp.dot(a, b, preferred_element_type=F32)


def _dot_nt(a, b):
    return lax.dot_general(a, b, (((1,), (1,)), ((), ())), preferred_element_type=F32)


def _dot_exact(x, ones, parts):
    if ones.dtype == F32:
        return _dot(x, ones)
    acc = None
    rem = x
    for _ in range(parts):
        piece = rem.astype(BF16)
        rem = rem - piece.astype(F32)
        term = _dot(piece, ones)
        acc = term if acc is None else acc + term
    return acc


def _rms(v):
    return lax.rsqrt(jnp.mean(v * v, axis=-1, keepdims=True) + EPS)


def _tok(tm, w):
    return pl.BlockSpec((tm, w), lambda i: (i, 0))


def _whole(shape, single=False):
    nd = len(shape)
    if single:
        return pl.BlockSpec(shape, lambda i: (0,) * nd, pipeline_mode=pl.Buffered(1))
    return pl.BlockSpec(shape, lambda i: (0,) * nd)


def _feat(rows, tm):
    return pl.BlockSpec((rows, tm), lambda i: (0, i))


def _inproj_fwd(x, g_pre, w_t, tm):
    s = x.shape[0]

    def body(x_ref, g_ref, w_ref, h_ref, q_ref, k_ref, v_ref, kt_ref, vt_ref, zt_ref, b_ref, c_ref, u_ref):
        xv = x_ref[...]
        h = ((xv * _rms(xv)) * g_ref[...]).astype(MXU_DTYPE)
        h_ref[...] = h

        def proj(lo, hi):
            return _dot_nt(h, w_ref[lo:hi, :])

        q_ref[...] = (proj(0, 512) * Q_SCALE).astype(MXU_DTYPE)
        kt = _dot_nt(w_ref[512:1024, :], h)
        vt = _dot_nt(w_ref[1024:OFF_F, :], h)
        kt_ref[...] = kt.astype(MXU_DTYPE)
        vt_ref[...] = vt.astype(MXU_DTYPE)
        k_ref[...] = kt.T.astype(MXU_DTYPE)
        v_ref[...] = vt.T.astype(MXU_DTYPE)
        zt_ref[...] = _dot_nt(w_ref[OFF_F:OFF_B, :], h)
        b_ref[...] = proj(OFF_B, OFF_C)
        c_ref[...] = proj(OFF_C, OFF_U)
        u_ref[...] = proj(OFF_U, IN_PAD)

    sd = jax.ShapeDtypeStruct
    return pl.pallas_call(
        body, name="inproj_fwd", grid=(s // tm,),
        in_specs=[_tok(tm, D_MODEL), _whole((1, D_MODEL)), _whole((IN_PAD, D_MODEL), single=True)],
        out_specs=[_tok(tm, D_MODEL), _tok(tm, 512), _tok(tm, 512), _tok(tm, 512), _feat(512, tm), _feat(512, tm),
                   _feat(128, tm), _tok(tm, 512), _tok(tm, 512), _tok(tm, 512)],
        out_shape=[sd((s, D_MODEL), MXU_DTYPE), sd((s, 512), MXU_DTYPE), sd((s, 512), MXU_DTYPE),
                   sd((s, 512), MXU_DTYPE), sd((512, s), MXU_DTYPE), sd((512, s), MXU_DTYPE), sd((128, s), F32),
                   sd((s, 512), F32), sd((s, 512), F32), sd((s, 512), F32)],
        compiler_params=_params(("arbitrary",)),
    )(x, g_pre, w_t)


def _tri(n, upper):
    r = lax.broadcasted_iota(jnp.int32, (n, n), 0)
    c = lax.broadcasted_iota(jnp.int32, (n, n), 1)
    return ((r <= c) if upper else (r >= c)).astype(MXU_DTYPE)


HEAD_ROWS = 16


def _rows_to_cols(v):
    return jnp.concatenate([v, jnp.zeros((128 - HEAD_ROWS, 128), F32)], axis=0).T


BIAS_PARTS = 3


def _bias_placement():
    place_q = np.zeros((BIAS_PARTS, 128, ATTN_W), np.float32)
    place_k = np.zeros((BIAS_PARTS, 128, ATTN_W), np.float32)
    ones_q = np.zeros((1, ATTN_W), np.float32)
    ones_k = np.zeros((1, ATTN_W), np.float32)
    for h in range(N_HEADS):
        base = 2 * HEAD_DIM * (h // 2) + HEAD_DIM * (1 - h % 2)
        for part in range(BIAS_PARTS):
            place_q[part, h, base + part] = 1.0
            place_k[part, h, base + BIAS_PARTS + part] = -1.0
        ones_q[0, base + BIAS_PARTS:base + 2 * BIAS_PARTS] = 1.0
        ones_k[0, base:base + BIAS_PARTS] = 1.0
    return (jnp.asarray(place_q, MXU_DTYPE), jnp.asarray(place_k, MXU_DTYPE), jnp.asarray(ones_q), jnp.asarray(ones_k))


def _forget_fwd(z_t, b_col):
    s = z_t.shape[1]
    nb = s // 128

    def body(z_ref, b_ref, pq_ref, pk_ref, oq_ref, ok_ref, c_ref, cc_ref, qa_ref, ka_ref):
        upper = _tri(128, True)

        carry = jnp.zeros((HEAD_ROWS, 1), F32)
        for n in range(nb):
            off = n * 128
            lf = jax.nn.log_sigmoid(z_ref[0:HEAD_ROWS, off:off + 128] + b_ref[...])
            cs = _dot_exact(lf, upper, 3) + carry
            c_ref[:, off:off + 128] = cs
            cols = _rows_to_cols(cs)
            cc_ref[off:off + 128, :] = cols
            qa = jnp.broadcast_to(oq_ref[...], (128, ATTN_W))
            ka = jnp.broadcast_to(ok_ref[...], (128, ATTN_W))
            rem = cols
            for part in range(BIAS_PARTS):
                piece = rem.astype(MXU_DTYPE)
                rem = rem - piece.astype(F32)
                qa = qa + _dot(piece, pq_ref[part])
                ka = ka + _dot(piece, pk_ref[part])
            qa_ref[off:off + 128, :] = qa.astype(MXU_DTYPE)
            ka_ref[off:off + 128, :] = ka.astype(MXU_DTYPE)
            carry = carry + jnp.sum(lf, axis=1, keepdims=True)

    sd = jax.ShapeDtypeStruct
    return pl.pallas_call(body, name="forget_fwd",
                          out_shape=[sd((HEAD_ROWS, s), F32), sd((s, 128), F32), sd((s, ATTN_W), MXU_DTYPE),
                                     sd((s, ATTN_W), MXU_DTYPE)],
                          compiler_params=_params())(z_t, b_col, *_bias_placement())


def _aligned(start, size):
    return pl.ds(start if isinstance(start, int) else pl.multiple_of(start, size), size)


def _pair_lanes(pp):
    return _aligned(pp * 2 * HEAD_DIM, 2 * HEAD_DIM)


def _head_rows(h):
    return _aligned(h * HEAD_DIM, HEAD_DIM)


def _only_head(block, hb):
    lane = lax.broadcasted_iota(jnp.int32, block.shape, 1)
    return jnp.where((lane >= HEAD_DIM) if hb else (lane < HEAD_DIM), block, jnp.zeros_like(block))


def _head_col(cols, h):
    lane = lax.broadcasted_iota(jnp.int32, cols.shape, 1)
    return jnp.sum(jnp.where(lane == h, cols, 0.0), axis=1, keepdims=True)


def _other_head(block, other, hb):
    lane = lax.broadcasted_iota(jnp.int32, block.shape, 1)
    return jnp.where((lane >= HEAD_DIM) if hb else (lane < HEAD_DIM), block, other)


def _attn_fwd(qs, k, v_t, q_bias, k_bias, t, shards):
    s = qs.shape[0]
    n = s // t
    pairs = [(i, j) for i in range(n) for j in range(i + 1)]
    it = jnp.asarray(np.array([p[0] for p in pairs], np.int32))
    jt = jnp.asarray(np.array([p[1] for p in pairs], np.int32))
    nw = len(shards)
    last = len(pairs) - 1
    mid = (2 * len(pairs)) // 3

    def body(it_ref, jt_ref, q_ref, k_ref, vt_ref, qb_ref, kb_ref, *rest):
        sh, (o_ref, lse_ref), got = rest[:nw], rest[nw:nw + 2], rest[nw + 2:2 * nw + 2]
        m_sc, l_sc, acc_sc, send, recv = rest[2 * nw + 2:]
        p = pl.program_id(0)
        i = it_ref[p]
        j = jt_ref[p]
        gather_start, gather_forward, gather_finish = _gather_stages(sh, got, send, recv)
        pl.when(p == 0)(gather_start)
        if mid < last:
            pl.when(p == mid)(gather_forward)

        @pl.when(j == 0)
        def _():
            m_sc[...] = jnp.full_like(m_sc, -1e30)
            l_sc[...] = jnp.ones_like(l_sc)
            acc_sc[...] = jnp.zeros_like(acc_sc)

        def pair_step(pp, diagonal):
            lanes = _pair_lanes(pp)
            kp = k_ref[:, lanes]
            qp = q_ref[:, lanes]
            kb = kb_ref[:, lanes]
            qb = qb_ref[:, lanes]
            for hb in range(2):
                h = 2 * pp + hb
                row = pl.ds(h, 1)
                rows = _head_rows(h)
                st = _dot_nt(_other_head(kp, kb, hb), _other_head(qp, qb, hb))
                if diagonal:
                    kpos = lax.broadcasted_iota(jnp.int32, (t, t), 0)
                    qpos = lax.broadcasted_iota(jnp.int32, (t, t), 1)
                    st = jnp.where(kpos <= qpos, st, -1e30)
                m_prev = m_sc[row, :]
                m_new = jnp.maximum(m_prev, jnp.max(st, axis=0, keepdims=True))
                alpha = jnp.exp(m_prev - m_new)
                pt = jnp.exp(st - m_new)
                l_sc[row, :] = alpha * l_sc[row, :] + jnp.sum(pt, axis=0, keepdims=True)
                acc_sc[rows, :] = acc_sc[rows, :] * alpha + _dot(vt_ref[rows, :], pt.astype(MXU_DTYPE))
                m_sc[row, :] = m_new

        @pl.when(j < i)
        def _():
            for pp in range(N_HEADS // 2):
                pair_step(pp, False)

        @pl.when(j == i)
        def _():
            for pp in range(N_HEADS // 2):
                pair_step(pp, True)
                sub = lax.broadcasted_iota(jnp.int32, (2 * HEAD_DIM, t), 0)
                l_pair = jnp.where(sub < HEAD_DIM, l_sc[pl.ds(2 * pp, 1), :], l_sc[pl.ds(2 * pp + 1, 1), :])
                o_t = acc_sc[_aligned(pp * 2 * HEAD_DIM, 2 * HEAD_DIM), :] / l_pair
                o_ref[:, _pair_lanes(pp)] = o_t.T

            lse_ref[...] = m_sc[...] + jnp.log(l_sc[...])

        @pl.when(p == last)
        def _():
            if mid >= last:
                gather_forward()
            gather_finish()

    gs = pltpu.PrefetchScalarGridSpec(
        num_scalar_prefetch=2, grid=(len(pairs),),
        in_specs=[pl.BlockSpec((t, ATTN_W), lambda p, it_, jt_: (it_[p], 0)),
                  pl.BlockSpec((t, ATTN_W), lambda p, it_, jt_: (jt_[p], 0)),
                  pl.BlockSpec((ATTN_W, t), lambda p, it_, jt_: (0, jt_[p])),
                  pl.BlockSpec((t, ATTN_W), lambda p, it_, jt_: (it_[p], 0)),
                  pl.BlockSpec((t, ATTN_W), lambda p, it_, jt_: (jt_[p], 0))] + [ANY] * nw,
        out_specs=[pl.BlockSpec((t, ATTN_W), lambda p, it_, jt_: (it_[p], 0)),
                   pl.BlockSpec((HEAD_ROWS, t), lambda p, it_, jt_: (0, it_[p]))] + [ANY] * nw,
        scratch_shapes=[pltpu.VMEM((HEAD_ROWS, t), F32), pltpu.VMEM((HEAD_ROWS, t), F32), pltpu.VMEM((ATTN_W, t), F32),
                        pltpu.SemaphoreType.DMA((nw, 6)), pltpu.SemaphoreType.DMA((nw, 6))])
    o, lse, *got = pl.pallas_call(
        body, name="attn_fwd", grid_spec=gs,
        out_shape=[jax.ShapeDtypeStruct((s, ATTN_W), F32), jax.ShapeDtypeStruct((HEAD_ROWS, s), F32)]
        + [jax.ShapeDtypeStruct((N_CHIPS,) + a.shape, a.dtype) for a in shards],
        compiler_params=_params(("arbitrary",)),
    )(it, jt, qs, k, v_t, q_bias, k_bias, *shards)
    me = 2 * lax.axis_index("x") + lax.axis_index("y")
    return o, lse, [lax.dynamic_update_index_in_dim(g, own, me, 0) for g, own in zip(got, shards)]


def _shift_down(cur, prev_ref, first):
    row = lax.broadcasted_iota(jnp.int32, cur.shape, 0)
    p7 = jnp.where(first, 0.0, prev_ref[0][7:8, :] * prev_ref[1][7:8, :])
    p6 = jnp.where(first, 0.0, prev_ref[0][6:7, :] * prev_ref[1][6:7, :])
    s1 = jnp.where(row == 0, p7, pltpu.roll(cur, 1, 0))
    s2 = jnp.where(row == 0, p6, jnp.where(row == 1, p7, pltpu.roll(cur, 2, 0)))
    return s1, s2


def _group_ms(v, gmat):
    return _dot_exact(v, gmat, 2) * (1.0 / HEAD_DIM)


def _mixer_fwd(x, o_attn, gate_b, gate_c, u, conv_w, g_attn, g_conv, w_out, g_post, gmat, tm):
    s = x.shape[0]

    def body(x_ref, o_ref, b_ref, c_ref, u_ref, cp_ref, up_ref, cw_ref, ga_ref, gc_ref, wo_ref, gp_ref, gm_ref,
             x2_ref, mg_ref, y_ref, z_ref):
        i = pl.program_id(0)
        cu = c_ref[...] * u_ref[...]
        cu1, cu2 = _shift_down(cu, (cp_ref, up_ref), i == 0)
        z = cw_ref[0:1, :] * cu2 + cw_ref[1:2, :] * cu1 + cw_ref[2:3, :] * cu
        z_ref[...] = z
        cv = b_ref[...] * z
        ov = o_ref[...]
        gm = gm_ref[...]
        ma = ((ov * lax.rsqrt(_group_ms(ov * ov, gm) + EPS)) * ga_ref[...]).astype(MXU_DTYPE)
        mc = ((cv * lax.rsqrt(_group_ms(cv * cv, gm) + EPS)) * gc_ref[...]).astype(MXU_DTYPE)
        mg_ref[:, 0:ATTN_W] = ma
        mg_ref[:, ATTN_W:D_MODEL] = mc
        y = _dot(ma, wo_ref[0:ATTN_W, :]) + _dot(mc, wo_ref[ATTN_W:D_MODEL, :])
        y_ref[...] = y
        x2_ref[...] = x_ref[...] + (y * _rms(y)) * gp_ref[...]

    halo = pl.BlockSpec((8, 512), lambda i: (jnp.maximum(i * (tm // 8) - 1, 0), 0))
    sd = jax.ShapeDtypeStruct
    return pl.pallas_call(
        body, name="mixer_fwd", grid=(s // tm,),
        in_specs=[_tok(tm, D_MODEL), _tok(tm, 512), _tok(tm, 512), _tok(tm, 512), _tok(tm, 512), halo, halo,
                  _whole((3, 512)), _whole((1, 512)), _whole((1, 512)), _whole((D_MODEL, D_MODEL), single=True),
                  _whole((1, D_MODEL)), _whole((512, 512))],
        out_specs=[_tok(tm, D_MODEL), _tok(tm, D_MODEL), _tok(tm, D_MODEL), _tok(tm, 512)],
        out_shape=[sd((s, D_MODEL), F32), sd((s, D_MODEL), MXU_DTYPE), sd((s, D_MODEL), F32), sd((s, 512), F32)],
        compiler_params=_params(("arbitrary",)),
    )(x, o_attn, gate_b, gate_c, u, gate_c, u, conv_w, g_attn, g_conv, w_out, g_post, gmat)


def _ffn_fwd(x2, target, g_pre, w_gu, w_dn, g_post, tm):
    s = x2.shape[0]

    def body(x_ref, t_ref, gpre_ref, wgu_ref, wdn_ref, gpost_ref,
             h_ref, g_ref, up_ref, a_ref, ff_ref, dout_ref, loss_ref):
        xv = x_ref[...]
        h = ((xv * _rms(xv)) * gpre_ref[...]).astype(MXU_DTYPE)
        h_ref[...] = h
        ff = jnp.zeros((tm, D_MODEL), F32)
        for j in range(2):
            cols = slice(j * FF_PIECE, (j + 1) * FF_PIECE)
            g = _dot(h, wgu_ref[j])
            up = _dot(h, wgu_ref[2 + j])
            a = ((g * jax.nn.sigmoid(g)) * up).astype(MXU_DTYPE)
            g_ref[:, cols] = g
            up_ref[:, cols] = up
            a_ref[:, cols] = a
            ff = ff + _dot(a, wdn_ref[j])
        ff_ref[...] = ff
        err = (xv + (ff * _rms(ff)) * gpost_ref[...]) - t_ref[...]
        dout_ref[...] = err * (1.0 / D_MODEL)
        part = jnp.sum(jnp.mean(err * err, axis=-1, keepdims=True), axis=0, keepdims=True)

        @pl.when(pl.program_id(0) == 0)
        def _():
            loss_ref[...] = jnp.zeros_like(loss_ref)

        loss_ref[...] += part

    sd = jax.ShapeDtypeStruct
    return pl.pallas_call(
        body, name="ffn_fwd", grid=(s // tm,),
        in_specs=[_tok(tm, D_MODEL), _tok(tm, D_MODEL), _whole((1, D_MODEL)),
                  _whole((4, D_MODEL, FF_PIECE), single=True), _whole((2, FF_PIECE, D_MODEL), single=True),
                  _whole((1, D_MODEL))],
        out_specs=[_tok(tm, D_MODEL), _tok(tm, D_FF), _tok(tm, D_FF), _tok(tm, D_FF), _tok(tm, D_MODEL),
                   _tok(tm, D_MODEL), _whole((8, 128))],
        out_shape=[sd((s, D_MODEL), MXU_DTYPE), sd((s, D_FF), F32), sd((s, D_FF), F32), sd((s, D_FF), MXU_DTYPE),
                   sd((s, D_MODEL), F32), sd((s, D_MODEL), F32), sd((8, 128), F32)],
        compiler_params=_params(("arbitrary",)),
    )(x2, target, g_pre, w_gu, w_dn, g_post)


def _norm_bwd(dy, normed, rinv, gain):
    t = dy * gain
    return rinv * (t - normed * jnp.mean(t * normed, axis=-1, keepdims=True))


def _acc_rows(ref, first, val):
    @pl.when(first)
    def _():
        ref[...] = jnp.zeros_like(ref)

    ref[...] += jnp.sum(val, axis=0, keepdims=True)


def _ffn_bwd(dout, ff, x2, g, up, g_post, g_pre, w_gu, w_dn, tm):
    s = x2.shape[0]

    def body(do_ref, ff_ref, x_ref, g_ref, up_ref, gpost_ref, gpre_ref, wgu_ref, wdn_ref,
             dx_ref, dff_ref, dgu_ref, dgpost_ref, dgpre_ref):
        first = pl.program_id(0) == 0
        ffv = ff_ref[...]
        rf = _rms(ffv)
        n = ffv * rf
        do = do_ref[...]
        _acc_rows(dgpost_ref, first, do * n)
        dff = _norm_bwd(do, n, rf, gpost_ref[...]).astype(MXU_DTYPE)
        dff_ref[...] = dff
        dh = jnp.zeros((tm, D_MODEL), F32)
        for j in range(2):
            cols = slice(j * FF_PIECE, (j + 1) * FF_PIECE)
            da = _dot_nt(dff, wdn_ref[j])
            gv = g_ref[:, cols]
            sg = jax.nn.sigmoid(gv)
            dg = (da * up_ref[:, cols] * (sg * (1.0 + gv * (1.0 - sg)))).astype(MXU_DTYPE)
            du = (da * (gv * sg)).astype(MXU_DTYPE)
            dgu_ref[:, cols] = dg
            dgu_ref[:, D_FF + j * FF_PIECE:D_FF + (j + 1) * FF_PIECE] = du
            dh = dh + _dot_nt(dg, wgu_ref[j]) + _dot_nt(du, wgu_ref[2 + j])
        xv = x_ref[...]
        r2 = _rms(xv)
        nx = xv * r2
        _acc_rows(dgpre_ref, first, dh * nx)
        dx_ref[...] = do + _norm_bwd(dh, nx, r2, gpre_ref[...])

    sd = jax.ShapeDtypeStruct
    return pl.pallas_call(
        body, name="ffn_bwd", grid=(s // tm,),
        in_specs=[_tok(tm, D_MODEL), _tok(tm, D_MODEL), _tok(tm, D_MODEL), _tok(tm, D_FF), _tok(tm, D_FF),
                  _whole((1, D_MODEL)), _whole((1, D_MODEL)),
                  _whole((4, D_MODEL, FF_PIECE), single=True), _whole((2, FF_PIECE, D_MODEL), single=True)],
        out_specs=[_tok(tm, D_MODEL), _tok(tm, D_MODEL), _tok(tm, 2 * D_FF), _whole((1, D_MODEL)),
                   _whole((1, D_MODEL))],
        out_shape=[sd((s, D_MODEL), F32), sd((s, D_MODEL), MXU_DTYPE), sd((s, 2 * D_FF), MXU_DTYPE),
                   sd((1, D_MODEL), F32), sd((1, D_MODEL), F32)],
        compiler_params=_params(("arbitrary",)),
    )(dout, ff, x2, g, up, g_post, g_pre, w_gu, w_dn)


def _tn_matmul(a, b, tm, tn, tk, name):
    s, m = a.shape
    n = b.shape[1]

    def body(a_ref, b_ref, o_ref):
        @pl.when(pl.program_id(2) == 0)
        def _():
            o_ref[...] = jnp.zeros_like(o_ref)

        o_ref[...] += lax.dot_general(a_ref[...], b_ref[...], (((0,), (0,)), ((), ())), preferred_element_type=F32)

    return pl.pallas_call(
        body, name=name, grid=(m // tm, n // tn, s // tk),
        in_specs=[pl.BlockSpec((tk, tm), lambda i, j, kk: (kk, i)), pl.BlockSpec((tk, tn), lambda i, j, kk: (kk, j))],
        out_specs=pl.BlockSpec((tm, tn), lambda i, j, kk: (i, j)),
        out_shape=jax.ShapeDtypeStruct((m, n), F32),
        compiler_params=_params(("arbitrary", "arbitrary", "arbitrary")),
    )(a, b)


def _mixer_bwd(dx2, y, o_attn, gate_b, z, g_post, g_attn, g_conv, w_out, gmat, sel, tm, ready, kinds):
    s = dx2.shape[0]
    nw = len(ready)
    nt = s // tm

    def body(d_ref, y_ref, o_ref, b_ref, z_ref, gp_ref, ga_ref, gc_ref, wo_ref, gm_ref, sel_ref, *rest):
        grads = rest[:nw]
        dy_ref, do_ref, db_ref, dz_ref, delta_ref, dgp_ref, dga_ref, dgc_ref = rest[nw:nw + 8]
        taken = rest[nw + 8:2 * nw + 8]
        send, recv = rest[2 * nw + 8:]
        first = pl.program_id(0) == 0
        pair_start, pair_finish = _pair_stages(grads, kinds, taken, send, recv)
        pl.when(first)(pair_start)
        yv = y_ref[...]
        ry = _rms(yv)
        ny = yv * ry
        d = d_ref[...]
        _acc_rows(dgp_ref, first, d * ny)
        dy = _norm_bwd(d, ny, ry, gp_ref[...]).astype(MXU_DTYPE)
        dy_ref[...] = dy
        dm = _dot_nt(dy, wo_ref[...])
        gm = gm_ref[...]

        def group_bwd(val, dmv, gain, dg_ref):
            rg = lax.rsqrt(_group_ms(val * val, gm) + EPS)
            nv = val * rg
            _acc_rows(dg_ref, first, dmv * nv)
            t = dmv * gain
            return rg * (t - nv * _group_ms(t * nv, gm))

        ov = o_ref[...]
        d_o = group_bwd(ov, dm[:, 0:ATTN_W], ga_ref[...], dga_ref)
        do_ref[...] = d_o.astype(MXU_DTYPE)
        delta_ref[...] = _dot_exact(d_o * ov, sel_ref[...], 2).T[0:HEAD_ROWS, :]
        zv = z_ref[...]
        bv = b_ref[...]
        d_cv = group_bwd(bv * zv, dm[:, ATTN_W:D_MODEL], gc_ref[...], dgc_ref)
        db_ref[...] = d_cv * zv
        dz_ref[...] = d_cv * bv
        pl.when(pl.program_id(0) == nt - 1)(pair_finish)

    sd = jax.ShapeDtypeStruct
    taken_shape = [sd((N_CHIPS, g.shape[-2], g.shape[-1] if kd == "rows" else g.shape[-1] // N_CHIPS), F32)
                   for g, kd in zip(ready, kinds)]
    out = pl.pallas_call(
        body, name="mixer_bwd", grid=(nt,),
        in_specs=[_tok(tm, D_MODEL), _tok(tm, D_MODEL), _tok(tm, 512), _tok(tm, 512), _tok(tm, 512),
                  _whole((1, D_MODEL)), _whole((1, 512)), _whole((1, 512)),
                  _whole((D_MODEL, D_MODEL), single=True), _whole((512, 512)), _whole((512, 128))] + [ANY] * nw,
        out_specs=[_tok(tm, D_MODEL), _tok(tm, 512), _tok(tm, 512), _tok(tm, 512), _feat(HEAD_ROWS, tm),
                   _whole((1, D_MODEL)), _whole((1, 512)), _whole((1, 512))] + [ANY] * nw,
        out_shape=[sd((s, D_MODEL), MXU_DTYPE), sd((s, 512), MXU_DTYPE), sd((s, 512), F32), sd((s, 512), F32),
                   sd((HEAD_ROWS, s), F32), sd((1, D_MODEL), F32), sd((1, 512), F32), sd((1, 512), F32)] + taken_shape,
        scratch_shapes=[pltpu.SemaphoreType.DMA((nw, N_CHIPS)), pltpu.SemaphoreType.DMA((nw, N_CHIPS))],
        compiler_params=_params(("arbitrary",)),
    )(dx2, y, o_attn, gate_b, z, g_post, g_attn, g_conv, w_out, gmat, sel, *ready)
    return out[:8], out[8:]


def _attn_bwd(qs, k, k_t, v, do, c_rows, c_cols, lse, delta, t, parts):
    s = qs.shape[0]
    n = s // t
    pairs = [(i, j) for j in range(n) for i in range(j, n)]
    it = jnp.asarray(np.array([p[0] for p in pairs], np.int32))
    jt = jnp.asarray(np.array([p[1] for p in pairs], np.int32))

    nw = len(parts)

    def body(it_ref, jt_ref, q_ref, k_ref, kt_ref, v_ref, do_ref, cq_ref, ck_ref, lse_ref, dl_ref, *rest):
        pb = rest[:nw]
        dq_ref, dk_ref, dv_ref, dc_ref, dcq_ref = rest[nw:nw + 5]
        rcv = rest[nw + 5:2 * nw + 5]
        dk_sc, dv_sc, dc_sc, send, recv = rest[2 * nw + 5:]
        p = pl.program_id(0)
        i = it_ref[p]
        j = jt_ref[p]
        chip_start, chip_finish = _chip_stages(pb, rcv, send, recv)

        @pl.when(p == 0)
        def _():
            chip_start()
            dq_ref[...] = jnp.zeros_like(dq_ref)
            dcq_ref[...] = jnp.zeros_like(dcq_ref)

        @pl.when(i == j)
        def _():
            dk_sc[...] = jnp.zeros_like(dk_sc)
            dv_sc[...] = jnp.zeros_like(dv_sc)
            dc_sc[...] = jnp.zeros_like(dc_sc)

        def pair_step(pp, diagonal):
            lanes = _pair_lanes(pp)
            qp = q_ref[:, lanes]
            kp = k_ref[:, lanes]
            vp = v_ref[:, lanes]
            dop = do_ref[:, lanes]
            ck_all = ck_ref[...]
            lane = lax.broadcasted_iota(jnp.int32, (t, 128), 1)
            for hb in range(2):
                h = 2 * pp + hb
                row = pl.ds(h, 1)
                bias = (cq_ref[row, :] - lse_ref[row, :]) - _head_col(ck_all, h)
                pt = jnp.exp(_dot_nt(_only_head(kp, hb), qp) + bias)
                if diagonal:
                    kpos = lax.broadcasted_iota(jnp.int32, (t, t), 0)
                    qpos = lax.broadcasted_iota(jnp.int32, (t, t), 1)
                    pt = jnp.where(kpos <= qpos, pt, 0.0)
                dv_sc[:, lanes] += _dot(pt.astype(MXU_DTYPE), _only_head(dop, hb))
                dst = pt * (_dot_nt(_only_head(vp, hb), dop) - dl_ref[row, :])
                dc_sc[...] -= jnp.where(lane == h, jnp.sum(dst, axis=1, keepdims=True), 0.0)
                dcq_ref[i, row, :] += jnp.sum(dst, axis=0, keepdims=True)
                dsb = dst.astype(MXU_DTYPE)
                dk_sc[:, lanes] += _dot(dsb, _only_head(qp, hb))
                rows = _head_rows(h)
                dq_ref[i, rows, :] += _dot(kt_ref[rows, :], dsb)

        @pl.when(i > j)
        def _():
            for pp in range(N_HEADS // 2):
                pair_step(pp, False)

        @pl.when(i == j)
        def _():
            for pp in range(N_HEADS // 2):
                pair_step(pp, True)

        @pl.when(i == n - 1)
        def _():
            dk_ref[...] = dk_sc[...]
            dv_ref[...] = dv_sc[...]
            dc_ref[...] = dc_sc[...]

        pl.when(p == len(pairs) - 1)(chip_finish)

    qi = lambda p, it_, jt_: (it_[p], 0)
    kj = lambda p, it_, jt_: (jt_[p], 0)
    row_i = lambda p, it_, jt_: (0, it_[p])
    gs = pltpu.PrefetchScalarGridSpec(
        num_scalar_prefetch=2, grid=(len(pairs),),
        in_specs=[pl.BlockSpec((t, ATTN_W), qi), pl.BlockSpec((t, ATTN_W), kj),
                  pl.BlockSpec((ATTN_W, t), lambda p, it_, jt_: (0, jt_[p])),
                  pl.BlockSpec((t, ATTN_W), kj), pl.BlockSpec((t, ATTN_W), qi),
                  pl.BlockSpec((HEAD_ROWS, t), row_i), pl.BlockSpec((t, 128), kj),
                  pl.BlockSpec((HEAD_ROWS, t), row_i), pl.BlockSpec((HEAD_ROWS, t), row_i)] + [ANY] * nw,
        out_specs=[pl.BlockSpec((n, ATTN_W, t), lambda p, it_, jt_: (0, 0, 0)),
                   pl.BlockSpec((t, ATTN_W), kj), pl.BlockSpec((t, ATTN_W), kj),
                   pl.BlockSpec((t, 128), kj),
                   pl.BlockSpec((n, HEAD_ROWS, t), lambda p, it_, jt_: (0, 0, 0))] + [ANY] * nw,
        scratch_shapes=[pltpu.VMEM((t, ATTN_W), F32), pltpu.VMEM((t, ATTN_W), F32),
                        pltpu.VMEM((t, 128), F32), pltpu.SemaphoreType.DMA((nw, 3)), pltpu.SemaphoreType.DMA((nw, 3))])
    sd = jax.ShapeDtypeStruct
    out = pl.pallas_call(
        body, name="attn_bwd", grid_spec=gs,
        out_shape=[sd((n, ATTN_W, t), F32), sd((s, ATTN_W), F32), sd((s, ATTN_W), F32),
                   sd((s, 128), F32), sd((n, HEAD_ROWS, t), F32)] + [sd((3,) + a.shape[1:], a.dtype) for a in parts],
        compiler_params=_params(("arbitrary",)),
    )(it, jt, qs, k, k_t, v, do, c_rows, c_cols, lse, delta, *parts)
    return out[:5], out[5:]


def _forget_bwd(dc_rows, dc_cols, z_t, b_col):
    s = z_t.shape[1]
    nb = s // 128

    def body(dr_ref, dcc_ref, z_ref, b_ref, dz_ref, db_ref):
        lower = _tri(128, False)
        real = lax.broadcasted_iota(jnp.int32, (HEAD_ROWS, 128), 0) < N_HEADS

        tail = jnp.zeros((HEAD_ROWS, 1), F32)
        dbias = jnp.zeros((HEAD_ROWS, 1), F32)
        for m in range(nb):
            off = (nb - 1 - m) * 128
            dc = dr_ref[:, off:off + 128] + dcc_ref[off:off + 128, :].T[0:HEAD_ROWS, :]
            dlf = _dot_exact(dc, lower, 3) + tail
            dz = dlf * jax.nn.sigmoid(-(z_ref[0:HEAD_ROWS, off:off + 128] + b_ref[...]))
            dz = jnp.where(real, dz, 0.0)
            dz_ref[off:off + 128, :] = _rows_to_cols(dz)
            tail = tail + jnp.sum(dc, axis=1, keepdims=True)
            dbias = dbias + jnp.sum(dz, axis=1, keepdims=True)
        db_ref[...] = jnp.broadcast_to(dbias, db_ref.shape)

    return pl.pallas_call(
        body, name="forget_bwd",
        out_shape=[jax.ShapeDtypeStruct((s, 128), F32), jax.ShapeDtypeStruct((HEAD_ROWS, 128), F32)],
        compiler_params=_params())(dc_rows, dc_cols, z_t, b_col)


def _inproj_bwd(dz, gate_c, u, conv_w, dq, dk, dv, dzf, db, x, dx2, g_pre, w_t, tm):
    s = x.shape[0]
    nt = s // tm
    t = dq.shape[2]
    per = tm // t
    assert tm % t == 0 and dq.shape[:2] == (nt * per, ATTN_W)

    def body(dz_ref, dzn_ref, c_ref, u_ref, cp_ref, up_ref, cw_ref, dq_ref, dk_ref, dv_ref, dzf_ref, db_ref,
             x_ref, dx2_ref, g_ref, w_ref, gx_ref, dp_ref, dg_ref, dcw_ref):
        i = pl.program_id(0)
        first = i == 0
        last = i == nt - 1
        dzv = dz_ref[...]
        row = lax.broadcasted_iota(jnp.int32, dzv.shape, 0)
        n0 = jnp.where(last, 0.0, dzn_ref[0:1, :])
        n1 = jnp.where(last, 0.0, dzn_ref[1:2, :])
        dz1 = jnp.where(row == tm - 1, n0, pltpu.roll(dzv, tm - 1, 0))
        dz2 = jnp.where(row == tm - 1, n1, jnp.where(row == tm - 2, n0, pltpu.roll(dzv, tm - 2, 0)))
        dcu = cw_ref[2:3, :] * dzv + cw_ref[1:2, :] * dz1 + cw_ref[0:1, :] * dz2
        cv = c_ref[...]
        uv = u_ref[...]
        cu = cv * uv
        cu1, cu2 = _shift_down(cu, (cp_ref, up_ref), first)

        @pl.when(first)
        def _():
            dcw_ref[...] = jnp.zeros_like(dcw_ref)

        dcw_ref[0:1, :] += jnp.sum(dzv * cu2, axis=0, keepdims=True)
        dcw_ref[1:2, :] += jnp.sum(dzv * cu1, axis=0, keepdims=True)
        dcw_ref[2:3, :] += jnp.sum(dzv * cu, axis=0, keepdims=True)

        for a in range(per):
            dp_ref[a * t:(a + 1) * t, 0:512] = (dq_ref[a].T * Q_SCALE).astype(MXU_DTYPE)
        dp_ref[:, 512:1024] = dk_ref[...].astype(MXU_DTYPE)
        dp_ref[:, 1024:OFF_F] = dv_ref[...].astype(MXU_DTYPE)
        dp_ref[:, OFF_F:OFF_B] = dzf_ref[...].astype(MXU_DTYPE)
        dp_ref[:, OFF_B:OFF_C] = db_ref[...].astype(MXU_DTYPE)
        dp_ref[:, OFF_C:OFF_U] = (dcu * uv).astype(MXU_DTYPE)
        dp_ref[:, OFF_U:IN_PAD] = (dcu * cv).astype(MXU_DTYPE)
        dh = _dot(dp_ref[...], w_ref[...])
        xv = x_ref[...]
        r1 = _rms(xv)
        nx = xv * r1
        _acc_rows(dg_ref, first, dh * nx)
        gx_ref[...] = dx2_ref[...] + _norm_bwd(dh, nx, r1, g_ref[...])

    prev = pl.BlockSpec((8, 512), lambda i: (jnp.maximum(i * (tm // 8) - 1, 0), 0))
    nxt = pl.BlockSpec((8, 512), lambda i: (jnp.minimum((i + 1) * (tm // 8), s // 8 - 1), 0))
    sd = jax.ShapeDtypeStruct
    return pl.pallas_call(
        body, name="inproj_bwd", grid=(nt,),
        in_specs=[_tok(tm, 512), nxt, _tok(tm, 512), _tok(tm, 512), prev, prev, _whole((3, 512)),
                  pl.BlockSpec((per, ATTN_W, t), lambda i: (i, 0, 0)), _tok(tm, 512), _tok(tm, 512),
                  _tok(tm, 128),
                  _tok(tm, 512),
                  _tok(tm, D_MODEL), _tok(tm, D_MODEL), _whole((1, D_MODEL)), _whole((IN_PAD, D_MODEL), single=True)],
        out_specs=[_tok(tm, D_MODEL), _tok(tm, IN_PAD), _whole((1, D_MODEL)), _whole((8, 512))],
        out_shape=[sd((s, D_MODEL), F32), sd((s, IN_PAD), MXU_DTYPE), sd((1, D_MODEL), F32), sd((8, 512), F32)],
        compiler_params=_params(("arbitrary",)),
    )(dz, dz, gate_c, u, gate_c, u, conv_w, dq, dk, dv, dzf, db, x, dx2, g_pre, w_t)


def _tile(s, want):
    return want if s % want == 0 else s


def _halves(a):
    return a.reshape(2, a.shape[0] // 2, a.shape[1])


def _device_step(x, target, w, mom1, mom2, w_in_t, m_in_t, v_in_t, c_idx, me_idx):
    s = x.shape[0]
    tm = _tile(s, 512)
    tf = _tile(s, 256)
    ta = _tile(s, 512)
    tkk = _tile(s, 2048)
    gidx = np.arange(512) // HEAD_DIM
    gmat = jnp.asarray(gidx[:, None] == gidx[None, :], MXU_DTYPE)
    sel = jnp.asarray(gidx[:, None] == np.arange(128)[None, :], MXU_DTYPE)
    g_mix_pre, g_mix_post, g_ffn_pre, g_ffn_post = w["g_mix_pre"], w["g_mix_post"], w["g_ffn_pre"], w["g_ffn_post"]
    g_attn, g_conv, b_forget = w["g_attn_out"], w["g_conv_out"], w["b_forget"]
    shard = {n: _halves(w[n][0].astype(MXU_DTYPE)) for n in BIG[1:]}
    piece_rows = IN_W // N_CHIPS

    g_in, conv_all = _gather_weights([w_in_t.reshape(piece_rows, D_MODEL).astype(MXU_DTYPE)], w["conv_w"][0])
    w_rows = g_in.reshape(IN_W, D_MODEL)
    w_t = jnp.concatenate([w_rows[:OFF_F + N_HEADS], jnp.zeros((OFF_B - OFF_F - N_HEADS, D_MODEL), MXU_DTYPE),
                           w_rows[OFF_F + N_HEADS:]], axis=0)
    conv_w = jnp.transpose(conv_all, (1, 0, 2)).reshape(3, CONV_W)

    h1, qs, k, v, k_t, v_t, z_t, gate_b, gate_c, u = _inproj_fwd(x, g_mix_pre, w_t, tm)
    b_col = jnp.pad(jnp.transpose(b_forget), ((0, HEAD_ROWS - N_HEADS), (0, 0)))
    c_rows, c_cols, q_bias, k_bias = _forget_fwd(z_t, b_col)
    o_attn, lse, (g_out, g_gu, g_dn) = _attn_fwd(qs, k, v_t, q_bias, k_bias, ta,
                                                 [shard["w_out"], shard["w_gate_up"], shard["w_down"]])
    w_out = g_out.reshape(D_MODEL, D_MODEL)
    w_gu = g_gu.reshape(N_CHIPS, D_MODEL, FF_PIECE)
    w_dn = g_dn.reshape(2, FF_PIECE, D_MODEL)
    x2, merged, y, z = _mixer_fwd(x, o_attn, gate_b, gate_c, u, conv_w, g_attn, g_conv, w_out, g_mix_post, gmat, tm)
    h2, g, up, a, ff, dout, loss_acc = _ffn_fwd(x2, target, g_ffn_pre, w_gu, w_dn, g_ffn_post, tf)

    dx2, dff, dgu, dg_ffn_post, dg_ffn_pre = _ffn_bwd(dout, ff, x2, g, up, g_ffn_post, g_ffn_pre, w_gu, w_dn, tf)
    dw_dn = _tn_matmul(a, dff, FF_PIECE, 1024, tkk, "dw_down").reshape(N_CHIPS, 2, D_FF // (2 * N_CHIPS), D_MODEL)
    dw_gu = _tn_matmul(h2, dgu, 1024, FF_PIECE, tkk, "dw_gate_up").reshape(2, D_MODEL // 2, 2 * D_FF)
    (dy, d_o, d_b, dz, delta, dg_mix_post, dg_attn, dg_conv), (a_gu, a_dn) = _mixer_bwd(
        dx2, y, o_attn, gate_b, z, g_mix_post, g_attn, g_conv, w_out, gmat, sel, tm, [dw_gu, dw_dn], ["cols", "rows"])
    dw_out = _tn_matmul(merged, dy, 1024, 1024, tkk, "dw_out")
    sum_gu = _pair_sum(c_idx, dw_gu, "cols", a_gu, "pair_sum_w_gate_up")
    sum_dn = _pair_sum(c_idx, dw_dn, "rows", a_dn, "pair_sum_w_down")
    (dq_t, dk, dv, dc_cols, dcq), (r_gu, r_dn) = _attn_bwd(
        qs, k, k_t, v, d_o, c_rows, c_cols, lse, delta, ta, [sum_gu[1], sum_dn[1]])
    dc_rows = jnp.transpose(dcq, (1, 0, 2)).reshape(HEAD_ROWS, s)
    dzf, db_f = _forget_bwd(dc_rows, dc_cols, z_t, b_col)
    grad_x, dproj, dg_mix_pre, dcw = _inproj_bwd(dz, gate_c, u, conv_w, dq_t, dk, dv, dzf, d_b,
                                                 x, dx2, g_mix_pre, w_t, tm)
    dw_t = _tn_matmul(dproj, h1, 640, 1024, tkk, "dw_in")
    dw_in = jnp.concatenate([dw_t[:OFF_F + N_HEADS], dw_t[OFF_B:]], axis=0).reshape(N_CHIPS, piece_rows, D_MODEL)
    dw_out = dw_out.reshape(N_CHIPS, 2, D_MODEL // (2 * N_CHIPS), D_MODEL)

    (t_gu, t_dn), (a_in, a_out) = _chip_sums_hosting(me_idx, [(sum_gu[0], r_gu), (sum_dn[0], r_dn)],
                                                     [dw_in, dw_out], ["lanes", "rows"])
    sum_in = _pair_sum(c_idx, dw_in, "lanes", a_in, "pair_sum_w_in")
    sum_out = _pair_sum(c_idx, dw_out, "rows", a_out, "pair_sum_w_out")
    s_gu, s_dn = _pair_share([t_gu, t_dn], "pair_share_swiglu")
    small = dict(b_forget=db_f[:N_HEADS, 0], g_attn_out=dg_attn, g_conv_out=dg_conv, g_mix_pre=dg_mix_pre,
                 g_mix_post=dg_mix_post, g_ffn_pre=dg_ffn_pre, g_ffn_post=dg_ffn_post)
    (new_gu, new_dn), (r_in, r_out), small_all = _adamw_hosting(
        c_idx, [(w[n][0], mine, theirs, mom1[n][0], mom2[n][0])
                for n, mine, theirs in (("w_gate_up", t_gu, s_gu), ("w_down", t_dn, s_dn))],
        2, [sum_in[1], sum_out[1]], _pack_small(small, dcw[:3], loss_acc[0, 0]))
    t_in = _chip_sum(me_idx, sum_in[0], r_in, "chip_sum_w_in")
    t_out = _chip_sum(me_idx, sum_out[0], r_out, "chip_sum_w_out")
    s_in, s_out = _pair_share([t_in, t_out], "pair_share_mixer")
    new_in = _adamw_lanes(c_idx, w_in_t, t_in, s_in, m_in_t, v_in_t, "adamw_w_in")
    new_out = _adamw(c_idx, w["w_out"][0], t_out, s_out, mom1["w_out"][0], mom2["w_out"][0], 2, "adamw_w_out")
    return grad_x, dict(w_in=new_in, w_out=new_out, w_gate_up=new_gu, w_down=new_dn), small_all


BIG = ("w_in", "w_out", "w_gate_up", "w_down")
ANY = pl.BlockSpec(memory_space=pl.ANY)


def _place():
    x, y, c = lax.axis_index("x"), lax.axis_index("y"), lax.axis_index("c")
    others = [(1 - x, y), (x, 1 - y), (1 - x, 1 - y)]
    return x, y, c, 2 * x + y, others, [2 * px + py for px, py in others]


def _remote(src, dst, send, recv, dev):
    return pltpu.make_async_remote_copy(src_ref=src, dst_ref=dst, send_sem=send, recv_sem=recv,
                                        device_id=dev, device_id_type=MESH_ID)


def _gather_stages(sh, outs, send, recv):
    x, y, c, me, others, chips = _place()
    sib = (x, y, 1 - c)
    every = [(w, kk) for w in range(len(sh)) for kk in range(3)]

    def half_of(ref, half, piece=None):
        ref = ref if piece is None else ref.at[piece]
        if len(ref.shape) == 3:
            return ref.at[half]
        hc = ref.shape[1] // 2
        return ref.at[:, pl.ds(pl.multiple_of(half * hc, 128), hc)]

    def first(w, kk):
        return _remote(half_of(sh[w], c), half_of(outs[w], c, me), send.at[w, kk], recv.at[w, kk], (*others[kk], c))

    def landed(w, kk):
        r = half_of(outs[w], c, chips[kk])
        return _remote(r, r, send.at[w, kk], recv.at[w, kk], (*others[kk], c))

    def onward(w, kk, half):
        r = half_of(outs[w], half, chips[kk])
        return _remote(r, r, send.at[w, 3 + kk], recv.at[w, 3 + kk], sib)

    def start():
        for w, kk in every:
            first(w, kk).start()

    def forward():
        for w, kk in every:
            landed(w, kk).wait_recv()
            onward(w, kk, c).start()

    def finish():
        for w, kk in every:
            onward(w, kk, 1 - c).wait_recv()
        for w, kk in every:
            first(w, kk).wait_send()
            onward(w, kk, c).wait_send()

    return start, forward, finish


def _pair_piece(ref, kind, p, half):
    if kind == "rows":
        return ref.at[p, half]
    if kind == "lanes":
        hc = ref.shape[2] // 2
        return ref.at[p, :, pl.ds(pl.multiple_of(half * hc, 128), hc)]
    cols = ref.shape[2] // N_CHIPS
    return ref.at[half, :, pl.ds(p * cols, cols)]


def _pair_stages(g, kinds, a, send, recv):
    x, y, c, _, _, _ = _place()
    copies = [_remote(_pair_piece(g[w], kinds[w], p, 1 - c), a[w].at[p], send.at[w, p], recv.at[w, p], (x, y, 1 - c))
              for w in range(len(g)) for p in range(N_CHIPS)]

    def start():
        for cp in copies:
            cp.start()

    def finish():
        for cp in copies:
            cp.wait()

    return start, finish


def _chip_stages(pb, rcv, send, recv):
    x, y, c, _, others, chips = _place()
    copies = [_remote(pb[w].at[chips[kk]], rcv[w].at[kk], send.at[w, kk], recv.at[w, kk], (*others[kk], c))
              for w in range(len(pb)) for kk in range(3)]

    def start():
        for cp in copies:
            cp.start()

    def finish():
        for cp in copies:
            cp.wait()

    return start, finish


def _gather_weights(shards, conv_w):
    n = len(shards)

    def body(*refs):
        sh, cw, outs, cwo = refs[:n], refs[n], refs[n + 1:2 * n + 1], refs[2 * n + 1]
        send, recv = refs[2 * n + 2:]
        x, y, c, me, others, chips = _place()
        start, forward, finish = _gather_stages(sh, outs, send, recv)
        start()
        small = [_remote(cw, cwo.at[me], send.at[n, kk], recv.at[n, kk], (*others[kk], c)) for kk in range(3)]
        for cp in small:
            cp.start()
        forward()
        for kk in range(3):
            _remote(cw, cwo.at[chips[kk]], send.at[n, kk], recv.at[n, kk], (*others[kk], c)).wait_recv()
        finish()
        for cp in small:
            cp.wait_send()

    out_shape = [jax.ShapeDtypeStruct((N_CHIPS,) + s.shape, s.dtype) for s in shards]
    out_shape.append(jax.ShapeDtypeStruct((N_CHIPS,) + conv_w.shape, conv_w.dtype))
    got = pl.pallas_call(
        body, name="gather_weights", in_specs=[ANY] * (n + 1), out_specs=[ANY] * (n + 1), out_shape=out_shape,
        scratch_shapes=[pltpu.SemaphoreType.DMA((n + 1, 6)), pltpu.SemaphoreType.DMA((n + 1, 6))],
    )(*shards, conv_w)
    me = 2 * lax.axis_index("x") + lax.axis_index("y")
    return [lax.dynamic_update_index_in_dim(g, own, me, 0) for g, own in zip(got, list(shards) + [conv_w])]


def _taken_shape(g, kind):
    if kind == "rows":
        return (N_CHIPS,) + g.shape[2:]
    if kind == "lanes":
        return g.shape[:2] + (g.shape[2] // 2,)
    return (N_CHIPS, g.shape[1], g.shape[2] // N_CHIPS)


def _pair_sum(c_idx, g, kind, a, name):
    _, half, cols = a.shape
    if kind == "rows":
        mine = pl.BlockSpec((1, 1, half, cols), lambda p, cr: (p, cr[0], 0, 0))
    elif kind == "lanes":
        mine = pl.BlockSpec((1, half, cols), lambda p, cr: (p, 0, cr[0]))
    else:
        mine = pl.BlockSpec((1, half, cols), lambda p, cr: (cr[0], 0, p))

    def body(c_ref, g_ref, a_ref, pf_ref, pb_ref):
        tot = (g_ref[0, 0] if kind == "rows" else g_ref[0]) + a_ref[0]
        pf_ref[0] = tot
        pb_ref[0] = tot.astype(BF16)

    gs = pltpu.PrefetchScalarGridSpec(
        num_scalar_prefetch=1, grid=(N_CHIPS,),
        in_specs=[mine,
                  pl.BlockSpec((1, half, cols), lambda p, cr: (p, 0, 0))],
        out_specs=[pl.BlockSpec((1, half, cols), lambda p, cr: (p, 0, 0)),
                   pl.BlockSpec((1, half, cols), lambda p, cr: (p, 0, 0))])
    return pl.pallas_call(
        body, name=name, grid_spec=gs,
        out_shape=[jax.ShapeDtypeStruct((N_CHIPS, half, cols), F32), jax.ShapeDtypeStruct((N_CHIPS, half, cols), BF16)],
        compiler_params=_params(("arbitrary",)),
    )(c_idx, g, a)


def _chip_sum(me_idx, pf, rcv, name):
    _, half, cols = pf.shape

    def body(me_ref, pf_ref, r_ref, t_ref):
        t_ref[...] = ((pf_ref[0] + r_ref[0].astype(F32)) + r_ref[1].astype(F32)) + r_ref[2].astype(F32)

    gs = pltpu.PrefetchScalarGridSpec(
        num_scalar_prefetch=1, grid=(1,),
        in_specs=[pl.BlockSpec((1, half, cols), lambda i, mr: (mr[0], 0, 0)),
                  pl.BlockSpec((3, half, cols), lambda i, mr: (0, 0, 0))],
        out_specs=pl.BlockSpec((half, cols), lambda i, mr: (0, 0)))
    return pl.pallas_call(
        body, name=name, grid_spec=gs, out_shape=jax.ShapeDtypeStruct((half, cols), F32),
        compiler_params=_params(("arbitrary",)),
    )(me_idx, pf, rcv)


def _small_stages(sm, smg, send, recv, loc):
    x, y, c, _, _, _ = _place()

    def peer(r):
        return (1 - x if r & 4 else x, 1 - y if r & 2 else y, 1 - c if r & 1 else c)

    mine = 4 * x + 2 * y + c
    own = pltpu.make_async_copy(sm, smg.at[mine], loc)
    copies = [_remote(sm, smg.at[mine], send.at[r - 1], recv.at[r - 1], peer(r)) for r in range(1, 8)]

    def start():
        own.start()
        for cp in copies:
            cp.start()

    def finish():
        for r in range(1, 8):
            px, py, pc = peer(r)
            _remote(sm, smg.at[4 * px + 2 * py + pc], send.at[r - 1], recv.at[r - 1], (px, py, pc)).wait_recv()
        for cp in copies:
            cp.wait_send()
        own.wait()

    return start, finish


def _chip_sums_hosting(me_idx, items, grads, kinds):
    k = len(items)
    n = len(grads)

    def body(me_ref, *refs):
        ins, g = refs[:2 * k], refs[2 * k:2 * k + n]
        outs, taken = refs[2 * k + n:3 * k + n], refs[3 * k + n:3 * k + 2 * n]
        start, finish = _pair_stages(g, kinds, taken, *refs[3 * k + 2 * n:])
        start()
        for w in range(k):
            pf_ref, r_ref = ins[2 * w], ins[2 * w + 1]
            outs[w][...] = ((pf_ref[0] + r_ref[0].astype(F32)) + r_ref[1].astype(F32)) + r_ref[2].astype(F32)
        finish()

    in_specs, out_specs, out_shape = [], [], []
    for pf, rcv in items:
        _, half, cols = pf.shape
        in_specs += [pl.BlockSpec((1, half, cols), lambda i, mr: (mr[0], 0, 0)),
                     pl.BlockSpec((3, half, cols), lambda i, mr: (0, 0, 0))]
        out_specs.append(pl.BlockSpec((half, cols), lambda i, mr: (0, 0)))
        out_shape.append(jax.ShapeDtypeStruct((half, cols), F32))
    out_shape += [jax.ShapeDtypeStruct(_taken_shape(g, kd), g.dtype) for g, kd in zip(grads, kinds)]
    gs = pltpu.PrefetchScalarGridSpec(
        num_scalar_prefetch=1, grid=(1,), in_specs=in_specs + [ANY] * n, out_specs=out_specs + [ANY] * n,
        scratch_shapes=[pltpu.SemaphoreType.DMA((n, N_CHIPS)), pltpu.SemaphoreType.DMA((n, N_CHIPS))])
    out = pl.pallas_call(body, name="chip_sums_hosting", grid_spec=gs, out_shape=out_shape,
                         compiler_params=_params(("arbitrary",)))(me_idx, *[a for it in items for a in it], *grads)
    return out[:k], out[k:]


def _adamw_hosting(c_idx, items, nb, parts, small):
    k = len(items)
    n = len(parts)

    def body(c_ref, *refs):
        ins, pb, sm = refs[:5 * k], refs[5 * k:5 * k + n], refs[5 * k + n]
        outs = refs[5 * k + n + 1:9 * k + n + 1]
        rcv, smg = refs[9 * k + n + 1:9 * k + 2 * n + 1], refs[9 * k + 2 * n + 1]
        send, recv, ssend, srecv, loc = refs[9 * k + 2 * n + 2:]
        chip_start, chip_finish = _chip_stages(pb, rcv, send, recv)
        small_start, small_finish = _small_stages(sm, smg, ssend, srecv, loc)
        hh, i = pl.program_id(0), pl.program_id(1)

        @pl.when((hh == 0) & (i == 0))
        def _():
            chip_start()
            small_start()

        for w in range(k):
            w_ref, a_ref, b_ref, m_ref, v_ref = ins[5 * w:5 * w + 5]
            g_ref, d_ref, nm_ref, nv_ref = outs[4 * w:4 * w + 4]
            g = jnp.where(hh == c_ref[0], a_ref[...], b_ref[...])
            g_ref[...] = g
            d_ref[...], nm_ref[...], nv_ref[...] = _adamw_math(w_ref[...], g, m_ref[...], v_ref[...])

        @pl.when((hh == 1) & (i == nb - 1))
        def _():
            chip_finish()
            small_finish()

    in_specs, out_specs, out_shape = [], [], []
    for w, _, _, _, _ in items:
        rows, cols = w.shape
        tr = rows // (2 * nb)
        full = pl.BlockSpec((tr, cols), lambda hh, i, cr: (hh * nb + i, 0))
        half = pl.BlockSpec((tr, cols), lambda hh, i, cr: (i, 0))
        in_specs += [full, half, half, full, full]
        out_specs += [full] * 4
        out_shape += [jax.ShapeDtypeStruct((rows, cols), F32)] * 4
    out_shape += [jax.ShapeDtypeStruct((3,) + p.shape[1:], p.dtype) for p in parts]
    out_shape.append(jax.ShapeDtypeStruct((8,) + small.shape, small.dtype))
    gs = pltpu.PrefetchScalarGridSpec(
        num_scalar_prefetch=1, grid=(2, nb), in_specs=in_specs + [ANY] * (n + 1), out_specs=out_specs + [ANY] * (n + 1),
        scratch_shapes=[pltpu.SemaphoreType.DMA((n, 3)), pltpu.SemaphoreType.DMA((n, 3)),
                        pltpu.SemaphoreType.DMA((7,)), pltpu.SemaphoreType.DMA((7,)), pltpu.SemaphoreType.DMA(())])
    out = pl.pallas_call(body, name="adamw_hosting", grid_spec=gs, out_shape=out_shape,
                         compiler_params=_params(("arbitrary", "arbitrary")),
                         )(c_idx, *[a for it in items for a in it], *parts, small)
    return [out[4 * w:4 * w + 4] for w in range(k)], out[4 * k:4 * k + n], out[4 * k + n]


def _pair_share(totals, name):
    n = len(totals)

    def body(*refs):
        t, g = refs[:n], refs[n:2 * n]
        send, recv = refs[2 * n:]
        x, y, c, _, _, _ = _place()
        copies = [_remote(t[w], g[w], send.at[w], recv.at[w], (x, y, 1 - c)) for w in range(n)]
        for cp in copies:
            cp.start()
        for cp in copies:
            cp.wait()

    return pl.pallas_call(
        body, name=name, in_specs=[ANY] * n, out_specs=[ANY] * n,
        out_shape=[jax.ShapeDtypeStruct(t.shape, t.dtype) for t in totals],
        scratch_shapes=[pltpu.SemaphoreType.DMA((n,)), pltpu.SemaphoreType.DMA((n,))],
    )(*totals)


def _adamw_math(w, g, m, v):
    m = ADAM_B1 * m + (1.0 - ADAM_B1) * g
    v = ADAM_B2 * v + (1.0 - ADAM_B2) * (g * g)
    m_hat = m / (1.0 - ADAM_B1 ** ADAM_STEP)
    v_hat = v / (1.0 - ADAM_B2 ** ADAM_STEP)
    delta = -ADAM_LR * (m_hat / (jnp.sqrt(v_hat) + ADAM_EPS) + ADAM_WD * w)
    return delta, m, v


def _adamw(c_idx, w, mine, theirs, m, v, nb, name):
    rows, cols = w.shape
    tr = rows // (2 * nb)

    def body(c_ref, w_ref, a_ref, b_ref, m_ref, v_ref, g_ref, d_ref, nm_ref, nv_ref):
        g = jnp.where(pl.program_id(0) == c_ref[0], a_ref[...], b_ref[...])
        g_ref[...] = g
        d_ref[...], nm_ref[...], nv_ref[...] = _adamw_math(w_ref[...], g, m_ref[...], v_ref[...])

    full = pl.BlockSpec((tr, cols), lambda hh, i, cr: (hh * nb + i, 0))
    half = pl.BlockSpec((tr, cols), lambda hh, i, cr: (i, 0))
    gs = pltpu.PrefetchScalarGridSpec(num_scalar_prefetch=1, grid=(2, nb), in_specs=[full, half, half, full, full],
                                      out_specs=[full] * 4)
    return pl.pallas_call(
        body, name=name, grid_spec=gs, out_shape=[jax.ShapeDtypeStruct((rows, cols), F32)] * 4,
        compiler_params=_params(("arbitrary", "arbitrary")),
    )(c_idx, w, mine, theirs, m, v)


def _adamw_lanes(c_idx, w, mine, theirs, m, v, name):
    rows, _, cols = w.shape
    hc = cols // 2

    def body(c_ref, w_ref, a_ref, b_ref, m_ref, v_ref, g_ref, d_ref, nm_ref, nv_ref):
        g = jnp.where(pl.program_id(0) == c_ref[0], a_ref[...], b_ref[...])
        g_ref[:, 0, :] = g
        d_ref[:, 0, :], nm_ref[:, 0, :], nv_ref[:, 0, :] = _adamw_math(w_ref[:, 0, :], g, m_ref[:, 0, :], v_ref[:, 0, :])

    full = pl.BlockSpec((rows, 1, hc), lambda hh, cr: (0, 0, hh))
    half = pl.BlockSpec((rows, hc), lambda hh, cr: (0, 0))
    gs = pltpu.PrefetchScalarGridSpec(num_scalar_prefetch=1, grid=(2,), in_specs=[full, half, half, full, full],
                                      out_specs=[full] * 4)
    return pl.pallas_call(
        body, name=name, grid_spec=gs, out_shape=[jax.ShapeDtypeStruct((rows, 1, cols), F32)] * 4,
        compiler_params=_params(("arbitrary",)),
    )(c_idx, w, mine, theirs, m, v)


SMALL = ("g_mix_pre", "g_mix_post", "g_ffn_pre", "g_ffn_post")
SMALL_ALL = SMALL + ("g_attn_out", "g_conv_out", "conv_w", "b_forget")
SMALL_AT = {"g_mix_pre": (0, 0, 1024), "g_mix_post": (1, 0, 1024), "g_ffn_pre": (2, 0, 1024),
            "g_ffn_post": (3, 0, 1024), "g_attn_out": (4, 0, 512), "g_conv_out": (4, 512, 512),
            "b_forget": (7, 0, N_HEADS)}
CONV_AT = ((5, 0), (5, 512), (6, 0))
LOSS_AT = (6, 512)


def _pack_small(t, conv_full, loss_sum):
    conv = jnp.concatenate([conv_full.reshape(1, 3 * CONV_W), loss_sum.reshape(1, 1),
                            jnp.zeros((1, 2048 - 3 * CONV_W - 1), F32)], axis=1).reshape(2, 1024)
    return jnp.concatenate([t[n].reshape(1, 1024) for n in SMALL]
                           + [jnp.concatenate([t["g_attn_out"].reshape(1, 512), t["g_conv_out"].reshape(1, 512)], axis=1),
                              conv, jnp.pad(t["b_forget"].reshape(1, N_HEADS), ((0, 0), (0, 1024 - N_HEADS)))], axis=0)


def _small_update(me_idx, gathered, w, m, v):
    def body(me_ref, gg_ref, *refs):
        k = len(SMALL_ALL)
        w_refs, m_refs, v_refs = refs[:k], refs[k:2 * k], refs[2 * k:3 * k]
        loss_ref = refs[3 * k]
        outs = refs[3 * k + 1:3 * k + 1 + 4 * k]
        sums = refs[-1]
        g = gg_ref[0]
        for dev in range(1, 8):
            g = g + gg_ref[dev]
        sums[...] = g
        loss_ref[...] = sums[LOSS_AT[0]:LOSS_AT[0] + 1, LOSS_AT[1]:LOSS_AT[1] + 1]
        mine = pl.multiple_of(me_ref[0] * 128, 128)
        for idx, name in enumerate(SMALL_ALL):
            g_ref, d_ref, nm_ref, nv_ref = outs[4 * idx:4 * idx + 4]
            if name == "conv_w":
                for r, (row, lo) in enumerate(CONV_AT):
                    gr = sums[row:row + 1, pl.ds(lo + mine, 128)]
                    g_ref[0, r:r + 1, :] = gr
                    d_ref[0, r:r + 1, :], nm_ref[0, r:r + 1, :], nv_ref[0, r:r + 1, :] = _adamw_math(
                        w_refs[idx][0, r:r + 1, :], gr, m_refs[idx][0, r:r + 1, :], v_refs[idx][0, r:r + 1, :])
            else:
                row, lo, n = SMALL_AT[name]
                gr = sums[row:row + 1, lo:lo + n]
                g_ref[...] = gr
                d_ref[...], nm_ref[...], nv_ref[...] = _adamw_math(w_refs[idx][...], gr, m_refs[idx][...],
                                                                    v_refs[idx][...])

    def whole(a):
        nd = a.ndim
        return pl.BlockSpec(a.shape, lambda i, mr: (0,) * nd)

    ins = [t[n] for t in (w, m, v) for n in SMALL_ALL]
    out_shape = [jax.ShapeDtypeStruct((1, 1), F32)]
    for n in SMALL_ALL:
        out_shape += [jax.ShapeDtypeStruct(w[n].shape, F32)] * 4
    gs = pltpu.PrefetchScalarGridSpec(
        num_scalar_prefetch=1, grid=(1,), in_specs=[whole(gathered)] + [whole(a) for a in ins],
        out_specs=[whole(o) for o in out_shape], scratch_shapes=[pltpu.VMEM((8, 1024), F32)])
    out = pl.pallas_call(body, name="small_update", grid_spec=gs, out_shape=out_shape,
                         compiler_params=_params(("arbitrary",)))(me_idx, gathered, *ins)
    return out[0], {n: out[1 + 4 * i:5 + 4 * i] for i, n in enumerate(SMALL_ALL)}


def kernel(x, w_in, b_forget, conv_w, g_attn_out, g_conv_out, w_out, g_mix_pre, g_mix_post, w_gate_up, w_down, g_ffn_pre, g_ffn_post, loss_target, m_w_in, m_b_forget, m_conv_w, m_g_attn_out, m_g_conv_out, m_w_out, m_g_mix_pre, m_g_mix_post, m_w_gate_up, m_w_down, m_g_ffn_pre, m_g_ffn_post, v_w_in, v_b_forget, v_conv_w, v_g_attn_out, v_g_conv_out, v_w_out, v_g_mix_pre, v_g_mix_post, v_w_gate_up, v_w_down, v_g_ffn_pre, v_g_ffn_post):
    w = dict(w_in=w_in, b_forget=b_forget, conv_w=conv_w, g_attn_out=g_attn_out, g_conv_out=g_conv_out, w_out=w_out,
             g_mix_pre=g_mix_pre, g_mix_post=g_mix_post, w_gate_up=w_gate_up, w_down=w_down, g_ffn_pre=g_ffn_pre,
             g_ffn_post=g_ffn_post)
    m = dict(w_in=m_w_in, b_forget=m_b_forget, conv_w=m_conv_w, g_attn_out=m_g_attn_out, g_conv_out=m_g_conv_out,
             w_out=m_w_out, g_mix_pre=m_g_mix_pre, g_mix_post=m_g_mix_post, w_gate_up=m_w_gate_up, w_down=m_w_down,
             g_ffn_pre=m_g_ffn_pre, g_ffn_post=m_g_ffn_post)
    v = dict(w_in=v_w_in, b_forget=v_b_forget, conv_w=v_conv_w, g_attn_out=v_g_attn_out, g_conv_out=v_g_conv_out,
             w_out=v_w_out, g_mix_pre=v_g_mix_pre, g_mix_post=v_g_mix_post, w_gate_up=v_w_gate_up, w_down=v_w_down,
             g_ffn_pre=v_g_ffn_pre, g_ffn_post=v_g_ffn_post)
    cx, cy, cc = lax.axis_index("x"), lax.axis_index("y"), lax.axis_index("c")
    me = 2 * cx + cy
    c_idx = cc.astype(jnp.int32).reshape(1)
    me_idx = me.astype(jnp.int32).reshape(1)

    stored = lambda a: jnp.transpose(a, (2, 0, 1))
    grad_x, big, small_all = _device_step(x[0], loss_target[0], w, m, v, stored(w_in), stored(m_w_in),
                                          stored(v_w_in), c_idx, me_idx)
    gsum, delta, new_m, new_v = {}, {}, {}, {}
    for n in BIG:
        back = (lambda r: jnp.transpose(r, (1, 2, 0))) if n == "w_in" else (lambda r: r[None])
        gsum[n], delta[n], new_m[n], new_v[n] = [back(r) for r in big[n]]
    loss_sum, small_new = _small_update(me_idx, small_all, w, m, v)
    for n in SMALL_ALL:
        gsum[n], delta[n], new_m[n], new_v[n] = small_new[n]
    loss = 0.5 * loss_sum[0, 0]

    order = ("w_in", "b_forget", "conv_w", "g_attn_out", "g_conv_out", "w_out", "g_mix_pre", "g_mix_post",
             "w_gate_up", "w_down", "g_ffn_pre", "g_ffn_post")
    return (loss, grad_x[None], *[gsum[n] for n in order], *[delta[n] for n in order],
            *[new_m[n] for n in order], *[new_v[n] for n in order])
```

```python
import functools

import jax
import jax.numpy as jnp
import numpy as np
from jax import lax
from jax.experimental import pallas as pl
from jax.experimental.pallas import tpu as pltpu

F32 = jnp.float32
BF16 = jnp.bfloat16
MXU_DTYPE = jnp.bfloat16

D_MODEL = 1024
HEAD_DIM = 64
N_HEADS = 8
ATTN_W = 512
CONV_W = 512
D_FF = 2816
FF_PIECE = 1408
EPS = 1e-6
Q_SCALE = HEAD_DIM ** -0.5

OFF_F = 1536
OFF_B = 1664
OFF_C = 2176
OFF_U = 2688
IN_PAD = 3200
IN_W = 3080
N_CHIPS = 4

ADAM_LR = 0.001
ADAM_B1 = 0.9
ADAM_B2 = 0.999
ADAM_EPS = 1e-08
ADAM_WD = 0.01
ADAM_STEP = 10

VMEM_LIMIT_V7X = 56 * 1024 * 1024
MESH_ID = pl.DeviceIdType.MESH


def _params(sem=None, vmem=VMEM_LIMIT_V7X):
    kw = {"vmem_limit_bytes": vmem}
    if sem is not None:
        kw["dimension_semantics"] = sem
    return pltpu.CompilerParams(**kw)


def _dot(a, b):
    return jnp.dot(a, b, preferred_element_type=F32)


def _dot_nt(a, b):
    return lax.dot_general(a, b, (((1,), (1,)), ((), ())), preferred_element_type=F32)


def _dot_exact(x, ones, parts):
    if ones.dtype == F32:
        return _dot(x, ones)
    acc = None
    rem = x
    for _ in range(parts):
        piece = rem.astype(BF16)
        rem = rem - piece.astype(F32)
        term = _dot(piece, ones)
        acc = term if acc is None else acc + term
    return acc


def _rms(v):
    return lax.rsqrt(jnp.mean(v * v, axis=-1, keepdims=True) + EPS)


def _tok(tm, w):
    return pl.BlockSpec((tm, w), lambda i: (i, 0))


def _whole(shape, single=False):
    nd = len(shape)
    if single:
        return pl.BlockSpec(shape, lambda i: (0,) * nd, pipeline_mode=pl.Buffered(1))
    return pl.BlockSpec(shape, lambda i: (0,) * nd)


def _feat(rows, tm):
    return pl.BlockSpec((rows, tm), lambda i: (0, i))


def _inproj_fwd(x, g_pre, w_t, tm):
    s = x.shape[0]

    def body(x_ref, g_ref, w_ref, h_ref, q_ref, k_ref, v_ref, kt_ref, vt_ref, zt_ref, b_ref, c_ref, u_ref):
        xv = x_ref[...]
        h = ((xv * _rms(xv)) * g_ref[...]).astype(MXU_DTYPE)
        h_ref[...] = h

        def proj(lo, hi):
            return _dot_nt(h, w_ref[lo:hi, :])

        q_ref[...] = (proj(0, 512) * Q_SCALE).astype(MXU_DTYPE)
        kt = _dot_nt(w_ref[512:1024, :], h)
        vt = _dot_nt(w_ref[1024:OFF_F, :], h)
        kt_ref[...] = kt.astype(MXU_DTYPE)
        vt_ref[...] = vt.astype(MXU_DTYPE)
        k_ref[...] = kt.T.astype(MXU_DTYPE)
        v_ref[...] = vt.T.astype(MXU_DTYPE)
        zt_ref[...] = _dot_nt(w_ref[OFF_F:OFF_B, :], h)
        b_ref[...] = proj(OFF_B, OFF_C)
        c_ref[...] = proj(OFF_C, OFF_U)
        u_ref[...] = proj(OFF_U, IN_PAD)

    sd = jax.ShapeDtypeStruct
    return pl.pallas_call(
        body, name="inproj_fwd", grid=(s // tm,),
        in_specs=[_tok(tm, D_MODEL), _whole((1, D_MODEL)), _whole((IN_PAD, D_MODEL), single=True)],
        out_specs=[_tok(tm, D_MODEL), _tok(tm, 512), _tok(tm, 512), _tok(tm, 512), _feat(512, tm), _feat(512, tm),
                   _feat(128, tm), _tok(tm, 512), _tok(tm, 512), _tok(tm, 512)],
        out_shape=[sd((s, D_MODEL), MXU_DTYPE), sd((s, 512), MXU_DTYPE), sd((s, 512), MXU_DTYPE),
                   sd((s, 512), MXU_DTYPE), sd((512, s), MXU_DTYPE), sd((512, s), MXU_DTYPE), sd((128, s), F32),
                   sd((s, 512), F32), sd((s, 512), F32), sd((s, 512), F32)],
        compiler_params=_params(("arbitrary",)),
    )(x, g_pre, w_t)


def _tri(n, upper):
    r = lax.broadcasted_iota(jnp.int32, (n, n), 0)
    c = lax.broadcasted_iota(jnp.int32, (n, n), 1)
    return ((r <= c) if upper else (r >= c)).astype(MXU_DTYPE)


HEAD_ROWS = 16


def _rows_to_cols(v):
    return jnp.concatenate([v, jnp.zeros((128 - HEAD_ROWS, 128), F32)], axis=0).T


BIAS_PARTS = 3


def _bias_placement():
    place_q = np.zeros((BIAS_PARTS, 128, ATTN_W), np.float32)
    place_k = np.zeros((BIAS_PARTS, 128, ATTN_W), np.float32)
    ones_q = np.zeros((1, ATTN_W), np.float32)
    ones_k = np.zeros((1, ATTN_W), np.float32)
    for h in range(N_HEADS):
        base = 2 * HEAD_DIM * (h // 2) + HEAD_DIM * (1 - h % 2)
        for part in range(BIAS_PARTS):
            place_q[part, h, base + part] = 1.0
            place_k[part, h, base + BIAS_PARTS + part] = -1.0
        ones_q[0, base + BIAS_PARTS:base + 2 * BIAS_PARTS] = 1.0
        ones_k[0, base:base + BIAS_PARTS] = 1.0
    return (jnp.asarray(place_q, MXU_DTYPE), jnp.asarray(place_k, MXU_DTYPE), jnp.asarray(ones_q), jnp.asarray(ones_k))


def _forget_fwd(z_t, b_col):
    s = z_t.shape[1]
    nb = s // 128

    def body(z_ref, b_ref, pq_ref, pk_ref, oq_ref, ok_ref, c_ref, cc_ref, qa_ref, ka_ref):
        upper = _tri(128, True)

        carry = jnp.zeros((HEAD_ROWS, 1), F32)
        for n in range(nb):
            off = n * 128
            lf = jax.nn.log_sigmoid(z_ref[0:HEAD_ROWS, off:off + 128] + b_ref[...])
            cs = _dot_exact(lf, upper, 3) + carry
            c_ref[:, off:off + 128] = cs
            cols = _rows_to_cols(cs)
            cc_ref[off:off + 128, :] = cols
            qa = jnp.broadcast_to(oq_ref[...], (128, ATTN_W))
            ka = jnp.broadcast_to(ok_ref[...], (128, ATTN_W))
            rem = cols
            for part in range(BIAS_PARTS):
                piece = rem.astype(MXU_DTYPE)
                rem = rem - piece.astype(F32)
                qa = qa + _dot(piece, pq_ref[part])
                ka = ka + _dot(piece, pk_ref[part])
            qa_ref[off:off + 128, :] = qa.astype(MXU_DTYPE)
            ka_ref[off:off + 128, :] = ka.astype(MXU_DTYPE)
            carry = carry + jnp.sum(lf, axis=1, keepdims=True)

    sd = jax.ShapeDtypeStruct
    return pl.pallas_call(body, name="forget_fwd",
                          out_shape=[sd((HEAD_ROWS, s), F32), sd((s, 128), F32), sd((s, ATTN_W), MXU_DTYPE),
                                     sd((s, ATTN_W), MXU_DTYPE)],
                          compiler_params=_params())(z_t, b_col, *_bias_placement())


def _aligned(start, size):
    return pl.ds(start if isinstance(start, int) else pl.multiple_of(start, size), size)


def _pair_lanes(pp):
    return _aligned(pp * 2 * HEAD_DIM, 2 * HEAD_DIM)


def _head_rows(h):
    return _aligned(h * HEAD_DIM, HEAD_DIM)


def _only_head(block, hb):
    lane = lax.broadcasted_iota(jnp.int32, block.shape, 1)
    return jnp.where((lane >= HEAD_DIM) if hb else (lane < HEAD_DIM), block, jnp.zeros_like(block))


def _head_col(cols, h):
    lane = lax.broadcasted_iota(jnp.int32, cols.shape, 1)
    return jnp.sum(jnp.where(lane == h, cols, 0.0), axis=1, keepdims=True)


def _other_head(block, other, hb):
    lane = lax.broadcasted_iota(jnp.int32, block.shape, 1)
    return jnp.where((lane >= HEAD_DIM) if hb else (lane < HEAD_DIM), block, other)


def _attn_fwd(qs, k, v_t, q_bias, k_bias, t, shards):
    s = qs.shape[0]
    n = s // t
    pairs = [(i, j) for i in range(n) for j in range(i + 1)]
    it = jnp.asarray(np.array([p[0] for p in pairs], np.int32))
    jt = jnp.asarray(np.array([p[1] for p in pairs], np.int32))
    nw = len(shards)
    last = len(pairs) - 1
    mid = (2 * len(pairs)) // 3

    def body(it_ref, jt_ref, q_ref, k_ref, vt_ref, qb_ref, kb_ref, *rest):
        sh, (o_ref, lse_ref), got = rest[:nw], rest[nw:nw + 2], rest[nw + 2:2 * nw + 2]
        m_sc, l_sc, acc_sc, send, recv = rest[2 * nw + 2:]
        p = pl.program_id(0)
        i = it_ref[p]
        j = jt_ref[p]
        gather_start, gather_forward, gather_finish = _gather_stages(sh, got, send, recv)
        pl.when(p == 0)(gather_start)
        if mid < last:
            pl.when(p == mid)(gather_forward)

        @pl.when(j == 0)
        def _():
            m_sc[...] = jnp.full_like(m_sc, -1e30)
            l_sc[...] = jnp.ones_like(l_sc)
            acc_sc[...] = jnp.zeros_like(acc_sc)

        def pair_step(pp, diagonal):
            lanes = _pair_lanes(pp)
            kp = k_ref[:, lanes]
            qp = q_ref[:, lanes]
            kb = kb_ref[:, lanes]
            qb = qb_ref[:, lanes]
            for hb in range(2):
                h = 2 * pp + hb
                row = pl.ds(h, 1)
                rows = _head_rows(h)
                st = _dot_nt(_other_head(kp, kb, hb), _other_head(qp, qb, hb))
                if diagonal:
                    kpos = lax.broadcasted_iota(jnp.int32, (t, t), 0)
                    qpos = lax.broadcasted_iota(jnp.int32, (t, t), 1)
                    st = jnp.where(kpos <= qpos, st, -1e30)
                m_prev = m_sc[row, :]
                m_new = jnp.maximum(m_prev, jnp.max(st, axis=0, keepdims=True))
                alpha = jnp.exp(m_prev - m_new)
                pt = jnp.exp(st - m_new)
                l_sc[row, :] = alpha * l_sc[row, :] + jnp.sum(pt, axis=0, keepdims=True)
                acc_sc[rows, :] = acc_sc[rows, :] * alpha + _dot(vt_ref[rows, :], pt.astype(MXU_DTYPE))
                m_sc[row, :] = m_new

        @pl.when(j < i)
        def _():
            for pp in range(N_HEADS // 2):
                pair_step(pp, False)

        @pl.when(j == i)
        def _():
            for pp in range(N_HEADS // 2):
                pair_step(pp, True)
                sub = lax.broadcasted_iota(jnp.int32, (2 * HEAD_DIM, t), 0)
                l_pair = jnp.where(sub < HEAD_DIM, l_sc[pl.ds(2 * pp, 1), :], l_sc[pl.ds(2 * pp + 1, 1), :])
                o_t = acc_sc[_aligned(pp * 2 * HEAD_DIM, 2 * HEAD_DIM), :] / l_pair
                o_ref[:, _pair_lanes(pp)] = o_t.T

            lse_ref[...] = m_sc[...] + jnp.log(l_sc[...])

        @pl.when(p == last)
        def _():
            if mid >= last:
                gather_forward()
            gather_finish()

    gs = pltpu.PrefetchScalarGridSpec(
        num_scalar_prefetch=2, grid=(len(pairs),),
        in_specs=[pl.BlockSpec((t, ATTN_W), lambda p, it_, jt_: (it_[p], 0)),
                  pl.BlockSpec((t, ATTN_W), lambda p, it_, jt_: (jt_[p], 0)),
                  pl.BlockSpec((ATTN_W, t), lambda p, it_, jt_: (0, jt_[p])),
                  pl.BlockSpec((t, ATTN_W), lambda p, it_, jt_: (it_[p], 0)),
                  pl.BlockSpec((t, ATTN_W), lambda p, it_, jt_: (jt_[p], 0))] + [ANY] * nw,
        out_specs=[pl.BlockSpec((t, ATTN_W), lambda p, it_, jt_: (it_[p], 0)),
                   pl.BlockSpec((HEAD_ROWS, t), lambda p, it_, jt_: (0, it_[p]))] + [ANY] * nw,
        scratch_shapes=[pltpu.VMEM((HEAD_ROWS, t), F32), pltpu.VMEM((HEAD_ROWS, t), F32), pltpu.VMEM((ATTN_W, t), F32),
                        pltpu.SemaphoreType.DMA((nw, 6)), pltpu.SemaphoreType.DMA((nw, 6))])
    o, lse, *got = pl.pallas_call(
        body, name="attn_fwd", grid_spec=gs,
        out_shape=[jax.ShapeDtypeStruct((s, ATTN_W), F32), jax.ShapeDtypeStruct((HEAD_ROWS, s), F32)]
        + [jax.ShapeDtypeStruct((N_CHIPS,) + a.shape, a.dtype) for a in shards],
        compiler_params=_params(("arbitrary",)),
    )(it, jt, qs, k, v_t, q_bias, k_bias, *shards)
    me = 2 * lax.axis_index("x") + lax.axis_index("y")
    return o, lse, [lax.dynamic_update_index_in_dim(g, own, me, 0) for g, own in zip(got, shards)]


def _shift_down(cur, prev_ref, first):
    row = lax.broadcasted_iota(jnp.int32, cur.shape, 0)
    p7 = jnp.where(first, 0.0, prev_ref[0][7:8, :] * prev_ref[1][7:8, :])
    p6 = jnp.where(first, 0.0, prev_ref[0][6:7, :] * prev_ref[1][6:7, :])
    s1 = jnp.where(row == 0, p7, pltpu.roll(cur, 1, 0))
    s2 = jnp.where(row == 0, p6, jnp.where(row == 1, p7, pltpu.roll(cur, 2, 0)))
    return s1, s2


def _group_ms(v, gmat):
    return _dot_exact(v, gmat, 2) * (1.0 / HEAD_DIM)


def _mixer_fwd(x, o_attn, gate_b, gate_c, u, conv_w, g_attn, g_conv, w_out, g_post, gmat, tm):
    s = x.shape[0]

    def body(x_ref, o_ref, b_ref, c_ref, u_ref, cp_ref, up_ref, cw_ref, ga_ref, gc_ref, wo_ref, gp_ref, gm_ref,
             x2_ref, mg_ref, y_ref, z_ref):
        i = pl.program_id(0)
        cu = c_ref[...] * u_ref[...]
        cu1, cu2 = _shift_down(cu, (cp_ref, up_ref), i == 0)
        z = cw_ref[0:1, :] * cu2 + cw_ref[1:2, :] * cu1 + cw_ref[2:3, :] * cu
        z_ref[...] = z
        cv = b_ref[...] * z
        ov = o_ref[...]
        gm = gm_ref[...]
        ma = ((ov * lax.rsqrt(_group_ms(ov * ov, gm) + EPS)) * ga_ref[...]).astype(MXU_DTYPE)
        mc = ((cv * lax.rsqrt(_group_ms(cv * cv, gm) + EPS)) * gc_ref[...]).astype(MXU_DTYPE)
        mg_ref[:, 0:ATTN_W] = ma
        mg_ref[:, ATTN_W:D_MODEL] = mc
        y = _dot(ma, wo_ref[0:ATTN_W, :]) + _dot(mc, wo_ref[ATTN_W:D_MODEL, :])
        y_ref[...] = y
        x2_ref[...] = x_ref[...] + (y * _rms(y)) * gp_ref[...]

    halo = pl.BlockSpec((8, 512), lambda i: (jnp.maximum(i * (tm // 8) - 1, 0), 0))
    sd = jax.ShapeDtypeStruct
    return pl.pallas_call(
        body, name="mixer_fwd", grid=(s // tm,),
        in_specs=[_tok(tm, D_MODEL), _tok(tm, 512), _tok(tm, 512), _tok(tm, 512), _tok(tm, 512), halo, halo,
                  _whole((3, 512)), _whole((1, 512)), _whole((1, 512)), _whole((D_MODEL, D_MODEL), single=True),
                  _whole((1, D_MODEL)), _whole((512, 512))],
        out_specs=[_tok(tm, D_MODEL), _tok(tm, D_MODEL), _tok(tm, D_MODEL), _tok(tm, 512)],
        out_shape=[sd((s, D_MODEL), F32), sd((s, D_MODEL), MXU_DTYPE), sd((s, D_MODEL), F32), sd((s, 512), F32)],
        compiler_params=_params(("arbitrary",)),
    )(x, o_attn, gate_b, gate_c, u, gate_c, u, conv_w, g_attn, g_conv, w_out, g_post, gmat)


def _ffn_fwd(x2, target, g_pre, w_gu, w_dn, g_post, tm):
    s = x2.shape[0]

    def body(x_ref, t_ref, gpre_ref, wgu_ref, wdn_ref, gpost_ref,
             h_ref, g_ref, up_ref, a_ref, ff_ref, dout_ref, loss_ref):
        xv = x_ref[...]
        h = ((xv * _rms(xv)) * gpre_ref[...]).astype(MXU_DTYPE)
        h_ref[...] = h
        ff = jnp.zeros((tm, D_MODEL), F32)
        for j in range(2):
            cols = slice(j * FF_PIECE, (j + 1) * FF_PIECE)
            g = _dot(h, wgu_ref[j])
            up = _dot(h, wgu_ref[2 + j])
            a = ((g * jax.nn.sigmoid(g)) * up).astype(MXU_DTYPE)
            g_ref[:, cols] = g
            up_ref[:, cols] = up
            a_ref[:, cols] = a
            ff = ff + _dot(a, wdn_ref[j])
        ff_ref[...] = ff
        err = (xv + (ff * _rms(ff)) * gpost_ref[...]) - t_ref[...]
        dout_ref[...] = err * (1.0 / D_MODEL)
        part = jnp.sum(jnp.mean(err * err, axis=-1, keepdims=True), axis=0, keepdims=True)

        @pl.when(pl.program_id(0) == 0)
        def _():
            loss_ref[...] = jnp.zeros_like(loss_ref)

        loss_ref[...] += part

    sd = jax.ShapeDtypeStruct
    return pl.pallas_call(
        body, name="ffn_fwd", grid=(s // tm,),
        in_specs=[_tok(tm, D_MODEL), _tok(tm, D_MODEL), _whole((1, D_MODEL)),
                  _whole((4, D_MODEL, FF_PIECE), single=True), _whole((2, FF_PIECE, D_MODEL), single=True),
                  _whole((1, D_MODEL))],
        out_specs=[_tok(tm, D_MODEL), _tok(tm, D_FF), _tok(tm, D_FF), _tok(tm, D_FF), _tok(tm, D_MODEL),
                   _tok(tm, D_MODEL), _whole((8, 128))],
        out_shape=[sd((s, D_MODEL), MXU_DTYPE), sd((s, D_FF), F32), sd((s, D_FF), F32), sd((s, D_FF), MXU_DTYPE),
                   sd((s, D_MODEL), F32), sd((s, D_MODEL), F32), sd((8, 128), F32)],
        compiler_params=_params(("arbitrary",)),
    )(x2, target, g_pre, w_gu, w_dn, g_post)


def _norm_bwd(dy, normed, rinv, gain):
    t = dy * gain
    return rinv * (t - normed * jnp.mean(t * normed, axis=-1, keepdims=True))


def _acc_rows(ref, first, val):
    @pl.when(first)
    def _():
        ref[...] = jnp.zeros_like(ref)

    ref[...] += jnp.sum(val, axis=0, keepdims=True)


def _ffn_bwd(dout, ff, x2, g, up, g_post, g_pre, w_gu, w_dn, tm):
    s = x2.shape[0]

    def body(do_ref, ff_ref, x_ref, g_ref, up_ref, gpost_ref, gpre_ref, wgu_ref, wdn_ref,
             dx_ref, dff_ref, dgu_ref, dgpost_ref, dgpre_ref):
        first = pl.program_id(0) == 0
        ffv = ff_ref[...]
        rf = _rms(ffv)
        n = ffv * rf
        do = do_ref[...]
        _acc_rows(dgpost_ref, first, do * n)
        dff = _norm_bwd(do, n, rf, gpost_ref[...]).astype(MXU_DTYPE)
        dff_ref[...] = dff
        dh = jnp.zeros((tm, D_MODEL), F32)
        for j in range(2):
            cols = slice(j * FF_PIECE, (j + 1) * FF_PIECE)
            da = _dot_nt(dff, wdn_ref[j])
            gv = g_ref[:, cols]
            sg = jax.nn.sigmoid(gv)
            dg = (da * up_ref[:, cols] * (sg * (1.0 + gv * (1.0 - sg)))).astype(MXU_DTYPE)
            du = (da * (gv * sg)).astype(MXU_DTYPE)
            dgu_ref[:, cols] = dg
            dgu_ref[:, D_FF + j * FF_PIECE:D_FF + (j + 1) * FF_PIECE] = du
            dh = dh + _dot_nt(dg, wgu_ref[j]) + _dot_nt(du, wgu_ref[2 + j])
        xv = x_ref[...]
        r2 = _rms(xv)
        nx = xv * r2
        _acc_rows(dgpre_ref, first, dh * nx)
        dx_ref[...] = do + _norm_bwd(dh, nx, r2, gpre_ref[...])

    sd = jax.ShapeDtypeStruct
    return pl.pallas_call(
        body, name="ffn_bwd", grid=(s // tm,),
        in_specs=[_tok(tm, D_MODEL), _tok(tm, D_MODEL), _tok(tm, D_MODEL), _tok(tm, D_FF), _tok(tm, D_FF),
                  _whole((1, D_MODEL)), _whole((1, D_MODEL)),
                  _whole((4, D_MODEL, FF_PIECE), single=True), _whole((2, FF_PIECE, D_MODEL), single=True)],
        out_specs=[_tok(tm, D_MODEL), _tok(tm, D_MODEL), _tok(tm, 2 * D_FF), _whole((1, D_MODEL)),
                   _whole((1, D_MODEL))],
        out_shape=[sd((s, D_MODEL), F32), sd((s, D_MODEL), MXU_DTYPE), sd((s, 2 * D_FF), MXU_DTYPE),
                   sd((1, D_MODEL), F32), sd((1, D_MODEL), F32)],
        compiler_params=_params(("arbitrary",)),
    )(dout, ff, x2, g, up, g_post, g_pre, w_gu, w_dn)


def _tn_matmul(a, b, tm, tn, tk, name):
    s, m = a.shape
    n = b.shape[1]

    def body(a_ref, b_ref, o_ref):
        @pl.when(pl.program_id(2) == 0)
        def _():
            o_ref[...] = jnp.zeros_like(o_ref)

        o_ref[...] += lax.dot_general(a_ref[...], b_ref[...], (((0,), (0,)), ((), ())), preferred_element_type=F32)

    return pl.pallas_call(
        body, name=name, grid=(m // tm, n // tn, s // tk),
        in_specs=[pl.BlockSpec((tk, tm), lambda i, j, kk: (kk, i)), pl.BlockSpec((tk, tn), lambda i, j, kk: (kk, j))],
        out_specs=pl.BlockSpec((tm, tn), lambda i, j, kk: (i, j)),
        out_shape=jax.ShapeDtypeStruct((m, n), F32),
        compiler_params=_params(("arbitrary", "arbitrary", "arbitrary")),
    )(a, b)


def _mixer_bwd(dx2, y, o_attn, gate_b, z, g_post, g_attn, g_conv, w_out, gmat, sel, tm, ready, kinds):
    s = dx2.shape[0]
    nw = len(ready)
    nt = s // tm

    def body(d_ref, y_ref, o_ref, b_ref, z_ref, gp_ref, ga_ref, gc_ref, wo_ref, gm_ref, sel_ref, *rest):
        grads = rest[:nw]
        dy_ref, do_ref, db_ref, dz_ref, delta_ref, dgp_ref, dga_ref, dgc_ref = rest[nw:nw + 8]
        taken = rest[nw + 8:2 * nw + 8]
        send, recv = rest[2 * nw + 8:]
        first = pl.program_id(0) == 0
        pair_start, pair_finish = _pair_stages(grads, kinds, taken, send, recv)
        pl.when(first)(pair_start)
        yv = y_ref[...]
        ry = _rms(yv)
        ny = yv * ry
        d = d_ref[...]
        _acc_rows(dgp_ref, first, d * ny)
        dy = _norm_bwd(d, ny, ry, gp_ref[...]).astype(MXU_DTYPE)
        dy_ref[...] = dy
        dm = _dot_nt(dy, wo_ref[...])
        gm = gm_ref[...]

        def group_bwd(val, dmv, gain, dg_ref):
            rg = lax.rsqrt(_group_ms(val * val, gm) + EPS)
            nv = val * rg
            _acc_rows(dg_ref, first, dmv * nv)
            t = dmv * gain
            return rg * (t - nv * _group_ms(t * nv, gm))

        ov = o_ref[...]
        d_o = group_bwd(ov, dm[:, 0:ATTN_W], ga_ref[...], dga_ref)
        do_ref[...] = d_o.astype(MXU_DTYPE)
        delta_ref[...] = _dot_exact(d_o * ov, sel_ref[...], 2).T[0:HEAD_ROWS, :]
        zv = z_ref[...]
        bv = b_ref[...]
        d_cv = group_bwd(bv * zv, dm[:, ATTN_W:D_MODEL], gc_ref[...], dgc_ref)
        db_ref[...] = d_cv * zv
        dz_ref[...] = d_cv * bv
        pl.when(pl.program_id(0) == nt - 1)(pair_finish)

    sd = jax.ShapeDtypeStruct
    taken_shape = [sd((N_CHIPS, g.shape[-2], g.shape[-1] if kd == "rows" else g.shape[-1] // N_CHIPS), F32)
                   for g, kd in zip(ready, kinds)]
    out = pl.pallas_call(
        body, name="mixer_bwd", grid=(nt,),
        in_specs=[_tok(tm, D_MODEL), _tok(tm, D_MODEL), _tok(tm, 512), _tok(tm, 512), _tok(tm, 512),
                  _whole((1, D_MODEL)), _whole((1, 512)), _whole((1, 512)),
                  _whole((D_MODEL, D_MODEL), single=True), _whole((512, 512)), _whole((512, 128))] + [ANY] * nw,
        out_specs=[_tok(tm, D_MODEL), _tok(tm, 512), _tok(tm, 512), _tok(tm, 512), _feat(HEAD_ROWS, tm),
                   _whole((1, D_MODEL)), _whole((1, 512)), _whole((1, 512))] + [ANY] * nw,
        out_shape=[sd((s, D_MODEL), MXU_DTYPE), sd((s, 512), MXU_DTYPE), sd((s, 512), F32), sd((s, 512), F32),
                   sd((HEAD_ROWS, s), F32), sd((1, D_MODEL), F32), sd((1, 512), F32), sd((1, 512), F32)] + taken_shape,
        scratch_shapes=[pltpu.SemaphoreType.DMA((nw, N_CHIPS)), pltpu.SemaphoreType.DMA((nw, N_CHIPS))],
        compiler_params=_params(("arbitrary",)),
    )(dx2, y, o_attn, gate_b, z, g_post, g_attn, g_conv, w_out, gmat, sel, *ready)
    return out[:8], out[8:]


def _attn_bwd(qs, k, k_t, v, do, c_rows, c_cols, lse, delta, t, parts):
    s = qs.shape[0]
    n = s // t
    pairs = [(i, j) for j in range(n) for i in range(j, n)]
    it = jnp.asarray(np.array([p[0] for p in pairs], np.int32))
    jt = jnp.asarray(np.array([p[1] for p in pairs], np.int32))

    nw = len(parts)

    def body(it_ref, jt_ref, q_ref, k_ref, kt_ref, v_ref, do_ref, cq_ref, ck_ref, lse_ref, dl_ref, *rest):
        pb = rest[:nw]
        dq_ref, dk_ref, dv_ref, dc_ref, dcq_ref = rest[nw:nw + 5]
        rcv = rest[nw + 5:2 * nw + 5]
        dk_sc, dv_sc, dc_sc, send, recv = rest[2 * nw + 5:]
        p = pl.program_id(0)
        i = it_ref[p]
        j = jt_ref[p]
        chip_start, chip_finish = _chip_stages(pb, rcv, send, recv)

        @pl.when(p == 0)
        def _():
            chip_start()
            dq_ref[...] = jnp.zeros_like(dq_ref)
            dcq_ref[...] = jnp.zeros_like(dcq_ref)

        @pl.when(i == j)
        def _():
            dk_sc[...] = jnp.zeros_like(dk_sc)
            dv_sc[...] = jnp.zeros_like(dv_sc)
            dc_sc[...] = jnp.zeros_like(dc_sc)

        def pair_step(pp, diagonal):
            lanes = _pair_lanes(pp)
            qp = q_ref[:, lanes]
            kp = k_ref[:, lanes]
            vp = v_ref[:, lanes]
            dop = do_ref[:, lanes]
            ck_all = ck_ref[...]
            lane = lax.broadcasted_iota(jnp.int32, (t, 128), 1)
            for hb in range(2):
                h = 2 * pp + hb
                row = pl.ds(h, 1)
                bias = (cq_ref[row, :] - lse_ref[row, :]) - _head_col(ck_all, h)
                pt = jnp.exp(_dot_nt(_only_head(kp, hb), qp) + bias)
                if diagonal:
                    kpos = lax.broadcasted_iota(jnp.int32, (t, t), 0)
                    qpos = lax.broadcasted_iota(jnp.int32, (t, t), 1)
                    pt = jnp.where(kpos <= qpos, pt, 0.0)
                dv_sc[:, lanes] += _dot(pt.astype(MXU_DTYPE), _only_head(dop, hb))
                dst = pt * (_dot_nt(_only_head(vp, hb), dop) - dl_ref[row, :])
                dc_sc[...] -= jnp.where(lane == h, jnp.sum(dst, axis=1, keepdims=True), 0.0)
                dcq_ref[i, row, :] += jnp.sum(dst, axis=0, keepdims=True)
                dsb = dst.astype(MXU_DTYPE)
                dk_sc[:, lanes] += _dot(dsb, _only_head(qp, hb))
                rows = _head_rows(h)
                dq_ref[i, rows, :] += _dot(kt_ref[rows, :], dsb)

        @pl.when(i > j)
        def _():
            for pp in range(N_HEADS // 2):
                pair_step(pp, False)

        @pl.when(i == j)
        def _():
            for pp in range(N_HEADS // 2):
                pair_step(pp, True)

        @pl.when(i == n - 1)
        def _():
            dk_ref[...] = dk_sc[...]
            dv_ref[...] = dv_sc[...]
            dc_ref[...] = dc_sc[...]

        pl.when(p == len(pairs) - 1)(chip_finish)

    qi = lambda p, it_, jt_: (it_[p], 0)
    kj = lambda p, it_, jt_: (jt_[p], 0)
    row_i = lambda p, it_, jt_: (0, it_[p])
    gs = pltpu.PrefetchScalarGridSpec(
        num_scalar_prefetch=2, grid=(len(pairs),),
        in_specs=[pl.BlockSpec((t, ATTN_W), qi), pl.BlockSpec((t, ATTN_W), kj),
                  pl.BlockSpec((ATTN_W, t), lambda p, it_, jt_: (0, jt_[p])),
                  pl.BlockSpec((t, ATTN_W), kj), pl.BlockSpec((t, ATTN_W), qi),
                  pl.BlockSpec((HEAD_ROWS, t), row_i), pl.BlockSpec((t, 128), kj),
                  pl.BlockSpec((HEAD_ROWS, t), row_i), pl.BlockSpec((HEAD_ROWS, t), row_i)] + [ANY] * nw,
        out_specs=[pl.BlockSpec((n, ATTN_W, t), lambda p, it_, jt_: (0, 0, 0)),
                   pl.BlockSpec((t, ATTN_W), kj), pl.BlockSpec((t, ATTN_W), kj),
                   pl.BlockSpec((t, 128), kj),
                   pl.BlockSpec((n, HEAD_ROWS, t), lambda p, it_, jt_: (0, 0, 0))] + [ANY] * nw,
        scratch_shapes=[pltpu.VMEM((t, ATTN_W), F32), pltpu.VMEM((t, ATTN_W), F32),
                        pltpu.VMEM((t, 128), F32), pltpu.SemaphoreType.DMA((nw, 3)), pltpu.SemaphoreType.DMA((nw, 3))])
    sd = jax.ShapeDtypeStruct
    out = pl.pallas_call(
        body, name="attn_bwd", grid_spec=gs,
        out_shape=[sd((n, ATTN_W, t), F32), sd((s, ATTN_W), F32), sd((s, ATTN_W), F32),
                   sd((s, 128), F32), sd((n, HEAD_ROWS, t), F32)] + [sd((3,) + a.shape[1:], a.dtype) for a in parts],
        compiler_params=_params(("arbitrary",)),
    )(it, jt, qs, k, k_t, v, do, c_rows, c_cols, lse, delta, *parts)
    return out[:5], out[5:]


def _forget_bwd(dc_rows, dc_cols, z_t, b_col):
    s = z_t.shape[1]
    nb = s // 128

    def body(dr_ref, dcc_ref, z_ref, b_ref, dz_ref, db_ref):
        lower = _tri(128, False)
        real = lax.broadcasted_iota(jnp.int32, (HEAD_ROWS, 128), 0) < N_HEADS

        tail = jnp.zeros((HEAD_ROWS, 1), F32)
        dbias = jnp.zeros((HEAD_ROWS, 1), F32)
        for m in range(nb):
            off = (nb - 1 - m) * 128
            dc = dr_ref[:, off:off + 128] + dcc_ref[off:off + 128, :].T[0:HEAD_ROWS, :]
            dlf = _dot_exact(dc, lower, 3) + tail
            dz = dlf * jax.nn.sigmoid(-(z_ref[0:HEAD_ROWS, off:off + 128] + b_ref[...]))
            dz = jnp.where(real, dz, 0.0)
            dz_ref[off:off + 128, :] = _rows_to_cols(dz)
            tail = tail + jnp.sum(dc, axis=1, keepdims=True)
            dbias = dbias + jnp.sum(dz, axis=1, keepdims=True)
        db_ref[...] = jnp.broadcast_to(dbias, db_ref.shape)

    return pl.pallas_call(
        body, name="forget_bwd",
        out_shape=[jax.ShapeDtypeStruct((s, 128), F32), jax.ShapeDtypeStruct((HEAD_ROWS, 128), F32)],
        compiler_params=_params())(dc_rows, dc_cols, z_t, b_col)


def _inproj_bwd(dz, gate_c, u, conv_w, dq, dk, dv, dzf, db, x, dx2, g_pre, w_t, tm):
    s = x.shape[0]
    nt = s // tm
    t = dq.shape[2]
    assert t % tm == 0 and dq.shape[:2] == (s // t, ATTN_W)
    per = t // tm

    def body(dz_ref, dzn_ref, c_ref, u_ref, cp_ref, up_ref, cw_ref, dq_ref, dk_ref, dv_ref, dzf_ref, db_ref,
             x_ref, dx2_ref, g_ref, w_ref, gx_ref, dp_ref, dg_ref, dcw_ref):
        i = pl.program_id(0)
        first = i == 0
        last = i == nt - 1
        dzv = dz_ref[...]
        row = lax.broadcasted_iota(jnp.int32, dzv.shape, 0)
        n0 = jnp.where(last, 0.0, dzn_ref[0:1, :])
        n1 = jnp.where(last, 0.0, dzn_ref[1:2, :])
        dz1 = jnp.where(row == tm - 1, n0, pltpu.roll(dzv, tm - 1, 0))
        dz2 = jnp.where(row == tm - 1, n1, jnp.where(row == tm - 2, n0, pltpu.roll(dzv, tm - 2, 0)))
        dcu = cw_ref[2:3, :] * dzv + cw_ref[1:2, :] * dz1 + cw_ref[0:1, :] * dz2
        cv = c_ref[...]
        uv = u_ref[...]
        cu = cv * uv
        cu1, cu2 = _shift_down(cu, (cp_ref, up_ref), first)

        @pl.when(first)
        def _():
            dcw_ref[...] = jnp.zeros_like(dcw_ref)

        dcw_ref[0:1, :] += jnp.sum(dzv * cu2, axis=0, keepdims=True)
        dcw_ref[1:2, :] += jnp.sum(dzv * cu1, axis=0, keepdims=True)
        dcw_ref[2:3, :] += jnp.sum(dzv * cu, axis=0, keepdims=True)

        dp_ref[:, 0:512] = (dq_ref[0].T * Q_SCALE).astype(MXU_DTYPE)
        dp_ref[:, 512:1024] = dk_ref[...].astype(MXU_DTYPE)
        dp_ref[:, 1024:OFF_F] = dv_ref[...].astype(MXU_DTYPE)
        dp_ref[:, OFF_F:OFF_B] = dzf_ref[...].astype(MXU_DTYPE)
        dp_ref[:, OFF_B:OFF_C] = db_ref[...].astype(MXU_DTYPE)
        dp_ref[:, OFF_C:OFF_U] = (dcu * uv).astype(MXU_DTYPE)
        dp_ref[:, OFF_U:IN_PAD] = (dcu * cv).astype(MXU_DTYPE)
        dh = _dot(dp_ref[...], w_ref[...])
        xv = x_ref[...]
        r1 = _rms(xv)
        nx = xv * r1
        _acc_rows(dg_ref, first, dh * nx)
        gx_ref[...] = dx2_ref[...] + _norm_bwd(dh, nx, r1, g_ref[...])

    prev = pl.BlockSpec((8, 512), lambda i: (jnp.maximum(i * (tm // 8) - 1, 0), 0))
    nxt = pl.BlockSpec((8, 512), lambda i: (jnp.minimum((i + 1) * (tm // 8), s // 8 - 1), 0))
    sd = jax.ShapeDtypeStruct
    return pl.pallas_call(
        body, name="inproj_bwd", grid=(nt,),
        in_specs=[_tok(tm, 512), nxt, _tok(tm, 512), _tok(tm, 512), prev, prev, _whole((3, 512)),
                  pl.BlockSpec((1, ATTN_W, tm), lambda i: (i // per, 0, i % per)), _tok(tm, 512), _tok(tm, 512),
                  _tok(tm, 128),
                  _tok(tm, 512),
                  _tok(tm, D_MODEL), _tok(tm, D_MODEL), _whole((1, D_MODEL)), _whole((IN_PAD, D_MODEL), single=True)],
        out_specs=[_tok(tm, D_MODEL), _tok(tm, IN_PAD), _whole((1, D_MODEL)), _whole((8, 512))],
        out_shape=[sd((s, D_MODEL), F32), sd((s, IN_PAD), MXU_DTYPE), sd((1, D_MODEL), F32), sd((8, 512), F32)],
        compiler_params=_params(("arbitrary",)),
    )(dz, dz, gate_c, u, gate_c, u, conv_w, dq, dk, dv, dzf, db, x, dx2, g_pre, w_t)


def _tile(s, want):
    return want if s % want == 0 else s


def _halves(a):
    return a.reshape(2, a.shape[0] // 2, a.shape[1])


def _device_step(x, target, w, mom1, mom2, w_in_t, m_in_t, v_in_t, c_idx, me_idx):
    s = x.shape[0]
    tm = _tile(s, 512)
    tf = _tile(s, 256)
    ta = _tile(s, 1024)
    tb = _tile(s, 512)
    tkk = _tile(s, 2048)
    gidx = np.arange(512) // HEAD_DIM
    gmat = jnp.asarray(gidx[:, None] == gidx[None, :], MXU_DTYPE)
    sel = jnp.asarray(gidx[:, None] == np.arange(128)[None, :], MXU_DTYPE)
    g_mix_pre, g_mix_post, g_ffn_pre, g_ffn_post = w["g_mix_pre"], w["g_mix_post"], w["g_ffn_pre"], w["g_ffn_post"]
    g_attn, g_conv, b_forget = w["g_attn_out"], w["g_conv_out"], w["b_forget"]
    shard = {n: _halves(w[n][0].astype(MXU_DTYPE)) for n in BIG[1:]}
    piece_rows = IN_W // N_CHIPS

    g_in, conv_all = _gather_weights([w_in_t.reshape(piece_rows, D_MODEL).astype(MXU_DTYPE)], w["conv_w"][0])
    w_rows = g_in.reshape(IN_W, D_MODEL)
    w_t = jnp.concatenate([w_rows[:OFF_F + N_HEADS], jnp.zeros((OFF_B - OFF_F - N_HEADS, D_MODEL), MXU_DTYPE),
                           w_rows[OFF_F + N_HEADS:]], axis=0)
    conv_w = jnp.transpose(conv_all, (1, 0, 2)).reshape(3, CONV_W)

    h1, qs, k, v, k_t, v_t, z_t, gate_b, gate_c, u = _inproj_fwd(x, g_mix_pre, w_t, tm)
    b_col = jnp.pad(jnp.transpose(b_forget), ((0, HEAD_ROWS - N_HEADS), (0, 0)))
    c_rows, c_cols, q_bias, k_bias = _forget_fwd(z_t, b_col)
    o_attn, lse, (g_out, g_gu, g_dn) = _attn_fwd(qs, k, v_t, q_bias, k_bias, ta,
                                                 [shard["w_out"], shard["w_gate_up"], shard["w_down"]])
    w_out = g_out.reshape(D_MODEL, D_MODEL)
    w_gu = g_gu.reshape(N_CHIPS, D_MODEL, FF_PIECE)
    w_dn = g_dn.reshape(2, FF_PIECE, D_MODEL)
    x2, merged, y, z = _mixer_fwd(x, o_attn, gate_b, gate_c, u, conv_w, g_attn, g_conv, w_out, g_mix_post, gmat, tm)
    h2, g, up, a, ff, dout, loss_acc = _ffn_fwd(x2, target, g_ffn_pre, w_gu, w_dn, g_ffn_post, tf)

    dx2, dff, dgu, dg_ffn_post, dg_ffn_pre = _ffn_bwd(dout, ff, x2, g, up, g_ffn_post, g_ffn_pre, w_gu, w_dn, tf)
    dw_dn = _tn_matmul(a, dff, FF_PIECE, 1024, tkk, "dw_down").reshape(N_CHIPS, 2, D_FF // (2 * N_CHIPS), D_MODEL)
    dw_gu = _tn_matmul(h2, dgu, 1024, FF_PIECE, tkk, "dw_gate_up").reshape(2, D_MODEL // 2, 2 * D_FF)
    (dy, d_o, d_b, dz, delta, dg_mix_post, dg_attn, dg_conv), (a_gu, a_dn) = _mixer_bwd(
        dx2, y, o_attn, gate_b, z, g_mix_post, g_attn, g_conv, w_out, gmat, sel, tm, [dw_gu, dw_dn], ["cols", "rows"])
    dw_out = _tn_matmul(merged, dy, 1024, 1024, tkk, "dw_out")
    place = jnp.concatenate([c_idx, me_idx])
    sum_gu = _pair_sum(place, dw_gu, "cols", a_gu, "pair_sum_w_gate_up")
    sum_dn = _pair_sum(place, dw_dn, "rows", a_dn, "pair_sum_w_down")
    (dq_t, dk, dv, dc_cols, dcq), (r_gu, r_dn) = _attn_bwd(
        qs, k, k_t, v, d_o, c_rows, c_cols, lse, delta, tb, [sum_gu[1], sum_dn[1]])
    dc_rows = jnp.transpose(dcq, (1, 0, 2)).reshape(HEAD_ROWS, s)
    dzf, db_f = _forget_bwd(dc_rows, dc_cols, z_t, b_col)
    grad_x, dproj, dg_mix_pre, dcw = _inproj_bwd(dz, gate_c, u, conv_w, dq_t, dk, dv, dzf, d_b,
                                                 x, dx2, g_mix_pre, w_t, tm)
    dw_t = _tn_matmul(dproj, h1, 640, 1024, tkk, "dw_in")
    dw_in = jnp.concatenate([dw_t[:OFF_F + N_HEADS], dw_t[OFF_B:]], axis=0).reshape(N_CHIPS, piece_rows, D_MODEL)
    dw_out = dw_out.reshape(N_CHIPS, 2, D_MODEL // (2 * N_CHIPS), D_MODEL)

    a_in, a_out = _pair_exchange([dw_in, dw_out], ["lanes", "rows"])
    sum_in = _pair_sum(place, dw_in, "lanes", a_in, "pair_sum_w_in")
    sum_out = _pair_sum(place, dw_out, "rows", a_out, "pair_sum_w_out")
    small = dict(b_forget=db_f[:N_HEADS, 0], g_attn_out=dg_attn, g_conv_out=dg_conv, g_mix_pre=dg_mix_pre,
                 g_mix_post=dg_mix_post, g_ffn_pre=dg_ffn_pre, g_ffn_post=dg_ffn_post)
    r_in, r_out, small_all = _chip_exchange([sum_in[1], sum_out[1]], _pack_small(small, dcw[:3], loss_acc[0, 0]))
    totals = [_chip_sum(sb[0], r, "chip_sum_" + n)
              for n, sb, r in zip(BIG, (sum_in, sum_out, sum_gu, sum_dn), (r_in, r_out, r_gu, r_dn))]
    shared = _pair_share(totals, "pair_share")
    new = {"w_in": _adamw_lanes(c_idx, w_in_t, totals[0], shared[0], m_in_t, v_in_t, "adamw_w_in")}
    for n, mine, theirs in list(zip(BIG, totals, shared))[1:]:
        new[n] = _adamw(c_idx, w[n][0], mine, theirs, mom1[n][0], mom2[n][0], 2, "adamw_" + n)
    return grad_x, new, small_all


BIG = ("w_in", "w_out", "w_gate_up", "w_down")
ANY = pl.BlockSpec(memory_space=pl.ANY)


def _place():
    x, y, c = lax.axis_index("x"), lax.axis_index("y"), lax.axis_index("c")
    others = [(1 - x, y), (x, 1 - y), (1 - x, 1 - y)]
    return x, y, c, 2 * x + y, others, [2 * px + py for px, py in others]


def _remote(src, dst, send, recv, dev):
    return pltpu.make_async_remote_copy(src_ref=src, dst_ref=dst, send_sem=send, recv_sem=recv,
                                        device_id=dev, device_id_type=MESH_ID)


def _gather_stages(sh, outs, send, recv):
    x, y, c, me, others, chips = _place()
    sib = (x, y, 1 - c)
    every = [(w, kk) for w in range(len(sh)) for kk in range(3)]

    def half_of(ref, half, piece=None):
        ref = ref if piece is None else ref.at[piece]
        if len(ref.shape) == 3:
            return ref.at[half]
        hc = ref.shape[1] // 2
        return ref.at[:, pl.ds(pl.multiple_of(half * hc, 128), hc)]

    def first(w, kk):
        return _remote(half_of(sh[w], c), half_of(outs[w], c, me), send.at[w, kk], recv.at[w, kk], (*others[kk], c))

    def landed(w, kk):
        r = half_of(outs[w], c, chips[kk])
        return _remote(r, r, send.at[w, kk], recv.at[w, kk], (*others[kk], c))

    def onward(w, kk, half):
        r = half_of(outs[w], half, chips[kk])
        return _remote(r, r, send.at[w, 3 + kk], recv.at[w, 3 + kk], sib)

    def start():
        for w, kk in every:
            first(w, kk).start()

    def forward():
        for w, kk in every:
            landed(w, kk).wait_recv()
            onward(w, kk, c).start()

    def finish():
        for w, kk in every:
            onward(w, kk, 1 - c).wait_recv()
        for w, kk in every:
            first(w, kk).wait_send()
            onward(w, kk, c).wait_send()

    return start, forward, finish


def _pair_piece(ref, kind, p, half):
    if kind == "rows":
        return ref.at[p, half]
    if kind == "lanes":
        hc = ref.shape[2] // 2
        return ref.at[p, :, pl.ds(pl.multiple_of(half * hc, 128), hc)]
    cols = ref.shape[2] // N_CHIPS
    return ref.at[half, :, pl.ds(p * cols, cols)]


def _pair_stages(g, kinds, a, send, recv):
    x, y, c, _, _, _ = _place()
    copies = [_remote(_pair_piece(g[w], kinds[w], p, 1 - c), a[w].at[p], send.at[w, p], recv.at[w, p], (x, y, 1 - c))
              for w in range(len(g)) for p in range(N_CHIPS)]

    def start():
        for cp in copies:
            cp.start()

    def finish():
        for cp in copies:
            cp.wait()

    return start, finish


def _chip_stages(pb, rcv, send, recv):
    x, y, c, _, others, chips = _place()
    copies = [_remote(pb[w].at[chips[kk]], rcv[w].at[kk], send.at[w, kk], recv.at[w, kk], (*others[kk], c))
              for w in range(len(pb)) for kk in range(3)]

    def start():
        for cp in copies:
            cp.start()

    def finish():
        for cp in copies:
            cp.wait()

    return start, finish


def _gather_weights(shards, conv_w):
    n = len(shards)

    def body(*refs):
        sh, cw, outs, cwo = refs[:n], refs[n], refs[n + 1:2 * n + 1], refs[2 * n + 1]
        send, recv = refs[2 * n + 2:]
        x, y, c, me, others, chips = _place()
        start, forward, finish = _gather_stages(sh, outs, send, recv)
        start()
        small = [_remote(cw, cwo.at[me], send.at[n, kk], recv.at[n, kk], (*others[kk], c)) for kk in range(3)]
        for cp in small:
            cp.start()
        forward()
        for kk in range(3):
            _remote(cw, cwo.at[chips[kk]], send.at[n, kk], recv.at[n, kk], (*others[kk], c)).wait_recv()
        finish()
        for cp in small:
            cp.wait_send()

    out_shape = [jax.ShapeDtypeStruct((N_CHIPS,) + s.shape, s.dtype) for s in shards]
    out_shape.append(jax.ShapeDtypeStruct((N_CHIPS,) + conv_w.shape, conv_w.dtype))
    got = pl.pallas_call(
        body, name="gather_weights", in_specs=[ANY] * (n + 1), out_specs=[ANY] * (n + 1), out_shape=out_shape,
        scratch_shapes=[pltpu.SemaphoreType.DMA((n + 1, 6)), pltpu.SemaphoreType.DMA((n + 1, 6))],
    )(*shards, conv_w)
    me = 2 * lax.axis_index("x") + lax.axis_index("y")
    return [lax.dynamic_update_index_in_dim(g, own, me, 0) for g, own in zip(got, list(shards) + [conv_w])]


def _taken_shape(g, kind):
    if kind == "rows":
        return (N_CHIPS,) + g.shape[2:]
    if kind == "lanes":
        return g.shape[:2] + (g.shape[2] // 2,)
    return (N_CHIPS, g.shape[1], g.shape[2] // N_CHIPS)


def _pair_sum(place, g, kind, a, name):
    _, half, cols = a.shape
    if kind == "rows":
        mine = pl.BlockSpec((1, 1, half, cols), lambda p, pr: (p, pr[0], 0, 0))
    elif kind == "lanes":
        mine = pl.BlockSpec((1, half, cols), lambda p, pr: (p, 0, pr[0]))
    else:
        mine = pl.BlockSpec((1, half, cols), lambda p, pr: (pr[0], 0, p))

    def body(place_ref, g_ref, a_ref, own_ref, pb_ref):
        tot = (g_ref[0, 0] if kind == "rows" else g_ref[0]) + a_ref[0]
        pb_ref[0] = tot.astype(BF16)

        @pl.when(pl.program_id(0) == place_ref[1])
        def _():
            own_ref[...] = tot

    gs = pltpu.PrefetchScalarGridSpec(
        num_scalar_prefetch=1, grid=(N_CHIPS,),
        in_specs=[mine,
                  pl.BlockSpec((1, half, cols), lambda p, pr: (p, 0, 0))],
        out_specs=[pl.BlockSpec((half, cols), lambda p, pr: (0, 0)),
                   pl.BlockSpec((1, half, cols), lambda p, pr: (p, 0, 0))])
    return pl.pallas_call(
        body, name=name, grid_spec=gs,
        out_shape=[jax.ShapeDtypeStruct((half, cols), F32), jax.ShapeDtypeStruct((N_CHIPS, half, cols), BF16)],
        compiler_params=_params(("arbitrary",)),
    )(place, g, a)


def _chip_sum(own, rcv, name):
    half, cols = own.shape

    def body(o_ref, r_ref, t_ref):
        t_ref[...] = ((o_ref[...] + r_ref[0].astype(F32)) + r_ref[1].astype(F32)) + r_ref[2].astype(F32)

    return pl.pallas_call(
        body, name=name, grid=(1,),
        in_specs=[pl.BlockSpec((half, cols), lambda i: (0, 0)), pl.BlockSpec((3, half, cols), lambda i: (0, 0, 0))],
        out_specs=pl.BlockSpec((half, cols), lambda i: (0, 0)),
        out_shape=jax.ShapeDtypeStruct((half, cols), F32), compiler_params=_params(("arbitrary",)),
    )(own, rcv)


def _small_stages(sm, smg, send, recv, loc):
    x, y, c, _, _, _ = _place()

    def peer(r):
        return (1 - x if r & 4 else x, 1 - y if r & 2 else y, 1 - c if r & 1 else c)

    mine = 4 * x + 2 * y + c
    own = pltpu.make_async_copy(sm, smg.at[mine], loc)
    copies = [_remote(sm, smg.at[mine], send.at[r - 1], recv.at[r - 1], peer(r)) for r in range(1, 8)]

    def start():
        own.start()
        for cp in copies:
            cp.start()

    def finish():
        for r in range(1, 8):
            px, py, pc = peer(r)
            _remote(sm, smg.at[4 * px + 2 * py + pc], send.at[r - 1], recv.at[r - 1], (px, py, pc)).wait_recv()
        for cp in copies:
            cp.wait_send()
        own.wait()

    return start, finish


def _pair_exchange(grads, kinds):
    n = len(grads)

    def body(*refs):
        start, finish = _pair_stages(refs[:n], kinds, refs[n:2 * n], *refs[2 * n:])
        start()
        finish()

    return pl.pallas_call(
        body, name="pair_exchange", in_specs=[ANY] * n, out_specs=[ANY] * n,
        out_shape=[jax.ShapeDtypeStruct(_taken_shape(g, kd), g.dtype) for g, kd in zip(grads, kinds)],
        scratch_shapes=[pltpu.SemaphoreType.DMA((n, N_CHIPS)), pltpu.SemaphoreType.DMA((n, N_CHIPS))],
    )(*grads)


def _chip_exchange(parts, small):
    n = len(parts)

    def body(*refs):
        pb, sm, rcv, smg = refs[:n], refs[n], refs[n + 1:2 * n + 1], refs[2 * n + 1]
        send, recv, ssend, srecv, loc = refs[2 * n + 2:]
        chip_start, chip_finish = _chip_stages(pb, rcv, send, recv)
        small_start, small_finish = _small_stages(sm, smg, ssend, srecv, loc)
        chip_start()
        small_start()
        chip_finish()
        small_finish()

    out_shape = [jax.ShapeDtypeStruct((3,) + p.shape[1:], p.dtype) for p in parts]
    out_shape.append(jax.ShapeDtypeStruct((8,) + small.shape, small.dtype))
    return pl.pallas_call(
        body, name="chip_exchange", in_specs=[ANY] * (n + 1), out_specs=[ANY] * (n + 1), out_shape=out_shape,
        scratch_shapes=[pltpu.SemaphoreType.DMA((n, 3)), pltpu.SemaphoreType.DMA((n, 3)),
                        pltpu.SemaphoreType.DMA((7,)), pltpu.SemaphoreType.DMA((7,)), pltpu.SemaphoreType.DMA(())],
    )(*parts, small)


def _pair_share(totals, name):
    n = len(totals)

    def body(*refs):
        t, g = refs[:n], refs[n:2 * n]
        send, recv = refs[2 * n:]
        x, y, c, _, _, _ = _place()
        copies = [_remote(t[w], g[w], send.at[w], recv.at[w], (x, y, 1 - c)) for w in range(n)]
        for cp in copies:
            cp.start()
        for cp in copies:
            cp.wait()

    return pl.pallas_call(
        body, name=name, in_specs=[ANY] * n, out_specs=[ANY] * n,
        out_shape=[jax.ShapeDtypeStruct(t.shape, t.dtype) for t in totals],
        scratch_shapes=[pltpu.SemaphoreType.DMA((n,)), pltpu.SemaphoreType.DMA((n,))],
    )(*totals)


def _adamw_math(w, g, m, v):
    m = ADAM_B1 * m + (1.0 - ADAM_B1) * g
    v = ADAM_B2 * v + (1.0 - ADAM_B2) * (g * g)
    m_hat = m / (1.0 - ADAM_B1 ** ADAM_STEP)
    v_hat = v / (1.0 - ADAM_B2 ** ADAM_STEP)
    delta = -ADAM_LR * (m_hat / (jnp.sqrt(v_hat) + ADAM_EPS) + ADAM_WD * w)
    return delta, m, v


def _adamw(c_idx, w, mine, theirs, m, v, nb, name):
    rows, cols = w.shape
    tr = rows // (2 * nb)

    def body(c_ref, w_ref, a_ref, b_ref, m_ref, v_ref, g_ref, d_ref, nm_ref, nv_ref):
        g = jnp.where(pl.program_id(0) == c_ref[0], a_ref[...], b_ref[...])
        g_ref[...] = g
        d_ref[...], nm_ref[...], nv_ref[...] = _adamw_math(w_ref[...], g, m_ref[...], v_ref[...])

    full = pl.BlockSpec((tr, cols), lambda hh, i, cr: (hh * nb + i, 0))
    half = pl.BlockSpec((tr, cols), lambda hh, i, cr: (i, 0))
    gs = pltpu.PrefetchScalarGridSpec(num_scalar_prefetch=1, grid=(2, nb), in_specs=[full, half, half, full, full],
                                      out_specs=[full] * 4)
    return pl.pallas_call(
        body, name=name, grid_spec=gs, out_shape=[jax.ShapeDtypeStruct((rows, cols), F32)] * 4,
        compiler_params=_params(("arbitrary", "arbitrary")),
    )(c_idx, w, mine, theirs, m, v)


def _adamw_lanes(c_idx, w, mine, theirs, m, v, name):
    rows, _, cols = w.shape
    hc = cols // 2

    def body(c_ref, w_ref, a_ref, b_ref, m_ref, v_ref, g_ref, d_ref, nm_ref, nv_ref):
        g = jnp.where(pl.program_id(0) == c_ref[0], a_ref[...], b_ref[...])
        g_ref[:, 0, :] = g
        d_ref[:, 0, :], nm_ref[:, 0, :], nv_ref[:, 0, :] = _adamw_math(w_ref[:, 0, :], g, m_ref[:, 0, :], v_ref[:, 0, :])

    full = pl.BlockSpec((rows, 1, hc), lambda hh, cr: (0, 0, hh))
    half = pl.BlockSpec((rows, hc), lambda hh, cr: (0, 0))
    gs = pltpu.PrefetchScalarGridSpec(num_scalar_prefetch=1, grid=(2,), in_specs=[full, half, half, full, full],
                                      out_specs=[full] * 4)
    return pl.pallas_call(
        body, name=name, grid_spec=gs, out_shape=[jax.ShapeDtypeStruct((rows, 1, cols), F32)] * 4,
        compiler_params=_params(("arbitrary",)),
    )(c_idx, w, mine, theirs, m, v)


SMALL = ("g_mix_pre", "g_mix_post", "g_ffn_pre", "g_ffn_post")
SMALL_ALL = SMALL + ("g_attn_out", "g_conv_out", "conv_w", "b_forget")
SMALL_AT = {"g_mix_pre": (0, 0, 1024), "g_mix_post": (1, 0, 1024), "g_ffn_pre": (2, 0, 1024),
            "g_ffn_post": (3, 0, 1024), "g_attn_out": (4, 0, 512), "g_conv_out": (4, 512, 512),
            "b_forget": (7, 0, N_HEADS)}
CONV_AT = ((5, 0), (5, 512), (6, 0))
LOSS_AT = (6, 512)


def _pack_small(t, conv_full, loss_sum):
    conv = jnp.concatenate([conv_full.reshape(1, 3 * CONV_W), loss_sum.reshape(1, 1),
                            jnp.zeros((1, 2048 - 3 * CONV_W - 1), F32)], axis=1).reshape(2, 1024)
    return jnp.concatenate([t[n].reshape(1, 1024) for n in SMALL]
                           + [jnp.concatenate([t["g_attn_out"].reshape(1, 512), t["g_conv_out"].reshape(1, 512)], axis=1),
                              conv, jnp.pad(t["b_forget"].reshape(1, N_HEADS), ((0, 0), (0, 1024 - N_HEADS)))], axis=0)


def _small_update(me_idx, gathered, w, m, v):
    def body(me_ref, gg_ref, *refs):
        k = len(SMALL_ALL)
        w_refs, m_refs, v_refs = refs[:k], refs[k:2 * k], refs[2 * k:3 * k]
        loss_ref = refs[3 * k]
        outs = refs[3 * k + 1:3 * k + 1 + 4 * k]
        sums = refs[-1]
        g = gg_ref[0]
        for dev in range(1, 8):
            g = g + gg_ref[dev]
        sums[...] = g
        loss_ref[...] = sums[LOSS_AT[0]:LOSS_AT[0] + 1, LOSS_AT[1]:LOSS_AT[1] + 1]
        mine = pl.multiple_of(me_ref[0] * 128, 128)
        for idx, name in enumerate(SMALL_ALL):
            g_ref, d_ref, nm_ref, nv_ref = outs[4 * idx:4 * idx + 4]
            if name == "conv_w":
                for r, (row, lo) in enumerate(CONV_AT):
                    gr = sums[row:row + 1, pl.ds(lo + mine, 128)]
                    g_ref[0, r:r + 1, :] = gr
                    d_ref[0, r:r + 1, :], nm_ref[0, r:r + 1, :], nv_ref[0, r:r + 1, :] = _adamw_math(
                        w_refs[idx][0, r:r + 1, :], gr, m_refs[idx][0, r:r + 1, :], v_refs[idx][0, r:r + 1, :])
            else:
                row, lo, n = SMALL_AT[name]
                gr = sums[row:row + 1, lo:lo + n]
                g_ref[...] = gr
                d_ref[...], nm_ref[...], nv_ref[...] = _adamw_math(w_refs[idx][...], gr, m_refs[idx][...],
                                                                    v_refs[idx][...])

    def whole(a):
        nd = a.ndim
        return pl.BlockSpec(a.shape, lambda i, mr: (0,) * nd)

    ins = [t[n] for t in (w, m, v) for n in SMALL_ALL]
    out_shape = [jax.ShapeDtypeStruct((1, 1), F32)]
    for n in SMALL_ALL:
        out_shape += [jax.ShapeDtypeStruct(w[n].shape, F32)] * 4
    gs = pltpu.PrefetchScalarGridSpec(
        num_scalar_prefetch=1, grid=(1,), in_specs=[whole(gathered)] + [whole(a) for a in ins],
        out_specs=[whole(o) for o in out_shape], scratch_shapes=[pltpu.VMEM((8, 1024), F32)])
    out = pl.pallas_call(body, name="small_update", grid_spec=gs, out_shape=out_shape,
                         compiler_params=_params(("arbitrary",)))(me_idx, gathered, *ins)
    return out[0], {n: out[1 + 4 * i:5 + 4 * i] for i, n in enumerate(SMALL_ALL)}


def kernel(x, w_in, b_forget, conv_w, g_attn_out, g_conv_out, w_out, g_mix_pre, g_mix_post, w_gate_up, w_down, g_ffn_pre, g_ffn_post, loss_target, m_w_in, m_b_forget, m_conv_w, m_g_attn_out, m_g_conv_out, m_w_out, m_g_mix_pre, m_g_mix_post, m_w_gate_up, m_w_down, m_g_ffn_pre, m_g_ffn_post, v_w_in, v_b_forget, v_conv_w, v_g_attn_out, v_g_conv_out, v_w_out, v_g_mix_pre, v_g_mix_post, v_w_gate_up, v_w_down, v_g_ffn_pre, v_g_ffn_post):
    w = dict(w_in=w_in, b_forget=b_forget, conv_w=conv_w, g_attn_out=g_attn_out, g_conv_out=g_conv_out, w_out=w_out,
             g_mix_pre=g_mix_pre, g_mix_post=g_mix_post, w_gate_up=w_gate_up, w_down=w_down, g_ffn_pre=g_ffn_pre,
             g_ffn_post=g_ffn_post)
    m = dict(w_in=m_w_in, b_forget=m_b_forget, conv_w=m_conv_w, g_attn_out=m_g_attn_out, g_conv_out=m_g_conv_out,
             w_out=m_w_out, g_mix_pre=m_g_mix_pre, g_mix_post=m_g_mix_post, w_gate_up=m_w_gate_up, w_down=m_w_down,
             g_ffn_pre=m_g_ffn_pre, g_ffn_post=m_g_ffn_post)
    v = dict(w_in=v_w_in, b_forget=v_b_forget, conv_w=v_conv_w, g_attn_out=v_g_attn_out, g_conv_out=v_g_conv_out,
             w_out=v_w_out, g_mix_pre=v_g_mix_pre, g_mix_post=v_g_mix_post, w_gate_up=v_w_gate_up, w_down=v_w_down,
             g_ffn_pre=v_g_ffn_pre, g_ffn_post=v_g_ffn_post)
    cx, cy, cc = lax.axis_index("x"), lax.axis_index("y"), lax.axis_index("c")
    me = 2 * cx + cy
    c_idx = cc.astype(jnp.int32).reshape(1)
    me_idx = me.astype(jnp.int32).reshape(1)

    stored = lambda a: jnp.transpose(a, (2, 0, 1))
    grad_x, big, small_all = _device_step(x[0], loss_target[0], w, m, v, stored(w_in), stored(m_w_in),
                                          stored(v_w_in), c_idx, me_idx)
    gsum, delta, new_m, new_v = {}, {}, {}, {}
    for n in BIG:
        back = (lambda r: jnp.transpose(r, (1, 2, 0))) if n == "w_in" else (lambda r: r[None])
        gsum[n], delta[n], new_m[n], new_v[n] = [back(r) for r in big[n]]
    loss_sum, small_new = _small_update(me_idx, small_all, w, m, v)
    for n in SMALL_ALL:
        gsum[n], delta[n], new_m[n], new_v[n] = small_new[n]
    loss = 0.5 * loss_sum[0, 0]

    order = ("w_in", "b_forget", "conv_w", "g_attn_out", "g_conv_out", "w_out", "g_mix_pre", "g_mix_post",
             "w_gate_up", "w_down", "g_ffn_pre", "g_ffn_post")
    return (loss, grad_x[None], *[gsum[n] for n in order], *[delta[n] for n in order],
            *[new_m[n] for n in order], *[new_v[n] for n in order])
```

```python
import functools

import jax
import jax.numpy as jnp
import numpy as np
from jax import lax
from jax.experimental import pallas as pl
from jax.experimental.pallas import tpu as pltpu

F32 = jnp.float32
BF16 = jnp.bfloat16
MXU_DTYPE = jnp.bfloat16

D_MODEL = 1024
HEAD_DIM = 64
N_HEADS = 8
ATTN_W = 512
CONV_W = 512
D_FF = 2816
FF_PIECE = 1408
EPS = 1e-6
Q_SCALE = HEAD_DIM ** -0.5

OFF_F = 1536
OFF_B = 1664
OFF_C = 2176
OFF_U = 2688
IN_PAD = 3200
IN_W = 3080
N_CHIPS = 4

ADAM_LR = 0.001
ADAM_B1 = 0.9
ADAM_B2 = 0.999
ADAM_EPS = 1e-08
ADAM_WD = 0.01
ADAM_STEP = 10

VMEM_LIMIT_V7X = 56 * 1024 * 1024
MESH_ID = pl.DeviceIdType.MESH


def _params(sem=None, vmem=VMEM_LIMIT_V7X):
    kw = {"vmem_limit_bytes": vmem}
    if sem is not None:
        kw["dimension_semantics"] = sem
    return pltpu.CompilerParams(**kw)


def _dot(a, b):
    return jnp.dot(a, b, preferred_element_type=F32)


def _dot_nt(a, b):
    return lax.dot_general(a, b, (((1,), (1,)), ((), ())), preferred_element_type=F32)


def _dot_exact(x, ones, parts):
    if ones.dtype == F32:
        return _dot(x, ones)
    acc = None
    rem = x
    for _ in range(parts):
        piece = rem.astype(BF16)
        rem = rem - piece.astype(F32)
        term = _dot(piece, ones)
        acc = term if acc is None else acc + term
    return acc


def _rms(v):
    return lax.rsqrt(jnp.mean(v * v, axis=-1, keepdims=True) + EPS)


def _tok(tm, w):
    return pl.BlockSpec((tm, w), lambda i: (i, 0))


def _whole(shape, single=False):
    nd = len(shape)
    if single:
        return pl.BlockSpec(shape, lambda i: (0,) * nd, pipeline_mode=pl.Buffered(1))
    return pl.BlockSpec(shape, lambda i: (0,) * nd)


def _feat(rows, tm):
    return pl.BlockSpec((rows, tm), lambda i: (0, i))


def _inproj_fwd(x, g_pre, w_t, tm):
    s = x.shape[0]

    def body(x_ref, g_ref, w_ref, h_ref, q_ref, k_ref, v_ref, kt_ref, vt_ref, zt_ref, b_ref, c_ref, u_ref):
        xv = x_ref[...]
        h = ((xv * _rms(xv)) * g_ref[...]).astype(MXU_DTYPE)
        h_ref[...] = h

        def proj(lo, hi):
            return _dot_nt(h, w_ref[lo:hi, :])

        q_ref[...] = (proj(0, 512) * Q_SCALE).astype(MXU_DTYPE)
        kt = _dot_nt(w_ref[512:1024, :], h)
        vt = _dot_nt(w_ref[1024:OFF_F, :], h)
        kt_ref[...] = kt.astype(MXU_DTYPE)
        vt_ref[...] = vt.astype(MXU_DTYPE)
        k_ref[...] = kt.T.astype(MXU_DTYPE)
        v_ref[...] = vt.T.astype(MXU_DTYPE)
        zt_ref[...] = _dot_nt(w_ref[OFF_F:OFF_B, :], h)
        b_ref[...] = proj(OFF_B, OFF_C)
        c_ref[...] = proj(OFF_C, OFF_U)
        u_ref[...] = proj(OFF_U, IN_PAD)

    sd = jax.ShapeDtypeStruct
    return pl.pallas_call(
        body, name="inproj_fwd", grid=(s // tm,),
        in_specs=[_tok(tm, D_MODEL), _whole((1, D_MODEL)), _whole((IN_PAD, D_MODEL), single=True)],
        out_specs=[_tok(tm, D_MODEL), _tok(tm, 512), _tok(tm, 512), _tok(tm, 512), _feat(512, tm), _feat(512, tm),
                   _feat(128, tm), _tok(tm, 512), _tok(tm, 512), _tok(tm, 512)],
        out_shape=[sd((s, D_MODEL), MXU_DTYPE), sd((s, 512), MXU_DTYPE), sd((s, 512), MXU_DTYPE),
                   sd((s, 512), MXU_DTYPE), sd((512, s), MXU_DTYPE), sd((512, s), MXU_DTYPE), sd((128, s), F32),
                   sd((s, 512), F32), sd((s, 512), F32), sd((s, 512), F32)],
        compiler_params=_params(("arbitrary",)),
    )(x, g_pre, w_t)


def _tri(n, upper):
    r = lax.broadcasted_iota(jnp.int32, (n, n), 0)
    c = lax.broadcasted_iota(jnp.int32, (n, n), 1)
    return ((r <= c) if upper else (r >= c)).astype(MXU_DTYPE)


HEAD_ROWS = 16


def _rows_to_cols(v):
    return jnp.concatenate([v, jnp.zeros((128 - HEAD_ROWS, 128), F32)], axis=0).T


BIAS_PARTS = 3


def _bias_placement():
    place_q = np.zeros((BIAS_PARTS, 128, ATTN_W), np.float32)
    place_k = np.zeros((BIAS_PARTS, 128, ATTN_W), np.float32)
    ones_q = np.zeros((1, ATTN_W), np.float32)
    ones_k = np.zeros((1, ATTN_W), np.float32)
    for h in range(N_HEADS):
        base = 2 * HEAD_DIM * (h // 2) + HEAD_DIM * (1 - h % 2)
        for part in range(BIAS_PARTS):
            place_q[part, h, base + part] = 1.0
            place_k[part, h, base + BIAS_PARTS + part] = -1.0
        ones_q[0, base + BIAS_PARTS:base + 2 * BIAS_PARTS] = 1.0
        ones_k[0, base:base + BIAS_PARTS] = 1.0
    return (jnp.asarray(place_q, MXU_DTYPE), jnp.asarray(place_k, MXU_DTYPE), jnp.asarray(ones_q), jnp.asarray(ones_k))


def _forget_fwd(z_t, b_col):
    s = z_t.shape[1]
    nb = s // 128

    def body(z_ref, b_ref, pq_ref, pk_ref, oq_ref, ok_ref, c_ref, cc_ref, qa_ref, ka_ref):
        upper = _tri(128, True)

        carry = jnp.zeros((HEAD_ROWS, 1), F32)
        for n in range(nb):
            off = n * 128
            lf = jax.nn.log_sigmoid(z_ref[0:HEAD_ROWS, off:off + 128] + b_ref[...])
            cs = _dot_exact(lf, upper, 3) + carry
            c_ref[:, off:off + 128] = cs
            cc_ref[off:off + 128, :] = _rows_to_cols(cs)
            carry = carry + jnp.sum(lf, axis=1, keepdims=True)

        rb = min(s, 512)
        for off in range(0, s, rb):
            qa = jnp.broadcast_to(oq_ref[...], (rb, ATTN_W))
            ka = jnp.broadcast_to(ok_ref[...], (rb, ATTN_W))
            rem = cc_ref[off:off + rb, :]
            for part in range(BIAS_PARTS):
                piece = rem.astype(MXU_DTYPE)
                rem = rem - piece.astype(F32)
                qa = qa + _dot(piece, pq_ref[part])
                ka = ka + _dot(piece, pk_ref[part])
            qa_ref[off:off + rb, :] = qa.astype(MXU_DTYPE)
            ka_ref[off:off + rb, :] = ka.astype(MXU_DTYPE)

    sd = jax.ShapeDtypeStruct
    return pl.pallas_call(body, name="forget_fwd",
                          out_shape=[sd((HEAD_ROWS, s), F32), sd((s, 128), F32), sd((s, ATTN_W), MXU_DTYPE),
                                     sd((s, ATTN_W), MXU_DTYPE)],
                          compiler_params=_params())(z_t, b_col, *_bias_placement())


def _aligned(start, size):
    return pl.ds(start if isinstance(start, int) else pl.multiple_of(start, size), size)


def _pair_lanes(pp):
    return _aligned(pp * 2 * HEAD_DIM, 2 * HEAD_DIM)


def _head_rows(h):
    return _aligned(h * HEAD_DIM, HEAD_DIM)


def _only_head(block, hb):
    lane = lax.broadcasted_iota(jnp.int32, block.shape, 1)
    return jnp.where((lane >= HEAD_DIM) if hb else (lane < HEAD_DIM), block, jnp.zeros_like(block))


def _head_col(cols, h):
    lane = lax.broadcasted_iota(jnp.int32, cols.shape, 1)
    return jnp.sum(jnp.where(lane == h, cols, 0.0), axis=1, keepdims=True)


def _other_head(block, other, hb):
    lane = lax.broadcasted_iota(jnp.int32, block.shape, 1)
    return jnp.where((lane >= HEAD_DIM) if hb else (lane < HEAD_DIM), block, other)


def _attn_fwd(qs, k, v_t, q_bias, k_bias, t, shards):
    s = qs.shape[0]
    n = s // t
    pairs = [(i, j) for i in range(n) for j in range(i + 1)]
    it = jnp.asarray(np.array([p[0] for p in pairs], np.int32))
    jt = jnp.asarray(np.array([p[1] for p in pairs], np.int32))
    nw = len(shards)
    last = len(pairs) - 1
    mid = (2 * len(pairs)) // 3

    def body(it_ref, jt_ref, q_ref, k_ref, vt_ref, qb_ref, kb_ref, *rest):
        sh, (o_ref, lse_ref), got = rest[:nw], rest[nw:nw + 2], rest[nw + 2:2 * nw + 2]
        m_sc, l_sc, acc_sc, send, recv = rest[2 * nw + 2:]
        p = pl.program_id(0)
        i = it_ref[p]
        j = jt_ref[p]
        gather_start, gather_forward, gather_finish = _gather_stages(sh, got, send, recv)
        pl.when(p == 0)(gather_start)
        if mid < last:
            pl.when(p == mid)(gather_forward)

        @pl.when(j == 0)
        def _():
            m_sc[...] = jnp.full_like(m_sc, -1e30)
            l_sc[...] = jnp.ones_like(l_sc)
            acc_sc[...] = jnp.zeros_like(acc_sc)

        def pair_step(pp, diagonal):
            lanes = _pair_lanes(pp)
            kp = k_ref[:, lanes]
            qp = q_ref[:, lanes]
            kb = kb_ref[:, lanes]
            qb = qb_ref[:, lanes]
            for hb in range(2):
                h = 2 * pp + hb
                row = pl.ds(h, 1)
                rows = _head_rows(h)
                st = _dot_nt(_other_head(kp, kb, hb), _other_head(qp, qb, hb))
                if diagonal:
                    kpos = lax.broadcasted_iota(jnp.int32, (t, t), 0)
                    qpos = lax.broadcasted_iota(jnp.int32, (t, t), 1)
                    st = jnp.where(kpos <= qpos, st, -1e30)
                m_prev = m_sc[row, :]
                m_new = jnp.maximum(m_prev, jnp.max(st, axis=0, keepdims=True))
                alpha = jnp.exp(m_prev - m_new)
                pt = jnp.exp(st - m_new)
                l_sc[row, :] = alpha * l_sc[row, :] + jnp.sum(pt, axis=0, keepdims=True)
                acc_sc[rows, :] = acc_sc[rows, :] * alpha + _dot(vt_ref[rows, :], pt.astype(MXU_DTYPE))
                m_sc[row, :] = m_new

        @pl.when(j < i)
        def _():
            for pp in range(N_HEADS // 2):
                pair_step(pp, False)

        @pl.when(j == i)
        def _():
            for pp in range(N_HEADS // 2):
                pair_step(pp, True)
                sub = lax.broadcasted_iota(jnp.int32, (2 * HEAD_DIM, t), 0)
                l_pair = jnp.where(sub < HEAD_DIM, l_sc[pl.ds(2 * pp, 1), :], l_sc[pl.ds(2 * pp + 1, 1), :])
                o_t = acc_sc[_aligned(pp * 2 * HEAD_DIM, 2 * HEAD_DIM), :] / l_pair
                o_ref[:, _pair_lanes(pp)] = o_t.T

            lse_ref[...] = m_sc[...] + jnp.log(l_sc[...])

        @pl.when(p == last)
        def _():
            if mid >= last:
                gather_forward()
            gather_finish()

    gs = pltpu.PrefetchScalarGridSpec(
        num_scalar_prefetch=2, grid=(len(pairs),),
        in_specs=[pl.BlockSpec((t, ATTN_W), lambda p, it_, jt_: (it_[p], 0)),
                  pl.BlockSpec((t, ATTN_W), lambda p, it_, jt_: (jt_[p], 0)),
                  pl.BlockSpec((ATTN_W, t), lambda p, it_, jt_: (0, jt_[p])),
                  pl.BlockSpec((t, ATTN_W), lambda p, it_, jt_: (it_[p], 0)),
                  pl.BlockSpec((t, ATTN_W), lambda p, it_, jt_: (jt_[p], 0))] + [ANY] * nw,
        out_specs=[pl.BlockSpec((t, ATTN_W), lambda p, it_, jt_: (it_[p], 0)),
                   pl.BlockSpec((HEAD_ROWS, t), lambda p, it_, jt_: (0, it_[p]))] + [ANY] * nw,
        scratch_shapes=[pltpu.VMEM((HEAD_ROWS, t), F32), pltpu.VMEM((HEAD_ROWS, t), F32), pltpu.VMEM((ATTN_W, t), F32),
                        pltpu.SemaphoreType.DMA((nw, 6)), pltpu.SemaphoreType.DMA((nw, 6))])
    o, lse, *got = pl.pallas_call(
        body, name="attn_fwd", grid_spec=gs,
        out_shape=[jax.ShapeDtypeStruct((s, ATTN_W), F32), jax.ShapeDtypeStruct((HEAD_ROWS, s), F32)]
        + [jax.ShapeDtypeStruct((N_CHIPS,) + a.shape, a.dtype) for a in shards],
        compiler_params=_params(("arbitrary",)),
    )(it, jt, qs, k, v_t, q_bias, k_bias, *shards)
    me = 2 * lax.axis_index("x") + lax.axis_index("y")
    return o, lse, [lax.dynamic_update_index_in_dim(g, own, me, 0) for g, own in zip(got, shards)]


def _shift_down(cur, prev_ref, first):
    row = lax.broadcasted_iota(jnp.int32, cur.shape, 0)
    p7 = jnp.where(first, 0.0, prev_ref[0][7:8, :] * prev_ref[1][7:8, :])
    p6 = jnp.where(first, 0.0, prev_ref[0][6:7, :] * prev_ref[1][6:7, :])
    s1 = jnp.where(row == 0, p7, pltpu.roll(cur, 1, 0))
    s2 = jnp.where(row == 0, p6, jnp.where(row == 1, p7, pltpu.roll(cur, 2, 0)))
    return s1, s2


def _group_ms(v, gmat):
    return _dot_exact(v, gmat, 2) * (1.0 / HEAD_DIM)


def _mixer_fwd(x, o_attn, gate_b, gate_c, u, conv_w, g_attn, g_conv, w_out, g_post, gmat, tm):
    s = x.shape[0]

    def body(x_ref, o_ref, b_ref, c_ref, u_ref, cp_ref, up_ref, cw_ref, ga_ref, gc_ref, wo_ref, gp_ref, gm_ref,
             x2_ref, mg_ref, y_ref, z_ref):
        i = pl.program_id(0)
        cu = c_ref[...] * u_ref[...]
        cu1, cu2 = _shift_down(cu, (cp_ref, up_ref), i == 0)
        z = cw_ref[0:1, :] * cu2 + cw_ref[1:2, :] * cu1 + cw_ref[2:3, :] * cu
        z_ref[...] = z
        cv = b_ref[...] * z
        ov = o_ref[...]
        gm = gm_ref[...]
        ma = ((ov * lax.rsqrt(_group_ms(ov * ov, gm) + EPS)) * ga_ref[...]).astype(MXU_DTYPE)
        mc = ((cv * lax.rsqrt(_group_ms(cv * cv, gm) + EPS)) * gc_ref[...]).astype(MXU_DTYPE)
        mg_ref[:, 0:ATTN_W] = ma
        mg_ref[:, ATTN_W:D_MODEL] = mc
        y = _dot(ma, wo_ref[0:ATTN_W, :]) + _dot(mc, wo_ref[ATTN_W:D_MODEL, :])
        y_ref[...] = y
        x2_ref[...] = x_ref[...] + (y * _rms(y)) * gp_ref[...]

    halo = pl.BlockSpec((8, 512), lambda i: (jnp.maximum(i * (tm // 8) - 1, 0), 0))
    sd = jax.ShapeDtypeStruct
    return pl.pallas_call(
        body, name="mixer_fwd", grid=(s // tm,),
        in_specs=[_tok(tm, D_MODEL), _tok(tm, 512), _tok(tm, 512), _tok(tm, 512), _tok(tm, 512), halo, halo,
                  _whole((3, 512)), _whole((1, 512)), _whole((1, 512)), _whole((D_MODEL, D_MODEL), single=True),
                  _whole((1, D_MODEL)), _whole((512, 512))],
        out_specs=[_tok(tm, D_MODEL), _tok(tm, D_MODEL), _tok(tm, D_MODEL), _tok(tm, 512)],
        out_shape=[sd((s, D_MODEL), F32), sd((s, D_MODEL), MXU_DTYPE), sd((s, D_MODEL), F32), sd((s, 512), F32)],
        compiler_params=_params(("arbitrary",)),
    )(x, o_attn, gate_b, gate_c, u, gate_c, u, conv_w, g_attn, g_conv, w_out, g_post, gmat)


def _ffn_fwd(x2, target, g_pre, w_gu, w_dn, g_post, tm):
    s = x2.shape[0]

    def body(x_ref, t_ref, gpre_ref, wgu_ref, wdn_ref, gpost_ref,
             h_ref, g_ref, up_ref, a_ref, ff_ref, dout_ref, loss_ref):
        xv = x_ref[...]
        h = ((xv * _rms(xv)) * gpre_ref[...]).astype(MXU_DTYPE)
        h_ref[...] = h
        ff = jnp.zeros((tm, D_MODEL), F32)
        for j in range(2):
            cols = slice(j * FF_PIECE, (j + 1) * FF_PIECE)
            g = _dot(h, wgu_ref[j])
            up = _dot(h, wgu_ref[2 + j])
            a = ((g * jax.nn.sigmoid(g)) * up).astype(MXU_DTYPE)
            g_ref[:, cols] = g.astype(MXU_DTYPE)
            up_ref[:, cols] = up.astype(MXU_DTYPE)
            a_ref[:, cols] = a
            ff = ff + _dot(a, wdn_ref[j])
        ff_ref[...] = ff
        err = (xv + (ff * _rms(ff)) * gpost_ref[...]) - t_ref[...]
        dout_ref[...] = err * (1.0 / D_MODEL)
        part = jnp.sum(jnp.mean(err * err, axis=-1, keepdims=True), axis=0, keepdims=True)

        @pl.when(pl.program_id(0) == 0)
        def _():
            loss_ref[...] = jnp.zeros_like(loss_ref)

        loss_ref[...] += part

    sd = jax.ShapeDtypeStruct
    return pl.pallas_call(
        body, name="ffn_fwd", grid=(s // tm,),
        in_specs=[_tok(tm, D_MODEL), _tok(tm, D_MODEL), _whole((1, D_MODEL)),
                  _whole((4, D_MODEL, FF_PIECE), single=True), _whole((2, FF_PIECE, D_MODEL), single=True),
                  _whole((1, D_MODEL))],
        out_specs=[_tok(tm, D_MODEL), _tok(tm, D_FF), _tok(tm, D_FF), _tok(tm, D_FF), _tok(tm, D_MODEL),
                   _tok(tm, D_MODEL), _whole((8, 128))],
        out_shape=[sd((s, D_MODEL), MXU_DTYPE), sd((s, D_FF), MXU_DTYPE), sd((s, D_FF), MXU_DTYPE),
                   sd((s, D_FF), MXU_DTYPE), sd((s, D_MODEL), F32), sd((s, D_MODEL), F32), sd((8, 128), F32)],
        compiler_params=_params(("arbitrary",)),
    )(x2, target, g_pre, w_gu, w_dn, g_post)


def _norm_bwd(dy, normed, rinv, gain):
    t = dy * gain
    return rinv * (t - normed * jnp.mean(t * normed, axis=-1, keepdims=True))


def _acc_rows(ref, first, val):
    @pl.when(first)
    def _():
        ref[...] = jnp.zeros_like(ref)

    ref[...] += jnp.sum(val, axis=0, keepdims=True)


def _ffn_bwd(dout, ff, x2, g, up, g_post, g_pre, w_gu, w_dn, tm):
    s = x2.shape[0]

    def body(do_ref, ff_ref, x_ref, g_ref, up_ref, gpost_ref, gpre_ref, wgu_ref, wdn_ref,
             dx_ref, dff_ref, dgu_ref, dgpost_ref, dgpre_ref):
        first = pl.program_id(0) == 0
        ffv = ff_ref[...]
        rf = _rms(ffv)
        n = ffv * rf
        do = do_ref[...]
        _acc_rows(dgpost_ref, first, do * n)
        dff = _norm_bwd(do, n, rf, gpost_ref[...]).astype(MXU_DTYPE)
        dff_ref[...] = dff
        dh = jnp.zeros((tm, D_MODEL), F32)
        for j in range(2):
            cols = slice(j * FF_PIECE, (j + 1) * FF_PIECE)
            da = _dot_nt(dff, wdn_ref[j])
            gv = g_ref[:, cols].astype(F32)
            sg = jax.nn.sigmoid(gv)
            dg = (da * up_ref[:, cols].astype(F32) * (sg * (1.0 + gv * (1.0 - sg)))).astype(MXU_DTYPE)
            du = (da * (gv * sg)).astype(MXU_DTYPE)
            dgu_ref[:, cols] = dg
            dgu_ref[:, D_FF + j * FF_PIECE:D_FF + (j + 1) * FF_PIECE] = du
            dh = dh + _dot_nt(dg, wgu_ref[j]) + _dot_nt(du, wgu_ref[2 + j])
        xv = x_ref[...]
        r2 = _rms(xv)
        nx = xv * r2
        _acc_rows(dgpre_ref, first, dh * nx)
        dx_ref[...] = do + _norm_bwd(dh, nx, r2, gpre_ref[...])

    sd = jax.ShapeDtypeStruct
    return pl.pallas_call(
        body, name="ffn_bwd", grid=(s // tm,),
        in_specs=[_tok(tm, D_MODEL), _tok(tm, D_MODEL), _tok(tm, D_MODEL), _tok(tm, D_FF), _tok(tm, D_FF),
                  _whole((1, D_MODEL)), _whole((1, D_MODEL)),
                  _whole((4, D_MODEL, FF_PIECE), single=True), _whole((2, FF_PIECE, D_MODEL), single=True)],
        out_specs=[_tok(tm, D_MODEL), _tok(tm, D_MODEL), _tok(tm, 2 * D_FF), _whole((1, D_MODEL)),
                   _whole((1, D_MODEL))],
        out_shape=[sd((s, D_MODEL), F32), sd((s, D_MODEL), MXU_DTYPE), sd((s, 2 * D_FF), MXU_DTYPE),
                   sd((1, D_MODEL), F32), sd((1, D_MODEL), F32)],
        compiler_params=_params(("arbitrary",)),
    )(dout, ff, x2, g, up, g_post, g_pre, w_gu, w_dn)


def _tn_matmul(a, b, tm, tn, tk, name):
    s, m = a.shape
    n = b.shape[1]

    def body(a_ref, b_ref, o_ref):
        @pl.when(pl.program_id(2) == 0)
        def _():
            o_ref[...] = jnp.zeros_like(o_ref)

        o_ref[...] += lax.dot_general(a_ref[...], b_ref[...], (((0,), (0,)), ((), ())), preferred_element_type=F32)

    return pl.pallas_call(
        body, name=name, grid=(m // tm, n // tn, s // tk),
        in_specs=[pl.BlockSpec((tk, tm), lambda i, j, kk: (kk, i)), pl.BlockSpec((tk, tn), lambda i, j, kk: (kk, j))],
        out_specs=pl.BlockSpec((tm, tn), lambda i, j, kk: (i, j)),
        out_shape=jax.ShapeDtypeStruct((m, n), F32),
        compiler_params=_params(("arbitrary", "arbitrary", "arbitrary")),
    )(a, b)


def _mixer_bwd(dx2, y, o_attn, gate_b, z, g_post, g_attn, g_conv, w_out, gmat, sel, tm, ready, kinds):
    s = dx2.shape[0]
    nw = len(ready)
    nt = s // tm

    def body(d_ref, y_ref, o_ref, b_ref, z_ref, gp_ref, ga_ref, gc_ref, wo_ref, gm_ref, sel_ref, *rest):
        grads = rest[:nw]
        dy_ref, do_ref, db_ref, dz_ref, delta_ref, dgp_ref, dga_ref, dgc_ref = rest[nw:nw + 8]
        taken = rest[nw + 8:2 * nw + 8]
        send, recv = rest[2 * nw + 8:]
        first = pl.program_id(0) == 0
        pair_start, pair_finish = _pair_stages(grads, kinds, taken, send, recv)
        pl.when(first)(pair_start)
        yv = y_ref[...]
        ry = _rms(yv)
        ny = yv * ry
        d = d_ref[...]
        _acc_rows(dgp_ref, first, d * ny)
        dy = _norm_bwd(d, ny, ry, gp_ref[...]).astype(MXU_DTYPE)
        dy_ref[...] = dy
        dm = _dot_nt(dy, wo_ref[...])
        gm = gm_ref[...]

        def group_bwd(val, dmv, gain, dg_ref):
            rg = lax.rsqrt(_group_ms(val * val, gm) + EPS)
            nv = val * rg
            _acc_rows(dg_ref, first, dmv * nv)
            t = dmv * gain
            return rg * (t - nv * _group_ms(t * nv, gm))

        ov = o_ref[...]
        d_o = group_bwd(ov, dm[:, 0:ATTN_W], ga_ref[...], dga_ref)
        do_ref[...] = d_o.astype(MXU_DTYPE)
        delta_ref[...] = _dot_exact(d_o * ov, sel_ref[...], 2).T[0:HEAD_ROWS, :]
        zv = z_ref[...]
        bv = b_ref[...]
        d_cv = group_bwd(bv * zv, dm[:, ATTN_W:D_MODEL], gc_ref[...], dgc_ref)
        db_ref[...] = d_cv * zv
        dz_ref[...] = d_cv * bv
        pl.when(pl.program_id(0) == nt - 1)(pair_finish)

    sd = jax.ShapeDtypeStruct
    taken_shape = [sd((N_CHIPS, g.shape[-2], g.shape[-1] if kd == "rows" else g.shape[-1] // N_CHIPS), F32)
                   for g, kd in zip(ready, kinds)]
    out = pl.pallas_call(
        body, name="mixer_bwd", grid=(nt,),
        in_specs=[_tok(tm, D_MODEL), _tok(tm, D_MODEL), _tok(tm, 512), _tok(tm, 512), _tok(tm, 512),
                  _whole((1, D_MODEL)), _whole((1, 512)), _whole((1, 512)),
                  _whole((D_MODEL, D_MODEL), single=True), _whole((512, 512)), _whole((512, 128))] + [ANY] * nw,
        out_specs=[_tok(tm, D_MODEL), _tok(tm, 512), _tok(tm, 512), _tok(tm, 512), _feat(HEAD_ROWS, tm),
                   _whole((1, D_MODEL)), _whole((1, 512)), _whole((1, 512))] + [ANY] * nw,
        out_shape=[sd((s, D_MODEL), MXU_DTYPE), sd((s, 512), MXU_DTYPE), sd((s, 512), F32), sd((s, 512), F32),
                   sd((HEAD_ROWS, s), F32), sd((1, D_MODEL), F32), sd((1, 512), F32), sd((1, 512), F32)] + taken_shape,
        scratch_shapes=[pltpu.SemaphoreType.DMA((nw, N_CHIPS)), pltpu.SemaphoreType.DMA((nw, N_CHIPS))],
        compiler_params=_params(("arbitrary",)),
    )(dx2, y, o_attn, gate_b, z, g_post, g_attn, g_conv, w_out, gmat, sel, *ready)
    return out[:8], out[8:]


def _attn_bwd(qs, k, k_t, v, do, c_rows, c_cols, lse, delta, t, parts):
    s = qs.shape[0]
    n = s // t
    pairs = [(i, j) for j in range(n) for i in range(j, n)]
    it = jnp.asarray(np.array([p[0] for p in pairs], np.int32))
    jt = jnp.asarray(np.array([p[1] for p in pairs], np.int32))

    nw = len(parts)

    def body(it_ref, jt_ref, q_ref, k_ref, kt_ref, v_ref, do_ref, cq_ref, ck_ref, lse_ref, dl_ref, *rest):
        pb = rest[:nw]
        dq_ref, dk_ref, dv_ref, dc_ref, dcq_ref = rest[nw:nw + 5]
        rcv = rest[nw + 5:2 * nw + 5]
        dk_sc, dv_sc, dc_sc, send, recv = rest[2 * nw + 5:]
        p = pl.program_id(0)
        i = it_ref[p]
        j = jt_ref[p]
        chip_start, chip_finish = _chip_stages(pb, rcv, send, recv)

        @pl.when(p == 0)
        def _():
            chip_start()
            dq_ref[...] = jnp.zeros_like(dq_ref)
            dcq_ref[...] = jnp.zeros_like(dcq_ref)

        @pl.when(i == j)
        def _():
            dk_sc[...] = jnp.zeros_like(dk_sc)
            dv_sc[...] = jnp.zeros_like(dv_sc)
            dc_sc[...] = jnp.zeros_like(dc_sc)

        def pair_step(pp, diagonal):
            lanes = _pair_lanes(pp)
            qp = q_ref[:, lanes]
            kp = k_ref[:, lanes]
            vp = v_ref[:, lanes]
            dop = do_ref[:, lanes]
            ck_all = ck_ref[...]
            lane = lax.broadcasted_iota(jnp.int32, (t, 128), 1)
            for hb in range(2):
                h = 2 * pp + hb
                row = pl.ds(h, 1)
                bias = (cq_ref[row, :] - lse_ref[row, :]) - _head_col(ck_all, h)
                pt = jnp.exp(_dot_nt(_only_head(kp, hb), qp) + bias)
                if diagonal:
                    kpos = lax.broadcasted_iota(jnp.int32, (t, t), 0)
                    qpos = lax.broadcasted_iota(jnp.int32, (t, t), 1)
                    pt = jnp.where(kpos <= qpos, pt, 0.0)
                dv_sc[:, lanes] += _dot(pt.astype(MXU_DTYPE), _only_head(dop, hb))
                dst = pt * (_dot_nt(_only_head(vp, hb), dop) - dl_ref[row, :])
                dc_sc[...] -= jnp.where(lane == h, jnp.sum(dst, axis=1, keepdims=True), 0.0)
                dcq_ref[i, row, :] += jnp.sum(dst, axis=0, keepdims=True)
                dsb = dst.astype(MXU_DTYPE)
                dk_sc[:, lanes] += _dot(dsb, _only_head(qp, hb))
                rows = _head_rows(h)
                dq_ref[i, rows, :] += _dot(kt_ref[rows, :], dsb)

        @pl.when(i > j)
        def _():
            for pp in range(N_HEADS // 2):
                pair_step(pp, False)

        @pl.when(i == j)
        def _():
            for pp in range(N_HEADS // 2):
                pair_step(pp, True)

        @pl.when(i == n - 1)
        def _():
            dk_ref[...] = dk_sc[...]
            dv_ref[...] = dv_sc[...]
            dc_ref[...] = dc_sc[...]

        pl.when(p == len(pairs) - 1)(chip_finish)

    qi = lambda p, it_, jt_: (it_[p], 0)
    kj = lambda p, it_, jt_: (jt_[p], 0)
    row_i = lambda p, it_, jt_: (0, it_[p])
    gs = pltpu.PrefetchScalarGridSpec(
        num_scalar_prefetch=2, grid=(len(pairs),),
        in_specs=[pl.BlockSpec((t, ATTN_W), qi), pl.BlockSpec((t, ATTN_W), kj),
                  pl.BlockSpec((ATTN_W, t), lambda p, it_, jt_: (0, jt_[p])),
                  pl.BlockSpec((t, ATTN_W), kj), pl.BlockSpec((t, ATTN_W), qi),
                  pl.BlockSpec((HEAD_ROWS, t), row_i), pl.BlockSpec((t, 128), kj),
                  pl.BlockSpec((HEAD_ROWS, t), row_i), pl.BlockSpec((HEAD_ROWS, t), row_i)] + [ANY] * nw,
        out_specs=[pl.BlockSpec((n, ATTN_W, t), lambda p, it_, jt_: (0, 0, 0)),
                   pl.BlockSpec((t, ATTN_W), kj), pl.BlockSpec((t, ATTN_W), kj),
                   pl.BlockSpec((t, 128), kj),
                   pl.BlockSpec((n, HEAD_ROWS, t), lambda p, it_, jt_: (0, 0, 0))] + [ANY] * nw,
        scratch_shapes=[pltpu.VMEM((t, ATTN_W), F32), pltpu.VMEM((t, ATTN_W), F32),
                        pltpu.VMEM((t, 128), F32), pltpu.SemaphoreType.DMA((nw, 3)), pltpu.SemaphoreType.DMA((nw, 3))])
    sd = jax.ShapeDtypeStruct
    out = pl.pallas_call(
        body, name="attn_bwd", grid_spec=gs,
        out_shape=[sd((n, ATTN_W, t), F32), sd((s, ATTN_W), F32), sd((s, ATTN_W), F32),
                   sd((s, 128), F32), sd((n, HEAD_ROWS, t), F32)] + [sd((3,) + a.shape[1:], a.dtype) for a in parts],
        compiler_params=_params(("arbitrary",)),
    )(it, jt, qs, k, k_t, v, do, c_rows, c_cols, lse, delta, *parts)
    return out[:5], out[5:]


def _forget_bwd(dc_rows, dc_cols, z_t, b_col):
    s = z_t.shape[1]
    nb = s // 128

    def body(dr_ref, dcc_ref, z_ref, b_ref, dz_ref, db_ref):
        lower = _tri(128, False)
        real = lax.broadcasted_iota(jnp.int32, (HEAD_ROWS, 128), 0) < N_HEADS

        tail = jnp.zeros((HEAD_ROWS, 1), F32)
        dbias = jnp.zeros((HEAD_ROWS, 1), F32)
        for m in range(nb):
            off = (nb - 1 - m) * 128
            dc = dr_ref[:, off:off + 128] + dcc_ref[off:off + 128, :].T[0:HEAD_ROWS, :]
            dlf = _dot_exact(dc, lower, 3) + tail
            dz = dlf * jax.nn.sigmoid(-(z_ref[0:HEAD_ROWS, off:off + 128] + b_ref[...]))
            dz = jnp.where(real, dz, 0.0)
            dz_ref[off:off + 128, :] = _rows_to_cols(dz)
            tail = tail + jnp.sum(dc, axis=1, keepdims=True)
            dbias = dbias + jnp.sum(dz, axis=1, keepdims=True)
        db_ref[...] = jnp.broadcast_to(dbias, db_ref.shape)

    return pl.pallas_call(
        body, name="forget_bwd",
        out_shape=[jax.ShapeDtypeStruct((s, 128), F32), jax.ShapeDtypeStruct((HEAD_ROWS, 128), F32)],
        compiler_params=_params())(dc_rows, dc_cols, z_t, b_col)


def _inproj_bwd(dz, gate_c, u, conv_w, dq, dk, dv, dzf, db, x, dx2, g_pre, w_t, tm):
    s = x.shape[0]
    nt = s // tm
    t = dq.shape[2]
    assert t % tm == 0 and dq.shape[:2] == (s // t, ATTN_W)
    per = t // tm

    def body(dz_ref, dzn_ref, c_ref, u_ref, cp_ref, up_ref, cw_ref, dq_ref, dk_ref, dv_ref, dzf_ref, db_ref,
             x_ref, dx2_ref, g_ref, w_ref, gx_ref, dp_ref, dg_ref, dcw_ref):
        i = pl.program_id(0)
        first = i == 0
        last = i == nt - 1
        dzv = dz_ref[...]
        row = lax.broadcasted_iota(jnp.int32, dzv.shape, 0)
        n0 = jnp.where(last, 0.0, dzn_ref[0:1, :])
        n1 = jnp.where(last, 0.0, dzn_ref[1:2, :])
        dz1 = jnp.where(row == tm - 1, n0, pltpu.roll(dzv, tm - 1, 0))
        dz2 = jnp.where(row == tm - 1, n1, jnp.where(row == tm - 2, n0, pltpu.roll(dzv, tm - 2, 0)))
        dcu = cw_ref[2:3, :] * dzv + cw_ref[1:2, :] * dz1 + cw_ref[0:1, :] * dz2
        cv = c_ref[...]
        uv = u_ref[...]
        cu = cv * uv
        cu1, cu2 = _shift_down(cu, (cp_ref, up_ref), first)

        @pl.when(first)
        def _():
            dcw_ref[...] = jnp.zeros_like(dcw_ref)

        dcw_ref[0:1, :] += jnp.sum(dzv * cu2, axis=0, keepdims=True)
        dcw_ref[1:2, :] += jnp.sum(dzv * cu1, axis=0, keepdims=True)
        dcw_ref[2:3, :] += jnp.sum(dzv * cu, axis=0, keepdims=True)

        dp_ref[:, 0:512] = (dq_ref[0].T * Q_SCALE).astype(MXU_DTYPE)
        dp_ref[:, 512:1024] = dk_ref[...].astype(MXU_DTYPE)
        dp_ref[:, 1024:OFF_F] = dv_ref[...].astype(MXU_DTYPE)
        dp_ref[:, OFF_F:OFF_B] = dzf_ref[...].astype(MXU_DTYPE)
        dp_ref[:, OFF_B:OFF_C] = db_ref[...].astype(MXU_DTYPE)
        dp_ref[:, OFF_C:OFF_U] = (dcu * uv).astype(MXU_DTYPE)
        dp_ref[:, OFF_U:IN_PAD] = (dcu * cv).astype(MXU_DTYPE)
        dh = _dot(dp_ref[...], w_ref[...])
        xv = x_ref[...]
        r1 = _rms(xv)
        nx = xv * r1
        _acc_rows(dg_ref, first, dh * nx)
        gx_ref[...] = dx2_ref[...] + _norm_bwd(dh, nx, r1, g_ref[...])

    prev = pl.BlockSpec((8, 512), lambda i: (jnp.maximum(i * (tm // 8) - 1, 0), 0))
    nxt = pl.BlockSpec((8, 512), lambda i: (jnp.minimum((i + 1) * (tm // 8), s // 8 - 1), 0))
    sd = jax.ShapeDtypeStruct
    return pl.pallas_call(
        body, name="inproj_bwd", grid=(nt,),
        in_specs=[_tok(tm, 512), nxt, _tok(tm, 512), _tok(tm, 512), prev, prev, _whole((3, 512)),
                  pl.BlockSpec((1, ATTN_W, tm), lambda i: (i // per, 0, i % per)), _tok(tm, 512), _tok(tm, 512),
                  _tok(tm, 128),
                  _tok(tm, 512),
                  _tok(tm, D_MODEL), _tok(tm, D_MODEL), _whole((1, D_MODEL)), _whole((IN_PAD, D_MODEL), single=True)],
        out_specs=[_tok(tm, D_MODEL), _tok(tm, IN_PAD), _whole((1, D_MODEL)), _whole((8, 512))],
        out_shape=[sd((s, D_MODEL), F32), sd((s, IN_PAD), MXU_DTYPE), sd((1, D_MODEL), F32), sd((8, 512), F32)],
        compiler_params=_params(("arbitrary",)),
    )(dz, dz, gate_c, u, gate_c, u, conv_w, dq, dk, dv, dzf, db, x, dx2, g_pre, w_t)


def _tile(s, want):
    return want if s % want == 0 else s


def _halves(a):
    return a.reshape(2, a.shape[0] // 2, a.shape[1])


def _device_step(x, target, w, mom1, mom2, w_in_t, m_in_t, v_in_t, c_idx, me_idx):
    s = x.shape[0]
    tm = _tile(s, 512)
    tf = _tile(s, 256)
    ta = _tile(s, 512)
    tkk = _tile(s, 2048)
    gidx = np.arange(512) // HEAD_DIM
    gmat = jnp.asarray(gidx[:, None] == gidx[None, :], MXU_DTYPE)
    sel = jnp.asarray(gidx[:, None] == np.arange(128)[None, :], MXU_DTYPE)
    g_mix_pre, g_mix_post, g_ffn_pre, g_ffn_post = w["g_mix_pre"], w["g_mix_post"], w["g_ffn_pre"], w["g_ffn_post"]
    g_attn, g_conv, b_forget = w["g_attn_out"], w["g_conv_out"], w["b_forget"]
    shard = {n: _halves(w[n][0].astype(MXU_DTYPE)) for n in BIG[1:]}
    piece_rows = IN_W // N_CHIPS

    g_in, conv_all = _gather_weights([w_in_t.reshape(piece_rows, D_MODEL).astype(MXU_DTYPE)], w["conv_w"][0])
    w_rows = g_in.reshape(IN_W, D_MODEL)
    w_t = jnp.concatenate([w_rows[:OFF_F + N_HEADS], jnp.zeros((OFF_B - OFF_F - N_HEADS, D_MODEL), MXU_DTYPE),
                           w_rows[OFF_F + N_HEADS:]], axis=0)
    conv_w = jnp.transpose(conv_all, (1, 0, 2)).reshape(3, CONV_W)

    h1, qs, k, v, k_t, v_t, z_t, gate_b, gate_c, u = _inproj_fwd(x, g_mix_pre, w_t, tm)
    b_col = jnp.pad(jnp.transpose(b_forget), ((0, HEAD_ROWS - N_HEADS), (0, 0)))
    c_rows, c_cols, q_bias, k_bias = _forget_fwd(z_t, b_col)
    o_attn, lse, (g_out, g_gu, g_dn) = _attn_fwd(qs, k, v_t, q_bias, k_bias, ta,
                                                 [shard["w_out"], shard["w_gate_up"], shard["w_down"]])
    w_out = g_out.reshape(D_MODEL, D_MODEL)
    w_gu = g_gu.reshape(N_CHIPS, D_MODEL, FF_PIECE)
    w_dn = g_dn.reshape(2, FF_PIECE, D_MODEL)
    x2, merged, y, z = _mixer_fwd(x, o_attn, gate_b, gate_c, u, conv_w, g_attn, g_conv, w_out, g_mix_post, gmat, tm)
    h2, g, up, a, ff, dout, loss_acc = _ffn_fwd(x2, target, g_ffn_pre, w_gu, w_dn, g_ffn_post, tf)

    dx2, dff, dgu, dg_ffn_post, dg_ffn_pre = _ffn_bwd(dout, ff, x2, g, up, g_ffn_post, g_ffn_pre, w_gu, w_dn, tf)
    dw_dn = _tn_matmul(a, dff, FF_PIECE, 1024, tkk, "dw_down").reshape(N_CHIPS, 2, D_FF // (2 * N_CHIPS), D_MODEL)
    dw_gu = _tn_matmul(h2, dgu, 1024, FF_PIECE, tkk, "dw_gate_up").reshape(2, D_MODEL // 2, 2 * D_FF)
    (dy, d_o, d_b, dz, delta, dg_mix_post, dg_attn, dg_conv), (a_gu, a_dn) = _mixer_bwd(
        dx2, y, o_attn, gate_b, z, g_mix_post, g_attn, g_conv, w_out, gmat, sel, tm, [dw_gu, dw_dn], ["cols", "rows"])
    dw_out = _tn_matmul(merged, dy, 1024, 1024, tkk, "dw_out")
    place = jnp.concatenate([c_idx, me_idx])
    sum_gu = _pair_sum(place, dw_gu, "cols", a_gu, "pair_sum_w_gate_up")
    sum_dn = _pair_sum(place, dw_dn, "rows", a_dn, "pair_sum_w_down")
    (dq_t, dk, dv, dc_cols, dcq), (r_gu, r_dn) = _attn_bwd(
        qs, k, k_t, v, d_o, c_rows, c_cols, lse, delta, ta, [sum_gu[1], sum_dn[1]])
    dc_rows = jnp.transpose(dcq, (1, 0, 2)).reshape(HEAD_ROWS, s)
    dzf, db_f = _forget_bwd(dc_rows, dc_cols, z_t, b_col)
    grad_x, dproj, dg_mix_pre, dcw = _inproj_bwd(dz, gate_c, u, conv_w, dq_t, dk, dv, dzf, d_b,
                                                 x, dx2, g_mix_pre, w_t, tm)
    dw_t = _tn_matmul(dproj, h1, 640, 1024, tkk, "dw_in")
    dw_in = jnp.concatenate([dw_t[:OFF_F + N_HEADS], dw_t[OFF_B:]], axis=0).reshape(N_CHIPS, piece_rows, D_MODEL)
    dw_out = dw_out.reshape(N_CHIPS, 2, D_MODEL // (2 * N_CHIPS), D_MODEL)

    a_in, a_out = _pair_exchange([dw_in, dw_out], ["lanes", "rows"])
    sum_in = _pair_sum(place, dw_in, "lanes", a_in, "pair_sum_w_in")
    sum_out = _pair_sum(place, dw_out, "rows", a_out, "pair_sum_w_out")
    small = dict(b_forget=db_f[:N_HEADS, 0], g_attn_out=dg_attn, g_conv_out=dg_conv, g_mix_pre=dg_mix_pre,
                 g_mix_post=dg_mix_post, g_ffn_pre=dg_ffn_pre, g_ffn_post=dg_ffn_post)
    (r_in, r_out), small_all = _chip_exchange([sum_in[1], sum_out[1]], _pack_small(small, dcw[:3], loss_acc[0, 0]))
    totals = [_chip_sum(sb[0], r, "chip_sum_" + n)
              for n, sb, r in zip(BIG, (sum_in, sum_out, sum_gu, sum_dn), (r_in, r_out, r_gu, r_dn))]
    shared = _pair_share(totals, "pair_share")
    new = {"w_in": _adamw_lanes(c_idx, w_in_t, totals[0], shared[0], m_in_t, v_in_t, "adamw_w_in")}
    for n, mine, theirs in list(zip(BIG, totals, shared))[1:]:
        new[n] = _adamw(c_idx, w[n][0], mine, theirs, mom1[n][0], mom2[n][0], 2, "adamw_" + n)
    return grad_x, new, small_all


BIG = ("w_in", "w_out", "w_gate_up", "w_down")
ANY = pl.BlockSpec(memory_space=pl.ANY)


def _place():
    x, y, c = lax.axis_index("x"), lax.axis_index("y"), lax.axis_index("c")
    others = [(1 - x, y), (x, 1 - y), (1 - x, 1 - y)]
    return x, y, c, 2 * x + y, others, [2 * px + py for px, py in others]


def _remote(src, dst, send, recv, dev):
    return pltpu.make_async_remote_copy(src_ref=src, dst_ref=dst, send_sem=send, recv_sem=recv,
                                        device_id=dev, device_id_type=MESH_ID)


def _gather_stages(sh, outs, send, recv):
    x, y, c, me, others, chips = _place()
    sib = (x, y, 1 - c)
    every = [(w, kk) for w in range(len(sh)) for kk in range(3)]

    def half_of(ref, half, piece=None):
        ref = ref if piece is None else ref.at[piece]
        if len(ref.shape) == 3:
            return ref.at[half]
        hc = ref.shape[1] // 2
        return ref.at[:, pl.ds(pl.multiple_of(half * hc, 128), hc)]

    def first(w, kk):
        return _remote(half_of(sh[w], c), half_of(outs[w], c, me), send.at[w, kk], recv.at[w, kk], (*others[kk], c))

    def landed(w, kk):
        r = half_of(outs[w], c, chips[kk])
        return _remote(r, r, send.at[w, kk], recv.at[w, kk], (*others[kk], c))

    def onward(w, kk, half):
        r = half_of(outs[w], half, chips[kk])
        return _remote(r, r, send.at[w, 3 + kk], recv.at[w, 3 + kk], sib)

    def start():
        for w, kk in every:
            first(w, kk).start()

    def forward():
        for w, kk in every:
            landed(w, kk).wait_recv()
            onward(w, kk, c).start()

    def finish():
        for w, kk in every:
            onward(w, kk, 1 - c).wait_recv()
        for w, kk in every:
            first(w, kk).wait_send()
            onward(w, kk, c).wait_send()

    return start, forward, finish


def _pair_piece(ref, kind, p, half):
    if kind == "rows":
        return ref.at[p, half]
    if kind == "lanes":
        hc = ref.shape[2] // 2
        return ref.at[p, :, pl.ds(pl.multiple_of(half * hc, 128), hc)]
    cols = ref.shape[2] // N_CHIPS
    return ref.at[half, :, pl.ds(p * cols, cols)]


def _pair_stages(g, kinds, a, send, recv):
    x, y, c, _, _, _ = _place()
    copies = [_remote(_pair_piece(g[w], kinds[w], p, 1 - c), a[w].at[p], send.at[w, p], recv.at[w, p], (x, y, 1 - c))
              for w in range(len(g)) for p in range(N_CHIPS)]

    def start():
        for cp in copies:
            cp.start()

    def finish():
        for cp in copies:
            cp.wait()

    return start, finish


def _chip_stages(pb, rcv, send, recv):
    x, y, c, _, others, chips = _place()
    copies = [_remote(pb[w].at[chips[kk]], rcv[w].at[kk], send.at[w, kk], recv.at[w, kk], (*others[kk], c))
              for w in range(len(pb)) for kk in range(3)]

    def start():
        for cp in copies:
            cp.start()

    def finish():
        for cp in copies:
            cp.wait()

    return start, finish


def _gather_weights(shards, conv_w):
    n = len(shards)

    def body(*refs):
        sh, cw, outs, cwo = refs[:n], refs[n], refs[n + 1:2 * n + 1], refs[2 * n + 1]
        send, recv = refs[2 * n + 2:]
        x, y, c, me, others, chips = _place()
        start, forward, finish = _gather_stages(sh, outs, send, recv)
        start()
        small = [_remote(cw, cwo.at[me], send.at[n, kk], recv.at[n, kk], (*others[kk], c)) for kk in range(3)]
        for cp in small:
            cp.start()
        forward()
        for kk in range(3):
            _remote(cw, cwo.at[chips[kk]], send.at[n, kk], recv.at[n, kk], (*others[kk], c)).wait_recv()
        finish()
        for cp in small:
            cp.wait_send()

    out_shape = [jax.ShapeDtypeStruct((N_CHIPS,) + s.shape, s.dtype) for s in shards]
    out_shape.append(jax.ShapeDtypeStruct((N_CHIPS,) + conv_w.shape, conv_w.dtype))
    got = pl.pallas_call(
        body, name="gather_weights", in_specs=[ANY] * (n + 1), out_specs=[ANY] * (n + 1), out_shape=out_shape,
        scratch_shapes=[pltpu.SemaphoreType.DMA((n + 1, 6)), pltpu.SemaphoreType.DMA((n + 1, 6))],
    )(*shards, conv_w)
    me = 2 * lax.axis_index("x") + lax.axis_index("y")
    return [lax.dynamic_update_index_in_dim(g, own, me, 0) for g, own in zip(got, list(shards) + [conv_w])]


def _taken_shape(g, kind):
    if kind == "rows":
        return (N_CHIPS,) + g.shape[2:]
    if kind == "lanes":
        return g.shape[:2] + (g.shape[2] // 2,)
    return (N_CHIPS, g.shape[1], g.shape[2] // N_CHIPS)


def _pair_sum(place, g, kind, a, name):
    _, half, cols = a.shape
    if kind == "rows":
        mine = pl.BlockSpec((1, 1, half, cols), lambda p, pr: (p, pr[0], 0, 0))
    elif kind == "lanes":
        mine = pl.BlockSpec((1, half, cols), lambda p, pr: (p, 0, pr[0]))
    else:
        mine = pl.BlockSpec((1, half, cols), lambda p, pr: (pr[0], 0, p))

    def body(place_ref, g_ref, a_ref, own_ref, pb_ref):
        tot = (g_ref[0, 0] if kind == "rows" else g_ref[0]) + a_ref[0]
        pb_ref[0] = tot.astype(BF16)

        @pl.when(pl.program_id(0) == place_ref[1])
        def _():
            own_ref[...] = tot

    gs = pltpu.PrefetchScalarGridSpec(
        num_scalar_prefetch=1, grid=(N_CHIPS,),
        in_specs=[mine,
                  pl.BlockSpec((1, half, cols), lambda p, pr: (p, 0, 0))],
        out_specs=[pl.BlockSpec((half, cols), lambda p, pr: (0, 0)),
                   pl.BlockSpec((1, half, cols), lambda p, pr: (p, 0, 0))])
    return pl.pallas_call(
        body, name=name, grid_spec=gs,
        out_shape=[jax.ShapeDtypeStruct((half, cols), F32), jax.ShapeDtypeStruct((N_CHIPS, half, cols), BF16)],
        compiler_params=_params(("arbitrary",)),
    )(place, g, a)


def _chip_sum(own, rcv, name):
    half, cols = own.shape

    def body(o_ref, r_ref, t_ref):
        t_ref[...] = ((o_ref[...] + r_ref[0].astype(F32)) + r_ref[1].astype(F32)) + r_ref[2].astype(F32)

    return pl.pallas_call(
        body, name=name, grid=(1,),
        in_specs=[pl.BlockSpec((half, cols), lambda i: (0, 0)), pl.BlockSpec((3, half, cols), lambda i: (0, 0, 0))],
        out_specs=pl.BlockSpec((half, cols), lambda i: (0, 0)),
        out_shape=jax.ShapeDtypeStruct((half, cols), F32), compiler_params=_params(("arbitrary",)),
    )(own, rcv)


def _small_stages(sm, smg, send, recv):
    x, y, c, _, _, _ = _place()

    def peer(r):
        return (1 - x if r & 4 else x, 1 - y if r & 2 else y, 1 - c if r & 1 else c)

    mine = 4 * x + 2 * y + c
    copies = [_remote(sm, smg.at[mine], send.at[r - 1], recv.at[r - 1], peer(r)) for r in range(1, 8)]

    def start():
        for cp in copies:
            cp.start()

    def finish():
        for r in range(1, 8):
            px, py, pc = peer(r)
            _remote(sm, smg.at[4 * px + 2 * py + pc], send.at[r - 1], recv.at[r - 1], (px, py, pc)).wait_recv()
        for cp in copies:
            cp.wait_send()

    return start, finish


def _pair_exchange(grads, kinds):
    n = len(grads)

    def body(*refs):
        start, finish = _pair_stages(refs[:n], kinds, refs[n:2 * n], *refs[2 * n:])
        start()
        finish()

    return pl.pallas_call(
        body, name="pair_exchange", in_specs=[ANY] * n, out_specs=[ANY] * n,
        out_shape=[jax.ShapeDtypeStruct(_taken_shape(g, kd), g.dtype) for g, kd in zip(grads, kinds)],
        scratch_shapes=[pltpu.SemaphoreType.DMA((n, N_CHIPS)), pltpu.SemaphoreType.DMA((n, N_CHIPS))],
    )(*grads)


def _chip_exchange(parts, small):
    n = len(parts)

    def body(*refs):
        pb, sm, rcv, smg = refs[:n], refs[n], refs[n + 1:2 * n + 1], refs[2 * n + 1]
        send, recv, ssend, srecv = refs[2 * n + 2:]
        chip_start, chip_finish = _chip_stages(pb, rcv, send, recv)
        small_start, small_finish = _small_stages(sm, smg, ssend, srecv)
        chip_start()
        small_start()
        chip_finish()
        small_finish()

    out_shape = [jax.ShapeDtypeStruct((3,) + p.shape[1:], p.dtype) for p in parts]
    out_shape.append(jax.ShapeDtypeStruct((8,) + small.shape, small.dtype))
    *arrived, small_land = pl.pallas_call(
        body, name="chip_exchange", in_specs=[ANY] * (n + 1), out_specs=[ANY] * (n + 1), out_shape=out_shape,
        scratch_shapes=[pltpu.SemaphoreType.DMA((n, 3)), pltpu.SemaphoreType.DMA((n, 3)),
                        pltpu.SemaphoreType.DMA((7,)), pltpu.SemaphoreType.DMA((7,))],
    )(*parts, small)
    mine = 4 * lax.axis_index("x") + 2 * lax.axis_index("y") + lax.axis_index("c")
    return arrived, lax.dynamic_update_index_in_dim(small_land, small, mine, 0)


def _pair_share(totals, name):
    n = len(totals)

    def body(*refs):
        t, g = refs[:n], refs[n:2 * n]
        send, recv = refs[2 * n:]
        x, y, c, _, _, _ = _place()
        copies = [_remote(t[w], g[w], send.at[w], recv.at[w], (x, y, 1 - c)) for w in range(n)]
        for cp in copies:
            cp.start()
        for cp in copies:
            cp.wait()

    return pl.pallas_call(
        body, name=name, in_specs=[ANY] * n, out_specs=[ANY] * n,
        out_shape=[jax.ShapeDtypeStruct(t.shape, t.dtype) for t in totals],
        scratch_shapes=[pltpu.SemaphoreType.DMA((n,)), pltpu.SemaphoreType.DMA((n,))],
    )(*totals)


def _adamw_math(w, g, m, v):
    m = ADAM_B1 * m + (1.0 - ADAM_B1) * g
    v = ADAM_B2 * v + (1.0 - ADAM_B2) * (g * g)
    m_hat = m / (1.0 - ADAM_B1 ** ADAM_STEP)
    v_hat = v / (1.0 - ADAM_B2 ** ADAM_STEP)
    delta = -ADAM_LR * (m_hat / (jnp.sqrt(v_hat) + ADAM_EPS) + ADAM_WD * w)
    return delta, m, v


def _adamw(c_idx, w, mine, theirs, m, v, nb, name):
    rows, cols = w.shape
    tr = rows // (2 * nb)

    def body(c_ref, w_ref, a_ref, b_ref, m_ref, v_ref, g_ref, d_ref, nm_ref, nv_ref):
        g = jnp.where(pl.program_id(0) == c_ref[0], a_ref[...], b_ref[...])
        g_ref[...] = g
        d_ref[...], nm_ref[...], nv_ref[...] = _adamw_math(w_ref[...], g, m_ref[...], v_ref[...])

    full = pl.BlockSpec((tr, cols), lambda hh, i, cr: (hh * nb + i, 0))
    half = pl.BlockSpec((tr, cols), lambda hh, i, cr: (i, 0))
    gs = pltpu.PrefetchScalarGridSpec(num_scalar_prefetch=1, grid=(2, nb), in_specs=[full, half, half, full, full],
                                      out_specs=[full] * 4)
    return pl.pallas_call(
        body, name=name, grid_spec=gs, out_shape=[jax.ShapeDtypeStruct((rows, cols), F32)] * 4,
        compiler_params=_params(("arbitrary", "arbitrary")),
    )(c_idx, w, mine, theirs, m, v)


def _adamw_lanes(c_idx, w, mine, theirs, m, v, name):
    rows, _, cols = w.shape
    hc = cols // 2

    def body(c_ref, w_ref, a_ref, b_ref, m_ref, v_ref, g_ref, d_ref, nm_ref, nv_ref):
        g = jnp.where(pl.program_id(0) == c_ref[0], a_ref[...], b_ref[...])
        g_ref[:, 0, :] = g
        d_ref[:, 0, :], nm_ref[:, 0, :], nv_ref[:, 0, :] = _adamw_math(w_ref[:, 0, :], g, m_ref[:, 0, :], v_ref[:, 0, :])

    full = pl.BlockSpec((rows, 1, hc), lambda hh, cr: (0, 0, hh))
    half = pl.BlockSpec((rows, hc), lambda hh, cr: (0, 0))
    gs = pltpu.PrefetchScalarGridSpec(num_scalar_prefetch=1, grid=(2,), in_specs=[full, half, half, full, full],
                                      out_specs=[full] * 4)
    return pl.pallas_call(
        body, name=name, grid_spec=gs, out_shape=[jax.ShapeDtypeStruct((rows, 1, cols), F32)] * 4,
        compiler_params=_params(("arbitrary",)),
    )(c_idx, w, mine, theirs, m, v)


SMALL = ("g_mix_pre", "g_mix_post", "g_ffn_pre", "g_ffn_post")
SMALL_ALL = SMALL + ("g_attn_out", "g_conv_out", "conv_w", "b_forget")
SMALL_AT = {"g_mix_pre": (0, 0, 1024), "g_mix_post": (1, 0, 1024), "g_ffn_pre": (2, 0, 1024),
            "g_ffn_post": (3, 0, 1024), "g_attn_out": (4, 0, 512), "g_conv_out": (4, 512, 512),
            "b_forget": (7, 0, N_HEADS)}
CONV_AT = ((5, 0), (5, 512), (6, 0))
LOSS_AT = (6, 512)


def _pack_small(t, conv_full, loss_sum):
    conv = jnp.concatenate([conv_full.reshape(1, 3 * CONV_W), loss_sum.reshape(1, 1),
                            jnp.zeros((1, 2048 - 3 * CONV_W - 1), F32)], axis=1).reshape(2, 1024)
    return jnp.concatenate([t[n].reshape(1, 1024) for n in SMALL]
                           + [jnp.concatenate([t["g_attn_out"].reshape(1, 512), t["g_conv_out"].reshape(1, 512)], axis=1),
                              conv, jnp.pad(t["b_forget"].reshape(1, N_HEADS), ((0, 0), (0, 1024 - N_HEADS)))], axis=0)


def _small_update(me_idx, gathered, w, m, v):
    def body(me_ref, gg_ref, *refs):
        k = len(SMALL_ALL)
        w_refs, m_refs, v_refs = refs[:k], refs[k:2 * k], refs[2 * k:3 * k]
        loss_ref = refs[3 * k]
        outs = refs[3 * k + 1:3 * k + 1 + 4 * k]
        sums = refs[-1]
        g = gg_ref[0]
        for dev in range(1, 8):
            g = g + gg_ref[dev]
        sums[...] = g
        loss_ref[...] = sums[LOSS_AT[0]:LOSS_AT[0] + 1, LOSS_AT[1]:LOSS_AT[1] + 1]
        mine = pl.multiple_of(me_ref[0] * 128, 128)
        for idx, name in enumerate(SMALL_ALL):
            g_ref, d_ref, nm_ref, nv_ref = outs[4 * idx:4 * idx + 4]
            if name == "conv_w":
                for r, (row, lo) in enumerate(CONV_AT):
                    gr = sums[row:row + 1, pl.ds(lo + mine, 128)]
                    g_ref[0, r:r + 1, :] = gr
                    d_ref[0, r:r + 1, :], nm_ref[0, r:r + 1, :], nv_ref[0, r:r + 1, :] = _adamw_math(
                        w_refs[idx][0, r:r + 1, :], gr, m_refs[idx][0, r:r + 1, :], v_refs[idx][0, r:r + 1, :])
            else:
                row, lo, n = SMALL_AT[name]
                gr = sums[row:row + 1, lo:lo + n]
                g_ref[...] = gr
                d_ref[...], nm_ref[...], nv_ref[...] = _adamw_math(w_refs[idx][...], gr, m_refs[idx][...],
                                                                    v_refs[idx][...])

    def whole(a):
        nd = a.ndim
        return pl.BlockSpec(a.shape, lambda i, mr: (0,) * nd)

    ins = [t[n] for t in (w, m, v) for n in SMALL_ALL]
    out_shape = [jax.ShapeDtypeStruct((1, 1), F32)]
    for n in SMALL_ALL:
        out_shape += [jax.ShapeDtypeStruct(w[n].shape, F32)] * 4
    gs = pltpu.PrefetchScalarGridSpec(
        num_scalar_prefetch=1, grid=(1,), in_specs=[whole(gathered)] + [whole(a) for a in ins],
        out_specs=[whole(o) for o in out_shape], scratch_shapes=[pltpu.VMEM((8, 1024), F32)])
    out = pl.pallas_call(body, name="small_update", grid_spec=gs, out_shape=out_shape,
                         compiler_params=_params(("arbitrary",)))(me_idx, gathered, *ins)
    return out[0], {n: out[1 + 4 * i:5 + 4 * i] for i, n in enumerate(SMALL_ALL)}


def kernel(x, w_in, b_forget, conv_w, g_attn_out, g_conv_out, w_out, g_mix_pre, g_mix_post, w_gate_up, w_down, g_ffn_pre, g_ffn_post, loss_target, m_w_in, m_b_forget, m_conv_w, m_g_attn_out, m_g_conv_out, m_w_out, m_g_mix_pre, m_g_mix_post, m_w_gate_up, m_w_down, m_g_ffn_pre, m_g_ffn_post, v_w_in, v_b_forget, v_conv_w, v_g_attn_out, v_g_conv_out, v_w_out, v_g_mix_pre, v_g_mix_post, v_w_gate_up, v_w_down, v_g_ffn_pre, v_g_ffn_post):
    w = dict(w_in=w_in, b_forget=b_forget, conv_w=conv_w, g_attn_out=g_attn_out, g_conv_out=g_conv_out, w_out=w_out,
             g_mix_pre=g_mix_pre, g_mix_post=g_mix_post, w_gate_up=w_gate_up, w_down=w_down, g_ffn_pre=g_ffn_pre,
             g_ffn_post=g_ffn_post)
    m = dict(w_in=m_w_in, b_forget=m_b_forget, conv_w=m_conv_w, g_attn_out=m_g_attn_out, g_conv_out=m_g_conv_out,
             w_out=m_w_out, g_mix_pre=m_g_mix_pre, g_mix_post=m_g_mix_post, w_gate_up=m_w_gate_up, w_down=m_w_down,
             g_ffn_pre=m_g_ffn_pre, g_ffn_post=m_g_ffn_post)
    v = dict(w_in=v_w_in, b_forget=v_b_forget, conv_w=v_conv_w, g_attn_out=v_g_attn_out, g_conv_out=v_g_conv_out,
             w_out=v_w_out, g_mix_pre=v_g_mix_pre, g_mix_post=v_g_mix_post, w_gate_up=v_w_gate_up, w_down=v_w_down,
             g_ffn_pre=v_g_ffn_pre, g_ffn_post=v_g_ffn_post)
    cx, cy, cc = lax.axis_index("x"), lax.axis_index("y"), lax.axis_index("c")
    me = 2 * cx + cy
    c_idx = cc.astype(jnp.int32).reshape(1)
    me_idx = me.astype(jnp.int32).reshape(1)

    stored = lambda a: jnp.transpose(a, (2, 0, 1))
    grad_x, big, small_all = _device_step(x[0], loss_target[0], w, m, v, stored(w_in), stored(m_w_in),
                                          stored(v_w_in), c_idx, me_idx)
    gsum, delta, new_m, new_v = {}, {}, {}, {}
    for n in BIG:
        back = (lambda r: jnp.transpose(r, (1, 2, 0))) if n == "w_in" else (lambda r: r[None])
        gsum[n], delta[n], new_m[n], new_v[n] = [back(r) for r in big[n]]
    loss_sum, small_new = _small_update(me_idx, small_all, w, m, v)
    for n in SMALL_ALL:
        gsum[n], delta[n], new_m[n], new_v[n] = small_new[n]
    loss = 0.5 * loss_sum[0, 0]

    order = ("w_in", "b_forget", "conv_w", "g_attn_out", "g_conv_out", "w_out", "g_mix_pre", "g_mix_post",
             "w_gate_up", "w_down", "g_ffn_pre", "g_ffn_post")
    return (loss, grad_x[None], *[gsum[n] for n in order], *[delta[n] for n in order],
            *[new_m[n] for n in order], *[new_v[n] for n in order])
```

```python
import functools

import jax
import jax.numpy as jnp
import numpy as np
from jax import lax
from jax.experimental import pallas as pl
from jax.experimental.pallas import tpu as pltpu

F32 = jnp.float32
BF16 = jnp.bfloat16
MXU_DTYPE = jnp.bfloat16

D_MODEL = 1024
HEAD_DIM = 64
N_HEADS = 8
ATTN_W = 512
CONV_W = 512
D_FF = 2816
FF_PIECE = 1408
EPS = 1e-6
Q_SCALE = HEAD_DIM ** -0.5

OFF_F = 1536
OFF_B = 1664
OFF_C = 2176
OFF_U = 2688
IN_PAD = 3200
IN_W = 3080
N_CHIPS = 4

ADAM_LR = 0.001
ADAM_B1 = 0.9
ADAM_B2 = 0.999
ADAM_EPS = 1e-08
ADAM_WD = 0.01
ADAM_STEP = 10

VMEM_LIMIT_V7X = 56 * 1024 * 1024
MESH_ID = pl.DeviceIdType.MESH


def _params(sem=None, vmem=VMEM_LIMIT_V7X):
    kw = {"vmem_limit_bytes": vmem}
    if sem is not None:
        kw["dimension_semantics"] = sem
    return pltpu.CompilerParams(**kw)


def _dot(a, b):
    return jnp.dot(a, b, preferred_element_type=F32)


def _dot_nt(a, b):
    return lax.dot_general(a, b, (((1,), (1,)), ((), ())), preferred_element_type=F32)


def _dot_exact(x, ones, parts):
    if ones.dtype == F32:
        return _dot(x, ones)
    acc = None
    rem = x
    for _ in range(parts):
        piece = rem.astype(BF16)
        rem = rem - piece.astype(F32)
        term = _dot(piece, ones)
        acc = term if acc is None else acc + term
    return acc


def _rms(v):
    return lax.rsqrt(jnp.mean(v * v, axis=-1, keepdims=True) + EPS)


def _tok(tm, w):
    return pl.BlockSpec((tm, w), lambda i: (i, 0))


def _whole(shape, single=False):
    nd = len(shape)
    if single:
        return pl.BlockSpec(shape, lambda i: (0,) * nd, pipeline_mode=pl.Buffered(1))
    return pl.BlockSpec(shape, lambda i: (0,) * nd)


def _feat(rows, tm):
    return pl.BlockSpec((rows, tm), lambda i: (0, i))


def _inproj_fwd(x, g_pre, w_t, tm):
    s = x.shape[0]

    def body(x_ref, g_ref, w_ref, h_ref, q_ref, k_ref, v_ref, kt_ref, vt_ref, zt_ref, b_ref, c_ref, u_ref):
        xv = x_ref[...]
        h = ((xv * _rms(xv)) * g_ref[...]).astype(MXU_DTYPE)
        h_ref[...] = h

        def proj(lo, hi):
            return _dot_nt(h, w_ref[lo:hi, :])

        q_ref[...] = (proj(0, 512) * Q_SCALE).astype(MXU_DTYPE)
        kt = _dot_nt(w_ref[512:1024, :], h)
        vt = _dot_nt(w_ref[1024:OFF_F, :], h)
        kt_ref[...] = kt.astype(MXU_DTYPE)
        vt_ref[...] = vt.astype(MXU_DTYPE)
        k_ref[...] = kt.T.astype(MXU_DTYPE)
        v_ref[...] = vt.T.astype(MXU_DTYPE)
        zt_ref[...] = _dot_nt(w_ref[OFF_F:OFF_B, :], h)
        b_ref[...] = proj(OFF_B, OFF_C)
        c_ref[...] = proj(OFF_C, OFF_U)
        u_ref[...] = proj(OFF_U, IN_PAD)

    sd = jax.ShapeDtypeStruct
    return pl.pallas_call(
        body, name="inproj_fwd", grid=(s // tm,),
        in_specs=[_tok(tm, D_MODEL), _whole((1, D_MODEL)), _whole((IN_PAD, D_MODEL), single=True)],
        out_specs=[_tok(tm, D_MODEL), _tok(tm, 512), _tok(tm, 512), _tok(tm, 512), _feat(512, tm), _feat(512, tm),
                   _feat(128, tm), _tok(tm, 512), _tok(tm, 512), _tok(tm, 512)],
        out_shape=[sd((s, D_MODEL), MXU_DTYPE), sd((s, 512), MXU_DTYPE), sd((s, 512), MXU_DTYPE),
                   sd((s, 512), MXU_DTYPE), sd((512, s), MXU_DTYPE), sd((512, s), MXU_DTYPE), sd((128, s), F32),
                   sd((s, 512), F32), sd((s, 512), F32), sd((s, 512), F32)],
        compiler_params=_params(("arbitrary",)),
    )(x, g_pre, w_t)


def _tri(n, upper):
    r = lax.broadcasted_iota(jnp.int32, (n, n), 0)
    c = lax.broadcasted_iota(jnp.int32, (n, n), 1)
    return ((r <= c) if upper else (r >= c)).astype(MXU_DTYPE)


HEAD_ROWS = 16


def _rows_to_cols(v):
    return jnp.concatenate([v, jnp.zeros((128 - HEAD_ROWS, 128), F32)], axis=0).T


BIAS_PARTS = 3


def _bias_placement():
    place_q = np.zeros((BIAS_PARTS, 128, ATTN_W), np.float32)
    place_k = np.zeros((BIAS_PARTS, 128, ATTN_W), np.float32)
    ones_q = np.zeros((1, ATTN_W), np.float32)
    ones_k = np.zeros((1, ATTN_W), np.float32)
    for h in range(N_HEADS):
        base = 2 * HEAD_DIM * (h // 2) + HEAD_DIM * (1 - h % 2)
        for part in range(BIAS_PARTS):
            place_q[part, h, base + part] = 1.0
            place_k[part, h, base + BIAS_PARTS + part] = -1.0
        ones_q[0, base + BIAS_PARTS:base + 2 * BIAS_PARTS] = 1.0
        ones_k[0, base:base + BIAS_PARTS] = 1.0
    return (jnp.asarray(place_q, MXU_DTYPE), jnp.asarray(place_k, MXU_DTYPE), jnp.asarray(ones_q), jnp.asarray(ones_k))


def _forget_fwd(z_t, b_col):
    s = z_t.shape[1]
    nb = s // 128

    def body(z_ref, b_ref, pq_ref, pk_ref, oq_ref, ok_ref, c_ref, cc_ref, qa_ref, ka_ref):
        upper = _tri(128, True)

        carry = jnp.zeros((HEAD_ROWS, 1), F32)
        for n in range(nb):
            off = n * 128
            lf = jax.nn.log_sigmoid(z_ref[0:HEAD_ROWS, off:off + 128] + b_ref[...])
            cs = _dot_exact(lf, upper, 3) + carry
            c_ref[:, off:off + 128] = cs
            cc_ref[off:off + 128, :] = _rows_to_cols(cs)
            carry = carry + jnp.sum(lf, axis=1, keepdims=True)

        rb = min(s, 512)
        for off in range(0, s, rb):
            qa = jnp.broadcast_to(oq_ref[...], (rb, ATTN_W))
            ka = jnp.broadcast_to(ok_ref[...], (rb, ATTN_W))
            rem = cc_ref[off:off + rb, :]
            for part in range(BIAS_PARTS):
                piece = rem.astype(MXU_DTYPE)
                rem = rem - piece.astype(F32)
                qa = qa + _dot(piece, pq_ref[part])
                ka = ka + _dot(piece, pk_ref[part])
            qa_ref[off:off + rb, :] = qa.astype(MXU_DTYPE)
            ka_ref[off:off + rb, :] = ka.astype(MXU_DTYPE)

    sd = jax.ShapeDtypeStruct
    return pl.pallas_call(body, name="forget_fwd",
                          out_shape=[sd((HEAD_ROWS, s), F32), sd((s, 128), F32), sd((s, ATTN_W), MXU_DTYPE),
                                     sd((s, ATTN_W), MXU_DTYPE)],
                          compiler_params=_params())(z_t, b_col, *_bias_placement())


def _aligned(start, size):
    return pl.ds(start if isinstance(start, int) else pl.multiple_of(start, size), size)


def _pair_lanes(pp):
    return _aligned(pp * 2 * HEAD_DIM, 2 * HEAD_DIM)


def _head_rows(h):
    return _aligned(h * HEAD_DIM, HEAD_DIM)


def _only_head(block, hb):
    lane = lax.broadcasted_iota(jnp.int32, block.shape, 1)
    return jnp.where((lane >= HEAD_DIM) if hb else (lane < HEAD_DIM), block, jnp.zeros_like(block))


def _head_col(cols, h):
    lane = lax.broadcasted_iota(jnp.int32, cols.shape, 1)
    return jnp.sum(jnp.where(lane == h, cols, 0.0), axis=1, keepdims=True)


def _other_head(block, other, hb):
    lane = lax.broadcasted_iota(jnp.int32, block.shape, 1)
    return jnp.where((lane >= HEAD_DIM) if hb else (lane < HEAD_DIM), block, other)


def _attn_fwd(qs, k, v_t, q_bias, k_bias, t, shards):
    s = qs.shape[0]
    n = s // t
    pairs = [(i, j) for i in range(n) for j in range(i + 1)]
    it = jnp.asarray(np.array([p[0] for p in pairs], np.int32))
    jt = jnp.asarray(np.array([p[1] for p in pairs], np.int32))
    nw = len(shards)
    last = len(pairs) - 1
    mid = (2 * len(pairs)) // 3

    def body(it_ref, jt_ref, q_ref, k_ref, vt_ref, qb_ref, kb_ref, *rest):
        sh, (o_ref, lse_ref), got = rest[:nw], rest[nw:nw + 2], rest[nw + 2:2 * nw + 2]
        m_sc, l_sc, acc_sc, send, recv = rest[2 * nw + 2:]
        p = pl.program_id(0)
        i = it_ref[p]
        j = jt_ref[p]
        gather_start, gather_forward, gather_finish = _gather_stages(sh, got, send, recv)
        pl.when(p == 0)(gather_start)
        if mid < last:
            pl.when(p == mid)(gather_forward)

        @pl.when(j == 0)
        def _():
            m_sc[...] = jnp.full_like(m_sc, -1e30)
            l_sc[...] = jnp.ones_like(l_sc)
            acc_sc[...] = jnp.zeros_like(acc_sc)

        def pair_step(pp, diagonal):
            lanes = _pair_lanes(pp)
            kp = k_ref[:, lanes]
            qp = q_ref[:, lanes]
            kb = kb_ref[:, lanes]
            qb = qb_ref[:, lanes]
            for hb in range(2):
                h = 2 * pp + hb
                row = pl.ds(h, 1)
                rows = _head_rows(h)
                st = _dot_nt(_other_head(kp, kb, hb), _other_head(qp, qb, hb))
                if diagonal:
                    kpos = lax.broadcasted_iota(jnp.int32, (t, t), 0)
                    qpos = lax.broadcasted_iota(jnp.int32, (t, t), 1)
                    st = jnp.where(kpos <= qpos, st, -1e30)
                m_prev = m_sc[row, :]
                m_new = jnp.maximum(m_prev, jnp.max(st, axis=0, keepdims=True))
                alpha = jnp.exp(m_prev - m_new)
                pt = jnp.exp(st - m_new)
                l_sc[row, :] = alpha * l_sc[row, :] + jnp.sum(pt, axis=0, keepdims=True)
                acc_sc[rows, :] = acc_sc[rows, :] * alpha + _dot(vt_ref[rows, :], pt.astype(MXU_DTYPE))
                m_sc[row, :] = m_new

        @pl.when(j < i)
        def _():
            for pp in range(N_HEADS // 2):
                pair_step(pp, False)

        @pl.when(j == i)
        def _():
            for pp in range(N_HEADS // 2):
                pair_step(pp, True)
                sub = lax.broadcasted_iota(jnp.int32, (2 * HEAD_DIM, t), 0)
                l_pair = jnp.where(sub < HEAD_DIM, l_sc[pl.ds(2 * pp, 1), :], l_sc[pl.ds(2 * pp + 1, 1), :])
                o_t = acc_sc[_aligned(pp * 2 * HEAD_DIM, 2 * HEAD_DIM), :] / l_pair
                o_ref[:, _pair_lanes(pp)] = o_t.T

            lse_ref[...] = m_sc[...] + jnp.log(l_sc[...])

        @pl.when(p == last)
        def _():
            if mid >= last:
                gather_forward()
            gather_finish()

    gs = pltpu.PrefetchScalarGridSpec(
        num_scalar_prefetch=2, grid=(len(pairs),),
        in_specs=[pl.BlockSpec((t, ATTN_W), lambda p, it_, jt_: (it_[p], 0)),
                  pl.BlockSpec((t, ATTN_W), lambda p, it_, jt_: (jt_[p], 0)),
                  pl.BlockSpec((ATTN_W, t), lambda p, it_, jt_: (0, jt_[p])),
                  pl.BlockSpec((t, ATTN_W), lambda p, it_, jt_: (it_[p], 0)),
                  pl.BlockSpec((t, ATTN_W), lambda p, it_, jt_: (jt_[p], 0))] + [ANY] * nw,
        out_specs=[pl.BlockSpec((t, ATTN_W), lambda p, it_, jt_: (it_[p], 0)),
                   pl.BlockSpec((HEAD_ROWS, t), lambda p, it_, jt_: (0, it_[p]))] + [ANY] * nw,
        scratch_shapes=[pltpu.VMEM((HEAD_ROWS, t), F32), pltpu.VMEM((HEAD_ROWS, t), F32), pltpu.VMEM((ATTN_W, t), F32),
                        pltpu.SemaphoreType.DMA((nw, 6)), pltpu.SemaphoreType.DMA((nw, 6))])
    o, lse, *got = pl.pallas_call(
        body, name="attn_fwd", grid_spec=gs,
        out_shape=[jax.ShapeDtypeStruct((s, ATTN_W), F32), jax.ShapeDtypeStruct((HEAD_ROWS, s), F32)]
        + [jax.ShapeDtypeStruct((N_CHIPS,) + a.shape, a.dtype) for a in shards],
        compiler_params=_params(("arbitrary",)),
    )(it, jt, qs, k, v_t, q_bias, k_bias, *shards)
    me = 2 * lax.axis_index("x") + lax.axis_index("y")
    return o, lse, [lax.dynamic_update_index_in_dim(g, own, me, 0) for g, own in zip(got, shards)]


def _shift_down(cur, prev_ref, first, cols=slice(None)):
    row = lax.broadcasted_iota(jnp.int32, cur.shape, 0)
    p7 = jnp.where(first, 0.0, prev_ref[0][7:8, cols] * prev_ref[1][7:8, cols])
    p6 = jnp.where(first, 0.0, prev_ref[0][6:7, cols] * prev_ref[1][6:7, cols])
    s1 = jnp.where(row == 0, p7, pltpu.roll(cur, 1, 0))
    s2 = jnp.where(row == 0, p6, jnp.where(row == 1, p7, pltpu.roll(cur, 2, 0)))
    return s1, s2


def _group_ms(v, gmat):
    return _dot_exact(v, gmat, 1) * (1.0 / HEAD_DIM)


def _mixer_fwd(x, o_attn, gate_b, gate_c, u, conv_w, g_attn, g_conv, w_out, g_post, gmat, tm):
    s = x.shape[0]

    def body(x_ref, o_ref, b_ref, c_ref, u_ref, cp_ref, up_ref, cw_ref, ga_ref, gc_ref, wo_ref, gp_ref, gm_ref,
             x2_ref, mg_ref, y_ref, z_ref):
        i = pl.program_id(0)
        gm = gm_ref[0:128, 0:128]
        for lo in range(0, ATTN_W, 128):
            cols = slice(lo, lo + 128)
            cu = c_ref[:, cols] * u_ref[:, cols]
            cu1, cu2 = _shift_down(cu, (cp_ref, up_ref), i == 0, cols)
            z = cw_ref[0:1, cols] * cu2 + cw_ref[1:2, cols] * cu1 + cw_ref[2:3, cols] * cu
            z_ref[:, cols] = z
            cv = b_ref[:, cols] * z
            ov = o_ref[:, cols]
            mg_ref[:, cols] = ((ov * lax.rsqrt(_group_ms(ov * ov, gm) + EPS)) * ga_ref[:, cols]).astype(MXU_DTYPE)
            mg_ref[:, ATTN_W + lo:ATTN_W + lo + 128] = (
                (cv * lax.rsqrt(_group_ms(cv * cv, gm) + EPS)) * gc_ref[:, cols]).astype(MXU_DTYPE)
        y = _dot(mg_ref[...], wo_ref[...])
        y_ref[...] = y
        x2_ref[...] = x_ref[...] + (y * _rms(y)) * gp_ref[...]

    halo = pl.BlockSpec((8, 512), lambda i: (jnp.maximum(i * (tm // 8) - 1, 0), 0))
    sd = jax.ShapeDtypeStruct
    return pl.pallas_call(
        body, name="mixer_fwd", grid=(s // tm,),
        in_specs=[_tok(tm, D_MODEL), _tok(tm, 512), _tok(tm, 512), _tok(tm, 512), _tok(tm, 512), halo, halo,
                  _whole((3, 512)), _whole((1, 512)), _whole((1, 512)), _whole((D_MODEL, D_MODEL), single=True),
                  _whole((1, D_MODEL)), _whole((512, 512))],
        out_specs=[_tok(tm, D_MODEL), _tok(tm, D_MODEL), _tok(tm, D_MODEL), _tok(tm, 512)],
        out_shape=[sd((s, D_MODEL), F32), sd((s, D_MODEL), MXU_DTYPE), sd((s, D_MODEL), F32), sd((s, 512), F32)],
        compiler_params=_params(("arbitrary",)),
    )(x, o_attn, gate_b, gate_c, u, gate_c, u, conv_w, g_attn, g_conv, w_out, g_post, gmat)


def _ffn_fwd(x2, target, g_pre, w_gu, w_dn, g_post, tm):
    s = x2.shape[0]

    def body(x_ref, t_ref, gpre_ref, wgu_ref, wdn_ref, gpost_ref,
             h_ref, g_ref, up_ref, a_ref, ff_ref, dout_ref, loss_ref):
        xv = x_ref[...]
        h = ((xv * _rms(xv)) * gpre_ref[...]).astype(MXU_DTYPE)
        h_ref[...] = h
        ff = jnp.zeros((tm, D_MODEL), F32)
        for j in range(2):
            cols = slice(j * FF_PIECE, (j + 1) * FF_PIECE)
            g = _dot(h, wgu_ref[j])
            up = _dot(h, wgu_ref[2 + j])
            a = ((g * jax.nn.sigmoid(g)) * up).astype(MXU_DTYPE)
            g_ref[:, cols] = g.astype(MXU_DTYPE)
            up_ref[:, cols] = up.astype(MXU_DTYPE)
            a_ref[:, cols] = a
            ff = ff + _dot(a, wdn_ref[j])
        ff_ref[...] = ff
        err = (xv + (ff * _rms(ff)) * gpost_ref[...]) - t_ref[...]
        dout_ref[...] = err * (1.0 / D_MODEL)
        part = jnp.sum(jnp.mean(err * err, axis=-1, keepdims=True), axis=0, keepdims=True)

        @pl.when(pl.program_id(0) == 0)
        def _():
            loss_ref[...] = jnp.zeros_like(loss_ref)

        loss_ref[...] += part

    sd = jax.ShapeDtypeStruct
    return pl.pallas_call(
        body, name="ffn_fwd", grid=(s // tm,),
        in_specs=[_tok(tm, D_MODEL), _tok(tm, D_MODEL), _whole((1, D_MODEL)),
                  _whole((4, D_MODEL, FF_PIECE), single=True), _whole((2, FF_PIECE, D_MODEL), single=True),
                  _whole((1, D_MODEL))],
        out_specs=[_tok(tm, D_MODEL), _tok(tm, D_FF), _tok(tm, D_FF), _tok(tm, D_FF), _tok(tm, D_MODEL),
                   _tok(tm, D_MODEL), _whole((8, 128))],
        out_shape=[sd((s, D_MODEL), MXU_DTYPE), sd((s, D_FF), MXU_DTYPE), sd((s, D_FF), MXU_DTYPE),
                   sd((s, D_FF), MXU_DTYPE), sd((s, D_MODEL), F32), sd((s, D_MODEL), F32), sd((8, 128), F32)],
        compiler_params=_params(("arbitrary",)),
    )(x2, target, g_pre, w_gu, w_dn, g_post)


def _norm_bwd(dy, normed, rinv, gain):
    t = dy * gain
    return rinv * (t - normed * jnp.mean(t * normed, axis=-1, keepdims=True))


def _acc_rows(ref, first, val):
    @pl.when(first)
    def _():
        ref[...] = jnp.zeros_like(ref)

    ref[...] += jnp.sum(val, axis=0, keepdims=True)


def _ffn_bwd(dout, ff, x2, g, up, g_post, g_pre, w_gu, w_dn, tm):
    s = x2.shape[0]

    def body(do_ref, ff_ref, x_ref, g_ref, up_ref, gpost_ref, gpre_ref, wgu_ref, wdn_ref,
             dx_ref, dff_ref, dgu_ref, dgpost_ref, dgpre_ref):
        first = pl.program_id(0) == 0
        ffv = ff_ref[...]
        rf = _rms(ffv)
        n = ffv * rf
        do = do_ref[...]
        _acc_rows(dgpost_ref, first, do * n)
        dff = _norm_bwd(do, n, rf, gpost_ref[...]).astype(MXU_DTYPE)
        dff_ref[...] = dff
        dh = jnp.zeros((tm, D_MODEL), F32)
        for j in range(2):
            cols = slice(j * FF_PIECE, (j + 1) * FF_PIECE)
            da = _dot_nt(dff, wdn_ref[j])
            gv = g_ref[:, cols].astype(F32)
            sg = jax.nn.sigmoid(gv)
            dg = (da * up_ref[:, cols].astype(F32) * (sg * (1.0 + gv * (1.0 - sg)))).astype(MXU_DTYPE)
            du = (da * (gv * sg)).astype(MXU_DTYPE)
            dgu_ref[:, cols] = dg
            dgu_ref[:, D_FF + j * FF_PIECE:D_FF + (j + 1) * FF_PIECE] = du
            dh = dh + _dot_nt(dg, wgu_ref[j]) + _dot_nt(du, wgu_ref[2 + j])
        xv = x_ref[...]
        r2 = _rms(xv)
        nx = xv * r2
        _acc_rows(dgpre_ref, first, dh * nx)
        dx_ref[...] = do + _norm_bwd(dh, nx, r2, gpre_ref[...])

    sd = jax.ShapeDtypeStruct
    return pl.pallas_call(
        body, name="ffn_bwd", grid=(s // tm,),
        in_specs=[_tok(tm, D_MODEL), _tok(tm, D_MODEL), _tok(tm, D_MODEL), _tok(tm, D_FF), _tok(tm, D_FF),
                  _whole((1, D_MODEL)), _whole((1, D_MODEL)),
                  _whole((4, D_MODEL, FF_PIECE), single=True), _whole((2, FF_PIECE, D_MODEL), single=True)],
        out_specs=[_tok(tm, D_MODEL), _tok(tm, D_MODEL), _tok(tm, 2 * D_FF), _whole((1, D_MODEL)),
                   _whole((1, D_MODEL))],
        out_shape=[sd((s, D_MODEL), F32), sd((s, D_MODEL), MXU_DTYPE), sd((s, 2 * D_FF), MXU_DTYPE),
                   sd((1, D_MODEL), F32), sd((1, D_MODEL), F32)],
        compiler_params=_params(("arbitrary",)),
    )(dout, ff, x2, g, up, g_post, g_pre, w_gu, w_dn)


def _tn_matmul(a, b, tm, tn, tk, name):
    s, m = a.shape
    n = b.shape[1]

    def body(a_ref, b_ref, o_ref):
        @pl.when(pl.program_id(2) == 0)
        def _():
            o_ref[...] = jnp.zeros_like(o_ref)

        o_ref[...] += lax.dot_general(a_ref[...], b_ref[...], (((0,), (0,)), ((), ())), preferred_element_type=F32)

    return pl.pallas_call(
        body, name=name, grid=(m // tm, n // tn, s // tk),
        in_specs=[pl.BlockSpec((tk, tm), lambda i, j, kk: (kk, i)), pl.BlockSpec((tk, tn), lambda i, j, kk: (kk, j))],
        out_specs=pl.BlockSpec((tm, tn), lambda i, j, kk: (i, j)),
        out_shape=jax.ShapeDtypeStruct((m, n), F32),
        compiler_params=_params(("arbitrary", "arbitrary", "arbitrary")),
    )(a, b)


def _mixer_bwd(dx2, y, o_attn, gate_b, z, g_post, g_attn, g_conv, w_out, gmat, sel, tm, ready, kinds):
    s = dx2.shape[0]
    nw = len(ready)
    nt = s // tm

    def body(d_ref, y_ref, o_ref, b_ref, z_ref, gp_ref, ga_ref, gc_ref, wo_ref, gm_ref, sel_ref, *rest):
        grads = rest[:nw]
        dy_ref, do_ref, db_ref, dz_ref, delta_ref, dgp_ref, dga_ref, dgc_ref = rest[nw:nw + 8]
        taken = rest[nw + 8:2 * nw + 8]
        send, recv = rest[2 * nw + 8:]
        first = pl.program_id(0) == 0
        pair_start, pair_finish = _pair_stages(grads, kinds, taken, send, recv)
        pl.when(first)(pair_start)
        yv = y_ref[...]
        ry = _rms(yv)
        ny = yv * ry
        d = d_ref[...]
        _acc_rows(dgp_ref, first, d * ny)
        dy = _norm_bwd(d, ny, ry, gp_ref[...]).astype(MXU_DTYPE)
        dy_ref[...] = dy
        dm = _dot_nt(dy, wo_ref[...])
        gm = gm_ref[0:128, 0:128]

        @pl.when(first)
        def _():
            dga_ref[...] = jnp.zeros_like(dga_ref)
            dgc_ref[...] = jnp.zeros_like(dgc_ref)

        def group_bwd(val, dmv, gain_ref, dg_ref, cols):
            rg = lax.rsqrt(_group_ms(val * val, gm) + EPS)
            nv = val * rg
            dg_ref[:, cols] += jnp.sum(dmv * nv, axis=0, keepdims=True)
            t = dmv * gain_ref[:, cols]
            return rg * (t - nv * _group_ms(t * nv, gm))

        delta = jnp.zeros((tm, 128), F32)
        for lo in range(0, ATTN_W, 128):
            cols = slice(lo, lo + 128)
            ov = o_ref[:, cols]
            d_o = group_bwd(ov, dm[:, cols], ga_ref, dga_ref, cols)
            do_ref[:, cols] = d_o.astype(MXU_DTYPE)
            delta = delta + _dot_exact(d_o * ov, sel_ref[cols, :], 2)
            zv = z_ref[:, cols]
            bv = b_ref[:, cols]
            d_cv = group_bwd(bv * zv, dm[:, ATTN_W + lo:ATTN_W + lo + 128], gc_ref, dgc_ref, cols)
            db_ref[:, cols] = d_cv * zv
            dz_ref[:, cols] = d_cv * bv
        delta_ref[...] = delta.T[0:HEAD_ROWS, :]
        pl.when(pl.program_id(0) == nt - 1)(pair_finish)

    sd = jax.ShapeDtypeStruct
    taken_shape = [sd((N_CHIPS, g.shape[-2], g.shape[-1] if kd == "rows" else g.shape[-1] // N_CHIPS), F32)
                   for g, kd in zip(ready, kinds)]
    out = pl.pallas_call(
        body, name="mixer_bwd", grid=(nt,),
        in_specs=[_tok(tm, D_MODEL), _tok(tm, D_MODEL), _tok(tm, 512), _tok(tm, 512), _tok(tm, 512),
                  _whole((1, D_MODEL)), _whole((1, 512)), _whole((1, 512)),
                  _whole((D_MODEL, D_MODEL), single=True), _whole((512, 512)), _whole((512, 128))] + [ANY] * nw,
        out_specs=[_tok(tm, D_MODEL), _tok(tm, 512), _tok(tm, 512), _tok(tm, 512), _feat(HEAD_ROWS, tm),
                   _whole((1, D_MODEL)), _whole((1, 512)), _whole((1, 512))] + [ANY] * nw,
        out_shape=[sd((s, D_MODEL), MXU_DTYPE), sd((s, 512), MXU_DTYPE), sd((s, 512), F32), sd((s, 512), F32),
                   sd((HEAD_ROWS, s), F32), sd((1, D_MODEL), F32), sd((1, 512), F32), sd((1, 512), F32)] + taken_shape,
        scratch_shapes=[pltpu.SemaphoreType.DMA((nw, N_CHIPS)), pltpu.SemaphoreType.DMA((nw, N_CHIPS))],
        compiler_params=_params(("arbitrary",)),
    )(dx2, y, o_attn, gate_b, z, g_post, g_attn, g_conv, w_out, gmat, sel, *ready)
    return out[:8], out[8:]


def _attn_bwd(qs, k, k_t, v, do, c_rows, c_cols, lse, delta, t, parts):
    s = qs.shape[0]
    n = s // t
    pairs = [(i, j) for j in range(n) for i in range(j, n)]
    it = jnp.asarray(np.array([p[0] for p in pairs], np.int32))
    jt = jnp.asarray(np.array([p[1] for p in pairs], np.int32))

    nw = len(parts)

    def body(it_ref, jt_ref, q_ref, k_ref, kt_ref, v_ref, do_ref, cq_ref, ck_ref, lse_ref, dl_ref, *rest):
        pb = rest[:nw]
        dq_ref, dk_ref, dv_ref, dc_ref, dcq_ref = rest[nw:nw + 5]
        rcv = rest[nw + 5:2 * nw + 5]
        dk_sc, dv_sc, dc_sc, send, recv = rest[2 * nw + 5:]
        p = pl.program_id(0)
        i = it_ref[p]
        j = jt_ref[p]
        chip_start, chip_finish = _chip_stages(pb, rcv, send, recv)

        @pl.when(p == 0)
        def _():
            chip_start()
            dq_ref[...] = jnp.zeros_like(dq_ref)
            dcq_ref[...] = jnp.zeros_like(dcq_ref)

        @pl.when(i == j)
        def _():
            dk_sc[...] = jnp.zeros_like(dk_sc)
            dv_sc[...] = jnp.zeros_like(dv_sc)
            dc_sc[...] = jnp.zeros_like(dc_sc)

        def pair_step(pp, diagonal):
            lanes = _pair_lanes(pp)
            qp = q_ref[:, lanes]
            kp = k_ref[:, lanes]
            vp = v_ref[:, lanes]
            dop = do_ref[:, lanes]
            ck_all = ck_ref[...]
            lane = lax.broadcasted_iota(jnp.int32, (t, 128), 1)
            for hb in range(2):
                h = 2 * pp + hb
                row = pl.ds(h, 1)
                bias = (cq_ref[row, :] - lse_ref[row, :]) - _head_col(ck_all, h)
                pt = jnp.exp(_dot_nt(_only_head(kp, hb), qp) + bias)
                if diagonal:
                    kpos = lax.broadcasted_iota(jnp.int32, (t, t), 0)
                    qpos = lax.broadcasted_iota(jnp.int32, (t, t), 1)
                    pt = jnp.where(kpos <= qpos, pt, 0.0)
                dv_sc[:, lanes] += _dot(pt.astype(MXU_DTYPE), _only_head(dop, hb))
                dst = pt * (_dot_nt(_only_head(vp, hb), dop) - dl_ref[row, :])
                dc_sc[...] -= jnp.where(lane == h, jnp.sum(dst, axis=1, keepdims=True), 0.0)
                dcq_ref[i, row, :] += jnp.sum(dst, axis=0, keepdims=True)
                dsb = dst.astype(MXU_DTYPE)
                dk_sc[:, lanes] += _dot(dsb, _only_head(qp, hb))
                rows = _head_rows(h)
                dq_ref[i, rows, :] += _dot(kt_ref[rows, :], dsb)

        @pl.when(i > j)
        def _():
            for pp in range(N_HEADS // 2):
                pair_step(pp, False)

        @pl.when(i == j)
        def _():
            for pp in range(N_HEADS // 2):
                pair_step(pp, True)

        @pl.when(i == n - 1)
        def _():
            dk_ref[...] = dk_sc[...]
            dv_ref[...] = dv_sc[...]
            dc_ref[...] = dc_sc[...]

        pl.when(p == len(pairs) - 1)(chip_finish)

    qi = lambda p, it_, jt_: (it_[p], 0)
    kj = lambda p, it_, jt_: (jt_[p], 0)
    row_i = lambda p, it_, jt_: (0, it_[p])
    gs = pltpu.PrefetchScalarGridSpec(
        num_scalar_prefetch=2, grid=(len(pairs),),
        in_specs=[pl.BlockSpec((t, ATTN_W), qi), pl.BlockSpec((t, ATTN_W), kj),
                  pl.BlockSpec((ATTN_W, t), lambda p, it_, jt_: (0, jt_[p])),
                  pl.BlockSpec((t, ATTN_W), kj), pl.BlockSpec((t, ATTN_W), qi),
                  pl.BlockSpec((HEAD_ROWS, t), row_i), pl.BlockSpec((t, 128), kj),
                  pl.BlockSpec((HEAD_ROWS, t), row_i), pl.BlockSpec((HEAD_ROWS, t), row_i)] + [ANY] * nw,
        out_specs=[pl.BlockSpec((n, ATTN_W, t), lambda p, it_, jt_: (0, 0, 0)),
                   pl.BlockSpec((t, ATTN_W), kj), pl.BlockSpec((t, ATTN_W), kj),
                   pl.BlockSpec((t, 128), kj),
                   pl.BlockSpec((n, HEAD_ROWS, t), lambda p, it_, jt_: (0, 0, 0))] + [ANY] * nw,
        scratch_shapes=[pltpu.VMEM((t, ATTN_W), F32), pltpu.VMEM((t, ATTN_W), F32),
                        pltpu.VMEM((t, 128), F32), pltpu.SemaphoreType.DMA((nw, 3)), pltpu.SemaphoreType.DMA((nw, 3))])
    sd = jax.ShapeDtypeStruct
    out = pl.pallas_call(
        body, name="attn_bwd", grid_spec=gs,
        out_shape=[sd((n, ATTN_W, t), F32), sd((s, ATTN_W), F32), sd((s, ATTN_W), F32),
                   sd((s, 128), F32), sd((n, HEAD_ROWS, t), F32)] + [sd((3,) + a.shape[1:], a.dtype) for a in parts],
        compiler_params=_params(("arbitrary",)),
    )(it, jt, qs, k, k_t, v, do, c_rows, c_cols, lse, delta, *parts)
    return out[:5], out[5:]


def _forget_bwd(dc_rows, dc_cols, z_t, b_col):
    s = z_t.shape[1]
    nb = s // 128

    def body(dr_ref, dcc_ref, z_ref, b_ref, dz_ref, db_ref):
        lower = _tri(128, False)
        real = lax.broadcasted_iota(jnp.int32, (HEAD_ROWS, 128), 0) < N_HEADS

        tail = jnp.zeros((HEAD_ROWS, 1), F32)
        dbias = jnp.zeros((HEAD_ROWS, 1), F32)
        for m in range(nb):
            off = (nb - 1 - m) * 128
            dc = dr_ref[:, off:off + 128] + dcc_ref[off:off + 128, :].T[0:HEAD_ROWS, :]
            dlf = _dot_exact(dc, lower, 3) + tail
            dz = dlf * jax.nn.sigmoid(-(z_ref[0:HEAD_ROWS, off:off + 128] + b_ref[...]))
            dz = jnp.where(real, dz, 0.0)
            dz_ref[off:off + 128, :] = _rows_to_cols(dz)
            tail = tail + jnp.sum(dc, axis=1, keepdims=True)
            dbias = dbias + jnp.sum(dz, axis=1, keepdims=True)
        db_ref[...] = jnp.broadcast_to(dbias, db_ref.shape)

    return pl.pallas_call(
        body, name="forget_bwd",
        out_shape=[jax.ShapeDtypeStruct((s, 128), F32), jax.ShapeDtypeStruct((HEAD_ROWS, 128), F32)],
        compiler_params=_params())(dc_rows, dc_cols, z_t, b_col)


def _inproj_bwd(dz, gate_c, u, conv_w, dq, dk, dv, dzf, db, x, dx2, g_pre, w_t, tm):
    s = x.shape[0]
    nt = s // tm
    t = dq.shape[2]
    assert t % tm == 0 and dq.shape[:2] == (s // t, ATTN_W)
    per = t // tm

    def body(dz_ref, dzn_ref, c_ref, u_ref, cp_ref, up_ref, cw_ref, dq_ref, dk_ref, dv_ref, dzf_ref, db_ref,
             x_ref, dx2_ref, g_ref, w_ref, gx_ref, dp_ref, dg_ref, dcw_ref):
        i = pl.program_id(0)
        first = i == 0
        last = i == nt - 1
        @pl.when(first)
        def _():
            dcw_ref[...] = jnp.zeros_like(dcw_ref)

        for lo in range(0, CONV_W, 128):
            cols = slice(lo, lo + 128)
            dzv = dz_ref[:, cols]
            row = lax.broadcasted_iota(jnp.int32, dzv.shape, 0)
            n0 = jnp.where(last, 0.0, dzn_ref[0:1, cols])
            n1 = jnp.where(last, 0.0, dzn_ref[1:2, cols])
            dz1 = jnp.where(row == tm - 1, n0, pltpu.roll(dzv, tm - 1, 0))
            dz2 = jnp.where(row == tm - 1, n1, jnp.where(row == tm - 2, n0, pltpu.roll(dzv, tm - 2, 0)))
            dcu = cw_ref[2:3, cols] * dzv + cw_ref[1:2, cols] * dz1 + cw_ref[0:1, cols] * dz2
            cv = c_ref[:, cols]
            uv = u_ref[:, cols]
            cu = cv * uv
            cu1, cu2 = _shift_down(cu, (cp_ref, up_ref), first, cols)
            dcw_ref[0:1, cols] += jnp.sum(dzv * cu2, axis=0, keepdims=True)
            dcw_ref[1:2, cols] += jnp.sum(dzv * cu1, axis=0, keepdims=True)
            dcw_ref[2:3, cols] += jnp.sum(dzv * cu, axis=0, keepdims=True)
            dp_ref[:, OFF_C + lo:OFF_C + lo + 128] = (dcu * uv).astype(MXU_DTYPE)
            dp_ref[:, OFF_U + lo:OFF_U + lo + 128] = (dcu * cv).astype(MXU_DTYPE)

        dp_ref[:, 0:512] = (dq_ref[0].T * Q_SCALE).astype(MXU_DTYPE)
        dp_ref[:, 512:1024] = dk_ref[...].astype(MXU_DTYPE)
        dp_ref[:, 1024:OFF_F] = dv_ref[...].astype(MXU_DTYPE)
        dp_ref[:, OFF_F:OFF_B] = dzf_ref[...].astype(MXU_DTYPE)
        dp_ref[:, OFF_B:OFF_C] = db_ref[...].astype(MXU_DTYPE)
        dh = _dot(dp_ref[...], w_ref[...])
        xv = x_ref[...]
        r1 = _rms(xv)
        nx = xv * r1
        _acc_rows(dg_ref, first, dh * nx)
        gx_ref[...] = dx2_ref[...] + _norm_bwd(dh, nx, r1, g_ref[...])

    prev = pl.BlockSpec((8, 512), lambda i: (jnp.maximum(i * (tm // 8) - 1, 0), 0))
    nxt = pl.BlockSpec((8, 512), lambda i: (jnp.minimum((i + 1) * (tm // 8), s // 8 - 1), 0))
    sd = jax.ShapeDtypeStruct
    return pl.pallas_call(
        body, name="inproj_bwd", grid=(nt,),
        in_specs=[_tok(tm, 512), nxt, _tok(tm, 512), _tok(tm, 512), prev, prev, _whole((3, 512)),
                  pl.BlockSpec((1, ATTN_W, tm), lambda i: (i // per, 0, i % per)), _tok(tm, 512), _tok(tm, 512),
                  _tok(tm, 128),
                  _tok(tm, 512),
                  _tok(tm, D_MODEL), _tok(tm, D_MODEL), _whole((1, D_MODEL)), _whole((IN_PAD, D_MODEL), single=True)],
        out_specs=[_tok(tm, D_MODEL), _tok(tm, IN_PAD), _whole((1, D_MODEL)), _whole((8, 512))],
        out_shape=[sd((s, D_MODEL), F32), sd((s, IN_PAD), MXU_DTYPE), sd((1, D_MODEL), F32), sd((8, 512), F32)],
        compiler_params=_params(("arbitrary",)),
    )(dz, dz, gate_c, u, gate_c, u, conv_w, dq, dk, dv, dzf, db, x, dx2, g_pre, w_t)


def _tile(s, want):
    return want if s % want == 0 else s


def _halves(a):
    return a.reshape(2, a.shape[0] // 2, a.shape[1])


def _device_step(x, target, w, mom1, mom2, w_in_t, m_in_t, v_in_t, c_idx, me_idx):
    s = x.shape[0]
    tm = _tile(s, 512)
    tf = _tile(s, 256)
    ta = _tile(s, 512)
    tkk = _tile(s, 2048)
    gidx = np.arange(512) // HEAD_DIM
    gmat = jnp.asarray(gidx[:, None] == gidx[None, :], MXU_DTYPE)
    sel = jnp.asarray(gidx[:, None] == np.arange(128)[None, :], MXU_DTYPE)
    g_mix_pre, g_mix_post, g_ffn_pre, g_ffn_post = w["g_mix_pre"], w["g_mix_post"], w["g_ffn_pre"], w["g_ffn_post"]
    g_attn, g_conv, b_forget = w["g_attn_out"], w["g_conv_out"], w["b_forget"]
    shard = {n: _halves(w[n][0].astype(MXU_DTYPE)) for n in BIG[1:]}
    piece_rows = IN_W // N_CHIPS

    g_in, conv_all = _gather_weights([w_in_t.reshape(piece_rows, D_MODEL).astype(MXU_DTYPE)], w["conv_w"][0])
    w_rows = g_in.reshape(IN_W, D_MODEL)
    w_t = jnp.concatenate([w_rows[:OFF_F + N_HEADS], jnp.zeros((OFF_B - OFF_F - N_HEADS, D_MODEL), MXU_DTYPE),
                           w_rows[OFF_F + N_HEADS:]], axis=0)
    conv_w = jnp.transpose(conv_all, (1, 0, 2)).reshape(3, CONV_W)

    h1, qs, k, v, k_t, v_t, z_t, gate_b, gate_c, u = _inproj_fwd(x, g_mix_pre, w_t, tm)
    b_col = jnp.pad(jnp.transpose(b_forget), ((0, HEAD_ROWS - N_HEADS), (0, 0)))
    c_rows, c_cols, q_bias, k_bias = _forget_fwd(z_t, b_col)
    o_attn, lse, (g_out, g_gu, g_dn) = _attn_fwd(qs, k, v_t, q_bias, k_bias, ta,
                                                 [shard["w_out"], shard["w_gate_up"], shard["w_down"]])
    w_out = g_out.reshape(D_MODEL, D_MODEL)
    w_gu = g_gu.reshape(N_CHIPS, D_MODEL, FF_PIECE)
    w_dn = g_dn.reshape(2, FF_PIECE, D_MODEL)
    x2, merged, y, z = _mixer_fwd(x, o_attn, gate_b, gate_c, u, conv_w, g_attn, g_conv, w_out, g_mix_post, gmat, tm)
    h2, g, up, a, ff, dout, loss_acc = _ffn_fwd(x2, target, g_ffn_pre, w_gu, w_dn, g_ffn_post, tf)

    dx2, dff, dgu, dg_ffn_post, dg_ffn_pre = _ffn_bwd(dout, ff, x2, g, up, g_ffn_post, g_ffn_pre, w_gu, w_dn, tf)
    dw_dn = _tn_matmul(a, dff, FF_PIECE, 1024, tkk, "dw_down").reshape(N_CHIPS, 2, D_FF // (2 * N_CHIPS), D_MODEL)
    dw_gu = _tn_matmul(h2, dgu, 1024, FF_PIECE, tkk, "dw_gate_up").reshape(2, D_MODEL // 2, 2 * D_FF)
    (dy, d_o, d_b, dz, delta, dg_mix_post, dg_attn, dg_conv), (a_gu, a_dn) = _mixer_bwd(
        dx2, y, o_attn, gate_b, z, g_mix_post, g_attn, g_conv, w_out, gmat, sel, tm, [dw_gu, dw_dn], ["cols", "rows"])
    dw_out = _tn_matmul(merged, dy, 1024, 1024, tkk, "dw_out")
    place = jnp.concatenate([c_idx, me_idx])
    sum_gu = _pair_sum(place, dw_gu, "cols", a_gu, "pair_sum_w_gate_up")
    sum_dn = _pair_sum(place, dw_dn, "rows", a_dn, "pair_sum_w_down")
    (dq_t, dk, dv, dc_cols, dcq), (r_gu, r_dn) = _attn_bwd(
        qs, k, k_t, v, d_o, c_rows, c_cols, lse, delta, ta, [sum_gu[1], sum_dn[1]])
    dc_rows = jnp.transpose(dcq, (1, 0, 2)).reshape(HEAD_ROWS, s)
    dzf, db_f = _forget_bwd(dc_rows, dc_cols, z_t, b_col)
    grad_x, dproj, dg_mix_pre, dcw = _inproj_bwd(dz, gate_c, u, conv_w, dq_t, dk, dv, dzf, d_b,
                                                 x, dx2, g_mix_pre, w_t, tm)
    dw_t = _tn_matmul(dproj, h1, 640, 1024, tkk, "dw_in")
    dw_in = jnp.concatenate([dw_t[:OFF_F + N_HEADS], dw_t[OFF_B:]], axis=0).reshape(N_CHIPS, piece_rows, D_MODEL)
    dw_out = dw_out.reshape(N_CHIPS, 2, D_MODEL // (2 * N_CHIPS), D_MODEL)

    a_in, a_out = _pair_exchange([dw_in, dw_out], ["lanes", "rows"])
    sum_in = _pair_sum(place, dw_in, "lanes", a_in, "pair_sum_w_in")
    sum_out = _pair_sum(place, dw_out, "rows", a_out, "pair_sum_w_out")
    small = dict(b_forget=db_f[:N_HEADS, 0], g_attn_out=dg_attn, g_conv_out=dg_conv, g_mix_pre=dg_mix_pre,
                 g_mix_post=dg_mix_post, g_ffn_pre=dg_ffn_pre, g_ffn_post=dg_ffn_post)
    (r_in, r_out), small_all = _chip_exchange([sum_in[1], sum_out[1]], _pack_small(small, dcw[:3], loss_acc[0, 0]))
    totals = [_chip_sum(sb[0], r, "chip_sum_" + n)
              for n, sb, r in zip(BIG, (sum_in, sum_out, sum_gu, sum_dn), (r_in, r_out, r_gu, r_dn))]
    shared = _pair_share(totals, "pair_share")
    new = {"w_in": _adamw_lanes(c_idx, w_in_t, totals[0], shared[0], m_in_t, v_in_t, "adamw_w_in")}
    for n, mine, theirs in list(zip(BIG, totals, shared))[1:]:
        new[n] = _adamw(c_idx, w[n][0], mine, theirs, mom1[n][0], mom2[n][0], 2, "adamw_" + n)
    return grad_x, new, small_all


BIG = ("w_in", "w_out", "w_gate_up", "w_down")
ANY = pl.BlockSpec(memory_space=pl.ANY)


def _place():
    x, y, c = lax.axis_index("x"), lax.axis_index("y"), lax.axis_index("c")
    others = [(1 - x, y), (x, 1 - y), (1 - x, 1 - y)]
    return x, y, c, 2 * x + y, others, [2 * px + py for px, py in others]


def _remote(src, dst, send, recv, dev):
    return pltpu.make_async_remote_copy(src_ref=src, dst_ref=dst, send_sem=send, recv_sem=recv,
                                        device_id=dev, device_id_type=MESH_ID)


def _gather_stages(sh, outs, send, recv):
    x, y, c, me, others, chips = _place()
    sib = (x, y, 1 - c)
    every = [(w, kk) for w in range(len(sh)) for kk in range(3)]

    def half_of(ref, half, piece=None):
        ref = ref if piece is None else ref.at[piece]
        if len(ref.shape) == 3:
            return ref.at[half]
        hc = ref.shape[1] // 2
        return ref.at[:, pl.ds(pl.multiple_of(half * hc, 128), hc)]

    def first(w, kk):
        return _remote(half_of(sh[w], c), half_of(outs[w], c, me), send.at[w, kk], recv.at[w, kk], (*others[kk], c))

    def landed(w, kk):
        r = half_of(outs[w], c, chips[kk])
        return _remote(r, r, send.at[w, kk], recv.at[w, kk], (*others[kk], c))

    def onward(w, kk, half):
        r = half_of(outs[w], half, chips[kk])
        return _remote(r, r, send.at[w, 3 + kk], recv.at[w, 3 + kk], sib)

    def start():
        for w, kk in every:
            first(w, kk).start()

    def forward():
        for w, kk in every:
            landed(w, kk).wait_recv()
            onward(w, kk, c).start()

    def finish():
        for w, kk in every:
            onward(w, kk, 1 - c).wait_recv()
        for w, kk in every:
            first(w, kk).wait_send()
            onward(w, kk, c).wait_send()

    return start, forward, finish


def _pair_piece(ref, kind, p, half):
    if kind == "rows":
        return ref.at[p, half]
    if kind == "lanes":
        hc = ref.shape[2] // 2
        return ref.at[p, :, pl.ds(pl.multiple_of(half * hc, 128), hc)]
    cols = ref.shape[2] // N_CHIPS
    return ref.at[half, :, pl.ds(p * cols, cols)]


def _pair_stages(g, kinds, a, send, recv):
    x, y, c, _, _, _ = _place()
    copies = [_remote(_pair_piece(g[w], kinds[w], p, 1 - c), a[w].at[p], send.at[w, p], recv.at[w, p], (x, y, 1 - c))
              for w in range(len(g)) for p in range(N_CHIPS)]

    def start():
        for cp in copies:
            cp.start()

    def finish():
        for cp in copies:
            cp.wait()

    return start, finish


def _chip_stages(pb, rcv, send, recv):
    x, y, c, _, others, chips = _place()
    copies = [_remote(pb[w].at[chips[kk]], rcv[w].at[kk], send.at[w, kk], recv.at[w, kk], (*others[kk], c))
              for w in range(len(pb)) for kk in range(3)]

    def start():
        for cp in copies:
            cp.start()

    def finish():
        for cp in copies:
            cp.wait()

    return start, finish


def _gather_weights(shards, conv_w):
    n = len(shards)

    def body(*refs):
        sh, cw, outs, cwo = refs[:n], refs[n], refs[n + 1:2 * n + 1], refs[2 * n + 1]
        send, recv = refs[2 * n + 2:]
        x, y, c, me, others, chips = _place()
        start, forward, finish = _gather_stages(sh, outs, send, recv)
        start()
        small = [_remote(cw, cwo.at[me], send.at[n, kk], recv.at[n, kk], (*others[kk], c)) for kk in range(3)]
        for cp in small:
            cp.start()
        forward()
        for kk in range(3):
            _remote(cw, cwo.at[chips[kk]], send.at[n, kk], recv.at[n, kk], (*others[kk], c)).wait_recv()
        finish()
        for cp in small:
            cp.wait_send()

    out_shape = [jax.ShapeDtypeStruct((N_CHIPS,) + s.shape, s.dtype) for s in shards]
    out_shape.append(jax.ShapeDtypeStruct((N_CHIPS,) + conv_w.shape, conv_w.dtype))
    got = pl.pallas_call(
        body, name="gather_weights", in_specs=[ANY] * (n + 1), out_specs=[ANY] * (n + 1), out_shape=out_shape,
        scratch_shapes=[pltpu.SemaphoreType.DMA((n + 1, 6)), pltpu.SemaphoreType.DMA((n + 1, 6))],
    )(*shards, conv_w)
    me = 2 * lax.axis_index("x") + lax.axis_index("y")
    return [lax.dynamic_update_index_in_dim(g, own, me, 0) for g, own in zip(got, list(shards) + [conv_w])]


def _taken_shape(g, kind):
    if kind == "rows":
        return (N_CHIPS,) + g.shape[2:]
    if kind == "lanes":
        return g.shape[:2] + (g.shape[2] // 2,)
    return (N_CHIPS, g.shape[1], g.shape[2] // N_CHIPS)


def _pair_sum(place, g, kind, a, name):
    _, half, cols = a.shape
    if kind == "rows":
        mine = pl.BlockSpec((1, 1, half, cols), lambda p, pr: (p, pr[0], 0, 0))
    elif kind == "lanes":
        mine = pl.BlockSpec((1, half, cols), lambda p, pr: (p, 0, pr[0]))
    else:
        mine = pl.BlockSpec((1, half, cols), lambda p, pr: (pr[0], 0, p))

    def body(place_ref, g_ref, a_ref, own_ref, pb_ref):
        tot = (g_ref[0, 0] if kind == "rows" else g_ref[0]) + a_ref[0]
        pb_ref[0] = tot.astype(BF16)

        @pl.when(pl.program_id(0) == place_ref[1])
        def _():
            own_ref[...] = tot

    gs = pltpu.PrefetchScalarGridSpec(
        num_scalar_prefetch=1, grid=(N_CHIPS,),
        in_specs=[mine,
                  pl.BlockSpec((1, half, cols), lambda p, pr: (p, 0, 0))],
        out_specs=[pl.BlockSpec((half, cols), lambda p, pr: (0, 0)),
                   pl.BlockSpec((1, half, cols), lambda p, pr: (p, 0, 0))])
    return pl.pallas_call(
        body, name=name, grid_spec=gs,
        out_shape=[jax.ShapeDtypeStruct((half, cols), F32), jax.ShapeDtypeStruct((N_CHIPS, half, cols), BF16)],
        compiler_params=_params(("arbitrary",)),
    )(place, g, a)


def _chip_sum(own, rcv, name):
    half, cols = own.shape

    def body(o_ref, r_ref, t_ref):
        t_ref[...] = ((o_ref[...] + r_ref[0].astype(F32)) + r_ref[1].astype(F32)) + r_ref[2].astype(F32)

    return pl.pallas_call(
        body, name=name, grid=(1,),
        in_specs=[pl.BlockSpec((half, cols), lambda i: (0, 0)), pl.BlockSpec((3, half, cols), lambda i: (0, 0, 0))],
        out_specs=pl.BlockSpec((half, cols), lambda i: (0, 0)),
        out_shape=jax.ShapeDtypeStruct((half, cols), F32), compiler_params=_params(("arbitrary",)),
    )(own, rcv)


def _small_stages(sm, smg, send, recv):
    x, y, c, _, _, _ = _place()

    def peer(r):
        return (1 - x if r & 4 else x, 1 - y if r & 2 else y, 1 - c if r & 1 else c)

    mine = 4 * x + 2 * y + c
    copies = [_remote(sm, smg.at[mine], send.at[r - 1], recv.at[r - 1], peer(r)) for r in range(1, 8)]

    def start():
        for cp in copies:
            cp.start()

    def finish():
        for r in range(1, 8):
            px, py, pc = peer(r)
            _remote(sm, smg.at[4 * px + 2 * py + pc], send.at[r - 1], recv.at[r - 1], (px, py, pc)).wait_recv()
        for cp in copies:
            cp.wait_send()

    return start, finish


def _pair_exchange(grads, kinds):
    n = len(grads)

    def body(*refs):
        start, finish = _pair_stages(refs[:n], kinds, refs[n:2 * n], *refs[2 * n:])
        start()
        finish()

    return pl.pallas_call(
        body, name="pair_exchange", in_specs=[ANY] * n, out_specs=[ANY] * n,
        out_shape=[jax.ShapeDtypeStruct(_taken_shape(g, kd), g.dtype) for g, kd in zip(grads, kinds)],
        scratch_shapes=[pltpu.SemaphoreType.DMA((n, N_CHIPS)), pltpu.SemaphoreType.DMA((n, N_CHIPS))],
    )(*grads)


def _chip_exchange(parts, small):
    n = len(parts)

    def body(*refs):
        pb, sm, rcv, smg = refs[:n], refs[n], refs[n + 1:2 * n + 1], refs[2 * n + 1]
        send, recv, ssend, srecv = refs[2 * n + 2:]
        chip_start, chip_finish = _chip_stages(pb, rcv, send, recv)
        small_start, small_finish = _small_stages(sm, smg, ssend, srecv)
        chip_start()
        small_start()
        chip_finish()
        small_finish()

    out_shape = [jax.ShapeDtypeStruct((3,) + p.shape[1:], p.dtype) for p in parts]
    out_shape.append(jax.ShapeDtypeStruct((8,) + small.shape, small.dtype))
    *arrived, small_land = pl.pallas_call(
        body, name="chip_exchange", in_specs=[ANY] * (n + 1), out_specs=[ANY] * (n + 1), out_shape=out_shape,
        scratch_shapes=[pltpu.SemaphoreType.DMA((n, 3)), pltpu.SemaphoreType.DMA((n, 3)),
                        pltpu.SemaphoreType.DMA((7,)), pltpu.SemaphoreType.DMA((7,))],
    )(*parts, small)
    mine = 4 * lax.axis_index("x") + 2 * lax.axis_index("y") + lax.axis_index("c")
    return arrived, lax.dynamic_update_index_in_dim(small_land, small, mine, 0)


def _pair_share(totals, name):
    n = len(totals)

    def body(*refs):
        t, g = refs[:n], refs[n:2 * n]
        send, recv = refs[2 * n:]
        x, y, c, _, _, _ = _place()
        copies = [_remote(t[w], g[w], send.at[w], recv.at[w], (x, y, 1 - c)) for w in range(n)]
        for cp in copies:
            cp.start()
        for cp in copies:
            cp.wait()

    return pl.pallas_call(
        body, name=name, in_specs=[ANY] * n, out_specs=[ANY] * n,
        out_shape=[jax.ShapeDtypeStruct(t.shape, t.dtype) for t in totals],
        scratch_shapes=[pltpu.SemaphoreType.DMA((n,)), pltpu.SemaphoreType.DMA((n,))],
    )(*totals)


def _adamw_math(w, g, m, v):
    m = ADAM_B1 * m + (1.0 - ADAM_B1) * g
    v = ADAM_B2 * v + (1.0 - ADAM_B2) * (g * g)
    m_hat = m / (1.0 - ADAM_B1 ** ADAM_STEP)
    v_hat = v / (1.0 - ADAM_B2 ** ADAM_STEP)
    delta = -ADAM_LR * (m_hat / (jnp.sqrt(v_hat) + ADAM_EPS) + ADAM_WD * w)
    return delta, m, v


def _adamw(c_idx, w, mine, theirs, m, v, nb, name):
    rows, cols = w.shape
    tr = rows // (2 * nb)

    def body(c_ref, w_ref, a_ref, b_ref, m_ref, v_ref, g_ref, d_ref, nm_ref, nv_ref):
        g = jnp.where(pl.program_id(0) == c_ref[0], a_ref[...], b_ref[...])
        g_ref[...] = g
        d_ref[...], nm_ref[...], nv_ref[...] = _adamw_math(w_ref[...], g, m_ref[...], v_ref[...])

    full = pl.BlockSpec((tr, cols), lambda hh, i, cr: (hh * nb + i, 0))
    half = pl.BlockSpec((tr, cols), lambda hh, i, cr: (i, 0))
    gs = pltpu.PrefetchScalarGridSpec(num_scalar_prefetch=1, grid=(2, nb), in_specs=[full, half, half, full, full],
                                      out_specs=[full] * 4)
    return pl.pallas_call(
        body, name=name, grid_spec=gs, out_shape=[jax.ShapeDtypeStruct((rows, cols), F32)] * 4,
        compiler_params=_params(("arbitrary", "arbitrary")),
    )(c_idx, w, mine, theirs, m, v)


def _adamw_lanes(c_idx, w, mine, theirs, m, v, name):
    rows, _, cols = w.shape
    hc = cols // 2

    def body(c_ref, w_ref, a_ref, b_ref, m_ref, v_ref, g_ref, d_ref, nm_ref, nv_ref):
        g = jnp.where(pl.program_id(0) == c_ref[0], a_ref[...], b_ref[...])
        g_ref[:, 0, :] = g
        d_ref[:, 0, :], nm_ref[:, 0, :], nv_ref[:, 0, :] = _adamw_math(w_ref[:, 0, :], g, m_ref[:, 0, :], v_ref[:, 0, :])

    full = pl.BlockSpec((rows, 1, hc), lambda hh, cr: (0, 0, hh))
    half = pl.BlockSpec((rows, hc), lambda hh, cr: (0, 0))
    gs = pltpu.PrefetchScalarGridSpec(num_scalar_prefetch=1, grid=(2,), in_specs=[full, half, half, full, full],
                                      out_specs=[full] * 4)
    return pl.pallas_call(
        body, name=name, grid_spec=gs, out_shape=[jax.ShapeDtypeStruct((rows, 1, cols), F32)] * 4,
        compiler_params=_params(("arbitrary",)),
    )(c_idx, w, mine, theirs, m, v)


SMALL = ("g_mix_pre", "g_mix_post", "g_ffn_pre", "g_ffn_post")
SMALL_ALL = SMALL + ("g_attn_out", "g_conv_out", "conv_w", "b_forget")
SMALL_AT = {"g_mix_pre": (0, 0, 1024), "g_mix_post": (1, 0, 1024), "g_ffn_pre": (2, 0, 1024),
            "g_ffn_post": (3, 0, 1024), "g_attn_out": (4, 0, 512), "g_conv_out": (4, 512, 512),
            "b_forget": (7, 0, N_HEADS)}
CONV_AT = ((5, 0), (5, 512), (6, 0))
LOSS_AT = (6, 512)


def _pack_small(t, conv_full, loss_sum):
    conv = jnp.concatenate([conv_full.reshape(1, 3 * CONV_W), loss_sum.reshape(1, 1),
                            jnp.zeros((1, 2048 - 3 * CONV_W - 1), F32)], axis=1).reshape(2, 1024)
    return jnp.concatenate([t[n].reshape(1, 1024) for n in SMALL]
                           + [jnp.concatenate([t["g_attn_out"].reshape(1, 512), t["g_conv_out"].reshape(1, 512)], axis=1),
                              conv, jnp.pad(t["b_forget"].reshape(1, N_HEADS), ((0, 0), (0, 1024 - N_HEADS)))], axis=0)


def _small_update(me_idx, gathered, w, m, v):
    def body(me_ref, gg_ref, *refs):
        k = len(SMALL_ALL)
        w_refs, m_refs, v_refs = refs[:k], refs[k:2 * k], refs[2 * k:3 * k]
        loss_ref = refs[3 * k]
        outs = refs[3 * k + 1:3 * k + 1 + 4 * k]
        sums = refs[-1]
        g = gg_ref[0]
        for dev in range(1, 8):
            g = g + gg_ref[dev]
        sums[...] = g
        loss_ref[...] = sums[LOSS_AT[0]:LOSS_AT[0] + 1, LOSS_AT[1]:LOSS_AT[1] + 1]
        mine = pl.multiple_of(me_ref[0] * 128, 128)
        for idx, name in enumerate(SMALL_ALL):
            g_ref, d_ref, nm_ref, nv_ref = outs[4 * idx:4 * idx + 4]
            if name == "conv_w":
                for r, (row, lo) in enumerate(CONV_AT):
                    gr = sums[row:row + 1, pl.ds(lo + mine, 128)]
                    g_ref[0, r:r + 1, :] = gr
                    d_ref[0, r:r + 1, :], nm_ref[0, r:r + 1, :], nv_ref[0, r:r + 1, :] = _adamw_math(
                        w_refs[idx][0, r:r + 1, :], gr, m_refs[idx][0, r:r + 1, :], v_refs[idx][0, r:r + 1, :])
            else:
                row, lo, n = SMALL_AT[name]
                gr = sums[row:row + 1, lo:lo + n]
                g_ref[...] = gr
                d_ref[...], nm_ref[...], nv_ref[...] = _adamw_math(w_refs[idx][...], gr, m_refs[idx][...],
                                                                    v_refs[idx][...])

    def whole(a):
        nd = a.ndim
        return pl.BlockSpec(a.shape, lambda i, mr: (0,) * nd)

    ins = [t[n] for t in (w, m, v) for n in SMALL_ALL]
    out_shape = [jax.ShapeDtypeStruct((1, 1), F32)]
    for n in SMALL_ALL:
        out_shape += [jax.ShapeDtypeStruct(w[n].shape, F32)] * 4
    gs = pltpu.PrefetchScalarGridSpec(
        num_scalar_prefetch=1, grid=(1,), in_specs=[whole(gathered)] + [whole(a) for a in ins],
        out_specs=[whole(o) for o in out_shape], scratch_shapes=[pltpu.VMEM((8, 1024), F32)])
    out = pl.pallas_call(body, name="small_update", grid_spec=gs, out_shape=out_shape,
                         compiler_params=_params(("arbitrary",)))(me_idx, gathered, *ins)
    return out[0], {n: out[1 + 4 * i:5 + 4 * i] for i, n in enumerate(SMALL_ALL)}


def kernel(x, w_in, b_forget, conv_w, g_attn_out, g_conv_out, w_out, g_mix_pre, g_mix_post, w_gate_up, w_down, g_ffn_pre, g_ffn_post, loss_target, m_w_in, m_b_forget, m_conv_w, m_g_attn_out, m_g_conv_out, m_w_out, m_g_mix_pre, m_g_mix_post, m_w_gate_up, m_w_down, m_g_ffn_pre, m_g_ffn_post, v_w_in, v_b_forget, v_conv_w, v_g_attn_out, v_g_conv_out, v_w_out, v_g_mix_pre, v_g_mix_post, v_w_gate_up, v_w_down, v_g_ffn_pre, v_g_ffn_post):
    w = dict(w_in=w_in, b_forget=b_forget, conv_w=conv_w, g_attn_out=g_attn_out, g_conv_out=g_conv_out, w_out=w_out,
             g_mix_pre=g_mix_pre, g_mix_post=g_mix_post, w_gate_up=w_gate_up, w_down=w_down, g_ffn_pre=g_ffn_pre,
             g_ffn_post=g_ffn_post)
    m = dict(w_in=m_w_in, b_forget=m_b_forget, conv_w=m_conv_w, g_attn_out=m_g_attn_out, g_conv_out=m_g_conv_out,
             w_out=m_w_out, g_mix_pre=m_g_mix_pre, g_mix_post=m_g_mix_post, w_gate_up=m_w_gate_up, w_down=m_w_down,
             g_ffn_pre=m_g_ffn_pre, g_ffn_post=m_g_ffn_post)
    v = dict(w_in=v_w_in, b_forget=v_b_forget, conv_w=v_conv_w, g_attn_out=v_g_attn_out, g_conv_out=v_g_conv_out,
             w_out=v_w_out, g_mix_pre=v_g_mix_pre, g_mix_post=v_g_mix_post, w_gate_up=v_w_gate_up, w_down=v_w_down,
             g_ffn_pre=v_g_ffn_pre, g_ffn_post=v_g_ffn_post)
    cx, cy, cc = lax.axis_index("x"), lax.axis_index("y"), lax.axis_index("c")
    me = 2 * cx + cy
    c_idx = cc.astype(jnp.int32).reshape(1)
    me_idx = me.astype(jnp.int32).reshape(1)

    stored = lambda a: jnp.transpose(a, (2, 0, 1))
    grad_x, big, small_all = _device_step(x[0], loss_target[0], w, m, v, stored(w_in), stored(m_w_in),
                                          stored(v_w_in), c_idx, me_idx)
    gsum, delta, new_m, new_v = {}, {}, {}, {}
    for n in BIG:
        back = (lambda r: jnp.transpose(r, (1, 2, 0))) if n == "w_in" else (lambda r: r[None])
        gsum[n], delta[n], new_m[n], new_v[n] = [back(r) for r in big[n]]
    loss_sum, small_new = _small_update(me_idx, small_all, w, m, v)
    for n in SMALL_ALL:
        gsum[n], delta[n], new_m[n], new_v[n] = small_new[n]
    loss = 0.5 * loss_sum[0, 0]

    order = ("w_in", "b_forget", "conv_w", "g_attn_out", "g_conv_out", "w_out", "g_mix_pre", "g_mix_post",
             "w_gate_up", "w_down", "g_ffn_pre", "g_ffn_post")
    return (loss, grad_x[None], *[gsum[n] for n in order], *[delta[n] for n in order],
            *[new_m[n] for n in order], *[new_v[n] for n in order])
```

```python
import functools

import jax
import jax.numpy as jnp
import numpy as np
from jax import lax
from jax.experimental import pallas as pl
from jax.experimental.pallas import tpu as pltpu

F32 = jnp.float32
BF16 = jnp.bfloat16
MXU_DTYPE = jnp.bfloat16

D_MODEL = 1024
HEAD_DIM = 64
N_HEADS = 8
ATTN_W = 512
CONV_W = 512
D_FF = 2816
FF_PIECE = 1408
EPS = 1e-6
Q_SCALE = HEAD_DIM ** -0.5

OFF_F = 1536
OFF_B = 1664
OFF_C = 2176
OFF_U = 2688
IN_PAD = 3200
IN_W = 3080
N_CHIPS = 4

ADAM_LR = 0.001
ADAM_B1 = 0.9
ADAM_B2 = 0.999
ADAM_EPS = 1e-08
ADAM_WD = 0.01
ADAM_STEP = 10

VMEM_LIMIT_V7X = 56 * 1024 * 1024
MESH_ID = pl.DeviceIdType.MESH


def _params(sem=None, vmem=VMEM_LIMIT_V7X):
    kw = {"vmem_limit_bytes": vmem}
    if sem is not None:
        kw["dimension_semantics"] = sem
    return pltpu.CompilerParams(**kw)


def _dot(a, b):
    return jnp.dot(a, b, preferred_element_type=F32)


def _dot_nt(a, b):
    return lax.dot_general(a, b, (((1,), (1,)), ((), ())), preferred_element_type=F32)


def _dot_exact(x, ones, parts):
    if ones.dtype == F32:
        return _dot(x, ones)
    acc = None
    rem = x
    for _ in range(parts):
        piece = rem.astype(BF16)
        rem = rem - piece.astype(F32)
        term = _dot(piece, ones)
        acc = term if acc is None else acc + term
    return acc


def _rms(v):
    return lax.rsqrt(jnp.mean(v * v, axis=-1, keepdims=True) + EPS)


def _tok(tm, w):
    return pl.BlockSpec((tm, w), lambda i: (i, 0))


def _whole(shape, single=False):
    nd = len(shape)
    if single:
        return pl.BlockSpec(shape, lambda i: (0,) * nd, pipeline_mode=pl.Buffered(1))
    return pl.BlockSpec(shape, lambda i: (0,) * nd)


def _feat(rows, tm):
    return pl.BlockSpec((rows, tm), lambda i: (0, i))


def _inproj_fwd(x, g_pre, w_t, tm):
    s = x.shape[0]

    def body(x_ref, g_ref, w_ref, h_ref, q_ref, k_ref, v_ref, kt_ref, vt_ref, zt_ref, b_ref, c_ref, u_ref):
        xv = x_ref[...]
        h = ((xv * _rms(xv)) * g_ref[...]).astype(MXU_DTYPE)
        h_ref[...] = h

        def proj(lo, hi):
            return _dot_nt(h, w_ref[lo:hi, :])

        q_ref[...] = (proj(0, 512) * Q_SCALE).astype(MXU_DTYPE)
        kt = _dot_nt(w_ref[512:1024, :], h)
        vt = _dot_nt(w_ref[1024:OFF_F, :], h)
        kt_ref[...] = kt.astype(MXU_DTYPE)
        vt_ref[...] = vt.astype(MXU_DTYPE)
        k_ref[...] = kt.T.astype(MXU_DTYPE)
        v_ref[...] = vt.T.astype(MXU_DTYPE)
        zt_ref[...] = _dot_nt(w_ref[OFF_F:OFF_B, :], h)
        b_ref[...] = proj(OFF_B, OFF_C).astype(MXU_DTYPE)
        c_ref[...] = proj(OFF_C, OFF_U).astype(MXU_DTYPE)
        u_ref[...] = proj(OFF_U, IN_PAD).astype(MXU_DTYPE)

    sd = jax.ShapeDtypeStruct
    return pl.pallas_call(
        body, name="inproj_fwd", grid=(s // tm,),
        in_specs=[_tok(tm, D_MODEL), _whole((1, D_MODEL)), _whole((IN_PAD, D_MODEL), single=True)],
        out_specs=[_tok(tm, D_MODEL), _tok(tm, 512), _tok(tm, 512), _tok(tm, 512), _feat(512, tm), _feat(512, tm),
                   _feat(128, tm), _tok(tm, 512), _tok(tm, 512), _tok(tm, 512)],
        out_shape=[sd((s, D_MODEL), MXU_DTYPE), sd((s, 512), MXU_DTYPE), sd((s, 512), MXU_DTYPE),
                   sd((s, 512), MXU_DTYPE), sd((512, s), MXU_DTYPE), sd((512, s), MXU_DTYPE), sd((128, s), F32),
                   sd((s, 512), MXU_DTYPE), sd((s, 512), MXU_DTYPE), sd((s, 512), MXU_DTYPE)],
        compiler_params=_params(("arbitrary",)),
    )(x, g_pre, w_t)


def _tri(n, upper):
    r = lax.broadcasted_iota(jnp.int32, (n, n), 0)
    c = lax.broadcasted_iota(jnp.int32, (n, n), 1)
    return ((r <= c) if upper else (r >= c)).astype(MXU_DTYPE)


HEAD_ROWS = 16


def _rows_to_cols(v):
    return jnp.concatenate([v, jnp.zeros((128 - HEAD_ROWS, 128), F32)], axis=0).T


BIAS_PARTS = 3


def _bias_placement():
    place_q = np.zeros((BIAS_PARTS, 128, ATTN_W), np.float32)
    place_k = np.zeros((BIAS_PARTS, 128, ATTN_W), np.float32)
    ones_q = np.zeros((1, ATTN_W), np.float32)
    ones_k = np.zeros((1, ATTN_W), np.float32)
    for h in range(N_HEADS):
        base = 2 * HEAD_DIM * (h // 2) + HEAD_DIM * (1 - h % 2)
        for part in range(BIAS_PARTS):
            place_q[part, h, base + part] = 1.0
            place_k[part, h, base + BIAS_PARTS + part] = -1.0
        ones_q[0, base + BIAS_PARTS:base + 2 * BIAS_PARTS] = 1.0
        ones_k[0, base:base + BIAS_PARTS] = 1.0
    return (jnp.asarray(place_q, MXU_DTYPE), jnp.asarray(place_k, MXU_DTYPE), jnp.asarray(ones_q), jnp.asarray(ones_k))


def _forget_fwd(z_t, b_col):
    s = z_t.shape[1]
    nb = s // 128

    def body(z_ref, b_ref, pq_ref, pk_ref, oq_ref, ok_ref, c_ref, cc_ref, qa_ref, ka_ref):
        upper = _tri(128, True)

        carry = jnp.zeros((HEAD_ROWS, 1), F32)
        for n in range(nb):
            off = n * 128
            lf = jax.nn.log_sigmoid(z_ref[0:HEAD_ROWS, off:off + 128] + b_ref[...])
            cs = _dot_exact(lf, upper, 3) + carry
            c_ref[:, off:off + 128] = cs
            cc_ref[off:off + 128, :] = _rows_to_cols(cs)
            carry = carry + jnp.sum(lf, axis=1, keepdims=True)

        rb = min(s, 512)
        for off in range(0, s, rb):
            qa = jnp.broadcast_to(oq_ref[...], (rb, ATTN_W))
            ka = jnp.broadcast_to(ok_ref[...], (rb, ATTN_W))
            rem = cc_ref[off:off + rb, :]
            for part in range(BIAS_PARTS):
                piece = rem.astype(MXU_DTYPE)
                rem = rem - piece.astype(F32)
                qa = qa + _dot(piece, pq_ref[part])
                ka = ka + _dot(piece, pk_ref[part])
            qa_ref[off:off + rb, :] = qa.astype(MXU_DTYPE)
            ka_ref[off:off + rb, :] = ka.astype(MXU_DTYPE)

    sd = jax.ShapeDtypeStruct
    return pl.pallas_call(body, name="forget_fwd",
                          out_shape=[sd((HEAD_ROWS, s), F32), sd((s, 128), F32), sd((s, ATTN_W), MXU_DTYPE),
                                     sd((s, ATTN_W), MXU_DTYPE)],
                          compiler_params=_params())(z_t, b_col, *_bias_placement())


def _aligned(start, size):
    return pl.ds(start if isinstance(start, int) else pl.multiple_of(start, size), size)


def _pair_lanes(pp):
    return _aligned(pp * 2 * HEAD_DIM, 2 * HEAD_DIM)


def _head_rows(h):
    return _aligned(h * HEAD_DIM, HEAD_DIM)


def _only_head(block, hb):
    lane = lax.broadcasted_iota(jnp.int32, block.shape, 1)
    return jnp.where((lane >= HEAD_DIM) if hb else (lane < HEAD_DIM), block, jnp.zeros_like(block))


def _head_col(cols, h):
    lane = lax.broadcasted_iota(jnp.int32, cols.shape, 1)
    return jnp.sum(jnp.where(lane == h, cols, 0.0), axis=1, keepdims=True)


def _other_head(block, other, hb):
    lane = lax.broadcasted_iota(jnp.int32, block.shape, 1)
    return jnp.where((lane >= HEAD_DIM) if hb else (lane < HEAD_DIM), block, other)


def _attn_fwd(qs, k, v_t, q_bias, k_bias, t, shards):
    s = qs.shape[0]
    n = s // t
    pairs = [(i, j) for i in range(n) for j in range(i + 1)]
    it = jnp.asarray(np.array([p[0] for p in pairs], np.int32))
    jt = jnp.asarray(np.array([p[1] for p in pairs], np.int32))
    nw = len(shards)
    last = len(pairs) - 1
    mid = (2 * len(pairs)) // 3

    def body(it_ref, jt_ref, q_ref, k_ref, vt_ref, qb_ref, kb_ref, *rest):
        sh, (o_ref, lse_ref), got = rest[:nw], rest[nw:nw + 2], rest[nw + 2:2 * nw + 2]
        m_sc, l_sc, acc_sc, send, recv = rest[2 * nw + 2:]
        p = pl.program_id(0)
        i = it_ref[p]
        j = jt_ref[p]
        gather_start, gather_forward, gather_finish = _gather_stages(sh, got, send, recv)
        pl.when(p == 0)(gather_start)
        if mid < last:
            pl.when(p == mid)(gather_forward)

        @pl.when(j == 0)
        def _():
            m_sc[...] = jnp.full_like(m_sc, -1e30)
            l_sc[...] = jnp.ones_like(l_sc)
            acc_sc[...] = jnp.zeros_like(acc_sc)

        def pair_step(pp, diagonal):
            lanes = _pair_lanes(pp)
            kp = k_ref[:, lanes]
            qp = q_ref[:, lanes]
            kb = kb_ref[:, lanes]
            qb = qb_ref[:, lanes]
            for hb in range(2):
                h = 2 * pp + hb
                row = pl.ds(h, 1)
                rows = _head_rows(h)
                st = _dot_nt(_other_head(kp, kb, hb), _other_head(qp, qb, hb))
                if diagonal:
                    kpos = lax.broadcasted_iota(jnp.int32, (t, t), 0)
                    qpos = lax.broadcasted_iota(jnp.int32, (t, t), 1)
                    st = jnp.where(kpos <= qpos, st, -1e30)
                m_prev = m_sc[row, :]
                m_new = jnp.maximum(m_prev, jnp.max(st, axis=0, keepdims=True))
                alpha = jnp.exp(m_prev - m_new)
                pt = jnp.exp(st - m_new)
                l_sc[row, :] = alpha * l_sc[row, :] + jnp.sum(pt, axis=0, keepdims=True)
                acc_sc[rows, :] = acc_sc[rows, :] * alpha + _dot(vt_ref[rows, :], pt.astype(MXU_DTYPE))
                m_sc[row, :] = m_new

        @pl.when(j < i)
        def _():
            for pp in range(N_HEADS // 2):
                pair_step(pp, False)

        @pl.when(j == i)
        def _():
            for pp in range(N_HEADS // 2):
                pair_step(pp, True)
                sub = lax.broadcasted_iota(jnp.int32, (2 * HEAD_DIM, t), 0)
                l_pair = jnp.where(sub < HEAD_DIM, l_sc[pl.ds(2 * pp, 1), :], l_sc[pl.ds(2 * pp + 1, 1), :])
                o_t = acc_sc[_aligned(pp * 2 * HEAD_DIM, 2 * HEAD_DIM), :] / l_pair
                o_ref[:, _pair_lanes(pp)] = o_t.T

            lse_ref[...] = m_sc[...] + jnp.log(l_sc[...])

        @pl.when(p == last)
        def _():
            if mid >= last:
                gather_forward()
            gather_finish()

    gs = pltpu.PrefetchScalarGridSpec(
        num_scalar_prefetch=2, grid=(len(pairs),),
        in_specs=[pl.BlockSpec((t, ATTN_W), lambda p, it_, jt_: (it_[p], 0)),
                  pl.BlockSpec((t, ATTN_W), lambda p, it_, jt_: (jt_[p], 0)),
                  pl.BlockSpec((ATTN_W, t), lambda p, it_, jt_: (0, jt_[p])),
                  pl.BlockSpec((t, ATTN_W), lambda p, it_, jt_: (it_[p], 0)),
                  pl.BlockSpec((t, ATTN_W), lambda p, it_, jt_: (jt_[p], 0))] + [ANY] * nw,
        out_specs=[pl.BlockSpec((t, ATTN_W), lambda p, it_, jt_: (it_[p], 0)),
                   pl.BlockSpec((HEAD_ROWS, t), lambda p, it_, jt_: (0, it_[p]))] + [ANY] * nw,
        scratch_shapes=[pltpu.VMEM((HEAD_ROWS, t), F32), pltpu.VMEM((HEAD_ROWS, t), F32), pltpu.VMEM((ATTN_W, t), F32),
                        pltpu.SemaphoreType.DMA((nw, 6)), pltpu.SemaphoreType.DMA((nw, 6))])
    o, lse, *got = pl.pallas_call(
        body, name="attn_fwd", grid_spec=gs,
        out_shape=[jax.ShapeDtypeStruct((s, ATTN_W), F32), jax.ShapeDtypeStruct((HEAD_ROWS, s), F32)]
        + [jax.ShapeDtypeStruct((N_CHIPS,) + a.shape, a.dtype) for a in shards],
        compiler_params=_params(("arbitrary",)),
    )(it, jt, qs, k, v_t, q_bias, k_bias, *shards)
    me = 2 * lax.axis_index("x") + lax.axis_index("y")
    return o, lse, [lax.dynamic_update_index_in_dim(g, own, me, 0) for g, own in zip(got, shards)]


HALO = 16


def _halo_before(tm):
    return pl.BlockSpec((HALO, CONV_W), lambda i: (jnp.maximum(i * (tm // HALO) - 1, 0), 0))


def _shift_down(cur, prev_ref, first, cols=slice(None)):
    row = lax.broadcasted_iota(jnp.int32, cur.shape, 0)

    def before(r):
        prod = prev_ref[0][r:r + 1, cols].astype(F32) * prev_ref[1][r:r + 1, cols].astype(F32)
        return jnp.where(first, 0.0, prod)

    p7, p6 = before(HALO - 1), before(HALO - 2)
    s1 = jnp.where(row == 0, p7, pltpu.roll(cur, 1, 0))
    s2 = jnp.where(row == 0, p6, jnp.where(row == 1, p7, pltpu.roll(cur, 2, 0)))
    return s1, s2


def _group_ms(v, gmat):
    return _dot_exact(v, gmat, 1) * (1.0 / HEAD_DIM)


def _mixer_fwd(x, o_attn, gate_b, gate_c, u, conv_w, g_attn, g_conv, w_out, g_post, gmat, tm):
    s = x.shape[0]

    def body(x_ref, o_ref, b_ref, c_ref, u_ref, cp_ref, up_ref, cw_ref, ga_ref, gc_ref, wo_ref, gp_ref, gm_ref,
             x2_ref, mg_ref, y_ref, z_ref):
        i = pl.program_id(0)
        gm = gm_ref[0:128, 0:128]
        for lo in range(0, ATTN_W, 128):
            cols = slice(lo, lo + 128)
            cu = c_ref[:, cols].astype(F32) * u_ref[:, cols].astype(F32)
            cu1, cu2 = _shift_down(cu, (cp_ref, up_ref), i == 0, cols)
            z = cw_ref[0:1, cols] * cu2 + cw_ref[1:2, cols] * cu1 + cw_ref[2:3, cols] * cu
            z_ref[:, cols] = z
            cv = b_ref[:, cols].astype(F32) * z
            ov = o_ref[:, cols]
            mg_ref[:, cols] = ((ov * lax.rsqrt(_group_ms(ov * ov, gm) + EPS)) * ga_ref[:, cols]).astype(MXU_DTYPE)
            mg_ref[:, ATTN_W + lo:ATTN_W + lo + 128] = (
                (cv * lax.rsqrt(_group_ms(cv * cv, gm) + EPS)) * gc_ref[:, cols]).astype(MXU_DTYPE)
        y = _dot(mg_ref[...], wo_ref[...])
        y_ref[...] = y
        x2_ref[...] = x_ref[...] + (y * _rms(y)) * gp_ref[...]

    halo = _halo_before(tm)
    sd = jax.ShapeDtypeStruct
    return pl.pallas_call(
        body, name="mixer_fwd", grid=(s // tm,),
        in_specs=[_tok(tm, D_MODEL), _tok(tm, 512), _tok(tm, 512), _tok(tm, 512), _tok(tm, 512), halo, halo,
                  _whole((3, 512)), _whole((1, 512)), _whole((1, 512)), _whole((D_MODEL, D_MODEL), single=True),
                  _whole((1, D_MODEL)), _whole((512, 512))],
        out_specs=[_tok(tm, D_MODEL), _tok(tm, D_MODEL), _tok(tm, D_MODEL), _tok(tm, 512)],
        out_shape=[sd((s, D_MODEL), F32), sd((s, D_MODEL), MXU_DTYPE), sd((s, D_MODEL), F32), sd((s, 512), F32)],
        compiler_params=_params(("arbitrary",)),
    )(x, o_attn, gate_b, gate_c, u, gate_c, u, conv_w, g_attn, g_conv, w_out, g_post, gmat)


def _ffn_fwd(x2, target, g_pre, w_gu, w_dn, g_post, tm):
    s = x2.shape[0]

    def body(x_ref, t_ref, gpre_ref, wgu_ref, wdn_ref, gpost_ref,
             h_ref, g_ref, up_ref, a_ref, ff_ref, dout_ref, loss_ref):
        xv = x_ref[...]
        h = ((xv * _rms(xv)) * gpre_ref[...]).astype(MXU_DTYPE)
        h_ref[...] = h
        ff = jnp.zeros((tm, D_MODEL), F32)
        for j in range(2):
            cols = slice(j * FF_PIECE, (j + 1) * FF_PIECE)
            g = _dot(h, wgu_ref[j])
            up = _dot(h, wgu_ref[2 + j])
            a = ((g * jax.nn.sigmoid(g)) * up).astype(MXU_DTYPE)
            g_ref[:, cols] = g.astype(MXU_DTYPE)
            up_ref[:, cols] = up.astype(MXU_DTYPE)
            a_ref[:, cols] = a
            ff = ff + _dot(a, wdn_ref[j])
        ff_ref[...] = ff
        err = (xv + (ff * _rms(ff)) * gpost_ref[...]) - t_ref[...]
        dout_ref[...] = err * (1.0 / D_MODEL)
        part = jnp.sum(jnp.mean(err * err, axis=-1, keepdims=True), axis=0, keepdims=True)

        @pl.when(pl.program_id(0) == 0)
        def _():
            loss_ref[...] = jnp.zeros_like(loss_ref)

        loss_ref[...] += part

    sd = jax.ShapeDtypeStruct
    return pl.pallas_call(
        body, name="ffn_fwd", grid=(s // tm,),
        in_specs=[_tok(tm, D_MODEL), _tok(tm, D_MODEL), _whole((1, D_MODEL)),
                  _whole((4, D_MODEL, FF_PIECE), single=True), _whole((2, FF_PIECE, D_MODEL), single=True),
                  _whole((1, D_MODEL))],
        out_specs=[_tok(tm, D_MODEL), _tok(tm, D_FF), _tok(tm, D_FF), _tok(tm, D_FF), _tok(tm, D_MODEL),
                   _tok(tm, D_MODEL), _whole((8, 128))],
        out_shape=[sd((s, D_MODEL), MXU_DTYPE), sd((s, D_FF), MXU_DTYPE), sd((s, D_FF), MXU_DTYPE),
                   sd((s, D_FF), MXU_DTYPE), sd((s, D_MODEL), F32), sd((s, D_MODEL), F32), sd((8, 128), F32)],
        compiler_params=_params(("arbitrary",)),
    )(x2, target, g_pre, w_gu, w_dn, g_post)


def _norm_bwd(dy, normed, rinv, gain):
    t = dy * gain
    return rinv * (t - normed * jnp.mean(t * normed, axis=-1, keepdims=True))


def _acc_rows(ref, first, val):
    @pl.when(first)
    def _():
        ref[...] = jnp.zeros_like(ref)

    ref[...] += jnp.sum(val, axis=0, keepdims=True)


def _ffn_bwd(dout, ff, x2, g, up, g_post, g_pre, w_gu, w_dn, tm):
    s = x2.shape[0]

    def body(do_ref, ff_ref, x_ref, g_ref, up_ref, gpost_ref, gpre_ref, wgu_ref, wdn_ref,
             dx_ref, dff_ref, dgu_ref, dgpost_ref, dgpre_ref):
        first = pl.program_id(0) == 0
        ffv = ff_ref[...]
        rf = _rms(ffv)
        n = ffv * rf
        do = do_ref[...]
        _acc_rows(dgpost_ref, first, do * n)
        dff = _norm_bwd(do, n, rf, gpost_ref[...]).astype(MXU_DTYPE)
        dff_ref[...] = dff
        dh = jnp.zeros((tm, D_MODEL), F32)
        for j in range(2):
            cols = slice(j * FF_PIECE, (j + 1) * FF_PIECE)
            da = _dot_nt(dff, wdn_ref[j])
            gv = g_ref[:, cols].astype(F32)
            sg = jax.nn.sigmoid(gv)
            dg = (da * up_ref[:, cols].astype(F32) * (sg * (1.0 + gv * (1.0 - sg)))).astype(MXU_DTYPE)
            du = (da * (gv * sg)).astype(MXU_DTYPE)
            dgu_ref[:, cols] = dg
            dgu_ref[:, D_FF + j * FF_PIECE:D_FF + (j + 1) * FF_PIECE] = du
            dh = dh + _dot_nt(dg, wgu_ref[j]) + _dot_nt(du, wgu_ref[2 + j])
        xv = x_ref[...]
        r2 = _rms(xv)
        nx = xv * r2
        _acc_rows(dgpre_ref, first, dh * nx)
        dx_ref[...] = do + _norm_bwd(dh, nx, r2, gpre_ref[...])

    sd = jax.ShapeDtypeStruct
    return pl.pallas_call(
        body, name="ffn_bwd", grid=(s // tm,),
        in_specs=[_tok(tm, D_MODEL), _tok(tm, D_MODEL), _tok(tm, D_MODEL), _tok(tm, D_FF), _tok(tm, D_FF),
                  _whole((1, D_MODEL)), _whole((1, D_MODEL)),
                  _whole((4, D_MODEL, FF_PIECE), single=True), _whole((2, FF_PIECE, D_MODEL), single=True)],
        out_specs=[_tok(tm, D_MODEL), _tok(tm, D_MODEL), _tok(tm, 2 * D_FF), _whole((1, D_MODEL)),
                   _whole((1, D_MODEL))],
        out_shape=[sd((s, D_MODEL), F32), sd((s, D_MODEL), MXU_DTYPE), sd((s, 2 * D_FF), MXU_DTYPE),
                   sd((1, D_MODEL), F32), sd((1, D_MODEL), F32)],
        compiler_params=_params(("arbitrary",)),
    )(dout, ff, x2, g, up, g_post, g_pre, w_gu, w_dn)


def _tn_matmul(a, b, tm, tn, tk, name):
    s, m = a.shape
    n = b.shape[1]

    def body(a_ref, b_ref, o_ref):
        @pl.when(pl.program_id(2) == 0)
        def _():
            o_ref[...] = jnp.zeros_like(o_ref)

        o_ref[...] += lax.dot_general(a_ref[...], b_ref[...], (((0,), (0,)), ((), ())), preferred_element_type=F32)

    return pl.pallas_call(
        body, name=name, grid=(m // tm, n // tn, s // tk),
        in_specs=[pl.BlockSpec((tk, tm), lambda i, j, kk: (kk, i)), pl.BlockSpec((tk, tn), lambda i, j, kk: (kk, j))],
        out_specs=pl.BlockSpec((tm, tn), lambda i, j, kk: (i, j)),
        out_shape=jax.ShapeDtypeStruct((m, n), F32),
        compiler_params=_params(("arbitrary", "arbitrary", "arbitrary")),
    )(a, b)


def _mixer_bwd(dx2, y, o_attn, gate_b, z, g_post, g_attn, g_conv, w_out, gmat, sel, tm, ready, kinds):
    s = dx2.shape[0]
    nw = len(ready)
    nt = s // tm

    def body(d_ref, y_ref, o_ref, b_ref, z_ref, gp_ref, ga_ref, gc_ref, wo_ref, gm_ref, sel_ref, *rest):
        grads = rest[:nw]
        dy_ref, do_ref, db_ref, dz_ref, delta_ref, dgp_ref, dga_ref, dgc_ref = rest[nw:nw + 8]
        taken = rest[nw + 8:2 * nw + 8]
        send, recv = rest[2 * nw + 8:]
        first = pl.program_id(0) == 0
        pair_start, pair_finish = _pair_stages(grads, kinds, taken, send, recv)
        pl.when(first)(pair_start)
        yv = y_ref[...]
        ry = _rms(yv)
        ny = yv * ry
        d = d_ref[...]
        _acc_rows(dgp_ref, first, d * ny)
        dy = _norm_bwd(d, ny, ry, gp_ref[...]).astype(MXU_DTYPE)
        dy_ref[...] = dy
        dm = _dot_nt(dy, wo_ref[...])
        gm = gm_ref[0:128, 0:128]

        @pl.when(first)
        def _():
            dga_ref[...] = jnp.zeros_like(dga_ref)
            dgc_ref[...] = jnp.zeros_like(dgc_ref)

        def group_bwd(val, dmv, gain_ref, dg_ref, cols):
            rg = lax.rsqrt(_group_ms(val * val, gm) + EPS)
            nv = val * rg
            dg_ref[:, cols] += jnp.sum(dmv * nv, axis=0, keepdims=True)
            t = dmv * gain_ref[:, cols]
            return rg * (t - nv * _group_ms(t * nv, gm))

        delta = jnp.zeros((tm, 128), F32)
        for lo in range(0, ATTN_W, 128):
            cols = slice(lo, lo + 128)
            ov = o_ref[:, cols]
            d_o = group_bwd(ov, dm[:, cols], ga_ref, dga_ref, cols)
            do_ref[:, cols] = d_o.astype(MXU_DTYPE)
            delta = delta + _dot_exact(d_o * ov, sel_ref[cols, :], 2)
            zv = z_ref[:, cols]
            bv = b_ref[:, cols].astype(F32)
            d_cv = group_bwd(bv * zv, dm[:, ATTN_W + lo:ATTN_W + lo + 128], gc_ref, dgc_ref, cols)
            db_ref[:, cols] = (d_cv * zv).astype(MXU_DTYPE)
            dz_ref[:, cols] = d_cv * bv
        delta_ref[...] = delta.T[0:HEAD_ROWS, :]
        pl.when(pl.program_id(0) == nt - 1)(pair_finish)

    sd = jax.ShapeDtypeStruct
    taken_shape = [sd((N_CHIPS, g.shape[-2], g.shape[-1] if kd == "rows" else g.shape[-1] // N_CHIPS), F32)
                   for g, kd in zip(ready, kinds)]
    out = pl.pallas_call(
        body, name="mixer_bwd", grid=(nt,),
        in_specs=[_tok(tm, D_MODEL), _tok(tm, D_MODEL), _tok(tm, 512), _tok(tm, 512), _tok(tm, 512),
                  _whole((1, D_MODEL)), _whole((1, 512)), _whole((1, 512)),
                  _whole((D_MODEL, D_MODEL), single=True), _whole((512, 512)), _whole((512, 128))] + [ANY] * nw,
        out_specs=[_tok(tm, D_MODEL), _tok(tm, 512), _tok(tm, 512), _tok(tm, 512), _feat(HEAD_ROWS, tm),
                   _whole((1, D_MODEL)), _whole((1, 512)), _whole((1, 512))] + [ANY] * nw,
        out_shape=[sd((s, D_MODEL), MXU_DTYPE), sd((s, 512), MXU_DTYPE), sd((s, 512), MXU_DTYPE), sd((s, 512), F32),
                   sd((HEAD_ROWS, s), F32), sd((1, D_MODEL), F32), sd((1, 512), F32), sd((1, 512), F32)] + taken_shape,
        scratch_shapes=[pltpu.SemaphoreType.DMA((nw, N_CHIPS)), pltpu.SemaphoreType.DMA((nw, N_CHIPS))],
        compiler_params=_params(("arbitrary",)),
    )(dx2, y, o_attn, gate_b, z, g_post, g_attn, g_conv, w_out, gmat, sel, *ready)
    return out[:8], out[8:]


def _attn_bwd(qs, k, k_t, v, do, c_rows, c_cols, lse, delta, t, parts):
    s = qs.shape[0]
    n = s // t
    pairs = [(i, j) for j in range(n) for i in range(j, n)]
    it = jnp.asarray(np.array([p[0] for p in pairs], np.int32))
    jt = jnp.asarray(np.array([p[1] for p in pairs], np.int32))

    nw = len(parts)

    def body(it_ref, jt_ref, q_ref, k_ref, kt_ref, v_ref, do_ref, cq_ref, ck_ref, lse_ref, dl_ref, *rest):
        pb = rest[:nw]
        dq_ref, dk_ref, dv_ref, dc_ref, dcq_ref = rest[nw:nw + 5]
        rcv = rest[nw + 5:2 * nw + 5]
        dk_sc, dv_sc, dc_sc, send, recv = rest[2 * nw + 5:]
        p = pl.program_id(0)
        i = it_ref[p]
        j = jt_ref[p]
        chip_start, chip_finish = _chip_stages(pb, rcv, send, recv)

        @pl.when(p == 0)
        def _():
            chip_start()
            dq_ref[...] = jnp.zeros_like(dq_ref)
            dcq_ref[...] = jnp.zeros_like(dcq_ref)

        @pl.when(i == j)
        def _():
            dk_sc[...] = jnp.zeros_like(dk_sc)
            dv_sc[...] = jnp.zeros_like(dv_sc)
            dc_sc[...] = jnp.zeros_like(dc_sc)

        def pair_step(pp, diagonal):
            lanes = _pair_lanes(pp)
            qp = q_ref[:, lanes]
            kp = k_ref[:, lanes]
            vp = v_ref[:, lanes]
            dop = do_ref[:, lanes]
            ck_all = ck_ref[...]
            lane = lax.broadcasted_iota(jnp.int32, (t, 128), 1)
            for hb in range(2):
                h = 2 * pp + hb
                row = pl.ds(h, 1)
                bias = (cq_ref[row, :] - lse_ref[row, :]) - _head_col(ck_all, h)
                pt = jnp.exp(_dot_nt(_only_head(kp, hb), qp) + bias)
                if diagonal:
                    kpos = lax.broadcasted_iota(jnp.int32, (t, t), 0)
                    qpos = lax.broadcasted_iota(jnp.int32, (t, t), 1)
                    pt = jnp.where(kpos <= qpos, pt, 0.0)
                dv_sc[:, lanes] += _dot(pt.astype(MXU_DTYPE), _only_head(dop, hb))
                dst = pt * (_dot_nt(_only_head(vp, hb), dop) - dl_ref[row, :])
                dc_sc[...] -= jnp.where(lane == h, jnp.sum(dst, axis=1, keepdims=True), 0.0)
                dcq_ref[i, row, :] += jnp.sum(dst, axis=0, keepdims=True)
                dsb = dst.astype(MXU_DTYPE)
                dk_sc[:, lanes] += _dot(dsb, _only_head(qp, hb))
                rows = _head_rows(h)
                dq_ref[i, rows, :] += _dot(kt_ref[rows, :], dsb)

        @pl.when(i > j)
        def _():
            for pp in range(N_HEADS // 2):
                pair_step(pp, False)

        @pl.when(i == j)
        def _():
            for pp in range(N_HEADS // 2):
                pair_step(pp, True)

        @pl.when(i == n - 1)
        def _():
            dk_ref[...] = dk_sc[...].astype(MXU_DTYPE)
            dv_ref[...] = dv_sc[...].astype(MXU_DTYPE)
            dc_ref[...] = dc_sc[...]

        pl.when(p == len(pairs) - 1)(chip_finish)

    qi = lambda p, it_, jt_: (it_[p], 0)
    kj = lambda p, it_, jt_: (jt_[p], 0)
    row_i = lambda p, it_, jt_: (0, it_[p])
    gs = pltpu.PrefetchScalarGridSpec(
        num_scalar_prefetch=2, grid=(len(pairs),),
        in_specs=[pl.BlockSpec((t, ATTN_W), qi), pl.BlockSpec((t, ATTN_W), kj),
                  pl.BlockSpec((ATTN_W, t), lambda p, it_, jt_: (0, jt_[p])),
                  pl.BlockSpec((t, ATTN_W), kj), pl.BlockSpec((t, ATTN_W), qi),
                  pl.BlockSpec((HEAD_ROWS, t), row_i), pl.BlockSpec((t, 128), kj),
                  pl.BlockSpec((HEAD_ROWS, t), row_i), pl.BlockSpec((HEAD_ROWS, t), row_i)] + [ANY] * nw,
        out_specs=[pl.BlockSpec((n, ATTN_W, t), lambda p, it_, jt_: (0, 0, 0)),
                   pl.BlockSpec((t, ATTN_W), kj), pl.BlockSpec((t, ATTN_W), kj),
                   pl.BlockSpec((t, 128), kj),
                   pl.BlockSpec((n, HEAD_ROWS, t), lambda p, it_, jt_: (0, 0, 0))] + [ANY] * nw,
        scratch_shapes=[pltpu.VMEM((t, ATTN_W), F32), pltpu.VMEM((t, ATTN_W), F32),
                        pltpu.VMEM((t, 128), F32), pltpu.SemaphoreType.DMA((nw, 3)), pltpu.SemaphoreType.DMA((nw, 3))])
    sd = jax.ShapeDtypeStruct
    out = pl.pallas_call(
        body, name="attn_bwd", grid_spec=gs,
        out_shape=[sd((n, ATTN_W, t), F32), sd((s, ATTN_W), MXU_DTYPE), sd((s, ATTN_W), MXU_DTYPE),
                   sd((s, 128), F32), sd((n, HEAD_ROWS, t), F32)] + [sd((3,) + a.shape[1:], a.dtype) for a in parts],
        compiler_params=_params(("arbitrary",)),
    )(it, jt, qs, k, k_t, v, do, c_rows, c_cols, lse, delta, *parts)
    return out[:5], out[5:]


def _forget_bwd(dc_rows, dc_cols, z_t, b_col):
    s = z_t.shape[1]
    nb = s // 128

    def body(dr_ref, dcc_ref, z_ref, b_ref, dz_ref, db_ref):
        lower = _tri(128, False)
        real = lax.broadcasted_iota(jnp.int32, (HEAD_ROWS, 128), 0) < N_HEADS

        tail = jnp.zeros((HEAD_ROWS, 1), F32)
        dbias = jnp.zeros((HEAD_ROWS, 1), F32)
        for m in range(nb):
            off = (nb - 1 - m) * 128
            dc = dr_ref[:, off:off + 128] + dcc_ref[off:off + 128, :].T[0:HEAD_ROWS, :]
            dlf = _dot_exact(dc, lower, 3) + tail
            dz = dlf * jax.nn.sigmoid(-(z_ref[0:HEAD_ROWS, off:off + 128] + b_ref[...]))
            dz = jnp.where(real, dz, 0.0)
            dz_ref[off:off + 128, :] = _rows_to_cols(dz)
            tail = tail + jnp.sum(dc, axis=1, keepdims=True)
            dbias = dbias + jnp.sum(dz, axis=1, keepdims=True)
        db_ref[...] = jnp.broadcast_to(dbias, db_ref.shape)

    return pl.pallas_call(
        body, name="forget_bwd",
        out_shape=[jax.ShapeDtypeStruct((s, 128), F32), jax.ShapeDtypeStruct((HEAD_ROWS, 128), F32)],
        compiler_params=_params())(dc_rows, dc_cols, z_t, b_col)


def _inproj_bwd(dz, gate_c, u, conv_w, dq, dk, dv, dzf, db, x, dx2, g_pre, w_t, tm):
    s = x.shape[0]
    nt = s // tm
    t = dq.shape[2]
    assert t % tm == 0 and dq.shape[:2] == (s // t, ATTN_W)
    per = t // tm

    def body(dz_ref, dzn_ref, c_ref, u_ref, cp_ref, up_ref, cw_ref, dq_ref, dk_ref, dv_ref, dzf_ref, db_ref,
             x_ref, dx2_ref, g_ref, w_ref, gx_ref, dp_ref, dg_ref, dcw_ref):
        i = pl.program_id(0)
        first = i == 0
        last = i == nt - 1
        @pl.when(first)
        def _():
            dcw_ref[...] = jnp.zeros_like(dcw_ref)

        for lo in range(0, CONV_W, 128):
            cols = slice(lo, lo + 128)
            dzv = dz_ref[:, cols]
            row = lax.broadcasted_iota(jnp.int32, dzv.shape, 0)
            n0 = jnp.where(last, 0.0, dzn_ref[0:1, cols])
            n1 = jnp.where(last, 0.0, dzn_ref[1:2, cols])
            dz1 = jnp.where(row == tm - 1, n0, pltpu.roll(dzv, tm - 1, 0))
            dz2 = jnp.where(row == tm - 1, n1, jnp.where(row == tm - 2, n0, pltpu.roll(dzv, tm - 2, 0)))
            dcu = cw_ref[2:3, cols] * dzv + cw_ref[1:2, cols] * dz1 + cw_ref[0:1, cols] * dz2
            cv = c_ref[:, cols].astype(F32)
            uv = u_ref[:, cols].astype(F32)
            cu = cv * uv
            cu1, cu2 = _shift_down(cu, (cp_ref, up_ref), first, cols)
            dcw_ref[0:1, cols] += jnp.sum(dzv * cu2, axis=0, keepdims=True)
            dcw_ref[1:2, cols] += jnp.sum(dzv * cu1, axis=0, keepdims=True)
            dcw_ref[2:3, cols] += jnp.sum(dzv * cu, axis=0, keepdims=True)
            dp_ref[:, OFF_C + lo:OFF_C + lo + 128] = (dcu * uv).astype(MXU_DTYPE)
            dp_ref[:, OFF_U + lo:OFF_U + lo + 128] = (dcu * cv).astype(MXU_DTYPE)

        dp_ref[:, 0:512] = (dq_ref[0].T * Q_SCALE).astype(MXU_DTYPE)
        dp_ref[:, 512:1024] = dk_ref[...].astype(MXU_DTYPE)
        dp_ref[:, 1024:OFF_F] = dv_ref[...].astype(MXU_DTYPE)
        dp_ref[:, OFF_F:OFF_B] = dzf_ref[...].astype(MXU_DTYPE)
        dp_ref[:, OFF_B:OFF_C] = db_ref[...].astype(MXU_DTYPE)
        dh = _dot(dp_ref[...], w_ref[...])
        xv = x_ref[...]
        r1 = _rms(xv)
        nx = xv * r1
        _acc_rows(dg_ref, first, dh * nx)
        gx_ref[...] = dx2_ref[...] + _norm_bwd(dh, nx, r1, g_ref[...])

    prev = _halo_before(tm)
    nxt = pl.BlockSpec((8, 512), lambda i: (jnp.minimum((i + 1) * (tm // 8), s // 8 - 1), 0))
    sd = jax.ShapeDtypeStruct
    return pl.pallas_call(
        body, name="inproj_bwd", grid=(nt,),
        in_specs=[_tok(tm, 512), nxt, _tok(tm, 512), _tok(tm, 512), prev, prev, _whole((3, 512)),
                  pl.BlockSpec((1, ATTN_W, tm), lambda i: (i // per, 0, i % per)), _tok(tm, 512), _tok(tm, 512),
                  _tok(tm, 128),
                  _tok(tm, 512),
                  _tok(tm, D_MODEL), _tok(tm, D_MODEL), _whole((1, D_MODEL)), _whole((IN_PAD, D_MODEL), single=True)],
        out_specs=[_tok(tm, D_MODEL), _tok(tm, IN_PAD), _whole((1, D_MODEL)), _whole((8, 512))],
        out_shape=[sd((s, D_MODEL), F32), sd((s, IN_PAD), MXU_DTYPE), sd((1, D_MODEL), F32), sd((8, 512), F32)],
        compiler_params=_params(("arbitrary",)),
    )(dz, dz, gate_c, u, gate_c, u, conv_w, dq, dk, dv, dzf, db, x, dx2, g_pre, w_t)


def _tile(s, want):
    return want if s % want == 0 else s


def _halves(a):
    return a.reshape(2, a.shape[0] // 2, a.shape[1])


def _device_step(x, target, w, mom1, mom2, w_in_t, m_in_t, v_in_t, c_idx, me_idx):
    s = x.shape[0]
    tm = _tile(s, 512)
    tf = _tile(s, 256)
    ta = _tile(s, 512)
    tkk = _tile(s, 2048)
    gidx = np.arange(512) // HEAD_DIM
    gmat = jnp.asarray(gidx[:, None] == gidx[None, :], MXU_DTYPE)
    sel = jnp.asarray(gidx[:, None] == np.arange(128)[None, :], MXU_DTYPE)
    g_mix_pre, g_mix_post, g_ffn_pre, g_ffn_post = w["g_mix_pre"], w["g_mix_post"], w["g_ffn_pre"], w["g_ffn_post"]
    g_attn, g_conv, b_forget = w["g_attn_out"], w["g_conv_out"], w["b_forget"]
    shard = {n: _halves(w[n][0].astype(MXU_DTYPE)) for n in BIG[1:]}
    piece_rows = IN_W // N_CHIPS

    g_in, conv_all = _gather_weights([w_in_t.reshape(piece_rows, D_MODEL).astype(MXU_DTYPE)], w["conv_w"][0])
    w_rows = g_in.reshape(IN_W, D_MODEL)
    w_t = jnp.concatenate([w_rows[:OFF_F + N_HEADS], jnp.zeros((OFF_B - OFF_F - N_HEADS, D_MODEL), MXU_DTYPE),
                           w_rows[OFF_F + N_HEADS:]], axis=0)
    conv_w = jnp.transpose(conv_all, (1, 0, 2)).reshape(3, CONV_W)

    h1, qs, k, v, k_t, v_t, z_t, gate_b, gate_c, u = _inproj_fwd(x, g_mix_pre, w_t, tm)
    b_col = jnp.pad(jnp.transpose(b_forget), ((0, HEAD_ROWS - N_HEADS), (0, 0)))
    c_rows, c_cols, q_bias, k_bias = _forget_fwd(z_t, b_col)
    o_attn, lse, (g_out, g_gu, g_dn) = _attn_fwd(qs, k, v_t, q_bias, k_bias, ta,
                                                 [shard["w_out"], shard["w_gate_up"], shard["w_down"]])
    w_out = g_out.reshape(D_MODEL, D_MODEL)
    w_gu = g_gu.reshape(N_CHIPS, D_MODEL, FF_PIECE)
    w_dn = g_dn.reshape(2, FF_PIECE, D_MODEL)
    x2, merged, y, z = _mixer_fwd(x, o_attn, gate_b, gate_c, u, conv_w, g_attn, g_conv, w_out, g_mix_post, gmat, tm)
    h2, g, up, a, ff, dout, loss_acc = _ffn_fwd(x2, target, g_ffn_pre, w_gu, w_dn, g_ffn_post, tf)

    dx2, dff, dgu, dg_ffn_post, dg_ffn_pre = _ffn_bwd(dout, ff, x2, g, up, g_ffn_post, g_ffn_pre, w_gu, w_dn, tf)
    dw_dn = _tn_matmul(a, dff, FF_PIECE, 1024, tkk, "dw_down").reshape(N_CHIPS, 2, D_FF // (2 * N_CHIPS), D_MODEL)
    dw_gu = _tn_matmul(h2, dgu, 1024, FF_PIECE, tkk, "dw_gate_up").reshape(2, D_MODEL // 2, 2 * D_FF)
    (dy, d_o, d_b, dz, delta, dg_mix_post, dg_attn, dg_conv), (a_gu, a_dn) = _mixer_bwd(
        dx2, y, o_attn, gate_b, z, g_mix_post, g_attn, g_conv, w_out, gmat, sel, tm, [dw_gu, dw_dn], ["cols", "rows"])
    dw_out = _tn_matmul(merged, dy, 1024, 1024, tkk, "dw_out")
    place = jnp.concatenate([c_idx, me_idx])
    sum_gu = _pair_sum(place, dw_gu, "cols", a_gu, "pair_sum_w_gate_up")
    sum_dn = _pair_sum(place, dw_dn, "rows", a_dn, "pair_sum_w_down")
    (dq_t, dk, dv, dc_cols, dcq), (r_gu, r_dn) = _attn_bwd(
        qs, k, k_t, v, d_o, c_rows, c_cols, lse, delta, ta, [sum_gu[1], sum_dn[1]])
    dc_rows = jnp.transpose(dcq, (1, 0, 2)).reshape(HEAD_ROWS, s)
    dzf, db_f = _forget_bwd(dc_rows, dc_cols, z_t, b_col)
    grad_x, dproj, dg_mix_pre, dcw = _inproj_bwd(dz, gate_c, u, conv_w, dq_t, dk, dv, dzf, d_b,
                                                 x, dx2, g_mix_pre, w_t, tm)
    dw_t = _tn_matmul(dproj, h1, 640, 1024, tkk, "dw_in")
    dw_in = jnp.concatenate([dw_t[:OFF_F + N_HEADS], dw_t[OFF_B:]], axis=0).reshape(N_CHIPS, piece_rows, D_MODEL)
    dw_out = dw_out.reshape(N_CHIPS, 2, D_MODEL // (2 * N_CHIPS), D_MODEL)

    a_in, a_out = _pair_exchange([dw_in, dw_out], ["lanes", "rows"])
    sum_in = _pair_sum(place, dw_in, "lanes", a_in, "pair_sum_w_in")
    sum_out = _pair_sum(place, dw_out, "rows", a_out, "pair_sum_w_out")
    small = dict(b_forget=db_f[:N_HEADS, 0], g_attn_out=dg_attn, g_conv_out=dg_conv, g_mix_pre=dg_mix_pre,
                 g_mix_post=dg_mix_post, g_ffn_pre=dg_ffn_pre, g_ffn_post=dg_ffn_post)
    (r_in, r_out), small_all = _chip_exchange([sum_in[1], sum_out[1]], _pack_small(small, dcw[:3], loss_acc[0, 0]))
    totals = [_chip_sum(sb[0], r, "chip_sum_" + n)
              for n, sb, r in zip(BIG, (sum_in, sum_out, sum_gu, sum_dn), (r_in, r_out, r_gu, r_dn))]
    shared = _pair_share(totals, "pair_share")
    new = {"w_in": _adamw_lanes(c_idx, w_in_t, totals[0], shared[0], m_in_t, v_in_t, "adamw_w_in")}
    for n, mine, theirs in list(zip(BIG, totals, shared))[1:]:
        new[n] = _adamw(c_idx, w[n][0], mine, theirs, mom1[n][0], mom2[n][0], 2, "adamw_" + n)
    return grad_x, new, small_all


BIG = ("w_in", "w_out", "w_gate_up", "w_down")
ANY = pl.BlockSpec(memory_space=pl.ANY)


def _place():
    x, y, c = lax.axis_index("x"), lax.axis_index("y"), lax.axis_index("c")
    others = [(1 - x, y), (x, 1 - y), (1 - x, 1 - y)]
    return x, y, c, 2 * x + y, others, [2 * px + py for px, py in others]


def _remote(src, dst, send, recv, dev):
    return pltpu.make_async_remote_copy(src_ref=src, dst_ref=dst, send_sem=send, recv_sem=recv,
                                        device_id=dev, device_id_type=MESH_ID)


def _gather_stages(sh, outs, send, recv):
    x, y, c, me, others, chips = _place()
    sib = (x, y, 1 - c)
    every = [(w, kk) for w in range(len(sh)) for kk in range(3)]

    def half_of(ref, half, piece=None):
        ref = ref if piece is None else ref.at[piece]
        if len(ref.shape) == 3:
            return ref.at[half]
        hc = ref.shape[1] // 2
        return ref.at[:, pl.ds(pl.multiple_of(half * hc, 128), hc)]

    def first(w, kk):
        return _remote(half_of(sh[w], c), half_of(outs[w], c, me), send.at[w, kk], recv.at[w, kk], (*others[kk], c))

    def landed(w, kk):
        r = half_of(outs[w], c, chips[kk])
        return _remote(r, r, send.at[w, kk], recv.at[w, kk], (*others[kk], c))

    def onward(w, kk, half):
        r = half_of(outs[w], half, chips[kk])
        return _remote(r, r, send.at[w, 3 + kk], recv.at[w, 3 + kk], sib)

    def start():
        for w, kk in every:
            first(w, kk).start()

    def forward():
        for w, kk in every:
            landed(w, kk).wait_recv()
            onward(w, kk, c).start()

    def finish():
        for w, kk in every:
            onward(w, kk, 1 - c).wait_recv()
        for w, kk in every:
            first(w, kk).wait_send()
            onward(w, kk, c).wait_send()

    return start, forward, finish


def _pair_piece(ref, kind, p, half):
    if kind == "rows":
        return ref.at[p, half]
    if kind == "lanes":
        hc = ref.shape[2] // 2
        return ref.at[p, :, pl.ds(pl.multiple_of(half * hc, 128), hc)]
    cols = ref.shape[2] // N_CHIPS
    return ref.at[half, :, pl.ds(p * cols, cols)]


def _pair_stages(g, kinds, a, send, recv):
    x, y, c, _, _, _ = _place()
    copies = [_remote(_pair_piece(g[w], kinds[w], p, 1 - c), a[w].at[p], send.at[w, p], recv.at[w, p], (x, y, 1 - c))
              for w in range(len(g)) for p in range(N_CHIPS)]

    def start():
        for cp in copies:
            cp.start()

    def finish():
        for cp in copies:
            cp.wait()

    return start, finish


def _chip_stages(pb, rcv, send, recv):
    x, y, c, _, others, chips = _place()
    copies = [_remote(pb[w].at[chips[kk]], rcv[w].at[kk], send.at[w, kk], recv.at[w, kk], (*others[kk], c))
              for w in range(len(pb)) for kk in range(3)]

    def start():
        for cp in copies:
            cp.start()

    def finish():
        for cp in copies:
            cp.wait()

    return start, finish


def _gather_weights(shards, conv_w):
    n = len(shards)

    def body(*refs):
        sh, cw, outs, cwo = refs[:n], refs[n], refs[n + 1:2 * n + 1], refs[2 * n + 1]
        send, recv = refs[2 * n + 2:]
        x, y, c, me, others, chips = _place()
        start, forward, finish = _gather_stages(sh, outs, send, recv)
        start()
        small = [_remote(cw, cwo.at[me], send.at[n, kk], recv.at[n, kk], (*others[kk], c)) for kk in range(3)]
        for cp in small:
            cp.start()
        forward()
        for kk in range(3):
            _remote(cw, cwo.at[chips[kk]], send.at[n, kk], recv.at[n, kk], (*others[kk], c)).wait_recv()
        finish()
        for cp in small:
            cp.wait_send()

    out_shape = [jax.ShapeDtypeStruct((N_CHIPS,) + s.shape, s.dtype) for s in shards]
    out_shape.append(jax.ShapeDtypeStruct((N_CHIPS,) + conv_w.shape, conv_w.dtype))
    got = pl.pallas_call(
        body, name="gather_weights", in_specs=[ANY] * (n + 1), out_specs=[ANY] * (n + 1), out_shape=out_shape,
        scratch_shapes=[pltpu.SemaphoreType.DMA((n + 1, 6)), pltpu.SemaphoreType.DMA((n + 1, 6))],
    )(*shards, conv_w)
    me = 2 * lax.axis_index("x") + lax.axis_index("y")
    return [lax.dynamic_update_index_in_dim(g, own, me, 0) for g, own in zip(got, list(shards) + [conv_w])]


def _taken_shape(g, kind):
    if kind == "rows":
        return (N_CHIPS,) + g.shape[2:]
    if kind == "lanes":
        return g.shape[:2] + (g.shape[2] // 2,)
    return (N_CHIPS, g.shape[1], g.shape[2] // N_CHIPS)


def _pair_sum(place, g, kind, a, name):
    _, half, cols = a.shape
    if kind == "rows":
        mine = pl.BlockSpec((1, 1, half, cols), lambda p, pr: (p, pr[0], 0, 0))
    elif kind == "lanes":
        mine = pl.BlockSpec((1, half, cols), lambda p, pr: (p, 0, pr[0]))
    else:
        mine = pl.BlockSpec((1, half, cols), lambda p, pr: (pr[0], 0, p))

    def body(place_ref, g_ref, a_ref, own_ref, pb_ref):
        tot = (g_ref[0, 0] if kind == "rows" else g_ref[0]) + a_ref[0]
        pb_ref[0] = tot.astype(BF16)

        @pl.when(pl.program_id(0) == place_ref[1])
        def _():
            own_ref[...] = tot

    gs = pltpu.PrefetchScalarGridSpec(
        num_scalar_prefetch=1, grid=(N_CHIPS,),
        in_specs=[mine,
                  pl.BlockSpec((1, half, cols), lambda p, pr: (p, 0, 0))],
        out_specs=[pl.BlockSpec((half, cols), lambda p, pr: (0, 0)),
                   pl.BlockSpec((1, half, cols), lambda p, pr: (p, 0, 0))])
    return pl.pallas_call(
        body, name=name, grid_spec=gs,
        out_shape=[jax.ShapeDtypeStruct((half, cols), F32), jax.ShapeDtypeStruct((N_CHIPS, half, cols), BF16)],
        compiler_params=_params(("arbitrary",)),
    )(place, g, a)


def _chip_sum(own, rcv, name):
    half, cols = own.shape

    def body(o_ref, r_ref, t_ref):
        t_ref[...] = ((o_ref[...] + r_ref[0].astype(F32)) + r_ref[1].astype(F32)) + r_ref[2].astype(F32)

    return pl.pallas_call(
        body, name=name, grid=(1,),
        in_specs=[pl.BlockSpec((half, cols), lambda i: (0, 0)), pl.BlockSpec((3, half, cols), lambda i: (0, 0, 0))],
        out_specs=pl.BlockSpec((half, cols), lambda i: (0, 0)),
        out_shape=jax.ShapeDtypeStruct((half, cols), F32), compiler_params=_params(("arbitrary",)),
    )(own, rcv)


def _small_stages(sm, smg, send, recv):
    x, y, c, _, _, _ = _place()

    def peer(r):
        return (1 - x if r & 4 else x, 1 - y if r & 2 else y, 1 - c if r & 1 else c)

    mine = 4 * x + 2 * y + c
    copies = [_remote(sm, smg.at[mine], send.at[r - 1], recv.at[r - 1], peer(r)) for r in range(1, 8)]

    def start():
        for cp in copies:
            cp.start()

    def finish():
        for r in range(1, 8):
            px, py, pc = peer(r)
            _remote(sm, smg.at[4 * px + 2 * py + pc], send.at[r - 1], recv.at[r - 1], (px, py, pc)).wait_recv()
        for cp in copies:
            cp.wait_send()

    return start, finish


def _pair_exchange(grads, kinds):
    n = len(grads)

    def body(*refs):
        start, finish = _pair_stages(refs[:n], kinds, refs[n:2 * n], *refs[2 * n:])
        start()
        finish()

    return pl.pallas_call(
        body, name="pair_exchange", in_specs=[ANY] * n, out_specs=[ANY] * n,
        out_shape=[jax.ShapeDtypeStruct(_taken_shape(g, kd), g.dtype) for g, kd in zip(grads, kinds)],
        scratch_shapes=[pltpu.SemaphoreType.DMA((n, N_CHIPS)), pltpu.SemaphoreType.DMA((n, N_CHIPS))],
    )(*grads)


def _chip_exchange(parts, small):
    n = len(parts)

    def body(*refs):
        pb, sm, rcv, smg = refs[:n], refs[n], refs[n + 1:2 * n + 1], refs[2 * n + 1]
        send, recv, ssend, srecv = refs[2 * n + 2:]
        chip_start, chip_finish = _chip_stages(pb, rcv, send, recv)
        small_start, small_finish = _small_stages(sm, smg, ssend, srecv)
        chip_start()
        small_start()
        chip_finish()
        small_finish()

    out_shape = [jax.ShapeDtypeStruct((3,) + p.shape[1:], p.dtype) for p in parts]
    out_shape.append(jax.ShapeDtypeStruct((8,) + small.shape, small.dtype))
    *arrived, small_land = pl.pallas_call(
        body, name="chip_exchange", in_specs=[ANY] * (n + 1), out_specs=[ANY] * (n + 1), out_shape=out_shape,
        scratch_shapes=[pltpu.SemaphoreType.DMA((n, 3)), pltpu.SemaphoreType.DMA((n, 3)),
                        pltpu.SemaphoreType.DMA((7,)), pltpu.SemaphoreType.DMA((7,))],
    )(*parts, small)
    mine = 4 * lax.axis_index("x") + 2 * lax.axis_index("y") + lax.axis_index("c")
    return arrived, lax.dynamic_update_index_in_dim(small_land, small, mine, 0)


def _pair_share(totals, name):
    n = len(totals)

    def body(*refs):
        t, g = refs[:n], refs[n:2 * n]
        send, recv = refs[2 * n:]
        x, y, c, _, _, _ = _place()
        copies = [_remote(t[w], g[w], send.at[w], recv.at[w], (x, y, 1 - c)) for w in range(n)]
        for cp in copies:
            cp.start()
        for cp in copies:
            cp.wait()

    return pl.pallas_call(
        body, name=name, in_specs=[ANY] * n, out_specs=[ANY] * n,
        out_shape=[jax.ShapeDtypeStruct(t.shape, t.dtype) for t in totals],
        scratch_shapes=[pltpu.SemaphoreType.DMA((n,)), pltpu.SemaphoreType.DMA((n,))],
    )(*totals)


def _adamw_math(w, g, m, v):
    m = ADAM_B1 * m + (1.0 - ADAM_B1) * g
    v = ADAM_B2 * v + (1.0 - ADAM_B2) * (g * g)
    m_hat = m / (1.0 - ADAM_B1 ** ADAM_STEP)
    v_hat = v / (1.0 - ADAM_B2 ** ADAM_STEP)
    delta = -ADAM_LR * (m_hat / (jnp.sqrt(v_hat) + ADAM_EPS) + ADAM_WD * w)
    return delta, m, v


def _adamw(c_idx, w, mine, theirs, m, v, nb, name):
    rows, cols = w.shape
    tr = rows // (2 * nb)

    def body(c_ref, w_ref, a_ref, b_ref, m_ref, v_ref, g_ref, d_ref, nm_ref, nv_ref):
        g = jnp.where(pl.program_id(0) == c_ref[0], a_ref[...], b_ref[...])
        g_ref[...] = g
        d_ref[...], nm_ref[...], nv_ref[...] = _adamw_math(w_ref[...], g, m_ref[...], v_ref[...])

    full = pl.BlockSpec((tr, cols), lambda hh, i, cr: (hh * nb + i, 0))
    half = pl.BlockSpec((tr, cols), lambda hh, i, cr: (i, 0))
    gs = pltpu.PrefetchScalarGridSpec(num_scalar_prefetch=1, grid=(2, nb), in_specs=[full, half, half, full, full],
                                      out_specs=[full] * 4)
    return pl.pallas_call(
        body, name=name, grid_spec=gs, out_shape=[jax.ShapeDtypeStruct((rows, cols), F32)] * 4,
        compiler_params=_params(("arbitrary", "arbitrary")),
    )(c_idx, w, mine, theirs, m, v)


def _adamw_lanes(c_idx, w, mine, theirs, m, v, name):
    rows, _, cols = w.shape
    hc = cols // 2

    def body(c_ref, w_ref, a_ref, b_ref, m_ref, v_ref, g_ref, d_ref, nm_ref, nv_ref):
        g = jnp.where(pl.program_id(0) == c_ref[0], a_ref[...], b_ref[...])
        g_ref[:, 0, :] = g
        d_ref[:, 0, :], nm_ref[:, 0, :], nv_ref[:, 0, :] = _adamw_math(w_ref[:, 0, :], g, m_ref[:, 0, :], v_ref[:, 0, :])

    full = pl.BlockSpec((rows, 1, hc), lambda hh, cr: (0, 0, hh))
    half = pl.BlockSpec((rows, hc), lambda hh, cr: (0, 0))
    gs = pltpu.PrefetchScalarGridSpec(num_scalar_prefetch=1, grid=(2,), in_specs=[full, half, half, full, full],
                                      out_specs=[full] * 4)
    return pl.pallas_call(
        body, name=name, grid_spec=gs, out_shape=[jax.ShapeDtypeStruct((rows, 1, cols), F32)] * 4,
        compiler_params=_params(("arbitrary",)),
    )(c_idx, w, mine, theirs, m, v)


SMALL = ("g_mix_pre", "g_mix_post", "g_ffn_pre", "g_ffn_post")
SMALL_ALL = SMALL + ("g_attn_out", "g_conv_out", "conv_w", "b_forget")
SMALL_AT = {"g_mix_pre": (0, 0, 1024), "g_mix_post": (1, 0, 1024), "g_ffn_pre": (2, 0, 1024),
            "g_ffn_post": (3, 0, 1024), "g_attn_out": (4, 0, 512), "g_conv_out": (4, 512, 512),
            "b_forget": (7, 0, N_HEADS)}
CONV_AT = ((5, 0), (5, 512), (6, 0))
LOSS_AT = (6, 512)


def _pack_small(t, conv_full, loss_sum):
    conv = jnp.concatenate([conv_full.reshape(1, 3 * CONV_W), loss_sum.reshape(1, 1),
                            jnp.zeros((1, 2048 - 3 * CONV_W - 1), F32)], axis=1).reshape(2, 1024)
    return jnp.concatenate([t[n].reshape(1, 1024) for n in SMALL]
                           + [jnp.concatenate([t["g_attn_out"].reshape(1, 512), t["g_conv_out"].reshape(1, 512)], axis=1),
                              conv, jnp.pad(t["b_forget"].reshape(1, N_HEADS), ((0, 0), (0, 1024 - N_HEADS)))], axis=0)


def _small_update(me_idx, gathered, w, m, v):
    def body(me_ref, gg_ref, *refs):
        k = len(SMALL_ALL)
        w_refs, m_refs, v_refs = refs[:k], refs[k:2 * k], refs[2 * k:3 * k]
        loss_ref = refs[3 * k]
        outs = refs[3 * k + 1:3 * k + 1 + 4 * k]
        sums = refs[-1]
        g = gg_ref[0]
        for dev in range(1, 8):
            g = g + gg_ref[dev]
        sums[...] = g
        loss_ref[...] = sums[LOSS_AT[0]:LOSS_AT[0] + 1, LOSS_AT[1]:LOSS_AT[1] + 1]
        mine = pl.multiple_of(me_ref[0] * 128, 128)
        for idx, name in enumerate(SMALL_ALL):
            g_ref, d_ref, nm_ref, nv_ref = outs[4 * idx:4 * idx + 4]
            if name == "conv_w":
                for r, (row, lo) in enumerate(CONV_AT):
                    gr = sums[row:row + 1, pl.ds(lo + mine, 128)]
                    g_ref[0, r:r + 1, :] = gr
                    d_ref[0, r:r + 1, :], nm_ref[0, r:r + 1, :], nv_ref[0, r:r + 1, :] = _adamw_math(
                        w_refs[idx][0, r:r + 1, :], gr, m_refs[idx][0, r:r + 1, :], v_refs[idx][0, r:r + 1, :])
            else:
                row, lo, n = SMALL_AT[name]
                gr = sums[row:row + 1, lo:lo + n]
                g_ref[...] = gr
                d_ref[...], nm_ref[...], nv_ref[...] = _adamw_math(w_refs[idx][...], gr, m_refs[idx][...],
                                                                    v_refs[idx][...])

    def whole(a):
        nd = a.ndim
        return pl.BlockSpec(a.shape, lambda i, mr: (0,) * nd)

    ins = [t[n] for t in (w, m, v) for n in SMALL_ALL]
    out_shape = [jax.ShapeDtypeStruct((1, 1), F32)]
    for n in SMALL_ALL:
        out_shape += [jax.ShapeDtypeStruct(w[n].shape, F32)] * 4
    gs = pltpu.PrefetchScalarGridSpec(
        num_scalar_prefetch=1, grid=(1,), in_specs=[whole(gathered)] + [whole(a) for a in ins],
        out_specs=[whole(o) for o in out_shape], scratch_shapes=[pltpu.VMEM((8, 1024), F32)])
    out = pl.pallas_call(body, name="small_update", grid_spec=gs, out_shape=out_shape,
                         compiler_params=_params(("arbitrary",)))(me_idx, gathered, *ins)
    return out[0], {n: out[1 + 4 * i:5 + 4 * i] for i, n in enumerate(SMALL_ALL)}


def kernel(x, w_in, b_forget, conv_w, g_attn_out, g_conv_out, w_out, g_mix_pre, g_mix_post, w_gate_up, w_down, g_ffn_pre, g_ffn_post, loss_target, m_w_in, m_b_forget, m_conv_w, m_g_attn_out, m_g_conv_out, m_w_out, m_g_mix_pre, m_g_mix_post, m_w_gate_up, m_w_down, m_g_ffn_pre, m_g_ffn_post, v_w_in, v_b_forget, v_conv_w, v_g_attn_out, v_g_conv_out, v_w_out, v_g_mix_pre, v_g_mix_post, v_w_gate_up, v_w_down, v_g_ffn_pre, v_g_ffn_post):
    w = dict(w_in=w_in, b_forget=b_forget, conv_w=conv_w, g_attn_out=g_attn_out, g_conv_out=g_conv_out, w_out=w_out,
             g_mix_pre=g_mix_pre, g_mix_post=g_mix_post, w_gate_up=w_gate_up, w_down=w_down, g_ffn_pre=g_ffn_pre,
             g_ffn_post=g_ffn_post)
    m = dict(w_in=m_w_in, b_forget=m_b_forget, conv_w=m_conv_w, g_attn_out=m_g_attn_out, g_conv_out=m_g_conv_out,
             w_out=m_w_out, g_mix_pre=m_g_mix_pre, g_mix_post=m_g_mix_post, w_gate_up=m_w_gate_up, w_down=m_w_down,
             g_ffn_pre=m_g_ffn_pre, g_ffn_post=m_g_ffn_post)
    v = dict(w_in=v_w_in, b_forget=v_b_forget, conv_w=v_conv_w, g_attn_out=v_g_attn_out, g_conv_out=v_g_conv_out,
             w_out=v_w_out, g_mix_pre=v_g_mix_pre, g_mix_post=v_g_mix_post, w_gate_up=v_w_gate_up, w_down=v_w_down,
             g_ffn_pre=v_g_ffn_pre, g_ffn_post=v_g_ffn_post)
    cx, cy, cc = lax.axis_index("x"), lax.axis_index("y"), lax.axis_index("c")
    me = 2 * cx + cy
    c_idx = cc.astype(jnp.int32).reshape(1)
    me_idx = me.astype(jnp.int32).reshape(1)

    stored = lambda a: jnp.transpose(a, (2, 0, 1))
    grad_x, big, small_all = _device_step(x[0], loss_target[0], w, m, v, stored(w_in), stored(m_w_in),
                                          stored(v_w_in), c_idx, me_idx)
    gsum, delta, new_m, new_v = {}, {}, {}, {}
    for n in BIG:
        back = (lambda r: jnp.transpose(r, (1, 2, 0))) if n == "w_in" else (lambda r: r[None])
        gsum[n], delta[n], new_m[n], new_v[n] = [back(r) for r in big[n]]
    loss_sum, small_new = _small_update(me_idx, small_all, w, m, v)
    for n in SMALL_ALL:
        gsum[n], delta[n], new_m[n], new_v[n] = small_new[n]
    loss = 0.5 * loss_sum[0, 0]

    order = ("w_in", "b_forget", "conv_w", "g_attn_out", "g_conv_out", "w_out", "g_mix_pre", "g_mix_post",
             "w_gate_up", "w_down", "g_ffn_pre", "g_ffn_post")
    return (loss, grad_x[None], *[gsum[n] for n in order], *[delta[n] for n in order],
            *[new_m[n] for n in order], *[new_v[n] for n in order])
```

```python
import functools

import jax
import jax.numpy as jnp
import numpy as np
from jax import lax
from jax.experimental import pallas as pl
from jax.experimental.pallas import tpu as pltpu

F32 = jnp.float32
BF16 = jnp.bfloat16
MXU_DTYPE = jnp.bfloat16

D_MODEL = 1024
HEAD_DIM = 64
N_HEADS = 8
ATTN_W = 512
CONV_W = 512
D_FF = 2816
FF_PIECE = 1408
EPS = 1e-6
Q_SCALE = HEAD_DIM ** -0.5

OFF_F = 1536
OFF_B = 1664
OFF_C = 2176
OFF_U = 2688
IN_PAD = 3200
IN_W = 3080
N_CHIPS = 4

ADAM_LR = 0.001
ADAM_B1 = 0.9
ADAM_B2 = 0.999
ADAM_EPS = 1e-08
ADAM_WD = 0.01
ADAM_STEP = 10

VMEM_LIMIT_V7X = 56 * 1024 * 1024
MESH_ID = pl.DeviceIdType.MESH


def _params(sem=None, vmem=VMEM_LIMIT_V7X):
    kw = {"vmem_limit_bytes": vmem}
    if sem is not None:
        kw["dimension_semantics"] = sem
    return pltpu.CompilerParams(**kw)


def _dot(a, b):
    return jnp.dot(a, b, preferred_element_type=F32)


def _dot_nt(a, b):
    return lax.dot_general(a, b, (((1,), (1,)), ((), ())), preferred_element_type=F32)


def _dot_exact(x, ones, parts):
    if ones.dtype == F32:
        return _dot(x, ones)
    acc = None
    rem = x
    for _ in range(parts):
        piece = rem.astype(BF16)
        rem = rem - piece.astype(F32)
        term = _dot(piece, ones)
        acc = term if acc is None else acc + term
    return acc


def _rms(v):
    return lax.rsqrt(jnp.mean(v * v, axis=-1, keepdims=True) + EPS)


def _tok(tm, w):
    return pl.BlockSpec((tm, w), lambda i: (i, 0))


def _whole(shape, single=False):
    nd = len(shape)
    if single:
        return pl.BlockSpec(shape, lambda i: (0,) * nd, pipeline_mode=pl.Buffered(1))
    return pl.BlockSpec(shape, lambda i: (0,) * nd)


def _feat(rows, tm):
    return pl.BlockSpec((rows, tm), lambda i: (0, i))


def _inproj_fwd(x, g_pre, w_t, tm):
    s = x.shape[0]

    def body(x_ref, g_ref, w_ref, h_ref, q_ref, k_ref, v_ref, kt_ref, vt_ref, zt_ref, b_ref, c_ref, u_ref):
        xv = x_ref[...]
        h = ((xv * _rms(xv)) * g_ref[...]).astype(MXU_DTYPE)
        h_ref[...] = h

        def proj(lo, hi):
            return _dot_nt(h, w_ref[lo:hi, :])

        q_ref[...] = (proj(0, 512) * Q_SCALE).astype(MXU_DTYPE)
        kt = _dot_nt(w_ref[512:1024, :], h)
        vt = _dot_nt(w_ref[1024:OFF_F, :], h)
        kt_ref[...] = kt.astype(MXU_DTYPE)
        vt_ref[...] = vt.astype(MXU_DTYPE)
        k_ref[...] = kt.T.astype(MXU_DTYPE)
        v_ref[...] = vt.T.astype(MXU_DTYPE)
        zt_ref[...] = _dot_nt(w_ref[OFF_F:OFF_B, :], h)
        b_ref[...] = proj(OFF_B, OFF_C).astype(MXU_DTYPE)
        c_ref[...] = proj(OFF_C, OFF_U).astype(MXU_DTYPE)
        u_ref[...] = proj(OFF_U, IN_PAD).astype(MXU_DTYPE)

    sd = jax.ShapeDtypeStruct
    return pl.pallas_call(
        body, name="inproj_fwd", grid=(s // tm,),
        in_specs=[_tok(tm, D_MODEL), _whole((1, D_MODEL)), _whole((IN_PAD, D_MODEL), single=True)],
        out_specs=[_tok(tm, D_MODEL), _tok(tm, 512), _tok(tm, 512), _tok(tm, 512), _feat(512, tm), _feat(512, tm),
                   _feat(128, tm), _tok(tm, 512), _tok(tm, 512), _tok(tm, 512)],
        out_shape=[sd((s, D_MODEL), MXU_DTYPE), sd((s, 512), MXU_DTYPE), sd((s, 512), MXU_DTYPE),
                   sd((s, 512), MXU_DTYPE), sd((512, s), MXU_DTYPE), sd((512, s), MXU_DTYPE), sd((128, s), F32),
                   sd((s, 512), MXU_DTYPE), sd((s, 512), MXU_DTYPE), sd((s, 512), MXU_DTYPE)],
        compiler_params=_params(("arbitrary",)),
    )(x, g_pre, w_t)


def _tri(n, upper):
    r = lax.broadcasted_iota(jnp.int32, (n, n), 0)
    c = lax.broadcasted_iota(jnp.int32, (n, n), 1)
    return ((r <= c) if upper else (r >= c)).astype(MXU_DTYPE)


HEAD_ROWS = 16


def _rows_to_cols(v):
    return jnp.concatenate([v, jnp.zeros((128 - HEAD_ROWS, 128), F32)], axis=0).T


BIAS_PARTS = 3


def _bias_placement():
    place_q = np.zeros((BIAS_PARTS, 128, ATTN_W), np.float32)
    place_k = np.zeros((BIAS_PARTS, 128, ATTN_W), np.float32)
    ones_q = np.zeros((1, ATTN_W), np.float32)
    ones_k = np.zeros((1, ATTN_W), np.float32)
    for h in range(N_HEADS):
        base = 2 * HEAD_DIM * (h // 2) + HEAD_DIM * (1 - h % 2)
        for part in range(BIAS_PARTS):
            place_q[part, h, base + part] = 1.0
            place_k[part, h, base + BIAS_PARTS + part] = -1.0
        ones_q[0, base + BIAS_PARTS:base + 2 * BIAS_PARTS] = 1.0
        ones_k[0, base:base + BIAS_PARTS] = 1.0
    return (jnp.asarray(place_q, MXU_DTYPE), jnp.asarray(place_k, MXU_DTYPE), jnp.asarray(ones_q), jnp.asarray(ones_k))


def _forget_fwd(z_t, b_col):
    s = z_t.shape[1]
    nb = s // 128

    def body(z_ref, b_ref, pq_ref, pk_ref, oq_ref, ok_ref, c_ref, cc_ref, qa_ref, ka_ref):
        upper = _tri(128, True)

        carry = jnp.zeros((HEAD_ROWS, 1), F32)
        for n in range(nb):
            off = n * 128
            lf = jax.nn.log_sigmoid(z_ref[0:HEAD_ROWS, off:off + 128] + b_ref[...])
            cs = _dot_exact(lf, upper, 3) + carry
            c_ref[:, off:off + 128] = cs
            cc_ref[off:off + 128, :] = _rows_to_cols(cs)
            carry = carry + jnp.sum(lf, axis=1, keepdims=True)

        rb = min(s, 512)
        for off in range(0, s, rb):
            qa = jnp.broadcast_to(oq_ref[...], (rb, ATTN_W))
            ka = jnp.broadcast_to(ok_ref[...], (rb, ATTN_W))
            rem = cc_ref[off:off + rb, :]
            for part in range(BIAS_PARTS):
                piece = rem.astype(MXU_DTYPE)
                rem = rem - piece.astype(F32)
                qa = qa + _dot(piece, pq_ref[part])
                ka = ka + _dot(piece, pk_ref[part])
            qa_ref[off:off + rb, :] = qa.astype(MXU_DTYPE)
            ka_ref[off:off + rb, :] = ka.astype(MXU_DTYPE)

    sd = jax.ShapeDtypeStruct
    return pl.pallas_call(body, name="forget_fwd",
                          out_shape=[sd((HEAD_ROWS, s), F32), sd((s, 128), F32), sd((s, ATTN_W), MXU_DTYPE),
                                     sd((s, ATTN_W), MXU_DTYPE)],
                          compiler_params=_params())(z_t, b_col, *_bias_placement())


def _aligned(start, size):
    return pl.ds(start if isinstance(start, int) else pl.multiple_of(start, size), size)


def _pair_lanes(pp):
    return _aligned(pp * 2 * HEAD_DIM, 2 * HEAD_DIM)


def _head_rows(h):
    return _aligned(h * HEAD_DIM, HEAD_DIM)


def _only_head(block, hb):
    lane = lax.broadcasted_iota(jnp.int32, block.shape, 1)
    return jnp.where((lane >= HEAD_DIM) if hb else (lane < HEAD_DIM), block, jnp.zeros_like(block))


def _head_col(cols, h):
    lane = lax.broadcasted_iota(jnp.int32, cols.shape, 1)
    return jnp.sum(jnp.where(lane == h, cols, 0.0), axis=1, keepdims=True)


def _other_head(block, other, hb):
    lane = lax.broadcasted_iota(jnp.int32, block.shape, 1)
    return jnp.where((lane >= HEAD_DIM) if hb else (lane < HEAD_DIM), block, other)


def _attn_fwd(qs, k, v_t, q_bias, k_bias, t, shards):
    s = qs.shape[0]
    n = s // t
    pairs = [(i, j) for i in range(n) for j in range(i + 1)]
    it = jnp.asarray(np.array([p[0] for p in pairs], np.int32))
    jt = jnp.asarray(np.array([p[1] for p in pairs], np.int32))
    nw = len(shards)
    last = len(pairs) - 1
    mid = (2 * len(pairs)) // 3

    def body(it_ref, jt_ref, q_ref, k_ref, vt_ref, qb_ref, kb_ref, *rest):
        sh, (o_ref, lse_ref), got = rest[:nw], rest[nw:nw + 2], rest[nw + 2:2 * nw + 2]
        m_sc, l_sc, acc_sc, send, recv = rest[2 * nw + 2:]
        p = pl.program_id(0)
        i = it_ref[p]
        j = jt_ref[p]
        gather_start, gather_forward, gather_finish = _gather_stages(sh, got, send, recv)
        pl.when(p == 0)(gather_start)
        if mid < last:
            pl.when(p == mid)(gather_forward)

        @pl.when(j == 0)
        def _():
            m_sc[...] = jnp.full_like(m_sc, -1e30)
            l_sc[...] = jnp.ones_like(l_sc)
            acc_sc[...] = jnp.zeros_like(acc_sc)

        def pair_step(pp, diagonal):
            lanes = _pair_lanes(pp)
            kp = k_ref[:, lanes]
            qp = q_ref[:, lanes]
            kb = kb_ref[:, lanes]
            qb = qb_ref[:, lanes]
            for hb in range(2):
                h = 2 * pp + hb
                row = pl.ds(h, 1)
                rows = _head_rows(h)
                st = _dot_nt(_other_head(kp, kb, hb), _other_head(qp, qb, hb))
                if diagonal:
                    kpos = lax.broadcasted_iota(jnp.int32, (t, t), 0)
                    qpos = lax.broadcasted_iota(jnp.int32, (t, t), 1)
                    st = jnp.where(kpos <= qpos, st, -1e30)
                m_prev = m_sc[row, :]
                m_new = jnp.maximum(m_prev, jnp.max(st, axis=0, keepdims=True))
                alpha = jnp.exp(m_prev - m_new)
                pt = jnp.exp(st - m_new)
                l_sc[row, :] = alpha * l_sc[row, :] + jnp.sum(pt, axis=0, keepdims=True)
                acc_sc[rows, :] = acc_sc[rows, :] * alpha + _dot(vt_ref[rows, :], pt.astype(MXU_DTYPE))
                m_sc[row, :] = m_new

        @pl.when(j < i)
        def _():
            for pp in range(N_HEADS // 2):
                pair_step(pp, False)

        @pl.when(j == i)
        def _():
            for pp in range(N_HEADS // 2):
                pair_step(pp, True)
                sub = lax.broadcasted_iota(jnp.int32, (2 * HEAD_DIM, t), 0)
                l_pair = jnp.where(sub < HEAD_DIM, l_sc[pl.ds(2 * pp, 1), :], l_sc[pl.ds(2 * pp + 1, 1), :])
                o_t = acc_sc[_aligned(pp * 2 * HEAD_DIM, 2 * HEAD_DIM), :] / l_pair
                o_ref[:, _pair_lanes(pp)] = o_t.T

            lse_ref[...] = m_sc[...] + jnp.log(l_sc[...])

        @pl.when(p == last)
        def _():
            if mid >= last:
                gather_forward()
            gather_finish()

    gs = pltpu.PrefetchScalarGridSpec(
        num_scalar_prefetch=2, grid=(len(pairs),),
        in_specs=[pl.BlockSpec((t, ATTN_W), lambda p, it_, jt_: (it_[p], 0)),
                  pl.BlockSpec((t, ATTN_W), lambda p, it_, jt_: (jt_[p], 0)),
                  pl.BlockSpec((ATTN_W, t), lambda p, it_, jt_: (0, jt_[p])),
                  pl.BlockSpec((t, ATTN_W), lambda p, it_, jt_: (it_[p], 0)),
                  pl.BlockSpec((t, ATTN_W), lambda p, it_, jt_: (jt_[p], 0))] + [ANY] * nw,
        out_specs=[pl.BlockSpec((t, ATTN_W), lambda p, it_, jt_: (it_[p], 0)),
                   pl.BlockSpec((HEAD_ROWS, t), lambda p, it_, jt_: (0, it_[p]))] + [ANY] * nw,
        scratch_shapes=[pltpu.VMEM((HEAD_ROWS, t), F32), pltpu.VMEM((HEAD_ROWS, t), F32), pltpu.VMEM((ATTN_W, t), F32),
                        pltpu.SemaphoreType.DMA((nw, 6)), pltpu.SemaphoreType.DMA((nw, 6))])
    o, lse, *got = pl.pallas_call(
        body, name="attn_fwd", grid_spec=gs,
        out_shape=[jax.ShapeDtypeStruct((s, ATTN_W), F32), jax.ShapeDtypeStruct((HEAD_ROWS, s), F32)]
        + [jax.ShapeDtypeStruct((N_CHIPS,) + a.shape, a.dtype) for a in shards],
        compiler_params=_params(("arbitrary",)),
    )(it, jt, qs, k, v_t, q_bias, k_bias, *shards)
    me = 2 * lax.axis_index("x") + lax.axis_index("y")
    return o, lse, [lax.dynamic_update_index_in_dim(g, own, me, 0) for g, own in zip(got, shards)]


HALO = 16


def _halo_before(tm):
    return pl.BlockSpec((HALO, CONV_W), lambda i: (jnp.maximum(i * (tm // HALO) - 1, 0), 0))


def _shift_down(cur, prev_ref, first, cols=slice(None)):
    row = lax.broadcasted_iota(jnp.int32, cur.shape, 0)

    def before(r):
        prod = prev_ref[0][r:r + 1, cols].astype(F32) * prev_ref[1][r:r + 1, cols].astype(F32)
        return jnp.where(first, 0.0, prod)

    p7, p6 = before(HALO - 1), before(HALO - 2)
    s1 = jnp.where(row == 0, p7, pltpu.roll(cur, 1, 0))
    s2 = jnp.where(row == 0, p6, jnp.where(row == 1, p7, pltpu.roll(cur, 2, 0)))
    return s1, s2


def _group_ms(v, gmat):
    return _dot_exact(v, gmat, 1) * (1.0 / HEAD_DIM)


def _mixer_fwd(x, o_attn, gate_b, gate_c, u, conv_w, g_attn, g_conv, w_out, g_post, gmat, tm):
    s = x.shape[0]

    def body(x_ref, o_ref, b_ref, c_ref, u_ref, cp_ref, up_ref, cw_ref, ga_ref, gc_ref, wo_ref, gp_ref, gm_ref,
             x2_ref, mg_ref, y_ref, z_ref):
        i = pl.program_id(0)
        gm = gm_ref[0:128, 0:128]
        for lo in range(0, ATTN_W, 128):
            cols = slice(lo, lo + 128)
            cu = c_ref[:, cols].astype(F32) * u_ref[:, cols].astype(F32)
            cu1, cu2 = _shift_down(cu, (cp_ref, up_ref), i == 0, cols)
            z = cw_ref[0:1, cols] * cu2 + cw_ref[1:2, cols] * cu1 + cw_ref[2:3, cols] * cu
            z_ref[:, cols] = z
            cv = b_ref[:, cols].astype(F32) * z
            ov = o_ref[:, cols]
            mg_ref[:, cols] = ((ov * lax.rsqrt(_group_ms(ov * ov, gm) + EPS)) * ga_ref[:, cols]).astype(MXU_DTYPE)
            mg_ref[:, ATTN_W + lo:ATTN_W + lo + 128] = (
                (cv * lax.rsqrt(_group_ms(cv * cv, gm) + EPS)) * gc_ref[:, cols]).astype(MXU_DTYPE)
        y = _dot(mg_ref[...], wo_ref[...])
        y_ref[...] = y
        x2_ref[...] = x_ref[...] + (y * _rms(y)) * gp_ref[...]

    halo = _halo_before(tm)
    sd = jax.ShapeDtypeStruct
    return pl.pallas_call(
        body, name="mixer_fwd", grid=(s // tm,),
        in_specs=[_tok(tm, D_MODEL), _tok(tm, 512), _tok(tm, 512), _tok(tm, 512), _tok(tm, 512), halo, halo,
                  _whole((3, 512)), _whole((1, 512)), _whole((1, 512)), _whole((D_MODEL, D_MODEL), single=True),
                  _whole((1, D_MODEL)), _whole((512, 512))],
        out_specs=[_tok(tm, D_MODEL), _tok(tm, D_MODEL), _tok(tm, D_MODEL), _tok(tm, 512)],
        out_shape=[sd((s, D_MODEL), F32), sd((s, D_MODEL), MXU_DTYPE), sd((s, D_MODEL), F32), sd((s, 512), F32)],
        compiler_params=_params(("arbitrary",)),
    )(x, o_attn, gate_b, gate_c, u, gate_c, u, conv_w, g_attn, g_conv, w_out, g_post, gmat)


def _ffn_fwd(x2, target, g_pre, w_gu, w_dn, g_post, tm):
    s = x2.shape[0]

    def body(x_ref, t_ref, gpre_ref, wgu_ref, wdn_ref, gpost_ref,
             h_ref, g_ref, up_ref, a_ref, ff_ref, dout_ref, loss_ref):
        xv = x_ref[...]
        h = ((xv * _rms(xv)) * gpre_ref[...]).astype(MXU_DTYPE)
        h_ref[...] = h
        ff = jnp.zeros((tm, D_MODEL), F32)
        for j in range(2):
            cols = slice(j * FF_PIECE, (j + 1) * FF_PIECE)
            g = _dot(h, wgu_ref[j])
            up = _dot(h, wgu_ref[2 + j])
            a = ((g * jax.nn.sigmoid(g)) * up).astype(MXU_DTYPE)
            g_ref[:, cols] = g.astype(MXU_DTYPE)
            up_ref[:, cols] = up.astype(MXU_DTYPE)
            a_ref[:, cols] = a
            ff = ff + _dot(a, wdn_ref[j])
        ff_ref[...] = ff
        err = (xv + (ff * _rms(ff)) * gpost_ref[...]) - t_ref[...]
        dout_ref[...] = err * (1.0 / D_MODEL)
        part = jnp.sum(jnp.mean(err * err, axis=-1, keepdims=True), axis=0, keepdims=True)

        @pl.when(pl.program_id(0) == 0)
        def _():
            loss_ref[...] = jnp.zeros_like(loss_ref)

        loss_ref[...] += part

    sd = jax.ShapeDtypeStruct
    return pl.pallas_call(
        body, name="ffn_fwd", grid=(s // tm,),
        in_specs=[_tok(tm, D_MODEL), _tok(tm, D_MODEL), _whole((1, D_MODEL)),
                  _whole((4, D_MODEL, FF_PIECE), single=True), _whole((2, FF_PIECE, D_MODEL), single=True),
                  _whole((1, D_MODEL))],
        out_specs=[_tok(tm, D_MODEL), _tok(tm, D_FF), _tok(tm, D_FF), _tok(tm, D_FF), _tok(tm, D_MODEL),
                   _tok(tm, D_MODEL), _whole((8, 128))],
        out_shape=[sd((s, D_MODEL), MXU_DTYPE), sd((s, D_FF), MXU_DTYPE), sd((s, D_FF), MXU_DTYPE),
                   sd((s, D_FF), MXU_DTYPE), sd((s, D_MODEL), F32), sd((s, D_MODEL), F32), sd((8, 128), F32)],
        compiler_params=_params(("arbitrary",)),
    )(x2, target, g_pre, w_gu, w_dn, g_post)


def _norm_bwd(dy, normed, rinv, gain):
    t = dy * gain
    return rinv * (t - normed * jnp.mean(t * normed, axis=-1, keepdims=True))


def _acc_rows(ref, first, val):
    @pl.when(first)
    def _():
        ref[...] = jnp.zeros_like(ref)

    ref[...] += jnp.sum(val, axis=0, keepdims=True)


def _ffn_bwd(dout, ff, x2, g, up, g_post, g_pre, w_gu, w_dn, tm):
    s = x2.shape[0]

    def body(do_ref, ff_ref, x_ref, g_ref, up_ref, gpost_ref, gpre_ref, wgu_ref, wdn_ref,
             dx_ref, dff_ref, dgu_ref, dgpost_ref, dgpre_ref):
        first = pl.program_id(0) == 0
        ffv = ff_ref[...]
        rf = _rms(ffv)
        n = ffv * rf
        do = do_ref[...]
        _acc_rows(dgpost_ref, first, do * n)
        dff = _norm_bwd(do, n, rf, gpost_ref[...]).astype(MXU_DTYPE)
        dff_ref[...] = dff
        dh = jnp.zeros((tm, D_MODEL), F32)
        for j in range(2):
            cols = slice(j * FF_PIECE, (j + 1) * FF_PIECE)
            da = _dot_nt(dff, wdn_ref[j])
            gv = g_ref[:, cols].astype(F32)
            sg = jax.nn.sigmoid(gv)
            dg = (da * up_ref[:, cols].astype(F32) * (sg * (1.0 + gv * (1.0 - sg)))).astype(MXU_DTYPE)
            du = (da * (gv * sg)).astype(MXU_DTYPE)
            dgu_ref[:, cols] = dg
            dgu_ref[:, D_FF + j * FF_PIECE:D_FF + (j + 1) * FF_PIECE] = du
            dh = dh + _dot_nt(dg, wgu_ref[j]) + _dot_nt(du, wgu_ref[2 + j])
        xv = x_ref[...]
        r2 = _rms(xv)
        nx = xv * r2
        _acc_rows(dgpre_ref, first, dh * nx)
        dx_ref[...] = do + _norm_bwd(dh, nx, r2, gpre_ref[...])

    sd = jax.ShapeDtypeStruct
    return pl.pallas_call(
        body, name="ffn_bwd", grid=(s // tm,),
        in_specs=[_tok(tm, D_MODEL), _tok(tm, D_MODEL), _tok(tm, D_MODEL), _tok(tm, D_FF), _tok(tm, D_FF),
                  _whole((1, D_MODEL)), _whole((1, D_MODEL)),
                  _whole((4, D_MODEL, FF_PIECE), single=True), _whole((2, FF_PIECE, D_MODEL), single=True)],
        out_specs=[_tok(tm, D_MODEL), _tok(tm, D_MODEL), _tok(tm, 2 * D_FF), _whole((1, D_MODEL)),
                   _whole((1, D_MODEL))],
        out_shape=[sd((s, D_MODEL), F32), sd((s, D_MODEL), MXU_DTYPE), sd((s, 2 * D_FF), MXU_DTYPE),
                   sd((1, D_MODEL), F32), sd((1, D_MODEL), F32)],
        compiler_params=_params(("arbitrary",)),
    )(dout, ff, x2, g, up, g_post, g_pre, w_gu, w_dn)


def _tn_matmul(a, b, tm, tn, tk, name):
    s, m = a.shape
    n = b.shape[1]

    def body(a_ref, b_ref, o_ref):
        @pl.when(pl.program_id(2) == 0)
        def _():
            o_ref[...] = jnp.zeros_like(o_ref)

        o_ref[...] += lax.dot_general(a_ref[...], b_ref[...], (((0,), (0,)), ((), ())), preferred_element_type=F32)

    return pl.pallas_call(
        body, name=name, grid=(m // tm, n // tn, s // tk),
        in_specs=[pl.BlockSpec((tk, tm), lambda i, j, kk: (kk, i)), pl.BlockSpec((tk, tn), lambda i, j, kk: (kk, j))],
        out_specs=pl.BlockSpec((tm, tn), lambda i, j, kk: (i, j)),
        out_shape=jax.ShapeDtypeStruct((m, n), F32),
        compiler_params=_params(("arbitrary", "arbitrary", "arbitrary")),
    )(a, b)


def _mixer_bwd(dx2, y, o_attn, gate_b, z, g_post, g_attn, g_conv, w_out, gmat, sel, tm, ready, kinds):
    s = dx2.shape[0]
    nw = len(ready)
    nt = s // tm

    def body(d_ref, y_ref, o_ref, b_ref, z_ref, gp_ref, ga_ref, gc_ref, wo_ref, gm_ref, sel_ref, *rest):
        grads = rest[:nw]
        dy_ref, do_ref, db_ref, dz_ref, delta_ref, dgp_ref, dga_ref, dgc_ref = rest[nw:nw + 8]
        taken = rest[nw + 8:2 * nw + 8]
        send, recv = rest[2 * nw + 8:]
        first = pl.program_id(0) == 0
        pair_start, pair_finish = _pair_stages(grads, kinds, taken, send, recv)
        pl.when(first)(pair_start)
        yv = y_ref[...]
        ry = _rms(yv)
        ny = yv * ry
        d = d_ref[...]
        _acc_rows(dgp_ref, first, d * ny)
        dy = _norm_bwd(d, ny, ry, gp_ref[...]).astype(MXU_DTYPE)
        dy_ref[...] = dy
        dm = _dot_nt(dy, wo_ref[...])
        gm = gm_ref[0:128, 0:128]

        @pl.when(first)
        def _():
            dga_ref[...] = jnp.zeros_like(dga_ref)
            dgc_ref[...] = jnp.zeros_like(dgc_ref)

        def group_bwd(val, dmv, gain_ref, dg_ref, cols):
            rg = lax.rsqrt(_group_ms(val * val, gm) + EPS)
            nv = val * rg
            dg_ref[:, cols] += jnp.sum(dmv * nv, axis=0, keepdims=True)
            t = dmv * gain_ref[:, cols]
            return rg * (t - nv * _group_ms(t * nv, gm))

        delta = jnp.zeros((tm, 128), F32)
        for lo in range(0, ATTN_W, 128):
            cols = slice(lo, lo + 128)
            ov = o_ref[:, cols]
            d_o = group_bwd(ov, dm[:, cols], ga_ref, dga_ref, cols)
            do_ref[:, cols] = d_o.astype(MXU_DTYPE)
            delta = delta + _dot_exact(d_o * ov, sel_ref[cols, :], 2)
            zv = z_ref[:, cols]
            bv = b_ref[:, cols].astype(F32)
            d_cv = group_bwd(bv * zv, dm[:, ATTN_W + lo:ATTN_W + lo + 128], gc_ref, dgc_ref, cols)
            db_ref[:, cols] = (d_cv * zv).astype(MXU_DTYPE)
            dz_ref[:, cols] = d_cv * bv
        delta_ref[...] = delta.T[0:HEAD_ROWS, :]
        pl.when(pl.program_id(0) == nt - 1)(pair_finish)

    sd = jax.ShapeDtypeStruct
    taken_shape = [sd((N_CHIPS, g.shape[-2], g.shape[-1] if kd == "rows" else g.shape[-1] // N_CHIPS), F32)
                   for g, kd in zip(ready, kinds)]
    out = pl.pallas_call(
        body, name="mixer_bwd", grid=(nt,),
        in_specs=[_tok(tm, D_MODEL), _tok(tm, D_MODEL), _tok(tm, 512), _tok(tm, 512), _tok(tm, 512),
                  _whole((1, D_MODEL)), _whole((1, 512)), _whole((1, 512)),
                  _whole((D_MODEL, D_MODEL), single=True), _whole((512, 512)), _whole((512, 128))] + [ANY] * nw,
        out_specs=[_tok(tm, D_MODEL), _tok(tm, 512), _tok(tm, 512), _tok(tm, 512), _feat(HEAD_ROWS, tm),
                   _whole((1, D_MODEL)), _whole((1, 512)), _whole((1, 512))] + [ANY] * nw,
        out_shape=[sd((s, D_MODEL), MXU_DTYPE), sd((s, 512), MXU_DTYPE), sd((s, 512), MXU_DTYPE), sd((s, 512), F32),
                   sd((HEAD_ROWS, s), F32), sd((1, D_MODEL), F32), sd((1, 512), F32), sd((1, 512), F32)] + taken_shape,
        scratch_shapes=[pltpu.SemaphoreType.DMA((nw, N_CHIPS)), pltpu.SemaphoreType.DMA((nw, N_CHIPS))],
        compiler_params=_params(("arbitrary",)),
    )(dx2, y, o_attn, gate_b, z, g_post, g_attn, g_conv, w_out, gmat, sel, *ready)
    return out[:8], out[8:]


def _attn_bwd(qs, k, k_t, v, do, c_rows, c_cols, lse, delta, t, parts):
    s = qs.shape[0]
    n = s // t
    pairs = [(i, j) for j in range(n) for i in range(j, n)]
    it = jnp.asarray(np.array([p[0] for p in pairs], np.int32))
    jt = jnp.asarray(np.array([p[1] for p in pairs], np.int32))

    nw = len(parts)

    def body(it_ref, jt_ref, q_ref, k_ref, kt_ref, v_ref, do_ref, cq_ref, ck_ref, lse_ref, dl_ref, *rest):
        pb = rest[:nw]
        dq_ref, dk_ref, dv_ref, dc_ref, dcq_ref = rest[nw:nw + 5]
        rcv = rest[nw + 5:2 * nw + 5]
        dk_sc, dv_sc, dc_sc, send, recv = rest[2 * nw + 5:]
        p = pl.program_id(0)
        i = it_ref[p]
        j = jt_ref[p]
        chip_start, chip_finish = _chip_stages(pb, rcv, send, recv)

        @pl.when(p == 0)
        def _():
            chip_start()
            dq_ref[...] = jnp.zeros_like(dq_ref)
            dcq_ref[...] = jnp.zeros_like(dcq_ref)

        @pl.when(i == j)
        def _():
            dk_sc[...] = jnp.zeros_like(dk_sc)
            dv_sc[...] = jnp.zeros_like(dv_sc)
            dc_sc[...] = jnp.zeros_like(dc_sc)

        def pair_step(pp, diagonal):
            lanes = _pair_lanes(pp)
            qp = q_ref[:, lanes]
            kp = k_ref[:, lanes]
            vp = v_ref[:, lanes]
            dop = do_ref[:, lanes]
            ck_all = ck_ref[...]
            lane = lax.broadcasted_iota(jnp.int32, (t, 128), 1)
            for hb in range(2):
                h = 2 * pp + hb
                row = pl.ds(h, 1)
                bias = (cq_ref[row, :] - lse_ref[row, :]) - _head_col(ck_all, h)
                pt = jnp.exp(_dot_nt(_only_head(kp, hb), qp) + bias)
                if diagonal:
                    kpos = lax.broadcasted_iota(jnp.int32, (t, t), 0)
                    qpos = lax.broadcasted_iota(jnp.int32, (t, t), 1)
                    pt = jnp.where(kpos <= qpos, pt, 0.0)
                dv_sc[:, lanes] += _dot(pt.astype(MXU_DTYPE), _only_head(dop, hb))
                dst = pt * (_dot_nt(_only_head(vp, hb), dop) - dl_ref[row, :])
                dc_sc[...] -= jnp.where(lane == h, jnp.sum(dst, axis=1, keepdims=True), 0.0)
                dcq_ref[i, row, :] += jnp.sum(dst, axis=0, keepdims=True)
                dsb = dst.astype(MXU_DTYPE)
                dk_sc[:, lanes] += _dot(dsb, _only_head(qp, hb))
                rows = _head_rows(h)
                dq_ref[i, rows, :] += _dot(kt_ref[rows, :], dsb)

        @pl.when(i > j)
        def _():
            for pp in range(N_HEADS // 2):
                pair_step(pp, False)

        @pl.when(i == j)
        def _():
            for pp in range(N_HEADS // 2):
                pair_step(pp, True)

        @pl.when(i == n - 1)
        def _():
            dk_ref[...] = dk_sc[...].astype(MXU_DTYPE)
            dv_ref[...] = dv_sc[...].astype(MXU_DTYPE)
            dc_ref[...] = dc_sc[...]

        pl.when(p == len(pairs) - 1)(chip_finish)

    qi = lambda p, it_, jt_: (it_[p], 0)
    kj = lambda p, it_, jt_: (jt_[p], 0)
    row_i = lambda p, it_, jt_: (0, it_[p])
    gs = pltpu.PrefetchScalarGridSpec(
        num_scalar_prefetch=2, grid=(len(pairs),),
        in_specs=[pl.BlockSpec((t, ATTN_W), qi), pl.BlockSpec((t, ATTN_W), kj),
                  pl.BlockSpec((ATTN_W, t), lambda p, it_, jt_: (0, jt_[p])),
                  pl.BlockSpec((t, ATTN_W), kj), pl.BlockSpec((t, ATTN_W), qi),
                  pl.BlockSpec((HEAD_ROWS, t), row_i), pl.BlockSpec((t, 128), kj),
                  pl.BlockSpec((HEAD_ROWS, t), row_i), pl.BlockSpec((HEAD_ROWS, t), row_i)] + [ANY] * nw,
        out_specs=[pl.BlockSpec((n, ATTN_W, t), lambda p, it_, jt_: (0, 0, 0)),
                   pl.BlockSpec((t, ATTN_W), kj), pl.BlockSpec((t, ATTN_W), kj),
                   pl.BlockSpec((t, 128), kj),
                   pl.BlockSpec((n, HEAD_ROWS, t), lambda p, it_, jt_: (0, 0, 0))] + [ANY] * nw,
        scratch_shapes=[pltpu.VMEM((t, ATTN_W), F32), pltpu.VMEM((t, ATTN_W), F32),
                        pltpu.VMEM((t, 128), F32), pltpu.SemaphoreType.DMA((nw, 3)), pltpu.SemaphoreType.DMA((nw, 3))])
    sd = jax.ShapeDtypeStruct
    out = pl.pallas_call(
        body, name="attn_bwd", grid_spec=gs,
        out_shape=[sd((n, ATTN_W, t), F32), sd((s, ATTN_W), MXU_DTYPE), sd((s, ATTN_W), MXU_DTYPE),
                   sd((s, 128), F32), sd((n, HEAD_ROWS, t), F32)] + [sd((3,) + a.shape[1:], a.dtype) for a in parts],
        compiler_params=_params(("arbitrary",)),
    )(it, jt, qs, k, k_t, v, do, c_rows, c_cols, lse, delta, *parts)
    return out[:5], out[5:]


def _forget_bwd(dc_rows, dc_cols, z_t, b_col):
    s = z_t.shape[1]
    nb = s // 128

    def body(dr_ref, dcc_ref, z_ref, b_ref, dz_ref, db_ref):
        lower = _tri(128, False)
        real = lax.broadcasted_iota(jnp.int32, (HEAD_ROWS, 128), 0) < N_HEADS

        tail = jnp.zeros((HEAD_ROWS, 1), F32)
        dbias = jnp.zeros((HEAD_ROWS, 1), F32)
        for m in range(nb):
            off = (nb - 1 - m) * 128
            dc = dr_ref[:, off:off + 128] + dcc_ref[off:off + 128, :].T[0:HEAD_ROWS, :]
            dlf = _dot_exact(dc, lower, 3) + tail
            dz = dlf * jax.nn.sigmoid(-(z_ref[0:HEAD_ROWS, off:off + 128] + b_ref[...]))
            dz = jnp.where(real, dz, 0.0)
            dz_ref[off:off + 128, :] = _rows_to_cols(dz)
            tail = tail + jnp.sum(dc, axis=1, keepdims=True)
            dbias = dbias + jnp.sum(dz, axis=1, keepdims=True)
        db_ref[...] = jnp.broadcast_to(dbias, db_ref.shape)

    return pl.pallas_call(
        body, name="forget_bwd",
        out_shape=[jax.ShapeDtypeStruct((s, 128), F32), jax.ShapeDtypeStruct((HEAD_ROWS, 128), F32)],
        compiler_params=_params())(dc_rows, dc_cols, z_t, b_col)


def _inproj_bwd(dz, gate_c, u, conv_w, dq, dk, dv, dzf, db, x, dx2, g_pre, w_t, tm):
    s = x.shape[0]
    nt = s // tm
    t = dq.shape[2]
    assert t % tm == 0 and dq.shape[:2] == (s // t, ATTN_W)
    per = t // tm

    def body(dz_ref, dzn_ref, c_ref, u_ref, cp_ref, up_ref, cw_ref, dq_ref, dk_ref, dv_ref, dzf_ref, db_ref,
             x_ref, dx2_ref, g_ref, w_ref, gx_ref, dp_ref, dg_ref, dcw_ref):
        i = pl.program_id(0)
        first = i == 0
        last = i == nt - 1
        @pl.when(first)
        def _():
            dcw_ref[...] = jnp.zeros_like(dcw_ref)

        for lo in range(0, CONV_W, 128):
            cols = slice(lo, lo + 128)
            dzv = dz_ref[:, cols]
            row = lax.broadcasted_iota(jnp.int32, dzv.shape, 0)
            n0 = jnp.where(last, 0.0, dzn_ref[0:1, cols])
            n1 = jnp.where(last, 0.0, dzn_ref[1:2, cols])
            dz1 = jnp.where(row == tm - 1, n0, pltpu.roll(dzv, tm - 1, 0))
            dz2 = jnp.where(row == tm - 1, n1, jnp.where(row == tm - 2, n0, pltpu.roll(dzv, tm - 2, 0)))
            dcu = cw_ref[2:3, cols] * dzv + cw_ref[1:2, cols] * dz1 + cw_ref[0:1, cols] * dz2
            cv = c_ref[:, cols].astype(F32)
            uv = u_ref[:, cols].astype(F32)
            cu = cv * uv
            cu1, cu2 = _shift_down(cu, (cp_ref, up_ref), first, cols)
            dcw_ref[0:1, cols] += jnp.sum(dzv * cu2, axis=0, keepdims=True)
            dcw_ref[1:2, cols] += jnp.sum(dzv * cu1, axis=0, keepdims=True)
            dcw_ref[2:3, cols] += jnp.sum(dzv * cu, axis=0, keepdims=True)
            dp_ref[:, OFF_C + lo:OFF_C + lo + 128] = (dcu * uv).astype(MXU_DTYPE)
            dp_ref[:, OFF_U + lo:OFF_U + lo + 128] = (dcu * cv).astype(MXU_DTYPE)

        dp_ref[:, 0:512] = (dq_ref[0].T * Q_SCALE).astype(MXU_DTYPE)
        dp_ref[:, 512:1024] = dk_ref[...].astype(MXU_DTYPE)
        dp_ref[:, 1024:OFF_F] = dv_ref[...].astype(MXU_DTYPE)
        dp_ref[:, OFF_F:OFF_B] = dzf_ref[...].astype(MXU_DTYPE)
        dp_ref[:, OFF_B:OFF_C] = db_ref[...].astype(MXU_DTYPE)
        dh = _dot(dp_ref[...], w_ref[...])
        xv = x_ref[...]
        r1 = _rms(xv)
        nx = xv * r1
        _acc_rows(dg_ref, first, dh * nx)
        gx_ref[...] = dx2_ref[...] + _norm_bwd(dh, nx, r1, g_ref[...])

    prev = _halo_before(tm)
    nxt = pl.BlockSpec((8, 512), lambda i: (jnp.minimum((i + 1) * (tm // 8), s // 8 - 1), 0))
    sd = jax.ShapeDtypeStruct
    return pl.pallas_call(
        body, name="inproj_bwd", grid=(nt,),
        in_specs=[_tok(tm, 512), nxt, _tok(tm, 512), _tok(tm, 512), prev, prev, _whole((3, 512)),
                  pl.BlockSpec((1, ATTN_W, tm), lambda i: (i // per, 0, i % per)), _tok(tm, 512), _tok(tm, 512),
                  _tok(tm, 128),
                  _tok(tm, 512),
                  _tok(tm, D_MODEL), _tok(tm, D_MODEL), _whole((1, D_MODEL)), _whole((IN_PAD, D_MODEL), single=True)],
        out_specs=[_tok(tm, D_MODEL), _tok(tm, IN_PAD), _whole((1, D_MODEL)), _whole((8, 512))],
        out_shape=[sd((s, D_MODEL), F32), sd((s, IN_PAD), MXU_DTYPE), sd((1, D_MODEL), F32), sd((8, 512), F32)],
        compiler_params=_params(("arbitrary",)),
    )(dz, dz, gate_c, u, gate_c, u, conv_w, dq, dk, dv, dzf, db, x, dx2, g_pre, w_t)


def _tile(s, want):
    return want if s % want == 0 else s


def _halves(a):
    return a.reshape(2, a.shape[0] // 2, a.shape[1])


def _device_step(x, target, w, mom1, mom2, w_in_t, m_in_t, v_in_t, c_idx, me_idx):
    s = x.shape[0]
    tm = _tile(s, 512)
    tf = _tile(s, 256)
    ta = _tile(s, 512)
    tkk = _tile(s, 2048)
    gidx = np.arange(512) // HEAD_DIM
    gmat = jnp.asarray(gidx[:, None] == gidx[None, :], MXU_DTYPE)
    sel = jnp.asarray(gidx[:, None] == np.arange(128)[None, :], MXU_DTYPE)
    g_mix_pre, g_mix_post, g_ffn_pre, g_ffn_post = w["g_mix_pre"], w["g_mix_post"], w["g_ffn_pre"], w["g_ffn_post"]
    g_attn, g_conv, b_forget = w["g_attn_out"], w["g_conv_out"], w["b_forget"]
    shard = {n: _halves(w[n][0].astype(MXU_DTYPE)) for n in BIG[1:]}
    piece_rows = IN_W // N_CHIPS

    g_in, conv_all = _gather_weights([w_in_t.reshape(piece_rows, D_MODEL).astype(MXU_DTYPE)], w["conv_w"][0])
    w_rows = g_in.reshape(IN_W, D_MODEL)
    w_t = jnp.concatenate([w_rows[:OFF_F + N_HEADS], jnp.zeros((OFF_B - OFF_F - N_HEADS, D_MODEL), MXU_DTYPE),
                           w_rows[OFF_F + N_HEADS:]], axis=0)
    conv_w = jnp.transpose(conv_all, (1, 0, 2)).reshape(3, CONV_W)

    h1, qs, k, v, k_t, v_t, z_t, gate_b, gate_c, u = _inproj_fwd(x, g_mix_pre, w_t, tm)
    b_col = jnp.pad(jnp.transpose(b_forget), ((0, HEAD_ROWS - N_HEADS), (0, 0)))
    c_rows, c_cols, q_bias, k_bias = _forget_fwd(z_t, b_col)
    o_attn, lse, (g_out, g_gu, g_dn) = _attn_fwd(qs, k, v_t, q_bias, k_bias, ta,
                                                 [shard["w_out"], shard["w_gate_up"], shard["w_down"]])
    w_out = g_out.reshape(D_MODEL, D_MODEL)
    w_gu = g_gu.reshape(N_CHIPS, D_MODEL, FF_PIECE)
    w_dn = g_dn.reshape(2, FF_PIECE, D_MODEL)
    x2, merged, y, z = _mixer_fwd(x, o_attn, gate_b, gate_c, u, conv_w, g_attn, g_conv, w_out, g_mix_post, gmat, tm)
    h2, g, up, a, ff, dout, loss_acc = _ffn_fwd(x2, target, g_ffn_pre, w_gu, w_dn, g_ffn_post, tf)

    dx2, dff, dgu, dg_ffn_post, dg_ffn_pre = _ffn_bwd(dout, ff, x2, g, up, g_ffn_post, g_ffn_pre, w_gu, w_dn, tf)
    dw_dn = _tn_matmul(a, dff, FF_PIECE, 1024, tkk, "dw_down").reshape(N_CHIPS, 2, D_FF // (2 * N_CHIPS), D_MODEL)
    dw_gu = _tn_matmul(h2, dgu, 1024, FF_PIECE, tkk, "dw_gate_up").reshape(2, D_MODEL // 2, 2 * D_FF)
    (dy, d_o, d_b, dz, delta, dg_mix_post, dg_attn, dg_conv), (a_gu, a_dn) = _mixer_bwd(
        dx2, y, o_attn, gate_b, z, g_mix_post, g_attn, g_conv, w_out, gmat, sel, tm, [dw_gu, dw_dn], ["cols", "rows"])
    dw_out = _tn_matmul(merged, dy, 1024, 1024, tkk, "dw_out").reshape(N_CHIPS, 2, D_MODEL // (2 * N_CHIPS), D_MODEL)
    place = jnp.concatenate([c_idx, me_idx])
    *sum_gu, a_out = _pair_sum(place, dw_gu, "cols", a_gu, "pair_sum_w_gate_up", [dw_out], ["rows"])
    sum_dn = _pair_sum(place, dw_dn, "rows", a_dn, "pair_sum_w_down")
    sum_out = _pair_sum(place, dw_out, "rows", a_out, "pair_sum_w_out")
    (dq_t, dk, dv, dc_cols, dcq), (r_gu, r_dn, r_out) = _attn_bwd(
        qs, k, k_t, v, d_o, c_rows, c_cols, lse, delta, ta, [sum_gu[1], sum_dn[1], sum_out[1]])
    dc_rows = jnp.transpose(dcq, (1, 0, 2)).reshape(HEAD_ROWS, s)
    dzf, db_f = _forget_bwd(dc_rows, dc_cols, z_t, b_col)
    grad_x, dproj, dg_mix_pre, dcw = _inproj_bwd(dz, gate_c, u, conv_w, dq_t, dk, dv, dzf, d_b,
                                                 x, dx2, g_mix_pre, w_t, tm)
    dw_t = _tn_matmul(dproj, h1, 640, 1024, tkk, "dw_in")
    dw_in = jnp.concatenate([dw_t[:OFF_F + N_HEADS], dw_t[OFF_B:]], axis=0).reshape(N_CHIPS, piece_rows, D_MODEL)

    (a_in,) = _pair_exchange([dw_in], ["lanes"])
    sum_in = _pair_sum(place, dw_in, "lanes", a_in, "pair_sum_w_in")
    small = dict(b_forget=db_f[:N_HEADS, 0], g_attn_out=dg_attn, g_conv_out=dg_conv, g_mix_pre=dg_mix_pre,
                 g_mix_post=dg_mix_post, g_ffn_pre=dg_ffn_pre, g_ffn_post=dg_ffn_post)
    (r_in,), small_all = _chip_exchange([sum_in[1]], _pack_small(small, dcw[:3], loss_acc[0, 0]))
    totals = [_chip_sum(sb[0], r, "chip_sum_" + n)
              for n, sb, r in zip(BIG, (sum_in, sum_out, sum_gu, sum_dn), (r_in, r_out, r_gu, r_dn))]
    shared = _pair_share(totals, "pair_share")
    new = {"w_in": _adamw_lanes(c_idx, w_in_t, totals[0], shared[0], m_in_t, v_in_t, "adamw_w_in")}
    for n, mine, theirs in list(zip(BIG, totals, shared))[1:]:
        new[n] = _adamw(c_idx, w[n][0], mine, theirs, mom1[n][0], mom2[n][0], 2, "adamw_" + n)
    return grad_x, new, small_all


BIG = ("w_in", "w_out", "w_gate_up", "w_down")
ANY = pl.BlockSpec(memory_space=pl.ANY)


def _place():
    x, y, c = lax.axis_index("x"), lax.axis_index("y"), lax.axis_index("c")
    others = [(1 - x, y), (x, 1 - y), (1 - x, 1 - y)]
    return x, y, c, 2 * x + y, others, [2 * px + py for px, py in others]


def _remote(src, dst, send, recv, dev):
    return pltpu.make_async_remote_copy(src_ref=src, dst_ref=dst, send_sem=send, recv_sem=recv,
                                        device_id=dev, device_id_type=MESH_ID)


def _gather_stages(sh, outs, send, recv):
    x, y, c, me, others, chips = _place()
    sib = (x, y, 1 - c)
    every = [(w, kk) for w in range(len(sh)) for kk in range(3)]

    def half_of(ref, half, piece=None):
        ref = ref if piece is None else ref.at[piece]
        if len(ref.shape) == 3:
            return ref.at[half]
        hc = ref.shape[1] // 2
        return ref.at[:, pl.ds(pl.multiple_of(half * hc, 128), hc)]

    def first(w, kk):
        return _remote(half_of(sh[w], c), half_of(outs[w], c, me), send.at[w, kk], recv.at[w, kk], (*others[kk], c))

    def landed(w, kk):
        r = half_of(outs[w], c, chips[kk])
        return _remote(r, r, send.at[w, kk], recv.at[w, kk], (*others[kk], c))

    def onward(w, kk, half):
        r = half_of(outs[w], half, chips[kk])
        return _remote(r, r, send.at[w, 3 + kk], recv.at[w, 3 + kk], sib)

    def start():
        for w, kk in every:
            first(w, kk).start()

    def forward():
        for w, kk in every:
            landed(w, kk).wait_recv()
            onward(w, kk, c).start()

    def finish():
        for w, kk in every:
            onward(w, kk, 1 - c).wait_recv()
        for w, kk in every:
            first(w, kk).wait_send()
            onward(w, kk, c).wait_send()

    return start, forward, finish


def _pair_piece(ref, kind, p, half):
    if kind == "rows":
        return ref.at[p, half]
    if kind == "lanes":
        hc = ref.shape[2] // 2
        return ref.at[p, :, pl.ds(pl.multiple_of(half * hc, 128), hc)]
    cols = ref.shape[2] // N_CHIPS
    return ref.at[half, :, pl.ds(p * cols, cols)]


def _pair_stages(g, kinds, a, send, recv):
    x, y, c, _, _, _ = _place()
    copies = [_remote(_pair_piece(g[w], kinds[w], p, 1 - c), a[w].at[p], send.at[w, p], recv.at[w, p], (x, y, 1 - c))
              for w in range(len(g)) for p in range(N_CHIPS)]

    def start():
        for cp in copies:
            cp.start()

    def finish():
        for cp in copies:
            cp.wait()

    return start, finish


def _chip_stages(pb, rcv, send, recv):
    x, y, c, _, others, chips = _place()
    copies = [_remote(pb[w].at[chips[kk]], rcv[w].at[kk], send.at[w, kk], recv.at[w, kk], (*others[kk], c))
              for w in range(len(pb)) for kk in range(3)]

    def start():
        for cp in copies:
            cp.start()

    def finish():
        for cp in copies:
            cp.wait()

    return start, finish


def _gather_weights(shards, conv_w):
    n = len(shards)

    def body(*refs):
        sh, cw, outs, cwo = refs[:n], refs[n], refs[n + 1:2 * n + 1], refs[2 * n + 1]
        send, recv = refs[2 * n + 2:]
        x, y, c, me, others, chips = _place()
        start, forward, finish = _gather_stages(sh, outs, send, recv)
        start()
        small = [_remote(cw, cwo.at[me], send.at[n, kk], recv.at[n, kk], (*others[kk], c)) for kk in range(3)]
        for cp in small:
            cp.start()
        forward()
        for kk in range(3):
            _remote(cw, cwo.at[chips[kk]], send.at[n, kk], recv.at[n, kk], (*others[kk], c)).wait_recv()
        finish()
        for cp in small:
            cp.wait_send()

    out_shape = [jax.ShapeDtypeStruct((N_CHIPS,) + s.shape, s.dtype) for s in shards]
    out_shape.append(jax.ShapeDtypeStruct((N_CHIPS,) + conv_w.shape, conv_w.dtype))
    got = pl.pallas_call(
        body, name="gather_weights", in_specs=[ANY] * (n + 1), out_specs=[ANY] * (n + 1), out_shape=out_shape,
        scratch_shapes=[pltpu.SemaphoreType.DMA((n + 1, 6)), pltpu.SemaphoreType.DMA((n + 1, 6))],
    )(*shards, conv_w)
    me = 2 * lax.axis_index("x") + lax.axis_index("y")
    return [lax.dynamic_update_index_in_dim(g, own, me, 0) for g, own in zip(got, list(shards) + [conv_w])]


def _taken_shape(g, kind):
    if kind == "rows":
        return (N_CHIPS,) + g.shape[2:]
    if kind == "lanes":
        return g.shape[:2] + (g.shape[2] // 2,)
    return (N_CHIPS, g.shape[1], g.shape[2] // N_CHIPS)


def _pair_sum(place, g, kind, a, name, ready=(), ready_kinds=()):
    _, half, cols = a.shape
    nw = len(ready)
    if kind == "rows":
        mine = pl.BlockSpec((1, 1, half, cols), lambda p, pr: (p, pr[0], 0, 0))
    elif kind == "lanes":
        mine = pl.BlockSpec((1, half, cols), lambda p, pr: (p, 0, pr[0]))
    else:
        mine = pl.BlockSpec((1, half, cols), lambda p, pr: (pr[0], 0, p))

    def body(place_ref, g_ref, a_ref, *rest):
        grads, (own_ref, pb_ref), taken = rest[:nw], rest[nw:nw + 2], rest[nw + 2:2 * nw + 2]
        if nw:
            pair_start, pair_finish = _pair_stages(grads, ready_kinds, taken, *rest[2 * nw + 2:])
            pl.when(pl.program_id(0) == 0)(pair_start)
        tot = (g_ref[0, 0] if kind == "rows" else g_ref[0]) + a_ref[0]
        pb_ref[0] = tot.astype(BF16)

        @pl.when(pl.program_id(0) == place_ref[1])
        def _():
            own_ref[...] = tot

        if nw:
            pl.when(pl.program_id(0) == N_CHIPS - 1)(pair_finish)

    sems = [pltpu.SemaphoreType.DMA((nw, N_CHIPS)), pltpu.SemaphoreType.DMA((nw, N_CHIPS))] if nw else []
    gs = pltpu.PrefetchScalarGridSpec(
        num_scalar_prefetch=1, grid=(N_CHIPS,),
        in_specs=[mine, pl.BlockSpec((1, half, cols), lambda p, pr: (p, 0, 0))] + [ANY] * nw,
        out_specs=[pl.BlockSpec((half, cols), lambda p, pr: (0, 0)),
                   pl.BlockSpec((1, half, cols), lambda p, pr: (p, 0, 0))] + [ANY] * nw,
        scratch_shapes=sems)
    out = pl.pallas_call(
        body, name=name, grid_spec=gs,
        out_shape=[jax.ShapeDtypeStruct((half, cols), F32), jax.ShapeDtypeStruct((N_CHIPS, half, cols), BF16)]
        + [jax.ShapeDtypeStruct(_taken_shape(r, kd), r.dtype) for r, kd in zip(ready, ready_kinds)],
        compiler_params=_params(("arbitrary",)),
    )(place, g, a, *ready)
    return list(out)


def _chip_sum(own, rcv, name):
    half, cols = own.shape

    def body(o_ref, r_ref, t_ref):
        t_ref[...] = ((o_ref[...] + r_ref[0].astype(F32)) + r_ref[1].astype(F32)) + r_ref[2].astype(F32)

    return pl.pallas_call(
        body, name=name, grid=(1,),
        in_specs=[pl.BlockSpec((half, cols), lambda i: (0, 0)), pl.BlockSpec((3, half, cols), lambda i: (0, 0, 0))],
        out_specs=pl.BlockSpec((half, cols), lambda i: (0, 0)),
        out_shape=jax.ShapeDtypeStruct((half, cols), F32), compiler_params=_params(("arbitrary",)),
    )(own, rcv)


def _small_stages(sm, smg, send, recv):
    x, y, c, _, _, _ = _place()

    def peer(r):
        return (1 - x if r & 4 else x, 1 - y if r & 2 else y, 1 - c if r & 1 else c)

    mine = 4 * x + 2 * y + c
    copies = [_remote(sm, smg.at[mine], send.at[r - 1], recv.at[r - 1], peer(r)) for r in range(1, 8)]

    def start():
        for cp in copies:
            cp.start()

    def finish():
        for r in range(1, 8):
            px, py, pc = peer(r)
            _remote(sm, smg.at[4 * px + 2 * py + pc], send.at[r - 1], recv.at[r - 1], (px, py, pc)).wait_recv()
        for cp in copies:
            cp.wait_send()

    return start, finish


def _pair_exchange(grads, kinds):
    n = len(grads)

    def body(*refs):
        start, finish = _pair_stages(refs[:n], kinds, refs[n:2 * n], *refs[2 * n:])
        start()
        finish()

    return pl.pallas_call(
        body, name="pair_exchange", in_specs=[ANY] * n, out_specs=[ANY] * n,
        out_shape=[jax.ShapeDtypeStruct(_taken_shape(g, kd), g.dtype) for g, kd in zip(grads, kinds)],
        scratch_shapes=[pltpu.SemaphoreType.DMA((n, N_CHIPS)), pltpu.SemaphoreType.DMA((n, N_CHIPS))],
    )(*grads)


def _chip_exchange(parts, small):
    n = len(parts)

    def body(*refs):
        pb, sm, rcv, smg = refs[:n], refs[n], refs[n + 1:2 * n + 1], refs[2 * n + 1]
        send, recv, ssend, srecv = refs[2 * n + 2:]
        chip_start, chip_finish = _chip_stages(pb, rcv, send, recv)
        small_start, small_finish = _small_stages(sm, smg, ssend, srecv)
        chip_start()
        small_start()
        chip_finish()
        small_finish()

    out_shape = [jax.ShapeDtypeStruct((3,) + p.shape[1:], p.dtype) for p in parts]
    out_shape.append(jax.ShapeDtypeStruct((8,) + small.shape, small.dtype))
    *arrived, small_land = pl.pallas_call(
        body, name="chip_exchange", in_specs=[ANY] * (n + 1), out_specs=[ANY] * (n + 1), out_shape=out_shape,
        scratch_shapes=[pltpu.SemaphoreType.DMA((n, 3)), pltpu.SemaphoreType.DMA((n, 3)),
                        pltpu.SemaphoreType.DMA((7,)), pltpu.SemaphoreType.DMA((7,))],
    )(*parts, small)
    mine = 4 * lax.axis_index("x") + 2 * lax.axis_index("y") + lax.axis_index("c")
    return arrived, lax.dynamic_update_index_in_dim(small_land, small, mine, 0)


def _pair_share(totals, name):
    n = len(totals)

    def body(*refs):
        t, g = refs[:n], refs[n:2 * n]
        send, recv = refs[2 * n:]
        x, y, c, _, _, _ = _place()
        copies = [_remote(t[w], g[w], send.at[w], recv.at[w], (x, y, 1 - c)) for w in range(n)]
        for cp in copies:
            cp.start()
        for cp in copies:
            cp.wait()

    return pl.pallas_call(
        body, name=name, in_specs=[ANY] * n, out_specs=[ANY] * n,
        out_shape=[jax.ShapeDtypeStruct(t.shape, t.dtype) for t in totals],
        scratch_shapes=[pltpu.SemaphoreType.DMA((n,)), pltpu.SemaphoreType.DMA((n,))],
    )(*totals)


def _adamw_math(w, g, m, v):
    m = ADAM_B1 * m + (1.0 - ADAM_B1) * g
    v = ADAM_B2 * v + (1.0 - ADAM_B2) * (g * g)
    m_hat = m / (1.0 - ADAM_B1 ** ADAM_STEP)
    v_hat = v / (1.0 - ADAM_B2 ** ADAM_STEP)
    delta = -ADAM_LR * (m_hat / (jnp.sqrt(v_hat) + ADAM_EPS) + ADAM_WD * w)
    return delta, m, v


def _adamw(c_idx, w, mine, theirs, m, v, nb, name):
    rows, cols = w.shape
    tr = rows // (2 * nb)

    def body(c_ref, w_ref, a_ref, b_ref, m_ref, v_ref, g_ref, d_ref, nm_ref, nv_ref):
        g = jnp.where(pl.program_id(0) == c_ref[0], a_ref[...], b_ref[...])
        g_ref[...] = g
        d_ref[...], nm_ref[...], nv_ref[...] = _adamw_math(w_ref[...], g, m_ref[...], v_ref[...])

    full = pl.BlockSpec((tr, cols), lambda hh, i, cr: (hh * nb + i, 0))
    half = pl.BlockSpec((tr, cols), lambda hh, i, cr: (i, 0))
    gs = pltpu.PrefetchScalarGridSpec(num_scalar_prefetch=1, grid=(2, nb), in_specs=[full, half, half, full, full],
                                      out_specs=[full] * 4)
    return pl.pallas_call(
        body, name=name, grid_spec=gs, out_shape=[jax.ShapeDtypeStruct((rows, cols), F32)] * 4,
        compiler_params=_params(("arbitrary", "arbitrary")),
    )(c_idx, w, mine, theirs, m, v)


def _adamw_lanes(c_idx, w, mine, theirs, m, v, name):
    rows, _, cols = w.shape
    hc = cols // 2

    def body(c_ref, w_ref, a_ref, b_ref, m_ref, v_ref, g_ref, d_ref, nm_ref, nv_ref):
        g = jnp.where(pl.program_id(0) == c_ref[0], a_ref[...], b_ref[...])
        g_ref[:, 0, :] = g
        d_ref[:, 0, :], nm_ref[:, 0, :], nv_ref[:, 0, :] = _adamw_math(w_ref[:, 0, :], g, m_ref[:, 0, :], v_ref[:, 0, :])

    full = pl.BlockSpec((rows, 1, hc), lambda hh, cr: (0, 0, hh))
    half = pl.BlockSpec((rows, hc), lambda hh, cr: (0, 0))
    gs = pltpu.PrefetchScalarGridSpec(num_scalar_prefetch=1, grid=(2,), in_specs=[full, half, half, full, full],
                                      out_specs=[full] * 4)
    return pl.pallas_call(
        body, name=name, grid_spec=gs, out_shape=[jax.ShapeDtypeStruct((rows, 1, cols), F32)] * 4,
        compiler_params=_params(("arbitrary",)),
    )(c_idx, w, mine, theirs, m, v)


SMALL = ("g_mix_pre", "g_mix_post", "g_ffn_pre", "g_ffn_post")
SMALL_ALL = SMALL + ("g_attn_out", "g_conv_out", "conv_w", "b_forget")
SMALL_AT = {"g_mix_pre": (0, 0, 1024), "g_mix_post": (1, 0, 1024), "g_ffn_pre": (2, 0, 1024),
            "g_ffn_post": (3, 0, 1024), "g_attn_out": (4, 0, 512), "g_conv_out": (4, 512, 512),
            "b_forget": (7, 0, N_HEADS)}
CONV_AT = ((5, 0), (5, 512), (6, 0))
LOSS_AT = (6, 512)


def _pack_small(t, conv_full, loss_sum):
    conv = jnp.concatenate([conv_full.reshape(1, 3 * CONV_W), loss_sum.reshape(1, 1),
                            jnp.zeros((1, 2048 - 3 * CONV_W - 1), F32)], axis=1).reshape(2, 1024)
    return jnp.concatenate([t[n].reshape(1, 1024) for n in SMALL]
                           + [jnp.concatenate([t["g_attn_out"].reshape(1, 512), t["g_conv_out"].reshape(1, 512)], axis=1),
                              conv, jnp.pad(t["b_forget"].reshape(1, N_HEADS), ((0, 0), (0, 1024 - N_HEADS)))], axis=0)


def _small_update(me_idx, gathered, w, m, v):
    def body(me_ref, gg_ref, *refs):
        k = len(SMALL_ALL)
        w_refs, m_refs, v_refs = refs[:k], refs[k:2 * k], refs[2 * k:3 * k]
        loss_ref = refs[3 * k]
        outs = refs[3 * k + 1:3 * k + 1 + 4 * k]
        sums = refs[-1]
        g = gg_ref[0]
        for dev in range(1, 8):
            g = g + gg_ref[dev]
        sums[...] = g
        loss_ref[...] = sums[LOSS_AT[0]:LOSS_AT[0] + 1, LOSS_AT[1]:LOSS_AT[1] + 1]
        mine = pl.multiple_of(me_ref[0] * 128, 128)
        for idx, name in enumerate(SMALL_ALL):
            g_ref, d_ref, nm_ref, nv_ref = outs[4 * idx:4 * idx + 4]
            if name == "conv_w":
                for r, (row, lo) in enumerate(CONV_AT):
                    gr = sums[row:row + 1, pl.ds(lo + mine, 128)]
                    g_ref[0, r:r + 1, :] = gr
                    d_ref[0, r:r + 1, :], nm_ref[0, r:r + 1, :], nv_ref[0, r:r + 1, :] = _adamw_math(
                        w_refs[idx][0, r:r + 1, :], gr, m_refs[idx][0, r:r + 1, :], v_refs[idx][0, r:r + 1, :])
            else:
                row, lo, n = SMALL_AT[name]
                gr = sums[row:row + 1, lo:lo + n]
                g_ref[...] = gr
                d_ref[...], nm_ref[...], nv_ref[...] = _adamw_math(w_refs[idx][...], gr, m_refs[idx][...],
                                                                    v_refs[idx][...])

    def whole(a):
        nd = a.ndim
        return pl.BlockSpec(a.shape, lambda i, mr: (0,) * nd)

    ins = [t[n] for t in (w, m, v) for n in SMALL_ALL]
    out_shape = [jax.ShapeDtypeStruct((1, 1), F32)]
    for n in SMALL_ALL:
        out_shape += [jax.ShapeDtypeStruct(w[n].shape, F32)] * 4
    gs = pltpu.PrefetchScalarGridSpec(
        num_scalar_prefetch=1, grid=(1,), in_specs=[whole(gathered)] + [whole(a) for a in ins],
        out_specs=[whole(o) for o in out_shape], scratch_shapes=[pltpu.VMEM((8, 1024), F32)])
    out = pl.pallas_call(body, name="small_update", grid_spec=gs, out_shape=out_shape,
                         compiler_params=_params(("arbitrary",)))(me_idx, gathered, *ins)
    return out[0], {n: out[1 + 4 * i:5 + 4 * i] for i, n in enumerate(SMALL_ALL)}


def kernel(x, w_in, b_forget, conv_w, g_attn_out, g_conv_out, w_out, g_mix_pre, g_mix_post, w_gate_up, w_down, g_ffn_pre, g_ffn_post, loss_target, m_w_in, m_b_forget, m_conv_w, m_g_attn_out, m_g_conv_out, m_w_out, m_g_mix_pre, m_g_mix_post, m_w_gate_up, m_w_down, m_g_ffn_pre, m_g_ffn_post, v_w_in, v_b_forget, v_conv_w, v_g_attn_out, v_g_conv_out, v_w_out, v_g_mix_pre, v_g_mix_post, v_w_gate_up, v_w_down, v_g_ffn_pre, v_g_ffn_post):
    w = dict(w_in=w_in, b_forget=b_forget, conv_w=conv_w, g_attn_out=g_attn_out, g_conv_out=g_conv_out, w_out=w_out,
             g_mix_pre=g_mix_pre, g_mix_post=g_mix_post, w_gate_up=w_gate_up, w_down=w_down, g_ffn_pre=g_ffn_pre,
             g_ffn_post=g_ffn_post)
    m = dict(w_in=m_w_in, b_forget=m_b_forget, conv_w=m_conv_w, g_attn_out=m_g_attn_out, g_conv_out=m_g_conv_out,
             w_out=m_w_out, g_mix_pre=m_g_mix_pre, g_mix_post=m_g_mix_post, w_gate_up=m_w_gate_up, w_down=m_w_down,
             g_ffn_pre=m_g_ffn_pre, g_ffn_post=m_g_ffn_post)
    v = dict(w_in=v_w_in, b_forget=v_b_forget, conv_w=v_conv_w, g_attn_out=v_g_attn_out, g_conv_out=v_g_conv_out,
             w_out=v_w_out, g_mix_pre=v_g_mix_pre, g_mix_post=v_g_mix_post, w_gate_up=v_w_gate_up, w_down=v_w_down,
             g_ffn_pre=v_g_ffn_pre, g_ffn_post=v_g_ffn_post)
    cx, cy, cc = lax.axis_index("x"), lax.axis_index("y"), lax.axis_index("c")
    me = 2 * cx + cy
    c_idx = cc.astype(jnp.int32).reshape(1)
    me_idx = me.astype(jnp.int32).reshape(1)

    stored = lambda a: jnp.transpose(a, (2, 0, 1))
    grad_x, big, small_all = _device_step(x[0], loss_target[0], w, m, v, stored(w_in), stored(m_w_in),
                                          stored(v_w_in), c_idx, me_idx)
    gsum, delta, new_m, new_v = {}, {}, {}, {}
    for n in BIG:
        back = (lambda r: jnp.transpose(r, (1, 2, 0))) if n == "w_in" else (lambda r: r[None])
        gsum[n], delta[n], new_m[n], new_v[n] = [back(r) for r in big[n]]
    loss_sum, small_new = _small_update(me_idx, small_all, w, m, v)
    for n in SMALL_ALL:
        gsum[n], delta[n], new_m[n], new_v[n] = small_new[n]
    loss = 0.5 * loss_sum[0, 0]

    order = ("w_in", "b_forget", "conv_w", "g_attn_out", "g_conv_out", "w_out", "g_mix_pre", "g_mix_post",
             "w_gate_up", "w_down", "g_ffn_pre", "g_ffn_post")
    return (loss, grad_x[None], *[gsum[n] for n in order], *[delta[n] for n in order],
            *[new_m[n] for n in order], *[new_v[n] for n in order])
```

```python
import functools

import jax
import jax.numpy as jnp
import numpy as np
from jax import lax
from jax.experimental import pallas as pl
from jax.experimental.pallas import tpu as pltpu

F32 = jnp.float32
BF16 = jnp.bfloat16
MXU_DTYPE = jnp.bfloat16

D_MODEL = 1024
HEAD_DIM = 64
N_HEADS = 8
ATTN_W = 512
CONV_W = 512
D_FF = 2816
FF_PIECE = 1408
EPS = 1e-6
Q_SCALE = HEAD_DIM ** -0.5

OFF_F = 1536
OFF_B = 1664
OFF_C = 2176
OFF_U = 2688
IN_PAD = 3200
IN_W = 3080
N_CHIPS = 4

ADAM_LR = 0.001
ADAM_B1 = 0.9
ADAM_B2 = 0.999
ADAM_EPS = 1e-08
ADAM_WD = 0.01
ADAM_STEP = 10

VMEM_LIMIT_V7X = 56 * 1024 * 1024
MESH_ID = pl.DeviceIdType.MESH


def _params(sem=None, vmem=VMEM_LIMIT_V7X):
    kw = {"vmem_limit_bytes": vmem}
    if sem is not None:
        kw["dimension_semantics"] = sem
    return pltpu.CompilerParams(**kw)


def _dot(a, b):
    return jnp.dot(a, b, preferred_element_type=F32)


def _dot_nt(a, b):
    return lax.dot_general(a, b, (((1,), (1,)), ((), ())), preferred_element_type=F32)


def _dot_exact(x, ones, parts):
    if ones.dtype == F32:
        return _dot(x, ones)
    acc = None
    rem = x
    for _ in range(parts):
        piece = rem.astype(BF16)
        rem = rem - piece.astype(F32)
        term = _dot(piece, ones)
        acc = term if acc is None else acc + term
    return acc


def _rms(v):
    return lax.rsqrt(jnp.mean(v * v, axis=-1, keepdims=True) + EPS)


def _tok(tm, w):
    return pl.BlockSpec((tm, w), lambda i: (i, 0))


def _whole(shape, single=False):
    nd = len(shape)
    if single:
        return pl.BlockSpec(shape, lambda i: (0,) * nd, pipeline_mode=pl.Buffered(1))
    return pl.BlockSpec(shape, lambda i: (0,) * nd)


def _feat(rows, tm):
    return pl.BlockSpec((rows, tm), lambda i: (0, i))


def _inproj_fwd(x, g_pre, w_t, tm):
    s = x.shape[0]

    def body(x_ref, g_ref, w_ref, h_ref, q_ref, k_ref, v_ref, kt_ref, vt_ref, zt_ref, b_ref, c_ref, u_ref):
        xv = x_ref[...]
        h = ((xv * _rms(xv)) * g_ref[...]).astype(MXU_DTYPE)
        h_ref[...] = h

        def proj(lo, hi):
            return _dot_nt(h, w_ref[lo:hi, :])

        q_ref[...] = (proj(0, 512) * Q_SCALE).astype(MXU_DTYPE)
        kt = _dot_nt(w_ref[512:1024, :], h)
        vt = _dot_nt(w_ref[1024:OFF_F, :], h)
        kt_ref[...] = kt.astype(MXU_DTYPE)
        vt_ref[...] = vt.astype(MXU_DTYPE)
        k_ref[...] = kt.T.astype(MXU_DTYPE)
        v_ref[...] = vt.T.astype(MXU_DTYPE)
        zt_ref[...] = _dot_nt(w_ref[OFF_F:OFF_B, :], h)
        b_ref[...] = proj(OFF_B, OFF_C).astype(MXU_DTYPE)
        c_ref[...] = proj(OFF_C, OFF_U).astype(MXU_DTYPE)
        u_ref[...] = proj(OFF_U, IN_PAD).astype(MXU_DTYPE)

    sd = jax.ShapeDtypeStruct
    return pl.pallas_call(
        body, name="inproj_fwd", grid=(s // tm,),
        in_specs=[_tok(tm, D_MODEL), _whole((1, D_MODEL)), _whole((IN_PAD, D_MODEL), single=True)],
        out_specs=[_tok(tm, D_MODEL), _tok(tm, 512), _tok(tm, 512), _tok(tm, 512), _feat(512, tm), _feat(512, tm),
                   _feat(128, tm), _tok(tm, 512), _tok(tm, 512), _tok(tm, 512)],
        out_shape=[sd((s, D_MODEL), MXU_DTYPE), sd((s, 512), MXU_DTYPE), sd((s, 512), MXU_DTYPE),
                   sd((s, 512), MXU_DTYPE), sd((512, s), MXU_DTYPE), sd((512, s), MXU_DTYPE), sd((128, s), F32),
                   sd((s, 512), MXU_DTYPE), sd((s, 512), MXU_DTYPE), sd((s, 512), MXU_DTYPE)],
        compiler_params=_params(("arbitrary",)),
    )(x, g_pre, w_t)


def _tri(n, upper):
    r = lax.broadcasted_iota(jnp.int32, (n, n), 0)
    c = lax.broadcasted_iota(jnp.int32, (n, n), 1)
    return ((r <= c) if upper else (r >= c)).astype(MXU_DTYPE)


HEAD_ROWS = 16


def _rows_to_cols(v):
    return jnp.concatenate([v, jnp.zeros((128 - HEAD_ROWS, 128), F32)], axis=0).T


BIAS_PARTS = 3


def _bias_placement():
    place_q = np.zeros((BIAS_PARTS, 128, ATTN_W), np.float32)
    place_k = np.zeros((BIAS_PARTS, 128, ATTN_W), np.float32)
    ones_q = np.zeros((1, ATTN_W), np.float32)
    ones_k = np.zeros((1, ATTN_W), np.float32)
    for h in range(N_HEADS):
        base = 2 * HEAD_DIM * (h // 2) + HEAD_DIM * (1 - h % 2)
        for part in range(BIAS_PARTS):
            place_q[part, h, base + part] = 1.0
            place_k[part, h, base + BIAS_PARTS + part] = -1.0
        ones_q[0, base + BIAS_PARTS:base + 2 * BIAS_PARTS] = 1.0
        ones_k[0, base:base + BIAS_PARTS] = 1.0
    return (jnp.asarray(place_q, MXU_DTYPE), jnp.asarray(place_k, MXU_DTYPE), jnp.asarray(ones_q), jnp.asarray(ones_k))


def _forget_fwd(z_t, b_col):
    s = z_t.shape[1]
    nb = s // 128

    def body(z_ref, b_ref, pq_ref, pk_ref, oq_ref, ok_ref, c_ref, cc_ref, qa_ref, ka_ref):
        upper = _tri(128, True)

        carry = jnp.zeros((HEAD_ROWS, 1), F32)
        for n in range(nb):
            off = n * 128
            lf = jax.nn.log_sigmoid(z_ref[0:HEAD_ROWS, off:off + 128] + b_ref[...])
            cs = _dot_exact(lf, upper, 3) + carry
            c_ref[:, off:off + 128] = cs
            cc_ref[off:off + 128, :] = _rows_to_cols(cs)
            carry = carry + jnp.sum(lf, axis=1, keepdims=True)

        rb = min(s, 512)
        for off in range(0, s, rb):
            qa = jnp.broadcast_to(oq_ref[...], (rb, ATTN_W))
            ka = jnp.broadcast_to(ok_ref[...], (rb, ATTN_W))
            rem = cc_ref[off:off + rb, :]
            for part in range(BIAS_PARTS):
                piece = rem.astype(MXU_DTYPE)
                rem = rem - piece.astype(F32)
                qa = qa + _dot(piece, pq_ref[part])
                ka = ka + _dot(piece, pk_ref[part])
            qa_ref[off:off + rb, :] = qa.astype(MXU_DTYPE)
            ka_ref[off:off + rb, :] = ka.astype(MXU_DTYPE)

    sd = jax.ShapeDtypeStruct
    return pl.pallas_call(body, name="forget_fwd",
                          out_shape=[sd((HEAD_ROWS, s), F32), sd((s, 128), F32), sd((s, ATTN_W), MXU_DTYPE),
                                     sd((s, ATTN_W), MXU_DTYPE)],
                          compiler_params=_params())(z_t, b_col, *_bias_placement())


def _aligned(start, size):
    return pl.ds(start if isinstance(start, int) else pl.multiple_of(start, size), size)


def _pair_lanes(pp):
    return _aligned(pp * 2 * HEAD_DIM, 2 * HEAD_DIM)


def _head_rows(h):
    return _aligned(h * HEAD_DIM, HEAD_DIM)


def _only_head(block, hb):
    lane = lax.broadcasted_iota(jnp.int32, block.shape, 1)
    return jnp.where((lane >= HEAD_DIM) if hb else (lane < HEAD_DIM), block, jnp.zeros_like(block))


def _head_col(cols, h):
    lane = lax.broadcasted_iota(jnp.int32, cols.shape, 1)
    return jnp.sum(jnp.where(lane == h, cols, 0.0), axis=1, keepdims=True)


def _other_head(block, other, hb):
    lane = lax.broadcasted_iota(jnp.int32, block.shape, 1)
    return jnp.where((lane >= HEAD_DIM) if hb else (lane < HEAD_DIM), block, other)


def _attn_fwd(qs, k, v_t, q_bias, k_bias, t, shards):
    s = qs.shape[0]
    n = s // t
    pairs = [(i, j) for i in range(n) for j in range(i + 1)]
    it = jnp.asarray(np.array([p[0] for p in pairs], np.int32))
    jt = jnp.asarray(np.array([p[1] for p in pairs], np.int32))
    nw = len(shards)
    last = len(pairs) - 1
    mid = (2 * len(pairs)) // 3

    def body(it_ref, jt_ref, q_ref, k_ref, vt_ref, qb_ref, kb_ref, *rest):
        sh, (o_ref, lse_ref), got = rest[:nw], rest[nw:nw + 2], rest[nw + 2:2 * nw + 2]
        m_sc, l_sc, acc_sc, send, recv = rest[2 * nw + 2:]
        p = pl.program_id(0)
        i = it_ref[p]
        j = jt_ref[p]
        gather_start, gather_forward, gather_finish = _gather_stages(sh, got, send, recv)
        pl.when(p == 0)(gather_start)
        if mid < last:
            pl.when(p == mid)(gather_forward)

        @pl.when(j == 0)
        def _():
            m_sc[...] = jnp.full_like(m_sc, -1e30)
            l_sc[...] = jnp.ones_like(l_sc)
            acc_sc[...] = jnp.zeros_like(acc_sc)

        def pair_step(pp, diagonal):
            lanes = _pair_lanes(pp)
            kp = k_ref[:, lanes]
            qp = q_ref[:, lanes]
            kb = kb_ref[:, lanes]
            qb = qb_ref[:, lanes]
            for hb in range(2):
                h = 2 * pp + hb
                row = pl.ds(h, 1)
                rows = _head_rows(h)
                st = _dot_nt(_other_head(kp, kb, hb), _other_head(qp, qb, hb))
                if diagonal:
                    kpos = lax.broadcasted_iota(jnp.int32, (t, t), 0)
                    qpos = lax.broadcasted_iota(jnp.int32, (t, t), 1)
                    st = jnp.where(kpos <= qpos, st, -1e30)
                m_prev = m_sc[row, :]
                m_new = jnp.maximum(m_prev, jnp.max(st, axis=0, keepdims=True))
                alpha = jnp.exp(m_prev - m_new)
                pt = jnp.exp(st - m_new)
                l_sc[row, :] = alpha * l_sc[row, :] + jnp.sum(pt, axis=0, keepdims=True)
                acc_sc[rows, :] = acc_sc[rows, :] * alpha + _dot(vt_ref[rows, :], pt.astype(MXU_DTYPE))
                m_sc[row, :] = m_new

        @pl.when(j < i)
        def _():
            for pp in range(N_HEADS // 2):
                pair_step(pp, False)

        @pl.when(j == i)
        def _():
            for pp in range(N_HEADS // 2):
                pair_step(pp, True)
                sub = lax.broadcasted_iota(jnp.int32, (2 * HEAD_DIM, t), 0)
                l_pair = jnp.where(sub < HEAD_DIM, l_sc[pl.ds(2 * pp, 1), :], l_sc[pl.ds(2 * pp + 1, 1), :])
                o_t = acc_sc[_aligned(pp * 2 * HEAD_DIM, 2 * HEAD_DIM), :] / l_pair
                o_ref[:, _pair_lanes(pp)] = o_t.T.astype(MXU_DTYPE)

            lse_ref[...] = m_sc[...] + jnp.log(l_sc[...])

        @pl.when(p == last)
        def _():
            if mid >= last:
                gather_forward()
            gather_finish()

    gs = pltpu.PrefetchScalarGridSpec(
        num_scalar_prefetch=2, grid=(len(pairs),),
        in_specs=[pl.BlockSpec((t, ATTN_W), lambda p, it_, jt_: (it_[p], 0)),
                  pl.BlockSpec((t, ATTN_W), lambda p, it_, jt_: (jt_[p], 0)),
                  pl.BlockSpec((ATTN_W, t), lambda p, it_, jt_: (0, jt_[p])),
                  pl.BlockSpec((t, ATTN_W), lambda p, it_, jt_: (it_[p], 0)),
                  pl.BlockSpec((t, ATTN_W), lambda p, it_, jt_: (jt_[p], 0))] + [ANY] * nw,
        out_specs=[pl.BlockSpec((t, ATTN_W), lambda p, it_, jt_: (it_[p], 0)),
                   pl.BlockSpec((HEAD_ROWS, t), lambda p, it_, jt_: (0, it_[p]))] + [ANY] * nw,
        scratch_shapes=[pltpu.VMEM((HEAD_ROWS, t), F32), pltpu.VMEM((HEAD_ROWS, t), F32), pltpu.VMEM((ATTN_W, t), F32),
                        pltpu.SemaphoreType.DMA((nw, 6)), pltpu.SemaphoreType.DMA((nw, 6))])
    o, lse, *got = pl.pallas_call(
        body, name="attn_fwd", grid_spec=gs,
        out_shape=[jax.ShapeDtypeStruct((s, ATTN_W), MXU_DTYPE), jax.ShapeDtypeStruct((HEAD_ROWS, s), F32)]
        + [jax.ShapeDtypeStruct((N_CHIPS,) + a.shape, a.dtype) for a in shards],
        compiler_params=_params(("arbitrary",)),
    )(it, jt, qs, k, v_t, q_bias, k_bias, *shards)
    me = 2 * lax.axis_index("x") + lax.axis_index("y")
    return o, lse, [lax.dynamic_update_index_in_dim(g, own, me, 0) for g, own in zip(got, shards)]


HALO = 16


def _halo_before(tm):
    return pl.BlockSpec((HALO, CONV_W), lambda i: (jnp.maximum(i * (tm // HALO) - 1, 0), 0))


def _shift_down(cur, prev_ref, first, cols=slice(None)):
    row = lax.broadcasted_iota(jnp.int32, cur.shape, 0)

    def before(r):
        prod = prev_ref[0][r:r + 1, cols].astype(F32) * prev_ref[1][r:r + 1, cols].astype(F32)
        return jnp.where(first, 0.0, prod)

    p7, p6 = before(HALO - 1), before(HALO - 2)
    s1 = jnp.where(row == 0, p7, pltpu.roll(cur, 1, 0))
    s2 = jnp.where(row == 0, p6, jnp.where(row == 1, p7, pltpu.roll(cur, 2, 0)))
    return s1, s2


def _group_ms(v, gmat):
    return _dot_exact(v, gmat, 1) * (1.0 / HEAD_DIM)


def _mixer_fwd(x, o_attn, gate_b, gate_c, u, conv_w, g_attn, g_conv, w_out, g_post, gmat, tm):
    s = x.shape[0]

    def body(x_ref, o_ref, b_ref, c_ref, u_ref, cp_ref, up_ref, cw_ref, ga_ref, gc_ref, wo_ref, gp_ref, gm_ref,
             x2_ref, mg_ref, y_ref, z_ref):
        i = pl.program_id(0)
        gm = gm_ref[0:128, 0:128]
        for lo in range(0, ATTN_W, 128):
            cols = slice(lo, lo + 128)
            cu = c_ref[:, cols].astype(F32) * u_ref[:, cols].astype(F32)
            cu1, cu2 = _shift_down(cu, (cp_ref, up_ref), i == 0, cols)
            z = cw_ref[0:1, cols] * cu2 + cw_ref[1:2, cols] * cu1 + cw_ref[2:3, cols] * cu
            z_ref[:, cols] = z.astype(MXU_DTYPE)
            cv = b_ref[:, cols].astype(F32) * z
            ov = o_ref[:, cols].astype(F32)
            mg_ref[:, cols] = ((ov * lax.rsqrt(_group_ms(ov * ov, gm) + EPS)) * ga_ref[:, cols]).astype(MXU_DTYPE)
            mg_ref[:, ATTN_W + lo:ATTN_W + lo + 128] = (
                (cv * lax.rsqrt(_group_ms(cv * cv, gm) + EPS)) * gc_ref[:, cols]).astype(MXU_DTYPE)
        y = _dot(mg_ref[...], wo_ref[...])
        y_ref[...] = y
        x2_ref[...] = x_ref[...] + (y * _rms(y)) * gp_ref[...]

    halo = _halo_before(tm)
    sd = jax.ShapeDtypeStruct
    return pl.pallas_call(
        body, name="mixer_fwd", grid=(s // tm,),
        in_specs=[_tok(tm, D_MODEL), _tok(tm, 512), _tok(tm, 512), _tok(tm, 512), _tok(tm, 512), halo, halo,
                  _whole((3, 512)), _whole((1, 512)), _whole((1, 512)), _whole((D_MODEL, D_MODEL), single=True),
                  _whole((1, D_MODEL)), _whole((512, 512))],
        out_specs=[_tok(tm, D_MODEL), _tok(tm, D_MODEL), _tok(tm, D_MODEL), _tok(tm, 512)],
        out_shape=[sd((s, D_MODEL), F32), sd((s, D_MODEL), MXU_DTYPE), sd((s, D_MODEL), F32), sd((s, 512), MXU_DTYPE)],
        compiler_params=_params(("arbitrary",)),
    )(x, o_attn, gate_b, gate_c, u, gate_c, u, conv_w, g_attn, g_conv, w_out, g_post, gmat)


def _ffn_fwd(x2, target, g_pre, w_gu, w_dn, g_post, tm):
    s = x2.shape[0]

    def body(x_ref, t_ref, gpre_ref, wgu_ref, wdn_ref, gpost_ref,
             h_ref, g_ref, up_ref, a_ref, ff_ref, dout_ref, loss_ref):
        xv = x_ref[...]
        h = ((xv * _rms(xv)) * gpre_ref[...]).astype(MXU_DTYPE)
        h_ref[...] = h
        ff = jnp.zeros((tm, D_MODEL), F32)
        for j in range(2):
            cols = slice(j * FF_PIECE, (j + 1) * FF_PIECE)
            g = _dot(h, wgu_ref[j])
            up = _dot(h, wgu_ref[2 + j])
            a = ((g * jax.nn.sigmoid(g)) * up).astype(MXU_DTYPE)
            g_ref[:, cols] = g.astype(MXU_DTYPE)
            up_ref[:, cols] = up.astype(MXU_DTYPE)
            a_ref[:, cols] = a
            ff = ff + _dot(a, wdn_ref[j])
        ff_ref[...] = ff
        err = (xv + (ff * _rms(ff)) * gpost_ref[...]) - t_ref[...]
        dout_ref[...] = err * (1.0 / D_MODEL)
        part = jnp.sum(jnp.mean(err * err, axis=-1, keepdims=True), axis=0, keepdims=True)

        @pl.when(pl.program_id(0) == 0)
        def _():
            loss_ref[...] = jnp.zeros_like(loss_ref)

        loss_ref[...] += part

    sd = jax.ShapeDtypeStruct
    return pl.pallas_call(
        body, name="ffn_fwd", grid=(s // tm,),
        in_specs=[_tok(tm, D_MODEL), _tok(tm, D_MODEL), _whole((1, D_MODEL)),
                  _whole((4, D_MODEL, FF_PIECE), single=True), _whole((2, FF_PIECE, D_MODEL), single=True),
                  _whole((1, D_MODEL))],
        out_specs=[_tok(tm, D_MODEL), _tok(tm, D_FF), _tok(tm, D_FF), _tok(tm, D_FF), _tok(tm, D_MODEL),
                   _tok(tm, D_MODEL), _whole((8, 128))],
        out_shape=[sd((s, D_MODEL), MXU_DTYPE), sd((s, D_FF), MXU_DTYPE), sd((s, D_FF), MXU_DTYPE),
                   sd((s, D_FF), MXU_DTYPE), sd((s, D_MODEL), F32), sd((s, D_MODEL), F32), sd((8, 128), F32)],
        compiler_params=_params(("arbitrary",)),
    )(x2, target, g_pre, w_gu, w_dn, g_post)


def _norm_bwd(dy, normed, rinv, gain):
    t = dy * gain
    return rinv * (t - normed * jnp.mean(t * normed, axis=-1, keepdims=True))


def _acc_rows(ref, first, val):
    @pl.when(first)
    def _():
        ref[...] = jnp.zeros_like(ref)

    ref[...] += jnp.sum(val, axis=0, keepdims=True)


def _ffn_bwd(dout, ff, x2, g, up, g_post, g_pre, w_gu, w_dn, tm):
    s = x2.shape[0]

    def body(do_ref, ff_ref, x_ref, g_ref, up_ref, gpost_ref, gpre_ref, wgu_ref, wdn_ref,
             dx_ref, dff_ref, dgu_ref, dgpost_ref, dgpre_ref):
        first = pl.program_id(0) == 0
        ffv = ff_ref[...]
        rf = _rms(ffv)
        n = ffv * rf
        do = do_ref[...]
        _acc_rows(dgpost_ref, first, do * n)
        dff = _norm_bwd(do, n, rf, gpost_ref[...]).astype(MXU_DTYPE)
        dff_ref[...] = dff
        dh = jnp.zeros((tm, D_MODEL), F32)
        for j in range(2):
            cols = slice(j * FF_PIECE, (j + 1) * FF_PIECE)
            da = _dot_nt(dff, wdn_ref[j])
            gv = g_ref[:, cols].astype(F32)
            sg = jax.nn.sigmoid(gv)
            dg = (da * up_ref[:, cols].astype(F32) * (sg * (1.0 + gv * (1.0 - sg)))).astype(MXU_DTYPE)
            du = (da * (gv * sg)).astype(MXU_DTYPE)
            dgu_ref[:, cols] = dg
            dgu_ref[:, D_FF + j * FF_PIECE:D_FF + (j + 1) * FF_PIECE] = du
            dh = dh + _dot_nt(dg, wgu_ref[j]) + _dot_nt(du, wgu_ref[2 + j])
        xv = x_ref[...]
        r2 = _rms(xv)
        nx = xv * r2
        _acc_rows(dgpre_ref, first, dh * nx)
        dx_ref[...] = do + _norm_bwd(dh, nx, r2, gpre_ref[...])

    sd = jax.ShapeDtypeStruct
    return pl.pallas_call(
        body, name="ffn_bwd", grid=(s // tm,),
        in_specs=[_tok(tm, D_MODEL), _tok(tm, D_MODEL), _tok(tm, D_MODEL), _tok(tm, D_FF), _tok(tm, D_FF),
                  _whole((1, D_MODEL)), _whole((1, D_MODEL)),
                  _whole((4, D_MODEL, FF_PIECE), single=True), _whole((2, FF_PIECE, D_MODEL), single=True)],
        out_specs=[_tok(tm, D_MODEL), _tok(tm, D_MODEL), _tok(tm, 2 * D_FF), _whole((1, D_MODEL)),
                   _whole((1, D_MODEL))],
        out_shape=[sd((s, D_MODEL), F32), sd((s, D_MODEL), MXU_DTYPE), sd((s, 2 * D_FF), MXU_DTYPE),
                   sd((1, D_MODEL), F32), sd((1, D_MODEL), F32)],
        compiler_params=_params(("arbitrary",)),
    )(dout, ff, x2, g, up, g_post, g_pre, w_gu, w_dn)


def _tn_matmul(a, b, tm, tn, tk, name, totals=()):
    s, m = a.shape
    n = b.shape[1]
    nw = len(totals)
    grid = (m // tm, n // tn, s // tk)

    def body(a_ref, b_ref, *rest):
        o_ref = rest[nw]
        if nw:
            step = (pl.program_id(0) * grid[1] + pl.program_id(1)) * grid[2] + pl.program_id(2)
            share_start, share_finish = _share_stages(rest[:nw], rest[nw + 1:2 * nw + 1], *rest[2 * nw + 1:])
            pl.when(step == 0)(share_start)

        @pl.when(pl.program_id(2) == 0)
        def _():
            o_ref[...] = jnp.zeros_like(o_ref)

        o_ref[...] += lax.dot_general(a_ref[...], b_ref[...], (((0,), (0,)), ((), ())), preferred_element_type=F32)
        if nw:
            pl.when(step == grid[0] * grid[1] * grid[2] - 1)(share_finish)

    out = pl.pallas_call(
        body, name=name, grid=grid,
        in_specs=[pl.BlockSpec((tk, tm), lambda i, j, kk: (kk, i)), pl.BlockSpec((tk, tn), lambda i, j, kk: (kk, j))]
        + [ANY] * nw,
        out_specs=[pl.BlockSpec((tm, tn), lambda i, j, kk: (i, j))] + [ANY] * nw,
        out_shape=[jax.ShapeDtypeStruct((m, n), F32)] + [jax.ShapeDtypeStruct(t.shape, t.dtype) for t in totals],
        scratch_shapes=[pltpu.SemaphoreType.DMA((nw,)), pltpu.SemaphoreType.DMA((nw,))] if nw else [],
        compiler_params=_params(("arbitrary", "arbitrary", "arbitrary")),
    )(a, b, *totals)
    return (out[0], out[1:]) if nw else out[0]


def _mixer_bwd(dx2, y, o_attn, gate_b, z, g_post, g_attn, g_conv, w_out, gmat, sel, tm, ready, kinds):
    s = dx2.shape[0]
    nw = len(ready)
    nt = s // tm

    def body(d_ref, y_ref, o_ref, b_ref, z_ref, gp_ref, ga_ref, gc_ref, wo_ref, gm_ref, sel_ref, *rest):
        grads = rest[:nw]
        dy_ref, do_ref, db_ref, dz_ref, delta_ref, dgp_ref, dga_ref, dgc_ref = rest[nw:nw + 8]
        taken = rest[nw + 8:2 * nw + 8]
        send, recv = rest[2 * nw + 8:]
        first = pl.program_id(0) == 0
        pair_start, pair_finish = _pair_stages(grads, kinds, taken, send, recv)
        pl.when(first)(pair_start)
        yv = y_ref[...]
        ry = _rms(yv)
        ny = yv * ry
        d = d_ref[...]
        _acc_rows(dgp_ref, first, d * ny)
        dy = _norm_bwd(d, ny, ry, gp_ref[...]).astype(MXU_DTYPE)
        dy_ref[...] = dy
        dm = _dot_nt(dy, wo_ref[...])
        gm = gm_ref[0:128, 0:128]

        @pl.when(first)
        def _():
            dga_ref[...] = jnp.zeros_like(dga_ref)
            dgc_ref[...] = jnp.zeros_like(dgc_ref)

        def group_bwd(val, dmv, gain_ref, dg_ref, cols):
            rg = lax.rsqrt(_group_ms(val * val, gm) + EPS)
            nv = val * rg
            dg_ref[:, cols] += jnp.sum(dmv * nv, axis=0, keepdims=True)
            t = dmv * gain_ref[:, cols]
            return rg * (t - nv * _group_ms(t * nv, gm))

        delta = jnp.zeros((tm, 128), F32)
        for lo in range(0, ATTN_W, 128):
            cols = slice(lo, lo + 128)
            ov = o_ref[:, cols].astype(F32)
            d_o = group_bwd(ov, dm[:, cols], ga_ref, dga_ref, cols)
            do_ref[:, cols] = d_o.astype(MXU_DTYPE)
            delta = delta + _dot_exact(d_o * ov, sel_ref[cols, :], 2)
            zv = z_ref[:, cols].astype(F32)
            bv = b_ref[:, cols].astype(F32)
            d_cv = group_bwd(bv * zv, dm[:, ATTN_W + lo:ATTN_W + lo + 128], gc_ref, dgc_ref, cols)
            db_ref[:, cols] = (d_cv * zv).astype(MXU_DTYPE)
            dz_ref[:, cols] = d_cv * bv
        delta_ref[...] = delta.T[0:HEAD_ROWS, :]
        pl.when(pl.program_id(0) == nt - 1)(pair_finish)

    sd = jax.ShapeDtypeStruct
    taken_shape = [sd((N_CHIPS, g.shape[-2], g.shape[-1] if kd == "rows" else g.shape[-1] // N_CHIPS), F32)
                   for g, kd in zip(ready, kinds)]
    out = pl.pallas_call(
        body, name="mixer_bwd", grid=(nt,),
        in_specs=[_tok(tm, D_MODEL), _tok(tm, D_MODEL), _tok(tm, 512), _tok(tm, 512), _tok(tm, 512),
                  _whole((1, D_MODEL)), _whole((1, 512)), _whole((1, 512)),
                  _whole((D_MODEL, D_MODEL), single=True), _whole((512, 512)), _whole((512, 128))] + [ANY] * nw,
        out_specs=[_tok(tm, D_MODEL), _tok(tm, 512), _tok(tm, 512), _tok(tm, 512), _feat(HEAD_ROWS, tm),
                   _whole((1, D_MODEL)), _whole((1, 512)), _whole((1, 512))] + [ANY] * nw,
        out_shape=[sd((s, D_MODEL), MXU_DTYPE), sd((s, 512), MXU_DTYPE), sd((s, 512), MXU_DTYPE), sd((s, 512), F32),
                   sd((HEAD_ROWS, s), F32), sd((1, D_MODEL), F32), sd((1, 512), F32), sd((1, 512), F32)] + taken_shape,
        scratch_shapes=[pltpu.SemaphoreType.DMA((nw, N_CHIPS)), pltpu.SemaphoreType.DMA((nw, N_CHIPS))],
        compiler_params=_params(("arbitrary",)),
    )(dx2, y, o_attn, gate_b, z, g_post, g_attn, g_conv, w_out, gmat, sel, *ready)
    return out[:8], out[8:]


def _attn_bwd(qs, k, k_t, v, do, c_rows, c_cols, lse, delta, t, parts):
    s = qs.shape[0]
    n = s // t
    pairs = [(i, j) for j in range(n) for i in range(j, n)]
    it = jnp.asarray(np.array([p[0] for p in pairs], np.int32))
    jt = jnp.asarray(np.array([p[1] for p in pairs], np.int32))

    nw = len(parts)

    def body(it_ref, jt_ref, q_ref, k_ref, kt_ref, v_ref, do_ref, cq_ref, ck_ref, lse_ref, dl_ref, *rest):
        pb = rest[:nw]
        dq_ref, dk_ref, dv_ref, dc_ref, dcq_ref = rest[nw:nw + 5]
        rcv = rest[nw + 5:2 * nw + 5]
        dk_sc, dv_sc, dc_sc, send, recv = rest[2 * nw + 5:]
        p = pl.program_id(0)
        i = it_ref[p]
        j = jt_ref[p]
        chip_start, chip_finish = _chip_stages(pb, rcv, send, recv)

        @pl.when(p == 0)
        def _():
            chip_start()
            dq_ref[...] = jnp.zeros_like(dq_ref)
            dcq_ref[...] = jnp.zeros_like(dcq_ref)

        @pl.when(i == j)
        def _():
            dk_sc[...] = jnp.zeros_like(dk_sc)
            dv_sc[...] = jnp.zeros_like(dv_sc)
            dc_sc[...] = jnp.zeros_like(dc_sc)

        def pair_step(pp, diagonal):
            lanes = _pair_lanes(pp)
            qp = q_ref[:, lanes]
            kp = k_ref[:, lanes]
            vp = v_ref[:, lanes]
            dop = do_ref[:, lanes]
            ck_all = ck_ref[...]
            lane = lax.broadcasted_iota(jnp.int32, (t, 128), 1)
            for hb in range(2):
                h = 2 * pp + hb
                row = pl.ds(h, 1)
                bias = (cq_ref[row, :] - lse_ref[row, :]) - _head_col(ck_all, h)
                pt = jnp.exp(_dot_nt(_only_head(kp, hb), qp) + bias)
                if diagonal:
                    kpos = lax.broadcasted_iota(jnp.int32, (t, t), 0)
                    qpos = lax.broadcasted_iota(jnp.int32, (t, t), 1)
                    pt = jnp.where(kpos <= qpos, pt, 0.0)
                dv_sc[:, lanes] += _dot(pt.astype(MXU_DTYPE), _only_head(dop, hb))
                dst = pt * (_dot_nt(_only_head(vp, hb), dop) - dl_ref[row, :])
                dc_sc[...] -= jnp.where(lane == h, jnp.sum(dst, axis=1, keepdims=True), 0.0)
                dcq_ref[i, row, :] += jnp.sum(dst, axis=0, keepdims=True)
                dsb = dst.astype(MXU_DTYPE)
                dk_sc[:, lanes] += _dot(dsb, _only_head(qp, hb))
                rows = _head_rows(h)
                dq_ref[i, rows, :] += _dot(kt_ref[rows, :], dsb)

        @pl.when(i > j)
        def _():
            for pp in range(N_HEADS // 2):
                pair_step(pp, False)

        @pl.when(i == j)
        def _():
            for pp in range(N_HEADS // 2):
                pair_step(pp, True)

        @pl.when(i == n - 1)
        def _():
            dk_ref[...] = dk_sc[...].astype(MXU_DTYPE)
            dv_ref[...] = dv_sc[...].astype(MXU_DTYPE)
            dc_ref[...] = dc_sc[...]

        pl.when(p == len(pairs) - 1)(chip_finish)

    qi = lambda p, it_, jt_: (it_[p], 0)
    kj = lambda p, it_, jt_: (jt_[p], 0)
    row_i = lambda p, it_, jt_: (0, it_[p])
    gs = pltpu.PrefetchScalarGridSpec(
        num_scalar_prefetch=2, grid=(len(pairs),),
        in_specs=[pl.BlockSpec((t, ATTN_W), qi), pl.BlockSpec((t, ATTN_W), kj),
                  pl.BlockSpec((ATTN_W, t), lambda p, it_, jt_: (0, jt_[p])),
                  pl.BlockSpec((t, ATTN_W), kj), pl.BlockSpec((t, ATTN_W), qi),
                  pl.BlockSpec((HEAD_ROWS, t), row_i), pl.BlockSpec((t, 128), kj),
                  pl.BlockSpec((HEAD_ROWS, t), row_i), pl.BlockSpec((HEAD_ROWS, t), row_i)] + [ANY] * nw,
        out_specs=[pl.BlockSpec((n, ATTN_W, t), lambda p, it_, jt_: (0, 0, 0)),
                   pl.BlockSpec((t, ATTN_W), kj), pl.BlockSpec((t, ATTN_W), kj),
                   pl.BlockSpec((t, 128), kj),
                   pl.BlockSpec((n, HEAD_ROWS, t), lambda p, it_, jt_: (0, 0, 0))] + [ANY] * nw,
        scratch_shapes=[pltpu.VMEM((t, ATTN_W), F32), pltpu.VMEM((t, ATTN_W), F32),
                        pltpu.VMEM((t, 128), F32), pltpu.SemaphoreType.DMA((nw, 3)), pltpu.SemaphoreType.DMA((nw, 3))])
    sd = jax.ShapeDtypeStruct
    out = pl.pallas_call(
        body, name="attn_bwd", grid_spec=gs,
        out_shape=[sd((n, ATTN_W, t), F32), sd((s, ATTN_W), MXU_DTYPE), sd((s, ATTN_W), MXU_DTYPE),
                   sd((s, 128), F32), sd((n, HEAD_ROWS, t), F32)] + [sd((3,) + a.shape[1:], a.dtype) for a in parts],
        compiler_params=_params(("arbitrary",)),
    )(it, jt, qs, k, k_t, v, do, c_rows, c_cols, lse, delta, *parts)
    return out[:5], out[5:]


def _forget_bwd(dc_rows, dc_cols, z_t, b_col):
    s = z_t.shape[1]
    nb = s // 128

    def body(dr_ref, dcc_ref, z_ref, b_ref, dz_ref, db_ref):
        lower = _tri(128, False)
        real = lax.broadcasted_iota(jnp.int32, (HEAD_ROWS, 128), 0) < N_HEADS

        tail = jnp.zeros((HEAD_ROWS, 1), F32)
        dbias = jnp.zeros((HEAD_ROWS, 1), F32)
        for m in range(nb):
            off = (nb - 1 - m) * 128
            dc = dr_ref[:, off:off + 128] + dcc_ref[off:off + 128, :].T[0:HEAD_ROWS, :]
            dlf = _dot_exact(dc, lower, 3) + tail
            dz = dlf * jax.nn.sigmoid(-(z_ref[0:HEAD_ROWS, off:off + 128] + b_ref[...]))
            dz = jnp.where(real, dz, 0.0)
            dz_ref[off:off + 128, :] = _rows_to_cols(dz)
            tail = tail + jnp.sum(dc, axis=1, keepdims=True)
            dbias = dbias + jnp.sum(dz, axis=1, keepdims=True)
        db_ref[...] = jnp.broadcast_to(dbias, db_ref.shape)

    return pl.pallas_call(
        body, name="forget_bwd",
        out_shape=[jax.ShapeDtypeStruct((s, 128), F32), jax.ShapeDtypeStruct((HEAD_ROWS, 128), F32)],
        compiler_params=_params())(dc_rows, dc_cols, z_t, b_col)


def _inproj_bwd(dz, gate_c, u, conv_w, dq, dk, dv, dzf, db, x, dx2, g_pre, w_t, tm):
    s = x.shape[0]
    nt = s // tm
    t = dq.shape[2]
    assert t % tm == 0 and dq.shape[:2] == (s // t, ATTN_W)
    per = t // tm

    def body(dz_ref, dzn_ref, c_ref, u_ref, cp_ref, up_ref, cw_ref, dq_ref, dk_ref, dv_ref, dzf_ref, db_ref,
             x_ref, dx2_ref, g_ref, w_ref, gx_ref, dp_ref, dg_ref, dcw_ref):
        i = pl.program_id(0)
        first = i == 0
        last = i == nt - 1
        @pl.when(first)
        def _():
            dcw_ref[...] = jnp.zeros_like(dcw_ref)

        for lo in range(0, CONV_W, 128):
            cols = slice(lo, lo + 128)
            dzv = dz_ref[:, cols]
            row = lax.broadcasted_iota(jnp.int32, dzv.shape, 0)
            n0 = jnp.where(last, 0.0, dzn_ref[0:1, cols])
            n1 = jnp.where(last, 0.0, dzn_ref[1:2, cols])
            dz1 = jnp.where(row == tm - 1, n0, pltpu.roll(dzv, tm - 1, 0))
            dz2 = jnp.where(row == tm - 1, n1, jnp.where(row == tm - 2, n0, pltpu.roll(dzv, tm - 2, 0)))
            dcu = cw_ref[2:3, cols] * dzv + cw_ref[1:2, cols] * dz1 + cw_ref[0:1, cols] * dz2
            cv = c_ref[:, cols].astype(F32)
            uv = u_ref[:, cols].astype(F32)
            cu = cv * uv
            cu1, cu2 = _shift_down(cu, (cp_ref, up_ref), first, cols)
            dcw_ref[0:1, cols] += jnp.sum(dzv * cu2, axis=0, keepdims=True)
            dcw_ref[1:2, cols] += jnp.sum(dzv * cu1, axis=0, keepdims=True)
            dcw_ref[2:3, cols] += jnp.sum(dzv * cu, axis=0, keepdims=True)
            dp_ref[:, OFF_C + lo:OFF_C + lo + 128] = (dcu * uv).astype(MXU_DTYPE)
            dp_ref[:, OFF_U + lo:OFF_U + lo + 128] = (dcu * cv).astype(MXU_DTYPE)

        dp_ref[:, 0:512] = (dq_ref[0].T * Q_SCALE).astype(MXU_DTYPE)
        dp_ref[:, 512:1024] = dk_ref[...].astype(MXU_DTYPE)
        dp_ref[:, 1024:OFF_F] = dv_ref[...].astype(MXU_DTYPE)
        dp_ref[:, OFF_F:OFF_B] = dzf_ref[...].astype(MXU_DTYPE)
        dp_ref[:, OFF_B:OFF_C] = db_ref[...].astype(MXU_DTYPE)
        dh = _dot(dp_ref[...], w_ref[...])
        xv = x_ref[...]
        r1 = _rms(xv)
        nx = xv * r1
        _acc_rows(dg_ref, first, dh * nx)
        gx_ref[...] = dx2_ref[...] + _norm_bwd(dh, nx, r1, g_ref[...])

    prev = _halo_before(tm)
    nxt = pl.BlockSpec((8, 512), lambda i: (jnp.minimum((i + 1) * (tm // 8), s // 8 - 1), 0))
    sd = jax.ShapeDtypeStruct
    return pl.pallas_call(
        body, name="inproj_bwd", grid=(nt,),
        in_specs=[_tok(tm, 512), nxt, _tok(tm, 512), _tok(tm, 512), prev, prev, _whole((3, 512)),
                  pl.BlockSpec((1, ATTN_W, tm), lambda i: (i // per, 0, i % per)), _tok(tm, 512), _tok(tm, 512),
                  _tok(tm, 128),
                  _tok(tm, 512),
                  _tok(tm, D_MODEL), _tok(tm, D_MODEL), _whole((1, D_MODEL)), _whole((IN_PAD, D_MODEL), single=True)],
        out_specs=[_tok(tm, D_MODEL), _tok(tm, IN_PAD), _whole((1, D_MODEL)), _whole((8, 512))],
        out_shape=[sd((s, D_MODEL), F32), sd((s, IN_PAD), MXU_DTYPE), sd((1, D_MODEL), F32), sd((8, 512), F32)],
        compiler_params=_params(("arbitrary",)),
    )(dz, dz, gate_c, u, gate_c, u, conv_w, dq, dk, dv, dzf, db, x, dx2, g_pre, w_t)


def _tile(s, want):
    return want if s % want == 0 else s


def _halves(a):
    return a.reshape(2, a.shape[0] // 2, a.shape[1])


def _device_step(x, target, w, mom1, mom2, w_in_t, m_in_t, v_in_t, c_idx, me_idx):
    s = x.shape[0]
    tm = _tile(s, 512)
    tf = _tile(s, 256)
    ta = _tile(s, 512)
    tkk = _tile(s, 2048)
    gidx = np.arange(512) // HEAD_DIM
    gmat = jnp.asarray(gidx[:, None] == gidx[None, :], MXU_DTYPE)
    sel = jnp.asarray(gidx[:, None] == np.arange(128)[None, :], MXU_DTYPE)
    g_mix_pre, g_mix_post, g_ffn_pre, g_ffn_post = w["g_mix_pre"], w["g_mix_post"], w["g_ffn_pre"], w["g_ffn_post"]
    g_attn, g_conv, b_forget = w["g_attn_out"], w["g_conv_out"], w["b_forget"]
    shard = {n: _halves(w[n][0].astype(MXU_DTYPE)) for n in BIG[1:]}
    piece_rows = IN_W // N_CHIPS

    g_in, conv_all = _gather_weights([w_in_t.reshape(piece_rows, D_MODEL).astype(MXU_DTYPE)], w["conv_w"][0])
    w_rows = g_in.reshape(IN_W, D_MODEL)
    w_t = jnp.concatenate([w_rows[:OFF_F + N_HEADS], jnp.zeros((OFF_B - OFF_F - N_HEADS, D_MODEL), MXU_DTYPE),
                           w_rows[OFF_F + N_HEADS:]], axis=0)
    conv_w = jnp.transpose(conv_all, (1, 0, 2)).reshape(3, CONV_W)

    h1, qs, k, v, k_t, v_t, z_t, gate_b, gate_c, u = _inproj_fwd(x, g_mix_pre, w_t, tm)
    b_col = jnp.pad(jnp.transpose(b_forget), ((0, HEAD_ROWS - N_HEADS), (0, 0)))
    c_rows, c_cols, q_bias, k_bias = _forget_fwd(z_t, b_col)
    o_attn, lse, (g_out, g_gu, g_dn) = _attn_fwd(qs, k, v_t, q_bias, k_bias, ta,
                                                 [shard["w_out"], shard["w_gate_up"], shard["w_down"]])
    w_out = g_out.reshape(D_MODEL, D_MODEL)
    w_gu = g_gu.reshape(N_CHIPS, D_MODEL, FF_PIECE)
    w_dn = g_dn.reshape(2, FF_PIECE, D_MODEL)
    x2, merged, y, z = _mixer_fwd(x, o_attn, gate_b, gate_c, u, conv_w, g_attn, g_conv, w_out, g_mix_post, gmat, tm)
    h2, g, up, a, ff, dout, loss_acc = _ffn_fwd(x2, target, g_ffn_pre, w_gu, w_dn, g_ffn_post, tf)

    dx2, dff, dgu, dg_ffn_post, dg_ffn_pre = _ffn_bwd(dout, ff, x2, g, up, g_ffn_post, g_ffn_pre, w_gu, w_dn, tf)
    dw_dn = _tn_matmul(a, dff, FF_PIECE, 1024, tkk, "dw_down").reshape(N_CHIPS, 2, D_FF // (2 * N_CHIPS), D_MODEL)
    dw_gu = _tn_matmul(h2, dgu, 1024, FF_PIECE, tkk, "dw_gate_up").reshape(2, D_MODEL // 2, 2 * D_FF)
    (dy, d_o, d_b, dz, delta, dg_mix_post, dg_attn, dg_conv), (a_gu, a_dn) = _mixer_bwd(
        dx2, y, o_attn, gate_b, z, g_mix_post, g_attn, g_conv, w_out, gmat, sel, tm, [dw_gu, dw_dn], ["cols", "rows"])
    dw_out = _tn_matmul(merged, dy, 1024, 1024, tkk, "dw_out").reshape(N_CHIPS, 2, D_MODEL // (2 * N_CHIPS), D_MODEL)
    place = jnp.concatenate([c_idx, me_idx])
    *sum_gu, a_out = _pair_sum(place, dw_gu, "cols", a_gu, "pair_sum_w_gate_up", [dw_out], ["rows"])
    sum_dn = _pair_sum(place, dw_dn, "rows", a_dn, "pair_sum_w_down")
    sum_out = _pair_sum(place, dw_out, "rows", a_out, "pair_sum_w_out")
    (dq_t, dk, dv, dc_cols, dcq), (r_gu, r_dn, r_out) = _attn_bwd(
        qs, k, k_t, v, d_o, c_rows, c_cols, lse, delta, ta, [sum_gu[1], sum_dn[1], sum_out[1]])
    dc_rows = jnp.transpose(dcq, (1, 0, 2)).reshape(HEAD_ROWS, s)
    dzf, db_f = _forget_bwd(dc_rows, dc_cols, z_t, b_col)
    grad_x, dproj, dg_mix_pre, dcw = _inproj_bwd(dz, gate_c, u, conv_w, dq_t, dk, dv, dzf, d_b,
                                                 x, dx2, g_mix_pre, w_t, tm)
    done = [_chip_sum(sb[0], r, "chip_sum_" + n)
            for n, sb, r in zip(BIG[1:], (sum_out, sum_gu, sum_dn), (r_out, r_gu, r_dn))]
    dw_t, done_theirs = _tn_matmul(dproj, h1, 640, 1024, tkk, "dw_in", done)
    dw_in = jnp.concatenate([dw_t[:OFF_F + N_HEADS], dw_t[OFF_B:]], axis=0).reshape(N_CHIPS, piece_rows, D_MODEL)

    (a_in,) = _pair_exchange([dw_in], ["lanes"])
    sum_in = _pair_sum(place, dw_in, "lanes", a_in, "pair_sum_w_in")
    small = dict(b_forget=db_f[:N_HEADS, 0], g_attn_out=dg_attn, g_conv_out=dg_conv, g_mix_pre=dg_mix_pre,
                 g_mix_post=dg_mix_post, g_ffn_pre=dg_ffn_pre, g_ffn_post=dg_ffn_post)
    (r_in,), small_all = _chip_exchange([sum_in[1]], _pack_small(small, dcw[:3], loss_acc[0, 0]))
    t_in = _chip_sum(sum_in[0], r_in, "chip_sum_w_in")
    (s_in,) = _pair_share([t_in], "pair_share_w_in")
    new = {"w_in": _adamw_lanes(c_idx, w_in_t, t_in, s_in, m_in_t, v_in_t, "adamw_w_in")}
    for n, mine, theirs in zip(BIG[1:], done, done_theirs):
        new[n] = _adamw(c_idx, w[n][0], mine, theirs, mom1[n][0], mom2[n][0], 2, "adamw_" + n)
    return grad_x, new, small_all


BIG = ("w_in", "w_out", "w_gate_up", "w_down")
ANY = pl.BlockSpec(memory_space=pl.ANY)


def _place():
    x, y, c = lax.axis_index("x"), lax.axis_index("y"), lax.axis_index("c")
    others = [(1 - x, y), (x, 1 - y), (1 - x, 1 - y)]
    return x, y, c, 2 * x + y, others, [2 * px + py for px, py in others]


def _remote(src, dst, send, recv, dev):
    return pltpu.make_async_remote_copy(src_ref=src, dst_ref=dst, send_sem=send, recv_sem=recv,
                                        device_id=dev, device_id_type=MESH_ID)


def _gather_stages(sh, outs, send, recv):
    x, y, c, me, others, chips = _place()
    sib = (x, y, 1 - c)
    every = [(w, kk) for w in range(len(sh)) for kk in range(3)]

    def half_of(ref, half, piece=None):
        ref = ref if piece is None else ref.at[piece]
        if len(ref.shape) == 3:
            return ref.at[half]
        hc = ref.shape[1] // 2
        return ref.at[:, pl.ds(pl.multiple_of(half * hc, 128), hc)]

    def first(w, kk):
        return _remote(half_of(sh[w], c), half_of(outs[w], c, me), send.at[w, kk], recv.at[w, kk], (*others[kk], c))

    def landed(w, kk):
        r = half_of(outs[w], c, chips[kk])
        return _remote(r, r, send.at[w, kk], recv.at[w, kk], (*others[kk], c))

    def onward(w, kk, half):
        r = half_of(outs[w], half, chips[kk])
        return _remote(r, r, send.at[w, 3 + kk], recv.at[w, 3 + kk], sib)

    def start():
        for w, kk in every:
            first(w, kk).start()

    def forward():
        for w, kk in every:
            landed(w, kk).wait_recv()
            onward(w, kk, c).start()

    def finish():
        for w, kk in every:
            onward(w, kk, 1 - c).wait_recv()
        for w, kk in every:
            first(w, kk).wait_send()
            onward(w, kk, c).wait_send()

    return start, forward, finish


def _pair_piece(ref, kind, p, half):
    if kind == "rows":
        return ref.at[p, half]
    if kind == "lanes":
        hc = ref.shape[2] // 2
        return ref.at[p, :, pl.ds(pl.multiple_of(half * hc, 128), hc)]
    cols = ref.shape[2] // N_CHIPS
    return ref.at[half, :, pl.ds(p * cols, cols)]


def _pair_stages(g, kinds, a, send, recv):
    x, y, c, _, _, _ = _place()
    copies = [_remote(_pair_piece(g[w], kinds[w], p, 1 - c), a[w].at[p], send.at[w, p], recv.at[w, p], (x, y, 1 - c))
              for w in range(len(g)) for p in range(N_CHIPS)]

    def start():
        for cp in copies:
            cp.start()

    def finish():
        for cp in copies:
            cp.wait()

    return start, finish


def _chip_stages(pb, rcv, send, recv):
    x, y, c, _, others, chips = _place()
    copies = [_remote(pb[w].at[chips[kk]], rcv[w].at[kk], send.at[w, kk], recv.at[w, kk], (*others[kk], c))
              for w in range(len(pb)) for kk in range(3)]

    def start():
        for cp in copies:
            cp.start()

    def finish():
        for cp in copies:
            cp.wait()

    return start, finish


def _gather_weights(shards, conv_w):
    n = len(shards)

    def body(*refs):
        sh, cw, outs, cwo = refs[:n], refs[n], refs[n + 1:2 * n + 1], refs[2 * n + 1]
        send, recv = refs[2 * n + 2:]
        x, y, c, me, others, chips = _place()
        start, forward, finish = _gather_stages(sh, outs, send, recv)
        start()
        small = [_remote(cw, cwo.at[me], send.at[n, kk], recv.at[n, kk], (*others[kk], c)) for kk in range(3)]
        for cp in small:
            cp.start()
        forward()
        for kk in range(3):
            _remote(cw, cwo.at[chips[kk]], send.at[n, kk], recv.at[n, kk], (*others[kk], c)).wait_recv()
        finish()
        for cp in small:
            cp.wait_send()

    out_shape = [jax.ShapeDtypeStruct((N_CHIPS,) + s.shape, s.dtype) for s in shards]
    out_shape.append(jax.ShapeDtypeStruct((N_CHIPS,) + conv_w.shape, conv_w.dtype))
    got = pl.pallas_call(
        body, name="gather_weights", in_specs=[ANY] * (n + 1), out_specs=[ANY] * (n + 1), out_shape=out_shape,
        scratch_shapes=[pltpu.SemaphoreType.DMA((n + 1, 6)), pltpu.SemaphoreType.DMA((n + 1, 6))],
    )(*shards, conv_w)
    me = 2 * lax.axis_index("x") + lax.axis_index("y")
    return [lax.dynamic_update_index_in_dim(g, own, me, 0) for g, own in zip(got, list(shards) + [conv_w])]


def _taken_shape(g, kind):
    if kind == "rows":
        return (N_CHIPS,) + g.shape[2:]
    if kind == "lanes":
        return g.shape[:2] + (g.shape[2] // 2,)
    return (N_CHIPS, g.shape[1], g.shape[2] // N_CHIPS)


def _pair_sum(place, g, kind, a, name, ready=(), ready_kinds=()):
    _, half, cols = a.shape
    nw = len(ready)
    if kind == "rows":
        mine = pl.BlockSpec((1, 1, half, cols), lambda p, pr: (p, pr[0], 0, 0))
    elif kind == "lanes":
        mine = pl.BlockSpec((1, half, cols), lambda p, pr: (p, 0, pr[0]))
    else:
        mine = pl.BlockSpec((1, half, cols), lambda p, pr: (pr[0], 0, p))

    def body(place_ref, g_ref, a_ref, *rest):
        grads, (own_ref, pb_ref), taken = rest[:nw], rest[nw:nw + 2], rest[nw + 2:2 * nw + 2]
        if nw:
            pair_start, pair_finish = _pair_stages(grads, ready_kinds, taken, *rest[2 * nw + 2:])
            pl.when(pl.program_id(0) == 0)(pair_start)
        tot = (g_ref[0, 0] if kind == "rows" else g_ref[0]) + a_ref[0]
        pb_ref[0] = tot.astype(BF16)

        @pl.when(pl.program_id(0) == place_ref[1])
        def _():
            own_ref[...] = tot

        if nw:
            pl.when(pl.program_id(0) == N_CHIPS - 1)(pair_finish)

    sems = [pltpu.SemaphoreType.DMA((nw, N_CHIPS)), pltpu.SemaphoreType.DMA((nw, N_CHIPS))] if nw else []
    gs = pltpu.PrefetchScalarGridSpec(
        num_scalar_prefetch=1, grid=(N_CHIPS,),
        in_specs=[mine, pl.BlockSpec((1, half, cols), lambda p, pr: (p, 0, 0))] + [ANY] * nw,
        out_specs=[pl.BlockSpec((half, cols), lambda p, pr: (0, 0)),
                   pl.BlockSpec((1, half, cols), lambda p, pr: (p, 0, 0))] + [ANY] * nw,
        scratch_shapes=sems)
    out = pl.pallas_call(
        body, name=name, grid_spec=gs,
        out_shape=[jax.ShapeDtypeStruct((half, cols), F32), jax.ShapeDtypeStruct((N_CHIPS, half, cols), BF16)]
        + [jax.ShapeDtypeStruct(_taken_shape(r, kd), r.dtype) for r, kd in zip(ready, ready_kinds)],
        compiler_params=_params(("arbitrary",)),
    )(place, g, a, *ready)
    return list(out)


def _chip_sum(own, rcv, name):
    half, cols = own.shape

    def body(o_ref, r_ref, t_ref):
        t_ref[...] = ((o_ref[...] + r_ref[0].astype(F32)) + r_ref[1].astype(F32)) + r_ref[2].astype(F32)

    return pl.pallas_call(
        body, name=name, grid=(1,),
        in_specs=[pl.BlockSpec((half, cols), lambda i: (0, 0)), pl.BlockSpec((3, half, cols), lambda i: (0, 0, 0))],
        out_specs=pl.BlockSpec((half, cols), lambda i: (0, 0)),
        out_shape=jax.ShapeDtypeStruct((half, cols), F32), compiler_params=_params(("arbitrary",)),
    )(own, rcv)


def _small_stages(sm, smg, send, recv):
    x, y, c, _, _, _ = _place()

    def peer(r):
        return (1 - x if r & 4 else x, 1 - y if r & 2 else y, 1 - c if r & 1 else c)

    mine = 4 * x + 2 * y + c
    copies = [_remote(sm, smg.at[mine], send.at[r - 1], recv.at[r - 1], peer(r)) for r in range(1, 8)]

    def start():
        for cp in copies:
            cp.start()

    def finish():
        for r in range(1, 8):
            px, py, pc = peer(r)
            _remote(sm, smg.at[4 * px + 2 * py + pc], send.at[r - 1], recv.at[r - 1], (px, py, pc)).wait_recv()
        for cp in copies:
            cp.wait_send()

    return start, finish


def _pair_exchange(grads, kinds):
    n = len(grads)

    def body(*refs):
        start, finish = _pair_stages(refs[:n], kinds, refs[n:2 * n], *refs[2 * n:])
        start()
        finish()

    return pl.pallas_call(
        body, name="pair_exchange", in_specs=[ANY] * n, out_specs=[ANY] * n,
        out_shape=[jax.ShapeDtypeStruct(_taken_shape(g, kd), g.dtype) for g, kd in zip(grads, kinds)],
        scratch_shapes=[pltpu.SemaphoreType.DMA((n, N_CHIPS)), pltpu.SemaphoreType.DMA((n, N_CHIPS))],
    )(*grads)


def _chip_exchange(parts, small):
    n = len(parts)

    def body(*refs):
        pb, sm, rcv, smg = refs[:n], refs[n], refs[n + 1:2 * n + 1], refs[2 * n + 1]
        send, recv, ssend, srecv = refs[2 * n + 2:]
        chip_start, chip_finish = _chip_stages(pb, rcv, send, recv)
        small_start, small_finish = _small_stages(sm, smg, ssend, srecv)
        chip_start()
        small_start()
        chip_finish()
        small_finish()

    out_shape = [jax.ShapeDtypeStruct((3,) + p.shape[1:], p.dtype) for p in parts]
    out_shape.append(jax.ShapeDtypeStruct((8,) + small.shape, small.dtype))
    *arrived, small_land = pl.pallas_call(
        body, name="chip_exchange", in_specs=[ANY] * (n + 1), out_specs=[ANY] * (n + 1), out_shape=out_shape,
        scratch_shapes=[pltpu.SemaphoreType.DMA((n, 3)), pltpu.SemaphoreType.DMA((n, 3)),
                        pltpu.SemaphoreType.DMA((7,)), pltpu.SemaphoreType.DMA((7,))],
    )(*parts, small)
    mine = 4 * lax.axis_index("x") + 2 * lax.axis_index("y") + lax.axis_index("c")
    return arrived, lax.dynamic_update_index_in_dim(small_land, small, mine, 0)


def _share_stages(t, g, send, recv):
    x, y, c, _, _, _ = _place()
    copies = [_remote(t[w], g[w], send.at[w], recv.at[w], (x, y, 1 - c)) for w in range(len(t))]

    def start():
        for cp in copies:
            cp.start()

    def finish():
        for cp in copies:
            cp.wait()

    return start, finish


def _pair_share(totals, name):
    n = len(totals)

    def body(*refs):
        start, finish = _share_stages(refs[:n], refs[n:2 * n], *refs[2 * n:])
        start()
        finish()

    return pl.pallas_call(
        body, name=name, in_specs=[ANY] * n, out_specs=[ANY] * n,
        out_shape=[jax.ShapeDtypeStruct(t.shape, t.dtype) for t in totals],
        scratch_shapes=[pltpu.SemaphoreType.DMA((n,)), pltpu.SemaphoreType.DMA((n,))],
    )(*totals)


def _adamw_math(w, g, m, v):
    m = ADAM_B1 * m + (1.0 - ADAM_B1) * g
    v = ADAM_B2 * v + (1.0 - ADAM_B2) * (g * g)
    m_hat = m / (1.0 - ADAM_B1 ** ADAM_STEP)
    v_hat = v / (1.0 - ADAM_B2 ** ADAM_STEP)
    delta = -ADAM_LR * (m_hat / (jnp.sqrt(v_hat) + ADAM_EPS) + ADAM_WD * w)
    return delta, m, v


def _adamw(c_idx, w, mine, theirs, m, v, nb, name):
    rows, cols = w.shape
    tr = rows // (2 * nb)

    def body(c_ref, w_ref, a_ref, b_ref, m_ref, v_ref, g_ref, d_ref, nm_ref, nv_ref):
        g = jnp.where(pl.program_id(0) == c_ref[0], a_ref[...], b_ref[...])
        g_ref[...] = g
        d_ref[...], nm_ref[...], nv_ref[...] = _adamw_math(w_ref[...], g, m_ref[...], v_ref[...])

    full = pl.BlockSpec((tr, cols), lambda hh, i, cr: (hh * nb + i, 0))
    half = pl.BlockSpec((tr, cols), lambda hh, i, cr: (i, 0))
    gs = pltpu.PrefetchScalarGridSpec(num_scalar_prefetch=1, grid=(2, nb), in_specs=[full, half, half, full, full],
                                      out_specs=[full] * 4)
    return pl.pallas_call(
        body, name=name, grid_spec=gs, out_shape=[jax.ShapeDtypeStruct((rows, cols), F32)] * 4,
        compiler_params=_params(("arbitrary", "arbitrary")),
    )(c_idx, w, mine, theirs, m, v)


def _adamw_lanes(c_idx, w, mine, theirs, m, v, name):
    rows, _, cols = w.shape
    hc = cols // 2

    def body(c_ref, w_ref, a_ref, b_ref, m_ref, v_ref, g_ref, d_ref, nm_ref, nv_ref):
        g = jnp.where(pl.program_id(0) == c_ref[0], a_ref[...], b_ref[...])
        g_ref[:, 0, :] = g
        d_ref[:, 0, :], nm_ref[:, 0, :], nv_ref[:, 0, :] = _adamw_math(w_ref[:, 0, :], g, m_ref[:, 0, :], v_ref[:, 0, :])

    full = pl.BlockSpec((rows, 1, hc), lambda hh, cr: (0, 0, hh))
    half = pl.BlockSpec((rows, hc), lambda hh, cr: (0, 0))
    gs = pltpu.PrefetchScalarGridSpec(num_scalar_prefetch=1, grid=(2,), in_specs=[full, half, half, full, full],
                                      out_specs=[full] * 4)
    return pl.pallas_call(
        body, name=name, grid_spec=gs, out_shape=[jax.ShapeDtypeStruct((rows, 1, cols), F32)] * 4,
        compiler_params=_params(("arbitrary",)),
    )(c_idx, w, mine, theirs, m, v)


SMALL = ("g_mix_pre", "g_mix_post", "g_ffn_pre", "g_ffn_post")
SMALL_ALL = SMALL + ("g_attn_out", "g_conv_out", "conv_w", "b_forget")
SMALL_AT = {"g_mix_pre": (0, 0, 1024), "g_mix_post": (1, 0, 1024), "g_ffn_pre": (2, 0, 1024),
            "g_ffn_post": (3, 0, 1024), "g_attn_out": (4, 0, 512), "g_conv_out": (4, 512, 512),
            "b_forget": (7, 0, N_HEADS)}
CONV_AT = ((5, 0), (5, 512), (6, 0))
LOSS_AT = (6, 512)


def _pack_small(t, conv_full, loss_sum):
    conv = jnp.concatenate([conv_full.reshape(1, 3 * CONV_W), loss_sum.reshape(1, 1),
                            jnp.zeros((1, 2048 - 3 * CONV_W - 1), F32)], axis=1).reshape(2, 1024)
    return jnp.concatenate([t[n].reshape(1, 1024) for n in SMALL]
                           + [jnp.concatenate([t["g_attn_out"].reshape(1, 512), t["g_conv_out"].reshape(1, 512)], axis=1),
                              conv, jnp.pad(t["b_forget"].reshape(1, N_HEADS), ((0, 0), (0, 1024 - N_HEADS)))], axis=0)


def _small_update(me_idx, gathered, w, m, v):
    def body(me_ref, gg_ref, *refs):
        k = len(SMALL_ALL)
        w_refs, m_refs, v_refs = refs[:k], refs[k:2 * k], refs[2 * k:3 * k]
        loss_ref = refs[3 * k]
        outs = refs[3 * k + 1:3 * k + 1 + 4 * k]
        sums = refs[-1]
        g = gg_ref[0]
        for dev in range(1, 8):
            g = g + gg_ref[dev]
        sums[...] = g
        loss_ref[...] = sums[LOSS_AT[0]:LOSS_AT[0] + 1, LOSS_AT[1]:LOSS_AT[1] + 1]
        mine = pl.multiple_of(me_ref[0] * 128, 128)
        for idx, name in enumerate(SMALL_ALL):
            g_ref, d_ref, nm_ref, nv_ref = outs[4 * idx:4 * idx + 4]
            if name == "conv_w":
                for r, (row, lo) in enumerate(CONV_AT):
                    gr = sums[row:row + 1, pl.ds(lo + mine, 128)]
                    g_ref[0, r:r + 1, :] = gr
                    d_ref[0, r:r + 1, :], nm_ref[0, r:r + 1, :], nv_ref[0, r:r + 1, :] = _adamw_math(
                        w_refs[idx][0, r:r + 1, :], gr, m_refs[idx][0, r:r + 1, :], v_refs[idx][0, r:r + 1, :])
            else:
                row, lo, n = SMALL_AT[name]
                gr = sums[row:row + 1, lo:lo + n]
                g_ref[...] = gr
                d_ref[...], nm_ref[...], nv_ref[...] = _adamw_math(w_refs[idx][...], gr, m_refs[idx][...],
                                                                    v_refs[idx][...])

    def whole(a):
        nd = a.ndim
        return pl.BlockSpec(a.shape, lambda i, mr: (0,) * nd)

    ins = [t[n] for t in (w, m, v) for n in SMALL_ALL]
    out_shape = [jax.ShapeDtypeStruct((1, 1), F32)]
    for n in SMALL_ALL:
        out_shape += [jax.ShapeDtypeStruct(w[n].shape, F32)] * 4
    gs = pltpu.PrefetchScalarGridSpec(
        num_scalar_prefetch=1, grid=(1,), in_specs=[whole(gathered)] + [whole(a) for a in ins],
        out_specs=[whole(o) for o in out_shape], scratch_shapes=[pltpu.VMEM((8, 1024), F32)])
    out = pl.pallas_call(body, name="small_update", grid_spec=gs, out_shape=out_shape,
                         compiler_params=_params(("arbitrary",)))(me_idx, gathered, *ins)
    return out[0], {n: out[1 + 4 * i:5 + 4 * i] for i, n in enumerate(SMALL_ALL)}


def kernel(x, w_in, b_forget, conv_w, g_attn_out, g_conv_out, w_out, g_mix_pre, g_mix_post, w_gate_up, w_down, g_ffn_pre, g_ffn_post, loss_target, m_w_in, m_b_forget, m_conv_w, m_g_attn_out, m_g_conv_out, m_w_out, m_g_mix_pre, m_g_mix_post, m_w_gate_up, m_w_down, m_g_ffn_pre, m_g_ffn_post, v_w_in, v_b_forget, v_conv_w, v_g_attn_out, v_g_conv_out, v_w_out, v_g_mix_pre, v_g_mix_post, v_w_gate_up, v_w_down, v_g_ffn_pre, v_g_ffn_post):
    w = dict(w_in=w_in, b_forget=b_forget, conv_w=conv_w, g_attn_out=g_attn_out, g_conv_out=g_conv_out, w_out=w_out,
             g_mix_pre=g_mix_pre, g_mix_post=g_mix_post, w_gate_up=w_gate_up, w_down=w_down, g_ffn_pre=g_ffn_pre,
             g_ffn_post=g_ffn_post)
    m = dict(w_in=m_w_in, b_forget=m_b_forget, conv_w=m_conv_w, g_attn_out=m_g_attn_out, g_conv_out=m_g_conv_out,
             w_out=m_w_out, g_mix_pre=m_g_mix_pre, g_mix_post=m_g_mix_post, w_gate_up=m_w_gate_up, w_down=m_w_down,
             g_ffn_pre=m_g_ffn_pre, g_ffn_post=m_g_ffn_post)
    v = dict(w_in=v_w_in, b_forget=v_b_forget, conv_w=v_conv_w, g_attn_out=v_g_attn_out, g_conv_out=v_g_conv_out,
             w_out=v_w_out, g_mix_pre=v_g_mix_pre, g_mix_post=v_g_mix_post, w_gate_up=v_w_gate_up, w_down=v_w_down,
             g_ffn_pre=v_g_ffn_pre, g_ffn_post=v_g_ffn_post)
    cx, cy, cc = lax.axis_index("x"), lax.axis_index("y"), lax.axis_index("c")
    me = 2 * cx + cy
    c_idx = cc.astype(jnp.int32).reshape(1)
    me_idx = me.astype(jnp.int32).reshape(1)

    stored = lambda a: jnp.transpose(a, (2, 0, 1))
    grad_x, big, small_all = _device_step(x[0], loss_target[0], w, m, v, stored(w_in), stored(m_w_in),
                                          stored(v_w_in), c_idx, me_idx)
    gsum, delta, new_m, new_v = {}, {}, {}, {}
    for n in BIG:
        back = (lambda r: jnp.transpose(r, (1, 2, 0))) if n == "w_in" else (lambda r: r[None])
        gsum[n], delta[n], new_m[n], new_v[n] = [back(r) for r in big[n]]
    loss_sum, small_new = _small_update(me_idx, small_all, w, m, v)
    for n in SMALL_ALL:
        gsum[n], delta[n], new_m[n], new_v[n] = small_new[n]
    loss = 0.5 * loss_sum[0, 0]

    order = ("w_in", "b_forget", "conv_w", "g_attn_out", "g_conv_out", "w_out", "g_mix_pre", "g_mix_post",
             "w_gate_up", "w_down", "g_ffn_pre", "g_ffn_post")
    return (loss, grad_x[None], *[gsum[n] for n in order], *[delta[n] for n in order],
            *[new_m[n] for n in order], *[new_v[n] for n in order])
```

```python
import jax
import jax.numpy as jnp
import numpy as np
from jax import lax
from jax.experimental import pallas as pl
from jax.experimental.pallas import tpu as pltpu

F32 = jnp.float32
BF16 = jnp.bfloat16
MXU_DTYPE = jnp.bfloat16

D_MODEL = 1024
HEAD_DIM = 64
N_HEADS = 8
ATTN_W = 512
CONV_W = 512
D_FF = 2816
FF_PIECE = 1408
EPS = 1e-6
Q_SCALE = HEAD_DIM ** -0.5

OFF_F = 1536
OFF_B = 1664
OFF_C = 2176
OFF_U = 2688
IN_PAD = 3200
IN_W = 3080
N_CHIPS = 4

ADAM_LR = 0.001
ADAM_B1 = 0.9
ADAM_B2 = 0.999
ADAM_EPS = 1e-08
ADAM_WD = 0.01
ADAM_STEP = 10

VMEM_LIMIT_V7X = 56 * 1024 * 1024
MESH_ID = pl.DeviceIdType.MESH


def _params(sem=None, vmem=VMEM_LIMIT_V7X):
    kw = {"vmem_limit_bytes": vmem}
    if sem is not None:
        kw["dimension_semantics"] = sem
    return pltpu.CompilerParams(**kw)


def _dot(a, b):
    return jnp.dot(a, b, preferred_element_type=F32)


def _dot_nt(a, b):
    return lax.dot_general(a, b, (((1,), (1,)), ((), ())), preferred_element_type=F32)


def _dot_exact(x, ones, parts):
    if ones.dtype == F32:
        return _dot(x, ones)
    acc = None
    rem = x
    for _ in range(parts):
        piece = rem.astype(BF16)
        rem = rem - piece.astype(F32)
        term = _dot(piece, ones)
        acc = term if acc is None else acc + term
    return acc


def _rms(v):
    return lax.rsqrt(jnp.mean(v * v, axis=-1, keepdims=True) + EPS)


def _tok(tm, w):
    return pl.BlockSpec((tm, w), lambda i: (i, 0))


def _whole(shape, single=False):
    nd = len(shape)
    if single:
        return pl.BlockSpec(shape, lambda i: (0,) * nd, pipeline_mode=pl.Buffered(1))
    return pl.BlockSpec(shape, lambda i: (0,) * nd)


def _feat(rows, tm):
    return pl.BlockSpec((rows, tm), lambda i: (0, i))


def _inproj_fwd(x, g_pre, w_t, tm):
    s = x.shape[0]

    def body(x_ref, g_ref, w_ref, h_ref, q_ref, k_ref, v_ref, kt_ref, vt_ref, zt_ref, b_ref, c_ref, u_ref):
        xv = x_ref[...]
        h = ((xv * _rms(xv)) * g_ref[...]).astype(MXU_DTYPE)
        h_ref[...] = h

        def proj(lo, hi):
            return _dot_nt(h, w_ref[lo:hi, :])

        q_ref[...] = (proj(0, 512) * Q_SCALE).astype(MXU_DTYPE)
        kt = _dot_nt(w_ref[512:1024, :], h)
        vt = _dot_nt(w_ref[1024:OFF_F, :], h)
        kt_ref[...] = kt.astype(MXU_DTYPE)
        vt_ref[...] = vt.astype(MXU_DTYPE)
        k_ref[...] = kt.T.astype(MXU_DTYPE)
        v_ref[...] = vt.T.astype(MXU_DTYPE)
        zt_ref[...] = _dot_nt(w_ref[OFF_F:OFF_B, :], h)
        b_ref[...] = proj(OFF_B, OFF_C).astype(MXU_DTYPE)
        c_ref[...] = proj(OFF_C, OFF_U).astype(MXU_DTYPE)
        u_ref[...] = proj(OFF_U, IN_PAD).astype(MXU_DTYPE)

    sd = jax.ShapeDtypeStruct
    return pl.pallas_call(
        body, name="inproj_fwd", grid=(s // tm,),
        in_specs=[_tok(tm, D_MODEL), _whole((1, D_MODEL)), _whole((IN_PAD, D_MODEL), single=True)],
        out_specs=[_tok(tm, D_MODEL), _tok(tm, 512), _tok(tm, 512), _tok(tm, 512), _feat(512, tm), _feat(512, tm),
                   _feat(128, tm), _tok(tm, 512), _tok(tm, 512), _tok(tm, 512)],
        out_shape=[sd((s, D_MODEL), MXU_DTYPE), sd((s, 512), MXU_DTYPE), sd((s, 512), MXU_DTYPE),
                   sd((s, 512), MXU_DTYPE), sd((512, s), MXU_DTYPE), sd((512, s), MXU_DTYPE), sd((128, s), F32),
                   sd((s, 512), MXU_DTYPE), sd((s, 512), MXU_DTYPE), sd((s, 512), MXU_DTYPE)],
        compiler_params=_params(("arbitrary",)),
    )(x, g_pre, w_t)


def _tri(n, upper):
    r = lax.broadcasted_iota(jnp.int32, (n, n), 0)
    c = lax.broadcasted_iota(jnp.int32, (n, n), 1)
    return ((r <= c) if upper else (r >= c)).astype(MXU_DTYPE)


HEAD_ROWS = 16


def _rows_to_cols(v):
    return jnp.concatenate([v, jnp.zeros((128 - HEAD_ROWS, 128), F32)], axis=0).T


BIAS_PARTS = 3


def _bias_placement():
    place_q = np.zeros((BIAS_PARTS, 128, ATTN_W), np.float32)
    place_k = np.zeros((BIAS_PARTS, 128, ATTN_W), np.float32)
    ones_q = np.zeros((1, ATTN_W), np.float32)
    ones_k = np.zeros((1, ATTN_W), np.float32)
    for h in range(N_HEADS):
        base = 2 * HEAD_DIM * (h // 2) + HEAD_DIM * (1 - h % 2)
        for part in range(BIAS_PARTS):
            place_q[part, h, base + part] = 1.0
            place_k[part, h, base + BIAS_PARTS + part] = -1.0
        ones_q[0, base + BIAS_PARTS:base + 2 * BIAS_PARTS] = 1.0
        ones_k[0, base:base + BIAS_PARTS] = 1.0
    return (jnp.asarray(place_q, MXU_DTYPE), jnp.asarray(place_k, MXU_DTYPE), jnp.asarray(ones_q), jnp.asarray(ones_k))


def _forget_fwd(z_t, b_col):
    s = z_t.shape[1]
    nb = s // 128

    def body(z_ref, b_ref, pq_ref, pk_ref, oq_ref, ok_ref, qa_ref, ka_ref, cc_ref):
        upper = _tri(128, True)

        carry = jnp.zeros((HEAD_ROWS, 1), F32)
        for n in range(nb):
            off = n * 128
            lf = jax.nn.log_sigmoid(z_ref[0:HEAD_ROWS, off:off + 128] + b_ref[...])
            cc_ref[off:off + 128, :] = _rows_to_cols(_dot_exact(lf, upper, 3) + carry)
            carry = carry + jnp.sum(lf, axis=1, keepdims=True)

        rb = min(s, 512)
        for off in range(0, s, rb):
            qa = jnp.broadcast_to(oq_ref[...], (rb, ATTN_W))
            ka = jnp.broadcast_to(ok_ref[...], (rb, ATTN_W))
            rem = cc_ref[off:off + rb, :]
            for part in range(BIAS_PARTS):
                piece = rem.astype(MXU_DTYPE)
                rem = rem - piece.astype(F32)
                qa = qa + _dot(piece, pq_ref[part])
                ka = ka + _dot(piece, pk_ref[part])
            qa_ref[off:off + rb, :] = qa.astype(MXU_DTYPE)
            ka_ref[off:off + rb, :] = ka.astype(MXU_DTYPE)

    sd = jax.ShapeDtypeStruct
    return pl.pallas_call(body, name="forget_fwd",
                          out_shape=[sd((s, ATTN_W), MXU_DTYPE), sd((s, ATTN_W), MXU_DTYPE)],
                          scratch_shapes=[pltpu.VMEM((s, 128), F32)],
                          compiler_params=_params())(z_t, b_col, *_bias_placement())


def _aligned(start, size):
    return pl.ds(start if isinstance(start, int) else pl.multiple_of(start, size), size)


def _pair_lanes(pp):
    return _aligned(pp * 2 * HEAD_DIM, 2 * HEAD_DIM)


def _head_rows(h):
    return _aligned(h * HEAD_DIM, HEAD_DIM)


def _only_head(block, hb):
    lane = lax.broadcasted_iota(jnp.int32, block.shape, 1)
    return jnp.where((lane >= HEAD_DIM) if hb else (lane < HEAD_DIM), block, jnp.zeros_like(block))


def _other_head(block, other, hb):
    lane = lax.broadcasted_iota(jnp.int32, block.shape, 1)
    return jnp.where((lane >= HEAD_DIM) if hb else (lane < HEAD_DIM), block, other)


def _attn_fwd(qs, k, v_t, q_bias, k_bias, t, shards):
    s = qs.shape[0]
    n = s // t
    pairs = [(i, j) for i in range(n) for j in range(i + 1)]
    it = jnp.asarray(np.array([p[0] for p in pairs], np.int32))
    jt = jnp.asarray(np.array([p[1] for p in pairs], np.int32))
    nw = len(shards)
    last = len(pairs) - 1
    mid = (2 * len(pairs)) // 3

    def body(it_ref, jt_ref, q_ref, k_ref, vt_ref, qb_ref, kb_ref, *rest):
        sh, (o_ref, lse_ref, p_ref, pm_ref), got = rest[:nw], rest[nw:nw + 4], rest[nw + 4:2 * nw + 4]
        m_sc, l_sc, acc_sc, send, recv = rest[2 * nw + 4:]
        p = pl.program_id(0)
        i = it_ref[p]
        j = jt_ref[p]
        gather_start, gather_forward, gather_finish = _gather_stages(sh, got, send, recv)
        pl.when(p == 0)(gather_start)
        if mid < last:
            pl.when(p == mid)(gather_forward)

        @pl.when(j == 0)
        def _():
            m_sc[...] = jnp.full_like(m_sc, -1e30)
            l_sc[...] = jnp.ones_like(l_sc)
            acc_sc[...] = jnp.zeros_like(acc_sc)

        pm_ref[...] = jnp.zeros_like(pm_ref)

        def pair_step(pp, diagonal):
            lanes = _pair_lanes(pp)
            kp = k_ref[:, lanes]
            qp = q_ref[:, lanes]
            kb = kb_ref[:, lanes]
            qb = qb_ref[:, lanes]
            for hb in range(2):
                h = 2 * pp + hb
                row = pl.ds(h, 1)
                rows = _head_rows(h)
                st = _dot_nt(_other_head(kp, kb, hb), _other_head(qp, qb, hb))
                if diagonal:
                    kpos = lax.broadcasted_iota(jnp.int32, (t, t), 0)
                    qpos = lax.broadcasted_iota(jnp.int32, (t, t), 1)
                    st = jnp.where(kpos <= qpos, st, -1e30)
                m_prev = m_sc[row, :]
                m_new = jnp.maximum(m_prev, jnp.max(st, axis=0, keepdims=True))
                alpha = jnp.exp(m_prev - m_new)
                pt = jnp.exp(st - m_new)
                l_sc[row, :] = alpha * l_sc[row, :] + jnp.sum(pt, axis=0, keepdims=True)
                ptb = pt.astype(MXU_DTYPE)
                acc_sc[rows, :] = acc_sc[rows, :] * alpha + _dot(vt_ref[rows, :], ptb)
                m_sc[row, :] = m_new
                p_ref[0, h] = ptb
                pm_ref[0, row, :] = m_new

        @pl.when(j < i)
        def _():
            for pp in range(N_HEADS // 2):
                pair_step(pp, False)

        @pl.when(j == i)
        def _():
            for pp in range(N_HEADS // 2):
                pair_step(pp, True)
                sub = lax.broadcasted_iota(jnp.int32, (2 * HEAD_DIM, t), 0)
                l_pair = jnp.where(sub < HEAD_DIM, l_sc[pl.ds(2 * pp, 1), :], l_sc[pl.ds(2 * pp + 1, 1), :])
                o_t = acc_sc[_aligned(pp * 2 * HEAD_DIM, 2 * HEAD_DIM), :] / l_pair
                o_ref[:, _pair_lanes(pp)] = o_t.T.astype(MXU_DTYPE)

            lse_ref[...] = m_sc[...] + jnp.log(l_sc[...])

        @pl.when(p == last)
        def _():
            if mid >= last:
                gather_forward()
            gather_finish()

    gs = pltpu.PrefetchScalarGridSpec(
        num_scalar_prefetch=2, grid=(len(pairs),),
        in_specs=[pl.BlockSpec((t, ATTN_W), lambda p, it_, jt_: (it_[p], 0)),
                  pl.BlockSpec((t, ATTN_W), lambda p, it_, jt_: (jt_[p], 0)),
                  pl.BlockSpec((ATTN_W, t), lambda p, it_, jt_: (0, jt_[p])),
                  pl.BlockSpec((t, ATTN_W), lambda p, it_, jt_: (it_[p], 0)),
                  pl.BlockSpec((t, ATTN_W), lambda p, it_, jt_: (jt_[p], 0))] + [ANY] * nw,
        out_specs=[pl.BlockSpec((t, ATTN_W), lambda p, it_, jt_: (it_[p], 0)),
                   pl.BlockSpec((HEAD_ROWS, t), lambda p, it_, jt_: (0, it_[p])),
                   pl.BlockSpec((1, N_HEADS, t, t), lambda p, it_, jt_: (p, 0, 0, 0)),
                   pl.BlockSpec((1, HEAD_ROWS, t), lambda p, it_, jt_: (p, 0, 0))] + [ANY] * nw,
        scratch_shapes=[pltpu.VMEM((HEAD_ROWS, t), F32), pltpu.VMEM((HEAD_ROWS, t), F32), pltpu.VMEM((ATTN_W, t), F32),
                        pltpu.SemaphoreType.DMA((nw, 6)), pltpu.SemaphoreType.DMA((nw, 6))])
    sd = jax.ShapeDtypeStruct
    o, lse, probs, probs_max, *got = pl.pallas_call(
        body, name="attn_fwd", grid_spec=gs,
        out_shape=[sd((s, ATTN_W), MXU_DTYPE), sd((HEAD_ROWS, s), F32), sd((len(pairs), N_HEADS, t, t), MXU_DTYPE),
                   sd((len(pairs), HEAD_ROWS, t), F32)]
        + [sd((N_CHIPS,) + a.shape, a.dtype) for a in shards],
        compiler_params=_params(("arbitrary",)),
    )(it, jt, qs, k, v_t, q_bias, k_bias, *shards)
    me = 2 * lax.axis_index("x") + lax.axis_index("y")
    return o, lse, probs, probs_max, [lax.dynamic_update_index_in_dim(g, own, me, 0) for g, own in zip(got, shards)]


HALO = 16


def _halo_before(tm):
    return pl.BlockSpec((HALO, CONV_W), lambda i: (jnp.maximum(i * (tm // HALO) - 1, 0), 0))


def _shift_down(cur, prev_ref, first, cols=slice(None)):
    row = lax.broadcasted_iota(jnp.int32, cur.shape, 0)

    def before(r):
        prod = prev_ref[0][r:r + 1, cols].astype(F32) * prev_ref[1][r:r + 1, cols].astype(F32)
        return jnp.where(first, 0.0, prod)

    p7, p6 = before(HALO - 1), before(HALO - 2)
    s1 = jnp.where(row == 0, p7, pltpu.roll(cur, 1, 0))
    s2 = jnp.where(row == 0, p6, jnp.where(row == 1, p7, pltpu.roll(cur, 2, 0)))
    return s1, s2


def _group_ms(v, gmat):
    return _dot_exact(v, gmat, 1) * (1.0 / HEAD_DIM)


def _mixer_fwd(x, o_attn, gate_b, gate_c, u, conv_w, g_attn, g_conv, w_out, g_post, gmat, tm):
    s = x.shape[0]

    def body(x_ref, o_ref, b_ref, c_ref, u_ref, cp_ref, up_ref, cw_ref, ga_ref, gc_ref, wo_ref, gp_ref, gm_ref,
             x2_ref, mg_ref, y_ref, z_ref):
        i = pl.program_id(0)
        gm = gm_ref[0:128, 0:128]
        for lo in range(0, ATTN_W, 128):
            cols = slice(lo, lo + 128)
            cu = c_ref[:, cols].astype(F32) * u_ref[:, cols].astype(F32)
            cu1, cu2 = _shift_down(cu, (cp_ref, up_ref), i == 0, cols)
            z = cw_ref[0:1, cols] * cu2 + cw_ref[1:2, cols] * cu1 + cw_ref[2:3, cols] * cu
            z_ref[:, cols] = z.astype(MXU_DTYPE)
            cv = b_ref[:, cols].astype(F32) * z
            ov = o_ref[:, cols].astype(F32)
            mg_ref[:, cols] = ((ov * lax.rsqrt(_group_ms(ov * ov, gm) + EPS)) * ga_ref[:, cols]).astype(MXU_DTYPE)
            mg_ref[:, ATTN_W + lo:ATTN_W + lo + 128] = (
                (cv * lax.rsqrt(_group_ms(cv * cv, gm) + EPS)) * gc_ref[:, cols]).astype(MXU_DTYPE)
        y = _dot(mg_ref[...], wo_ref[...])
        y_ref[...] = y
        x2_ref[...] = x_ref[...] + (y * _rms(y)) * gp_ref[...]

    halo = _halo_before(tm)
    sd = jax.ShapeDtypeStruct
    return pl.pallas_call(
        body, name="mixer_fwd", grid=(s // tm,),
        in_specs=[_tok(tm, D_MODEL), _tok(tm, 512), _tok(tm, 512), _tok(tm, 512), _tok(tm, 512), halo, halo,
                  _whole((3, 512)), _whole((1, 512)), _whole((1, 512)), _whole((D_MODEL, D_MODEL), single=True),
                  _whole((1, D_MODEL)), _whole((512, 512))],
        out_specs=[_tok(tm, D_MODEL), _tok(tm, D_MODEL), _tok(tm, D_MODEL), _tok(tm, 512)],
        out_shape=[sd((s, D_MODEL), F32), sd((s, D_MODEL), MXU_DTYPE), sd((s, D_MODEL), F32), sd((s, 512), MXU_DTYPE)],
        compiler_params=_params(("arbitrary",)),
    )(x, o_attn, gate_b, gate_c, u, gate_c, u, conv_w, g_attn, g_conv, w_out, g_post, gmat)


def _ffn_fwd(x2, target, g_pre, w_gu, w_dn, g_post, tm):
    s = x2.shape[0]

    def body(x_ref, t_ref, gpre_ref, wgu_ref, wdn_ref, gpost_ref,
             h_ref, g_ref, up_ref, a_ref, ff_ref, dout_ref, loss_ref):
        xv = x_ref[...]
        h = ((xv * _rms(xv)) * gpre_ref[...]).astype(MXU_DTYPE)
        h_ref[...] = h
        ff = jnp.zeros((tm, D_MODEL), F32)
        for j in range(2):
            cols = slice(j * FF_PIECE, (j + 1) * FF_PIECE)
            g = _dot(h, wgu_ref[j])
            up = _dot(h, wgu_ref[2 + j])
            a = ((g * jax.nn.sigmoid(g)) * up).astype(MXU_DTYPE)
            g_ref[:, cols] = g.astype(MXU_DTYPE)
            up_ref[:, cols] = up.astype(MXU_DTYPE)
            a_ref[:, cols] = a
            ff = ff + _dot(a, wdn_ref[j])
        ff_ref[...] = ff
        err = (xv + (ff * _rms(ff)) * gpost_ref[...]) - t_ref[...]
        dout_ref[...] = err * (1.0 / D_MODEL)
        part = jnp.sum(jnp.mean(err * err, axis=-1, keepdims=True), axis=0, keepdims=True)

        @pl.when(pl.program_id(0) == 0)
        def _():
            loss_ref[...] = jnp.zeros_like(loss_ref)

        loss_ref[...] += part

    sd = jax.ShapeDtypeStruct
    return pl.pallas_call(
        body, name="ffn_fwd", grid=(s // tm,),
        in_specs=[_tok(tm, D_MODEL), _tok(tm, D_MODEL), _whole((1, D_MODEL)),
                  _whole((4, D_MODEL, FF_PIECE), single=True), _whole((2, FF_PIECE, D_MODEL), single=True),
                  _whole((1, D_MODEL))],
        out_specs=[_tok(tm, D_MODEL), _tok(tm, D_FF), _tok(tm, D_FF), _tok(tm, D_FF), _tok(tm, D_MODEL),
                   _tok(tm, D_MODEL), _whole((8, 128))],
        out_shape=[sd((s, D_MODEL), MXU_DTYPE), sd((s, D_FF), MXU_DTYPE), sd((s, D_FF), MXU_DTYPE),
                   sd((s, D_FF), MXU_DTYPE), sd((s, D_MODEL), F32), sd((s, D_MODEL), F32), sd((8, 128), F32)],
        compiler_params=_params(("arbitrary",)),
    )(x2, target, g_pre, w_gu, w_dn, g_post)


def _norm_bwd(dy, normed, rinv, gain):
    t = dy * gain
    return rinv * (t - normed * jnp.mean(t * normed, axis=-1, keepdims=True))


def _acc_rows(ref, first, val):
    @pl.when(first)
    def _():
        ref[...] = jnp.zeros_like(ref)

    ref[...] += jnp.sum(val, axis=0, keepdims=True)


def _ffn_bwd(dout, ff, x2, g, up, g_post, g_pre, w_gu, w_dn, tm):
    s = x2.shape[0]

    def body(do_ref, ff_ref, x_ref, g_ref, up_ref, gpost_ref, gpre_ref, wgu_ref, wdn_ref,
             dx_ref, dff_ref, dgu_ref, dgpost_ref, dgpre_ref):
        first = pl.program_id(0) == 0
        ffv = ff_ref[...]
        rf = _rms(ffv)
        n = ffv * rf
        do = do_ref[...]
        _acc_rows(dgpost_ref, first, do * n)
        dff = _norm_bwd(do, n, rf, gpost_ref[...]).astype(MXU_DTYPE)
        dff_ref[...] = dff
        dh = jnp.zeros((tm, D_MODEL), F32)
        for j in range(2):
            cols = slice(j * FF_PIECE, (j + 1) * FF_PIECE)
            da = _dot_nt(dff, wdn_ref[j])
            gv = g_ref[:, cols].astype(F32)
            sg = jax.nn.sigmoid(gv)
            dg = (da * up_ref[:, cols].astype(F32) * (sg * (1.0 + gv * (1.0 - sg)))).astype(MXU_DTYPE)
            du = (da * (gv * sg)).astype(MXU_DTYPE)
            dgu_ref[:, cols] = dg
            dgu_ref[:, D_FF + j * FF_PIECE:D_FF + (j + 1) * FF_PIECE] = du
            dh = dh + _dot_nt(dg, wgu_ref[j]) + _dot_nt(du, wgu_ref[2 + j])
        xv = x_ref[...]
        r2 = _rms(xv)
        nx = xv * r2
        _acc_rows(dgpre_ref, first, dh * nx)
        dx_ref[...] = do + _norm_bwd(dh, nx, r2, gpre_ref[...])

    sd = jax.ShapeDtypeStruct
    return pl.pallas_call(
        body, name="ffn_bwd", grid=(s // tm,),
        in_specs=[_tok(tm, D_MODEL), _tok(tm, D_MODEL), _tok(tm, D_MODEL), _tok(tm, D_FF), _tok(tm, D_FF),
                  _whole((1, D_MODEL)), _whole((1, D_MODEL)),
                  _whole((4, D_MODEL, FF_PIECE), single=True), _whole((2, FF_PIECE, D_MODEL), single=True)],
        out_specs=[_tok(tm, D_MODEL), _tok(tm, D_MODEL), _tok(tm, 2 * D_FF), _whole((1, D_MODEL)),
                   _whole((1, D_MODEL))],
        out_shape=[sd((s, D_MODEL), F32), sd((s, D_MODEL), MXU_DTYPE), sd((s, 2 * D_FF), MXU_DTYPE),
                   sd((1, D_MODEL), F32), sd((1, D_MODEL), F32)],
        compiler_params=_params(("arbitrary",)),
    )(dout, ff, x2, g, up, g_post, g_pre, w_gu, w_dn)


def _tn_matmul(a, b, tm, tn, tk, name, totals=()):
    s, m = a.shape
    n = b.shape[1]
    nw = len(totals)
    grid = (m // tm, n // tn, s // tk)

    def body(a_ref, b_ref, *rest):
        o_ref = rest[nw]
        if nw:
            step = (pl.program_id(0) * grid[1] + pl.program_id(1)) * grid[2] + pl.program_id(2)
            share_start, share_finish = _share_stages(rest[:nw], rest[nw + 1:2 * nw + 1], *rest[2 * nw + 1:])
            pl.when(step == 0)(share_start)

        @pl.when(pl.program_id(2) == 0)
        def _():
            o_ref[...] = jnp.zeros_like(o_ref)

        o_ref[...] += lax.dot_general(a_ref[...], b_ref[...], (((0,), (0,)), ((), ())), preferred_element_type=F32)
        if nw:
            pl.when(step == grid[0] * grid[1] * grid[2] - 1)(share_finish)

    out = pl.pallas_call(
        body, name=name, grid=grid,
        in_specs=[pl.BlockSpec((tk, tm), lambda i, j, kk: (kk, i)), pl.BlockSpec((tk, tn), lambda i, j, kk: (kk, j))]
        + [ANY] * nw,
        out_specs=[pl.BlockSpec((tm, tn), lambda i, j, kk: (i, j))] + [ANY] * nw,
        out_shape=[jax.ShapeDtypeStruct((m, n), F32)] + [jax.ShapeDtypeStruct(t.shape, t.dtype) for t in totals],
        scratch_shapes=[pltpu.SemaphoreType.DMA((nw,)), pltpu.SemaphoreType.DMA((nw,))] if nw else [],
        compiler_params=_params(("arbitrary", "arbitrary", "arbitrary")),
    )(a, b, *totals)
    return (out[0], out[1:]) if nw else out[0]


def _mixer_bwd(dx2, y, o_attn, gate_b, z, g_post, g_attn, g_conv, w_out, gmat, sel, tm, ready, kinds):
    s = dx2.shape[0]
    nw = len(ready)
    nt = s // tm

    def body(d_ref, y_ref, o_ref, b_ref, z_ref, gp_ref, ga_ref, gc_ref, wo_ref, gm_ref, sel_ref, *rest):
        grads = rest[:nw]
        dy_ref, do_ref, db_ref, dz_ref, delta_ref, dgp_ref, dga_ref, dgc_ref = rest[nw:nw + 8]
        taken = rest[nw + 8:2 * nw + 8]
        send, recv = rest[2 * nw + 8:]
        first = pl.program_id(0) == 0
        pair_start, pair_finish = _pair_stages(grads, kinds, taken, send, recv)
        pl.when(first)(pair_start)
        yv = y_ref[...]
        ry = _rms(yv)
        ny = yv * ry
        d = d_ref[...]
        _acc_rows(dgp_ref, first, d * ny)
        dy = _norm_bwd(d, ny, ry, gp_ref[...]).astype(MXU_DTYPE)
        dy_ref[...] = dy
        dm = _dot_nt(dy, wo_ref[...])
        gm = gm_ref[0:128, 0:128]

        @pl.when(first)
        def _():
            dga_ref[...] = jnp.zeros_like(dga_ref)
            dgc_ref[...] = jnp.zeros_like(dgc_ref)

        def group_bwd(val, dmv, gain_ref, dg_ref, cols):
            rg = lax.rsqrt(_group_ms(val * val, gm) + EPS)
            nv = val * rg
            dg_ref[:, cols] += jnp.sum(dmv * nv, axis=0, keepdims=True)
            t = dmv * gain_ref[:, cols]
            return rg * (t - nv * _group_ms(t * nv, gm))

        delta = jnp.zeros((tm, 128), F32)
        for lo in range(0, ATTN_W, 128):
            cols = slice(lo, lo + 128)
            ov = o_ref[:, cols].astype(F32)
            d_o = group_bwd(ov, dm[:, cols], ga_ref, dga_ref, cols)
            do_ref[:, cols] = d_o.astype(MXU_DTYPE)
            delta = delta + _dot_exact(d_o * ov, sel_ref[cols, :], 2)
            zv = z_ref[:, cols].astype(F32)
            bv = b_ref[:, cols].astype(F32)
            d_cv = group_bwd(bv * zv, dm[:, ATTN_W + lo:ATTN_W + lo + 128], gc_ref, dgc_ref, cols)
            db_ref[:, cols] = (d_cv * zv).astype(MXU_DTYPE)
            dz_ref[:, cols] = d_cv * bv
        delta_ref[...] = delta.T[0:HEAD_ROWS, :]
        pl.when(pl.program_id(0) == nt - 1)(pair_finish)

    sd = jax.ShapeDtypeStruct
    taken_shape = [sd((N_CHIPS, g.shape[-2], g.shape[-1] if kd == "rows" else g.shape[-1] // N_CHIPS), F32)
                   for g, kd in zip(ready, kinds)]
    out = pl.pallas_call(
        body, name="mixer_bwd", grid=(nt,),
        in_specs=[_tok(tm, D_MODEL), _tok(tm, D_MODEL), _tok(tm, 512), _tok(tm, 512), _tok(tm, 512),
                  _whole((1, D_MODEL)), _whole((1, 512)), _whole((1, 512)),
                  _whole((D_MODEL, D_MODEL), single=True), _whole((512, 512)), _whole((512, 128))] + [ANY] * nw,
        out_specs=[_tok(tm, D_MODEL), _tok(tm, 512), _tok(tm, 512), _tok(tm, 512), _feat(HEAD_ROWS, tm),
                   _whole((1, D_MODEL)), _whole((1, 512)), _whole((1, 512))] + [ANY] * nw,
        out_shape=[sd((s, D_MODEL), MXU_DTYPE), sd((s, 512), MXU_DTYPE), sd((s, 512), MXU_DTYPE), sd((s, 512), F32),
                   sd((HEAD_ROWS, s), F32), sd((1, D_MODEL), F32), sd((1, 512), F32), sd((1, 512), F32)] + taken_shape,
        scratch_shapes=[pltpu.SemaphoreType.DMA((nw, N_CHIPS)), pltpu.SemaphoreType.DMA((nw, N_CHIPS))],
        compiler_params=_params(("arbitrary",)),
    )(dx2, y, o_attn, gate_b, z, g_post, g_attn, g_conv, w_out, gmat, sel, *ready)
    return out[:8], out[8:]


def _attn_bwd(qs, k_t, v, do, probs, probs_max, lse, delta, t, parts):
    s = qs.shape[0]
    n = s // t
    pairs = [(i, j) for j in range(n) for i in range(j, n)]
    it = jnp.asarray(np.array([p[0] for p in pairs], np.int32))
    jt = jnp.asarray(np.array([p[1] for p in pairs], np.int32))
    ft = jnp.asarray(np.array([p[0] * (p[0] + 1) // 2 + p[1] for p in pairs], np.int32))

    nw = len(parts)

    def body(it_ref, jt_ref, ft_ref, q_ref, kt_ref, v_ref, do_ref, p_ref, pm_ref, lse_ref, dl_ref, *rest):
        pb = rest[:nw]
        dq_ref, dk_ref, dv_ref, dc_ref, dcq_ref = rest[nw:nw + 5]
        rcv = rest[nw + 5:2 * nw + 5]
        dk_sc, dv_sc, dc_sc, send, recv = rest[2 * nw + 5:]
        p = pl.program_id(0)
        i = it_ref[p]
        j = jt_ref[p]
        chip_start, chip_finish = _chip_stages(pb, rcv, send, recv)

        @pl.when(p == 0)
        def _():
            chip_start()
            dq_ref[...] = jnp.zeros_like(dq_ref)
            dcq_ref[...] = jnp.zeros_like(dcq_ref)

        @pl.when(i == j)
        def _():
            dk_sc[...] = jnp.zeros_like(dk_sc)
            dv_sc[...] = jnp.zeros_like(dv_sc)
            dc_sc[...] = jnp.zeros_like(dc_sc)

        def pair_step(pp):
            lanes = _pair_lanes(pp)
            qp = q_ref[:, lanes]
            vp = v_ref[:, lanes]
            dop = do_ref[:, lanes]
            lane = lax.broadcasted_iota(jnp.int32, (t, 128), 1)
            for hb in range(2):
                h = 2 * pp + hb
                row = pl.ds(h, 1)
                pt = p_ref[0, h].astype(F32) * jnp.exp(pm_ref[0, row, :] - lse_ref[row, :])
                dv_sc[:, lanes] += _dot(pt.astype(MXU_DTYPE), _only_head(dop, hb))
                dst = pt * (_dot_nt(_only_head(vp, hb), dop) - dl_ref[row, :])
                dc_sc[...] -= jnp.where(lane == h, jnp.sum(dst, axis=1, keepdims=True), 0.0)
                dcq_ref[i, row, :] += jnp.sum(dst, axis=0, keepdims=True)
                dsb = dst.astype(MXU_DTYPE)
                dk_sc[:, lanes] += _dot(dsb, _only_head(qp, hb))
                rows = _head_rows(h)
                dq_ref[i, rows, :] += _dot(kt_ref[rows, :], dsb)

        for pp in range(N_HEADS // 2):
            pair_step(pp)

        @pl.when(i == n - 1)
        def _():
            dk_ref[...] = dk_sc[...].astype(MXU_DTYPE)
            dv_ref[...] = dv_sc[...].astype(MXU_DTYPE)
            dc_ref[...] = dc_sc[...]

        pl.when(p == len(pairs) - 1)(chip_finish)

    qi = lambda p, it_, jt_, ft_: (it_[p], 0)
    kj = lambda p, it_, jt_, ft_: (jt_[p], 0)
    row_i = lambda p, it_, jt_, ft_: (0, it_[p])
    gs = pltpu.PrefetchScalarGridSpec(
        num_scalar_prefetch=3, grid=(len(pairs),),
        in_specs=[pl.BlockSpec((t, ATTN_W), qi),
                  pl.BlockSpec((ATTN_W, t), lambda p, it_, jt_, ft_: (0, jt_[p])),
                  pl.BlockSpec((t, ATTN_W), kj), pl.BlockSpec((t, ATTN_W), qi),
                  pl.BlockSpec((1, N_HEADS, t, t), lambda p, it_, jt_, ft_: (ft_[p], 0, 0, 0)),
                  pl.BlockSpec((1, HEAD_ROWS, t), lambda p, it_, jt_, ft_: (ft_[p], 0, 0)),
                  pl.BlockSpec((HEAD_ROWS, t), row_i), pl.BlockSpec((HEAD_ROWS, t), row_i)] + [ANY] * nw,
        out_specs=[pl.BlockSpec((n, ATTN_W, t), lambda p, it_, jt_, ft_: (0, 0, 0)),
                   pl.BlockSpec((t, ATTN_W), kj), pl.BlockSpec((t, ATTN_W), kj),
                   pl.BlockSpec((t, 128), kj),
                   pl.BlockSpec((n, HEAD_ROWS, t), lambda p, it_, jt_, ft_: (0, 0, 0))] + [ANY] * nw,
        scratch_shapes=[pltpu.VMEM((t, ATTN_W), F32), pltpu.VMEM((t, ATTN_W), F32),
                        pltpu.VMEM((t, 128), F32), pltpu.SemaphoreType.DMA((nw, 3)), pltpu.SemaphoreType.DMA((nw, 3))])
    sd = jax.ShapeDtypeStruct
    out = pl.pallas_call(
        body, name="attn_bwd", grid_spec=gs,
        out_shape=[sd((n, ATTN_W, t), F32), sd((s, ATTN_W), MXU_DTYPE), sd((s, ATTN_W), MXU_DTYPE),
                   sd((s, 128), F32), sd((n, HEAD_ROWS, t), F32)] + [sd((3,) + a.shape[1:], a.dtype) for a in parts],
        compiler_params=_params(("arbitrary",)),
    )(it, jt, ft, qs, k_t, v, do, probs, probs_max, lse, delta, *parts)
    return out[:5], out[5:]


def _forget_bwd(dc_rows, dc_cols, z_t, b_col):
    s = z_t.shape[1]
    nb = s // 128

    def body(dr_ref, dcc_ref, z_ref, b_ref, dz_ref, db_ref):
        lower = _tri(128, False)
        real = lax.broadcasted_iota(jnp.int32, (HEAD_ROWS, 128), 0) < N_HEADS

        tail = jnp.zeros((HEAD_ROWS, 1), F32)
        dbias = jnp.zeros((HEAD_ROWS, 1), F32)
        for m in range(nb):
            off = (nb - 1 - m) * 128
            dc = dr_ref[:, off:off + 128] + dcc_ref[off:off + 128, :].T[0:HEAD_ROWS, :]
            dlf = _dot_exact(dc, lower, 3) + tail
            dz = dlf * jax.nn.sigmoid(-(z_ref[0:HEAD_ROWS, off:off + 128] + b_ref[...]))
            dz = jnp.where(real, dz, 0.0)
            dz_ref[off:off + 128, :] = _rows_to_cols(dz)
            tail = tail + jnp.sum(dc, axis=1, keepdims=True)
            dbias = dbias + jnp.sum(dz, axis=1, keepdims=True)
        db_ref[...] = jnp.broadcast_to(dbias, db_ref.shape)

    return pl.pallas_call(
        body, name="forget_bwd",
        out_shape=[jax.ShapeDtypeStruct((s, 128), F32), jax.ShapeDtypeStruct((HEAD_ROWS, 128), F32)],
        compiler_params=_params())(dc_rows, dc_cols, z_t, b_col)


def _inproj_bwd(dz, gate_c, u, conv_w, dq, dk, dv, dzf, db, x, dx2, g_pre, w_t, tm):
    s = x.shape[0]
    nt = s // tm
    t = dq.shape[2]
    assert t % tm == 0 and dq.shape[:2] == (s // t, ATTN_W)
    per = t // tm

    def body(dz_ref, dzn_ref, c_ref, u_ref, cp_ref, up_ref, cw_ref, dq_ref, dk_ref, dv_ref, dzf_ref, db_ref,
             x_ref, dx2_ref, g_ref, w_ref, gx_ref, dp_ref, dg_ref, dcw_ref):
        i = pl.program_id(0)
        first = i == 0
        last = i == nt - 1
        @pl.when(first)
        def _():
            dcw_ref[...] = jnp.zeros_like(dcw_ref)

        for lo in range(0, CONV_W, 128):
            cols = slice(lo, lo + 128)
            dzv = dz_ref[:, cols]
            row = lax.broadcasted_iota(jnp.int32, dzv.shape, 0)
            n0 = jnp.where(last, 0.0, dzn_ref[0:1, cols])
            n1 = jnp.where(last, 0.0, dzn_ref[1:2, cols])
            dz1 = jnp.where(row == tm - 1, n0, pltpu.roll(dzv, tm - 1, 0))
            dz2 = jnp.where(row == tm - 1, n1, jnp.where(row == tm - 2, n0, pltpu.roll(dzv, tm - 2, 0)))
            dcu = cw_ref[2:3, cols] * dzv + cw_ref[1:2, cols] * dz1 + cw_ref[0:1, cols] * dz2
            cv = c_ref[:, cols].astype(F32)
            uv = u_ref[:, cols].astype(F32)
            cu = cv * uv
            cu1, cu2 = _shift_down(cu, (cp_ref, up_ref), first, cols)
            dcw_ref[0:1, cols] += jnp.sum(dzv * cu2, axis=0, keepdims=True)
            dcw_ref[1:2, cols] += jnp.sum(dzv * cu1, axis=0, keepdims=True)
            dcw_ref[2:3, cols] += jnp.sum(dzv * cu, axis=0, keepdims=True)
            dp_ref[:, OFF_C + lo:OFF_C + lo + 128] = (dcu * uv).astype(MXU_DTYPE)
            dp_ref[:, OFF_U + lo:OFF_U + lo + 128] = (dcu * cv).astype(MXU_DTYPE)

        dp_ref[:, 0:512] = (dq_ref[0].T * Q_SCALE).astype(MXU_DTYPE)
        dp_ref[:, 512:1024] = dk_ref[...].astype(MXU_DTYPE)
        dp_ref[:, 1024:OFF_F] = dv_ref[...].astype(MXU_DTYPE)
        dp_ref[:, OFF_F:OFF_B] = dzf_ref[...].astype(MXU_DTYPE)
        dp_ref[:, OFF_B:OFF_C] = db_ref[...].astype(MXU_DTYPE)
        dh = _dot(dp_ref[...], w_ref[...])
        xv = x_ref[...]
        r1 = _rms(xv)
        nx = xv * r1
        _acc_rows(dg_ref, first, dh * nx)
        gx_ref[...] = dx2_ref[...] + _norm_bwd(dh, nx, r1, g_ref[...])

    prev = _halo_before(tm)
    nxt = pl.BlockSpec((8, 512), lambda i: (jnp.minimum((i + 1) * (tm // 8), s // 8 - 1), 0))
    sd = jax.ShapeDtypeStruct
    return pl.pallas_call(
        body, name="inproj_bwd", grid=(nt,),
        in_specs=[_tok(tm, 512), nxt, _tok(tm, 512), _tok(tm, 512), prev, prev, _whole((3, 512)),
                  pl.BlockSpec((1, ATTN_W, tm), lambda i: (i // per, 0, i % per)), _tok(tm, 512), _tok(tm, 512),
                  _tok(tm, 128),
                  _tok(tm, 512),
                  _tok(tm, D_MODEL), _tok(tm, D_MODEL), _whole((1, D_MODEL)), _whole((IN_PAD, D_MODEL), single=True)],
        out_specs=[_tok(tm, D_MODEL), _tok(tm, IN_PAD), _whole((1, D_MODEL)), _whole((8, 512))],
        out_shape=[sd((s, D_MODEL), F32), sd((s, IN_PAD), MXU_DTYPE), sd((1, D_MODEL), F32), sd((8, 512), F32)],
        compiler_params=_params(("arbitrary",)),
    )(dz, dz, gate_c, u, gate_c, u, conv_w, dq, dk, dv, dzf, db, x, dx2, g_pre, w_t)


def _tile(s, want):
    return want if s % want == 0 else s


def _halves(a):
    return a.reshape(2, a.shape[0] // 2, a.shape[1])


def _device_step(x, target, w, mom1, mom2, w_in_t, m_in_t, v_in_t, c_idx, me_idx):
    s = x.shape[0]
    tm = _tile(s, 512)
    tf = _tile(s, 256)
    ta = _tile(s, 512)
    tkk = _tile(s, 2048)
    gidx = np.arange(512) // HEAD_DIM
    gmat = jnp.asarray(gidx[:, None] == gidx[None, :], MXU_DTYPE)
    sel = jnp.asarray(gidx[:, None] == np.arange(128)[None, :], MXU_DTYPE)
    g_mix_pre, g_mix_post, g_ffn_pre, g_ffn_post = w["g_mix_pre"], w["g_mix_post"], w["g_ffn_pre"], w["g_ffn_post"]
    g_attn, g_conv, b_forget = w["g_attn_out"], w["g_conv_out"], w["b_forget"]
    shard = {n: _halves(w[n][0].astype(MXU_DTYPE)) for n in BIG[1:]}
    piece_rows = IN_W // N_CHIPS

    g_in, conv_all = _gather_weights([w_in_t.reshape(piece_rows, D_MODEL).astype(MXU_DTYPE)], w["conv_w"][0])
    w_rows = g_in.reshape(IN_W, D_MODEL)
    w_t = jnp.concatenate([w_rows[:OFF_F + N_HEADS], jnp.zeros((OFF_B - OFF_F - N_HEADS, D_MODEL), MXU_DTYPE),
                           w_rows[OFF_F + N_HEADS:]], axis=0)
    conv_w = jnp.transpose(conv_all, (1, 0, 2)).reshape(3, CONV_W)

    h1, qs, k, v, k_t, v_t, z_t, gate_b, gate_c, u = _inproj_fwd(x, g_mix_pre, w_t, tm)
    b_col = jnp.pad(jnp.transpose(b_forget), ((0, HEAD_ROWS - N_HEADS), (0, 0)))
    q_bias, k_bias = _forget_fwd(z_t, b_col)
    o_attn, lse, probs, probs_max, (g_out, g_gu, g_dn) = _attn_fwd(
        qs, k, v_t, q_bias, k_bias, ta, [shard["w_out"], shard["w_gate_up"], shard["w_down"]])
    w_out = g_out.reshape(D_MODEL, D_MODEL)
    w_gu = g_gu.reshape(N_CHIPS, D_MODEL, FF_PIECE)
    w_dn = g_dn.reshape(2, FF_PIECE, D_MODEL)
    x2, merged, y, z = _mixer_fwd(x, o_attn, gate_b, gate_c, u, conv_w, g_attn, g_conv, w_out, g_mix_post, gmat, tm)
    h2, g, up, a, ff, dout, loss_acc = _ffn_fwd(x2, target, g_ffn_pre, w_gu, w_dn, g_ffn_post, tf)

    dx2, dff, dgu, dg_ffn_post, dg_ffn_pre = _ffn_bwd(dout, ff, x2, g, up, g_ffn_post, g_ffn_pre, w_gu, w_dn, tf)
    dw_dn = _tn_matmul(a, dff, FF_PIECE, 1024, tkk, "dw_down").reshape(N_CHIPS, 2, D_FF // (2 * N_CHIPS), D_MODEL)
    dw_gu = _tn_matmul(h2, dgu, 1024, FF_PIECE, tkk, "dw_gate_up").reshape(2, D_MODEL // 2, 2 * D_FF)
    (dy, d_o, d_b, dz, delta, dg_mix_post, dg_attn, dg_conv), (a_gu, a_dn) = _mixer_bwd(
        dx2, y, o_attn, gate_b, z, g_mix_post, g_attn, g_conv, w_out, gmat, sel, tm, [dw_gu, dw_dn], ["cols", "rows"])
    dw_out = _tn_matmul(merged, dy, 1024, 1024, tkk, "dw_out").reshape(N_CHIPS, 2, D_MODEL // (2 * N_CHIPS), D_MODEL)
    place = jnp.concatenate([c_idx, me_idx])
    *sum_gu, a_out = _pair_sum(place, dw_gu, "cols", a_gu, "pair_sum_w_gate_up", [dw_out], ["rows"])
    sum_dn = _pair_sum(place, dw_dn, "rows", a_dn, "pair_sum_w_down")
    sum_out = _pair_sum(place, dw_out, "rows", a_out, "pair_sum_w_out")
    (dq_t, dk, dv, dc_cols, dcq), (r_gu, r_dn, r_out) = _attn_bwd(
        qs, k_t, v, d_o, probs, probs_max, lse, delta, ta, [sum_gu[1], sum_dn[1], sum_out[1]])
    dc_rows = jnp.transpose(dcq, (1, 0, 2)).reshape(HEAD_ROWS, s)
    dzf, db_f = _forget_bwd(dc_rows, dc_cols, z_t, b_col)
    grad_x, dproj, dg_mix_pre, dcw = _inproj_bwd(dz, gate_c, u, conv_w, dq_t, dk, dv, dzf, d_b,
                                                 x, dx2, g_mix_pre, w_t, tm)
    done = [_chip_sum(sb[0], r, "chip_sum_" + n)
            for n, sb, r in zip(BIG[1:], (sum_out, sum_gu, sum_dn), (r_out, r_gu, r_dn))]
    dw_t, done_theirs = _tn_matmul(dproj, h1, 640, 1024, tkk, "dw_in", done)
    dw_in = jnp.concatenate([dw_t[:OFF_F + N_HEADS], dw_t[OFF_B:]], axis=0).reshape(N_CHIPS, piece_rows, D_MODEL)

    (a_in,) = _pair_exchange([dw_in], ["lanes"])
    sum_in = _pair_sum(place, dw_in, "lanes", a_in, "pair_sum_w_in")
    small = dict(b_forget=db_f[:N_HEADS, 0], g_attn_out=dg_attn, g_conv_out=dg_conv, g_mix_pre=dg_mix_pre,
                 g_mix_post=dg_mix_post, g_ffn_pre=dg_ffn_pre, g_ffn_post=dg_ffn_post)
    (r_in,), small_all = _chip_exchange([sum_in[1]], _pack_small(small, dcw[:3], loss_acc[0, 0]))
    t_in = _chip_sum(sum_in[0], r_in, "chip_sum_w_in")
    (s_in,) = _pair_share([t_in], "pair_share_w_in")
    new = {"w_in": _adamw_lanes(c_idx, w_in_t, t_in, s_in, m_in_t, v_in_t, "adamw_w_in")}
    for n, mine, theirs in zip(BIG[1:], done, done_theirs):
        new[n] = _adamw(c_idx, w[n][0], mine, theirs, mom1[n][0], mom2[n][0], 2, "adamw_" + n)
    return grad_x, new, small_all


BIG = ("w_in", "w_out", "w_gate_up", "w_down")
ANY = pl.BlockSpec(memory_space=pl.ANY)


def _place():
    x, y, c = lax.axis_index("x"), lax.axis_index("y"), lax.axis_index("c")
    others = [(1 - x, y), (x, 1 - y), (1 - x, 1 - y)]
    return x, y, c, 2 * x + y, others, [2 * px + py for px, py in others]


def _remote(src, dst, send, recv, dev):
    return pltpu.make_async_remote_copy(src_ref=src, dst_ref=dst, send_sem=send, recv_sem=recv,
                                        device_id=dev, device_id_type=MESH_ID)


def _gather_stages(sh, outs, send, recv):
    x, y, c, me, others, chips = _place()
    sib = (x, y, 1 - c)
    every = [(w, kk) for w in range(len(sh)) for kk in range(3)]

    def half_of(ref, half, piece=None):
        ref = ref if piece is None else ref.at[piece]
        if len(ref.shape) == 3:
            return ref.at[half]
        hc = ref.shape[1] // 2
        return ref.at[:, pl.ds(pl.multiple_of(half * hc, 128), hc)]

    def first(w, kk):
        return _remote(half_of(sh[w], c), half_of(outs[w], c, me), send.at[w, kk], recv.at[w, kk], (*others[kk], c))

    def landed(w, kk):
        r = half_of(outs[w], c, chips[kk])
        return _remote(r, r, send.at[w, kk], recv.at[w, kk], (*others[kk], c))

    def onward(w, kk, half):
        r = half_of(outs[w], half, chips[kk])
        return _remote(r, r, send.at[w, 3 + kk], recv.at[w, 3 + kk], sib)

    def start():
        for w, kk in every:
            first(w, kk).start()

    def forward():
        for w, kk in every:
            landed(w, kk).wait_recv()
            onward(w, kk, c).start()

    def finish():
        for w, kk in every:
            onward(w, kk, 1 - c).wait_recv()
        for w, kk in every:
            first(w, kk).wait_send()
            onward(w, kk, c).wait_send()

    return start, forward, finish


def _pair_piece(ref, kind, p, half):
    if kind == "rows":
        return ref.at[p, half]
    if kind == "lanes":
        hc = ref.shape[2] // 2
        return ref.at[p, :, pl.ds(pl.multiple_of(half * hc, 128), hc)]
    cols = ref.shape[2] // N_CHIPS
    return ref.at[half, :, pl.ds(p * cols, cols)]


def _pair_stages(g, kinds, a, send, recv):
    x, y, c, _, _, _ = _place()
    copies = [_remote(_pair_piece(g[w], kinds[w], p, 1 - c), a[w].at[p], send.at[w, p], recv.at[w, p], (x, y, 1 - c))
              for w in range(len(g)) for p in range(N_CHIPS)]

    def start():
        for cp in copies:
            cp.start()

    def finish():
        for cp in copies:
            cp.wait()

    return start, finish


def _chip_stages(pb, rcv, send, recv):
    x, y, c, _, others, chips = _place()
    copies = [_remote(pb[w].at[chips[kk]], rcv[w].at[kk], send.at[w, kk], recv.at[w, kk], (*others[kk], c))
              for w in range(len(pb)) for kk in range(3)]

    def start():
        for cp in copies:
            cp.start()

    def finish():
        for cp in copies:
            cp.wait()

    return start, finish


def _gather_weights(shards, conv_w):
    n = len(shards)

    def body(*refs):
        sh, cw, outs, cwo = refs[:n], refs[n], refs[n + 1:2 * n + 1], refs[2 * n + 1]
        send, recv = refs[2 * n + 2:]
        x, y, c, me, others, chips = _place()
        start, forward, finish = _gather_stages(sh, outs, send, recv)
        start()
        small = [_remote(cw, cwo.at[me], send.at[n, kk], recv.at[n, kk], (*others[kk], c)) for kk in range(3)]
        for cp in small:
            cp.start()
        forward()
        for kk in range(3):
            _remote(cw, cwo.at[chips[kk]], send.at[n, kk], recv.at[n, kk], (*others[kk], c)).wait_recv()
        finish()
        for cp in small:
            cp.wait_send()

    out_shape = [jax.ShapeDtypeStruct((N_CHIPS,) + s.shape, s.dtype) for s in shards]
    out_shape.append(jax.ShapeDtypeStruct((N_CHIPS,) + conv_w.shape, conv_w.dtype))
    got = pl.pallas_call(
        body, name="gather_weights", in_specs=[ANY] * (n + 1), out_specs=[ANY] * (n + 1), out_shape=out_shape,
        scratch_shapes=[pltpu.SemaphoreType.DMA((n + 1, 6)), pltpu.SemaphoreType.DMA((n + 1, 6))],
    )(*shards, conv_w)
    me = 2 * lax.axis_index("x") + lax.axis_index("y")
    return [lax.dynamic_update_index_in_dim(g, own, me, 0) for g, own in zip(got, list(shards) + [conv_w])]


def _taken_shape(g, kind):
    if kind == "rows":
        return (N_CHIPS,) + g.shape[2:]
    if kind == "lanes":
        return g.shape[:2] + (g.shape[2] // 2,)
    return (N_CHIPS, g.shape[1], g.shape[2] // N_CHIPS)


def _pair_sum(place, g, kind, a, name, ready=(), ready_kinds=()):
    _, half, cols = a.shape
    nw = len(ready)
    if kind == "rows":
        mine = pl.BlockSpec((1, 1, half, cols), lambda p, pr: (p, pr[0], 0, 0))
    elif kind == "lanes":
        mine = pl.BlockSpec((1, half, cols), lambda p, pr: (p, 0, pr[0]))
    else:
        mine = pl.BlockSpec((1, half, cols), lambda p, pr: (pr[0], 0, p))

    def body(place_ref, g_ref, a_ref, *rest):
        grads, (own_ref, pb_ref), taken = rest[:nw], rest[nw:nw + 2], rest[nw + 2:2 * nw + 2]
        if nw:
            pair_start, pair_finish = _pair_stages(grads, ready_kinds, taken, *rest[2 * nw + 2:])
            pl.when(pl.program_id(0) == 0)(pair_start)
        tot = (g_ref[0, 0] if kind == "rows" else g_ref[0]) + a_ref[0]
        pb_ref[0] = tot.astype(BF16)

        @pl.when(pl.program_id(0) == place_ref[1])
        def _():
            own_ref[...] = tot

        if nw:
            pl.when(pl.program_id(0) == N_CHIPS - 1)(pair_finish)

    sems = [pltpu.SemaphoreType.DMA((nw, N_CHIPS)), pltpu.SemaphoreType.DMA((nw, N_CHIPS))] if nw else []
    gs = pltpu.PrefetchScalarGridSpec(
        num_scalar_prefetch=1, grid=(N_CHIPS,),
        in_specs=[mine, pl.BlockSpec((1, half, cols), lambda p, pr: (p, 0, 0))] + [ANY] * nw,
        out_specs=[pl.BlockSpec((half, cols), lambda p, pr: (0, 0)),
                   pl.BlockSpec((1, half, cols), lambda p, pr: (p, 0, 0))] + [ANY] * nw,
        scratch_shapes=sems)
    out = pl.pallas_call(
        body, name=name, grid_spec=gs,
        out_shape=[jax.ShapeDtypeStruct((half, cols), F32), jax.ShapeDtypeStruct((N_CHIPS, half, cols), BF16)]
        + [jax.ShapeDtypeStruct(_taken_shape(r, kd), r.dtype) for r, kd in zip(ready, ready_kinds)],
        compiler_params=_params(("arbitrary",)),
    )(place, g, a, *ready)
    return list(out)


def _chip_sum(own, rcv, name):
    half, cols = own.shape

    def body(o_ref, r_ref, t_ref):
        t_ref[...] = ((o_ref[...] + r_ref[0].astype(F32)) + r_ref[1].astype(F32)) + r_ref[2].astype(F32)

    return pl.pallas_call(
        body, name=name, grid=(1,),
        in_specs=[pl.BlockSpec((half, cols), lambda i: (0, 0)), pl.BlockSpec((3, half, cols), lambda i: (0, 0, 0))],
        out_specs=pl.BlockSpec((half, cols), lambda i: (0, 0)),
        out_shape=jax.ShapeDtypeStruct((half, cols), F32), compiler_params=_params(("arbitrary",)),
    )(own, rcv)


def _small_stages(sm, smg, send, recv):
    x, y, c, _, _, _ = _place()

    def peer(r):
        return (1 - x if r & 4 else x, 1 - y if r & 2 else y, 1 - c if r & 1 else c)

    mine = 4 * x + 2 * y + c
    copies = [_remote(sm, smg.at[mine], send.at[r - 1], recv.at[r - 1], peer(r)) for r in range(1, 8)]

    def start():
        for cp in copies:
            cp.start()

    def finish():
        for r in range(1, 8):
            px, py, pc = peer(r)
            _remote(sm, smg.at[4 * px + 2 * py + pc], send.at[r - 1], recv.at[r - 1], (px, py, pc)).wait_recv()
        for cp in copies:
            cp.wait_send()

    return start, finish


def _pair_exchange(grads, kinds):
    n = len(grads)

    def body(*refs):
        start, finish = _pair_stages(refs[:n], kinds, refs[n:2 * n], *refs[2 * n:])
        start()
        finish()

    return pl.pallas_call(
        body, name="pair_exchange", in_specs=[ANY] * n, out_specs=[ANY] * n,
        out_shape=[jax.ShapeDtypeStruct(_taken_shape(g, kd), g.dtype) for g, kd in zip(grads, kinds)],
        scratch_shapes=[pltpu.SemaphoreType.DMA((n, N_CHIPS)), pltpu.SemaphoreType.DMA((n, N_CHIPS))],
    )(*grads)


def _chip_exchange(parts, small):
    n = len(parts)

    def body(*refs):
        pb, sm, rcv, smg = refs[:n], refs[n], refs[n + 1:2 * n + 1], refs[2 * n + 1]
        send, recv, ssend, srecv = refs[2 * n + 2:]
        chip_start, chip_finish = _chip_stages(pb, rcv, send, recv)
        small_start, small_finish = _small_stages(sm, smg, ssend, srecv)
        chip_start()
        small_start()
        chip_finish()
        small_finish()

    out_shape = [jax.ShapeDtypeStruct((3,) + p.shape[1:], p.dtype) for p in parts]
    out_shape.append(jax.ShapeDtypeStruct((8,) + small.shape, small.dtype))
    *arrived, small_land = pl.pallas_call(
        body, name="chip_exchange", in_specs=[ANY] * (n + 1), out_specs=[ANY] * (n + 1), out_shape=out_shape,
        scratch_shapes=[pltpu.SemaphoreType.DMA((n, 3)), pltpu.SemaphoreType.DMA((n, 3)),
                        pltpu.SemaphoreType.DMA((7,)), pltpu.SemaphoreType.DMA((7,))],
    )(*parts, small)
    mine = 4 * lax.axis_index("x") + 2 * lax.axis_index("y") + lax.axis_index("c")
    return arrived, lax.dynamic_update_index_in_dim(small_land, small, mine, 0)


def _share_stages(t, g, send, recv):
    x, y, c, _, _, _ = _place()
    copies = [_remote(t[w], g[w], send.at[w], recv.at[w], (x, y, 1 - c)) for w in range(len(t))]

    def start():
        for cp in copies:
            cp.start()

    def finish():
        for cp in copies:
            cp.wait()

    return start, finish


def _pair_share(totals, name):
    n = len(totals)

    def body(*refs):
        start, finish = _share_stages(refs[:n], refs[n:2 * n], *refs[2 * n:])
        start()
        finish()

    return pl.pallas_call(
        body, name=name, in_specs=[ANY] * n, out_specs=[ANY] * n,
        out_shape=[jax.ShapeDtypeStruct(t.shape, t.dtype) for t in totals],
        scratch_shapes=[pltpu.SemaphoreType.DMA((n,)), pltpu.SemaphoreType.DMA((n,))],
    )(*totals)


def _adamw_math(w, g, m, v):
    m = ADAM_B1 * m + (1.0 - ADAM_B1) * g
    v = ADAM_B2 * v + (1.0 - ADAM_B2) * (g * g)
    m_hat = m / (1.0 - ADAM_B1 ** ADAM_STEP)
    v_hat = v / (1.0 - ADAM_B2 ** ADAM_STEP)
    delta = -ADAM_LR * (m_hat / (jnp.sqrt(v_hat) + ADAM_EPS) + ADAM_WD * w)
    return delta, m, v


def _adamw(c_idx, w, mine, theirs, m, v, nb, name):
    rows, cols = w.shape
    tr = rows // (2 * nb)

    def body(c_ref, w_ref, a_ref, b_ref, m_ref, v_ref, g_ref, d_ref, nm_ref, nv_ref):
        g = jnp.where(pl.program_id(0) == c_ref[0], a_ref[...], b_ref[...])
        g_ref[...] = g
        d_ref[...], nm_ref[...], nv_ref[...] = _adamw_math(w_ref[...], g, m_ref[...], v_ref[...])

    full = pl.BlockSpec((tr, cols), lambda hh, i, cr: (hh * nb + i, 0))
    half = pl.BlockSpec((tr, cols), lambda hh, i, cr: (i, 0))
    gs = pltpu.PrefetchScalarGridSpec(num_scalar_prefetch=1, grid=(2, nb), in_specs=[full, half, half, full, full],
                                      out_specs=[full] * 4)
    return pl.pallas_call(
        body, name=name, grid_spec=gs, out_shape=[jax.ShapeDtypeStruct((rows, cols), F32)] * 4,
        compiler_params=_params(("arbitrary", "arbitrary")),
    )(c_idx, w, mine, theirs, m, v)


def _adamw_lanes(c_idx, w, mine, theirs, m, v, name):
    rows, _, cols = w.shape
    hc = cols // 2

    def body(c_ref, w_ref, a_ref, b_ref, m_ref, v_ref, g_ref, d_ref, nm_ref, nv_ref):
        g = jnp.where(pl.program_id(0) == c_ref[0], a_ref[...], b_ref[...])
        g_ref[:, 0, :] = g
        d_ref[:, 0, :], nm_ref[:, 0, :], nv_ref[:, 0, :] = _adamw_math(w_ref[:, 0, :], g, m_ref[:, 0, :], v_ref[:, 0, :])

    full = pl.BlockSpec((rows, 1, hc), lambda hh, cr: (0, 0, hh))
    half = pl.BlockSpec((rows, hc), lambda hh, cr: (0, 0))
    gs = pltpu.PrefetchScalarGridSpec(num_scalar_prefetch=1, grid=(2,), in_specs=[full, half, half, full, full],
                                      out_specs=[full] * 4)
    return pl.pallas_call(
        body, name=name, grid_spec=gs, out_shape=[jax.ShapeDtypeStruct((rows, 1, cols), F32)] * 4,
        compiler_params=_params(("arbitrary",)),
    )(c_idx, w, mine, theirs, m, v)


SMALL = ("g_mix_pre", "g_mix_post", "g_ffn_pre", "g_ffn_post")
SMALL_ALL = SMALL + ("g_attn_out", "g_conv_out", "conv_w", "b_forget")
SMALL_AT = {"g_mix_pre": (0, 0, 1024), "g_mix_post": (1, 0, 1024), "g_ffn_pre": (2, 0, 1024),
            "g_ffn_post": (3, 0, 1024), "g_attn_out": (4, 0, 512), "g_conv_out": (4, 512, 512),
            "b_forget": (7, 0, N_HEADS)}
CONV_AT = ((5, 0), (5, 512), (6, 0))
LOSS_AT = (6, 512)


def _pack_small(t, conv_full, loss_sum):
    conv = jnp.concatenate([conv_full.reshape(1, 3 * CONV_W), loss_sum.reshape(1, 1),
                            jnp.zeros((1, 2048 - 3 * CONV_W - 1), F32)], axis=1).reshape(2, 1024)
    return jnp.concatenate([t[n].reshape(1, 1024) for n in SMALL]
                           + [jnp.concatenate([t["g_attn_out"].reshape(1, 512), t["g_conv_out"].reshape(1, 512)], axis=1),
                              conv, jnp.pad(t["b_forget"].reshape(1, N_HEADS), ((0, 0), (0, 1024 - N_HEADS)))], axis=0)


def _small_update(me_idx, gathered, w, m, v):
    def body(me_ref, gg_ref, *refs):
        k = len(SMALL_ALL)
        w_refs, m_refs, v_refs = refs[:k], refs[k:2 * k], refs[2 * k:3 * k]
        loss_ref = refs[3 * k]
        outs = refs[3 * k + 1:3 * k + 1 + 4 * k]
        sums = refs[-1]
        g = gg_ref[0]
        for dev in range(1, 8):
            g = g + gg_ref[dev]
        sums[...] = g
        loss_ref[...] = sums[LOSS_AT[0]:LOSS_AT[0] + 1, LOSS_AT[1]:LOSS_AT[1] + 1]
        mine = pl.multiple_of(me_ref[0] * 128, 128)
        for idx, name in enumerate(SMALL_ALL):
            g_ref, d_ref, nm_ref, nv_ref = outs[4 * idx:4 * idx + 4]
            if name == "conv_w":
                for r, (row, lo) in enumerate(CONV_AT):
                    gr = sums[row:row + 1, pl.ds(lo + mine, 128)]
                    g_ref[0, r:r + 1, :] = gr
                    d_ref[0, r:r + 1, :], nm_ref[0, r:r + 1, :], nv_ref[0, r:r + 1, :] = _adamw_math(
                        w_refs[idx][0, r:r + 1, :], gr, m_refs[idx][0, r:r + 1, :], v_refs[idx][0, r:r + 1, :])
            else:
                row, lo, n = SMALL_AT[name]
                gr = sums[row:row + 1, lo:lo + n]
                g_ref[...] = gr
                d_ref[...], nm_ref[...], nv_ref[...] = _adamw_math(w_refs[idx][...], gr, m_refs[idx][...],
                                                                    v_refs[idx][...])

    def whole(a):
        nd = a.ndim
        return pl.BlockSpec(a.shape, lambda i, mr: (0,) * nd)

    ins = [t[n] for t in (w, m, v) for n in SMALL_ALL]
    out_shape = [jax.ShapeDtypeStruct((1, 1), F32)]
    for n in SMALL_ALL:
        out_shape += [jax.ShapeDtypeStruct(w[n].shape, F32)] * 4
    gs = pltpu.PrefetchScalarGridSpec(
        num_scalar_prefetch=1, grid=(1,), in_specs=[whole(gathered)] + [whole(a) for a in ins],
        out_specs=[whole(o) for o in out_shape], scratch_shapes=[pltpu.VMEM((8, 1024), F32)])
    out = pl.pallas_call(body, name="small_update", grid_spec=gs, out_shape=out_shape,
                         compiler_params=_params(("arbitrary",)))(me_idx, gathered, *ins)
    return out[0], {n: out[1 + 4 * i:5 + 4 * i] for i, n in enumerate(SMALL_ALL)}


def kernel(x, w_in, b_forget, conv_w, g_attn_out, g_conv_out, w_out, g_mix_pre, g_mix_post, w_gate_up, w_down, g_ffn_pre, g_ffn_post, loss_target, m_w_in, m_b_forget, m_conv_w, m_g_attn_out, m_g_conv_out, m_w_out, m_g_mix_pre, m_g_mix_post, m_w_gate_up, m_w_down, m_g_ffn_pre, m_g_ffn_post, v_w_in, v_b_forget, v_conv_w, v_g_attn_out, v_g_conv_out, v_w_out, v_g_mix_pre, v_g_mix_post, v_w_gate_up, v_w_down, v_g_ffn_pre, v_g_ffn_post):
    w = dict(w_in=w_in, b_forget=b_forget, conv_w=conv_w, g_attn_out=g_attn_out, g_conv_out=g_conv_out, w_out=w_out,
             g_mix_pre=g_mix_pre, g_mix_post=g_mix_post, w_gate_up=w_gate_up, w_down=w_down, g_ffn_pre=g_ffn_pre,
             g_ffn_post=g_ffn_post)
    m = dict(w_in=m_w_in, b_forget=m_b_forget, conv_w=m_conv_w, g_attn_out=m_g_attn_out, g_conv_out=m_g_conv_out,
             w_out=m_w_out, g_mix_pre=m_g_mix_pre, g_mix_post=m_g_mix_post, w_gate_up=m_w_gate_up, w_down=m_w_down,
             g_ffn_pre=m_g_ffn_pre, g_ffn_post=m_g_ffn_post)
    v = dict(w_in=v_w_in, b_forget=v_b_forget, conv_w=v_conv_w, g_attn_out=v_g_attn_out, g_conv_out=v_g_conv_out,
             w_out=v_w_out, g_mix_pre=v_g_mix_pre, g_mix_post=v_g_mix_post, w_gate_up=v_w_gate_up, w_down=v_w_down,
             g_ffn_pre=v_g_ffn_pre, g_ffn_post=v_g_ffn_post)
    cx, cy, cc = lax.axis_index("x"), lax.axis_index("y"), lax.axis_index("c")
    me = 2 * cx + cy
    c_idx = cc.astype(jnp.int32).reshape(1)
    me_idx = me.astype(jnp.int32).reshape(1)

    stored = lambda a: jnp.transpose(a, (2, 0, 1))
    grad_x, big, small_all = _device_step(x[0], loss_target[0], w, m, v, stored(w_in), stored(m_w_in),
                                          stored(v_w_in), c_idx, me_idx)
    gsum, delta, new_m, new_v = {}, {}, {}, {}
    for n in BIG:
        back = (lambda r: jnp.transpose(r, (1, 2, 0))) if n == "w_in" else (lambda r: r[None])
        gsum[n], delta[n], new_m[n], new_v[n] = [back(r) for r in big[n]]
    loss_sum, small_new = _small_update(me_idx, small_all, w, m, v)
    for n in SMALL_ALL:
        gsum[n], delta[n], new_m[n], new_v[n] = small_new[n]
    loss = 0.5 * loss_sum[0, 0]

    order = ("w_in", "b_forget", "conv_w", "g_attn_out", "g_conv_out", "w_out", "g_mix_pre", "g_mix_post",
             "w_gate_up", "w_down", "g_ffn_pre", "g_ffn_post")
    return (loss, grad_x[None], *[gsum[n] for n in order], *[delta[n] for n in order],
            *[new_m[n] for n in order], *[new_v[n] for n in order])
```

```python
import jax
import jax.numpy as jnp
import numpy as np
from jax import lax
from jax.experimental import pallas as pl
from jax.experimental.pallas import tpu as pltpu

F32 = jnp.float32
BF16 = jnp.bfloat16
MXU_DTYPE = jnp.bfloat16

D_MODEL = 1024
HEAD_DIM = 64
N_HEADS = 8
ATTN_W = 512
CONV_W = 512
D_FF = 2816
FF_PIECE = 1408
EPS = 1e-6
Q_SCALE = HEAD_DIM ** -0.5

OFF_F = 1536
OFF_B = 1664
OFF_C = 2176
OFF_U = 2688
IN_PAD = 3200
IN_W = 3080
N_CHIPS = 4

ADAM_LR = 0.001
ADAM_B1 = 0.9
ADAM_B2 = 0.999
ADAM_EPS = 1e-08
ADAM_WD = 0.01
ADAM_STEP = 10

VMEM_LIMIT_V7X = 56 * 1024 * 1024
MESH_ID = pl.DeviceIdType.MESH


def _params(sem=None, vmem=VMEM_LIMIT_V7X):
    kw = {"vmem_limit_bytes": vmem}
    if sem is not None:
        kw["dimension_semantics"] = sem
    return pltpu.CompilerParams(**kw)


def _dot(a, b):
    return jnp.dot(a, b, preferred_element_type=F32)


def _dot_nt(a, b):
    return lax.dot_general(a, b, (((1,), (1,)), ((), ())), preferred_element_type=F32)


def _dot_exact(x, ones, parts):
    if ones.dtype == F32:
        return _dot(x, ones)
    acc = None
    rem = x
    for _ in range(parts):
        piece = rem.astype(BF16)
        rem = rem - piece.astype(F32)
        term = _dot(piece, ones)
        acc = term if acc is None else acc + term
    return acc


def _rms(v):
    return lax.rsqrt(jnp.mean(v * v, axis=-1, keepdims=True) + EPS)


def _tok(tm, w):
    return pl.BlockSpec((tm, w), lambda i: (i, 0))


def _whole(shape, single=False):
    nd = len(shape)
    if single:
        return pl.BlockSpec(shape, lambda i: (0,) * nd, pipeline_mode=pl.Buffered(1))
    return pl.BlockSpec(shape, lambda i: (0,) * nd)


def _feat(rows, tm):
    return pl.BlockSpec((rows, tm), lambda i: (0, i))


def _inproj_fwd(x, g_pre, w_t, tm):
    s = x.shape[0]

    def body(x_ref, g_ref, w_ref, h_ref, q_ref, k_ref, v_ref, kt_ref, vt_ref, zt_ref, b_ref, c_ref, u_ref):
        xv = x_ref[...]
        h = ((xv * _rms(xv)) * g_ref[...]).astype(MXU_DTYPE)
        h_ref[...] = h

        def proj(lo, hi):
            return _dot_nt(h, w_ref[lo:hi, :])

        q_ref[...] = (proj(0, 512) * Q_SCALE).astype(MXU_DTYPE)
        kt = _dot_nt(w_ref[512:1024, :], h)
        vt = _dot_nt(w_ref[1024:OFF_F, :], h)
        kt_ref[...] = kt.astype(MXU_DTYPE)
        vt_ref[...] = vt.astype(MXU_DTYPE)
        k_ref[...] = kt.T.astype(MXU_DTYPE)
        v_ref[...] = vt.T.astype(MXU_DTYPE)
        zt_ref[...] = _dot_nt(w_ref[OFF_F:OFF_B, :], h)
        b_ref[...] = proj(OFF_B, OFF_C).astype(MXU_DTYPE)
        c_ref[...] = proj(OFF_C, OFF_U).astype(MXU_DTYPE)
        u_ref[...] = proj(OFF_U, IN_PAD).astype(MXU_DTYPE)

    sd = jax.ShapeDtypeStruct
    return pl.pallas_call(
        body, name="inproj_fwd", grid=(s // tm,),
        in_specs=[_tok(tm, D_MODEL), _whole((1, D_MODEL)), _whole((IN_PAD, D_MODEL), single=True)],
        out_specs=[_tok(tm, D_MODEL), _tok(tm, 512), _tok(tm, 512), _tok(tm, 512), _feat(512, tm), _feat(512, tm),
                   _feat(128, tm), _tok(tm, 512), _tok(tm, 512), _tok(tm, 512)],
        out_shape=[sd((s, D_MODEL), MXU_DTYPE), sd((s, 512), MXU_DTYPE), sd((s, 512), MXU_DTYPE),
                   sd((s, 512), MXU_DTYPE), sd((512, s), MXU_DTYPE), sd((512, s), MXU_DTYPE), sd((128, s), F32),
                   sd((s, 512), MXU_DTYPE), sd((s, 512), MXU_DTYPE), sd((s, 512), MXU_DTYPE)],
        compiler_params=_params(("arbitrary",)),
    )(x, g_pre, w_t)


def _tri(n, upper):
    r = lax.broadcasted_iota(jnp.int32, (n, n), 0)
    c = lax.broadcasted_iota(jnp.int32, (n, n), 1)
    return ((r <= c) if upper else (r >= c)).astype(MXU_DTYPE)


HEAD_ROWS = 16


def _rows_to_cols(v):
    return jnp.concatenate([v, jnp.zeros((128 - HEAD_ROWS, 128), F32)], axis=0).T


BIAS_PARTS = 3


def _bias_placement():
    place_q = np.zeros((BIAS_PARTS, 128, ATTN_W), np.float32)
    place_k = np.zeros((BIAS_PARTS, 128, ATTN_W), np.float32)
    ones_q = np.zeros((1, ATTN_W), np.float32)
    ones_k = np.zeros((1, ATTN_W), np.float32)
    for h in range(N_HEADS):
        base = 2 * HEAD_DIM * (h // 2) + HEAD_DIM * (1 - h % 2)
        for part in range(BIAS_PARTS):
            place_q[part, h, base + part] = 1.0
            place_k[part, h, base + BIAS_PARTS + part] = -1.0
        ones_q[0, base + BIAS_PARTS:base + 2 * BIAS_PARTS] = 1.0
        ones_k[0, base:base + BIAS_PARTS] = 1.0
    return (jnp.asarray(place_q, MXU_DTYPE), jnp.asarray(place_k, MXU_DTYPE), jnp.asarray(ones_q), jnp.asarray(ones_k))


def _forget_fwd(z_t, b_col):
    s = z_t.shape[1]
    nb = s // 128

    def body(z_ref, b_ref, pq_ref, pk_ref, oq_ref, ok_ref, qa_ref, ka_ref, cc_ref):
        upper = _tri(128, True)

        carry = jnp.zeros((HEAD_ROWS, 1), F32)
        for n in range(nb):
            off = n * 128
            lf = jax.nn.log_sigmoid(z_ref[0:HEAD_ROWS, off:off + 128] + b_ref[...])
            cc_ref[off:off + 128, :] = _rows_to_cols(_dot_exact(lf, upper, 3) + carry)
            carry = carry + jnp.sum(lf, axis=1, keepdims=True)

        rb = min(s, 512)
        for off in range(0, s, rb):
            qa = jnp.broadcast_to(oq_ref[...], (rb, ATTN_W))
            ka = jnp.broadcast_to(ok_ref[...], (rb, ATTN_W))
            rem = cc_ref[off:off + rb, :]
            for part in range(BIAS_PARTS):
                piece = rem.astype(MXU_DTYPE)
                rem = rem - piece.astype(F32)
                qa = qa + _dot(piece, pq_ref[part])
                ka = ka + _dot(piece, pk_ref[part])
            qa_ref[off:off + rb, :] = qa.astype(MXU_DTYPE)
            ka_ref[off:off + rb, :] = ka.astype(MXU_DTYPE)

    sd = jax.ShapeDtypeStruct
    return pl.pallas_call(body, name="forget_fwd",
                          out_shape=[sd((s, ATTN_W), MXU_DTYPE), sd((s, ATTN_W), MXU_DTYPE)],
                          scratch_shapes=[pltpu.VMEM((s, 128), F32)],
                          compiler_params=_params())(z_t, b_col, *_bias_placement())


def _aligned(start, size):
    return pl.ds(start if isinstance(start, int) else pl.multiple_of(start, size), size)


def _pair_lanes(pp):
    return _aligned(pp * 2 * HEAD_DIM, 2 * HEAD_DIM)


def _head_rows(h):
    return _aligned(h * HEAD_DIM, HEAD_DIM)


def _only_head(block, hb):
    lane = lax.broadcasted_iota(jnp.int32, block.shape, 1)
    return jnp.where((lane >= HEAD_DIM) if hb else (lane < HEAD_DIM), block, jnp.zeros_like(block))


def _other_head(block, other, hb):
    lane = lax.broadcasted_iota(jnp.int32, block.shape, 1)
    return jnp.where((lane >= HEAD_DIM) if hb else (lane < HEAD_DIM), block, other)


def _attn_fwd(qs, k, v_t, q_bias, k_bias, t, shards):
    s = qs.shape[0]
    n = s // t
    pairs = [(i, j) for i in range(n) for j in range(i + 1)]
    it = jnp.asarray(np.array([p[0] for p in pairs], np.int32))
    jt = jnp.asarray(np.array([p[1] for p in pairs], np.int32))
    nw = len(shards)
    last = len(pairs) - 1
    mid = (2 * len(pairs)) // 3

    def body(it_ref, jt_ref, q_ref, k_ref, vt_ref, qb_ref, kb_ref, *rest):
        sh, (o_ref, lse_ref, p_ref, pm_ref), got = rest[:nw], rest[nw:nw + 4], rest[nw + 4:2 * nw + 4]
        m_sc, l_sc, acc_sc, send, recv = rest[2 * nw + 4:]
        p = pl.program_id(0)
        i = it_ref[p]
        j = jt_ref[p]
        gather_start, gather_forward, gather_finish = _gather_stages(sh, got, send, recv)
        pl.when(p == 0)(gather_start)
        if mid < last:
            pl.when(p == mid)(gather_forward)

        @pl.when(j == 0)
        def _():
            m_sc[...] = jnp.full_like(m_sc, -1e30)
            l_sc[...] = jnp.ones_like(l_sc)
            acc_sc[...] = jnp.zeros_like(acc_sc)

        pm_ref[...] = jnp.zeros_like(pm_ref)

        def pair_step(pp, diagonal):
            lanes = _pair_lanes(pp)
            kp = k_ref[:, lanes]
            qp = q_ref[:, lanes]
            kb = kb_ref[:, lanes]
            qb = qb_ref[:, lanes]
            for hb in range(2):
                h = 2 * pp + hb
                row = pl.ds(h, 1)
                rows = _head_rows(h)
                st = _dot_nt(_other_head(kp, kb, hb), _other_head(qp, qb, hb))
                if diagonal:
                    kpos = lax.broadcasted_iota(jnp.int32, (t, t), 0)
                    qpos = lax.broadcasted_iota(jnp.int32, (t, t), 1)
                    st = jnp.where(kpos <= qpos, st, -1e30)
                m_prev = m_sc[row, :]
                m_new = jnp.maximum(m_prev, jnp.max(st, axis=0, keepdims=True))
                alpha = jnp.exp(m_prev - m_new)
                pt = jnp.exp(st - m_new)
                l_sc[row, :] = alpha * l_sc[row, :] + jnp.sum(pt, axis=0, keepdims=True)
                ptb = pt.astype(MXU_DTYPE)
                acc_sc[rows, :] = acc_sc[rows, :] * alpha + _dot(vt_ref[rows, :], ptb)
                m_sc[row, :] = m_new
                p_ref[0, h] = ptb
                pm_ref[0, row, :] = m_new

        @pl.when(j < i)
        def _():
            for pp in range(N_HEADS // 2):
                pair_step(pp, False)

        @pl.when(j == i)
        def _():
            for pp in range(N_HEADS // 2):
                pair_step(pp, True)
                sub = lax.broadcasted_iota(jnp.int32, (2 * HEAD_DIM, t), 0)
                l_pair = jnp.where(sub < HEAD_DIM, l_sc[pl.ds(2 * pp, 1), :], l_sc[pl.ds(2 * pp + 1, 1), :])
                o_t = acc_sc[_aligned(pp * 2 * HEAD_DIM, 2 * HEAD_DIM), :] / l_pair
                o_ref[:, _pair_lanes(pp)] = o_t.T.astype(MXU_DTYPE)

            lse_ref[...] = m_sc[...] + jnp.log(l_sc[...])

        @pl.when(p == last)
        def _():
            if mid >= last:
                gather_forward()
            gather_finish()

    gs = pltpu.PrefetchScalarGridSpec(
        num_scalar_prefetch=2, grid=(len(pairs),),
        in_specs=[pl.BlockSpec((t, ATTN_W), lambda p, it_, jt_: (it_[p], 0)),
                  pl.BlockSpec((t, ATTN_W), lambda p, it_, jt_: (jt_[p], 0)),
                  pl.BlockSpec((ATTN_W, t), lambda p, it_, jt_: (0, jt_[p])),
                  pl.BlockSpec((t, ATTN_W), lambda p, it_, jt_: (it_[p], 0)),
                  pl.BlockSpec((t, ATTN_W), lambda p, it_, jt_: (jt_[p], 0))] + [ANY] * nw,
        out_specs=[pl.BlockSpec((t, ATTN_W), lambda p, it_, jt_: (it_[p], 0)),
                   pl.BlockSpec((HEAD_ROWS, t), lambda p, it_, jt_: (0, it_[p])),
                   pl.BlockSpec((1, N_HEADS, t, t), lambda p, it_, jt_: (p, 0, 0, 0)),
                   pl.BlockSpec((1, HEAD_ROWS, t), lambda p, it_, jt_: (p, 0, 0))] + [ANY] * nw,
        scratch_shapes=[pltpu.VMEM((HEAD_ROWS, t), F32), pltpu.VMEM((HEAD_ROWS, t), F32), pltpu.VMEM((ATTN_W, t), F32),
                        pltpu.SemaphoreType.DMA((nw, 6)), pltpu.SemaphoreType.DMA((nw, 6))])
    sd = jax.ShapeDtypeStruct
    o, lse, probs, probs_max, *got = pl.pallas_call(
        body, name="attn_fwd", grid_spec=gs,
        out_shape=[sd((s, ATTN_W), MXU_DTYPE), sd((HEAD_ROWS, s), F32), sd((len(pairs), N_HEADS, t, t), MXU_DTYPE),
                   sd((len(pairs), HEAD_ROWS, t), F32)]
        + [sd((N_CHIPS,) + a.shape, a.dtype) for a in shards],
        compiler_params=_params(("arbitrary",)),
    )(it, jt, qs, k, v_t, q_bias, k_bias, *shards)
    me = 2 * lax.axis_index("x") + lax.axis_index("y")
    return o, lse, probs, probs_max, [lax.dynamic_update_index_in_dim(g, own, me, 0) for g, own in zip(got, shards)]


HALO = 16


def _halo_before(tm):
    return pl.BlockSpec((HALO, CONV_W), lambda i: (jnp.maximum(i * (tm // HALO) - 1, 0), 0))


def _shift_down(cur, prev_ref, first, cols=slice(None)):
    row = lax.broadcasted_iota(jnp.int32, cur.shape, 0)

    def before(r):
        prod = prev_ref[0][r:r + 1, cols].astype(F32) * prev_ref[1][r:r + 1, cols].astype(F32)
        return jnp.where(first, 0.0, prod)

    p7, p6 = before(HALO - 1), before(HALO - 2)
    s1 = jnp.where(row == 0, p7, pltpu.roll(cur, 1, 0))
    s2 = jnp.where(row == 0, p6, jnp.where(row == 1, p7, pltpu.roll(cur, 2, 0)))
    return s1, s2


def _group_ms(v, gmat):
    return _dot_exact(v, gmat, 1) * (1.0 / HEAD_DIM)


def _mixer_fwd(x, o_attn, gate_b, gate_c, u, conv_w, g_attn, g_conv, w_out, g_post, gmat, tm):
    s = x.shape[0]

    def body(x_ref, o_ref, b_ref, c_ref, u_ref, cp_ref, up_ref, cw_ref, ga_ref, gc_ref, wo_ref, gp_ref, gm_ref,
             x2_ref, mg_ref, y_ref, z_ref):
        i = pl.program_id(0)
        gm = gm_ref[0:128, 0:128]
        for lo in range(0, ATTN_W, 128):
            cols = slice(lo, lo + 128)
            cu = c_ref[:, cols].astype(F32) * u_ref[:, cols].astype(F32)
            cu1, cu2 = _shift_down(cu, (cp_ref, up_ref), i == 0, cols)
            z = cw_ref[0:1, cols] * cu2 + cw_ref[1:2, cols] * cu1 + cw_ref[2:3, cols] * cu
            z_ref[:, cols] = z.astype(MXU_DTYPE)
            cv = b_ref[:, cols].astype(F32) * z
            ov = o_ref[:, cols].astype(F32)
            mg_ref[:, cols] = ((ov * lax.rsqrt(_group_ms(ov * ov, gm) + EPS)) * ga_ref[:, cols]).astype(MXU_DTYPE)
            mg_ref[:, ATTN_W + lo:ATTN_W + lo + 128] = (
                (cv * lax.rsqrt(_group_ms(cv * cv, gm) + EPS)) * gc_ref[:, cols]).astype(MXU_DTYPE)
        y = _dot(mg_ref[...], wo_ref[...])
        y_ref[...] = y
        x2_ref[...] = x_ref[...] + (y * _rms(y)) * gp_ref[...]

    halo = _halo_before(tm)
    sd = jax.ShapeDtypeStruct
    return pl.pallas_call(
        body, name="mixer_fwd", grid=(s // tm,),
        in_specs=[_tok(tm, D_MODEL), _tok(tm, 512), _tok(tm, 512), _tok(tm, 512), _tok(tm, 512), halo, halo,
                  _whole((3, 512)), _whole((1, 512)), _whole((1, 512)), _whole((D_MODEL, D_MODEL), single=True),
                  _whole((1, D_MODEL)), _whole((512, 512))],
        out_specs=[_tok(tm, D_MODEL), _tok(tm, D_MODEL), _tok(tm, D_MODEL), _tok(tm, 512)],
        out_shape=[sd((s, D_MODEL), F32), sd((s, D_MODEL), MXU_DTYPE), sd((s, D_MODEL), F32), sd((s, 512), MXU_DTYPE)],
        compiler_params=_params(("arbitrary",)),
    )(x, o_attn, gate_b, gate_c, u, gate_c, u, conv_w, g_attn, g_conv, w_out, g_post, gmat)


def _ffn_fwd(x2, target, g_pre, w_gu, w_dn, g_post, tm):
    s = x2.shape[0]

    def body(x_ref, t_ref, gpre_ref, wgu_ref, wdn_ref, gpost_ref,
             h_ref, g_ref, up_ref, a_ref, ff_ref, dout_ref, loss_ref):
        xv = x_ref[...]
        h = ((xv * _rms(xv)) * gpre_ref[...]).astype(MXU_DTYPE)
        h_ref[...] = h
        ff = jnp.zeros((tm, D_MODEL), F32)
        for j in range(2):
            cols = slice(j * FF_PIECE, (j + 1) * FF_PIECE)
            g = _dot(h, wgu_ref[j])
            up = _dot(h, wgu_ref[2 + j])
            a = ((g * jax.nn.sigmoid(g)) * up).astype(MXU_DTYPE)
            g_ref[:, cols] = g.astype(MXU_DTYPE)
            up_ref[:, cols] = up.astype(MXU_DTYPE)
            a_ref[:, cols] = a
            ff = ff + _dot(a, wdn_ref[j])
        ff_ref[...] = ff
        err = (xv + (ff * _rms(ff)) * gpost_ref[...]) - t_ref[...]
        dout_ref[...] = err * (1.0 / D_MODEL)
        part = jnp.sum(jnp.mean(err * err, axis=-1, keepdims=True), axis=0, keepdims=True)

        @pl.when(pl.program_id(0) == 0)
        def _():
            loss_ref[...] = jnp.zeros_like(loss_ref)

        loss_ref[...] += part

    sd = jax.ShapeDtypeStruct
    return pl.pallas_call(
        body, name="ffn_fwd", grid=(s // tm,),
        in_specs=[_tok(tm, D_MODEL), _tok(tm, D_MODEL), _whole((1, D_MODEL)),
                  _whole((4, D_MODEL, FF_PIECE), single=True), _whole((2, FF_PIECE, D_MODEL), single=True),
                  _whole((1, D_MODEL))],
        out_specs=[_tok(tm, D_MODEL), _tok(tm, D_FF), _tok(tm, D_FF), _tok(tm, D_FF), _tok(tm, D_MODEL),
                   _tok(tm, D_MODEL), _whole((8, 128))],
        out_shape=[sd((s, D_MODEL), MXU_DTYPE), sd((s, D_FF), MXU_DTYPE), sd((s, D_FF), MXU_DTYPE),
                   sd((s, D_FF), MXU_DTYPE), sd((s, D_MODEL), F32), sd((s, D_MODEL), F32), sd((8, 128), F32)],
        compiler_params=_params(("arbitrary",)),
    )(x2, target, g_pre, w_gu, w_dn, g_post)


def _norm_bwd(dy, normed, rinv, gain):
    t = dy * gain
    return rinv * (t - normed * jnp.mean(t * normed, axis=-1, keepdims=True))


def _acc_rows(ref, first, val):
    @pl.when(first)
    def _():
        ref[...] = jnp.zeros_like(ref)

    ref[...] += jnp.sum(val, axis=0, keepdims=True)


def _ffn_bwd(dout, ff, x2, g, up, g_post, g_pre, w_gu, w_dn, tm):
    s = x2.shape[0]

    def body(do_ref, ff_ref, x_ref, g_ref, up_ref, gpost_ref, gpre_ref, wgu_ref, wdn_ref,
             dx_ref, dff_ref, dgu_ref, dgpost_ref, dgpre_ref):
        first = pl.program_id(0) == 0
        ffv = ff_ref[...]
        rf = _rms(ffv)
        n = ffv * rf
        do = do_ref[...]
        _acc_rows(dgpost_ref, first, do * n)
        dff = _norm_bwd(do, n, rf, gpost_ref[...]).astype(MXU_DTYPE)
        dff_ref[...] = dff
        dh = jnp.zeros((tm, D_MODEL), F32)
        for j in range(2):
            cols = slice(j * FF_PIECE, (j + 1) * FF_PIECE)
            da = _dot_nt(dff, wdn_ref[j])
            gv = g_ref[:, cols].astype(F32)
            sg = jax.nn.sigmoid(gv)
            dg = (da * up_ref[:, cols].astype(F32) * (sg * (1.0 + gv * (1.0 - sg)))).astype(MXU_DTYPE)
            du = (da * (gv * sg)).astype(MXU_DTYPE)
            dgu_ref[:, cols] = dg
            dgu_ref[:, D_FF + j * FF_PIECE:D_FF + (j + 1) * FF_PIECE] = du
            dh = dh + _dot_nt(dg, wgu_ref[j]) + _dot_nt(du, wgu_ref[2 + j])
        xv = x_ref[...]
        r2 = _rms(xv)
        nx = xv * r2
        _acc_rows(dgpre_ref, first, dh * nx)
        dx_ref[...] = do + _norm_bwd(dh, nx, r2, gpre_ref[...])

    sd = jax.ShapeDtypeStruct
    return pl.pallas_call(
        body, name="ffn_bwd", grid=(s // tm,),
        in_specs=[_tok(tm, D_MODEL), _tok(tm, D_MODEL), _tok(tm, D_MODEL), _tok(tm, D_FF), _tok(tm, D_FF),
                  _whole((1, D_MODEL)), _whole((1, D_MODEL)),
                  _whole((4, D_MODEL, FF_PIECE), single=True), _whole((2, FF_PIECE, D_MODEL), single=True)],
        out_specs=[_tok(tm, D_MODEL), _tok(tm, D_MODEL), _tok(tm, 2 * D_FF), _whole((1, D_MODEL)),
                   _whole((1, D_MODEL))],
        out_shape=[sd((s, D_MODEL), F32), sd((s, D_MODEL), MXU_DTYPE), sd((s, 2 * D_FF), MXU_DTYPE),
                   sd((1, D_MODEL), F32), sd((1, D_MODEL), F32)],
        compiler_params=_params(("arbitrary",)),
    )(dout, ff, x2, g, up, g_post, g_pre, w_gu, w_dn)


def _tn_matmul(a, b, tm, tn, tk, name, totals=(), sent=()):
    s, m = a.shape
    n = b.shape[1]
    nw, ns = len(totals), len(sent)
    grid = (m // tm, n // tn, s // tk)

    def body(a_ref, b_ref, *rest):
        o_ref = rest[nw + ns]
        if nw + ns:
            step = (pl.program_id(0) * grid[1] + pl.program_id(1)) * grid[2] + pl.program_id(2)
            outs = rest[nw + ns + 1:2 * (nw + ns) + 1]
            sems = rest[2 * (nw + ns) + 1:]
            stages = []
            if nw:
                stages.append(_share_stages(rest[:nw], outs[:nw], sems[0], sems[1]))
            if ns:
                stages.append(_pair_stages(rest[nw:nw + ns], ["sent"] * ns, outs[nw:], sems[-2], sems[-1]))

            @pl.when(step == 0)
            def _():
                for start, _ in stages:
                    start()

        @pl.when(pl.program_id(2) == 0)
        def _():
            o_ref[...] = jnp.zeros_like(o_ref)

        o_ref[...] += lax.dot_general(a_ref[...], b_ref[...], (((0,), (0,)), ((), ())), preferred_element_type=F32)
        if nw + ns:
            @pl.when(step == grid[0] * grid[1] * grid[2] - 1)
            def _():
                for _, finish in stages:
                    finish()

    sems = ([pltpu.SemaphoreType.DMA((nw,)), pltpu.SemaphoreType.DMA((nw,))] if nw else []) + (
        [pltpu.SemaphoreType.DMA((ns, N_CHIPS)), pltpu.SemaphoreType.DMA((ns, N_CHIPS))] if ns else [])
    out = pl.pallas_call(
        body, name=name, grid=grid,
        in_specs=[pl.BlockSpec((tk, tm), lambda i, j, kk: (kk, i)), pl.BlockSpec((tk, tn), lambda i, j, kk: (kk, j))]
        + [ANY] * (nw + ns),
        out_specs=[pl.BlockSpec((tm, tn), lambda i, j, kk: (i, j))] + [ANY] * (nw + ns),
        out_shape=[jax.ShapeDtypeStruct((m, n), F32)] + [jax.ShapeDtypeStruct(t.shape, t.dtype) for t in totals]
        + [jax.ShapeDtypeStruct(g.shape, g.dtype) for g in sent],
        scratch_shapes=sems,
        compiler_params=_params(("arbitrary", "arbitrary", "arbitrary")),
    )(a, b, *totals, *sent)
    return (out[0], out[1:1 + nw], out[1 + nw:]) if nw + ns else out[0]


def _mixer_bwd(dx2, y, o_attn, gate_b, z, g_post, g_attn, g_conv, w_out, gmat, sel, tm, ready, kinds):
    s = dx2.shape[0]
    nw = len(ready)
    nt = s // tm

    def body(d_ref, y_ref, o_ref, b_ref, z_ref, gp_ref, ga_ref, gc_ref, wo_ref, gm_ref, sel_ref, *rest):
        grads = rest[:nw]
        dy_ref, do_ref, db_ref, dz_ref, delta_ref, dgp_ref, dga_ref, dgc_ref = rest[nw:nw + 8]
        taken = rest[nw + 8:2 * nw + 8]
        send, recv = rest[2 * nw + 8:]
        first = pl.program_id(0) == 0
        pair_start, pair_finish = _pair_stages(grads, kinds, taken, send, recv)
        pl.when(first)(pair_start)
        yv = y_ref[...]
        ry = _rms(yv)
        ny = yv * ry
        d = d_ref[...]
        _acc_rows(dgp_ref, first, d * ny)
        dy = _norm_bwd(d, ny, ry, gp_ref[...]).astype(MXU_DTYPE)
        dy_ref[...] = dy
        dm = _dot_nt(dy, wo_ref[...])
        gm = gm_ref[0:128, 0:128]

        @pl.when(first)
        def _():
            dga_ref[...] = jnp.zeros_like(dga_ref)
            dgc_ref[...] = jnp.zeros_like(dgc_ref)

        def group_bwd(val, dmv, gain_ref, dg_ref, cols):
            rg = lax.rsqrt(_group_ms(val * val, gm) + EPS)
            nv = val * rg
            dg_ref[:, cols] += jnp.sum(dmv * nv, axis=0, keepdims=True)
            t = dmv * gain_ref[:, cols]
            return rg * (t - nv * _group_ms(t * nv, gm))

        delta = jnp.zeros((tm, 128), F32)
        for lo in range(0, ATTN_W, 128):
            cols = slice(lo, lo + 128)
            ov = o_ref[:, cols].astype(F32)
            d_o = group_bwd(ov, dm[:, cols], ga_ref, dga_ref, cols)
            do_ref[:, cols] = d_o.astype(MXU_DTYPE)
            delta = delta + _dot_exact(d_o * ov, sel_ref[cols, :], 2)
            zv = z_ref[:, cols].astype(F32)
            bv = b_ref[:, cols].astype(F32)
            d_cv = group_bwd(bv * zv, dm[:, ATTN_W + lo:ATTN_W + lo + 128], gc_ref, dgc_ref, cols)
            db_ref[:, cols] = (d_cv * zv).astype(MXU_DTYPE)
            dz_ref[:, cols] = d_cv * bv
        delta_ref[...] = delta.T[0:HEAD_ROWS, :]
        pl.when(pl.program_id(0) == nt - 1)(pair_finish)

    sd = jax.ShapeDtypeStruct
    taken_shape = [sd((N_CHIPS, g.shape[-2], g.shape[-1] if kd == "rows" else g.shape[-1] // N_CHIPS), F32)
                   for g, kd in zip(ready, kinds)]
    out = pl.pallas_call(
        body, name="mixer_bwd", grid=(nt,),
        in_specs=[_tok(tm, D_MODEL), _tok(tm, D_MODEL), _tok(tm, 512), _tok(tm, 512), _tok(tm, 512),
                  _whole((1, D_MODEL)), _whole((1, 512)), _whole((1, 512)),
                  _whole((D_MODEL, D_MODEL), single=True), _whole((512, 512)), _whole((512, 128))] + [ANY] * nw,
        out_specs=[_tok(tm, D_MODEL), _tok(tm, 512), _tok(tm, 512), _tok(tm, 512), _feat(HEAD_ROWS, tm),
                   _whole((1, D_MODEL)), _whole((1, 512)), _whole((1, 512))] + [ANY] * nw,
        out_shape=[sd((s, D_MODEL), MXU_DTYPE), sd((s, 512), MXU_DTYPE), sd((s, 512), MXU_DTYPE), sd((s, 512), F32),
                   sd((HEAD_ROWS, s), F32), sd((1, D_MODEL), F32), sd((1, 512), F32), sd((1, 512), F32)] + taken_shape,
        scratch_shapes=[pltpu.SemaphoreType.DMA((nw, N_CHIPS)), pltpu.SemaphoreType.DMA((nw, N_CHIPS))],
        compiler_params=_params(("arbitrary",)),
    )(dx2, y, o_attn, gate_b, z, g_post, g_attn, g_conv, w_out, gmat, sel, *ready)
    return out[:8], out[8:]


def _attn_bwd(qs, k_t, v, do, probs, probs_max, lse, delta, t, parts):
    s = qs.shape[0]
    n = s // t
    pairs = [(i, j) for j in range(n) for i in range(j, n)]
    it = jnp.asarray(np.array([p[0] for p in pairs], np.int32))
    jt = jnp.asarray(np.array([p[1] for p in pairs], np.int32))
    ft = jnp.asarray(np.array([p[0] * (p[0] + 1) // 2 + p[1] for p in pairs], np.int32))

    nw = len(parts)

    def body(it_ref, jt_ref, ft_ref, q_ref, kt_ref, v_ref, do_ref, p_ref, pm_ref, lse_ref, dl_ref, *rest):
        pb = rest[:nw]
        dq_ref, dk_ref, dv_ref, dc_ref, dcq_ref = rest[nw:nw + 5]
        rcv = rest[nw + 5:2 * nw + 5]
        dk_sc, dv_sc, dc_sc, send, recv = rest[2 * nw + 5:]
        p = pl.program_id(0)
        i = it_ref[p]
        j = jt_ref[p]
        chip_start, chip_finish = _chip_stages(pb, rcv, send, recv)

        @pl.when(p == 0)
        def _():
            chip_start()
            dq_ref[...] = jnp.zeros_like(dq_ref)
            dcq_ref[...] = jnp.zeros_like(dcq_ref)

        @pl.when(i == j)
        def _():
            dk_sc[...] = jnp.zeros_like(dk_sc)
            dv_sc[...] = jnp.zeros_like(dv_sc)
            dc_sc[...] = jnp.zeros_like(dc_sc)

        def pair_step(pp):
            lanes = _pair_lanes(pp)
            qp = q_ref[:, lanes]
            vp = v_ref[:, lanes]
            dop = do_ref[:, lanes]
            lane = lax.broadcasted_iota(jnp.int32, (t, 128), 1)
            for hb in range(2):
                h = 2 * pp + hb
                row = pl.ds(h, 1)
                pt = p_ref[0, h].astype(F32) * jnp.exp(pm_ref[0, row, :] - lse_ref[row, :])
                dv_sc[:, lanes] += _dot(pt.astype(MXU_DTYPE), _only_head(dop, hb))
                dst = pt * (_dot_nt(_only_head(vp, hb), dop) - dl_ref[row, :])
                dc_sc[...] -= jnp.where(lane == h, jnp.sum(dst, axis=1, keepdims=True), 0.0)
                dcq_ref[i, row, :] += jnp.sum(dst, axis=0, keepdims=True)
                dsb = dst.astype(MXU_DTYPE)
                dk_sc[:, lanes] += _dot(dsb, _only_head(qp, hb))
                rows = _head_rows(h)
                dq_ref[i, rows, :] += _dot(kt_ref[rows, :], dsb)

        for pp in range(N_HEADS // 2):
            pair_step(pp)

        @pl.when(i == n - 1)
        def _():
            dk_ref[...] = dk_sc[...].astype(MXU_DTYPE)
            dv_ref[...] = dv_sc[...].astype(MXU_DTYPE)
            dc_ref[...] = dc_sc[...]

        pl.when(p == len(pairs) - 1)(chip_finish)

    qi = lambda p, it_, jt_, ft_: (it_[p], 0)
    kj = lambda p, it_, jt_, ft_: (jt_[p], 0)
    row_i = lambda p, it_, jt_, ft_: (0, it_[p])
    gs = pltpu.PrefetchScalarGridSpec(
        num_scalar_prefetch=3, grid=(len(pairs),),
        in_specs=[pl.BlockSpec((t, ATTN_W), qi),
                  pl.BlockSpec((ATTN_W, t), lambda p, it_, jt_, ft_: (0, jt_[p])),
                  pl.BlockSpec((t, ATTN_W), kj), pl.BlockSpec((t, ATTN_W), qi),
                  pl.BlockSpec((1, N_HEADS, t, t), lambda p, it_, jt_, ft_: (ft_[p], 0, 0, 0)),
                  pl.BlockSpec((1, HEAD_ROWS, t), lambda p, it_, jt_, ft_: (ft_[p], 0, 0)),
                  pl.BlockSpec((HEAD_ROWS, t), row_i), pl.BlockSpec((HEAD_ROWS, t), row_i)] + [ANY] * nw,
        out_specs=[pl.BlockSpec((n, ATTN_W, t), lambda p, it_, jt_, ft_: (0, 0, 0)),
                   pl.BlockSpec((t, ATTN_W), kj), pl.BlockSpec((t, ATTN_W), kj),
                   pl.BlockSpec((t, 128), kj),
                   pl.BlockSpec((n, HEAD_ROWS, t), lambda p, it_, jt_, ft_: (0, 0, 0))] + [ANY] * nw,
        scratch_shapes=[pltpu.VMEM((t, ATTN_W), F32), pltpu.VMEM((t, ATTN_W), F32),
                        pltpu.VMEM((t, 128), F32), pltpu.SemaphoreType.DMA((nw, 3)), pltpu.SemaphoreType.DMA((nw, 3))])
    sd = jax.ShapeDtypeStruct
    out = pl.pallas_call(
        body, name="attn_bwd", grid_spec=gs,
        out_shape=[sd((n, ATTN_W, t), F32), sd((s, ATTN_W), MXU_DTYPE), sd((s, ATTN_W), MXU_DTYPE),
                   sd((s, 128), F32), sd((n, HEAD_ROWS, t), F32)] + [sd((3,) + a.shape[1:], a.dtype) for a in parts],
        compiler_params=_params(("arbitrary",)),
    )(it, jt, ft, qs, k_t, v, do, probs, probs_max, lse, delta, *parts)
    return out[:5], out[5:]


def _forget_bwd(dc_rows, dc_cols, z_t, b_col):
    s = z_t.shape[1]
    nb = s // 128

    def body(dr_ref, dcc_ref, z_ref, b_ref, dz_ref, db_ref):
        lower = _tri(128, False)
        real = lax.broadcasted_iota(jnp.int32, (HEAD_ROWS, 128), 0) < N_HEADS

        tail = jnp.zeros((HEAD_ROWS, 1), F32)
        dbias = jnp.zeros((HEAD_ROWS, 1), F32)
        for m in range(nb):
            off = (nb - 1 - m) * 128
            dc = dr_ref[:, off:off + 128] + dcc_ref[off:off + 128, :].T[0:HEAD_ROWS, :]
            dlf = _dot_exact(dc, lower, 3) + tail
            dz = dlf * jax.nn.sigmoid(-(z_ref[0:HEAD_ROWS, off:off + 128] + b_ref[...]))
            dz = jnp.where(real, dz, 0.0)
            dz_ref[off:off + 128, :] = _rows_to_cols(dz)
            tail = tail + jnp.sum(dc, axis=1, keepdims=True)
            dbias = dbias + jnp.sum(dz, axis=1, keepdims=True)
        db_ref[...] = jnp.broadcast_to(dbias, db_ref.shape)

    return pl.pallas_call(
        body, name="forget_bwd",
        out_shape=[jax.ShapeDtypeStruct((s, 128), F32), jax.ShapeDtypeStruct((HEAD_ROWS, 128), F32)],
        compiler_params=_params())(dc_rows, dc_cols, z_t, b_col)


def _inproj_bwd(dz, gate_c, u, conv_w, dq, dk, dv, dzf, db, x, dx2, g_pre, w_t, tm):
    s = x.shape[0]
    nt = s // tm
    t = dq.shape[2]
    assert t % tm == 0 and dq.shape[:2] == (s // t, ATTN_W)
    per = t // tm

    def body(dz_ref, dzn_ref, c_ref, u_ref, cp_ref, up_ref, cw_ref, dq_ref, dk_ref, dv_ref, dzf_ref, db_ref,
             x_ref, dx2_ref, g_ref, w_ref, gx_ref, dp_ref, dg_ref, dcw_ref):
        i = pl.program_id(0)
        first = i == 0
        last = i == nt - 1
        @pl.when(first)
        def _():
            dcw_ref[...] = jnp.zeros_like(dcw_ref)

        for lo in range(0, CONV_W, 128):
            cols = slice(lo, lo + 128)
            dzv = dz_ref[:, cols]
            row = lax.broadcasted_iota(jnp.int32, dzv.shape, 0)
            n0 = jnp.where(last, 0.0, dzn_ref[0:1, cols])
            n1 = jnp.where(last, 0.0, dzn_ref[1:2, cols])
            dz1 = jnp.where(row == tm - 1, n0, pltpu.roll(dzv, tm - 1, 0))
            dz2 = jnp.where(row == tm - 1, n1, jnp.where(row == tm - 2, n0, pltpu.roll(dzv, tm - 2, 0)))
            dcu = cw_ref[2:3, cols] * dzv + cw_ref[1:2, cols] * dz1 + cw_ref[0:1, cols] * dz2
            cv = c_ref[:, cols].astype(F32)
            uv = u_ref[:, cols].astype(F32)
            cu = cv * uv
            cu1, cu2 = _shift_down(cu, (cp_ref, up_ref), first, cols)
            dcw_ref[0:1, cols] += jnp.sum(dzv * cu2, axis=0, keepdims=True)
            dcw_ref[1:2, cols] += jnp.sum(dzv * cu1, axis=0, keepdims=True)
            dcw_ref[2:3, cols] += jnp.sum(dzv * cu, axis=0, keepdims=True)
            dp_ref[:, OFF_C + lo:OFF_C + lo + 128] = (dcu * uv).astype(MXU_DTYPE)
            dp_ref[:, OFF_U + lo:OFF_U + lo + 128] = (dcu * cv).astype(MXU_DTYPE)

        dp_ref[:, 0:512] = (dq_ref[0].T * Q_SCALE).astype(MXU_DTYPE)
        dp_ref[:, 512:1024] = dk_ref[...].astype(MXU_DTYPE)
        dp_ref[:, 1024:OFF_F] = dv_ref[...].astype(MXU_DTYPE)
        dp_ref[:, OFF_F:OFF_B] = dzf_ref[...].astype(MXU_DTYPE)
        dp_ref[:, OFF_B:OFF_C] = db_ref[...].astype(MXU_DTYPE)
        dh = _dot(dp_ref[...], w_ref[...])
        xv = x_ref[...]
        r1 = _rms(xv)
        nx = xv * r1
        _acc_rows(dg_ref, first, dh * nx)
        gx_ref[...] = dx2_ref[...] + _norm_bwd(dh, nx, r1, g_ref[...])

    prev = _halo_before(tm)
    nxt = pl.BlockSpec((8, 512), lambda i: (jnp.minimum((i + 1) * (tm // 8), s // 8 - 1), 0))
    sd = jax.ShapeDtypeStruct
    return pl.pallas_call(
        body, name="inproj_bwd", grid=(nt,),
        in_specs=[_tok(tm, 512), nxt, _tok(tm, 512), _tok(tm, 512), prev, prev, _whole((3, 512)),
                  pl.BlockSpec((1, ATTN_W, tm), lambda i: (i // per, 0, i % per)), _tok(tm, 512), _tok(tm, 512),
                  _tok(tm, 128),
                  _tok(tm, 512),
                  _tok(tm, D_MODEL), _tok(tm, D_MODEL), _whole((1, D_MODEL)), _whole((IN_PAD, D_MODEL), single=True)],
        out_specs=[_tok(tm, D_MODEL), _tok(tm, IN_PAD), _whole((1, D_MODEL)), _whole((8, 512))],
        out_shape=[sd((s, D_MODEL), F32), sd((s, IN_PAD), MXU_DTYPE), sd((1, D_MODEL), F32), sd((8, 512), F32)],
        compiler_params=_params(("arbitrary",)),
    )(dz, dz, gate_c, u, gate_c, u, conv_w, dq, dk, dv, dzf, db, x, dx2, g_pre, w_t)


def _tile(s, want):
    return want if s % want == 0 else s


def _halves(a):
    return a.reshape(2, a.shape[0] // 2, a.shape[1])


def _device_step(x, target, w, mom1, mom2, w_in_t, m_in_t, v_in_t, c_idx, me_idx):
    s = x.shape[0]
    tm = _tile(s, 512)
    tf = _tile(s, 256)
    ta = _tile(s, 512)
    tkk = _tile(s, 2048)
    gidx = np.arange(512) // HEAD_DIM
    gmat = jnp.asarray(gidx[:, None] == gidx[None, :], MXU_DTYPE)
    sel = jnp.asarray(gidx[:, None] == np.arange(128)[None, :], MXU_DTYPE)
    g_mix_pre, g_mix_post, g_ffn_pre, g_ffn_post = w["g_mix_pre"], w["g_mix_post"], w["g_ffn_pre"], w["g_ffn_post"]
    g_attn, g_conv, b_forget = w["g_attn_out"], w["g_conv_out"], w["b_forget"]
    shard = {n: _halves(w[n][0].astype(MXU_DTYPE)) for n in BIG[1:]}
    piece_rows = IN_W // N_CHIPS

    g_in, conv_all = _gather_weights([w_in_t.reshape(piece_rows, D_MODEL).astype(MXU_DTYPE)], w["conv_w"][0])
    w_rows = g_in.reshape(IN_W, D_MODEL)
    w_t = jnp.concatenate([w_rows[:OFF_F + N_HEADS], jnp.zeros((OFF_B - OFF_F - N_HEADS, D_MODEL), MXU_DTYPE),
                           w_rows[OFF_F + N_HEADS:]], axis=0)
    conv_w = jnp.transpose(conv_all, (1, 0, 2)).reshape(3, CONV_W)

    h1, qs, k, v, k_t, v_t, z_t, gate_b, gate_c, u = _inproj_fwd(x, g_mix_pre, w_t, tm)
    b_col = jnp.pad(jnp.transpose(b_forget), ((0, HEAD_ROWS - N_HEADS), (0, 0)))
    q_bias, k_bias = _forget_fwd(z_t, b_col)
    o_attn, lse, probs, probs_max, (g_out, g_gu, g_dn) = _attn_fwd(
        qs, k, v_t, q_bias, k_bias, ta, [shard["w_out"], shard["w_gate_up"], shard["w_down"]])
    w_out = g_out.reshape(D_MODEL, D_MODEL)
    w_gu = g_gu.reshape(N_CHIPS, D_MODEL, FF_PIECE)
    w_dn = g_dn.reshape(2, FF_PIECE, D_MODEL)
    x2, merged, y, z = _mixer_fwd(x, o_attn, gate_b, gate_c, u, conv_w, g_attn, g_conv, w_out, g_mix_post, gmat, tm)
    h2, g, up, a, ff, dout, loss_acc = _ffn_fwd(x2, target, g_ffn_pre, w_gu, w_dn, g_ffn_post, tf)

    dx2, dff, dgu, dg_ffn_post, dg_ffn_pre = _ffn_bwd(dout, ff, x2, g, up, g_ffn_post, g_ffn_pre, w_gu, w_dn, tf)
    dw_dn = _tn_matmul(a, dff, FF_PIECE, 1024, tkk, "dw_down").reshape(N_CHIPS, 2, D_FF // (2 * N_CHIPS), D_MODEL)
    dw_gu = _tn_matmul(h2, dgu, 1024, FF_PIECE, tkk, "dw_gate_up").reshape(2, D_MODEL // 2, 2 * D_FF)
    (dy, d_o, d_b, dz, delta, dg_mix_post, dg_attn, dg_conv), (a_gu, a_dn) = _mixer_bwd(
        dx2, y, o_attn, gate_b, z, g_mix_post, g_attn, g_conv, w_out, gmat, sel, tm, [dw_gu, dw_dn], ["cols", "rows"])
    dw_out = _tn_matmul(merged, dy, 1024, 1024, tkk, "dw_out").reshape(N_CHIPS, 2, D_MODEL // (2 * N_CHIPS), D_MODEL)
    place = jnp.concatenate([c_idx, me_idx])
    *sum_gu, a_out = _pair_sum(place, dw_gu, "cols", a_gu, "pair_sum_w_gate_up", [dw_out], ["rows"])
    sum_dn = _pair_sum(place, dw_dn, "rows", a_dn, "pair_sum_w_down")
    sum_out = _pair_sum(place, dw_out, "rows", a_out, "pair_sum_w_out")
    (dq_t, dk, dv, dc_cols, dcq), (r_gu, r_dn, r_out) = _attn_bwd(
        qs, k_t, v, d_o, probs, probs_max, lse, delta, ta, [sum_gu[1], sum_dn[1], sum_out[1]])
    dc_rows = jnp.transpose(dcq, (1, 0, 2)).reshape(HEAD_ROWS, s)
    dzf, db_f = _forget_bwd(dc_rows, dc_cols, z_t, b_col)
    grad_x, dproj, dg_mix_pre, dcw = _inproj_bwd(dz, gate_c, u, conv_w, dq_t, dk, dv, dzf, d_b,
                                                 x, dx2, g_mix_pre, w_t, tm)
    done = [_chip_sum(sb[0], r, "chip_sum_" + n)
            for n, sb, r in zip(BIG[1:], (sum_out, sum_gu, sum_dn), (r_out, r_gu, r_dn))]
    half_cols = D_MODEL // 2

    def dw_in_half(which, name, **riders):
        h1_half = lax.dynamic_slice_in_dim(h1, which * half_cols, half_cols, axis=1)
        res = _tn_matmul(dproj, h1_half, 640, half_cols, tkk, name, **riders)
        dw_t, rest = (res[0], res[1:]) if riders else (res, ())
        pieces = jnp.concatenate([dw_t[:OFF_F + N_HEADS], dw_t[OFF_B:]], axis=0)
        return (pieces.reshape(N_CHIPS, piece_rows, half_cols), *rest)

    (dw_in_theirs,) = dw_in_half(1 - place[0], "dw_in_sibling")
    dw_in, done_theirs, (a_in,) = dw_in_half(place[0], "dw_in", totals=done, sent=[dw_in_theirs])

    sum_in = _pair_sum(place, dw_in, "kept", a_in, "pair_sum_w_in")
    small = dict(b_forget=db_f[:N_HEADS, 0], g_attn_out=dg_attn, g_conv_out=dg_conv, g_mix_pre=dg_mix_pre,
                 g_mix_post=dg_mix_post, g_ffn_pre=dg_ffn_pre, g_ffn_post=dg_ffn_post)
    (r_in,), small_all = _chip_exchange([sum_in[1]], _pack_small(small, dcw[:3], loss_acc[0, 0]))
    t_in = _chip_sum(sum_in[0], r_in, "chip_sum_w_in")
    (s_in,) = _pair_share([t_in], "pair_share_w_in")
    new = {"w_in": _adamw_lanes(c_idx, w_in_t, t_in, s_in, m_in_t, v_in_t, "adamw_w_in")}
    for n, mine, theirs in zip(BIG[1:], done, done_theirs):
        new[n] = _adamw(c_idx, w[n][0], mine, theirs, mom1[n][0], mom2[n][0], 2, "adamw_" + n)
    return grad_x, new, small_all


BIG = ("w_in", "w_out", "w_gate_up", "w_down")
ANY = pl.BlockSpec(memory_space=pl.ANY)


def _place():
    x, y, c = lax.axis_index("x"), lax.axis_index("y"), lax.axis_index("c")
    others = [(1 - x, y), (x, 1 - y), (1 - x, 1 - y)]
    return x, y, c, 2 * x + y, others, [2 * px + py for px, py in others]


def _remote(src, dst, send, recv, dev):
    return pltpu.make_async_remote_copy(src_ref=src, dst_ref=dst, send_sem=send, recv_sem=recv,
                                        device_id=dev, device_id_type=MESH_ID)


def _gather_stages(sh, outs, send, recv):
    x, y, c, me, others, chips = _place()
    sib = (x, y, 1 - c)
    every = [(w, kk) for w in range(len(sh)) for kk in range(3)]

    def half_of(ref, half, piece=None):
        ref = ref if piece is None else ref.at[piece]
        if len(ref.shape) == 3:
            return ref.at[half]
        hc = ref.shape[1] // 2
        return ref.at[:, pl.ds(pl.multiple_of(half * hc, 128), hc)]

    def first(w, kk):
        return _remote(half_of(sh[w], c), half_of(outs[w], c, me), send.at[w, kk], recv.at[w, kk], (*others[kk], c))

    def landed(w, kk):
        r = half_of(outs[w], c, chips[kk])
        return _remote(r, r, send.at[w, kk], recv.at[w, kk], (*others[kk], c))

    def onward(w, kk, half):
        r = half_of(outs[w], half, chips[kk])
        return _remote(r, r, send.at[w, 3 + kk], recv.at[w, 3 + kk], sib)

    def start():
        for w, kk in every:
            first(w, kk).start()

    def forward():
        for w, kk in every:
            landed(w, kk).wait_recv()
            onward(w, kk, c).start()

    def finish():
        for w, kk in every:
            onward(w, kk, 1 - c).wait_recv()
        for w, kk in every:
            first(w, kk).wait_send()
            onward(w, kk, c).wait_send()

    return start, forward, finish


def _pair_piece(ref, kind, p, half):
    if kind == "sent":
        return ref.at[p]
    if kind == "rows":
        return ref.at[p, half]
    if kind == "lanes":
        hc = ref.shape[2] // 2
        return ref.at[p, :, pl.ds(pl.multiple_of(half * hc, 128), hc)]
    cols = ref.shape[2] // N_CHIPS
    return ref.at[half, :, pl.ds(p * cols, cols)]


def _pair_stages(g, kinds, a, send, recv):
    x, y, c, _, _, _ = _place()
    copies = [_remote(_pair_piece(g[w], kinds[w], p, 1 - c), a[w].at[p], send.at[w, p], recv.at[w, p], (x, y, 1 - c))
              for w in range(len(g)) for p in range(N_CHIPS)]

    def start():
        for cp in copies:
            cp.start()

    def finish():
        for cp in copies:
            cp.wait()

    return start, finish


def _chip_stages(pb, rcv, send, recv):
    x, y, c, _, others, chips = _place()
    copies = [_remote(pb[w].at[chips[kk]], rcv[w].at[kk], send.at[w, kk], recv.at[w, kk], (*others[kk], c))
              for w in range(len(pb)) for kk in range(3)]

    def start():
        for cp in copies:
            cp.start()

    def finish():
        for cp in copies:
            cp.wait()

    return start, finish


def _gather_weights(shards, conv_w):
    n = len(shards)

    def body(*refs):
        sh, cw, outs, cwo = refs[:n], refs[n], refs[n + 1:2 * n + 1], refs[2 * n + 1]
        send, recv = refs[2 * n + 2:]
        x, y, c, me, others, chips = _place()
        start, forward, finish = _gather_stages(sh, outs, send, recv)
        start()
        small = [_remote(cw, cwo.at[me], send.at[n, kk], recv.at[n, kk], (*others[kk], c)) for kk in range(3)]
        for cp in small:
            cp.start()
        forward()
        for kk in range(3):
            _remote(cw, cwo.at[chips[kk]], send.at[n, kk], recv.at[n, kk], (*others[kk], c)).wait_recv()
        finish()
        for cp in small:
            cp.wait_send()

    out_shape = [jax.ShapeDtypeStruct((N_CHIPS,) + s.shape, s.dtype) for s in shards]
    out_shape.append(jax.ShapeDtypeStruct((N_CHIPS,) + conv_w.shape, conv_w.dtype))
    got = pl.pallas_call(
        body, name="gather_weights", in_specs=[ANY] * (n + 1), out_specs=[ANY] * (n + 1), out_shape=out_shape,
        scratch_shapes=[pltpu.SemaphoreType.DMA((n + 1, 6)), pltpu.SemaphoreType.DMA((n + 1, 6))],
    )(*shards, conv_w)
    me = 2 * lax.axis_index("x") + lax.axis_index("y")
    return [lax.dynamic_update_index_in_dim(g, own, me, 0) for g, own in zip(got, list(shards) + [conv_w])]


def _taken_shape(g, kind):
    if kind == "rows":
        return (N_CHIPS,) + g.shape[2:]
    if kind == "lanes":
        return g.shape[:2] + (g.shape[2] // 2,)
    return (N_CHIPS, g.shape[1], g.shape[2] // N_CHIPS)


def _pair_sum(place, g, kind, a, name, ready=(), ready_kinds=()):
    _, half, cols = a.shape
    nw = len(ready)
    if kind == "rows":
        mine = pl.BlockSpec((1, 1, half, cols), lambda p, pr: (p, pr[0], 0, 0))
    elif kind == "lanes":
        mine = pl.BlockSpec((1, half, cols), lambda p, pr: (p, 0, pr[0]))
    elif kind == "kept":
        mine = pl.BlockSpec((1, half, cols), lambda p, pr: (p, 0, 0))
    else:
        mine = pl.BlockSpec((1, half, cols), lambda p, pr: (pr[0], 0, p))

    def body(place_ref, g_ref, a_ref, *rest):
        grads, (own_ref, pb_ref), taken = rest[:nw], rest[nw:nw + 2], rest[nw + 2:2 * nw + 2]
        if nw:
            pair_start, pair_finish = _pair_stages(grads, ready_kinds, taken, *rest[2 * nw + 2:])
            pl.when(pl.program_id(0) == 0)(pair_start)
        tot = (g_ref[0, 0] if kind == "rows" else g_ref[0]) + a_ref[0]
        pb_ref[0] = tot.astype(BF16)

        @pl.when(pl.program_id(0) == place_ref[1])
        def _():
            own_ref[...] = tot

        if nw:
            pl.when(pl.program_id(0) == N_CHIPS - 1)(pair_finish)

    sems = [pltpu.SemaphoreType.DMA((nw, N_CHIPS)), pltpu.SemaphoreType.DMA((nw, N_CHIPS))] if nw else []
    gs = pltpu.PrefetchScalarGridSpec(
        num_scalar_prefetch=1, grid=(N_CHIPS,),
        in_specs=[mine, pl.BlockSpec((1, half, cols), lambda p, pr: (p, 0, 0))] + [ANY] * nw,
        out_specs=[pl.BlockSpec((half, cols), lambda p, pr: (0, 0)),
                   pl.BlockSpec((1, half, cols), lambda p, pr: (p, 0, 0))] + [ANY] * nw,
        scratch_shapes=sems)
    out = pl.pallas_call(
        body, name=name, grid_spec=gs,
        out_shape=[jax.ShapeDtypeStruct((half, cols), F32), jax.ShapeDtypeStruct((N_CHIPS, half, cols), BF16)]
        + [jax.ShapeDtypeStruct(_taken_shape(r, kd), r.dtype) for r, kd in zip(ready, ready_kinds)],
        compiler_params=_params(("arbitrary",)),
    )(place, g, a, *ready)
    return list(out)


def _chip_sum(own, rcv, name):
    half, cols = own.shape

    def body(o_ref, r_ref, t_ref):
        t_ref[...] = ((o_ref[...] + r_ref[0].astype(F32)) + r_ref[1].astype(F32)) + r_ref[2].astype(F32)

    return pl.pallas_call(
        body, name=name, grid=(1,),
        in_specs=[pl.BlockSpec((half, cols), lambda i: (0, 0)), pl.BlockSpec((3, half, cols), lambda i: (0, 0, 0))],
        out_specs=pl.BlockSpec((half, cols), lambda i: (0, 0)),
        out_shape=jax.ShapeDtypeStruct((half, cols), F32), compiler_params=_params(("arbitrary",)),
    )(own, rcv)


def _small_stages(sm, smg, send, recv):
    x, y, c, _, _, _ = _place()

    def peer(r):
        return (1 - x if r & 4 else x, 1 - y if r & 2 else y, 1 - c if r & 1 else c)

    mine = 4 * x + 2 * y + c
    copies = [_remote(sm, smg.at[mine], send.at[r - 1], recv.at[r - 1], peer(r)) for r in range(1, 8)]

    def start():
        for cp in copies:
            cp.start()

    def finish():
        for r in range(1, 8):
            px, py, pc = peer(r)
            _remote(sm, smg.at[4 * px + 2 * py + pc], send.at[r - 1], recv.at[r - 1], (px, py, pc)).wait_recv()
        for cp in copies:
            cp.wait_send()

    return start, finish


def _chip_exchange(parts, small):
    n = len(parts)

    def body(*refs):
        pb, sm, rcv, smg = refs[:n], refs[n], refs[n + 1:2 * n + 1], refs[2 * n + 1]
        send, recv, ssend, srecv = refs[2 * n + 2:]
        chip_start, chip_finish = _chip_stages(pb, rcv, send, recv)
        small_start, small_finish = _small_stages(sm, smg, ssend, srecv)
        chip_start()
        small_start()
        chip_finish()
        small_finish()

    out_shape = [jax.ShapeDtypeStruct((3,) + p.shape[1:], p.dtype) for p in parts]
    out_shape.append(jax.ShapeDtypeStruct((8,) + small.shape, small.dtype))
    *arrived, small_land = pl.pallas_call(
        body, name="chip_exchange", in_specs=[ANY] * (n + 1), out_specs=[ANY] * (n + 1), out_shape=out_shape,
        scratch_shapes=[pltpu.SemaphoreType.DMA((n, 3)), pltpu.SemaphoreType.DMA((n, 3)),
                        pltpu.SemaphoreType.DMA((7,)), pltpu.SemaphoreType.DMA((7,))],
    )(*parts, small)
    mine = 4 * lax.axis_index("x") + 2 * lax.axis_index("y") + lax.axis_index("c")
    return arrived, lax.dynamic_update_index_in_dim(small_land, small, mine, 0)


def _share_stages(t, g, send, recv):
    x, y, c, _, _, _ = _place()
    copies = [_remote(t[w], g[w], send.at[w], recv.at[w], (x, y, 1 - c)) for w in range(len(t))]

    def start():
        for cp in copies:
            cp.start()

    def finish():
        for cp in copies:
            cp.wait()

    return start, finish


def _pair_share(totals, name):
    n = len(totals)

    def body(*refs):
        start, finish = _share_stages(refs[:n], refs[n:2 * n], *refs[2 * n:])
        start()
        finish()

    return pl.pallas_call(
        body, name=name, in_specs=[ANY] * n, out_specs=[ANY] * n,
        out_shape=[jax.ShapeDtypeStruct(t.shape, t.dtype) for t in totals],
        scratch_shapes=[pltpu.SemaphoreType.DMA((n,)), pltpu.SemaphoreType.DMA((n,))],
    )(*totals)


def _adamw_math(w, g, m, v):
    m = ADAM_B1 * m + (1.0 - ADAM_B1) * g
    v = ADAM_B2 * v + (1.0 - ADAM_B2) * (g * g)
    m_hat = m / (1.0 - ADAM_B1 ** ADAM_STEP)
    v_hat = v / (1.0 - ADAM_B2 ** ADAM_STEP)
    delta = -ADAM_LR * (m_hat / (jnp.sqrt(v_hat) + ADAM_EPS) + ADAM_WD * w)
    return delta, m, v


def _adamw(c_idx, w, mine, theirs, m, v, nb, name):
    rows, cols = w.shape
    tr = rows // (2 * nb)

    def body(c_ref, w_ref, a_ref, b_ref, m_ref, v_ref, g_ref, d_ref, nm_ref, nv_ref):
        g = jnp.where(pl.program_id(0) == c_ref[0], a_ref[...], b_ref[...])
        g_ref[...] = g
        d_ref[...], nm_ref[...], nv_ref[...] = _adamw_math(w_ref[...], g, m_ref[...], v_ref[...])

    full = pl.BlockSpec((tr, cols), lambda hh, i, cr: (hh * nb + i, 0))
    half = pl.BlockSpec((tr, cols), lambda hh, i, cr: (i, 0))
    gs = pltpu.PrefetchScalarGridSpec(num_scalar_prefetch=1, grid=(2, nb), in_specs=[full, half, half, full, full],
                                      out_specs=[full] * 4)
    return pl.pallas_call(
        body, name=name, grid_spec=gs, out_shape=[jax.ShapeDtypeStruct((rows, cols), F32)] * 4,
        compiler_params=_params(("arbitrary", "arbitrary")),
    )(c_idx, w, mine, theirs, m, v)


def _adamw_lanes(c_idx, w, mine, theirs, m, v, name):
    rows, _, cols = w.shape
    hc = cols // 2

    def body(c_ref, w_ref, a_ref, b_ref, m_ref, v_ref, g_ref, d_ref, nm_ref, nv_ref):
        g = jnp.where(pl.program_id(0) == c_ref[0], a_ref[...], b_ref[...])
        g_ref[:, 0, :] = g
        d_ref[:, 0, :], nm_ref[:, 0, :], nv_ref[:, 0, :] = _adamw_math(w_ref[:, 0, :], g, m_ref[:, 0, :], v_ref[:, 0, :])

    full = pl.BlockSpec((rows, 1, hc), lambda hh, cr: (0, 0, hh))
    half = pl.BlockSpec((rows, hc), lambda hh, cr: (0, 0))
    gs = pltpu.PrefetchScalarGridSpec(num_scalar_prefetch=1, grid=(2,), in_specs=[full, half, half, full, full],
                                      out_specs=[full] * 4)
    return pl.pallas_call(
        body, name=name, grid_spec=gs, out_shape=[jax.ShapeDtypeStruct((rows, 1, cols), F32)] * 4,
        compiler_params=_params(("arbitrary",)),
    )(c_idx, w, mine, theirs, m, v)


SMALL = ("g_mix_pre", "g_mix_post", "g_ffn_pre", "g_ffn_post")
SMALL_ALL = SMALL + ("g_attn_out", "g_conv_out", "conv_w", "b_forget")
SMALL_AT = {"g_mix_pre": (0, 0, 1024), "g_mix_post": (1, 0, 1024), "g_ffn_pre": (2, 0, 1024),
            "g_ffn_post": (3, 0, 1024), "g_attn_out": (4, 0, 512), "g_conv_out": (4, 512, 512),
            "b_forget": (7, 0, N_HEADS)}
CONV_AT = ((5, 0), (5, 512), (6, 0))
LOSS_AT = (6, 512)


def _pack_small(t, conv_full, loss_sum):
    conv = jnp.concatenate([conv_full.reshape(1, 3 * CONV_W), loss_sum.reshape(1, 1),
                            jnp.zeros((1, 2048 - 3 * CONV_W - 1), F32)], axis=1).reshape(2, 1024)
    return jnp.concatenate([t[n].reshape(1, 1024) for n in SMALL]
                           + [jnp.concatenate([t["g_attn_out"].reshape(1, 512), t["g_conv_out"].reshape(1, 512)], axis=1),
                              conv, jnp.pad(t["b_forget"].reshape(1, N_HEADS), ((0, 0), (0, 1024 - N_HEADS)))], axis=0)


def _small_update(me_idx, gathered, w, m, v):
    def body(me_ref, gg_ref, *refs):
        k = len(SMALL_ALL)
        w_refs, m_refs, v_refs = refs[:k], refs[k:2 * k], refs[2 * k:3 * k]
        loss_ref = refs[3 * k]
        outs = refs[3 * k + 1:3 * k + 1 + 4 * k]
        sums = refs[-1]
        g = gg_ref[0]
        for dev in range(1, 8):
            g = g + gg_ref[dev]
        sums[...] = g
        loss_ref[...] = sums[LOSS_AT[0]:LOSS_AT[0] + 1, LOSS_AT[1]:LOSS_AT[1] + 1]
        mine = pl.multiple_of(me_ref[0] * 128, 128)
        for idx, name in enumerate(SMALL_ALL):
            g_ref, d_ref, nm_ref, nv_ref = outs[4 * idx:4 * idx + 4]
            if name == "conv_w":
                for r, (row, lo) in enumerate(CONV_AT):
                    gr = sums[row:row + 1, pl.ds(lo + mine, 128)]
                    g_ref[0, r:r + 1, :] = gr
                    d_ref[0, r:r + 1, :], nm_ref[0, r:r + 1, :], nv_ref[0, r:r + 1, :] = _adamw_math(
                        w_refs[idx][0, r:r + 1, :], gr, m_refs[idx][0, r:r + 1, :], v_refs[idx][0, r:r + 1, :])
            else:
                row, lo, n = SMALL_AT[name]
                gr = sums[row:row + 1, lo:lo + n]
                g_ref[...] = gr
                d_ref[...], nm_ref[...], nv_ref[...] = _adamw_math(w_refs[idx][...], gr, m_refs[idx][...],
                                                                    v_refs[idx][...])

    def whole(a):
        nd = a.ndim
        return pl.BlockSpec(a.shape, lambda i, mr: (0,) * nd)

    ins = [t[n] for t in (w, m, v) for n in SMALL_ALL]
    out_shape = [jax.ShapeDtypeStruct((1, 1), F32)]
    for n in SMALL_ALL:
        out_shape += [jax.ShapeDtypeStruct(w[n].shape, F32)] * 4
    gs = pltpu.PrefetchScalarGridSpec(
        num_scalar_prefetch=1, grid=(1,), in_specs=[whole(gathered)] + [whole(a) for a in ins],
        out_specs=[whole(o) for o in out_shape], scratch_shapes=[pltpu.VMEM((8, 1024), F32)])
    out = pl.pallas_call(body, name="small_update", grid_spec=gs, out_shape=out_shape,
                         compiler_params=_params(("arbitrary",)))(me_idx, gathered, *ins)
    return out[0], {n: out[1 + 4 * i:5 + 4 * i] for i, n in enumerate(SMALL_ALL)}


def kernel(x, w_in, b_forget, conv_w, g_attn_out, g_conv_out, w_out, g_mix_pre, g_mix_post, w_gate_up, w_down, g_ffn_pre, g_ffn_post, loss_target, m_w_in, m_b_forget, m_conv_w, m_g_attn_out, m_g_conv_out, m_w_out, m_g_mix_pre, m_g_mix_post, m_w_gate_up, m_w_down, m_g_ffn_pre, m_g_ffn_post, v_w_in, v_b_forget, v_conv_w, v_g_attn_out, v_g_conv_out, v_w_out, v_g_mix_pre, v_g_mix_post, v_w_gate_up, v_w_down, v_g_ffn_pre, v_g_ffn_post):
    w = dict(w_in=w_in, b_forget=b_forget, conv_w=conv_w, g_attn_out=g_attn_out, g_conv_out=g_conv_out, w_out=w_out,
             g_mix_pre=g_mix_pre, g_mix_post=g_mix_post, w_gate_up=w_gate_up, w_down=w_down, g_ffn_pre=g_ffn_pre,
             g_ffn_post=g_ffn_post)
    m = dict(w_in=m_w_in, b_forget=m_b_forget, conv_w=m_conv_w, g_attn_out=m_g_attn_out, g_conv_out=m_g_conv_out,
             w_out=m_w_out, g_mix_pre=m_g_mix_pre, g_mix_post=m_g_mix_post, w_gate_up=m_w_gate_up, w_down=m_w_down,
             g_ffn_pre=m_g_ffn_pre, g_ffn_post=m_g_ffn_post)
    v = dict(w_in=v_w_in, b_forget=v_b_forget, conv_w=v_conv_w, g_attn_out=v_g_attn_out, g_conv_out=v_g_conv_out,
             w_out=v_w_out, g_mix_pre=v_g_mix_pre, g_mix_post=v_g_mix_post, w_gate_up=v_w_gate_up, w_down=v_w_down,
             g_ffn_pre=v_g_ffn_pre, g_ffn_post=v_g_ffn_post)
    cx, cy, cc = lax.axis_index("x"), lax.axis_index("y"), lax.axis_index("c")
    me = 2 * cx + cy
    c_idx = cc.astype(jnp.int32).reshape(1)
    me_idx = me.astype(jnp.int32).reshape(1)

    stored = lambda a: jnp.transpose(a, (2, 0, 1))
    grad_x, big, small_all = _device_step(x[0], loss_target[0], w, m, v, stored(w_in), stored(m_w_in),
                                          stored(v_w_in), c_idx, me_idx)
    gsum, delta, new_m, new_v = {}, {}, {}, {}
    for n in BIG:
        back = (lambda r: jnp.transpose(r, (1, 2, 0))) if n == "w_in" else (lambda r: r[None])
        gsum[n], delta[n], new_m[n], new_v[n] = [back(r) for r in big[n]]
    loss_sum, small_new = _small_update(me_idx, small_all, w, m, v)
    for n in SMALL_ALL:
        gsum[n], delta[n], new_m[n], new_v[n] = small_new[n]
    loss = 0.5 * loss_sum[0, 0]

    order = ("w_in", "b_forget", "conv_w", "g_attn_out", "g_conv_out", "w_out", "g_mix_pre", "g_mix_post",
             "w_gate_up", "w_down", "g_ffn_pre", "g_ffn_post")
    return (loss, grad_x[None], *[gsum[n] for n in order], *[delta[n] for n in order],
            *[new_m[n] for n in order], *[new_v[n] for n in order])
```

```python
import jax
import jax.numpy as jnp
import numpy as np
from jax import lax
from jax.experimental import pallas as pl
from jax.experimental.pallas import tpu as pltpu

F32 = jnp.float32
BF16 = jnp.bfloat16
MXU_DTYPE = jnp.bfloat16

D_MODEL = 1024
HEAD_DIM = 64
N_HEADS = 8
ATTN_W = 512
CONV_W = 512
D_FF = 2816
FF_PIECE = 1408
EPS = 1e-6
Q_SCALE = HEAD_DIM ** -0.5

OFF_F = 1536
OFF_B = 1664
OFF_C = 2176
OFF_U = 2688
IN_PAD = 3200
IN_W = 3080
N_CHIPS = 4

ADAM_LR = 0.001
ADAM_B1 = 0.9
ADAM_B2 = 0.999
ADAM_EPS = 1e-08
ADAM_WD = 0.01
ADAM_STEP = 10

VMEM_LIMIT_V7X = 56 * 1024 * 1024
MESH_ID = pl.DeviceIdType.MESH


def _params(sem=None, vmem=VMEM_LIMIT_V7X):
    kw = {"vmem_limit_bytes": vmem}
    if sem is not None:
        kw["dimension_semantics"] = sem
    return pltpu.CompilerParams(**kw)


def _dot(a, b):
    return jnp.dot(a, b, preferred_element_type=F32)


def _dot_nt(a, b):
    return lax.dot_general(a, b, (((1,), (1,)), ((), ())), preferred_element_type=F32)


def _dot_exact(x, ones, parts):
    if ones.dtype == F32:
        return _dot(x, ones)
    acc = None
    rem = x
    for _ in range(parts):
        piece = rem.astype(BF16)
        rem = rem - piece.astype(F32)
        term = _dot(piece, ones)
        acc = term if acc is None else acc + term
    return acc


def _rms(v):
    return lax.rsqrt(jnp.mean(v * v, axis=-1, keepdims=True) + EPS)


def _tok(tm, w):
    return pl.BlockSpec((tm, w), lambda i: (i, 0))


def _whole(shape, single=False):
    nd = len(shape)
    if single:
        return pl.BlockSpec(shape, lambda i: (0,) * nd, pipeline_mode=pl.Buffered(1))
    return pl.BlockSpec(shape, lambda i: (0,) * nd)


def _feat(rows, tm):
    return pl.BlockSpec((rows, tm), lambda i: (0, i))


def _padded_rows(pieces_ref, w_ref):
    rows = pieces_ref.shape[1]
    gap_at = OFF_F + N_HEADS
    for p in range(N_CHIPS):
        lo, hi = p * rows, (p + 1) * rows
        for a, b, shift in ((lo, min(hi, gap_at), 0), (max(lo, gap_at), hi, OFF_B - gap_at)):
            if a < b:
                w_ref[a + shift:b + shift, :] = pieces_ref[p, a - lo:b - lo, :]
    w_ref[gap_at:OFF_B, :] = jnp.zeros((OFF_B - gap_at, w_ref.shape[1]), w_ref.dtype)


def _inproj_fwd(x, g_pre, pieces, tm):
    s = x.shape[0]

    def body(x_ref, g_ref, pieces_ref, h_ref, q_ref, k_ref, v_ref, kt_ref, vt_ref, zt_ref, b_ref, c_ref, u_ref,
             w_ref):
        @pl.when(pl.program_id(0) == 0)
        def _():
            _padded_rows(pieces_ref, w_ref)

        xv = x_ref[...]
        h = ((xv * _rms(xv)) * g_ref[...]).astype(MXU_DTYPE)
        h_ref[...] = h

        def proj(lo, hi):
            return _dot_nt(h, w_ref[lo:hi, :])

        q_ref[...] = (proj(0, 512) * Q_SCALE).astype(MXU_DTYPE)
        kt = _dot_nt(w_ref[512:1024, :], h)
        vt = _dot_nt(w_ref[1024:OFF_F, :], h)
        kt_ref[...] = kt.astype(MXU_DTYPE)
        vt_ref[...] = vt.astype(MXU_DTYPE)
        k_ref[...] = kt.T.astype(MXU_DTYPE)
        v_ref[...] = vt.T.astype(MXU_DTYPE)
        zt_ref[...] = _dot_nt(w_ref[OFF_F:OFF_B, :], h)
        b_ref[...] = proj(OFF_B, OFF_C).astype(MXU_DTYPE)
        c_ref[...] = proj(OFF_C, OFF_U).astype(MXU_DTYPE)
        u_ref[...] = proj(OFF_U, IN_PAD).astype(MXU_DTYPE)

    sd = jax.ShapeDtypeStruct
    return pl.pallas_call(
        body, name="inproj_fwd", grid=(s // tm,),
        in_specs=[_tok(tm, D_MODEL), _whole((1, D_MODEL)), _whole(pieces.shape, single=True)],
        out_specs=[_tok(tm, D_MODEL), _tok(tm, 512), _tok(tm, 512), _tok(tm, 512), _feat(512, tm), _feat(512, tm),
                   _feat(128, tm), _tok(tm, 512), _tok(tm, 512), _tok(tm, 512), _whole((IN_PAD, D_MODEL))],
        out_shape=[sd((s, D_MODEL), MXU_DTYPE), sd((s, 512), MXU_DTYPE), sd((s, 512), MXU_DTYPE),
                   sd((s, 512), MXU_DTYPE), sd((512, s), MXU_DTYPE), sd((512, s), MXU_DTYPE), sd((128, s), F32),
                   sd((s, 512), MXU_DTYPE), sd((s, 512), MXU_DTYPE), sd((s, 512), MXU_DTYPE),
                   sd((IN_PAD, D_MODEL), MXU_DTYPE)],
        compiler_params=_params(("arbitrary",)),
    )(x, g_pre, pieces)


def _tri(n, upper):
    r = lax.broadcasted_iota(jnp.int32, (n, n), 0)
    c = lax.broadcasted_iota(jnp.int32, (n, n), 1)
    return ((r <= c) if upper else (r >= c)).astype(MXU_DTYPE)


HEAD_ROWS = 16


def _rows_to_cols(v):
    return jnp.concatenate([v, jnp.zeros((128 - HEAD_ROWS, 128), F32)], axis=0).T


BIAS_PARTS = 3


def _bias_placement():
    place_q = np.zeros((BIAS_PARTS, 128, ATTN_W), np.float32)
    place_k = np.zeros((BIAS_PARTS, 128, ATTN_W), np.float32)
    ones_q = np.zeros((1, ATTN_W), np.float32)
    ones_k = np.zeros((1, ATTN_W), np.float32)
    for h in range(N_HEADS):
        base = 2 * HEAD_DIM * (h // 2) + HEAD_DIM * (1 - h % 2)
        for part in range(BIAS_PARTS):
            place_q[part, h, base + part] = 1.0
            place_k[part, h, base + BIAS_PARTS + part] = -1.0
        ones_q[0, base + BIAS_PARTS:base + 2 * BIAS_PARTS] = 1.0
        ones_k[0, base:base + BIAS_PARTS] = 1.0
    return (jnp.asarray(place_q, MXU_DTYPE), jnp.asarray(place_k, MXU_DTYPE), jnp.asarray(ones_q), jnp.asarray(ones_k))


def _forget_fwd(z_t, b_col):
    s = z_t.shape[1]
    nb = s // 128

    def body(z_ref, b_ref, pq_ref, pk_ref, oq_ref, ok_ref, qa_ref, ka_ref, cc_ref):
        upper = _tri(128, True)

        carry = jnp.zeros((HEAD_ROWS, 1), F32)
        for n in range(nb):
            off = n * 128
            lf = jax.nn.log_sigmoid(z_ref[0:HEAD_ROWS, off:off + 128] + b_ref[...])
            cc_ref[off:off + 128, :] = _rows_to_cols(_dot_exact(lf, upper, 3) + carry)
            carry = carry + jnp.sum(lf, axis=1, keepdims=True)

        rb = min(s, 512)
        for off in range(0, s, rb):
            qa = jnp.broadcast_to(oq_ref[...], (rb, ATTN_W))
            ka = jnp.broadcast_to(ok_ref[...], (rb, ATTN_W))
            rem = cc_ref[off:off + rb, :]
            for part in range(BIAS_PARTS):
                piece = rem.astype(MXU_DTYPE)
                rem = rem - piece.astype(F32)
                qa = qa + _dot(piece, pq_ref[part])
                ka = ka + _dot(piece, pk_ref[part])
            qa_ref[off:off + rb, :] = qa.astype(MXU_DTYPE)
            ka_ref[off:off + rb, :] = ka.astype(MXU_DTYPE)

    sd = jax.ShapeDtypeStruct
    return pl.pallas_call(body, name="forget_fwd",
                          out_shape=[sd((s, ATTN_W), MXU_DTYPE), sd((s, ATTN_W), MXU_DTYPE)],
                          scratch_shapes=[pltpu.VMEM((s, 128), F32)],
                          compiler_params=_params())(z_t, b_col, *_bias_placement())


def _aligned(start, size):
    return pl.ds(start if isinstance(start, int) else pl.multiple_of(start, size), size)


def _pair_lanes(pp):
    return _aligned(pp * 2 * HEAD_DIM, 2 * HEAD_DIM)


def _head_rows(h):
    return _aligned(h * HEAD_DIM, HEAD_DIM)


def _only_head(block, hb):
    lane = lax.broadcasted_iota(jnp.int32, block.shape, 1)
    return jnp.where((lane >= HEAD_DIM) if hb else (lane < HEAD_DIM), block, jnp.zeros_like(block))


def _other_head(block, other, hb):
    lane = lax.broadcasted_iota(jnp.int32, block.shape, 1)
    return jnp.where((lane >= HEAD_DIM) if hb else (lane < HEAD_DIM), block, other)


def _attn_fwd(qs, k, v_t, q_bias, k_bias, t, shards):
    s = qs.shape[0]
    n = s // t
    pairs = [(i, j) for i in range(n) for j in range(i + 1)]
    it = jnp.asarray(np.array([p[0] for p in pairs], np.int32))
    jt = jnp.asarray(np.array([p[1] for p in pairs], np.int32))
    nw = len(shards)
    last = len(pairs) - 1
    mid = (2 * len(pairs)) // 3

    def body(it_ref, jt_ref, q_ref, k_ref, vt_ref, qb_ref, kb_ref, *rest):
        sh, (o_ref, lse_ref, p_ref, pm_ref), got = rest[:nw], rest[nw:nw + 4], rest[nw + 4:2 * nw + 4]
        m_sc, l_sc, acc_sc, send, recv = rest[2 * nw + 4:]
        p = pl.program_id(0)
        i = it_ref[p]
        j = jt_ref[p]
        gather_start, gather_forward, gather_finish = _gather_stages(sh, got, send, recv)
        pl.when(p == 0)(gather_start)
        if mid < last:
            pl.when(p == mid)(gather_forward)

        @pl.when(j == 0)
        def _():
            m_sc[...] = jnp.full_like(m_sc, -1e30)
            l_sc[...] = jnp.ones_like(l_sc)
            acc_sc[...] = jnp.zeros_like(acc_sc)

        pm_ref[...] = jnp.zeros_like(pm_ref)

        def pair_step(pp, diagonal):
            lanes = _pair_lanes(pp)
            kp = k_ref[:, lanes]
            qp = q_ref[:, lanes]
            kb = kb_ref[:, lanes]
            qb = qb_ref[:, lanes]
            for hb in range(2):
                h = 2 * pp + hb
                row = pl.ds(h, 1)
                rows = _head_rows(h)
                st = _dot_nt(_other_head(kp, kb, hb), _other_head(qp, qb, hb))
                if diagonal:
                    kpos = lax.broadcasted_iota(jnp.int32, (t, t), 0)
                    qpos = lax.broadcasted_iota(jnp.int32, (t, t), 1)
                    st = jnp.where(kpos <= qpos, st, -1e30)
                m_prev = m_sc[row, :]
                m_new = jnp.maximum(m_prev, jnp.max(st, axis=0, keepdims=True))
                alpha = jnp.exp(m_prev - m_new)
                pt = jnp.exp(st - m_new)
                l_sc[row, :] = alpha * l_sc[row, :] + jnp.sum(pt, axis=0, keepdims=True)
                ptb = pt.astype(MXU_DTYPE)
                acc_sc[rows, :] = acc_sc[rows, :] * alpha + _dot(vt_ref[rows, :], ptb)
                m_sc[row, :] = m_new
                p_ref[0, h] = ptb
                pm_ref[0, row, :] = m_new

        @pl.when(j < i)
        def _():
            for pp in range(N_HEADS // 2):
                pair_step(pp, False)

        @pl.when(j == i)
        def _():
            for pp in range(N_HEADS // 2):
                pair_step(pp, True)
                sub = lax.broadcasted_iota(jnp.int32, (2 * HEAD_DIM, t), 0)
                l_pair = jnp.where(sub < HEAD_DIM, l_sc[pl.ds(2 * pp, 1), :], l_sc[pl.ds(2 * pp + 1, 1), :])
                o_t = acc_sc[_aligned(pp * 2 * HEAD_DIM, 2 * HEAD_DIM), :] / l_pair
                o_ref[:, _pair_lanes(pp)] = o_t.T.astype(MXU_DTYPE)

            lse_ref[...] = m_sc[...] + jnp.log(l_sc[...])

        @pl.when(p == last)
        def _():
            if mid >= last:
                gather_forward()
            gather_finish()

    gs = pltpu.PrefetchScalarGridSpec(
        num_scalar_prefetch=2, grid=(len(pairs),),
        in_specs=[pl.BlockSpec((t, ATTN_W), lambda p, it_, jt_: (it_[p], 0)),
                  pl.BlockSpec((t, ATTN_W), lambda p, it_, jt_: (jt_[p], 0)),
                  pl.BlockSpec((ATTN_W, t), lambda p, it_, jt_: (0, jt_[p])),
                  pl.BlockSpec((t, ATTN_W), lambda p, it_, jt_: (it_[p], 0)),
                  pl.BlockSpec((t, ATTN_W), lambda p, it_, jt_: (jt_[p], 0))] + [ANY] * nw,
        out_specs=[pl.BlockSpec((t, ATTN_W), lambda p, it_, jt_: (it_[p], 0)),
                   pl.BlockSpec((HEAD_ROWS, t), lambda p, it_, jt_: (0, it_[p])),
                   pl.BlockSpec((1, N_HEADS, t, t), lambda p, it_, jt_: (p, 0, 0, 0)),
                   pl.BlockSpec((1, HEAD_ROWS, t), lambda p, it_, jt_: (p, 0, 0))] + [ANY] * nw,
        scratch_shapes=[pltpu.VMEM((HEAD_ROWS, t), F32), pltpu.VMEM((HEAD_ROWS, t), F32), pltpu.VMEM((ATTN_W, t), F32),
                        pltpu.SemaphoreType.DMA((nw, 6)), pltpu.SemaphoreType.DMA((nw, 6))])
    sd = jax.ShapeDtypeStruct
    o, lse, probs, probs_max, *got = pl.pallas_call(
        body, name="attn_fwd", grid_spec=gs,
        out_shape=[sd((s, ATTN_W), MXU_DTYPE), sd((HEAD_ROWS, s), F32), sd((len(pairs), N_HEADS, t, t), MXU_DTYPE),
                   sd((len(pairs), HEAD_ROWS, t), F32)]
        + [sd((N_CHIPS,) + a.shape, a.dtype) for a in shards],
        compiler_params=_params(("arbitrary",)),
    )(it, jt, qs, k, v_t, q_bias, k_bias, *shards)
    me = 2 * lax.axis_index("x") + lax.axis_index("y")
    return o, lse, probs, probs_max, [lax.dynamic_update_index_in_dim(g, own, me, 0) for g, own in zip(got, shards)]


HALO = 16


def _halo_before(tm):
    return pl.BlockSpec((HALO, CONV_W), lambda i: (jnp.maximum(i * (tm // HALO) - 1, 0), 0))


def _shift_down(cur, prev_ref, first, cols=slice(None)):
    row = lax.broadcasted_iota(jnp.int32, cur.shape, 0)

    def before(r):
        prod = prev_ref[0][r:r + 1, cols].astype(F32) * prev_ref[1][r:r + 1, cols].astype(F32)
        return jnp.where(first, 0.0, prod)

    p7, p6 = before(HALO - 1), before(HALO - 2)
    s1 = jnp.where(row == 0, p7, pltpu.roll(cur, 1, 0))
    s2 = jnp.where(row == 0, p6, jnp.where(row == 1, p7, pltpu.roll(cur, 2, 0)))
    return s1, s2


def _group_ms(v, gmat):
    return _dot_exact(v, gmat, 1) * (1.0 / HEAD_DIM)


def _mixer_fwd(x, o_attn, gate_b, gate_c, u, conv_w, g_attn, g_conv, w_out, g_post, gmat, tm):
    s = x.shape[0]

    def body(x_ref, o_ref, b_ref, c_ref, u_ref, cp_ref, up_ref, cw_ref, ga_ref, gc_ref, wo_ref, gp_ref, gm_ref,
             x2_ref, mg_ref, y_ref, z_ref):
        i = pl.program_id(0)
        gm = gm_ref[0:128, 0:128]
        for lo in range(0, ATTN_W, 128):
            cols = slice(lo, lo + 128)
            cu = c_ref[:, cols].astype(F32) * u_ref[:, cols].astype(F32)
            cu1, cu2 = _shift_down(cu, (cp_ref, up_ref), i == 0, cols)
            z = cw_ref[0:1, cols] * cu2 + cw_ref[1:2, cols] * cu1 + cw_ref[2:3, cols] * cu
            z_ref[:, cols] = z.astype(MXU_DTYPE)
            cv = b_ref[:, cols].astype(F32) * z
            ov = o_ref[:, cols].astype(F32)
            mg_ref[:, cols] = ((ov * lax.rsqrt(_group_ms(ov * ov, gm) + EPS)) * ga_ref[:, cols]).astype(MXU_DTYPE)
            mg_ref[:, ATTN_W + lo:ATTN_W + lo + 128] = (
                (cv * lax.rsqrt(_group_ms(cv * cv, gm) + EPS)) * gc_ref[:, cols]).astype(MXU_DTYPE)
        y = _dot(mg_ref[...], wo_ref[...])
        y_ref[...] = y
        x2_ref[...] = x_ref[...] + (y * _rms(y)) * gp_ref[...]

    halo = _halo_before(tm)
    sd = jax.ShapeDtypeStruct
    return pl.pallas_call(
        body, name="mixer_fwd", grid=(s // tm,),
        in_specs=[_tok(tm, D_MODEL), _tok(tm, 512), _tok(tm, 512), _tok(tm, 512), _tok(tm, 512), halo, halo,
                  _whole((3, 512)), _whole((1, 512)), _whole((1, 512)), _whole((D_MODEL, D_MODEL), single=True),
                  _whole((1, D_MODEL)), _whole((512, 512))],
        out_specs=[_tok(tm, D_MODEL), _tok(tm, D_MODEL), _tok(tm, D_MODEL), _tok(tm, 512)],
        out_shape=[sd((s, D_MODEL), F32), sd((s, D_MODEL), MXU_DTYPE), sd((s, D_MODEL), F32), sd((s, 512), MXU_DTYPE)],
        compiler_params=_params(("arbitrary",)),
    )(x, o_attn, gate_b, gate_c, u, gate_c, u, conv_w, g_attn, g_conv, w_out, g_post, gmat)


def _ffn_fwd(x2, target, g_pre, w_gu, w_dn, g_post, tm):
    s = x2.shape[0]

    def body(x_ref, t_ref, gpre_ref, wgu_ref, wdn_ref, gpost_ref,
             h_ref, g_ref, up_ref, a_ref, ff_ref, dout_ref, loss_ref):
        xv = x_ref[...]
        h = ((xv * _rms(xv)) * gpre_ref[...]).astype(MXU_DTYPE)
        h_ref[...] = h
        ff = jnp.zeros((tm, D_MODEL), F32)
        for j in range(2):
            cols = slice(j * FF_PIECE, (j + 1) * FF_PIECE)
            g = _dot(h, wgu_ref[j])
            up = _dot(h, wgu_ref[2 + j])
            a = ((g * jax.nn.sigmoid(g)) * up).astype(MXU_DTYPE)
            g_ref[:, cols] = g.astype(MXU_DTYPE)
            up_ref[:, cols] = up.astype(MXU_DTYPE)
            a_ref[:, cols] = a
            ff = ff + _dot(a, wdn_ref[j])
        ff_ref[...] = ff
        err = (xv + (ff * _rms(ff)) * gpost_ref[...]) - t_ref[...]
        dout_ref[...] = err * (1.0 / D_MODEL)
        part = jnp.sum(jnp.mean(err * err, axis=-1, keepdims=True), axis=0, keepdims=True)

        @pl.when(pl.program_id(0) == 0)
        def _():
            loss_ref[...] = jnp.zeros_like(loss_ref)

        loss_ref[...] += part

    sd = jax.ShapeDtypeStruct
    return pl.pallas_call(
        body, name="ffn_fwd", grid=(s // tm,),
        in_specs=[_tok(tm, D_MODEL), _tok(tm, D_MODEL), _whole((1, D_MODEL)),
                  _whole((4, D_MODEL, FF_PIECE), single=True), _whole((2, FF_PIECE, D_MODEL), single=True),
                  _whole((1, D_MODEL))],
        out_specs=[_tok(tm, D_MODEL), _tok(tm, D_FF), _tok(tm, D_FF), _tok(tm, D_FF), _tok(tm, D_MODEL),
                   _tok(tm, D_MODEL), _whole((8, 128))],
        out_shape=[sd((s, D_MODEL), MXU_DTYPE), sd((s, D_FF), MXU_DTYPE), sd((s, D_FF), MXU_DTYPE),
                   sd((s, D_FF), MXU_DTYPE), sd((s, D_MODEL), F32), sd((s, D_MODEL), F32), sd((8, 128), F32)],
        compiler_params=_params(("arbitrary",)),
    )(x2, target, g_pre, w_gu, w_dn, g_post)


def _norm_bwd(dy, normed, rinv, gain):
    t = dy * gain
    return rinv * (t - normed * jnp.mean(t * normed, axis=-1, keepdims=True))


def _acc_rows(ref, first, val):
    @pl.when(first)
    def _():
        ref[...] = jnp.zeros_like(ref)

    ref[...] += jnp.sum(val, axis=0, keepdims=True)


def _ffn_bwd(dout, ff, x2, g, up, g_post, g_pre, w_gu, w_dn, tm):
    s = x2.shape[0]

    def body(do_ref, ff_ref, x_ref, g_ref, up_ref, gpost_ref, gpre_ref, wgu_ref, wdn_ref,
             dx_ref, dff_ref, dgu_ref, dgpost_ref, dgpre_ref):
        first = pl.program_id(0) == 0
        ffv = ff_ref[...]
        rf = _rms(ffv)
        n = ffv * rf
        do = do_ref[...]
        _acc_rows(dgpost_ref, first, do * n)
        dff = _norm_bwd(do, n, rf, gpost_ref[...]).astype(MXU_DTYPE)
        dff_ref[...] = dff
        dh = jnp.zeros((tm, D_MODEL), F32)
        for j in range(2):
            cols = slice(j * FF_PIECE, (j + 1) * FF_PIECE)
            da = _dot_nt(dff, wdn_ref[j])
            gv = g_ref[:, cols].astype(F32)
            sg = jax.nn.sigmoid(gv)
            dg = (da * up_ref[:, cols].astype(F32) * (sg * (1.0 + gv * (1.0 - sg)))).astype(MXU_DTYPE)
            du = (da * (gv * sg)).astype(MXU_DTYPE)
            dgu_ref[:, cols] = dg
            dgu_ref[:, D_FF + j * FF_PIECE:D_FF + (j + 1) * FF_PIECE] = du
            dh = dh + _dot_nt(dg, wgu_ref[j]) + _dot_nt(du, wgu_ref[2 + j])
        xv = x_ref[...]
        r2 = _rms(xv)
        nx = xv * r2
        _acc_rows(dgpre_ref, first, dh * nx)
        dx_ref[...] = do + _norm_bwd(dh, nx, r2, gpre_ref[...])

    sd = jax.ShapeDtypeStruct
    return pl.pallas_call(
        body, name="ffn_bwd", grid=(s // tm,),
        in_specs=[_tok(tm, D_MODEL), _tok(tm, D_MODEL), _tok(tm, D_MODEL), _tok(tm, D_FF), _tok(tm, D_FF),
                  _whole((1, D_MODEL)), _whole((1, D_MODEL)),
                  _whole((4, D_MODEL, FF_PIECE), single=True), _whole((2, FF_PIECE, D_MODEL), single=True)],
        out_specs=[_tok(tm, D_MODEL), _tok(tm, D_MODEL), _tok(tm, 2 * D_FF), _whole((1, D_MODEL)),
                   _whole((1, D_MODEL))],
        out_shape=[sd((s, D_MODEL), F32), sd((s, D_MODEL), MXU_DTYPE), sd((s, 2 * D_FF), MXU_DTYPE),
                   sd((1, D_MODEL), F32), sd((1, D_MODEL), F32)],
        compiler_params=_params(("arbitrary",)),
    )(dout, ff, x2, g, up, g_post, g_pre, w_gu, w_dn)


def _tn_matmul(a, b, tm, tn, tk, name, totals=()):
    s, m = a.shape
    n = b.shape[1]
    nw = len(totals)
    grid = (m // tm, n // tn, s // tk)

    def body(a_ref, b_ref, *rest):
        o_ref = rest[nw]
        if nw:
            step = (pl.program_id(0) * grid[1] + pl.program_id(1)) * grid[2] + pl.program_id(2)
            share_start, share_finish = _share_stages(rest[:nw], rest[nw + 1:2 * nw + 1], *rest[2 * nw + 1:])
            pl.when(step == 0)(share_start)

        @pl.when(pl.program_id(2) == 0)
        def _():
            o_ref[...] = jnp.zeros_like(o_ref)

        o_ref[...] += lax.dot_general(a_ref[...], b_ref[...], (((0,), (0,)), ((), ())), preferred_element_type=F32)
        if nw:
            pl.when(step == grid[0] * grid[1] * grid[2] - 1)(share_finish)

    out = pl.pallas_call(
        body, name=name, grid=grid,
        in_specs=[pl.BlockSpec((tk, tm), lambda i, j, kk: (kk, i)), pl.BlockSpec((tk, tn), lambda i, j, kk: (kk, j))]
        + [ANY] * nw,
        out_specs=[pl.BlockSpec((tm, tn), lambda i, j, kk: (i, j))] + [ANY] * nw,
        out_shape=[jax.ShapeDtypeStruct((m, n), F32)] + [jax.ShapeDtypeStruct(t.shape, t.dtype) for t in totals],
        scratch_shapes=[pltpu.SemaphoreType.DMA((nw,)), pltpu.SemaphoreType.DMA((nw,))] if nw else [],
        compiler_params=_params(("arbitrary", "arbitrary", "arbitrary")),
    )(a, b, *totals)
    return (out[0], out[1:]) if nw else out[0]


def _mixer_bwd(dx2, y, o_attn, gate_b, z, g_post, g_attn, g_conv, w_out, gmat, sel, tm, ready, kinds):
    s = dx2.shape[0]
    nw = len(ready)
    nt = s // tm

    def body(d_ref, y_ref, o_ref, b_ref, z_ref, gp_ref, ga_ref, gc_ref, wo_ref, gm_ref, sel_ref, *rest):
        grads = rest[:nw]
        dy_ref, do_ref, db_ref, dz_ref, delta_ref, dgp_ref, dga_ref, dgc_ref = rest[nw:nw + 8]
        taken = rest[nw + 8:2 * nw + 8]
        send, recv = rest[2 * nw + 8:]
        first = pl.program_id(0) == 0
        pair_start, pair_finish = _pair_stages(grads, kinds, taken, send, recv)
        pl.when(first)(pair_start)
        yv = y_ref[...]
        ry = _rms(yv)
        ny = yv * ry
        d = d_ref[...]
        _acc_rows(dgp_ref, first, d * ny)
        dy = _norm_bwd(d, ny, ry, gp_ref[...]).astype(MXU_DTYPE)
        dy_ref[...] = dy
        dm = _dot_nt(dy, wo_ref[...])
        gm = gm_ref[0:128, 0:128]

        @pl.when(first)
        def _():
            dga_ref[...] = jnp.zeros_like(dga_ref)
            dgc_ref[...] = jnp.zeros_like(dgc_ref)

        def group_bwd(val, dmv, gain_ref, dg_ref, cols):
            rg = lax.rsqrt(_group_ms(val * val, gm) + EPS)
            nv = val * rg
            dg_ref[:, cols] += jnp.sum(dmv * nv, axis=0, keepdims=True)
            t = dmv * gain_ref[:, cols]
            return rg * (t - nv * _group_ms(t * nv, gm))

        delta = jnp.zeros((tm, 128), F32)
        for lo in range(0, ATTN_W, 128):
            cols = slice(lo, lo + 128)
            ov = o_ref[:, cols].astype(F32)
            d_o = group_bwd(ov, dm[:, cols], ga_ref, dga_ref, cols)
            do_ref[:, cols] = d_o.astype(MXU_DTYPE)
            delta = delta + _dot_exact(d_o * ov, sel_ref[cols, :], 2)
            zv = z_ref[:, cols].astype(F32)
            bv = b_ref[:, cols].astype(F32)
            d_cv = group_bwd(bv * zv, dm[:, ATTN_W + lo:ATTN_W + lo + 128], gc_ref, dgc_ref, cols)
            db_ref[:, cols] = (d_cv * zv).astype(MXU_DTYPE)
            dz_ref[:, cols] = d_cv * bv
        delta_ref[...] = delta.T[0:HEAD_ROWS, :]
        pl.when(pl.program_id(0) == nt - 1)(pair_finish)

    sd = jax.ShapeDtypeStruct
    taken_shape = [sd((N_CHIPS, g.shape[-2], g.shape[-1] if kd == "rows" else g.shape[-1] // N_CHIPS), F32)
                   for g, kd in zip(ready, kinds)]
    out = pl.pallas_call(
        body, name="mixer_bwd", grid=(nt,),
        in_specs=[_tok(tm, D_MODEL), _tok(tm, D_MODEL), _tok(tm, 512), _tok(tm, 512), _tok(tm, 512),
                  _whole((1, D_MODEL)), _whole((1, 512)), _whole((1, 512)),
                  _whole((D_MODEL, D_MODEL), single=True), _whole((512, 512)), _whole((512, 128))] + [ANY] * nw,
        out_specs=[_tok(tm, D_MODEL), _tok(tm, 512), _tok(tm, 512), _tok(tm, 512), _feat(HEAD_ROWS, tm),
                   _whole((1, D_MODEL)), _whole((1, 512)), _whole((1, 512))] + [ANY] * nw,
        out_shape=[sd((s, D_MODEL), MXU_DTYPE), sd((s, 512), MXU_DTYPE), sd((s, 512), MXU_DTYPE), sd((s, 512), F32),
                   sd((HEAD_ROWS, s), F32), sd((1, D_MODEL), F32), sd((1, 512), F32), sd((1, 512), F32)] + taken_shape,
        scratch_shapes=[pltpu.SemaphoreType.DMA((nw, N_CHIPS)), pltpu.SemaphoreType.DMA((nw, N_CHIPS))],
        compiler_params=_params(("arbitrary",)),
    )(dx2, y, o_attn, gate_b, z, g_post, g_attn, g_conv, w_out, gmat, sel, *ready)
    return out[:8], out[8:]


def _attn_bwd(qs, k_t, v, do, probs, probs_max, lse, delta, t, parts):
    s = qs.shape[0]
    n = s // t
    pairs = [(i, j) for j in range(n) for i in range(j, n)]
    it = jnp.asarray(np.array([p[0] for p in pairs], np.int32))
    jt = jnp.asarray(np.array([p[1] for p in pairs], np.int32))
    ft = jnp.asarray(np.array([p[0] * (p[0] + 1) // 2 + p[1] for p in pairs], np.int32))

    nw = len(parts)

    def body(it_ref, jt_ref, ft_ref, q_ref, kt_ref, v_ref, do_ref, p_ref, pm_ref, lse_ref, dl_ref, *rest):
        pb = rest[:nw]
        dq_ref, dk_ref, dv_ref, dc_ref, dcq_ref = rest[nw:nw + 5]
        rcv = rest[nw + 5:2 * nw + 5]
        dk_sc, dv_sc, dc_sc, send, recv = rest[2 * nw + 5:]
        p = pl.program_id(0)
        i = it_ref[p]
        j = jt_ref[p]
        chip_start, chip_finish = _chip_stages(pb, rcv, send, recv)

        @pl.when(p == 0)
        def _():
            chip_start()
            dq_ref[...] = jnp.zeros_like(dq_ref)
            dcq_ref[...] = jnp.zeros_like(dcq_ref)

        @pl.when(i == j)
        def _():
            dk_sc[...] = jnp.zeros_like(dk_sc)
            dv_sc[...] = jnp.zeros_like(dv_sc)
            dc_sc[...] = jnp.zeros_like(dc_sc)

        def pair_step(pp):
            lanes = _pair_lanes(pp)
            qp = q_ref[:, lanes]
            vp = v_ref[:, lanes]
            dop = do_ref[:, lanes]
            lane = lax.broadcasted_iota(jnp.int32, (t, 128), 1)
            for hb in range(2):
                h = 2 * pp + hb
                row = pl.ds(h, 1)
                pt = p_ref[0, h].astype(F32) * jnp.exp(pm_ref[0, row, :] - lse_ref[row, :])
                dv_sc[:, lanes] += _dot(pt.astype(MXU_DTYPE), _only_head(dop, hb))
                dst = pt * (_dot_nt(_only_head(vp, hb), dop) - dl_ref[row, :])
                dc_sc[...] -= jnp.where(lane == h, jnp.sum(dst, axis=1, keepdims=True), 0.0)
                dcq_ref[i, row, :] += jnp.sum(dst, axis=0, keepdims=True)
                dsb = dst.astype(MXU_DTYPE)
                dk_sc[:, lanes] += _dot(dsb, _only_head(qp, hb))
                rows = _head_rows(h)
                dq_ref[i, rows, :] += _dot(kt_ref[rows, :], dsb)

        for pp in range(N_HEADS // 2):
            pair_step(pp)

        @pl.when(i == n - 1)
        def _():
            dk_ref[...] = dk_sc[...].astype(MXU_DTYPE)
            dv_ref[...] = dv_sc[...].astype(MXU_DTYPE)
            dc_ref[...] = dc_sc[...]

        pl.when(p == len(pairs) - 1)(chip_finish)

    qi = lambda p, it_, jt_, ft_: (it_[p], 0)
    kj = lambda p, it_, jt_, ft_: (jt_[p], 0)
    row_i = lambda p, it_, jt_, ft_: (0, it_[p])
    gs = pltpu.PrefetchScalarGridSpec(
        num_scalar_prefetch=3, grid=(len(pairs),),
        in_specs=[pl.BlockSpec((t, ATTN_W), qi),
                  pl.BlockSpec((ATTN_W, t), lambda p, it_, jt_, ft_: (0, jt_[p])),
                  pl.BlockSpec((t, ATTN_W), kj), pl.BlockSpec((t, ATTN_W), qi),
                  pl.BlockSpec((1, N_HEADS, t, t), lambda p, it_, jt_, ft_: (ft_[p], 0, 0, 0)),
                  pl.BlockSpec((1, HEAD_ROWS, t), lambda p, it_, jt_, ft_: (ft_[p], 0, 0)),
                  pl.BlockSpec((HEAD_ROWS, t), row_i), pl.BlockSpec((HEAD_ROWS, t), row_i)] + [ANY] * nw,
        out_specs=[pl.BlockSpec((n, ATTN_W, t), lambda p, it_, jt_, ft_: (0, 0, 0)),
                   pl.BlockSpec((t, ATTN_W), kj), pl.BlockSpec((t, ATTN_W), kj),
                   pl.BlockSpec((t, 128), kj),
                   pl.BlockSpec((n, HEAD_ROWS, t), lambda p, it_, jt_, ft_: (0, 0, 0))] + [ANY] * nw,
        scratch_shapes=[pltpu.VMEM((t, ATTN_W), F32), pltpu.VMEM((t, ATTN_W), F32),
                        pltpu.VMEM((t, 128), F32), pltpu.SemaphoreType.DMA((nw, 3)), pltpu.SemaphoreType.DMA((nw, 3))])
    sd = jax.ShapeDtypeStruct
    out = pl.pallas_call(
        body, name="attn_bwd", grid_spec=gs,
        out_shape=[sd((n, ATTN_W, t), F32), sd((s, ATTN_W), MXU_DTYPE), sd((s, ATTN_W), MXU_DTYPE),
                   sd((s, 128), F32), sd((n, HEAD_ROWS, t), F32)] + [sd((3,) + a.shape[1:], a.dtype) for a in parts],
        compiler_params=_params(("arbitrary",)),
    )(it, jt, ft, qs, k_t, v, do, probs, probs_max, lse, delta, *parts)
    return out[:5], out[5:]


def _forget_bwd(dc_rows, dc_cols, z_t, b_col):
    s = z_t.shape[1]
    nb = s // 128

    def body(dr_ref, dcc_ref, z_ref, b_ref, dz_ref, db_ref):
        lower = _tri(128, False)
        real = lax.broadcasted_iota(jnp.int32, (HEAD_ROWS, 128), 0) < N_HEADS

        tail = jnp.zeros((HEAD_ROWS, 1), F32)
        dbias = jnp.zeros((HEAD_ROWS, 1), F32)
        for m in range(nb):
            off = (nb - 1 - m) * 128
            dc = dr_ref[:, off:off + 128] + dcc_ref[off:off + 128, :].T[0:HEAD_ROWS, :]
            dlf = _dot_exact(dc, lower, 3) + tail
            dz = dlf * jax.nn.sigmoid(-(z_ref[0:HEAD_ROWS, off:off + 128] + b_ref[...]))
            dz = jnp.where(real, dz, 0.0)
            dz_ref[off:off + 128, :] = _rows_to_cols(dz)
            tail = tail + jnp.sum(dc, axis=1, keepdims=True)
            dbias = dbias + jnp.sum(dz, axis=1, keepdims=True)
        db_ref[...] = jnp.broadcast_to(dbias, db_ref.shape)

    return pl.pallas_call(
        body, name="forget_bwd",
        out_shape=[jax.ShapeDtypeStruct((s, 128), F32), jax.ShapeDtypeStruct((HEAD_ROWS, 128), F32)],
        compiler_params=_params())(dc_rows, dc_cols, z_t, b_col)


def _inproj_bwd(dz, gate_c, u, conv_w, dq, dk, dv, dzf, db, x, dx2, g_pre, w_t, tm):
    s = x.shape[0]
    nt = s // tm
    t = dq.shape[2]
    assert t % tm == 0 and dq.shape[:2] == (s // t, ATTN_W)
    per = t // tm

    def body(dz_ref, dzn_ref, c_ref, u_ref, cp_ref, up_ref, cw_ref, dq_ref, dk_ref, dv_ref, dzf_ref, db_ref,
             x_ref, dx2_ref, g_ref, w_ref, gx_ref, dp_ref, dg_ref, dcw_ref):
        i = pl.program_id(0)
        first = i == 0
        last = i == nt - 1
        @pl.when(first)
        def _():
            dcw_ref[...] = jnp.zeros_like(dcw_ref)

        for lo in range(0, CONV_W, 128):
            cols = slice(lo, lo + 128)
            dzv = dz_ref[:, cols]
            row = lax.broadcasted_iota(jnp.int32, dzv.shape, 0)
            n0 = jnp.where(last, 0.0, dzn_ref[0:1, cols])
            n1 = jnp.where(last, 0.0, dzn_ref[1:2, cols])
            dz1 = jnp.where(row == tm - 1, n0, pltpu.roll(dzv, tm - 1, 0))
            dz2 = jnp.where(row == tm - 1, n1, jnp.where(row == tm - 2, n0, pltpu.roll(dzv, tm - 2, 0)))
            dcu = cw_ref[2:3, cols] * dzv + cw_ref[1:2, cols] * dz1 + cw_ref[0:1, cols] * dz2
            cv = c_ref[:, cols].astype(F32)
            uv = u_ref[:, cols].astype(F32)
            cu = cv * uv
            cu1, cu2 = _shift_down(cu, (cp_ref, up_ref), first, cols)
            dcw_ref[0:1, cols] += jnp.sum(dzv * cu2, axis=0, keepdims=True)
            dcw_ref[1:2, cols] += jnp.sum(dzv * cu1, axis=0, keepdims=True)
            dcw_ref[2:3, cols] += jnp.sum(dzv * cu, axis=0, keepdims=True)
            dp_ref[:, OFF_C + lo:OFF_C + lo + 128] = (dcu * uv).astype(MXU_DTYPE)
            dp_ref[:, OFF_U + lo:OFF_U + lo + 128] = (dcu * cv).astype(MXU_DTYPE)

        dp_ref[:, 0:512] = (dq_ref[0].T * Q_SCALE).astype(MXU_DTYPE)
        dp_ref[:, 512:1024] = dk_ref[...].astype(MXU_DTYPE)
        dp_ref[:, 1024:OFF_F] = dv_ref[...].astype(MXU_DTYPE)
        dp_ref[:, OFF_F:OFF_B] = dzf_ref[...].astype(MXU_DTYPE)
        dp_ref[:, OFF_B:OFF_C] = db_ref[...].astype(MXU_DTYPE)
        dh = _dot(dp_ref[...], w_ref[...])
        xv = x_ref[...]
        r1 = _rms(xv)
        nx = xv * r1
        _acc_rows(dg_ref, first, dh * nx)
        gx_ref[...] = dx2_ref[...] + _norm_bwd(dh, nx, r1, g_ref[...])

    prev = _halo_before(tm)
    nxt = pl.BlockSpec((8, 512), lambda i: (jnp.minimum((i + 1) * (tm // 8), s // 8 - 1), 0))
    sd = jax.ShapeDtypeStruct
    return pl.pallas_call(
        body, name="inproj_bwd", grid=(nt,),
        in_specs=[_tok(tm, 512), nxt, _tok(tm, 512), _tok(tm, 512), prev, prev, _whole((3, 512)),
                  pl.BlockSpec((1, ATTN_W, tm), lambda i: (i // per, 0, i % per)), _tok(tm, 512), _tok(tm, 512),
                  _tok(tm, 128),
                  _tok(tm, 512),
                  _tok(tm, D_MODEL), _tok(tm, D_MODEL), _whole((1, D_MODEL)), _whole((IN_PAD, D_MODEL), single=True)],
        out_specs=[_tok(tm, D_MODEL), _tok(tm, IN_PAD), _whole((1, D_MODEL)), _whole((8, 512))],
        out_shape=[sd((s, D_MODEL), F32), sd((s, IN_PAD), MXU_DTYPE), sd((1, D_MODEL), F32), sd((8, 512), F32)],
        compiler_params=_params(("arbitrary",)),
    )(dz, dz, gate_c, u, gate_c, u, conv_w, dq, dk, dv, dzf, db, x, dx2, g_pre, w_t)


def _tile(s, want):
    return want if s % want == 0 else s


def _halves(a):
    return a.reshape(2, a.shape[0] // 2, a.shape[1])


def _device_step(x, target, w, mom1, mom2, w_in_t, m_in_t, v_in_t, c_idx, me_idx):
    s = x.shape[0]
    tm = _tile(s, 512)
    tf = _tile(s, 256)
    ta = _tile(s, 512)
    tkk = _tile(s, 2048)
    gidx = np.arange(512) // HEAD_DIM
    gmat = jnp.asarray(gidx[:, None] == gidx[None, :], MXU_DTYPE)
    sel = jnp.asarray(gidx[:, None] == np.arange(128)[None, :], MXU_DTYPE)
    g_mix_pre, g_mix_post, g_ffn_pre, g_ffn_post = w["g_mix_pre"], w["g_mix_post"], w["g_ffn_pre"], w["g_ffn_post"]
    g_attn, g_conv, b_forget = w["g_attn_out"], w["g_conv_out"], w["b_forget"]
    shard = {n: _halves(w[n][0].astype(MXU_DTYPE)) for n in BIG[1:]}
    piece_rows = IN_W // N_CHIPS

    g_in, conv_all = _gather_weights([w_in_t.reshape(piece_rows, D_MODEL).astype(MXU_DTYPE)], w["conv_w"][0])
    conv_w = jnp.transpose(conv_all, (1, 0, 2)).reshape(3, CONV_W)

    h1, qs, k, v, k_t, v_t, z_t, gate_b, gate_c, u, w_t = _inproj_fwd(x, g_mix_pre, g_in, tm)
    b_col = jnp.pad(jnp.transpose(b_forget), ((0, HEAD_ROWS - N_HEADS), (0, 0)))
    q_bias, k_bias = _forget_fwd(z_t, b_col)
    o_attn, lse, probs, probs_max, (g_out, g_gu, g_dn) = _attn_fwd(
        qs, k, v_t, q_bias, k_bias, ta, [shard["w_out"], shard["w_gate_up"], shard["w_down"]])
    w_out = g_out.reshape(D_MODEL, D_MODEL)
    w_gu = g_gu.reshape(N_CHIPS, D_MODEL, FF_PIECE)
    w_dn = g_dn.reshape(2, FF_PIECE, D_MODEL)
    x2, merged, y, z = _mixer_fwd(x, o_attn, gate_b, gate_c, u, conv_w, g_attn, g_conv, w_out, g_mix_post, gmat, tm)
    h2, g, up, a, ff, dout, loss_acc = _ffn_fwd(x2, target, g_ffn_pre, w_gu, w_dn, g_ffn_post, tf)

    dx2, dff, dgu, dg_ffn_post, dg_ffn_pre = _ffn_bwd(dout, ff, x2, g, up, g_ffn_post, g_ffn_pre, w_gu, w_dn, tf)
    dw_dn = _tn_matmul(a, dff, FF_PIECE, 1024, tkk, "dw_down").reshape(N_CHIPS, 2, D_FF // (2 * N_CHIPS), D_MODEL)
    dw_gu = _tn_matmul(h2, dgu, 1024, FF_PIECE, tkk, "dw_gate_up").reshape(2, D_MODEL // 2, 2 * D_FF)
    (dy, d_o, d_b, dz, delta, dg_mix_post, dg_attn, dg_conv), (a_gu, a_dn) = _mixer_bwd(
        dx2, y, o_attn, gate_b, z, g_mix_post, g_attn, g_conv, w_out, gmat, sel, tm, [dw_gu, dw_dn], ["cols", "rows"])
    dw_out = _tn_matmul(merged, dy, 1024, 1024, tkk, "dw_out").reshape(N_CHIPS, 2, D_MODEL // (2 * N_CHIPS), D_MODEL)
    place = jnp.concatenate([c_idx, me_idx])
    *sum_gu, a_out = _pair_sum(place, dw_gu, "cols", a_gu, "pair_sum_w_gate_up", [dw_out], ["rows"])
    sum_dn = _pair_sum(place, dw_dn, "rows", a_dn, "pair_sum_w_down")
    sum_out = _pair_sum(place, dw_out, "rows", a_out, "pair_sum_w_out")
    (dq_t, dk, dv, dc_cols, dcq), (r_gu, r_dn, r_out) = _attn_bwd(
        qs, k_t, v, d_o, probs, probs_max, lse, delta, ta, [sum_gu[1], sum_dn[1], sum_out[1]])
    dc_rows = jnp.transpose(dcq, (1, 0, 2)).reshape(HEAD_ROWS, s)
    dzf, db_f = _forget_bwd(dc_rows, dc_cols, z_t, b_col)
    grad_x, dproj, dg_mix_pre, dcw = _inproj_bwd(dz, gate_c, u, conv_w, dq_t, dk, dv, dzf, d_b,
                                                 x, dx2, g_mix_pre, w_t, tm)
    done = [_chip_sum(sb[0], r, "chip_sum_" + n)
            for n, sb, r in zip(BIG[1:], (sum_out, sum_gu, sum_dn), (r_out, r_gu, r_dn))]
    dw_t, done_theirs = _tn_matmul(dproj, h1, 640, 1024, tkk, "dw_in", done)
    dw_in = jnp.concatenate([dw_t[:OFF_F + N_HEADS], dw_t[OFF_B:]], axis=0).reshape(N_CHIPS, piece_rows, D_MODEL)

    (a_in,) = _pair_exchange([dw_in], ["lanes"])
    sum_in = _pair_sum(place, dw_in, "lanes", a_in, "pair_sum_w_in")
    small = dict(b_forget=db_f[:N_HEADS, 0], g_attn_out=dg_attn, g_conv_out=dg_conv, g_mix_pre=dg_mix_pre,
                 g_mix_post=dg_mix_post, g_ffn_pre=dg_ffn_pre, g_ffn_post=dg_ffn_post)
    (r_in,), small_all = _chip_exchange([sum_in[1]], _pack_small(small, dcw[:3], loss_acc[0, 0]))
    t_in = _chip_sum(sum_in[0], r_in, "chip_sum_w_in")
    (s_in,) = _pair_share([t_in], "pair_share_w_in")
    new = {"w_in": _adamw_lanes(c_idx, w_in_t, t_in, s_in, m_in_t, v_in_t, "adamw_w_in")}
    for n, mine, theirs in zip(BIG[1:], done, done_theirs):
        new[n] = _adamw(c_idx, w[n][0], mine, theirs, mom1[n][0], mom2[n][0], 2, "adamw_" + n)
    return grad_x, new, small_all


BIG = ("w_in", "w_out", "w_gate_up", "w_down")
ANY = pl.BlockSpec(memory_space=pl.ANY)


def _place():
    x, y, c = lax.axis_index("x"), lax.axis_index("y"), lax.axis_index("c")
    others = [(1 - x, y), (x, 1 - y), (1 - x, 1 - y)]
    return x, y, c, 2 * x + y, others, [2 * px + py for px, py in others]


def _remote(src, dst, send, recv, dev):
    return pltpu.make_async_remote_copy(src_ref=src, dst_ref=dst, send_sem=send, recv_sem=recv,
                                        device_id=dev, device_id_type=MESH_ID)


def _gather_stages(sh, outs, send, recv):
    x, y, c, me, others, chips = _place()
    sib = (x, y, 1 - c)
    every = [(w, kk) for w in range(len(sh)) for kk in range(3)]

    def half_of(ref, half, piece=None):
        ref = ref if piece is None else ref.at[piece]
        if len(ref.shape) == 3:
            return ref.at[half]
        hc = ref.shape[1] // 2
        return ref.at[:, pl.ds(pl.multiple_of(half * hc, 128), hc)]

    def first(w, kk):
        return _remote(half_of(sh[w], c), half_of(outs[w], c, me), send.at[w, kk], recv.at[w, kk], (*others[kk], c))

    def landed(w, kk):
        r = half_of(outs[w], c, chips[kk])
        return _remote(r, r, send.at[w, kk], recv.at[w, kk], (*others[kk], c))

    def onward(w, kk, half):
        r = half_of(outs[w], half, chips[kk])
        return _remote(r, r, send.at[w, 3 + kk], recv.at[w, 3 + kk], sib)

    def start():
        for w, kk in every:
            first(w, kk).start()

    def forward():
        for w, kk in every:
            landed(w, kk).wait_recv()
            onward(w, kk, c).start()

    def finish():
        for w, kk in every:
            onward(w, kk, 1 - c).wait_recv()
        for w, kk in every:
            first(w, kk).wait_send()
            onward(w, kk, c).wait_send()

    return start, forward, finish


def _pair_piece(ref, kind, p, half):
    if kind == "rows":
        return ref.at[p, half]
    if kind == "lanes":
        hc = ref.shape[2] // 2
        return ref.at[p, :, pl.ds(pl.multiple_of(half * hc, 128), hc)]
    cols = ref.shape[2] // N_CHIPS
    return ref.at[half, :, pl.ds(p * cols, cols)]


def _pair_stages(g, kinds, a, send, recv):
    x, y, c, _, _, _ = _place()
    copies = [_remote(_pair_piece(g[w], kinds[w], p, 1 - c), a[w].at[p], send.at[w, p], recv.at[w, p], (x, y, 1 - c))
              for w in range(len(g)) for p in range(N_CHIPS)]

    def start():
        for cp in copies:
            cp.start()

    def finish():
        for cp in copies:
            cp.wait()

    return start, finish


def _chip_stages(pb, rcv, send, recv):
    x, y, c, _, others, chips = _place()
    copies = [_remote(pb[w].at[chips[kk]], rcv[w].at[kk], send.at[w, kk], recv.at[w, kk], (*others[kk], c))
              for w in range(len(pb)) for kk in range(3)]

    def start():
        for cp in copies:
            cp.start()

    def finish():
        for cp in copies:
            cp.wait()

    return start, finish


def _gather_weights(shards, conv_w):
    n = len(shards)

    def body(*refs):
        sh, cw, outs, cwo = refs[:n], refs[n], refs[n + 1:2 * n + 1], refs[2 * n + 1]
        send, recv = refs[2 * n + 2:]
        x, y, c, me, others, chips = _place()
        start, forward, finish = _gather_stages(sh, outs, send, recv)
        start()
        small = [_remote(cw, cwo.at[me], send.at[n, kk], recv.at[n, kk], (*others[kk], c)) for kk in range(3)]
        for cp in small:
            cp.start()
        forward()
        for kk in range(3):
            _remote(cw, cwo.at[chips[kk]], send.at[n, kk], recv.at[n, kk], (*others[kk], c)).wait_recv()
        finish()
        for cp in small:
            cp.wait_send()

    out_shape = [jax.ShapeDtypeStruct((N_CHIPS,) + s.shape, s.dtype) for s in shards]
    out_shape.append(jax.ShapeDtypeStruct((N_CHIPS,) + conv_w.shape, conv_w.dtype))
    got = pl.pallas_call(
        body, name="gather_weights", in_specs=[ANY] * (n + 1), out_specs=[ANY] * (n + 1), out_shape=out_shape,
        scratch_shapes=[pltpu.SemaphoreType.DMA((n + 1, 6)), pltpu.SemaphoreType.DMA((n + 1, 6))],
    )(*shards, conv_w)
    me = 2 * lax.axis_index("x") + lax.axis_index("y")
    return [lax.dynamic_update_index_in_dim(g, own, me, 0) for g, own in zip(got, list(shards) + [conv_w])]


def _taken_shape(g, kind):
    if kind == "rows":
        return (N_CHIPS,) + g.shape[2:]
    if kind == "lanes":
        return g.shape[:2] + (g.shape[2] // 2,)
    return (N_CHIPS, g.shape[1], g.shape[2] // N_CHIPS)


def _pair_sum(place, g, kind, a, name, ready=(), ready_kinds=()):
    _, half, cols = a.shape
    nw = len(ready)
    if kind == "rows":
        mine = pl.BlockSpec((1, 1, half, cols), lambda p, pr: (p, pr[0], 0, 0))
    elif kind == "lanes":
        mine = pl.BlockSpec((1, half, cols), lambda p, pr: (p, 0, pr[0]))
    else:
        mine = pl.BlockSpec((1, half, cols), lambda p, pr: (pr[0], 0, p))

    def body(place_ref, g_ref, a_ref, *rest):
        grads, (own_ref, pb_ref), taken = rest[:nw], rest[nw:nw + 2], rest[nw + 2:2 * nw + 2]
        if nw:
            pair_start, pair_finish = _pair_stages(grads, ready_kinds, taken, *rest[2 * nw + 2:])
            pl.when(pl.program_id(0) == 0)(pair_start)
        tot = (g_ref[0, 0] if kind == "rows" else g_ref[0]) + a_ref[0]
        pb_ref[0] = tot.astype(BF16)

        @pl.when(pl.program_id(0) == place_ref[1])
        def _():
            own_ref[...] = tot

        if nw:
            pl.when(pl.program_id(0) == N_CHIPS - 1)(pair_finish)

    sems = [pltpu.SemaphoreType.DMA((nw, N_CHIPS)), pltpu.SemaphoreType.DMA((nw, N_CHIPS))] if nw else []
    gs = pltpu.PrefetchScalarGridSpec(
        num_scalar_prefetch=1, grid=(N_CHIPS,),
        in_specs=[mine, pl.BlockSpec((1, half, cols), lambda p, pr: (p, 0, 0))] + [ANY] * nw,
        out_specs=[pl.BlockSpec((half, cols), lambda p, pr: (0, 0)),
                   pl.BlockSpec((1, half, cols), lambda p, pr: (p, 0, 0))] + [ANY] * nw,
        scratch_shapes=sems)
    out = pl.pallas_call(
        body, name=name, grid_spec=gs,
        out_shape=[jax.ShapeDtypeStruct((half, cols), F32), jax.ShapeDtypeStruct((N_CHIPS, half, cols), BF16)]
        + [jax.ShapeDtypeStruct(_taken_shape(r, kd), r.dtype) for r, kd in zip(ready, ready_kinds)],
        compiler_params=_params(("arbitrary",)),
    )(place, g, a, *ready)
    return list(out)


def _chip_sum(own, rcv, name):
    half, cols = own.shape

    def body(o_ref, r_ref, t_ref):
        t_ref[...] = ((o_ref[...] + r_ref[0].astype(F32)) + r_ref[1].astype(F32)) + r_ref[2].astype(F32)

    return pl.pallas_call(
        body, name=name, grid=(1,),
        in_specs=[pl.BlockSpec((half, cols), lambda i: (0, 0)), pl.BlockSpec((3, half, cols), lambda i: (0, 0, 0))],
        out_specs=pl.BlockSpec((half, cols), lambda i: (0, 0)),
        out_shape=jax.ShapeDtypeStruct((half, cols), F32), compiler_params=_params(("arbitrary",)),
    )(own, rcv)


def _small_stages(sm, smg, send, recv):
    x, y, c, _, _, _ = _place()

    def peer(r):
        return (1 - x if r & 4 else x, 1 - y if r & 2 else y, 1 - c if r & 1 else c)

    mine = 4 * x + 2 * y + c
    copies = [_remote(sm, smg.at[mine], send.at[r - 1], recv.at[r - 1], peer(r)) for r in range(1, 8)]

    def start():
        for cp in copies:
            cp.start()

    def finish():
        for r in range(1, 8):
            px, py, pc = peer(r)
            _remote(sm, smg.at[4 * px + 2 * py + pc], send.at[r - 1], recv.at[r - 1], (px, py, pc)).wait_recv()
        for cp in copies:
            cp.wait_send()

    return start, finish


def _pair_exchange(grads, kinds):
    n = len(grads)

    def body(*refs):
        start, finish = _pair_stages(refs[:n], kinds, refs[n:2 * n], *refs[2 * n:])
        start()
        finish()

    return pl.pallas_call(
        body, name="pair_exchange", in_specs=[ANY] * n, out_specs=[ANY] * n,
        out_shape=[jax.ShapeDtypeStruct(_taken_shape(g, kd), g.dtype) for g, kd in zip(grads, kinds)],
        scratch_shapes=[pltpu.SemaphoreType.DMA((n, N_CHIPS)), pltpu.SemaphoreType.DMA((n, N_CHIPS))],
    )(*grads)


def _chip_exchange(parts, small):
    n = len(parts)

    def body(*refs):
        pb, sm, rcv, smg = refs[:n], refs[n], refs[n + 1:2 * n + 1], refs[2 * n + 1]
        send, recv, ssend, srecv = refs[2 * n + 2:]
        chip_start, chip_finish = _chip_stages(pb, rcv, send, recv)
        small_start, small_finish = _small_stages(sm, smg, ssend, srecv)
        chip_start()
        small_start()
        chip_finish()
        small_finish()

    out_shape = [jax.ShapeDtypeStruct((3,) + p.shape[1:], p.dtype) for p in parts]
    out_shape.append(jax.ShapeDtypeStruct((8,) + small.shape, small.dtype))
    *arrived, small_land = pl.pallas_call(
        body, name="chip_exchange", in_specs=[ANY] * (n + 1), out_specs=[ANY] * (n + 1), out_shape=out_shape,
        scratch_shapes=[pltpu.SemaphoreType.DMA((n, 3)), pltpu.SemaphoreType.DMA((n, 3)),
                        pltpu.SemaphoreType.DMA((7,)), pltpu.SemaphoreType.DMA((7,))],
    )(*parts, small)
    mine = 4 * lax.axis_index("x") + 2 * lax.axis_index("y") + lax.axis_index("c")
    return arrived, lax.dynamic_update_index_in_dim(small_land, small, mine, 0)


def _share_stages(t, g, send, recv):
    x, y, c, _, _, _ = _place()
    copies = [_remote(t[w], g[w], send.at[w], recv.at[w], (x, y, 1 - c)) for w in range(len(t))]

    def start():
        for cp in copies:
            cp.start()

    def finish():
        for cp in copies:
            cp.wait()

    return start, finish


def _pair_share(totals, name):
    n = len(totals)

    def body(*refs):
        start, finish = _share_stages(refs[:n], refs[n:2 * n], *refs[2 * n:])
        start()
        finish()

    return pl.pallas_call(
        body, name=name, in_specs=[ANY] * n, out_specs=[ANY] * n,
        out_shape=[jax.ShapeDtypeStruct(t.shape, t.dtype) for t in totals],
        scratch_shapes=[pltpu.SemaphoreType.DMA((n,)), pltpu.SemaphoreType.DMA((n,))],
    )(*totals)


def _adamw_math(w, g, m, v):
    m = ADAM_B1 * m + (1.0 - ADAM_B1) * g
    v = ADAM_B2 * v + (1.0 - ADAM_B2) * (g * g)
    m_hat = m / (1.0 - ADAM_B1 ** ADAM_STEP)
    v_hat = v / (1.0 - ADAM_B2 ** ADAM_STEP)
    delta = -ADAM_LR * (m_hat / (jnp.sqrt(v_hat) + ADAM_EPS) + ADAM_WD * w)
    return delta, m, v


def _adamw(c_idx, w, mine, theirs, m, v, nb, name):
    rows, cols = w.shape
    tr = rows // (2 * nb)

    def body(c_ref, w_ref, a_ref, b_ref, m_ref, v_ref, g_ref, d_ref, nm_ref, nv_ref):
        g = jnp.where(pl.program_id(0) == c_ref[0], a_ref[...], b_ref[...])
        g_ref[...] = g
        d_ref[...], nm_ref[...], nv_ref[...] = _adamw_math(w_ref[...], g, m_ref[...], v_ref[...])

    full = pl.BlockSpec((tr, cols), lambda hh, i, cr: (hh * nb + i, 0))
    half = pl.BlockSpec((tr, cols), lambda hh, i, cr: (i, 0))
    gs = pltpu.PrefetchScalarGridSpec(num_scalar_prefetch=1, grid=(2, nb), in_specs=[full, half, half, full, full],
                                      out_specs=[full] * 4)
    return pl.pallas_call(
        body, name=name, grid_spec=gs, out_shape=[jax.ShapeDtypeStruct((rows, cols), F32)] * 4,
        compiler_params=_params(("arbitrary", "arbitrary")),
    )(c_idx, w, mine, theirs, m, v)


def _adamw_lanes(c_idx, w, mine, theirs, m, v, name):
    rows, _, cols = w.shape
    hc = cols // 2

    def body(c_ref, w_ref, a_ref, b_ref, m_ref, v_ref, g_ref, d_ref, nm_ref, nv_ref):
        g = jnp.where(pl.program_id(0) == c_ref[0], a_ref[...], b_ref[...])
        g_ref[:, 0, :] = g
        d_ref[:, 0, :], nm_ref[:, 0, :], nv_ref[:, 0, :] = _adamw_math(w_ref[:, 0, :], g, m_ref[:, 0, :], v_ref[:, 0, :])

    full = pl.BlockSpec((rows, 1, hc), lambda hh, cr: (0, 0, hh))
    half = pl.BlockSpec((rows, hc), lambda hh, cr: (0, 0))
    gs = pltpu.PrefetchScalarGridSpec(num_scalar_prefetch=1, grid=(2,), in_specs=[full, half, half, full, full],
                                      out_specs=[full] * 4)
    return pl.pallas_call(
        body, name=name, grid_spec=gs, out_shape=[jax.ShapeDtypeStruct((rows, 1, cols), F32)] * 4,
        compiler_params=_params(("arbitrary",)),
    )(c_idx, w, mine, theirs, m, v)


SMALL = ("g_mix_pre", "g_mix_post", "g_ffn_pre", "g_ffn_post")
SMALL_ALL = SMALL + ("g_attn_out", "g_conv_out", "conv_w", "b_forget")
SMALL_AT = {"g_mix_pre": (0, 0, 1024), "g_mix_post": (1, 0, 1024), "g_ffn_pre": (2, 0, 1024),
            "g_ffn_post": (3, 0, 1024), "g_attn_out": (4, 0, 512), "g_conv_out": (4, 512, 512),
            "b_forget": (7, 0, N_HEADS)}
CONV_AT = ((5, 0), (5, 512), (6, 0))
LOSS_AT = (6, 512)


def _pack_small(t, conv_full, loss_sum):
    conv = jnp.concatenate([conv_full.reshape(1, 3 * CONV_W), loss_sum.reshape(1, 1),
                            jnp.zeros((1, 2048 - 3 * CONV_W - 1), F32)], axis=1).reshape(2, 1024)
    return jnp.concatenate([t[n].reshape(1, 1024) for n in SMALL]
                           + [jnp.concatenate([t["g_attn_out"].reshape(1, 512), t["g_conv_out"].reshape(1, 512)], axis=1),
                              conv, jnp.pad(t["b_forget"].reshape(1, N_HEADS), ((0, 0), (0, 1024 - N_HEADS)))], axis=0)


def _small_update(me_idx, gathered, w, m, v):
    def body(me_ref, gg_ref, *refs):
        k = len(SMALL_ALL)
        w_refs, m_refs, v_refs = refs[:k], refs[k:2 * k], refs[2 * k:3 * k]
        loss_ref = refs[3 * k]
        outs = refs[3 * k + 1:3 * k + 1 + 4 * k]
        sums = refs[-1]
        g = gg_ref[0]
        for dev in range(1, 8):
            g = g + gg_ref[dev]
        sums[...] = g
        loss_ref[...] = sums[LOSS_AT[0]:LOSS_AT[0] + 1, LOSS_AT[1]:LOSS_AT[1] + 1]
        mine = pl.multiple_of(me_ref[0] * 128, 128)
        for idx, name in enumerate(SMALL_ALL):
            g_ref, d_ref, nm_ref, nv_ref = outs[4 * idx:4 * idx + 4]
            if name == "conv_w":
                for r, (row, lo) in enumerate(CONV_AT):
                    gr = sums[row:row + 1, pl.ds(lo + mine, 128)]
                    g_ref[0, r:r + 1, :] = gr
                    d_ref[0, r:r + 1, :], nm_ref[0, r:r + 1, :], nv_ref[0, r:r + 1, :] = _adamw_math(
                        w_refs[idx][0, r:r + 1, :], gr, m_refs[idx][0, r:r + 1, :], v_refs[idx][0, r:r + 1, :])
            else:
                row, lo, n = SMALL_AT[name]
                gr = sums[row:row + 1, lo:lo + n]
                g_ref[...] = gr
                d_ref[...], nm_ref[...], nv_ref[...] = _adamw_math(w_refs[idx][...], gr, m_refs[idx][...],
                                                                    v_refs[idx][...])

    def whole(a):
        nd = a.ndim
        return pl.BlockSpec(a.shape, lambda i, mr: (0,) * nd)

    ins = [t[n] for t in (w, m, v) for n in SMALL_ALL]
    out_shape = [jax.ShapeDtypeStruct((1, 1), F32)]
    for n in SMALL_ALL:
        out_shape += [jax.ShapeDtypeStruct(w[n].shape, F32)] * 4
    gs = pltpu.PrefetchScalarGridSpec(
        num_scalar_prefetch=1, grid=(1,), in_specs=[whole(gathered)] + [whole(a) for a in ins],
        out_specs=[whole(o) for o in out_shape], scratch_shapes=[pltpu.VMEM((8, 1024), F32)])
    out = pl.pallas_call(body, name="small_update", grid_spec=gs, out_shape=out_shape,
                         compiler_params=_params(("arbitrary",)))(me_idx, gathered, *ins)
    return out[0], {n: out[1 + 4 * i:5 + 4 * i] for i, n in enumerate(SMALL_ALL)}


def kernel(x, w_in, b_forget, conv_w, g_attn_out, g_conv_out, w_out, g_mix_pre, g_mix_post, w_gate_up, w_down, g_ffn_pre, g_ffn_post, loss_target, m_w_in, m_b_forget, m_conv_w, m_g_attn_out, m_g_conv_out, m_w_out, m_g_mix_pre, m_g_mix_post, m_w_gate_up, m_w_down, m_g_ffn_pre, m_g_ffn_post, v_w_in, v_b_forget, v_conv_w, v_g_attn_out, v_g_conv_out, v_w_out, v_g_mix_pre, v_g_mix_post, v_w_gate_up, v_w_down, v_g_ffn_pre, v_g_ffn_post):
    w = dict(w_in=w_in, b_forget=b_forget, conv_w=conv_w, g_attn_out=g_attn_out, g_conv_out=g_conv_out, w_out=w_out,
             g_mix_pre=g_mix_pre, g_mix_post=g_mix_post, w_gate_up=w_gate_up, w_down=w_down, g_ffn_pre=g_ffn_pre,
             g_ffn_post=g_ffn_post)
    m = dict(w_in=m_w_in, b_forget=m_b_forget, conv_w=m_conv_w, g_attn_out=m_g_attn_out, g_conv_out=m_g_conv_out,
             w_out=m_w_out, g_mix_pre=m_g_mix_pre, g_mix_post=m_g_mix_post, w_gate_up=m_w_gate_up, w_down=m_w_down,
             g_ffn_pre=m_g_ffn_pre, g_ffn_post=m_g_ffn_post)
    v = dict(w_in=v_w_in, b_forget=v_b_forget, conv_w=v_conv_w, g_attn_out=v_g_attn_out, g_conv_out=v_g_conv_out,
             w_out=v_w_out, g_mix_pre=v_g_mix_pre, g_mix_post=v_g_mix_post, w_gate_up=v_w_gate_up, w_down=v_w_down,
             g_ffn_pre=v_g_ffn_pre, g_ffn_post=v_g_ffn_post)
    cx, cy, cc = lax.axis_index("x"), lax.axis_index("y"), lax.axis_index("c")
    me = 2 * cx + cy
    c_idx = cc.astype(jnp.int32).reshape(1)
    me_idx = me.astype(jnp.int32).reshape(1)

    stored = lambda a: jnp.transpose(a, (2, 0, 1))
    grad_x, big, small_all = _device_step(x[0], loss_target[0], w, m, v, stored(w_in), stored(m_w_in),
                                          stored(v_w_in), c_idx, me_idx)
    gsum, delta, new_m, new_v = {}, {}, {}, {}
    for n in BIG:
        back = (lambda r: jnp.transpose(r, (1, 2, 0))) if n == "w_in" else (lambda r: r[None])
        gsum[n], delta[n], new_m[n], new_v[n] = [back(r) for r in big[n]]
    loss_sum, small_new = _small_update(me_idx, small_all, w, m, v)
    for n in SMALL_ALL:
        gsum[n], delta[n], new_m[n], new_v[n] = small_new[n]
    loss = 0.5 * loss_sum[0, 0]

    order = ("w_in", "b_forget", "conv_w", "g_attn_out", "g_conv_out", "w_out", "g_mix_pre", "g_mix_post",
             "w_gate_up", "w_down", "g_ffn_pre", "g_ffn_post")
    return (loss, grad_x[None], *[gsum[n] for n in order], *[delta[n] for n in order],
            *[new_m[n] for n in order], *[new_v[n] for n in order])
```

```python
import jax
import jax.numpy as jnp
import numpy as np
from jax import lax
from jax.experimental import pallas as pl
from jax.experimental.pallas import tpu as pltpu

F32 = jnp.float32
BF16 = jnp.bfloat16
MXU_DTYPE = jnp.bfloat16

D_MODEL = 1024
HEAD_DIM = 64
N_HEADS = 8
ATTN_W = 512
CONV_W = 512
D_FF = 2816
FF_PIECE = 1408
EPS = 1e-6
Q_SCALE = HEAD_DIM ** -0.5

OFF_F = 1536
OFF_B = 1664
OFF_C = 2176
OFF_U = 2688
IN_PAD = 3200
IN_W = 3080
N_CHIPS = 4

ADAM_LR = 0.001
ADAM_B1 = 0.9
ADAM_B2 = 0.999
ADAM_EPS = 1e-08
ADAM_WD = 0.01
ADAM_STEP = 10

VMEM_LIMIT_V7X = 56 * 1024 * 1024
MESH_ID = pl.DeviceIdType.MESH


def _params(sem=None, vmem=VMEM_LIMIT_V7X):
    kw = {"vmem_limit_bytes": vmem}
    if sem is not None:
        kw["dimension_semantics"] = sem
    return pltpu.CompilerParams(**kw)


def _dot(a, b):
    return jnp.dot(a, b, preferred_element_type=F32)


def _dot_nt(a, b):
    return lax.dot_general(a, b, (((1,), (1,)), ((), ())), preferred_element_type=F32)


def _dot_exact(x, ones, parts):
    if ones.dtype == F32:
        return _dot(x, ones)
    acc = None
    rem = x
    for _ in range(parts):
        piece = rem.astype(BF16)
        rem = rem - piece.astype(F32)
        term = _dot(piece, ones)
        acc = term if acc is None else acc + term
    return acc


def _rms(v):
    return lax.rsqrt(jnp.mean(v * v, axis=-1, keepdims=True) + EPS)


def _tok(tm, w):
    return pl.BlockSpec((tm, w), lambda i: (i, 0))


def _whole(shape, single=False):
    nd = len(shape)
    if single:
        return pl.BlockSpec(shape, lambda i: (0,) * nd, pipeline_mode=pl.Buffered(1))
    return pl.BlockSpec(shape, lambda i: (0,) * nd)


def _feat(rows, tm):
    return pl.BlockSpec((rows, tm), lambda i: (0, i))


def _piece_windows(rows, first, last):
    gap_at = OFF_F + N_HEADS
    runs = []
    for p in range(N_CHIPS):
        lo, hi = p * rows, (p + 1) * rows
        for a, b, shift in ((lo, min(hi, gap_at), 0), (max(lo, gap_at), hi, OFF_B - gap_at)):
            a, b = max(a + shift, first), min(b + shift, last)
            if a < b:
                runs.append((p, a - shift - lo, a, b - a))
    return runs


def _padded_rows(pieces_ref, w_ref):
    for p, at, dst, count in _piece_windows(pieces_ref.shape[1], 0, IN_PAD):
        w_ref[dst:dst + count, :] = pieces_ref[p, at:at + count, :]
    w_ref[OFF_F + N_HEADS:OFF_B, :] = jnp.zeros((OFF_B - OFF_F - N_HEADS, w_ref.shape[1]), w_ref.dtype)


def _inproj_fwd(x, g_pre, pieces, tm):
    s = x.shape[0]

    def body(x_ref, g_ref, pieces_ref, h_ref, q_ref, k_ref, v_ref, kt_ref, vt_ref, zt_ref, b_ref, c_ref, u_ref,
             w_ref):
        @pl.when(pl.program_id(0) == 0)
        def _():
            _padded_rows(pieces_ref, w_ref)

        xv = x_ref[...]
        h = ((xv * _rms(xv)) * g_ref[...]).astype(MXU_DTYPE)
        h_ref[...] = h

        def proj(lo, hi):
            return _dot_nt(h, w_ref[lo:hi, :])

        q_ref[...] = (proj(0, 512) * Q_SCALE).astype(MXU_DTYPE)
        kt = _dot_nt(w_ref[512:1024, :], h)
        vt = _dot_nt(w_ref[1024:OFF_F, :], h)
        kt_ref[...] = kt.astype(MXU_DTYPE)
        vt_ref[...] = vt.astype(MXU_DTYPE)
        k_ref[...] = kt.T.astype(MXU_DTYPE)
        v_ref[...] = vt.T.astype(MXU_DTYPE)
        zt_ref[...] = _dot_nt(w_ref[OFF_F:OFF_B, :], h)
        b_ref[...] = proj(OFF_B, OFF_C).astype(MXU_DTYPE)
        c_ref[...] = proj(OFF_C, OFF_U).astype(MXU_DTYPE)
        u_ref[...] = proj(OFF_U, IN_PAD).astype(MXU_DTYPE)

    sd = jax.ShapeDtypeStruct
    return pl.pallas_call(
        body, name="inproj_fwd", grid=(s // tm,),
        in_specs=[_tok(tm, D_MODEL), _whole((1, D_MODEL)), _whole(pieces.shape, single=True)],
        out_specs=[_tok(tm, D_MODEL), _tok(tm, 512), _tok(tm, 512), _tok(tm, 512), _feat(512, tm), _feat(512, tm),
                   _feat(128, tm), _tok(tm, 512), _tok(tm, 512), _tok(tm, 512), _whole((IN_PAD, D_MODEL))],
        out_shape=[sd((s, D_MODEL), MXU_DTYPE), sd((s, 512), MXU_DTYPE), sd((s, 512), MXU_DTYPE),
                   sd((s, 512), MXU_DTYPE), sd((512, s), MXU_DTYPE), sd((512, s), MXU_DTYPE), sd((128, s), F32),
                   sd((s, 512), MXU_DTYPE), sd((s, 512), MXU_DTYPE), sd((s, 512), MXU_DTYPE),
                   sd((IN_PAD, D_MODEL), MXU_DTYPE)],
        compiler_params=_params(("arbitrary",)),
    )(x, g_pre, pieces)


def _tri(n, upper):
    r = lax.broadcasted_iota(jnp.int32, (n, n), 0)
    c = lax.broadcasted_iota(jnp.int32, (n, n), 1)
    return ((r <= c) if upper else (r >= c)).astype(MXU_DTYPE)


HEAD_ROWS = 16


def _rows_to_cols(v):
    return jnp.concatenate([v, jnp.zeros((128 - HEAD_ROWS, 128), F32)], axis=0).T


BIAS_PARTS = 3


def _bias_placement():
    place_q = np.zeros((BIAS_PARTS, 128, ATTN_W), np.float32)
    place_k = np.zeros((BIAS_PARTS, 128, ATTN_W), np.float32)
    ones_q = np.zeros((1, ATTN_W), np.float32)
    ones_k = np.zeros((1, ATTN_W), np.float32)
    for h in range(N_HEADS):
        base = 2 * HEAD_DIM * (h // 2) + HEAD_DIM * (1 - h % 2)
        for part in range(BIAS_PARTS):
            place_q[part, h, base + part] = 1.0
            place_k[part, h, base + BIAS_PARTS + part] = -1.0
        ones_q[0, base + BIAS_PARTS:base + 2 * BIAS_PARTS] = 1.0
        ones_k[0, base:base + BIAS_PARTS] = 1.0
    return (jnp.asarray(place_q, MXU_DTYPE), jnp.asarray(place_k, MXU_DTYPE), jnp.asarray(ones_q), jnp.asarray(ones_k))


def _forget_fwd(z_t, b_col):
    s = z_t.shape[1]
    nb = s // 128

    def body(z_ref, b_ref, pq_ref, pk_ref, oq_ref, ok_ref, qa_ref, ka_ref, cc_ref):
        upper = _tri(128, True)

        carry = jnp.zeros((HEAD_ROWS, 1), F32)
        for n in range(nb):
            off = n * 128
            lf = jax.nn.log_sigmoid(z_ref[0:HEAD_ROWS, off:off + 128] + b_ref[...])
            cc_ref[off:off + 128, :] = _rows_to_cols(_dot_exact(lf, upper, 3) + carry)
            carry = carry + jnp.sum(lf, axis=1, keepdims=True)

        rb = min(s, 512)
        for off in range(0, s, rb):
            qa = jnp.broadcast_to(oq_ref[...], (rb, ATTN_W))
            ka = jnp.broadcast_to(ok_ref[...], (rb, ATTN_W))
            rem = cc_ref[off:off + rb, :]
            for part in range(BIAS_PARTS):
                piece = rem.astype(MXU_DTYPE)
                rem = rem - piece.astype(F32)
                qa = qa + _dot(piece, pq_ref[part])
                ka = ka + _dot(piece, pk_ref[part])
            qa_ref[off:off + rb, :] = qa.astype(MXU_DTYPE)
            ka_ref[off:off + rb, :] = ka.astype(MXU_DTYPE)

    sd = jax.ShapeDtypeStruct
    return pl.pallas_call(body, name="forget_fwd",
                          out_shape=[sd((s, ATTN_W), MXU_DTYPE), sd((s, ATTN_W), MXU_DTYPE)],
                          scratch_shapes=[pltpu.VMEM((s, 128), F32)],
                          compiler_params=_params())(z_t, b_col, *_bias_placement())


def _aligned(start, size):
    return pl.ds(start if isinstance(start, int) else pl.multiple_of(start, size), size)


def _pair_lanes(pp):
    return _aligned(pp * 2 * HEAD_DIM, 2 * HEAD_DIM)


def _head_rows(h):
    return _aligned(h * HEAD_DIM, HEAD_DIM)


def _only_head(block, hb):
    lane = lax.broadcasted_iota(jnp.int32, block.shape, 1)
    return jnp.where((lane >= HEAD_DIM) if hb else (lane < HEAD_DIM), block, jnp.zeros_like(block))


def _other_head(block, other, hb):
    lane = lax.broadcasted_iota(jnp.int32, block.shape, 1)
    return jnp.where((lane >= HEAD_DIM) if hb else (lane < HEAD_DIM), block, other)


def _attn_fwd(qs, k, v_t, q_bias, k_bias, t, shards):
    s = qs.shape[0]
    n = s // t
    pairs = [(i, j) for i in range(n) for j in range(i + 1)]
    it = jnp.asarray(np.array([p[0] for p in pairs], np.int32))
    jt = jnp.asarray(np.array([p[1] for p in pairs], np.int32))
    nw = len(shards)
    last = len(pairs) - 1
    mid = (2 * len(pairs)) // 3

    def body(it_ref, jt_ref, q_ref, k_ref, vt_ref, qb_ref, kb_ref, *rest):
        sh, (o_ref, lse_ref, p_ref, pm_ref), got = rest[:nw], rest[nw:nw + 4], rest[nw + 4:2 * nw + 4]
        m_sc, l_sc, acc_sc, send, recv = rest[2 * nw + 4:]
        p = pl.program_id(0)
        i = it_ref[p]
        j = jt_ref[p]
        gather_start, gather_forward, gather_finish = _gather_stages(sh, got, send, recv)
        pl.when(p == 0)(gather_start)
        if mid < last:
            pl.when(p == mid)(gather_forward)

        @pl.when(j == 0)
        def _():
            m_sc[...] = jnp.full_like(m_sc, -1e30)
            l_sc[...] = jnp.ones_like(l_sc)
            acc_sc[...] = jnp.zeros_like(acc_sc)

        pm_ref[...] = jnp.zeros_like(pm_ref)

        def pair_step(pp, diagonal):
            lanes = _pair_lanes(pp)
            kp = k_ref[:, lanes]
            qp = q_ref[:, lanes]
            kb = kb_ref[:, lanes]
            qb = qb_ref[:, lanes]
            for hb in range(2):
                h = 2 * pp + hb
                row = pl.ds(h, 1)
                rows = _head_rows(h)
                st = _dot_nt(_other_head(kp, kb, hb), _other_head(qp, qb, hb))
                if diagonal:
                    kpos = lax.broadcasted_iota(jnp.int32, (t, t), 0)
                    qpos = lax.broadcasted_iota(jnp.int32, (t, t), 1)
                    st = jnp.where(kpos <= qpos, st, -1e30)
                m_prev = m_sc[row, :]
                m_new = jnp.maximum(m_prev, jnp.max(st, axis=0, keepdims=True))
                alpha = jnp.exp(m_prev - m_new)
                pt = jnp.exp(st - m_new)
                l_sc[row, :] = alpha * l_sc[row, :] + jnp.sum(pt, axis=0, keepdims=True)
                ptb = pt.astype(MXU_DTYPE)
                acc_sc[rows, :] = acc_sc[rows, :] * alpha + _dot(vt_ref[rows, :], ptb)
                m_sc[row, :] = m_new
                p_ref[0, h] = ptb
                pm_ref[0, row, :] = m_new

        @pl.when(j < i)
        def _():
            for pp in range(N_HEADS // 2):
                pair_step(pp, False)

        @pl.when(j == i)
        def _():
            for pp in range(N_HEADS // 2):
                pair_step(pp, True)
                sub = lax.broadcasted_iota(jnp.int32, (2 * HEAD_DIM, t), 0)
                l_pair = jnp.where(sub < HEAD_DIM, l_sc[pl.ds(2 * pp, 1), :], l_sc[pl.ds(2 * pp + 1, 1), :])
                o_t = acc_sc[_aligned(pp * 2 * HEAD_DIM, 2 * HEAD_DIM), :] / l_pair
                o_ref[:, _pair_lanes(pp)] = o_t.T.astype(MXU_DTYPE)

            lse_ref[...] = m_sc[...] + jnp.log(l_sc[...])

        @pl.when(p == last)
        def _():
            if mid >= last:
                gather_forward()
            gather_finish()

    gs = pltpu.PrefetchScalarGridSpec(
        num_scalar_prefetch=2, grid=(len(pairs),),
        in_specs=[pl.BlockSpec((t, ATTN_W), lambda p, it_, jt_: (it_[p], 0)),
                  pl.BlockSpec((t, ATTN_W), lambda p, it_, jt_: (jt_[p], 0)),
                  pl.BlockSpec((ATTN_W, t), lambda p, it_, jt_: (0, jt_[p])),
                  pl.BlockSpec((t, ATTN_W), lambda p, it_, jt_: (it_[p], 0)),
                  pl.BlockSpec((t, ATTN_W), lambda p, it_, jt_: (jt_[p], 0))] + [ANY] * nw,
        out_specs=[pl.BlockSpec((t, ATTN_W), lambda p, it_, jt_: (it_[p], 0)),
                   pl.BlockSpec((HEAD_ROWS, t), lambda p, it_, jt_: (0, it_[p])),
                   pl.BlockSpec((1, N_HEADS, t, t), lambda p, it_, jt_: (p, 0, 0, 0)),
                   pl.BlockSpec((1, HEAD_ROWS, t), lambda p, it_, jt_: (p, 0, 0))] + [ANY] * nw,
        scratch_shapes=[pltpu.VMEM((HEAD_ROWS, t), F32), pltpu.VMEM((HEAD_ROWS, t), F32), pltpu.VMEM((ATTN_W, t), F32),
                        pltpu.SemaphoreType.DMA((nw, 6)), pltpu.SemaphoreType.DMA((nw, 6))])
    sd = jax.ShapeDtypeStruct
    o, lse, probs, probs_max, *got = pl.pallas_call(
        body, name="attn_fwd", grid_spec=gs,
        out_shape=[sd((s, ATTN_W), MXU_DTYPE), sd((HEAD_ROWS, s), F32), sd((len(pairs), N_HEADS, t, t), MXU_DTYPE),
                   sd((len(pairs), HEAD_ROWS, t), F32)]
        + [sd((N_CHIPS,) + a.shape, a.dtype) for a in shards],
        compiler_params=_params(("arbitrary",)),
    )(it, jt, qs, k, v_t, q_bias, k_bias, *shards)
    me = 2 * lax.axis_index("x") + lax.axis_index("y")
    return o, lse, probs, probs_max, [lax.dynamic_update_index_in_dim(g, own, me, 0) for g, own in zip(got, shards)]


HALO = 16


def _halo_before(tm):
    return pl.BlockSpec((HALO, CONV_W), lambda i: (jnp.maximum(i * (tm // HALO) - 1, 0), 0))


def _shift_down(cur, prev_ref, first, cols=slice(None)):
    row = lax.broadcasted_iota(jnp.int32, cur.shape, 0)

    def before(r):
        prod = prev_ref[0][r:r + 1, cols].astype(F32) * prev_ref[1][r:r + 1, cols].astype(F32)
        return jnp.where(first, 0.0, prod)

    p7, p6 = before(HALO - 1), before(HALO - 2)
    s1 = jnp.where(row == 0, p7, pltpu.roll(cur, 1, 0))
    s2 = jnp.where(row == 0, p6, jnp.where(row == 1, p7, pltpu.roll(cur, 2, 0)))
    return s1, s2


def _group_ms(v, gmat):
    return _dot_exact(v, gmat, 1) * (1.0 / HEAD_DIM)


def _mixer_fwd(x, o_attn, gate_b, gate_c, u, conv_w, g_attn, g_conv, w_out, g_post, gmat, tm):
    s = x.shape[0]

    def body(x_ref, o_ref, b_ref, c_ref, u_ref, cp_ref, up_ref, cw_ref, ga_ref, gc_ref, wo_ref, gp_ref, gm_ref,
             x2_ref, mg_ref, y_ref, z_ref):
        i = pl.program_id(0)
        gm = gm_ref[0:128, 0:128]
        for lo in range(0, ATTN_W, 128):
            cols = slice(lo, lo + 128)
            cu = c_ref[:, cols].astype(F32) * u_ref[:, cols].astype(F32)
            cu1, cu2 = _shift_down(cu, (cp_ref, up_ref), i == 0, cols)
            z = cw_ref[0:1, cols] * cu2 + cw_ref[1:2, cols] * cu1 + cw_ref[2:3, cols] * cu
            z_ref[:, cols] = z.astype(MXU_DTYPE)
            cv = b_ref[:, cols].astype(F32) * z
            ov = o_ref[:, cols].astype(F32)
            mg_ref[:, cols] = ((ov * lax.rsqrt(_group_ms(ov * ov, gm) + EPS)) * ga_ref[:, cols]).astype(MXU_DTYPE)
            mg_ref[:, ATTN_W + lo:ATTN_W + lo + 128] = (
                (cv * lax.rsqrt(_group_ms(cv * cv, gm) + EPS)) * gc_ref[:, cols]).astype(MXU_DTYPE)
        y = _dot(mg_ref[...], wo_ref[...])
        y_ref[...] = y
        x2_ref[...] = x_ref[...] + (y * _rms(y)) * gp_ref[...]

    halo = _halo_before(tm)
    sd = jax.ShapeDtypeStruct
    return pl.pallas_call(
        body, name="mixer_fwd", grid=(s // tm,),
        in_specs=[_tok(tm, D_MODEL), _tok(tm, 512), _tok(tm, 512), _tok(tm, 512), _tok(tm, 512), halo, halo,
                  _whole((3, 512)), _whole((1, 512)), _whole((1, 512)), _whole((D_MODEL, D_MODEL), single=True),
                  _whole((1, D_MODEL)), _whole((512, 512))],
        out_specs=[_tok(tm, D_MODEL), _tok(tm, D_MODEL), _tok(tm, D_MODEL), _tok(tm, 512)],
        out_shape=[sd((s, D_MODEL), F32), sd((s, D_MODEL), MXU_DTYPE), sd((s, D_MODEL), F32), sd((s, 512), MXU_DTYPE)],
        compiler_params=_params(("arbitrary",)),
    )(x, o_attn, gate_b, gate_c, u, gate_c, u, conv_w, g_attn, g_conv, w_out, g_post, gmat)


def _ffn_fwd(x2, target, g_pre, w_gu, w_dn, g_post, tm):
    s = x2.shape[0]

    def body(x_ref, t_ref, gpre_ref, wgu_ref, wdn_ref, gpost_ref,
             h_ref, g_ref, up_ref, a_ref, ff_ref, dout_ref, loss_ref):
        xv = x_ref[...]
        h = ((xv * _rms(xv)) * gpre_ref[...]).astype(MXU_DTYPE)
        h_ref[...] = h
        ff = jnp.zeros((tm, D_MODEL), F32)
        for j in range(2):
            cols = slice(j * FF_PIECE, (j + 1) * FF_PIECE)
            g = _dot(h, wgu_ref[j])
            up = _dot(h, wgu_ref[2 + j])
            a = ((g * jax.nn.sigmoid(g)) * up).astype(MXU_DTYPE)
            g_ref[:, cols] = g.astype(MXU_DTYPE)
            up_ref[:, cols] = up.astype(MXU_DTYPE)
            a_ref[:, cols] = a
            ff = ff + _dot(a, wdn_ref[j])
        ff_ref[...] = ff
        err = (xv + (ff * _rms(ff)) * gpost_ref[...]) - t_ref[...]
        dout_ref[...] = err * (1.0 / D_MODEL)
        part = jnp.sum(jnp.mean(err * err, axis=-1, keepdims=True), axis=0, keepdims=True)

        @pl.when(pl.program_id(0) == 0)
        def _():
            loss_ref[...] = jnp.zeros_like(loss_ref)

        loss_ref[...] += part

    sd = jax.ShapeDtypeStruct
    return pl.pallas_call(
        body, name="ffn_fwd", grid=(s // tm,),
        in_specs=[_tok(tm, D_MODEL), _tok(tm, D_MODEL), _whole((1, D_MODEL)),
                  _whole((4, D_MODEL, FF_PIECE), single=True), _whole((2, FF_PIECE, D_MODEL), single=True),
                  _whole((1, D_MODEL))],
        out_specs=[_tok(tm, D_MODEL), _tok(tm, D_FF), _tok(tm, D_FF), _tok(tm, D_FF), _tok(tm, D_MODEL),
                   _tok(tm, D_MODEL), _whole((8, 128))],
        out_shape=[sd((s, D_MODEL), MXU_DTYPE), sd((s, D_FF), MXU_DTYPE), sd((s, D_FF), MXU_DTYPE),
                   sd((s, D_FF), MXU_DTYPE), sd((s, D_MODEL), F32), sd((s, D_MODEL), F32), sd((8, 128), F32)],
        compiler_params=_params(("arbitrary",)),
    )(x2, target, g_pre, w_gu, w_dn, g_post)


def _norm_bwd(dy, normed, rinv, gain):
    t = dy * gain
    return rinv * (t - normed * jnp.mean(t * normed, axis=-1, keepdims=True))


def _acc_rows(ref, first, val):
    @pl.when(first)
    def _():
        ref[...] = jnp.zeros_like(ref)

    ref[...] += jnp.sum(val, axis=0, keepdims=True)


def _ffn_bwd(dout, ff, x2, g, up, g_post, g_pre, w_gu, w_dn, tm):
    s = x2.shape[0]

    def body(do_ref, ff_ref, x_ref, g_ref, up_ref, gpost_ref, gpre_ref, wgu_ref, wdn_ref,
             dx_ref, dff_ref, dgu_ref, dgpost_ref, dgpre_ref):
        first = pl.program_id(0) == 0
        ffv = ff_ref[...]
        rf = _rms(ffv)
        n = ffv * rf
        do = do_ref[...]
        _acc_rows(dgpost_ref, first, do * n)
        dff = _norm_bwd(do, n, rf, gpost_ref[...]).astype(MXU_DTYPE)
        dff_ref[...] = dff
        dh = jnp.zeros((tm, D_MODEL), F32)
        for j in range(2):
            cols = slice(j * FF_PIECE, (j + 1) * FF_PIECE)
            da = _dot_nt(dff, wdn_ref[j])
            gv = g_ref[:, cols].astype(F32)
            sg = jax.nn.sigmoid(gv)
            dg = (da * up_ref[:, cols].astype(F32) * (sg * (1.0 + gv * (1.0 - sg)))).astype(MXU_DTYPE)
            du = (da * (gv * sg)).astype(MXU_DTYPE)
            dgu_ref[:, cols] = dg
            dgu_ref[:, D_FF + j * FF_PIECE:D_FF + (j + 1) * FF_PIECE] = du
            dh = dh + _dot_nt(dg, wgu_ref[j]) + _dot_nt(du, wgu_ref[2 + j])
        xv = x_ref[...]
        r2 = _rms(xv)
        nx = xv * r2
        _acc_rows(dgpre_ref, first, dh * nx)
        dx_ref[...] = do + _norm_bwd(dh, nx, r2, gpre_ref[...])

    sd = jax.ShapeDtypeStruct
    return pl.pallas_call(
        body, name="ffn_bwd", grid=(s // tm,),
        in_specs=[_tok(tm, D_MODEL), _tok(tm, D_MODEL), _tok(tm, D_MODEL), _tok(tm, D_FF), _tok(tm, D_FF),
                  _whole((1, D_MODEL)), _whole((1, D_MODEL)),
                  _whole((4, D_MODEL, FF_PIECE), single=True), _whole((2, FF_PIECE, D_MODEL), single=True)],
        out_specs=[_tok(tm, D_MODEL), _tok(tm, D_MODEL), _tok(tm, 2 * D_FF), _whole((1, D_MODEL)),
                   _whole((1, D_MODEL))],
        out_shape=[sd((s, D_MODEL), F32), sd((s, D_MODEL), MXU_DTYPE), sd((s, 2 * D_FF), MXU_DTYPE),
                   sd((1, D_MODEL), F32), sd((1, D_MODEL), F32)],
        compiler_params=_params(("arbitrary",)),
    )(dout, ff, x2, g, up, g_post, g_pre, w_gu, w_dn)


def _tn_matmul(a, b, tm, tn, tk, name, totals=(), piece_rows=None):
    s, m = a.shape
    n = b.shape[1]
    nw = len(totals)
    grid = (m // tm, n // tn, s // tk)
    assert piece_rows is None or tn == n

    def body(a_ref, b_ref, *rest):
        o_ref = rest[nw]
        acc_ref = o_ref if piece_rows is None else rest[-1]
        if nw:
            step = (pl.program_id(0) * grid[1] + pl.program_id(1)) * grid[2] + pl.program_id(2)
            share_start, share_finish = _share_stages(rest[:nw], rest[nw + 1:2 * nw + 1], *rest[2 * nw + 1:2 * nw + 3])
            pl.when(step == 0)(share_start)

        @pl.when(pl.program_id(2) == 0)
        def _():
            acc_ref[...] = jnp.zeros_like(acc_ref)

        acc_ref[...] += lax.dot_general(a_ref[...], b_ref[...], (((0,), (0,)), ((), ())), preferred_element_type=F32)
        if piece_rows is not None:
            for r in range(grid[0]):
                @pl.when((pl.program_id(0) == r) & (pl.program_id(2) == grid[2] - 1))
                def _():
                    for p, at, src, count in _piece_windows(piece_rows, r * tm, (r + 1) * tm):
                        o_ref[p, at:at + count, :] = acc_ref[src - r * tm:src - r * tm + count, :]
        if nw:
            pl.when(step == grid[0] * grid[1] * grid[2] - 1)(share_finish)

    if piece_rows is None:
        o_spec, o_shape = pl.BlockSpec((tm, tn), lambda i, j, kk: (i, j)), (m, n)
    else:
        o_spec, o_shape = pl.BlockSpec((N_CHIPS, piece_rows, n), lambda i, j, kk: (0, 0, 0)), (N_CHIPS, piece_rows, n)
    out = pl.pallas_call(
        body, name=name, grid=grid,
        in_specs=[pl.BlockSpec((tk, tm), lambda i, j, kk: (kk, i)), pl.BlockSpec((tk, tn), lambda i, j, kk: (kk, j))]
        + [ANY] * nw,
        out_specs=[o_spec] + [ANY] * nw,
        out_shape=[jax.ShapeDtypeStruct(o_shape, F32)] + [jax.ShapeDtypeStruct(t.shape, t.dtype) for t in totals],
        scratch_shapes=([pltpu.SemaphoreType.DMA((nw,)), pltpu.SemaphoreType.DMA((nw,))] if nw else [])
        + ([] if piece_rows is None else [pltpu.VMEM((tm, tn), F32)]),
        compiler_params=_params(("arbitrary", "arbitrary", "arbitrary")),
    )(a, b, *totals)
    return (out[0], out[1:]) if nw else out[0]


def _mixer_bwd(dx2, y, o_attn, gate_b, z, g_post, g_attn, g_conv, w_out, gmat, sel, tm, ready, kinds):
    s = dx2.shape[0]
    nw = len(ready)
    nt = s // tm

    def body(d_ref, y_ref, o_ref, b_ref, z_ref, gp_ref, ga_ref, gc_ref, wo_ref, gm_ref, sel_ref, *rest):
        grads = rest[:nw]
        dy_ref, do_ref, db_ref, dz_ref, delta_ref, dgp_ref, dga_ref, dgc_ref = rest[nw:nw + 8]
        taken = rest[nw + 8:2 * nw + 8]
        send, recv = rest[2 * nw + 8:]
        first = pl.program_id(0) == 0
        pair_start, pair_finish = _pair_stages(grads, kinds, taken, send, recv)
        pl.when(first)(pair_start)
        yv = y_ref[...]
        ry = _rms(yv)
        ny = yv * ry
        d = d_ref[...]
        _acc_rows(dgp_ref, first, d * ny)
        dy = _norm_bwd(d, ny, ry, gp_ref[...]).astype(MXU_DTYPE)
        dy_ref[...] = dy
        dm = _dot_nt(dy, wo_ref[...])
        gm = gm_ref[0:128, 0:128]

        @pl.when(first)
        def _():
            dga_ref[...] = jnp.zeros_like(dga_ref)
            dgc_ref[...] = jnp.zeros_like(dgc_ref)

        def group_bwd(val, dmv, gain_ref, dg_ref, cols):
            rg = lax.rsqrt(_group_ms(val * val, gm) + EPS)
            nv = val * rg
            dg_ref[:, cols] += jnp.sum(dmv * nv, axis=0, keepdims=True)
            t = dmv * gain_ref[:, cols]
            return rg * (t - nv * _group_ms(t * nv, gm))

        delta = jnp.zeros((tm, 128), F32)
        for lo in range(0, ATTN_W, 128):
            cols = slice(lo, lo + 128)
            ov = o_ref[:, cols].astype(F32)
            d_o = group_bwd(ov, dm[:, cols], ga_ref, dga_ref, cols)
            do_ref[:, cols] = d_o.astype(MXU_DTYPE)
            delta = delta + _dot_exact(d_o * ov, sel_ref[cols, :], 2)
            zv = z_ref[:, cols].astype(F32)
            bv = b_ref[:, cols].astype(F32)
            d_cv = group_bwd(bv * zv, dm[:, ATTN_W + lo:ATTN_W + lo + 128], gc_ref, dgc_ref, cols)
            db_ref[:, cols] = (d_cv * zv).astype(MXU_DTYPE)
            dz_ref[:, cols] = d_cv * bv
        delta_ref[...] = delta.T[0:HEAD_ROWS, :]
        pl.when(pl.program_id(0) == nt - 1)(pair_finish)

    sd = jax.ShapeDtypeStruct
    taken_shape = [sd((N_CHIPS, g.shape[-2], g.shape[-1] if kd == "rows" else g.shape[-1] // N_CHIPS), F32)
                   for g, kd in zip(ready, kinds)]
    out = pl.pallas_call(
        body, name="mixer_bwd", grid=(nt,),
        in_specs=[_tok(tm, D_MODEL), _tok(tm, D_MODEL), _tok(tm, 512), _tok(tm, 512), _tok(tm, 512),
                  _whole((1, D_MODEL)), _whole((1, 512)), _whole((1, 512)),
                  _whole((D_MODEL, D_MODEL), single=True), _whole((512, 512)), _whole((512, 128))] + [ANY] * nw,
        out_specs=[_tok(tm, D_MODEL), _tok(tm, 512), _tok(tm, 512), _tok(tm, 512), _feat(HEAD_ROWS, tm),
                   _whole((1, D_MODEL)), _whole((1, 512)), _whole((1, 512))] + [ANY] * nw,
        out_shape=[sd((s, D_MODEL), MXU_DTYPE), sd((s, 512), MXU_DTYPE), sd((s, 512), MXU_DTYPE), sd((s, 512), F32),
                   sd((HEAD_ROWS, s), F32), sd((1, D_MODEL), F32), sd((1, 512), F32), sd((1, 512), F32)] + taken_shape,
        scratch_shapes=[pltpu.SemaphoreType.DMA((nw, N_CHIPS)), pltpu.SemaphoreType.DMA((nw, N_CHIPS))],
        compiler_params=_params(("arbitrary",)),
    )(dx2, y, o_attn, gate_b, z, g_post, g_attn, g_conv, w_out, gmat, sel, *ready)
    return out[:8], out[8:]


def _attn_bwd(qs, k_t, v, do, probs, probs_max, lse, delta, t, parts):
    s = qs.shape[0]
    n = s // t
    pairs = [(i, j) for j in range(n) for i in range(j, n)]
    it = jnp.asarray(np.array([p[0] for p in pairs], np.int32))
    jt = jnp.asarray(np.array([p[1] for p in pairs], np.int32))
    ft = jnp.asarray(np.array([p[0] * (p[0] + 1) // 2 + p[1] for p in pairs], np.int32))

    nw = len(parts)

    def body(it_ref, jt_ref, ft_ref, q_ref, kt_ref, v_ref, do_ref, p_ref, pm_ref, lse_ref, dl_ref, *rest):
        pb = rest[:nw]
        dq_ref, dk_ref, dv_ref, dc_ref, dcq_ref = rest[nw:nw + 5]
        rcv = rest[nw + 5:2 * nw + 5]
        dk_sc, dv_sc, dc_sc, send, recv = rest[2 * nw + 5:]
        p = pl.program_id(0)
        i = it_ref[p]
        j = jt_ref[p]
        chip_start, chip_finish = _chip_stages(pb, rcv, send, recv)

        @pl.when(p == 0)
        def _():
            chip_start()
            dq_ref[...] = jnp.zeros_like(dq_ref)
            dcq_ref[...] = jnp.zeros_like(dcq_ref)

        @pl.when(i == j)
        def _():
            dk_sc[...] = jnp.zeros_like(dk_sc)
            dv_sc[...] = jnp.zeros_like(dv_sc)
            dc_sc[...] = jnp.zeros_like(dc_sc)

        def pair_step(pp):
            lanes = _pair_lanes(pp)
            qp = q_ref[:, lanes]
            vp = v_ref[:, lanes]
            dop = do_ref[:, lanes]
            lane = lax.broadcasted_iota(jnp.int32, (t, 128), 1)
            for hb in range(2):
                h = 2 * pp + hb
                row = pl.ds(h, 1)
                pt = p_ref[0, h].astype(F32) * jnp.exp(pm_ref[0, row, :] - lse_ref[row, :])
                dv_sc[:, lanes] += _dot(pt.astype(MXU_DTYPE), _only_head(dop, hb))
                dst = pt * (_dot_nt(_only_head(vp, hb), dop) - dl_ref[row, :])
                dc_sc[...] -= jnp.where(lane == h, jnp.sum(dst, axis=1, keepdims=True), 0.0)
                dcq_ref[i, row, :] += jnp.sum(dst, axis=0, keepdims=True)
                dsb = dst.astype(MXU_DTYPE)
                dk_sc[:, lanes] += _dot(dsb, _only_head(qp, hb))
                rows = _head_rows(h)
                dq_ref[i, rows, :] += _dot(kt_ref[rows, :], dsb)

        for pp in range(N_HEADS // 2):
            pair_step(pp)

        @pl.when(i == n - 1)
        def _():
            dk_ref[...] = dk_sc[...].astype(MXU_DTYPE)
            dv_ref[...] = dv_sc[...].astype(MXU_DTYPE)
            dc_ref[...] = dc_sc[...]

        pl.when(p == len(pairs) - 1)(chip_finish)

    qi = lambda p, it_, jt_, ft_: (it_[p], 0)
    kj = lambda p, it_, jt_, ft_: (jt_[p], 0)
    row_i = lambda p, it_, jt_, ft_: (0, it_[p])
    gs = pltpu.PrefetchScalarGridSpec(
        num_scalar_prefetch=3, grid=(len(pairs),),
        in_specs=[pl.BlockSpec((t, ATTN_W), qi),
                  pl.BlockSpec((ATTN_W, t), lambda p, it_, jt_, ft_: (0, jt_[p])),
                  pl.BlockSpec((t, ATTN_W), kj), pl.BlockSpec((t, ATTN_W), qi),
                  pl.BlockSpec((1, N_HEADS, t, t), lambda p, it_, jt_, ft_: (ft_[p], 0, 0, 0)),
                  pl.BlockSpec((1, HEAD_ROWS, t), lambda p, it_, jt_, ft_: (ft_[p], 0, 0)),
                  pl.BlockSpec((HEAD_ROWS, t), row_i), pl.BlockSpec((HEAD_ROWS, t), row_i)] + [ANY] * nw,
        out_specs=[pl.BlockSpec((n, ATTN_W, t), lambda p, it_, jt_, ft_: (0, 0, 0)),
                   pl.BlockSpec((t, ATTN_W), kj), pl.BlockSpec((t, ATTN_W), kj),
                   pl.BlockSpec((t, 128), kj),
                   pl.BlockSpec((n, HEAD_ROWS, t), lambda p, it_, jt_, ft_: (0, 0, 0))] + [ANY] * nw,
        scratch_shapes=[pltpu.VMEM((t, ATTN_W), F32), pltpu.VMEM((t, ATTN_W), F32),
                        pltpu.VMEM((t, 128), F32), pltpu.SemaphoreType.DMA((nw, 3)), pltpu.SemaphoreType.DMA((nw, 3))])
    sd = jax.ShapeDtypeStruct
    out = pl.pallas_call(
        body, name="attn_bwd", grid_spec=gs,
        out_shape=[sd((n, ATTN_W, t), F32), sd((s, ATTN_W), MXU_DTYPE), sd((s, ATTN_W), MXU_DTYPE),
                   sd((s, 128), F32), sd((n, HEAD_ROWS, t), F32)] + [sd((3,) + a.shape[1:], a.dtype) for a in parts],
        compiler_params=_params(("arbitrary",)),
    )(it, jt, ft, qs, k_t, v, do, probs, probs_max, lse, delta, *parts)
    return out[:5], out[5:]


def _forget_bwd(dc_rows, dc_cols, z_t, b_col):
    s = z_t.shape[1]
    nb = s // 128

    def body(dr_ref, dcc_ref, z_ref, b_ref, dz_ref, db_ref):
        lower = _tri(128, False)
        real = lax.broadcasted_iota(jnp.int32, (HEAD_ROWS, 128), 0) < N_HEADS

        tail = jnp.zeros((HEAD_ROWS, 1), F32)
        dbias = jnp.zeros((HEAD_ROWS, 1), F32)
        for m in range(nb):
            off = (nb - 1 - m) * 128
            dc = dr_ref[:, off:off + 128] + dcc_ref[off:off + 128, :].T[0:HEAD_ROWS, :]
            dlf = _dot_exact(dc, lower, 3) + tail
            dz = dlf * jax.nn.sigmoid(-(z_ref[0:HEAD_ROWS, off:off + 128] + b_ref[...]))
            dz = jnp.where(real, dz, 0.0)
            dz_ref[off:off + 128, :] = _rows_to_cols(dz)
            tail = tail + jnp.sum(dc, axis=1, keepdims=True)
            dbias = dbias + jnp.sum(dz, axis=1, keepdims=True)
        db_ref[...] = jnp.broadcast_to(dbias, db_ref.shape)

    return pl.pallas_call(
        body, name="forget_bwd",
        out_shape=[jax.ShapeDtypeStruct((s, 128), F32), jax.ShapeDtypeStruct((HEAD_ROWS, 128), F32)],
        compiler_params=_params())(dc_rows, dc_cols, z_t, b_col)


def _inproj_bwd(dz, gate_c, u, conv_w, dq, dk, dv, dzf, db, x, dx2, g_pre, w_t, tm):
    s = x.shape[0]
    nt = s // tm
    t = dq.shape[2]
    assert t % tm == 0 and dq.shape[:2] == (s // t, ATTN_W)
    per = t // tm

    def body(dz_ref, dzn_ref, c_ref, u_ref, cp_ref, up_ref, cw_ref, dq_ref, dk_ref, dv_ref, dzf_ref, db_ref,
             x_ref, dx2_ref, g_ref, w_ref, gx_ref, dp_ref, dg_ref, dcw_ref):
        i = pl.program_id(0)
        first = i == 0
        last = i == nt - 1
        @pl.when(first)
        def _():
            dcw_ref[...] = jnp.zeros_like(dcw_ref)

        for lo in range(0, CONV_W, 128):
            cols = slice(lo, lo + 128)
            dzv = dz_ref[:, cols]
            row = lax.broadcasted_iota(jnp.int32, dzv.shape, 0)
            n0 = jnp.where(last, 0.0, dzn_ref[0:1, cols])
            n1 = jnp.where(last, 0.0, dzn_ref[1:2, cols])
            dz1 = jnp.where(row == tm - 1, n0, pltpu.roll(dzv, tm - 1, 0))
            dz2 = jnp.where(row == tm - 1, n1, jnp.where(row == tm - 2, n0, pltpu.roll(dzv, tm - 2, 0)))
            dcu = cw_ref[2:3, cols] * dzv + cw_ref[1:2, cols] * dz1 + cw_ref[0:1, cols] * dz2
            cv = c_ref[:, cols].astype(F32)
            uv = u_ref[:, cols].astype(F32)
            cu = cv * uv
            cu1, cu2 = _shift_down(cu, (cp_ref, up_ref), first, cols)
            dcw_ref[0:1, cols] += jnp.sum(dzv * cu2, axis=0, keepdims=True)
            dcw_ref[1:2, cols] += jnp.sum(dzv * cu1, axis=0, keepdims=True)
            dcw_ref[2:3, cols] += jnp.sum(dzv * cu, axis=0, keepdims=True)
            dp_ref[:, OFF_C + lo:OFF_C + lo + 128] = (dcu * uv).astype(MXU_DTYPE)
            dp_ref[:, OFF_U + lo:OFF_U + lo + 128] = (dcu * cv).astype(MXU_DTYPE)

        dp_ref[:, 0:512] = (dq_ref[0].T * Q_SCALE).astype(MXU_DTYPE)
        dp_ref[:, 512:1024] = dk_ref[...].astype(MXU_DTYPE)
        dp_ref[:, 1024:OFF_F] = dv_ref[...].astype(MXU_DTYPE)
        dp_ref[:, OFF_F:OFF_B] = dzf_ref[...].astype(MXU_DTYPE)
        dp_ref[:, OFF_B:OFF_C] = db_ref[...].astype(MXU_DTYPE)
        dh = _dot(dp_ref[...], w_ref[...])
        xv = x_ref[...]
        r1 = _rms(xv)
        nx = xv * r1
        _acc_rows(dg_ref, first, dh * nx)
        gx_ref[...] = dx2_ref[...] + _norm_bwd(dh, nx, r1, g_ref[...])

    prev = _halo_before(tm)
    nxt = pl.BlockSpec((8, 512), lambda i: (jnp.minimum((i + 1) * (tm // 8), s // 8 - 1), 0))
    sd = jax.ShapeDtypeStruct
    return pl.pallas_call(
        body, name="inproj_bwd", grid=(nt,),
        in_specs=[_tok(tm, 512), nxt, _tok(tm, 512), _tok(tm, 512), prev, prev, _whole((3, 512)),
                  pl.BlockSpec((1, ATTN_W, tm), lambda i: (i // per, 0, i % per)), _tok(tm, 512), _tok(tm, 512),
                  _tok(tm, 128),
                  _tok(tm, 512),
                  _tok(tm, D_MODEL), _tok(tm, D_MODEL), _whole((1, D_MODEL)), _whole((IN_PAD, D_MODEL), single=True)],
        out_specs=[_tok(tm, D_MODEL), _tok(tm, IN_PAD), _whole((1, D_MODEL)), _whole((8, 512))],
        out_shape=[sd((s, D_MODEL), F32), sd((s, IN_PAD), MXU_DTYPE), sd((1, D_MODEL), F32), sd((8, 512), F32)],
        compiler_params=_params(("arbitrary",)),
    )(dz, dz, gate_c, u, gate_c, u, conv_w, dq, dk, dv, dzf, db, x, dx2, g_pre, w_t)


def _tile(s, want):
    return want if s % want == 0 else s


def _halves(a):
    return a.reshape(2, a.shape[0] // 2, a.shape[1])


def _device_step(x, target, w, mom1, mom2, w_in_t, m_in_t, v_in_t, c_idx, me_idx):
    s = x.shape[0]
    tm = _tile(s, 512)
    tf = _tile(s, 256)
    ta = _tile(s, 512)
    tkk = _tile(s, 2048)
    gidx = np.arange(512) // HEAD_DIM
    gmat = jnp.asarray(gidx[:, None] == gidx[None, :], MXU_DTYPE)
    sel = jnp.asarray(gidx[:, None] == np.arange(128)[None, :], MXU_DTYPE)
    g_mix_pre, g_mix_post, g_ffn_pre, g_ffn_post = w["g_mix_pre"], w["g_mix_post"], w["g_ffn_pre"], w["g_ffn_post"]
    g_attn, g_conv, b_forget = w["g_attn_out"], w["g_conv_out"], w["b_forget"]
    shard = {n: _halves(w[n][0].astype(MXU_DTYPE)) for n in BIG[1:]}
    piece_rows = IN_W // N_CHIPS

    g_in, conv_all = _gather_weights([w_in_t.reshape(piece_rows, D_MODEL).astype(MXU_DTYPE)], w["conv_w"][0])
    conv_w = jnp.transpose(conv_all, (1, 0, 2)).reshape(3, CONV_W)

    h1, qs, k, v, k_t, v_t, z_t, gate_b, gate_c, u, w_t = _inproj_fwd(x, g_mix_pre, g_in, tm)
    b_col = jnp.pad(jnp.transpose(b_forget), ((0, HEAD_ROWS - N_HEADS), (0, 0)))
    q_bias, k_bias = _forget_fwd(z_t, b_col)
    o_attn, lse, probs, probs_max, (g_out, g_gu, g_dn) = _attn_fwd(
        qs, k, v_t, q_bias, k_bias, ta, [shard["w_out"], shard["w_gate_up"], shard["w_down"]])
    w_out = g_out.reshape(D_MODEL, D_MODEL)
    w_gu = g_gu.reshape(N_CHIPS, D_MODEL, FF_PIECE)
    w_dn = g_dn.reshape(2, FF_PIECE, D_MODEL)
    x2, merged, y, z = _mixer_fwd(x, o_attn, gate_b, gate_c, u, conv_w, g_attn, g_conv, w_out, g_mix_post, gmat, tm)
    h2, g, up, a, ff, dout, loss_acc = _ffn_fwd(x2, target, g_ffn_pre, w_gu, w_dn, g_ffn_post, tf)

    dx2, dff, dgu, dg_ffn_post, dg_ffn_pre = _ffn_bwd(dout, ff, x2, g, up, g_ffn_post, g_ffn_pre, w_gu, w_dn, tf)
    dw_dn = _tn_matmul(a, dff, FF_PIECE, 1024, tkk, "dw_down").reshape(N_CHIPS, 2, D_FF // (2 * N_CHIPS), D_MODEL)
    dw_gu = _tn_matmul(h2, dgu, 1024, FF_PIECE, tkk, "dw_gate_up").reshape(2, D_MODEL // 2, 2 * D_FF)
    (dy, d_o, d_b, dz, delta, dg_mix_post, dg_attn, dg_conv), (a_gu, a_dn) = _mixer_bwd(
        dx2, y, o_attn, gate_b, z, g_mix_post, g_attn, g_conv, w_out, gmat, sel, tm, [dw_gu, dw_dn], ["cols", "rows"])
    dw_out = _tn_matmul(merged, dy, 1024, 1024, tkk, "dw_out").reshape(N_CHIPS, 2, D_MODEL // (2 * N_CHIPS), D_MODEL)
    place = jnp.concatenate([c_idx, me_idx])
    *sum_gu, a_out = _pair_sum(place, dw_gu, "cols", a_gu, "pair_sum_w_gate_up", [dw_out], ["rows"])
    sum_dn = _pair_sum(place, dw_dn, "rows", a_dn, "pair_sum_w_down")
    sum_out = _pair_sum(place, dw_out, "rows", a_out, "pair_sum_w_out")
    (dq_t, dk, dv, dc_cols, dcq), (r_gu, r_dn, r_out) = _attn_bwd(
        qs, k_t, v, d_o, probs, probs_max, lse, delta, ta, [sum_gu[1], sum_dn[1], sum_out[1]])
    dc_rows = jnp.transpose(dcq, (1, 0, 2)).reshape(HEAD_ROWS, s)
    dzf, db_f = _forget_bwd(dc_rows, dc_cols, z_t, b_col)
    grad_x, dproj, dg_mix_pre, dcw = _inproj_bwd(dz, gate_c, u, conv_w, dq_t, dk, dv, dzf, d_b,
                                                 x, dx2, g_mix_pre, w_t, tm)
    done = [_chip_sum(sb[0], r, "chip_sum_" + n)
            for n, sb, r in zip(BIG[1:], (sum_out, sum_gu, sum_dn), (r_out, r_gu, r_dn))]
    dw_in, done_theirs = _tn_matmul(dproj, h1, 640, 1024, tkk, "dw_in", done, piece_rows)

    (a_in,) = _pair_exchange([dw_in], ["lanes"])
    sum_in = _pair_sum(place, dw_in, "lanes", a_in, "pair_sum_w_in")
    small = dict(b_forget=db_f[:N_HEADS, 0], g_attn_out=dg_attn, g_conv_out=dg_conv, g_mix_pre=dg_mix_pre,
                 g_mix_post=dg_mix_post, g_ffn_pre=dg_ffn_pre, g_ffn_post=dg_ffn_post)
    (r_in,), small_all = _chip_exchange([sum_in[1]], _pack_small(small, dcw[:3], loss_acc[0, 0]))
    t_in = _chip_sum(sum_in[0], r_in, "chip_sum_w_in")
    (s_in,) = _pair_share([t_in], "pair_share_w_in")
    new = {"w_in": _adamw_lanes(c_idx, w_in_t, t_in, s_in, m_in_t, v_in_t, "adamw_w_in")}
    for n, mine, theirs in zip(BIG[1:], done, done_theirs):
        new[n] = _adamw(c_idx, w[n][0], mine, theirs, mom1[n][0], mom2[n][0], 2, "adamw_" + n)
    return grad_x, new, small_all


BIG = ("w_in", "w_out", "w_gate_up", "w_down")
ANY = pl.BlockSpec(memory_space=pl.ANY)


def _place():
    x, y, c = lax.axis_index("x"), lax.axis_index("y"), lax.axis_index("c")
    others = [(1 - x, y), (x, 1 - y), (1 - x, 1 - y)]
    return x, y, c, 2 * x + y, others, [2 * px + py for px, py in others]


def _remote(src, dst, send, recv, dev):
    return pltpu.make_async_remote_copy(src_ref=src, dst_ref=dst, send_sem=send, recv_sem=recv,
                                        device_id=dev, device_id_type=MESH_ID)


def _gather_stages(sh, outs, send, recv):
    x, y, c, me, others, chips = _place()
    sib = (x, y, 1 - c)
    every = [(w, kk) for w in range(len(sh)) for kk in range(3)]

    def half_of(ref, half, piece=None):
        ref = ref if piece is None else ref.at[piece]
        if len(ref.shape) == 3:
            return ref.at[half]
        hc = ref.shape[1] // 2
        return ref.at[:, pl.ds(pl.multiple_of(half * hc, 128), hc)]

    def first(w, kk):
        return _remote(half_of(sh[w], c), half_of(outs[w], c, me), send.at[w, kk], recv.at[w, kk], (*others[kk], c))

    def landed(w, kk):
        r = half_of(outs[w], c, chips[kk])
        return _remote(r, r, send.at[w, kk], recv.at[w, kk], (*others[kk], c))

    def onward(w, kk, half):
        r = half_of(outs[w], half, chips[kk])
        return _remote(r, r, send.at[w, 3 + kk], recv.at[w, 3 + kk], sib)

    def start():
        for w, kk in every:
            first(w, kk).start()

    def forward():
        for w, kk in every:
            landed(w, kk).wait_recv()
            onward(w, kk, c).start()

    def finish():
        for w, kk in every:
            onward(w, kk, 1 - c).wait_recv()
        for w, kk in every:
            first(w, kk).wait_send()
            onward(w, kk, c).wait_send()

    return start, forward, finish


def _pair_piece(ref, kind, p, half):
    if kind == "rows":
        return ref.at[p, half]
    if kind == "lanes":
        hc = ref.shape[2] // 2
        return ref.at[p, :, pl.ds(pl.multiple_of(half * hc, 128), hc)]
    cols = ref.shape[2] // N_CHIPS
    return ref.at[half, :, pl.ds(p * cols, cols)]


def _pair_stages(g, kinds, a, send, recv):
    x, y, c, _, _, _ = _place()
    copies = [_remote(_pair_piece(g[w], kinds[w], p, 1 - c), a[w].at[p], send.at[w, p], recv.at[w, p], (x, y, 1 - c))
              for w in range(len(g)) for p in range(N_CHIPS)]

    def start():
        for cp in copies:
            cp.start()

    def finish():
        for cp in copies:
            cp.wait()

    return start, finish


def _chip_stages(pb, rcv, send, recv):
    x, y, c, _, others, chips = _place()
    copies = [_remote(pb[w].at[chips[kk]], rcv[w].at[kk], send.at[w, kk], recv.at[w, kk], (*others[kk], c))
              for w in range(len(pb)) for kk in range(3)]

    def start():
        for cp in copies:
            cp.start()

    def finish():
        for cp in copies:
            cp.wait()

    return start, finish


def _gather_weights(shards, conv_w):
    n = len(shards)

    def body(*refs):
        sh, cw, outs, cwo = refs[:n], refs[n], refs[n + 1:2 * n + 1], refs[2 * n + 1]
        send, recv = refs[2 * n + 2:]
        x, y, c, me, others, chips = _place()
        start, forward, finish = _gather_stages(sh, outs, send, recv)
        start()
        small = [_remote(cw, cwo.at[me], send.at[n, kk], recv.at[n, kk], (*others[kk], c)) for kk in range(3)]
        for cp in small:
            cp.start()
        forward()
        for kk in range(3):
            _remote(cw, cwo.at[chips[kk]], send.at[n, kk], recv.at[n, kk], (*others[kk], c)).wait_recv()
        finish()
        for cp in small:
            cp.wait_send()

    out_shape = [jax.ShapeDtypeStruct((N_CHIPS,) + s.shape, s.dtype) for s in shards]
    out_shape.append(jax.ShapeDtypeStruct((N_CHIPS,) + conv_w.shape, conv_w.dtype))
    got = pl.pallas_call(
        body, name="gather_weights", in_specs=[ANY] * (n + 1), out_specs=[ANY] * (n + 1), out_shape=out_shape,
        scratch_shapes=[pltpu.SemaphoreType.DMA((n + 1, 6)), pltpu.SemaphoreType.DMA((n + 1, 6))],
    )(*shards, conv_w)
    me = 2 * lax.axis_index("x") + lax.axis_index("y")
    return [lax.dynamic_update_index_in_dim(g, own, me, 0) for g, own in zip(got, list(shards) + [conv_w])]


def _taken_shape(g, kind):
    if kind == "rows":
        return (N_CHIPS,) + g.shape[2:]
    if kind == "lanes":
        return g.shape[:2] + (g.shape[2] // 2,)
    return (N_CHIPS, g.shape[1], g.shape[2] // N_CHIPS)


def _pair_sum(place, g, kind, a, name, ready=(), ready_kinds=()):
    _, half, cols = a.shape
    nw = len(ready)
    if kind == "rows":
        mine = pl.BlockSpec((1, 1, half, cols), lambda p, pr: (p, pr[0], 0, 0))
    elif kind == "lanes":
        mine = pl.BlockSpec((1, half, cols), lambda p, pr: (p, 0, pr[0]))
    else:
        mine = pl.BlockSpec((1, half, cols), lambda p, pr: (pr[0], 0, p))

    def body(place_ref, g_ref, a_ref, *rest):
        grads, (own_ref, pb_ref), taken = rest[:nw], rest[nw:nw + 2], rest[nw + 2:2 * nw + 2]
        if nw:
            pair_start, pair_finish = _pair_stages(grads, ready_kinds, taken, *rest[2 * nw + 2:])
            pl.when(pl.program_id(0) == 0)(pair_start)
        tot = (g_ref[0, 0] if kind == "rows" else g_ref[0]) + a_ref[0]
        pb_ref[0] = tot.astype(BF16)

        @pl.when(pl.program_id(0) == place_ref[1])
        def _():
            own_ref[...] = tot

        if nw:
            pl.when(pl.program_id(0) == N_CHIPS - 1)(pair_finish)

    sems = [pltpu.SemaphoreType.DMA((nw, N_CHIPS)), pltpu.SemaphoreType.DMA((nw, N_CHIPS))] if nw else []
    gs = pltpu.PrefetchScalarGridSpec(
        num_scalar_prefetch=1, grid=(N_CHIPS,),
        in_specs=[mine, pl.BlockSpec((1, half, cols), lambda p, pr: (p, 0, 0))] + [ANY] * nw,
        out_specs=[pl.BlockSpec((half, cols), lambda p, pr: (0, 0)),
                   pl.BlockSpec((1, half, cols), lambda p, pr: (p, 0, 0))] + [ANY] * nw,
        scratch_shapes=sems)
    out = pl.pallas_call(
        body, name=name, grid_spec=gs,
        out_shape=[jax.ShapeDtypeStruct((half, cols), F32), jax.ShapeDtypeStruct((N_CHIPS, half, cols), BF16)]
        + [jax.ShapeDtypeStruct(_taken_shape(r, kd), r.dtype) for r, kd in zip(ready, ready_kinds)],
        compiler_params=_params(("arbitrary",)),
    )(place, g, a, *ready)
    return list(out)


def _chip_sum(own, rcv, name):
    half, cols = own.shape

    def body(o_ref, r_ref, t_ref):
        t_ref[...] = ((o_ref[...] + r_ref[0].astype(F32)) + r_ref[1].astype(F32)) + r_ref[2].astype(F32)

    return pl.pallas_call(
        body, name=name, grid=(1,),
        in_specs=[pl.BlockSpec((half, cols), lambda i: (0, 0)), pl.BlockSpec((3, half, cols), lambda i: (0, 0, 0))],
        out_specs=pl.BlockSpec((half, cols), lambda i: (0, 0)),
        out_shape=jax.ShapeDtypeStruct((half, cols), F32), compiler_params=_params(("arbitrary",)),
    )(own, rcv)


def _small_stages(sm, smg, send, recv):
    x, y, c, _, _, _ = _place()

    def peer(r):
        return (1 - x if r & 4 else x, 1 - y if r & 2 else y, 1 - c if r & 1 else c)

    mine = 4 * x + 2 * y + c
    copies = [_remote(sm, smg.at[mine], send.at[r - 1], recv.at[r - 1], peer(r)) for r in range(1, 8)]

    def start():
        for cp in copies:
            cp.start()

    def finish():
        for r in range(1, 8):
            px, py, pc = peer(r)
            _remote(sm, smg.at[4 * px + 2 * py + pc], send.at[r - 1], recv.at[r - 1], (px, py, pc)).wait_recv()
        for cp in copies:
            cp.wait_send()

    return start, finish


def _pair_exchange(grads, kinds):
    n = len(grads)

    def body(*refs):
        start, finish = _pair_stages(refs[:n], kinds, refs[n:2 * n], *refs[2 * n:])
        start()
        finish()

    return pl.pallas_call(
        body, name="pair_exchange", in_specs=[ANY] * n, out_specs=[ANY] * n,
        out_shape=[jax.ShapeDtypeStruct(_taken_shape(g, kd), g.dtype) for g, kd in zip(grads, kinds)],
        scratch_shapes=[pltpu.SemaphoreType.DMA((n, N_CHIPS)), pltpu.SemaphoreType.DMA((n, N_CHIPS))],
    )(*grads)


def _chip_exchange(parts, small):
    n = len(parts)

    def body(*refs):
        pb, sm, rcv, smg = refs[:n], refs[n], refs[n + 1:2 * n + 1], refs[2 * n + 1]
        send, recv, ssend, srecv = refs[2 * n + 2:]
        chip_start, chip_finish = _chip_stages(pb, rcv, send, recv)
        small_start, small_finish = _small_stages(sm, smg, ssend, srecv)
        chip_start()
        small_start()
        chip_finish()
        small_finish()

    out_shape = [jax.ShapeDtypeStruct((3,) + p.shape[1:], p.dtype) for p in parts]
    out_shape.append(jax.ShapeDtypeStruct((8,) + small.shape, small.dtype))
    *arrived, small_land = pl.pallas_call(
        body, name="chip_exchange", in_specs=[ANY] * (n + 1), out_specs=[ANY] * (n + 1), out_shape=out_shape,
        scratch_shapes=[pltpu.SemaphoreType.DMA((n, 3)), pltpu.SemaphoreType.DMA((n, 3)),
                        pltpu.SemaphoreType.DMA((7,)), pltpu.SemaphoreType.DMA((7,))],
    )(*parts, small)
    mine = 4 * lax.axis_index("x") + 2 * lax.axis_index("y") + lax.axis_index("c")
    return arrived, lax.dynamic_update_index_in_dim(small_land, small, mine, 0)


def _share_stages(t, g, send, recv):
    x, y, c, _, _, _ = _place()
    copies = [_remote(t[w], g[w], send.at[w], recv.at[w], (x, y, 1 - c)) for w in range(len(t))]

    def start():
        for cp in copies:
            cp.start()

    def finish():
        for cp in copies:
            cp.wait()

    return start, finish


def _pair_share(totals, name):
    n = len(totals)

    def body(*refs):
        start, finish = _share_stages(refs[:n], refs[n:2 * n], *refs[2 * n:])
        start()
        finish()

    return pl.pallas_call(
        body, name=name, in_specs=[ANY] * n, out_specs=[ANY] * n,
        out_shape=[jax.ShapeDtypeStruct(t.shape, t.dtype) for t in totals],
        scratch_shapes=[pltpu.SemaphoreType.DMA((n,)), pltpu.SemaphoreType.DMA((n,))],
    )(*totals)


def _adamw_math(w, g, m, v):
    m = ADAM_B1 * m + (1.0 - ADAM_B1) * g
    v = ADAM_B2 * v + (1.0 - ADAM_B2) * (g * g)
    m_hat = m / (1.0 - ADAM_B1 ** ADAM_STEP)
    v_hat = v / (1.0 - ADAM_B2 ** ADAM_STEP)
    delta = -ADAM_LR * (m_hat / (jnp.sqrt(v_hat) + ADAM_EPS) + ADAM_WD * w)
    return delta, m, v


def _adamw(c_idx, w, mine, theirs, m, v, nb, name):
    rows, cols = w.shape
    tr = rows // (2 * nb)

    def body(c_ref, w_ref, a_ref, b_ref, m_ref, v_ref, g_ref, d_ref, nm_ref, nv_ref):
        g = jnp.where(pl.program_id(0) == c_ref[0], a_ref[...], b_ref[...])
        g_ref[...] = g
        d_ref[...], nm_ref[...], nv_ref[...] = _adamw_math(w_ref[...], g, m_ref[...], v_ref[...])

    full = pl.BlockSpec((tr, cols), lambda hh, i, cr: (hh * nb + i, 0))
    half = pl.BlockSpec((tr, cols), lambda hh, i, cr: (i, 0))
    gs = pltpu.PrefetchScalarGridSpec(num_scalar_prefetch=1, grid=(2, nb), in_specs=[full, half, half, full, full],
                                      out_specs=[full] * 4)
    return pl.pallas_call(
        body, name=name, grid_spec=gs, out_shape=[jax.ShapeDtypeStruct((rows, cols), F32)] * 4,
        compiler_params=_params(("arbitrary", "arbitrary")),
    )(c_idx, w, mine, theirs, m, v)


def _adamw_lanes(c_idx, w, mine, theirs, m, v, name):
    rows, _, cols = w.shape
    hc = cols // 2

    def body(c_ref, w_ref, a_ref, b_ref, m_ref, v_ref, g_ref, d_ref, nm_ref, nv_ref):
        g = jnp.where(pl.program_id(0) == c_ref[0], a_ref[...], b_ref[...])
        g_ref[:, 0, :] = g
        d_ref[:, 0, :], nm_ref[:, 0, :], nv_ref[:, 0, :] = _adamw_math(w_ref[:, 0, :], g, m_ref[:, 0, :], v_ref[:, 0, :])

    full = pl.BlockSpec((rows, 1, hc), lambda hh, cr: (0, 0, hh))
    half = pl.BlockSpec((rows, hc), lambda hh, cr: (0, 0))
    gs = pltpu.PrefetchScalarGridSpec(num_scalar_prefetch=1, grid=(2,), in_specs=[full, half, half, full, full],
                                      out_specs=[full] * 4)
    return pl.pallas_call(
        body, name=name, grid_spec=gs, out_shape=[jax.ShapeDtypeStruct((rows, 1, cols), F32)] * 4,
        compiler_params=_params(("arbitrary",)),
    )(c_idx, w, mine, theirs, m, v)


SMALL = ("g_mix_pre", "g_mix_post", "g_ffn_pre", "g_ffn_post")
SMALL_ALL = SMALL + ("g_attn_out", "g_conv_out", "conv_w", "b_forget")
SMALL_AT = {"g_mix_pre": (0, 0, 1024), "g_mix_post": (1, 0, 1024), "g_ffn_pre": (2, 0, 1024),
            "g_ffn_post": (3, 0, 1024), "g_attn_out": (4, 0, 512), "g_conv_out": (4, 512, 512),
            "b_forget": (7, 0, N_HEADS)}
CONV_AT = ((5, 0), (5, 512), (6, 0))
LOSS_AT = (6, 512)


def _pack_small(t, conv_full, loss_sum):
    conv = jnp.concatenate([conv_full.reshape(1, 3 * CONV_W), loss_sum.reshape(1, 1),
                            jnp.zeros((1, 2048 - 3 * CONV_W - 1), F32)], axis=1).reshape(2, 1024)
    return jnp.concatenate([t[n].reshape(1, 1024) for n in SMALL]
                           + [jnp.concatenate([t["g_attn_out"].reshape(1, 512), t["g_conv_out"].reshape(1, 512)], axis=1),
                              conv, jnp.pad(t["b_forget"].reshape(1, N_HEADS), ((0, 0), (0, 1024 - N_HEADS)))], axis=0)


def _small_update(me_idx, gathered, w, m, v):
    def body(me_ref, gg_ref, *refs):
        k = len(SMALL_ALL)
        w_refs, m_refs, v_refs = refs[:k], refs[k:2 * k], refs[2 * k:3 * k]
        loss_ref = refs[3 * k]
        outs = refs[3 * k + 1:3 * k + 1 + 4 * k]
        sums = refs[-1]
        g = gg_ref[0]
        for dev in range(1, 8):
            g = g + gg_ref[dev]
        sums[...] = g
        loss_ref[...] = sums[LOSS_AT[0]:LOSS_AT[0] + 1, LOSS_AT[1]:LOSS_AT[1] + 1]
        mine = pl.multiple_of(me_ref[0] * 128, 128)
        for idx, name in enumerate(SMALL_ALL):
            g_ref, d_ref, nm_ref, nv_ref = outs[4 * idx:4 * idx + 4]
            if name == "conv_w":
                for r, (row, lo) in enumerate(CONV_AT):
                    gr = sums[row:row + 1, pl.ds(lo + mine, 128)]
                    g_ref[0, r:r + 1, :] = gr
                    d_ref[0, r:r + 1, :], nm_ref[0, r:r + 1, :], nv_ref[0, r:r + 1, :] = _adamw_math(
                        w_refs[idx][0, r:r + 1, :], gr, m_refs[idx][0, r:r + 1, :], v_refs[idx][0, r:r + 1, :])
            else:
                row, lo, n = SMALL_AT[name]
                gr = sums[row:row + 1, lo:lo + n]
                g_ref[...] = gr
                d_ref[...], nm_ref[...], nv_ref[...] = _adamw_math(w_refs[idx][...], gr, m_refs[idx][...],
                                                                    v_refs[idx][...])

    def whole(a):
        nd = a.ndim
        return pl.BlockSpec(a.shape, lambda i, mr: (0,) * nd)

    ins = [t[n] for t in (w, m, v) for n in SMALL_ALL]
    out_shape = [jax.ShapeDtypeStruct((1, 1), F32)]
    for n in SMALL_ALL:
        out_shape += [jax.ShapeDtypeStruct(w[n].shape, F32)] * 4
    gs = pltpu.PrefetchScalarGridSpec(
        num_scalar_prefetch=1, grid=(1,), in_specs=[whole(gathered)] + [whole(a) for a in ins],
        out_specs=[whole(o) for o in out_shape], scratch_shapes=[pltpu.VMEM((8, 1024), F32)])
    out = pl.pallas_call(body, name="small_update", grid_spec=gs, out_shape=out_shape,
                         compiler_params=_params(("arbitrary",)))(me_idx, gathered, *ins)
    return out[0], {n: out[1 + 4 * i:5 + 4 * i] for i, n in enumerate(SMALL_ALL)}


def kernel(x, w_in, b_forget, conv_w, g_attn_out, g_conv_out, w_out, g_mix_pre, g_mix_post, w_gate_up, w_down, g_ffn_pre, g_ffn_post, loss_target, m_w_in, m_b_forget, m_conv_w, m_g_attn_out, m_g_conv_out, m_w_out, m_g_mix_pre, m_g_mix_post, m_w_gate_up, m_w_down, m_g_ffn_pre, m_g_ffn_post, v_w_in, v_b_forget, v_conv_w, v_g_attn_out, v_g_conv_out, v_w_out, v_g_mix_pre, v_g_mix_post, v_w_gate_up, v_w_down, v_g_ffn_pre, v_g_ffn_post):
    w = dict(w_in=w_in, b_forget=b_forget, conv_w=conv_w, g_attn_out=g_attn_out, g_conv_out=g_conv_out, w_out=w_out,
             g_mix_pre=g_mix_pre, g_mix_post=g_mix_post, w_gate_up=w_gate_up, w_down=w_down, g_ffn_pre=g_ffn_pre,
             g_ffn_post=g_ffn_post)
    m = dict(w_in=m_w_in, b_forget=m_b_forget, conv_w=m_conv_w, g_attn_out=m_g_attn_out, g_conv_out=m_g_conv_out,
             w_out=m_w_out, g_mix_pre=m_g_mix_pre, g_mix_post=m_g_mix_post, w_gate_up=m_w_gate_up, w_down=m_w_down,
             g_ffn_pre=m_g_ffn_pre, g_ffn_post=m_g_ffn_post)
    v = dict(w_in=v_w_in, b_forget=v_b_forget, conv_w=v_conv_w, g_attn_out=v_g_attn_out, g_conv_out=v_g_conv_out,
             w_out=v_w_out, g_mix_pre=v_g_mix_pre, g_mix_post=v_g_mix_post, w_gate_up=v_w_gate_up, w_down=v_w_down,
             g_ffn_pre=v_g_ffn_pre, g_ffn_post=v_g_ffn_post)
    cx, cy, cc = lax.axis_index("x"), lax.axis_index("y"), lax.axis_index("c")
    me = 2 * cx + cy
    c_idx = cc.astype(jnp.int32).reshape(1)
    me_idx = me.astype(jnp.int32).reshape(1)

    stored = lambda a: jnp.transpose(a, (2, 0, 1))
    grad_x, big, small_all = _device_step(x[0], loss_target[0], w, m, v, stored(w_in), stored(m_w_in),
                                          stored(v_w_in), c_idx, me_idx)
    gsum, delta, new_m, new_v = {}, {}, {}, {}
    for n in BIG:
        back = (lambda r: jnp.transpose(r, (1, 2, 0))) if n == "w_in" else (lambda r: r[None])
        gsum[n], delta[n], new_m[n], new_v[n] = [back(r) for r in big[n]]
    loss_sum, small_new = _small_update(me_idx, small_all, w, m, v)
    for n in SMALL_ALL:
        gsum[n], delta[n], new_m[n], new_v[n] = small_new[n]
    loss = 0.5 * loss_sum[0, 0]

    order = ("w_in", "b_forget", "conv_w", "g_attn_out", "g_conv_out", "w_out", "g_mix_pre", "g_mix_post",
             "w_gate_up", "w_down", "g_ffn_pre", "g_ffn_post")
    return (loss, grad_x[None], *[gsum[n] for n in order], *[delta[n] for n in order],
            *[new_m[n] for n in order], *[new_v[n] for n in order])
```

```python
import jax
import jax.numpy as jnp
import numpy as np
from jax import lax
from jax.experimental import pallas as pl
from jax.experimental.pallas import tpu as pltpu

F32 = jnp.float32
BF16 = jnp.bfloat16
MXU_DTYPE = jnp.bfloat16

D_MODEL = 1024
HEAD_DIM = 64
N_HEADS = 8
ATTN_W = 512
CONV_W = 512
D_FF = 2816
FF_PIECE = 1408
EPS = 1e-6
Q_SCALE = HEAD_DIM ** -0.5

OFF_F = 1536
OFF_B = 1664
OFF_C = 2176
OFF_U = 2688
IN_PAD = 3200
IN_W = 3080
N_CHIPS = 4

ADAM_LR = 0.001
ADAM_B1 = 0.9
ADAM_B2 = 0.999
ADAM_EPS = 1e-08
ADAM_WD = 0.01
ADAM_STEP = 10

VMEM_LIMIT_V7X = 56 * 1024 * 1024
MESH_ID = pl.DeviceIdType.MESH


def _params(sem=None, vmem=VMEM_LIMIT_V7X):
    kw = {"vmem_limit_bytes": vmem}
    if sem is not None:
        kw["dimension_semantics"] = sem
    return pltpu.CompilerParams(**kw)


def _dot(a, b):
    return jnp.dot(a, b, preferred_element_type=F32)


def _dot_nt(a, b):
    return lax.dot_general(a, b, (((1,), (1,)), ((), ())), preferred_element_type=F32)


def _dot_exact(x, ones, parts):
    if ones.dtype == F32:
        return _dot(x, ones)
    acc = None
    rem = x
    for _ in range(parts):
        piece = rem.astype(BF16)
        rem = rem - piece.astype(F32)
        term = _dot(piece, ones)
        acc = term if acc is None else acc + term
    return acc


def _rms(v):
    return lax.rsqrt(jnp.mean(v * v, axis=-1, keepdims=True) + EPS)


def _tok(tm, w):
    return pl.BlockSpec((tm, w), lambda i: (i, 0))


def _whole(shape, single=False):
    nd = len(shape)
    if single:
        return pl.BlockSpec(shape, lambda i: (0,) * nd, pipeline_mode=pl.Buffered(1))
    return pl.BlockSpec(shape, lambda i: (0,) * nd)


def _feat(rows, tm):
    return pl.BlockSpec((rows, tm), lambda i: (0, i))


def _piece_windows(rows, first, last):
    gap_at = OFF_F + N_HEADS
    runs = []
    for p in range(N_CHIPS):
        lo, hi = p * rows, (p + 1) * rows
        for a, b, shift in ((lo, min(hi, gap_at), 0), (max(lo, gap_at), hi, OFF_B - gap_at)):
            a, b = max(a + shift, first), min(b + shift, last)
            if a < b:
                runs.append((p, a - shift - lo, a, b - a))
    return runs


def _padded_rows(pieces_ref, w_ref):
    for p, at, dst, count in _piece_windows(pieces_ref.shape[1], 0, IN_PAD):
        w_ref[dst:dst + count, :] = pieces_ref[p, at:at + count, :]
    w_ref[OFF_F + N_HEADS:OFF_B, :] = jnp.zeros((OFF_B - OFF_F - N_HEADS, w_ref.shape[1]), w_ref.dtype)


def _inproj_fwd(x, g_pre, pieces, tm):
    s = x.shape[0]

    def body(x_ref, g_ref, pieces_ref, h_ref, q_ref, k_ref, v_ref, kt_ref, vt_ref, zt_ref, b_ref, c_ref, u_ref,
             w_ref):
        @pl.when(pl.program_id(0) == 0)
        def _():
            _padded_rows(pieces_ref, w_ref)

        xv = x_ref[...]
        h = ((xv * _rms(xv)) * g_ref[...]).astype(MXU_DTYPE)
        h_ref[...] = h

        def proj(lo, hi):
            return _dot_nt(h, w_ref[lo:hi, :])

        q_ref[...] = (proj(0, 512) * Q_SCALE).astype(MXU_DTYPE)
        kt = _dot_nt(w_ref[512:1024, :], h)
        vt = _dot_nt(w_ref[1024:OFF_F, :], h)
        kt_ref[...] = kt.astype(MXU_DTYPE)
        vt_ref[...] = vt.astype(MXU_DTYPE)
        k_ref[...] = kt.T.astype(MXU_DTYPE)
        v_ref[...] = vt.T.astype(MXU_DTYPE)
        zt_ref[...] = _dot_nt(w_ref[OFF_F:OFF_B, :], h)
        b_ref[...] = proj(OFF_B, OFF_C).astype(MXU_DTYPE)
        c_ref[...] = proj(OFF_C, OFF_U).astype(MXU_DTYPE)
        u_ref[...] = proj(OFF_U, IN_PAD).astype(MXU_DTYPE)

    sd = jax.ShapeDtypeStruct
    return pl.pallas_call(
        body, name="inproj_fwd", grid=(s // tm,),
        in_specs=[_tok(tm, D_MODEL), _whole((1, D_MODEL)), _whole(pieces.shape, single=True)],
        out_specs=[_tok(tm, D_MODEL), _tok(tm, 512), _tok(tm, 512), _tok(tm, 512), _feat(512, tm), _feat(512, tm),
                   _feat(128, tm), _tok(tm, 512), _tok(tm, 512), _tok(tm, 512), _whole((IN_PAD, D_MODEL))],
        out_shape=[sd((s, D_MODEL), MXU_DTYPE), sd((s, 512), MXU_DTYPE), sd((s, 512), MXU_DTYPE),
                   sd((s, 512), MXU_DTYPE), sd((512, s), MXU_DTYPE), sd((512, s), MXU_DTYPE), sd((128, s), F32),
                   sd((s, 512), MXU_DTYPE), sd((s, 512), MXU_DTYPE), sd((s, 512), MXU_DTYPE),
                   sd((IN_PAD, D_MODEL), MXU_DTYPE)],
        compiler_params=_params(("arbitrary",)),
    )(x, g_pre, pieces)


def _tri(n, upper):
    r = lax.broadcasted_iota(jnp.int32, (n, n), 0)
    c = lax.broadcasted_iota(jnp.int32, (n, n), 1)
    return ((r <= c) if upper else (r >= c)).astype(MXU_DTYPE)


HEAD_ROWS = 16


def _rows_to_cols(v):
    return jnp.concatenate([v, jnp.zeros((128 - HEAD_ROWS, 128), F32)], axis=0).T


BIAS_PARTS = 3


def _bias_placement():
    place_q = np.zeros((BIAS_PARTS, 128, ATTN_W), np.float32)
    place_k = np.zeros((BIAS_PARTS, 128, ATTN_W), np.float32)
    ones_q = np.zeros((1, ATTN_W), np.float32)
    ones_k = np.zeros((1, ATTN_W), np.float32)
    for h in range(N_HEADS):
        base = 2 * HEAD_DIM * (h // 2) + HEAD_DIM * (1 - h % 2)
        for part in range(BIAS_PARTS):
            place_q[part, h, base + part] = 1.0
            place_k[part, h, base + BIAS_PARTS + part] = -1.0
        ones_q[0, base + BIAS_PARTS:base + 2 * BIAS_PARTS] = 1.0
        ones_k[0, base:base + BIAS_PARTS] = 1.0
    return (jnp.asarray(place_q, MXU_DTYPE), jnp.asarray(place_k, MXU_DTYPE), jnp.asarray(ones_q), jnp.asarray(ones_k))


def _forget_fwd(z_t, b_col):
    s = z_t.shape[1]
    nb = s // 128

    def body(z_ref, b_ref, pq_ref, pk_ref, oq_ref, ok_ref, qa_ref, ka_ref, cc_ref):
        upper = _tri(128, True)

        carry = jnp.zeros((HEAD_ROWS, 1), F32)
        for n in range(nb):
            off = n * 128
            lf = jax.nn.log_sigmoid(z_ref[0:HEAD_ROWS, off:off + 128] + b_ref[...])
            cc_ref[off:off + 128, :] = _rows_to_cols(_dot_exact(lf, upper, 3) + carry)
            carry = carry + jnp.sum(lf, axis=1, keepdims=True)

        rb = min(s, 512)
        for off in range(0, s, rb):
            qa = jnp.broadcast_to(oq_ref[...], (rb, ATTN_W))
            ka = jnp.broadcast_to(ok_ref[...], (rb, ATTN_W))
            rem = cc_ref[off:off + rb, :]
            for part in range(BIAS_PARTS):
                piece = rem.astype(MXU_DTYPE)
                rem = rem - piece.astype(F32)
                qa = qa + _dot(piece, pq_ref[part])
                ka = ka + _dot(piece, pk_ref[part])
            qa_ref[off:off + rb, :] = qa.astype(MXU_DTYPE)
            ka_ref[off:off + rb, :] = ka.astype(MXU_DTYPE)

    sd = jax.ShapeDtypeStruct
    return pl.pallas_call(body, name="forget_fwd",
                          out_shape=[sd((s, ATTN_W), MXU_DTYPE), sd((s, ATTN_W), MXU_DTYPE)],
                          scratch_shapes=[pltpu.VMEM((s, 128), F32)],
                          compiler_params=_params())(z_t, b_col, *_bias_placement())


def _aligned(start, size):
    return pl.ds(start if isinstance(start, int) else pl.multiple_of(start, size), size)


def _pair_lanes(pp):
    return _aligned(pp * 2 * HEAD_DIM, 2 * HEAD_DIM)


def _head_rows(h):
    return _aligned(h * HEAD_DIM, HEAD_DIM)


def _only_head(block, hb):
    lane = lax.broadcasted_iota(jnp.int32, block.shape, 1)
    return jnp.where((lane >= HEAD_DIM) if hb else (lane < HEAD_DIM), block, jnp.zeros_like(block))


def _other_head(block, other, hb):
    lane = lax.broadcasted_iota(jnp.int32, block.shape, 1)
    return jnp.where((lane >= HEAD_DIM) if hb else (lane < HEAD_DIM), block, other)


def _attn_fwd(qs, k, v_t, q_bias, k_bias, t, shards):
    s = qs.shape[0]
    n = s // t
    pairs = [(i, j) for i in range(n) for j in range(i + 1)]
    it = jnp.asarray(np.array([p[0] for p in pairs], np.int32))
    jt = jnp.asarray(np.array([p[1] for p in pairs], np.int32))
    nw = len(shards)
    last = len(pairs) - 1
    mid = (2 * len(pairs)) // 3

    def body(it_ref, jt_ref, q_ref, k_ref, vt_ref, qb_ref, kb_ref, *rest):
        sh, (o_ref, lse_ref, p_ref, pm_ref), got = rest[:nw], rest[nw:nw + 4], rest[nw + 4:2 * nw + 4]
        m_sc, l_sc, acc_sc, send, recv = rest[2 * nw + 4:]
        p = pl.program_id(0)
        i = it_ref[p]
        j = jt_ref[p]
        gather_start, gather_forward, gather_finish = _gather_stages(sh, got, send, recv)
        pl.when(p == 0)(gather_start)
        if mid < last:
            pl.when(p == mid)(gather_forward)

        @pl.when(j == 0)
        def _():
            m_sc[...] = jnp.full_like(m_sc, -1e30)
            l_sc[...] = jnp.ones_like(l_sc)
            acc_sc[...] = jnp.zeros_like(acc_sc)

        pm_ref[...] = jnp.zeros_like(pm_ref)

        def pair_step(pp, diagonal):
            lanes = _pair_lanes(pp)
            kp = k_ref[:, lanes]
            qp = q_ref[:, lanes]
            kb = kb_ref[:, lanes]
            qb = qb_ref[:, lanes]
            for hb in range(2):
                h = 2 * pp + hb
                row = pl.ds(h, 1)
                rows = _head_rows(h)
                st = _dot_nt(_other_head(kp, kb, hb), _other_head(qp, qb, hb))
                if diagonal:
                    kpos = lax.broadcasted_iota(jnp.int32, (t, t), 0)
                    qpos = lax.broadcasted_iota(jnp.int32, (t, t), 1)
                    st = jnp.where(kpos <= qpos, st, -1e30)
                m_prev = m_sc[row, :]
                m_new = jnp.maximum(m_prev, jnp.max(st, axis=0, keepdims=True))
                alpha = jnp.exp(m_prev - m_new)
                pt = jnp.exp(st - m_new)
                l_sc[row, :] = alpha * l_sc[row, :] + jnp.sum(pt, axis=0, keepdims=True)
                ptb = pt.astype(MXU_DTYPE)
                acc_sc[rows, :] = acc_sc[rows, :] * alpha + _dot(vt_ref[rows, :], ptb)
                m_sc[row, :] = m_new
                p_ref[0, h] = ptb
                pm_ref[0, row, :] = m_new

        @pl.when(j < i)
        def _():
            for pp in range(N_HEADS // 2):
                pair_step(pp, False)

        @pl.when(j == i)
        def _():
            for pp in range(N_HEADS // 2):
                pair_step(pp, True)
                sub = lax.broadcasted_iota(jnp.int32, (2 * HEAD_DIM, t), 0)
                l_pair = jnp.where(sub < HEAD_DIM, l_sc[pl.ds(2 * pp, 1), :], l_sc[pl.ds(2 * pp + 1, 1), :])
                o_t = acc_sc[_aligned(pp * 2 * HEAD_DIM, 2 * HEAD_DIM), :] / l_pair
                o_ref[:, _pair_lanes(pp)] = o_t.T.astype(MXU_DTYPE)

            lse_ref[...] = m_sc[...] + jnp.log(l_sc[...])

        @pl.when(p == last)
        def _():
            if mid >= last:
                gather_forward()
            gather_finish()

    gs = pltpu.PrefetchScalarGridSpec(
        num_scalar_prefetch=2, grid=(len(pairs),),
        in_specs=[pl.BlockSpec((t, ATTN_W), lambda p, it_, jt_: (it_[p], 0)),
                  pl.BlockSpec((t, ATTN_W), lambda p, it_, jt_: (jt_[p], 0)),
                  pl.BlockSpec((ATTN_W, t), lambda p, it_, jt_: (0, jt_[p])),
                  pl.BlockSpec((t, ATTN_W), lambda p, it_, jt_: (it_[p], 0)),
                  pl.BlockSpec((t, ATTN_W), lambda p, it_, jt_: (jt_[p], 0))] + [ANY] * nw,
        out_specs=[pl.BlockSpec((t, ATTN_W), lambda p, it_, jt_: (it_[p], 0)),
                   pl.BlockSpec((HEAD_ROWS, t), lambda p, it_, jt_: (0, it_[p])),
                   pl.BlockSpec((1, N_HEADS, t, t), lambda p, it_, jt_: (p, 0, 0, 0)),
                   pl.BlockSpec((1, HEAD_ROWS, t), lambda p, it_, jt_: (p, 0, 0))] + [ANY] * nw,
        scratch_shapes=[pltpu.VMEM((HEAD_ROWS, t), F32), pltpu.VMEM((HEAD_ROWS, t), F32), pltpu.VMEM((ATTN_W, t), F32),
                        pltpu.SemaphoreType.DMA((nw, 6)), pltpu.SemaphoreType.DMA((nw, 6))])
    sd = jax.ShapeDtypeStruct
    o, lse, probs, probs_max, *got = pl.pallas_call(
        body, name="attn_fwd", grid_spec=gs,
        out_shape=[sd((s, ATTN_W), MXU_DTYPE), sd((HEAD_ROWS, s), F32), sd((len(pairs), N_HEADS, t, t), MXU_DTYPE),
                   sd((len(pairs), HEAD_ROWS, t), F32)]
        + [sd((N_CHIPS,) + a.shape, a.dtype) for a in shards],
        compiler_params=_params(("arbitrary",)),
    )(it, jt, qs, k, v_t, q_bias, k_bias, *shards)
    me = 2 * lax.axis_index("x") + lax.axis_index("y")
    return o, lse, probs, probs_max, [lax.dynamic_update_index_in_dim(g, own, me, 0) for g, own in zip(got, shards)]


HALO = 16


def _halo_before(tm):
    return pl.BlockSpec((HALO, CONV_W), lambda i: (jnp.maximum(i * (tm // HALO) - 1, 0), 0))


def _shift_down(cur, prev_ref, first, cols=slice(None)):
    row = lax.broadcasted_iota(jnp.int32, cur.shape, 0)

    def before(r):
        prod = prev_ref[0][r:r + 1, cols].astype(F32) * prev_ref[1][r:r + 1, cols].astype(F32)
        return jnp.where(first, 0.0, prod)

    p7, p6 = before(HALO - 1), before(HALO - 2)
    s1 = jnp.where(row == 0, p7, pltpu.roll(cur, 1, 0))
    s2 = jnp.where(row == 0, p6, jnp.where(row == 1, p7, pltpu.roll(cur, 2, 0)))
    return s1, s2


def _group_ms(v, gmat):
    return _dot_exact(v, gmat, 1) * (1.0 / HEAD_DIM)


def _mixer_fwd(x, o_attn, gate_b, gate_c, u, conv_w, g_attn, g_conv, w_out, g_post, gmat, tm):
    s = x.shape[0]

    def body(x_ref, o_ref, b_ref, c_ref, u_ref, cp_ref, up_ref, cw_ref, ga_ref, gc_ref, wo_ref, gp_ref, gm_ref,
             x2_ref, mg_ref, y_ref, z_ref):
        i = pl.program_id(0)
        gm = gm_ref[0:128, 0:128]
        for lo in range(0, ATTN_W, 128):
            cols = slice(lo, lo + 128)
            cu = c_ref[:, cols].astype(F32) * u_ref[:, cols].astype(F32)
            cu1, cu2 = _shift_down(cu, (cp_ref, up_ref), i == 0, cols)
            z = cw_ref[0:1, cols] * cu2 + cw_ref[1:2, cols] * cu1 + cw_ref[2:3, cols] * cu
            z_ref[:, cols] = z.astype(MXU_DTYPE)
            cv = b_ref[:, cols].astype(F32) * z
            ov = o_ref[:, cols].astype(F32)
            mg_ref[:, cols] = ((ov * lax.rsqrt(_group_ms(ov * ov, gm) + EPS)) * ga_ref[:, cols]).astype(MXU_DTYPE)
            mg_ref[:, ATTN_W + lo:ATTN_W + lo + 128] = (
                (cv * lax.rsqrt(_group_ms(cv * cv, gm) + EPS)) * gc_ref[:, cols]).astype(MXU_DTYPE)
        y = _dot(mg_ref[...], wo_ref[...])
        y_ref[...] = y
        x2_ref[...] = x_ref[...] + (y * _rms(y)) * gp_ref[...]

    halo = _halo_before(tm)
    sd = jax.ShapeDtypeStruct
    return pl.pallas_call(
        body, name="mixer_fwd", grid=(s // tm,),
        in_specs=[_tok(tm, D_MODEL), _tok(tm, 512), _tok(tm, 512), _tok(tm, 512), _tok(tm, 512), halo, halo,
                  _whole((3, 512)), _whole((1, 512)), _whole((1, 512)), _whole((D_MODEL, D_MODEL), single=True),
                  _whole((1, D_MODEL)), _whole((512, 512))],
        out_specs=[_tok(tm, D_MODEL), _tok(tm, D_MODEL), _tok(tm, D_MODEL), _tok(tm, 512)],
        out_shape=[sd((s, D_MODEL), F32), sd((s, D_MODEL), MXU_DTYPE), sd((s, D_MODEL), F32), sd((s, 512), MXU_DTYPE)],
        compiler_params=_params(("arbitrary",)),
    )(x, o_attn, gate_b, gate_c, u, gate_c, u, conv_w, g_attn, g_conv, w_out, g_post, gmat)


def _ffn_fwd(x2, target, g_pre, w_gu, w_dn, g_post, tm):
    s = x2.shape[0]

    def body(x_ref, t_ref, gpre_ref, wgu_ref, wdn_ref, gpost_ref,
             h_ref, g_ref, up_ref, a_ref, ff_ref, dout_ref, loss_ref):
        xv = x_ref[...]
        h = ((xv * _rms(xv)) * gpre_ref[...]).astype(MXU_DTYPE)
        h_ref[...] = h
        ff = jnp.zeros((tm, D_MODEL), F32)
        for j in range(2):
            cols = slice(j * FF_PIECE, (j + 1) * FF_PIECE)
            g = _dot(h, wgu_ref[j])
            up = _dot(h, wgu_ref[2 + j])
            a = ((g * jax.nn.sigmoid(g)) * up).astype(MXU_DTYPE)
            g_ref[:, cols] = g.astype(MXU_DTYPE)
            up_ref[:, cols] = up.astype(MXU_DTYPE)
            a_ref[:, cols] = a
            ff = ff + _dot(a, wdn_ref[j])
        ff_ref[...] = ff
        err = (xv + (ff * _rms(ff)) * gpost_ref[...]) - t_ref[...]
        dout_ref[...] = err * (1.0 / D_MODEL)
        part = jnp.sum(jnp.mean(err * err, axis=-1, keepdims=True), axis=0, keepdims=True)

        @pl.when(pl.program_id(0) == 0)
        def _():
            loss_ref[...] = jnp.zeros_like(loss_ref)

        loss_ref[...] += part

    sd = jax.ShapeDtypeStruct
    return pl.pallas_call(
        body, name="ffn_fwd", grid=(s // tm,),
        in_specs=[_tok(tm, D_MODEL), _tok(tm, D_MODEL), _whole((1, D_MODEL)),
                  _whole((4, D_MODEL, FF_PIECE), single=True), _whole((2, FF_PIECE, D_MODEL), single=True),
                  _whole((1, D_MODEL))],
        out_specs=[_tok(tm, D_MODEL), _tok(tm, D_FF), _tok(tm, D_FF), _tok(tm, D_FF), _tok(tm, D_MODEL),
                   _tok(tm, D_MODEL), _whole((8, 128))],
        out_shape=[sd((s, D_MODEL), MXU_DTYPE), sd((s, D_FF), MXU_DTYPE), sd((s, D_FF), MXU_DTYPE),
                   sd((s, D_FF), MXU_DTYPE), sd((s, D_MODEL), F32), sd((s, D_MODEL), F32), sd((8, 128), F32)],
        compiler_params=_params(("arbitrary",)),
    )(x2, target, g_pre, w_gu, w_dn, g_post)


def _norm_bwd(dy, normed, rinv, gain):
    t = dy * gain
    return rinv * (t - normed * jnp.mean(t * normed, axis=-1, keepdims=True))


def _acc_rows(ref, first, val):
    @pl.when(first)
    def _():
        ref[...] = jnp.zeros_like(ref)

    ref[...] += jnp.sum(val, axis=0, keepdims=True)


def _ffn_bwd(dout, ff, x2, g, up, g_post, g_pre, w_gu, w_dn, tm):
    s = x2.shape[0]

    def body(do_ref, ff_ref, x_ref, g_ref, up_ref, gpost_ref, gpre_ref, wgu_ref, wdn_ref,
             dx_ref, dff_ref, dgu_ref, dgpost_ref, dgpre_ref):
        first = pl.program_id(0) == 0
        ffv = ff_ref[...]
        rf = _rms(ffv)
        n = ffv * rf
        do = do_ref[...]
        _acc_rows(dgpost_ref, first, do * n)
        dff = _norm_bwd(do, n, rf, gpost_ref[...]).astype(MXU_DTYPE)
        dff_ref[...] = dff
        dh = jnp.zeros((tm, D_MODEL), F32)
        for j in range(2):
            cols = slice(j * FF_PIECE, (j + 1) * FF_PIECE)
            da = _dot_nt(dff, wdn_ref[j])
            gv = g_ref[:, cols].astype(F32)
            sg = jax.nn.sigmoid(gv)
            dg = (da * up_ref[:, cols].astype(F32) * (sg * (1.0 + gv * (1.0 - sg)))).astype(MXU_DTYPE)
            du = (da * (gv * sg)).astype(MXU_DTYPE)
            dgu_ref[:, cols] = dg
            dgu_ref[:, D_FF + j * FF_PIECE:D_FF + (j + 1) * FF_PIECE] = du
            dh = dh + _dot_nt(dg, wgu_ref[j]) + _dot_nt(du, wgu_ref[2 + j])
        xv = x_ref[...]
        r2 = _rms(xv)
        nx = xv * r2
        _acc_rows(dgpre_ref, first, dh * nx)
        dx_ref[...] = do + _norm_bwd(dh, nx, r2, gpre_ref[...])

    sd = jax.ShapeDtypeStruct
    return pl.pallas_call(
        body, name="ffn_bwd", grid=(s // tm,),
        in_specs=[_tok(tm, D_MODEL), _tok(tm, D_MODEL), _tok(tm, D_MODEL), _tok(tm, D_FF), _tok(tm, D_FF),
                  _whole((1, D_MODEL)), _whole((1, D_MODEL)),
                  _whole((4, D_MODEL, FF_PIECE), single=True), _whole((2, FF_PIECE, D_MODEL), single=True)],
        out_specs=[_tok(tm, D_MODEL), _tok(tm, D_MODEL), _tok(tm, 2 * D_FF), _whole((1, D_MODEL)),
                   _whole((1, D_MODEL))],
        out_shape=[sd((s, D_MODEL), F32), sd((s, D_MODEL), MXU_DTYPE), sd((s, 2 * D_FF), MXU_DTYPE),
                   sd((1, D_MODEL), F32), sd((1, D_MODEL), F32)],
        compiler_params=_params(("arbitrary",)),
    )(dout, ff, x2, g, up, g_post, g_pre, w_gu, w_dn)


def _tn_matmul(a, b, tm, tn, tk, name, totals=(), piece_rows=None):
    s, m = a.shape
    n = b.shape[1]
    nw = len(totals)
    grid = (m // tm, n // tn, s // tk)
    assert piece_rows is None or tn == n

    def body(a_ref, b_ref, *rest):
        o_ref = rest[nw]
        acc_ref = o_ref if piece_rows is None else rest[-1]
        if nw:
            step = (pl.program_id(0) * grid[1] + pl.program_id(1)) * grid[2] + pl.program_id(2)
            share_start, share_finish = _share_stages(rest[:nw], rest[nw + 1:2 * nw + 1], *rest[2 * nw + 1:2 * nw + 3])
            pl.when(step == 0)(share_start)

        @pl.when(pl.program_id(2) == 0)
        def _():
            acc_ref[...] = jnp.zeros_like(acc_ref)

        acc_ref[...] += lax.dot_general(a_ref[...], b_ref[...], (((0,), (0,)), ((), ())), preferred_element_type=F32)
        if piece_rows is not None:
            for r in range(grid[0]):
                @pl.when((pl.program_id(0) == r) & (pl.program_id(2) == grid[2] - 1))
                def _():
                    for p, at, src, count in _piece_windows(piece_rows, r * tm, (r + 1) * tm):
                        o_ref[p, at:at + count, :] = acc_ref[src - r * tm:src - r * tm + count, :]
        if nw:
            pl.when(step == grid[0] * grid[1] * grid[2] - 1)(share_finish)

    if piece_rows is None:
        o_spec, o_shape = pl.BlockSpec((tm, tn), lambda i, j, kk: (i, j)), (m, n)
    else:
        o_spec, o_shape = pl.BlockSpec((N_CHIPS, piece_rows, n), lambda i, j, kk: (0, 0, 0)), (N_CHIPS, piece_rows, n)
    out = pl.pallas_call(
        body, name=name, grid=grid,
        in_specs=[pl.BlockSpec((tk, tm), lambda i, j, kk: (kk, i)), pl.BlockSpec((tk, tn), lambda i, j, kk: (kk, j))]
        + [ANY] * nw,
        out_specs=[o_spec] + [ANY] * nw,
        out_shape=[jax.ShapeDtypeStruct(o_shape, F32)] + [jax.ShapeDtypeStruct(t.shape, t.dtype) for t in totals],
        scratch_shapes=([pltpu.SemaphoreType.DMA((nw,)), pltpu.SemaphoreType.DMA((nw,))] if nw else [])
        + ([] if piece_rows is None else [pltpu.VMEM((tm, tn), F32)]),
        compiler_params=_params(("arbitrary", "arbitrary", "arbitrary")),
    )(a, b, *totals)
    return (out[0], out[1:]) if nw else out[0]


def _mixer_bwd(dx2, y, o_attn, gate_b, z, g_post, g_attn, g_conv, w_out, gmat, sel, tm, ready, kinds):
    s = dx2.shape[0]
    nw = len(ready)
    nt = s // tm

    def body(d_ref, y_ref, o_ref, b_ref, z_ref, gp_ref, ga_ref, gc_ref, wo_ref, gm_ref, sel_ref, *rest):
        grads = rest[:nw]
        dy_ref, do_ref, db_ref, dz_ref, delta_ref, dgp_ref, dga_ref, dgc_ref = rest[nw:nw + 8]
        taken = rest[nw + 8:2 * nw + 8]
        send, recv = rest[2 * nw + 8:]
        first = pl.program_id(0) == 0
        pair_start, pair_finish = _pair_stages(grads, kinds, taken, send, recv)
        pl.when(first)(pair_start)
        yv = y_ref[...]
        ry = _rms(yv)
        ny = yv * ry
        d = d_ref[...]
        _acc_rows(dgp_ref, first, d * ny)
        dy = _norm_bwd(d, ny, ry, gp_ref[...]).astype(MXU_DTYPE)
        dy_ref[...] = dy
        dm = _dot_nt(dy, wo_ref[...])
        gm = gm_ref[0:128, 0:128]

        @pl.when(first)
        def _():
            dga_ref[...] = jnp.zeros_like(dga_ref)
            dgc_ref[...] = jnp.zeros_like(dgc_ref)

        def group_bwd(val, dmv, gain_ref, dg_ref, cols):
            rg = lax.rsqrt(_group_ms(val * val, gm) + EPS)
            nv = val * rg
            dg_ref[:, cols] += jnp.sum(dmv * nv, axis=0, keepdims=True)
            t = dmv * gain_ref[:, cols]
            return rg * (t - nv * _group_ms(t * nv, gm))

        delta = jnp.zeros((tm, 128), F32)
        for lo in range(0, ATTN_W, 128):
            cols = slice(lo, lo + 128)
            ov = o_ref[:, cols].astype(F32)
            d_o = group_bwd(ov, dm[:, cols], ga_ref, dga_ref, cols)
            do_ref[:, cols] = d_o.astype(MXU_DTYPE)
            delta = delta + _dot_exact(d_o * ov, sel_ref[cols, :], 2)
            zv = z_ref[:, cols].astype(F32)
            bv = b_ref[:, cols].astype(F32)
            d_cv = group_bwd(bv * zv, dm[:, ATTN_W + lo:ATTN_W + lo + 128], gc_ref, dgc_ref, cols)
            db_ref[:, cols] = (d_cv * zv).astype(MXU_DTYPE)
            dz_ref[:, cols] = d_cv * bv
        delta_ref[...] = delta.T[0:HEAD_ROWS, :]
        pl.when(pl.program_id(0) == nt - 1)(pair_finish)

    sd = jax.ShapeDtypeStruct
    taken_shape = [sd((N_CHIPS, g.shape[-2], g.shape[-1] if kd == "rows" else g.shape[-1] // N_CHIPS), F32)
                   for g, kd in zip(ready, kinds)]
    out = pl.pallas_call(
        body, name="mixer_bwd", grid=(nt,),
        in_specs=[_tok(tm, D_MODEL), _tok(tm, D_MODEL), _tok(tm, 512), _tok(tm, 512), _tok(tm, 512),
                  _whole((1, D_MODEL)), _whole((1, 512)), _whole((1, 512)),
                  _whole((D_MODEL, D_MODEL), single=True), _whole((512, 512)), _whole((512, 128))] + [ANY] * nw,
        out_specs=[_tok(tm, D_MODEL), _tok(tm, 512), _tok(tm, 512), _tok(tm, 512), _feat(HEAD_ROWS, tm),
                   _whole((1, D_MODEL)), _whole((1, 512)), _whole((1, 512))] + [ANY] * nw,
        out_shape=[sd((s, D_MODEL), MXU_DTYPE), sd((s, 512), MXU_DTYPE), sd((s, 512), MXU_DTYPE), sd((s, 512), F32),
                   sd((HEAD_ROWS, s), F32), sd((1, D_MODEL), F32), sd((1, 512), F32), sd((1, 512), F32)] + taken_shape,
        scratch_shapes=[pltpu.SemaphoreType.DMA((nw, N_CHIPS)), pltpu.SemaphoreType.DMA((nw, N_CHIPS))],
        compiler_params=_params(("arbitrary",)),
    )(dx2, y, o_attn, gate_b, z, g_post, g_attn, g_conv, w_out, gmat, sel, *ready)
    return out[:8], out[8:]


def _attn_bwd(qs, k_t, v, do, probs, probs_max, lse, delta, t, parts):
    s = qs.shape[0]
    n = s // t
    pairs = [(i, j) for j in range(n) for i in range(j, n)]
    it = jnp.asarray(np.array([p[0] for p in pairs], np.int32))
    jt = jnp.asarray(np.array([p[1] for p in pairs], np.int32))
    ft = jnp.asarray(np.array([p[0] * (p[0] + 1) // 2 + p[1] for p in pairs], np.int32))

    nw = len(parts)

    def body(it_ref, jt_ref, ft_ref, q_ref, kt_ref, v_ref, do_ref, p_ref, pm_ref, lse_ref, dl_ref, *rest):
        pb = rest[:nw]
        dq_ref, dk_ref, dv_ref, dc_ref, dcq_ref = rest[nw:nw + 5]
        rcv = rest[nw + 5:2 * nw + 5]
        dk_sc, dv_sc, dc_sc, send, recv = rest[2 * nw + 5:]
        p = pl.program_id(0)
        i = it_ref[p]
        j = jt_ref[p]
        chip_start, chip_finish = _chip_stages(pb, rcv, send, recv)

        @pl.when(p == 0)
        def _():
            chip_start()
            dq_ref[...] = jnp.zeros_like(dq_ref)
            dcq_ref[...] = jnp.zeros_like(dcq_ref)

        @pl.when(i == j)
        def _():
            dk_sc[...] = jnp.zeros_like(dk_sc)
            dv_sc[...] = jnp.zeros_like(dv_sc)
            dc_sc[...] = jnp.zeros_like(dc_sc)

        def pair_step(pp):
            lanes = _pair_lanes(pp)
            qp = q_ref[:, lanes]
            vp = v_ref[:, lanes]
            dop = do_ref[:, lanes]
            lane = lax.broadcasted_iota(jnp.int32, (t, 128), 1)
            for hb in range(2):
                h = 2 * pp + hb
                row = pl.ds(h, 1)
                pt = p_ref[0, h].astype(F32) * jnp.exp(pm_ref[0, row, :] - lse_ref[row, :])
                dv_sc[:, lanes] += _dot(pt.astype(MXU_DTYPE), _only_head(dop, hb))
                dst = pt * (_dot_nt(_only_head(vp, hb), dop) - dl_ref[row, :])
                dc_sc[...] -= jnp.where(lane == h, jnp.sum(dst, axis=1, keepdims=True), 0.0)
                dcq_ref[i, row, :] += jnp.sum(dst, axis=0, keepdims=True)
                dsb = dst.astype(MXU_DTYPE)
                dk_sc[:, lanes] += _dot(dsb, _only_head(qp, hb))
                rows = _head_rows(h)
                dq_ref[i, rows, :] += _dot(kt_ref[rows, :], dsb)

        for pp in range(N_HEADS // 2):
            pair_step(pp)

        @pl.when(i == n - 1)
        def _():
            dk_ref[...] = dk_sc[...].astype(MXU_DTYPE)
            dv_ref[...] = dv_sc[...].astype(MXU_DTYPE)
            dc_ref[...] = dc_sc[...]

        pl.when(p == len(pairs) - 1)(chip_finish)

    qi = lambda p, it_, jt_, ft_: (it_[p], 0)
    kj = lambda p, it_, jt_, ft_: (jt_[p], 0)
    row_i = lambda p, it_, jt_, ft_: (0, it_[p])
    gs = pltpu.PrefetchScalarGridSpec(
        num_scalar_prefetch=3, grid=(len(pairs),),
        in_specs=[pl.BlockSpec((t, ATTN_W), qi),
                  pl.BlockSpec((ATTN_W, t), lambda p, it_, jt_, ft_: (0, jt_[p])),
                  pl.BlockSpec((t, ATTN_W), kj), pl.BlockSpec((t, ATTN_W), qi),
                  pl.BlockSpec((1, N_HEADS, t, t), lambda p, it_, jt_, ft_: (ft_[p], 0, 0, 0)),
                  pl.BlockSpec((1, HEAD_ROWS, t), lambda p, it_, jt_, ft_: (ft_[p], 0, 0)),
                  pl.BlockSpec((HEAD_ROWS, t), row_i), pl.BlockSpec((HEAD_ROWS, t), row_i)] + [ANY] * nw,
        out_specs=[pl.BlockSpec((n, ATTN_W, t), lambda p, it_, jt_, ft_: (0, 0, 0)),
                   pl.BlockSpec((t, ATTN_W), kj), pl.BlockSpec((t, ATTN_W), kj),
                   pl.BlockSpec((t, 128), kj),
                   pl.BlockSpec((n, HEAD_ROWS, t), lambda p, it_, jt_, ft_: (0, 0, 0))] + [ANY] * nw,
        scratch_shapes=[pltpu.VMEM((t, ATTN_W), F32), pltpu.VMEM((t, ATTN_W), F32),
                        pltpu.VMEM((t, 128), F32), pltpu.SemaphoreType.DMA((nw, 3)), pltpu.SemaphoreType.DMA((nw, 3))])
    sd = jax.ShapeDtypeStruct
    out = pl.pallas_call(
        body, name="attn_bwd", grid_spec=gs,
        out_shape=[sd((n, ATTN_W, t), F32), sd((s, ATTN_W), MXU_DTYPE), sd((s, ATTN_W), MXU_DTYPE),
                   sd((s, 128), F32), sd((n, HEAD_ROWS, t), F32)] + [sd((3,) + a.shape[1:], a.dtype) for a in parts],
        compiler_params=_params(("arbitrary",)),
    )(it, jt, ft, qs, k_t, v, do, probs, probs_max, lse, delta, *parts)
    return out[:5], out[5:]


def _forget_bwd(dc_rows, dc_cols, z_t, b_col):
    s = z_t.shape[1]
    nb = s // 128

    def body(dr_ref, dcc_ref, z_ref, b_ref, dz_ref, db_ref):
        lower = _tri(128, False)
        real = lax.broadcasted_iota(jnp.int32, (HEAD_ROWS, 128), 0) < N_HEADS

        tail = jnp.zeros((HEAD_ROWS, 1), F32)
        dbias = jnp.zeros((HEAD_ROWS, 1), F32)
        for m in range(nb):
            off = (nb - 1 - m) * 128
            dc = dr_ref[:, off:off + 128] + dcc_ref[off:off + 128, :].T[0:HEAD_ROWS, :]
            dlf = _dot_exact(dc, lower, 3) + tail
            dz = dlf * jax.nn.sigmoid(-(z_ref[0:HEAD_ROWS, off:off + 128] + b_ref[...]))
            dz = jnp.where(real, dz, 0.0)
            dz_ref[off:off + 128, :] = _rows_to_cols(dz)
            tail = tail + jnp.sum(dc, axis=1, keepdims=True)
            dbias = dbias + jnp.sum(dz, axis=1, keepdims=True)
        db_ref[...] = jnp.broadcast_to(dbias, db_ref.shape)

    return pl.pallas_call(
        body, name="forget_bwd",
        out_shape=[jax.ShapeDtypeStruct((s, 128), F32), jax.ShapeDtypeStruct((HEAD_ROWS, 128), F32)],
        compiler_params=_params())(dc_rows, dc_cols, z_t, b_col)


def _inproj_bwd(dz, gate_c, u, conv_w, dq, dk, dv, dzf, db, x, dx2, g_pre, w_t, tm):
    s = x.shape[0]
    nt = s // tm
    t = dq.shape[2]
    assert t % tm == 0 and dq.shape[:2] == (s // t, ATTN_W)
    per = t // tm

    def body(dz_ref, dzn_ref, c_ref, u_ref, cp_ref, up_ref, cw_ref, dq_ref, dk_ref, dv_ref, dzf_ref, db_ref,
             x_ref, dx2_ref, g_ref, w_ref, gx_ref, dp_ref, dg_ref, dcw_ref):
        i = pl.program_id(0)
        first = i == 0
        last = i == nt - 1
        @pl.when(first)
        def _():
            dcw_ref[...] = jnp.zeros_like(dcw_ref)

        for lo in range(0, CONV_W, 128):
            cols = slice(lo, lo + 128)
            dzv = dz_ref[:, cols]
            row = lax.broadcasted_iota(jnp.int32, dzv.shape, 0)
            n0 = jnp.where(last, 0.0, dzn_ref[0:1, cols])
            n1 = jnp.where(last, 0.0, dzn_ref[1:2, cols])
            dz1 = jnp.where(row == tm - 1, n0, pltpu.roll(dzv, tm - 1, 0))
            dz2 = jnp.where(row == tm - 1, n1, jnp.where(row == tm - 2, n0, pltpu.roll(dzv, tm - 2, 0)))
            dcu = cw_ref[2:3, cols] * dzv + cw_ref[1:2, cols] * dz1 + cw_ref[0:1, cols] * dz2
            cv = c_ref[:, cols].astype(F32)
            uv = u_ref[:, cols].astype(F32)
            cu = cv * uv
            cu1, cu2 = _shift_down(cu, (cp_ref, up_ref), first, cols)
            dcw_ref[0:1, cols] += jnp.sum(dzv * cu2, axis=0, keepdims=True)
            dcw_ref[1:2, cols] += jnp.sum(dzv * cu1, axis=0, keepdims=True)
            dcw_ref[2:3, cols] += jnp.sum(dzv * cu, axis=0, keepdims=True)
            dp_ref[:, OFF_C + lo:OFF_C + lo + 128] = (dcu * uv).astype(MXU_DTYPE)
            dp_ref[:, OFF_U + lo:OFF_U + lo + 128] = (dcu * cv).astype(MXU_DTYPE)

        dp_ref[:, 0:512] = (dq_ref[0].T * Q_SCALE).astype(MXU_DTYPE)
        dp_ref[:, 512:1024] = dk_ref[...].astype(MXU_DTYPE)
        dp_ref[:, 1024:OFF_F] = dv_ref[...].astype(MXU_DTYPE)
        dp_ref[:, OFF_F:OFF_B] = dzf_ref[...].astype(MXU_DTYPE)
        dp_ref[:, OFF_B:OFF_C] = db_ref[...].astype(MXU_DTYPE)
        dh = _dot(dp_ref[...], w_ref[...])
        xv = x_ref[...]
        r1 = _rms(xv)
        nx = xv * r1
        _acc_rows(dg_ref, first, dh * nx)
        gx_ref[...] = dx2_ref[...] + _norm_bwd(dh, nx, r1, g_ref[...])

    prev = _halo_before(tm)
    nxt = pl.BlockSpec((8, 512), lambda i: (jnp.minimum((i + 1) * (tm // 8), s // 8 - 1), 0))
    sd = jax.ShapeDtypeStruct
    return pl.pallas_call(
        body, name="inproj_bwd", grid=(nt,),
        in_specs=[_tok(tm, 512), nxt, _tok(tm, 512), _tok(tm, 512), prev, prev, _whole((3, 512)),
                  pl.BlockSpec((1, ATTN_W, tm), lambda i: (i // per, 0, i % per)), _tok(tm, 512), _tok(tm, 512),
                  _tok(tm, 128),
                  _tok(tm, 512),
                  _tok(tm, D_MODEL), _tok(tm, D_MODEL), _whole((1, D_MODEL)), _whole((IN_PAD, D_MODEL), single=True)],
        out_specs=[_tok(tm, D_MODEL), _tok(tm, IN_PAD), _whole((1, D_MODEL)), _whole((8, 512))],
        out_shape=[sd((s, D_MODEL), F32), sd((s, IN_PAD), MXU_DTYPE), sd((1, D_MODEL), F32), sd((8, 512), F32)],
        compiler_params=_params(("arbitrary",)),
    )(dz, dz, gate_c, u, gate_c, u, conv_w, dq, dk, dv, dzf, db, x, dx2, g_pre, w_t)


def _tile(s, want):
    return want if s % want == 0 else s


def _halves(a):
    return a.reshape(2, a.shape[0] // 2, a.shape[1])


def _device_step(x, target, w, mom1, mom2, w_in_t, m_in_t, v_in_t, c_idx, me_idx):
    s = x.shape[0]
    tm = _tile(s, 512)
    tf = _tile(s, 256)
    ta = _tile(s, 512)
    tkk = _tile(s, 2048)
    gidx = np.arange(512) // HEAD_DIM
    gmat = jnp.asarray(gidx[:, None] == gidx[None, :], MXU_DTYPE)
    sel = jnp.asarray(gidx[:, None] == np.arange(128)[None, :], MXU_DTYPE)
    g_mix_pre, g_mix_post, g_ffn_pre, g_ffn_post = w["g_mix_pre"], w["g_mix_post"], w["g_ffn_pre"], w["g_ffn_post"]
    g_attn, g_conv, b_forget = w["g_attn_out"], w["g_conv_out"], w["b_forget"]
    shard = {n: _halves(w[n][0].astype(MXU_DTYPE)) for n in BIG[1:]}
    piece_rows = IN_W // N_CHIPS

    g_in, conv_all = _gather_weights([w_in_t.reshape(piece_rows, D_MODEL).astype(MXU_DTYPE)], w["conv_w"][0])
    conv_w = jnp.transpose(conv_all, (1, 0, 2)).reshape(3, CONV_W)

    h1, qs, k, v, k_t, v_t, z_t, gate_b, gate_c, u, w_t = _inproj_fwd(x, g_mix_pre, g_in, tm)
    b_col = jnp.pad(jnp.transpose(b_forget), ((0, HEAD_ROWS - N_HEADS), (0, 0)))
    q_bias, k_bias = _forget_fwd(z_t, b_col)
    o_attn, lse, probs, probs_max, (g_out, g_gu, g_dn) = _attn_fwd(
        qs, k, v_t, q_bias, k_bias, ta, [shard["w_out"], shard["w_gate_up"], shard["w_down"]])
    w_out = g_out.reshape(D_MODEL, D_MODEL)
    w_gu = g_gu.reshape(N_CHIPS, D_MODEL, FF_PIECE)
    w_dn = g_dn.reshape(2, FF_PIECE, D_MODEL)
    x2, merged, y, z = _mixer_fwd(x, o_attn, gate_b, gate_c, u, conv_w, g_attn, g_conv, w_out, g_mix_post, gmat, tm)
    h2, g, up, a, ff, dout, loss_acc = _ffn_fwd(x2, target, g_ffn_pre, w_gu, w_dn, g_ffn_post, tf)

    dx2, dff, dgu, dg_ffn_post, dg_ffn_pre = _ffn_bwd(dout, ff, x2, g, up, g_ffn_post, g_ffn_pre, w_gu, w_dn, tf)
    dw_dn = _tn_matmul(a, dff, FF_PIECE, 1024, tkk, "dw_down").reshape(N_CHIPS, 2, D_FF // (2 * N_CHIPS), D_MODEL)
    dw_gu = _tn_matmul(h2, dgu, 1024, FF_PIECE, tkk, "dw_gate_up").reshape(2, D_MODEL // 2, 2 * D_FF)
    (dy, d_o, d_b, dz, delta, dg_mix_post, dg_attn, dg_conv), (a_gu, a_dn) = _mixer_bwd(
        dx2, y, o_attn, gate_b, z, g_mix_post, g_attn, g_conv, w_out, gmat, sel, tm, [dw_gu, dw_dn], ["cols", "rows"])
    dw_out = _tn_matmul(merged, dy, 1024, 1024, tkk, "dw_out").reshape(N_CHIPS, 2, D_MODEL // (2 * N_CHIPS), D_MODEL)
    place = jnp.concatenate([c_idx, me_idx])
    *sum_gu, a_out = _pair_sum(place, dw_gu, "cols", a_gu, "pair_sum_w_gate_up", [dw_out], ["rows"])
    sum_dn = _pair_sum(place, dw_dn, "rows", a_dn, "pair_sum_w_down")
    sum_out = _pair_sum(place, dw_out, "rows", a_out, "pair_sum_w_out")
    (dq_t, dk, dv, dc_cols, dcq), (r_gu, r_dn, r_out) = _attn_bwd(
        qs, k_t, v, d_o, probs, probs_max, lse, delta, ta, [sum_gu[1], sum_dn[1], sum_out[1]])
    dc_rows = jnp.transpose(dcq, (1, 0, 2)).reshape(HEAD_ROWS, s)
    dzf, db_f = _forget_bwd(dc_rows, dc_cols, z_t, b_col)
    grad_x, dproj, dg_mix_pre, dcw = _inproj_bwd(dz, gate_c, u, conv_w, dq_t, dk, dv, dzf, d_b,
                                                 x, dx2, g_mix_pre, w_t, tm)
    done = [_chip_sum(sb[0], r, "chip_sum_" + n)
            for n, sb, r in zip(BIG[1:], (sum_out, sum_gu, sum_dn), (r_out, r_gu, r_dn))]
    dw_in, done_theirs = _tn_matmul(dproj, h1, 640, 1024, tkk, "dw_in", done, piece_rows)

    (a_in,) = _pair_exchange([dw_in], ["lanes"])
    sum_in = _pair_sum(place, dw_in, "lanes", a_in, "pair_sum_w_in")
    small = dict(b_forget=db_f[:N_HEADS, 0], g_attn_out=dg_attn, g_conv_out=dg_conv, g_mix_pre=dg_mix_pre,
                 g_mix_post=dg_mix_post, g_ffn_pre=dg_ffn_pre, g_ffn_post=dg_ffn_post)
    (r_in,), small_all = _chip_exchange([sum_in[1]], _pack_small(small, dcw[:3], loss_acc[0, 0]))
    t_in = _chip_sum(sum_in[0], r_in, "chip_sum_w_in")
    (s_in,) = _pair_share([t_in], "pair_share_w_in")
    new = {"w_in": _adamw_lanes(c_idx, w_in_t, t_in, s_in, m_in_t, v_in_t, "adamw_w_in")}
    for n, mine, theirs in zip(BIG[1:], done, done_theirs):
        new[n] = _adamw(c_idx, w[n][0], mine, theirs, mom1[n][0], mom2[n][0], 2, "adamw_" + n)
    return grad_x, new, small_all


BIG = ("w_in", "w_out", "w_gate_up", "w_down")
ANY = pl.BlockSpec(memory_space=pl.ANY)


def _place():
    x, y, c = lax.axis_index("x"), lax.axis_index("y"), lax.axis_index("c")
    others = [(1 - x, y), (x, 1 - y), (1 - x, 1 - y)]
    return x, y, c, 2 * x + y, others, [2 * px + py for px, py in others]


def _remote(src, dst, send, recv, dev):
    return pltpu.make_async_remote_copy(src_ref=src, dst_ref=dst, send_sem=send, recv_sem=recv,
                                        device_id=dev, device_id_type=MESH_ID)


def _gather_stages(sh, outs, send, recv):
    x, y, c, me, others, chips = _place()
    sib = (x, y, 1 - c)
    every = [(w, kk) for w in range(len(sh)) for kk in range(3)]

    def half_of(ref, half, piece=None):
        ref = ref if piece is None else ref.at[piece]
        if len(ref.shape) == 3:
            return ref.at[half]
        hc = ref.shape[1] // 2
        return ref.at[:, pl.ds(pl.multiple_of(half * hc, 128), hc)]

    def first(w, kk):
        return _remote(half_of(sh[w], c), half_of(outs[w], c, me), send.at[w, kk], recv.at[w, kk], (*others[kk], c))

    def landed(w, kk):
        r = half_of(outs[w], c, chips[kk])
        return _remote(r, r, send.at[w, kk], recv.at[w, kk], (*others[kk], c))

    def onward(w, kk, half):
        r = half_of(outs[w], half, chips[kk])
        return _remote(r, r, send.at[w, 3 + kk], recv.at[w, 3 + kk], sib)

    def start():
        for w, kk in every:
            first(w, kk).start()

    def forward():
        for w, kk in every:
            landed(w, kk).wait_recv()
            onward(w, kk, c).start()

    def finish():
        for w, kk in every:
            onward(w, kk, 1 - c).wait_recv()
        for w, kk in every:
            first(w, kk).wait_send()
            onward(w, kk, c).wait_send()

    return start, forward, finish


def _pair_piece(ref, kind, p, half):
    if kind == "rows":
        return ref.at[p, half]
    if kind == "lanes":
        hc = ref.shape[2] // 2
        return ref.at[p, :, pl.ds(pl.multiple_of(half * hc, 128), hc)]
    cols = ref.shape[2] // N_CHIPS
    return ref.at[half, :, pl.ds(p * cols, cols)]


def _pair_stages(g, kinds, a, send, recv):
    x, y, c, _, _, _ = _place()
    copies = [_remote(_pair_piece(g[w], kinds[w], p, 1 - c), a[w].at[p], send.at[w, p], recv.at[w, p], (x, y, 1 - c))
              for w in range(len(g)) for p in range(N_CHIPS)]

    def start():
        for cp in copies:
            cp.start()

    def finish():
        for cp in copies:
            cp.wait()

    return start, finish


def _chip_stages(pb, rcv, send, recv):
    x, y, c, _, others, chips = _place()
    copies = [_remote(pb[w].at[chips[kk]], rcv[w].at[kk], send.at[w, kk], recv.at[w, kk], (*others[kk], c))
              for w in range(len(pb)) for kk in range(3)]

    def start():
        for cp in copies:
            cp.start()

    def finish():
        for cp in copies:
            cp.wait()

    return start, finish


def _gather_weights(shards, conv_w):
    n = len(shards)

    def body(*refs):
        sh, cw, outs, cwo = refs[:n], refs[n], refs[n + 1:2 * n + 1], refs[2 * n + 1]
        send, recv = refs[2 * n + 2:]
        x, y, c, me, others, chips = _place()
        start, forward, finish = _gather_stages(sh, outs, send, recv)
        start()
        small = [_remote(cw, cwo.at[me], send.at[n, kk], recv.at[n, kk], (*others[kk], c)) for kk in range(3)]
        for cp in small:
            cp.start()
        forward()
        for kk in range(3):
            _remote(cw, cwo.at[chips[kk]], send.at[n, kk], recv.at[n, kk], (*others[kk], c)).wait_recv()
        finish()
        for cp in small:
            cp.wait_send()

    out_shape = [jax.ShapeDtypeStruct((N_CHIPS,) + s.shape, s.dtype) for s in shards]
    out_shape.append(jax.ShapeDtypeStruct((N_CHIPS,) + conv_w.shape, conv_w.dtype))
    got = pl.pallas_call(
        body, name="gather_weights", in_specs=[ANY] * (n + 1), out_specs=[ANY] * (n + 1), out_shape=out_shape,
        scratch_shapes=[pltpu.SemaphoreType.DMA((n + 1, 6)), pltpu.SemaphoreType.DMA((n + 1, 6))],
    )(*shards, conv_w)
    me = 2 * lax.axis_index("x") + lax.axis_index("y")
    return [lax.dynamic_update_index_in_dim(g, own, me, 0) for g, own in zip(got, list(shards) + [conv_w])]


def _taken_shape(g, kind):
    if kind == "rows":
        return (N_CHIPS,) + g.shape[2:]
    if kind == "lanes":
        return g.shape[:2] + (g.shape[2] // 2,)
    return (N_CHIPS, g.shape[1], g.shape[2] // N_CHIPS)


def _pair_sum(place, g, kind, a, name, ready=(), ready_kinds=()):
    _, half, cols = a.shape
    nw = len(ready)
    if kind == "rows":
        mine = pl.BlockSpec((1, 1, half, cols), lambda p, pr: (p, pr[0], 0, 0))
    elif kind == "lanes":
        mine = pl.BlockSpec((1, half, cols), lambda p, pr: (p, 0, pr[0]))
    else:
        mine = pl.BlockSpec((1, half, cols), lambda p, pr: (pr[0], 0, p))

    def body(place_ref, g_ref, a_ref, *rest):
        grads, (own_ref, pb_ref), taken = rest[:nw], rest[nw:nw + 2], rest[nw + 2:2 * nw + 2]
        if nw:
            pair_start, pair_finish = _pair_stages(grads, ready_kinds, taken, *rest[2 * nw + 2:])
            pl.when(pl.program_id(0) == 0)(pair_start)
        tot = (g_ref[0, 0] if kind == "rows" else g_ref[0]) + a_ref[0]
        pb_ref[0] = tot.astype(BF16)

        @pl.when(pl.program_id(0) == place_ref[1])
        def _():
            own_ref[...] = tot

        if nw:
            pl.when(pl.program_id(0) == N_CHIPS - 1)(pair_finish)

    sems = [pltpu.SemaphoreType.DMA((nw, N_CHIPS)), pltpu.SemaphoreType.DMA((nw, N_CHIPS))] if nw else []
    gs = pltpu.PrefetchScalarGridSpec(
        num_scalar_prefetch=1, grid=(N_CHIPS,),
        in_specs=[mine, pl.BlockSpec((1, half, cols), lambda p, pr: (p, 0, 0))] + [ANY] * nw,
        out_specs=[pl.BlockSpec((half, cols), lambda p, pr: (0, 0)),
                   pl.BlockSpec((1, half, cols), lambda p, pr: (p, 0, 0))] + [ANY] * nw,
        scratch_shapes=sems)
    out = pl.pallas_call(
        body, name=name, grid_spec=gs,
        out_shape=[jax.ShapeDtypeStruct((half, cols), F32), jax.ShapeDtypeStruct((N_CHIPS, half, cols), BF16)]
        + [jax.ShapeDtypeStruct(_taken_shape(r, kd), r.dtype) for r, kd in zip(ready, ready_kinds)],
        compiler_params=_params(("arbitrary",)),
    )(place, g, a, *ready)
    return list(out)


def _chip_sum(own, rcv, name):
    half, cols = own.shape

    def body(o_ref, r_ref, t_ref):
        t_ref[...] = ((o_ref[...] + r_ref[0].astype(F32)) + r_ref[1].astype(F32)) + r_ref[2].astype(F32)

    return pl.pallas_call(
        body, name=name, grid=(1,),
        in_specs=[pl.BlockSpec((half, cols), lambda i: (0, 0)), pl.BlockSpec((3, half, cols), lambda i: (0, 0, 0))],
        out_specs=pl.BlockSpec((half, cols), lambda i: (0, 0)),
        out_shape=jax.ShapeDtypeStruct((half, cols), F32), compiler_params=_params(("arbitrary",)),
    )(own, rcv)


def _small_stages(sm, smg, send, recv):
    x, y, c, _, _, _ = _place()

    def peer(r):
        return (1 - x if r & 4 else x, 1 - y if r & 2 else y, 1 - c if r & 1 else c)

    mine = 4 * x + 2 * y + c
    copies = [_remote(sm, smg.at[mine], send.at[r - 1], recv.at[r - 1], peer(r)) for r in range(1, 8)]

    def start():
        for cp in copies:
            cp.start()

    def finish():
        for r in range(1, 8):
            px, py, pc = peer(r)
            _remote(sm, smg.at[4 * px + 2 * py + pc], send.at[r - 1], recv.at[r - 1], (px, py, pc)).wait_recv()
        for cp in copies:
            cp.wait_send()

    return start, finish


def _pair_exchange(grads, kinds):
    n = len(grads)

    def body(*refs):
        start, finish = _pair_stages(refs[:n], kinds, refs[n:2 * n], *refs[2 * n:])
        start()
        finish()

    return pl.pallas_call(
        body, name="pair_exchange", in_specs=[ANY] * n, out_specs=[ANY] * n,
        out_shape=[jax.ShapeDtypeStruct(_taken_shape(g, kd), g.dtype) for g, kd in zip(grads, kinds)],
        scratch_shapes=[pltpu.SemaphoreType.DMA((n, N_CHIPS)), pltpu.SemaphoreType.DMA((n, N_CHIPS))],
    )(*grads)


def _chip_exchange(parts, small):
    n = len(parts)

    def body(*refs):
        pb, sm, rcv, smg = refs[:n], refs[n], refs[n + 1:2 * n + 1], refs[2 * n + 1]
        send, recv, ssend, srecv = refs[2 * n + 2:]
        chip_start, chip_finish = _chip_stages(pb, rcv, send, recv)
        small_start, small_finish = _small_stages(sm, smg, ssend, srecv)
        chip_start()
        small_start()
        chip_finish()
        small_finish()

    out_shape = [jax.ShapeDtypeStruct((3,) + p.shape[1:], p.dtype) for p in parts]
    out_shape.append(jax.ShapeDtypeStruct((8,) + small.shape, small.dtype))
    *arrived, small_land = pl.pallas_call(
        body, name="chip_exchange", in_specs=[ANY] * (n + 1), out_specs=[ANY] * (n + 1), out_shape=out_shape,
        scratch_shapes=[pltpu.SemaphoreType.DMA((n, 3)), pltpu.SemaphoreType.DMA((n, 3)),
                        pltpu.SemaphoreType.DMA((7,)), pltpu.SemaphoreType.DMA((7,))],
    )(*parts, small)
    mine = 4 * lax.axis_index("x") + 2 * lax.axis_index("y") + lax.axis_index("c")
    return arrived, lax.dynamic_update_index_in_dim(small_land, small, mine, 0)


def _share_stages(t, g, send, recv):
    x, y, c, _, _, _ = _place()
    copies = [_remote(t[w], g[w], send.at[w], recv.at[w], (x, y, 1 - c)) for w in range(len(t))]

    def start():
        for cp in copies:
            cp.start()

    def finish():
        for cp in copies:
            cp.wait()

    return start, finish


def _pair_share(totals, name):
    n = len(totals)

    def body(*refs):
        start, finish = _share_stages(refs[:n], refs[n:2 * n], *refs[2 * n:])
        start()
        finish()

    return pl.pallas_call(
        body, name=name, in_specs=[ANY] * n, out_specs=[ANY] * n,
        out_shape=[jax.ShapeDtypeStruct(t.shape, t.dtype) for t in totals],
        scratch_shapes=[pltpu.SemaphoreType.DMA((n,)), pltpu.SemaphoreType.DMA((n,))],
    )(*totals)


def _adamw_math(w, g, m, v):
    m = ADAM_B1 * m + (1.0 - ADAM_B1) * g
    v = ADAM_B2 * v + (1.0 - ADAM_B2) * (g * g)
    m_hat = m / (1.0 - ADAM_B1 ** ADAM_STEP)
    v_hat = v / (1.0 - ADAM_B2 ** ADAM_STEP)
    delta = -ADAM_LR * (m_hat / (jnp.sqrt(v_hat) + ADAM_EPS) + ADAM_WD * w)
    return delta, m, v


def _adamw(c_idx, w, mine, theirs, m, v, nb, name):
    rows, cols = w.shape
    tr = rows // (2 * nb)

    def body(c_ref, w_ref, a_ref, b_ref, m_ref, v_ref, g_ref, d_ref, nm_ref, nv_ref):
        g = jnp.where(pl.program_id(0) == c_ref[0], a_ref[...], b_ref[...])
        g_ref[...] = g
        d_ref[...], nm_ref[...], nv_ref[...] = _adamw_math(w_ref[...], g, m_ref[...], v_ref[...])

    full = pl.BlockSpec((tr, cols), lambda hh, i, cr: (hh * nb + i, 0))
    half = pl.BlockSpec((tr, cols), lambda hh, i, cr: (i, 0))
    gs = pltpu.PrefetchScalarGridSpec(num_scalar_prefetch=1, grid=(2, nb), in_specs=[full, half, half, full, full],
                                      out_specs=[full] * 4)
    return pl.pallas_call(
        body, name=name, grid_spec=gs, out_shape=[jax.ShapeDtypeStruct((rows, cols), F32)] * 4,
        compiler_params=_params(("arbitrary", "arbitrary")),
    )(c_idx, w, mine, theirs, m, v)


def _adamw_lanes(c_idx, w, mine, theirs, m, v, name):
    rows, _, cols = w.shape
    hc = cols // 2

    def body(c_ref, w_ref, a_ref, b_ref, m_ref, v_ref, g_ref, d_ref, nm_ref, nv_ref):
        g = jnp.where(pl.program_id(0) == c_ref[0], a_ref[...], b_ref[...])
        g_ref[:, 0, :] = g
        d_ref[:, 0, :], nm_ref[:, 0, :], nv_ref[:, 0, :] = _adamw_math(w_ref[:, 0, :], g, m_ref[:, 0, :], v_ref[:, 0, :])

    full = pl.BlockSpec((rows, 1, hc), lambda hh, cr: (0, 0, hh))
    half = pl.BlockSpec((rows, hc), lambda hh, cr: (0, 0))
    gs = pltpu.PrefetchScalarGridSpec(num_scalar_prefetch=1, grid=(2,), in_specs=[full, half, half, full, full],
                                      out_specs=[full] * 4)
    return pl.pallas_call(
        body, name=name, grid_spec=gs, out_shape=[jax.ShapeDtypeStruct((rows, 1, cols), F32)] * 4,
        compiler_params=_params(("arbitrary",)),
    )(c_idx, w, mine, theirs, m, v)


SMALL = ("g_mix_pre", "g_mix_post", "g_ffn_pre", "g_ffn_post")
SMALL_ALL = SMALL + ("g_attn_out", "g_conv_out", "conv_w", "b_forget")
SMALL_AT = {"g_mix_pre": (0, 0, 1024), "g_mix_post": (1, 0, 1024), "g_ffn_pre": (2, 0, 1024),
            "g_ffn_post": (3, 0, 1024), "g_attn_out": (4, 0, 512), "g_conv_out": (4, 512, 512),
            "b_forget": (7, 0, N_HEADS)}
CONV_AT = ((5, 0), (5, 512), (6, 0))
LOSS_AT = (6, 512)


def _pack_small(t, conv_full, loss_sum):
    conv = jnp.concatenate([conv_full.reshape(1, 3 * CONV_W), loss_sum.reshape(1, 1),
                            jnp.zeros((1, 2048 - 3 * CONV_W - 1), F32)], axis=1).reshape(2, 1024)
    return jnp.concatenate([t[n].reshape(1, 1024) for n in SMALL]
                           + [jnp.concatenate([t["g_attn_out"].reshape(1, 512), t["g_conv_out"].reshape(1, 512)], axis=1),
                              conv, jnp.pad(t["b_forget"].reshape(1, N_HEADS), ((0, 0), (0, 1024 - N_HEADS)))], axis=0)


def _small_update(me_idx, gathered, w, m, v):
    def body(me_ref, gg_ref, *refs):
        k = len(SMALL_ALL)
        w_refs, m_refs, v_refs = refs[:k], refs[k:2 * k], refs[2 * k:3 * k]
        loss_ref = refs[3 * k]
        outs = refs[3 * k + 1:3 * k + 1 + 4 * k]
        sums = refs[-1]
        g = gg_ref[0]
        for dev in range(1, 8):
            g = g + gg_ref[dev]
        sums[...] = g
        loss_ref[...] = sums[LOSS_AT[0]:LOSS_AT[0] + 1, LOSS_AT[1]:LOSS_AT[1] + 1]
        mine = pl.multiple_of(me_ref[0] * 128, 128)
        for idx, name in enumerate(SMALL_ALL):
            g_ref, d_ref, nm_ref, nv_ref = outs[4 * idx:4 * idx + 4]
            if name == "conv_w":
                for r, (row, lo) in enumerate(CONV_AT):
                    gr = sums[row:row + 1, pl.ds(lo + mine, 128)]
                    g_ref[r] = gr
                    d_ref[r], nm_ref[r], nv_ref[r] = _adamw_math(w_refs[idx][r], gr, m_refs[idx][r], v_refs[idx][r])
            else:
                row, lo, n = SMALL_AT[name]
                gr = sums[row:row + 1, lo:lo + n]
                g_ref[...] = gr
                d_ref[...], nm_ref[...], nv_ref[...] = _adamw_math(w_refs[idx][...], gr, m_refs[idx][...],
                                                                    v_refs[idx][...])

    def whole(a):
        nd = a.ndim
        return pl.BlockSpec(a.shape, lambda i, mr: (0,) * nd)

    rows_first = lambda n, a: jnp.transpose(a, (1, 0, 2)) if n == "conv_w" else a
    ins = [rows_first(n, t[n]) for t in (w, m, v) for n in SMALL_ALL]
    out_shape = [jax.ShapeDtypeStruct((1, 1), F32)]
    for n in SMALL_ALL:
        out_shape += [jax.ShapeDtypeStruct(rows_first(n, w[n]).shape, F32)] * 4
    gs = pltpu.PrefetchScalarGridSpec(
        num_scalar_prefetch=1, grid=(1,), in_specs=[whole(gathered)] + [whole(a) for a in ins],
        out_specs=[whole(o) for o in out_shape], scratch_shapes=[pltpu.VMEM((8, 1024), F32)])
    out = pl.pallas_call(body, name="small_update", grid_spec=gs, out_shape=out_shape,
                         compiler_params=_params(("arbitrary",)))(me_idx, gathered, *ins)
    return out[0], {n: [rows_first(n, r) for r in out[1 + 4 * i:5 + 4 * i]] for i, n in enumerate(SMALL_ALL)}


def kernel(x, w_in, b_forget, conv_w, g_attn_out, g_conv_out, w_out, g_mix_pre, g_mix_post, w_gate_up, w_down, g_ffn_pre, g_ffn_post, loss_target, m_w_in, m_b_forget, m_conv_w, m_g_attn_out, m_g_conv_out, m_w_out, m_g_mix_pre, m_g_mix_post, m_w_gate_up, m_w_down, m_g_ffn_pre, m_g_ffn_post, v_w_in, v_b_forget, v_conv_w, v_g_attn_out, v_g_conv_out, v_w_out, v_g_mix_pre, v_g_mix_post, v_w_gate_up, v_w_down, v_g_ffn_pre, v_g_ffn_post):
    w = dict(w_in=w_in, b_forget=b_forget, conv_w=conv_w, g_attn_out=g_attn_out, g_conv_out=g_conv_out, w_out=w_out,
             g_mix_pre=g_mix_pre, g_mix_post=g_mix_post, w_gate_up=w_gate_up, w_down=w_down, g_ffn_pre=g_ffn_pre,
             g_ffn_post=g_ffn_post)
    m = dict(w_in=m_w_in, b_forget=m_b_forget, conv_w=m_conv_w, g_attn_out=m_g_attn_out, g_conv_out=m_g_conv_out,
             w_out=m_w_out, g_mix_pre=m_g_mix_pre, g_mix_post=m_g_mix_post, w_gate_up=m_w_gate_up, w_down=m_w_down,
             g_ffn_pre=m_g_ffn_pre, g_ffn_post=m_g_ffn_post)
    v = dict(w_in=v_w_in, b_forget=v_b_forget, conv_w=v_conv_w, g_attn_out=v_g_attn_out, g_conv_out=v_g_conv_out,
             w_out=v_w_out, g_mix_pre=v_g_mix_pre, g_mix_post=v_g_mix_post, w_gate_up=v_w_gate_up, w_down=v_w_down,
             g_ffn_pre=v_g_ffn_pre, g_ffn_post=v_g_ffn_post)
    cx, cy, cc = lax.axis_index("x"), lax.axis_index("y"), lax.axis_index("c")
    me = 2 * cx + cy
    c_idx = cc.astype(jnp.int32).reshape(1)
    me_idx = me.astype(jnp.int32).reshape(1)

    stored = lambda a: jnp.transpose(a, (2, 0, 1))
    grad_x, big, small_all = _device_step(x[0], loss_target[0], w, m, v, stored(w_in), stored(m_w_in),
                                          stored(v_w_in), c_idx, me_idx)
    gsum, delta, new_m, new_v = {}, {}, {}, {}
    for n in BIG:
        back = (lambda r: jnp.transpose(r, (1, 2, 0))) if n == "w_in" else (lambda r: r[None])
        gsum[n], delta[n], new_m[n], new_v[n] = [back(r) for r in big[n]]
    loss_sum, small_new = _small_update(me_idx, small_all, w, m, v)
    for n in SMALL_ALL:
        gsum[n], delta[n], new_m[n], new_v[n] = small_new[n]
    loss = 0.5 * loss_sum[0, 0]

    order = ("w_in", "b_forget", "conv_w", "g_attn_out", "g_conv_out", "w_out", "g_mix_pre", "g_mix_post",
             "w_gate_up", "w_down", "g_ffn_pre", "g_ffn_post")
    return (loss, grad_x[None], *[gsum[n] for n in order], *[delta[n] for n in order],
            *[new_m[n] for n in order], *[new_v[n] for n in order])
```

```python
import jax
import jax.numpy as jnp
import numpy as np
from jax import lax
from jax.experimental import pallas as pl
from jax.experimental.pallas import tpu as pltpu

F32 = jnp.float32
BF16 = jnp.bfloat16
MXU_DTYPE = jnp.bfloat16

D_MODEL = 1024
HEAD_DIM = 64
N_HEADS = 8
ATTN_W = 512
CONV_W = 512
D_FF = 2816
FF_PIECE = 1408
EPS = 1e-6
Q_SCALE = HEAD_DIM ** -0.5

OFF_F = 1536
OFF_B = 1664
OFF_C = 2176
OFF_U = 2688
IN_PAD = 3200
IN_W = 3080
N_CHIPS = 4

ADAM_LR = 0.001
ADAM_B1 = 0.9
ADAM_B2 = 0.999
ADAM_EPS = 1e-08
ADAM_WD = 0.01
ADAM_STEP = 10

VMEM_LIMIT_V7X = 56 * 1024 * 1024
MESH_ID = pl.DeviceIdType.MESH


def _params(sem=None, vmem=VMEM_LIMIT_V7X):
    kw = {"vmem_limit_bytes": vmem}
    if sem is not None:
        kw["dimension_semantics"] = sem
    return pltpu.CompilerParams(**kw)


def _dot(a, b):
    return jnp.dot(a, b, preferred_element_type=F32)


def _dot_nt(a, b):
    return lax.dot_general(a, b, (((1,), (1,)), ((), ())), preferred_element_type=F32)


def _dot_exact(x, ones, parts):
    if ones.dtype == F32:
        return _dot(x, ones)
    acc = None
    rem = x
    for _ in range(parts):
        piece = rem.astype(BF16)
        rem = rem - piece.astype(F32)
        term = _dot(piece, ones)
        acc = term if acc is None else acc + term
    return acc


def _rms(v):
    return lax.rsqrt(jnp.mean(v * v, axis=-1, keepdims=True) + EPS)


def _tok(tm, w):
    return pl.BlockSpec((tm, w), lambda i: (i, 0))


def _whole(shape, single=False):
    nd = len(shape)
    if single:
        return pl.BlockSpec(shape, lambda i: (0,) * nd, pipeline_mode=pl.Buffered(1))
    return pl.BlockSpec(shape, lambda i: (0,) * nd)


def _feat(rows, tm):
    return pl.BlockSpec((rows, tm), lambda i: (0, i))


def _piece_windows(rows, first, last):
    gap_at = OFF_F + N_HEADS
    runs = []
    for p in range(N_CHIPS):
        lo, hi = p * rows, (p + 1) * rows
        for a, b, shift in ((lo, min(hi, gap_at), 0), (max(lo, gap_at), hi, OFF_B - gap_at)):
            a, b = max(a + shift, first), min(b + shift, last)
            if a < b:
                runs.append((p, a - shift - lo, a, b - a))
    return runs


def _padded_rows(pieces_ref, w_ref):
    for p, at, dst, count in _piece_windows(pieces_ref.shape[1], 0, IN_PAD):
        w_ref[dst:dst + count, :] = pieces_ref[p, at:at + count, :]
    w_ref[OFF_F + N_HEADS:OFF_B, :] = jnp.zeros((OFF_B - OFF_F - N_HEADS, w_ref.shape[1]), w_ref.dtype)


def _inproj_fwd(x, g_pre, pieces, tm):
    s = x.shape[0]

    def body(x_ref, g_ref, pieces_ref, h_ref, q_ref, k_ref, v_ref, kt_ref, vt_ref, zt_ref, b_ref, c_ref, u_ref,
             w_ref):
        @pl.when(pl.program_id(0) == 0)
        def _():
            _padded_rows(pieces_ref, w_ref)

        xv = x_ref[...]
        h = ((xv * _rms(xv)) * g_ref[...]).astype(MXU_DTYPE)
        h_ref[...] = h

        def proj(lo, hi):
            return _dot_nt(h, w_ref[lo:hi, :])

        q_ref[...] = (proj(0, 512) * Q_SCALE).astype(MXU_DTYPE)
        kt = _dot_nt(w_ref[512:1024, :], h)
        vt = _dot_nt(w_ref[1024:OFF_F, :], h)
        kt_ref[...] = kt.astype(MXU_DTYPE)
        vt_ref[...] = vt.astype(MXU_DTYPE)
        k_ref[...] = kt.T.astype(MXU_DTYPE)
        v_ref[...] = vt.T.astype(MXU_DTYPE)
        zt_ref[...] = _dot_nt(w_ref[OFF_F:OFF_B, :], h)
        b_ref[...] = proj(OFF_B, OFF_C).astype(MXU_DTYPE)
        c_ref[...] = proj(OFF_C, OFF_U).astype(MXU_DTYPE)
        u_ref[...] = proj(OFF_U, IN_PAD).astype(MXU_DTYPE)

    sd = jax.ShapeDtypeStruct
    return pl.pallas_call(
        body, name="inproj_fwd", grid=(s // tm,),
        in_specs=[_tok(tm, D_MODEL), _whole((1, D_MODEL)), _whole(pieces.shape, single=True)],
        out_specs=[_tok(tm, D_MODEL), _tok(tm, 512), _tok(tm, 512), _tok(tm, 512), _feat(512, tm), _feat(512, tm),
                   _feat(128, tm), _tok(tm, 512), _tok(tm, 512), _tok(tm, 512), _whole((IN_PAD, D_MODEL))],
        out_shape=[sd((s, D_MODEL), MXU_DTYPE), sd((s, 512), MXU_DTYPE), sd((s, 512), MXU_DTYPE),
                   sd((s, 512), MXU_DTYPE), sd((512, s), MXU_DTYPE), sd((512, s), MXU_DTYPE), sd((128, s), F32),
                   sd((s, 512), MXU_DTYPE), sd((s, 512), MXU_DTYPE), sd((s, 512), MXU_DTYPE),
                   sd((IN_PAD, D_MODEL), MXU_DTYPE)],
        compiler_params=_params(("arbitrary",)),
    )(x, g_pre, pieces)


def _tri(n, upper):
    r = lax.broadcasted_iota(jnp.int32, (n, n), 0)
    c = lax.broadcasted_iota(jnp.int32, (n, n), 1)
    return ((r <= c) if upper else (r >= c)).astype(MXU_DTYPE)


HEAD_ROWS = 16


def _rows_to_cols(v):
    return jnp.concatenate([v, jnp.zeros((128 - HEAD_ROWS, 128), F32)], axis=0).T


BIAS_PARTS = 3


def _bias_placement():
    place_q = np.zeros((BIAS_PARTS, 128, ATTN_W), np.float32)
    place_k = np.zeros((BIAS_PARTS, 128, ATTN_W), np.float32)
    ones_q = np.zeros((1, ATTN_W), np.float32)
    ones_k = np.zeros((1, ATTN_W), np.float32)
    for h in range(N_HEADS):
        base = 2 * HEAD_DIM * (h // 2) + HEAD_DIM * (1 - h % 2)
        for part in range(BIAS_PARTS):
            place_q[part, h, base + part] = 1.0
            place_k[part, h, base + BIAS_PARTS + part] = -1.0
        ones_q[0, base + BIAS_PARTS:base + 2 * BIAS_PARTS] = 1.0
        ones_k[0, base:base + BIAS_PARTS] = 1.0
    return (jnp.asarray(place_q, MXU_DTYPE), jnp.asarray(place_k, MXU_DTYPE), jnp.asarray(ones_q), jnp.asarray(ones_k))


def _forget_fwd(z_t, b_col):
    s = z_t.shape[1]
    nb = s // 128

    def body(z_ref, b_ref, pq_ref, pk_ref, oq_ref, ok_ref, qa_ref, ka_ref, cc_ref):
        upper = _tri(128, True)

        carry = jnp.zeros((HEAD_ROWS, 1), F32)
        for n in range(nb):
            off = n * 128
            lf = jax.nn.log_sigmoid(z_ref[0:HEAD_ROWS, off:off + 128] + b_ref[...])
            cc_ref[off:off + 128, :] = _rows_to_cols(_dot_exact(lf, upper, 3) + carry)
            carry = carry + jnp.sum(lf, axis=1, keepdims=True)

        rb = min(s, 512)
        for off in range(0, s, rb):
            qa = jnp.broadcast_to(oq_ref[...], (rb, ATTN_W))
            ka = jnp.broadcast_to(ok_ref[...], (rb, ATTN_W))
            rem = cc_ref[off:off + rb, :]
            for part in range(BIAS_PARTS):
                piece = rem.astype(MXU_DTYPE)
                rem = rem - piece.astype(F32)
                qa = qa + _dot(piece, pq_ref[part])
                ka = ka + _dot(piece, pk_ref[part])
            qa_ref[off:off + rb, :] = qa.astype(MXU_DTYPE)
            ka_ref[off:off + rb, :] = ka.astype(MXU_DTYPE)

    sd = jax.ShapeDtypeStruct
    return pl.pallas_call(body, name="forget_fwd",
                          out_shape=[sd((s, ATTN_W), MXU_DTYPE), sd((s, ATTN_W), MXU_DTYPE)],
                          scratch_shapes=[pltpu.VMEM((s, 128), F32)],
                          compiler_params=_params())(z_t, b_col, *_bias_placement())


def _aligned(start, size):
    return pl.ds(start if isinstance(start, int) else pl.multiple_of(start, size), size)


def _pair_lanes(pp):
    return _aligned(pp * 2 * HEAD_DIM, 2 * HEAD_DIM)


def _head_rows(h):
    return _aligned(h * HEAD_DIM, HEAD_DIM)


def _only_head(block, hb):
    lane = lax.broadcasted_iota(jnp.int32, block.shape, 1)
    return jnp.where((lane >= HEAD_DIM) if hb else (lane < HEAD_DIM), block, jnp.zeros_like(block))


def _other_head(block, other, hb):
    lane = lax.broadcasted_iota(jnp.int32, block.shape, 1)
    return jnp.where((lane >= HEAD_DIM) if hb else (lane < HEAD_DIM), block, other)


def _attn_fwd(qs, k, v_t, q_bias, k_bias, t, shards):
    s = qs.shape[0]
    n = s // t
    pairs = [(i, j) for i in range(n) for j in range(i + 1)]
    it = jnp.asarray(np.array([p[0] for p in pairs], np.int32))
    jt = jnp.asarray(np.array([p[1] for p in pairs], np.int32))
    nw = len(shards)
    last = len(pairs) - 1
    mid = (2 * len(pairs)) // 3

    def body(it_ref, jt_ref, q_ref, k_ref, vt_ref, qb_ref, kb_ref, *rest):
        sh, (o_ref, lse_ref, p_ref, pm_ref), got = rest[:nw], rest[nw:nw + 4], rest[nw + 4:2 * nw + 4]
        m_sc, l_sc, acc_sc, send, recv = rest[2 * nw + 4:]
        p = pl.program_id(0)
        i = it_ref[p]
        j = jt_ref[p]
        gather_start, gather_forward, gather_finish = _gather_stages(sh, got, send, recv)
        pl.when(p == 0)(gather_start)
        if mid < last:
            pl.when(p == mid)(gather_forward)

        @pl.when(j == 0)
        def _():
            m_sc[...] = jnp.full_like(m_sc, -1e30)
            l_sc[...] = jnp.ones_like(l_sc)
            acc_sc[...] = jnp.zeros_like(acc_sc)

        pm_ref[...] = jnp.zeros_like(pm_ref)

        def pair_step(pp, diagonal):
            lanes = _pair_lanes(pp)
            kp = k_ref[:, lanes]
            qp = q_ref[:, lanes]
            kb = kb_ref[:, lanes]
            qb = qb_ref[:, lanes]
            for hb in range(2):
                h = 2 * pp + hb
                row = pl.ds(h, 1)
                rows = _head_rows(h)
                st = _dot_nt(_other_head(kp, kb, hb), _other_head(qp, qb, hb))
                if diagonal:
                    kpos = lax.broadcasted_iota(jnp.int32, (t, t), 0)
                    qpos = lax.broadcasted_iota(jnp.int32, (t, t), 1)
                    st = jnp.where(kpos <= qpos, st, -1e30)
                m_prev = m_sc[row, :]
                m_new = jnp.maximum(m_prev, jnp.max(st, axis=0, keepdims=True))
                alpha = jnp.exp(m_prev - m_new)
                pt = jnp.exp(st - m_new)
                l_sc[row, :] = alpha * l_sc[row, :] + jnp.sum(pt, axis=0, keepdims=True)
                ptb = pt.astype(MXU_DTYPE)
                acc_sc[rows, :] = acc_sc[rows, :] * alpha + _dot(vt_ref[rows, :], ptb)
                m_sc[row, :] = m_new
                p_ref[0, h] = ptb
                pm_ref[0, row, :] = m_new

        @pl.when(j < i)
        def _():
            for pp in range(N_HEADS // 2):
                pair_step(pp, False)

        @pl.when(j == i)
        def _():
            for pp in range(N_HEADS // 2):
                pair_step(pp, True)
                sub = lax.broadcasted_iota(jnp.int32, (2 * HEAD_DIM, t), 0)
                l_pair = jnp.where(sub < HEAD_DIM, l_sc[pl.ds(2 * pp, 1), :], l_sc[pl.ds(2 * pp + 1, 1), :])
                o_t = acc_sc[_aligned(pp * 2 * HEAD_DIM, 2 * HEAD_DIM), :] / l_pair
                o_ref[:, _pair_lanes(pp)] = o_t.T.astype(MXU_DTYPE)

            lse_ref[...] = m_sc[...] + jnp.log(l_sc[...])

        @pl.when(p == last)
        def _():
            if mid >= last:
                gather_forward()
            gather_finish()

    gs = pltpu.PrefetchScalarGridSpec(
        num_scalar_prefetch=2, grid=(len(pairs),),
        in_specs=[pl.BlockSpec((t, ATTN_W), lambda p, it_, jt_: (it_[p], 0)),
                  pl.BlockSpec((t, ATTN_W), lambda p, it_, jt_: (jt_[p], 0)),
                  pl.BlockSpec((ATTN_W, t), lambda p, it_, jt_: (0, jt_[p])),
                  pl.BlockSpec((t, ATTN_W), lambda p, it_, jt_: (it_[p], 0)),
                  pl.BlockSpec((t, ATTN_W), lambda p, it_, jt_: (jt_[p], 0))] + [ANY] * nw,
        out_specs=[pl.BlockSpec((t, ATTN_W), lambda p, it_, jt_: (it_[p], 0)),
                   pl.BlockSpec((HEAD_ROWS, t), lambda p, it_, jt_: (0, it_[p])),
                   pl.BlockSpec((1, N_HEADS, t, t), lambda p, it_, jt_: (p, 0, 0, 0)),
                   pl.BlockSpec((1, HEAD_ROWS, t), lambda p, it_, jt_: (p, 0, 0))] + [ANY] * nw,
        scratch_shapes=[pltpu.VMEM((HEAD_ROWS, t), F32), pltpu.VMEM((HEAD_ROWS, t), F32), pltpu.VMEM((ATTN_W, t), F32),
                        pltpu.SemaphoreType.DMA((nw, 6)), pltpu.SemaphoreType.DMA((nw, 6))])
    sd = jax.ShapeDtypeStruct
    o, lse, probs, probs_max, *got = pl.pallas_call(
        body, name="attn_fwd", grid_spec=gs,
        out_shape=[sd((s, ATTN_W), MXU_DTYPE), sd((HEAD_ROWS, s), F32), sd((len(pairs), N_HEADS, t, t), MXU_DTYPE),
                   sd((len(pairs), HEAD_ROWS, t), F32)]
        + [sd((N_CHIPS,) + a.shape, a.dtype) for a in shards],
        compiler_params=_params(("arbitrary",)),
    )(it, jt, qs, k, v_t, q_bias, k_bias, *shards)
    me = 2 * lax.axis_index("x") + lax.axis_index("y")
    return o, lse, probs, probs_max, [lax.dynamic_update_index_in_dim(g, own, me, 0) for g, own in zip(got, shards)]


HALO = 16


def _halo_before(tm):
    return pl.BlockSpec((HALO, CONV_W), lambda i: (jnp.maximum(i * (tm // HALO) - 1, 0), 0))


def _shift_down(cur, prev_ref, first, cols=slice(None)):
    row = lax.broadcasted_iota(jnp.int32, cur.shape, 0)

    def before(r):
        prod = prev_ref[0][r:r + 1, cols].astype(F32) * prev_ref[1][r:r + 1, cols].astype(F32)
        return jnp.where(first, 0.0, prod)

    p7, p6 = before(HALO - 1), before(HALO - 2)
    s1 = jnp.where(row == 0, p7, pltpu.roll(cur, 1, 0))
    s2 = jnp.where(row == 0, p6, jnp.where(row == 1, p7, pltpu.roll(cur, 2, 0)))
    return s1, s2


def _group_ms(v, gmat):
    return _dot_exact(v, gmat, 1) * (1.0 / HEAD_DIM)


def _mixer_fwd(x, o_attn, gate_b, gate_c, u, conv_w, g_attn, g_conv, w_out, g_post, gmat, tm):
    s = x.shape[0]

    def body(x_ref, o_ref, b_ref, c_ref, u_ref, cp_ref, up_ref, cw_ref, ga_ref, gc_ref, wo_ref, gp_ref, gm_ref,
             x2_ref, mg_ref, y_ref, z_ref):
        i = pl.program_id(0)
        gm = gm_ref[0:128, 0:128]
        for lo in range(0, ATTN_W, 128):
            cols = slice(lo, lo + 128)
            cu = c_ref[:, cols].astype(F32) * u_ref[:, cols].astype(F32)
            cu1, cu2 = _shift_down(cu, (cp_ref, up_ref), i == 0, cols)
            z = cw_ref[0:1, cols] * cu2 + cw_ref[1:2, cols] * cu1 + cw_ref[2:3, cols] * cu
            z_ref[:, cols] = z.astype(MXU_DTYPE)
            cv = b_ref[:, cols].astype(F32) * z
            ov = o_ref[:, cols].astype(F32)
            mg_ref[:, cols] = ((ov * lax.rsqrt(_group_ms(ov * ov, gm) + EPS)) * ga_ref[:, cols]).astype(MXU_DTYPE)
            mg_ref[:, ATTN_W + lo:ATTN_W + lo + 128] = (
                (cv * lax.rsqrt(_group_ms(cv * cv, gm) + EPS)) * gc_ref[:, cols]).astype(MXU_DTYPE)
        y = _dot(mg_ref[...], wo_ref[...])
        y_ref[...] = y
        x2_ref[...] = x_ref[...] + (y * _rms(y)) * gp_ref[...]

    halo = _halo_before(tm)
    sd = jax.ShapeDtypeStruct
    return pl.pallas_call(
        body, name="mixer_fwd", grid=(s // tm,),
        in_specs=[_tok(tm, D_MODEL), _tok(tm, 512), _tok(tm, 512), _tok(tm, 512), _tok(tm, 512), halo, halo,
                  _whole((3, 512)), _whole((1, 512)), _whole((1, 512)), _whole((D_MODEL, D_MODEL), single=True),
                  _whole((1, D_MODEL)), _whole((512, 512))],
        out_specs=[_tok(tm, D_MODEL), _tok(tm, D_MODEL), _tok(tm, D_MODEL), _tok(tm, 512)],
        out_shape=[sd((s, D_MODEL), F32), sd((s, D_MODEL), MXU_DTYPE), sd((s, D_MODEL), F32), sd((s, 512), MXU_DTYPE)],
        compiler_params=_params(("arbitrary",)),
    )(x, o_attn, gate_b, gate_c, u, gate_c, u, conv_w, g_attn, g_conv, w_out, g_post, gmat)


def _ffn_fwd(x2, target, g_pre, w_gu, w_dn, g_post, tm):
    s = x2.shape[0]

    def body(x_ref, t_ref, gpre_ref, wgu_ref, wdn_ref, gpost_ref,
             h_ref, g_ref, up_ref, a_ref, ff_ref, dout_ref, loss_ref):
        xv = x_ref[...]
        h = ((xv * _rms(xv)) * gpre_ref[...]).astype(MXU_DTYPE)
        h_ref[...] = h
        ff = jnp.zeros((tm, D_MODEL), F32)
        for j in range(2):
            cols = slice(j * FF_PIECE, (j + 1) * FF_PIECE)
            g = _dot(h, wgu_ref[j])
            up = _dot(h, wgu_ref[2 + j])
            a = ((g * jax.nn.sigmoid(g)) * up).astype(MXU_DTYPE)
            g_ref[:, cols] = g.astype(MXU_DTYPE)
            up_ref[:, cols] = up.astype(MXU_DTYPE)
            a_ref[:, cols] = a
            ff = ff + _dot(a, wdn_ref[j])
        ff_ref[...] = ff
        err = (xv + (ff * _rms(ff)) * gpost_ref[...]) - t_ref[...]
        dout_ref[...] = err * (1.0 / D_MODEL)
        part = jnp.sum(jnp.mean(err * err, axis=-1, keepdims=True), axis=0, keepdims=True)

        @pl.when(pl.program_id(0) == 0)
        def _():
            loss_ref[...] = jnp.zeros_like(loss_ref)

        loss_ref[...] += part

    sd = jax.ShapeDtypeStruct
    return pl.pallas_call(
        body, name="ffn_fwd", grid=(s // tm,),
        in_specs=[_tok(tm, D_MODEL), _tok(tm, D_MODEL), _whole((1, D_MODEL)),
                  _whole((4, D_MODEL, FF_PIECE), single=True), _whole((2, FF_PIECE, D_MODEL), single=True),
                  _whole((1, D_MODEL))],
        out_specs=[_tok(tm, D_MODEL), _tok(tm, D_FF), _tok(tm, D_FF), _tok(tm, D_FF), _tok(tm, D_MODEL),
                   _tok(tm, D_MODEL), _whole((8, 128))],
        out_shape=[sd((s, D_MODEL), MXU_DTYPE), sd((s, D_FF), MXU_DTYPE), sd((s, D_FF), MXU_DTYPE),
                   sd((s, D_FF), MXU_DTYPE), sd((s, D_MODEL), F32), sd((s, D_MODEL), F32), sd((8, 128), F32)],
        compiler_params=_params(("arbitrary",)),
    )(x2, target, g_pre, w_gu, w_dn, g_post)


def _norm_bwd(dy, normed, rinv, gain):
    t = dy * gain
    return rinv * (t - normed * jnp.mean(t * normed, axis=-1, keepdims=True))


def _acc_rows(ref, first, val):
    @pl.when(first)
    def _():
        ref[...] = jnp.zeros_like(ref)

    ref[...] += jnp.sum(val, axis=0, keepdims=True)


def _ffn_bwd(dout, ff, x2, g, up, g_post, g_pre, w_gu, w_dn, tm):
    s = x2.shape[0]

    def body(do_ref, ff_ref, x_ref, g_ref, up_ref, gpost_ref, gpre_ref, wgu_ref, wdn_ref,
             dx_ref, dff_ref, dgu_ref, dgpost_ref, dgpre_ref):
        first = pl.program_id(0) == 0
        ffv = ff_ref[...]
        rf = _rms(ffv)
        n = ffv * rf
        do = do_ref[...]
        _acc_rows(dgpost_ref, first, do * n)
        dff = _norm_bwd(do, n, rf, gpost_ref[...]).astype(MXU_DTYPE)
        dff_ref[...] = dff
        dh = jnp.zeros((tm, D_MODEL), F32)
        for j in range(2):
            cols = slice(j * FF_PIECE, (j + 1) * FF_PIECE)
            da = _dot_nt(dff, wdn_ref[j])
            gv = g_ref[:, cols].astype(F32)
            sg = jax.nn.sigmoid(gv)
            dg = (da * up_ref[:, cols].astype(F32) * (sg * (1.0 + gv * (1.0 - sg)))).astype(MXU_DTYPE)
            du = (da * (gv * sg)).astype(MXU_DTYPE)
            dgu_ref[:, cols] = dg
            dgu_ref[:, D_FF + j * FF_PIECE:D_FF + (j + 1) * FF_PIECE] = du
            dh = dh + _dot_nt(dg, wgu_ref[j]) + _dot_nt(du, wgu_ref[2 + j])
        xv = x_ref[...]
        r2 = _rms(xv)
        nx = xv * r2
        _acc_rows(dgpre_ref, first, dh * nx)
        dx_ref[...] = do + _norm_bwd(dh, nx, r2, gpre_ref[...])

    sd = jax.ShapeDtypeStruct
    return pl.pallas_call(
        body, name="ffn_bwd", grid=(s // tm,),
        in_specs=[_tok(tm, D_MODEL), _tok(tm, D_MODEL), _tok(tm, D_MODEL), _tok(tm, D_FF), _tok(tm, D_FF),
                  _whole((1, D_MODEL)), _whole((1, D_MODEL)),
                  _whole((4, D_MODEL, FF_PIECE), single=True), _whole((2, FF_PIECE, D_MODEL), single=True)],
        out_specs=[_tok(tm, D_MODEL), _tok(tm, D_MODEL), _tok(tm, 2 * D_FF), _whole((1, D_MODEL)),
                   _whole((1, D_MODEL))],
        out_shape=[sd((s, D_MODEL), F32), sd((s, D_MODEL), MXU_DTYPE), sd((s, 2 * D_FF), MXU_DTYPE),
                   sd((1, D_MODEL), F32), sd((1, D_MODEL), F32)],
        compiler_params=_params(("arbitrary",)),
    )(dout, ff, x2, g, up, g_post, g_pre, w_gu, w_dn)


def _tn_matmul(a, b, tm, tn, tk, name, totals=(), piece_rows=None):
    s, m = a.shape
    n = b.shape[1]
    nw = len(totals)
    grid = (m // tm, n // tn, s // tk)
    assert piece_rows is None or tn == n

    def body(a_ref, b_ref, *rest):
        o_ref = rest[nw]
        acc_ref = o_ref if piece_rows is None else rest[-1]
        if nw:
            step = (pl.program_id(0) * grid[1] + pl.program_id(1)) * grid[2] + pl.program_id(2)
            share_start, share_finish = _share_stages(rest[:nw], rest[nw + 1:2 * nw + 1], *rest[2 * nw + 1:2 * nw + 3])
            pl.when(step == 0)(share_start)

        @pl.when(pl.program_id(2) == 0)
        def _():
            acc_ref[...] = jnp.zeros_like(acc_ref)

        acc_ref[...] += lax.dot_general(a_ref[...], b_ref[...], (((0,), (0,)), ((), ())), preferred_element_type=F32)
        if piece_rows is not None:
            for r in range(grid[0]):
                @pl.when((pl.program_id(0) == r) & (pl.program_id(2) == grid[2] - 1))
                def _():
                    for p, at, src, count in _piece_windows(piece_rows, r * tm, (r + 1) * tm):
                        o_ref[p, at:at + count, :] = acc_ref[src - r * tm:src - r * tm + count, :]
        if nw:
            pl.when(step == grid[0] * grid[1] * grid[2] - 1)(share_finish)

    if piece_rows is None:
        o_spec, o_shape = pl.BlockSpec((tm, tn), lambda i, j, kk: (i, j)), (m, n)
    else:
        o_spec, o_shape = pl.BlockSpec((N_CHIPS, piece_rows, n), lambda i, j, kk: (0, 0, 0)), (N_CHIPS, piece_rows, n)
    out = pl.pallas_call(
        body, name=name, grid=grid,
        in_specs=[pl.BlockSpec((tk, tm), lambda i, j, kk: (kk, i)), pl.BlockSpec((tk, tn), lambda i, j, kk: (kk, j))]
        + [ANY] * nw,
        out_specs=[o_spec] + [ANY] * nw,
        out_shape=[jax.ShapeDtypeStruct(o_shape, F32)] + [jax.ShapeDtypeStruct(t.shape, t.dtype) for t in totals],
        scratch_shapes=([pltpu.SemaphoreType.DMA((nw,)), pltpu.SemaphoreType.DMA((nw,))] if nw else [])
        + ([] if piece_rows is None else [pltpu.VMEM((tm, tn), F32)]),
        compiler_params=_params(("arbitrary", "arbitrary", "arbitrary")),
    )(a, b, *totals)
    return (out[0], out[1:]) if nw else out[0]


def _mixer_bwd(dx2, y, o_attn, gate_b, z, g_post, g_attn, g_conv, w_out, gmat, sel, tm, ready, kinds):
    s = dx2.shape[0]
    nw = len(ready)
    nt = s // tm

    def body(d_ref, y_ref, o_ref, b_ref, z_ref, gp_ref, ga_ref, gc_ref, wo_ref, gm_ref, sel_ref, *rest):
        grads = rest[:nw]
        dy_ref, do_ref, db_ref, dz_ref, delta_ref, dgp_ref, dga_ref, dgc_ref = rest[nw:nw + 8]
        taken = rest[nw + 8:2 * nw + 8]
        send, recv = rest[2 * nw + 8:]
        first = pl.program_id(0) == 0
        pair_start, pair_finish = _pair_stages(grads, kinds, taken, send, recv)
        pl.when(first)(pair_start)
        yv = y_ref[...]
        ry = _rms(yv)
        ny = yv * ry
        d = d_ref[...]
        _acc_rows(dgp_ref, first, d * ny)
        dy = _norm_bwd(d, ny, ry, gp_ref[...]).astype(MXU_DTYPE)
        dy_ref[...] = dy
        dm = _dot_nt(dy, wo_ref[...])
        gm = gm_ref[0:128, 0:128]

        @pl.when(first)
        def _():
            dga_ref[...] = jnp.zeros_like(dga_ref)
            dgc_ref[...] = jnp.zeros_like(dgc_ref)

        def group_bwd(val, dmv, gain_ref, dg_ref, cols):
            rg = lax.rsqrt(_group_ms(val * val, gm) + EPS)
            nv = val * rg
            dg_ref[:, cols] += jnp.sum(dmv * nv, axis=0, keepdims=True)
            t = dmv * gain_ref[:, cols]
            return rg * (t - nv * _group_ms(t * nv, gm))

        delta = jnp.zeros((tm, 128), F32)
        for lo in range(0, ATTN_W, 128):
            cols = slice(lo, lo + 128)
            ov = o_ref[:, cols].astype(F32)
            d_o = group_bwd(ov, dm[:, cols], ga_ref, dga_ref, cols)
            do_ref[:, cols] = d_o.astype(MXU_DTYPE)
            delta = delta + _dot_exact(d_o * ov, sel_ref[cols, :], 2)
            zv = z_ref[:, cols].astype(F32)
            bv = b_ref[:, cols].astype(F32)
            d_cv = group_bwd(bv * zv, dm[:, ATTN_W + lo:ATTN_W + lo + 128], gc_ref, dgc_ref, cols)
            db_ref[:, cols] = (d_cv * zv).astype(MXU_DTYPE)
            dz_ref[:, cols] = d_cv * bv
        delta_ref[...] = delta.T[0:HEAD_ROWS, :]
        pl.when(pl.program_id(0) == nt - 1)(pair_finish)

    sd = jax.ShapeDtypeStruct
    taken_shape = [sd((N_CHIPS, g.shape[-2], g.shape[-1] if kd == "rows" else g.shape[-1] // N_CHIPS), F32)
                   for g, kd in zip(ready, kinds)]
    out = pl.pallas_call(
        body, name="mixer_bwd", grid=(nt,),
        in_specs=[_tok(tm, D_MODEL), _tok(tm, D_MODEL), _tok(tm, 512), _tok(tm, 512), _tok(tm, 512),
                  _whole((1, D_MODEL)), _whole((1, 512)), _whole((1, 512)),
                  _whole((D_MODEL, D_MODEL), single=True), _whole((512, 512)), _whole((512, 128))] + [ANY] * nw,
        out_specs=[_tok(tm, D_MODEL), _tok(tm, 512), _tok(tm, 512), _tok(tm, 512), _feat(HEAD_ROWS, tm),
                   _whole((1, D_MODEL)), _whole((1, 512)), _whole((1, 512))] + [ANY] * nw,
        out_shape=[sd((s, D_MODEL), MXU_DTYPE), sd((s, 512), MXU_DTYPE), sd((s, 512), MXU_DTYPE), sd((s, 512), F32),
                   sd((HEAD_ROWS, s), F32), sd((1, D_MODEL), F32), sd((1, 512), F32), sd((1, 512), F32)] + taken_shape,
        scratch_shapes=[pltpu.SemaphoreType.DMA((nw, N_CHIPS)), pltpu.SemaphoreType.DMA((nw, N_CHIPS))],
        compiler_params=_params(("arbitrary",)),
    )(dx2, y, o_attn, gate_b, z, g_post, g_attn, g_conv, w_out, gmat, sel, *ready)
    return out[:8], out[8:]


def _attn_bwd(qs, k_t, v, do, probs, probs_max, lse, delta, t, parts):
    s = qs.shape[0]
    n = s // t
    pairs = [(i, j) for j in range(n) for i in range(j, n)]
    it = jnp.asarray(np.array([p[0] for p in pairs], np.int32))
    jt = jnp.asarray(np.array([p[1] for p in pairs], np.int32))
    ft = jnp.asarray(np.array([p[0] * (p[0] + 1) // 2 + p[1] for p in pairs], np.int32))

    nw = len(parts)

    def body(it_ref, jt_ref, ft_ref, q_ref, kt_ref, v_ref, do_ref, p_ref, pm_ref, lse_ref, dl_ref, *rest):
        pb = rest[:nw]
        dq_ref, dk_ref, dv_ref, dc_ref, dcq_ref = rest[nw:nw + 5]
        rcv = rest[nw + 5:2 * nw + 5]
        dk_sc, dv_sc, dc_sc, send, recv = rest[2 * nw + 5:]
        p = pl.program_id(0)
        i = it_ref[p]
        j = jt_ref[p]
        chip_start, chip_finish = _chip_stages(pb, rcv, send, recv)

        @pl.when(p == 0)
        def _():
            chip_start()
            dq_ref[...] = jnp.zeros_like(dq_ref)
            dcq_ref[...] = jnp.zeros_like(dcq_ref)

        @pl.when(i == j)
        def _():
            dk_sc[...] = jnp.zeros_like(dk_sc)
            dv_sc[...] = jnp.zeros_like(dv_sc)
            dc_sc[...] = jnp.zeros_like(dc_sc)

        def pair_step(pp):
            lanes = _pair_lanes(pp)
            qp = q_ref[:, lanes]
            vp = v_ref[:, lanes]
            dop = do_ref[:, lanes]
            lane = lax.broadcasted_iota(jnp.int32, (t, 128), 1)
            for hb in range(2):
                h = 2 * pp + hb
                row = pl.ds(h, 1)
                pt = p_ref[0, h].astype(F32) * jnp.exp(pm_ref[0, row, :] - lse_ref[row, :])
                dv_sc[:, lanes] += _dot(pt.astype(MXU_DTYPE), _only_head(dop, hb))
                dst = pt * (_dot_nt(_only_head(vp, hb), dop) - dl_ref[row, :])
                dc_sc[...] -= jnp.where(lane == h, jnp.sum(dst, axis=1, keepdims=True), 0.0)
                dcq_ref[i, row, :] += jnp.sum(dst, axis=0, keepdims=True)
                dsb = dst.astype(MXU_DTYPE)
                dk_sc[:, lanes] += _dot(dsb, _only_head(qp, hb))
                rows = _head_rows(h)
                dq_ref[i, rows, :] += _dot(kt_ref[rows, :], dsb)

        for pp in range(N_HEADS // 2):
            pair_step(pp)

        @pl.when(i == n - 1)
        def _():
            dk_ref[...] = dk_sc[...].astype(MXU_DTYPE)
            dv_ref[...] = dv_sc[...].astype(MXU_DTYPE)
            dc_ref[...] = dc_sc[...]

        pl.when(p == len(pairs) - 1)(chip_finish)

    qi = lambda p, it_, jt_, ft_: (it_[p], 0)
    kj = lambda p, it_, jt_, ft_: (jt_[p], 0)
    row_i = lambda p, it_, jt_, ft_: (0, it_[p])
    gs = pltpu.PrefetchScalarGridSpec(
        num_scalar_prefetch=3, grid=(len(pairs),),
        in_specs=[pl.BlockSpec((t, ATTN_W), qi),
                  pl.BlockSpec((ATTN_W, t), lambda p, it_, jt_, ft_: (0, jt_[p])),
                  pl.BlockSpec((t, ATTN_W), kj), pl.BlockSpec((t, ATTN_W), qi),
                  pl.BlockSpec((1, N_HEADS, t, t), lambda p, it_, jt_, ft_: (ft_[p], 0, 0, 0)),
                  pl.BlockSpec((1, HEAD_ROWS, t), lambda p, it_, jt_, ft_: (ft_[p], 0, 0)),
                  pl.BlockSpec((HEAD_ROWS, t), row_i), pl.BlockSpec((HEAD_ROWS, t), row_i)] + [ANY] * nw,
        out_specs=[pl.BlockSpec((n, ATTN_W, t), lambda p, it_, jt_, ft_: (0, 0, 0)),
                   pl.BlockSpec((t, ATTN_W), kj), pl.BlockSpec((t, ATTN_W), kj),
                   pl.BlockSpec((t, 128), kj),
                   pl.BlockSpec((n, HEAD_ROWS, t), lambda p, it_, jt_, ft_: (0, 0, 0))] + [ANY] * nw,
        scratch_shapes=[pltpu.VMEM((t, ATTN_W), F32), pltpu.VMEM((t, ATTN_W), F32),
                        pltpu.VMEM((t, 128), F32), pltpu.SemaphoreType.DMA((nw, 3)), pltpu.SemaphoreType.DMA((nw, 3))])
    sd = jax.ShapeDtypeStruct
    out = pl.pallas_call(
        body, name="attn_bwd", grid_spec=gs,
        out_shape=[sd((n, ATTN_W, t), F32), sd((s, ATTN_W), MXU_DTYPE), sd((s, ATTN_W), MXU_DTYPE),
                   sd((s, 128), F32), sd((n, HEAD_ROWS, t), F32)] + [sd((3,) + a.shape[1:], a.dtype) for a in parts],
        compiler_params=_params(("arbitrary",)),
    )(it, jt, ft, qs, k_t, v, do, probs, probs_max, lse, delta, *parts)
    return out[:5], out[5:]


def _forget_bwd(dc_rows, dc_cols, z_t, b_col):
    s = z_t.shape[1]
    nb = s // 128

    def body(dr_ref, dcc_ref, z_ref, b_ref, dz_ref, db_ref):
        lower = _tri(128, False)
        real = lax.broadcasted_iota(jnp.int32, (HEAD_ROWS, 128), 0) < N_HEADS

        tail = jnp.zeros((HEAD_ROWS, 1), F32)
        dbias = jnp.zeros((HEAD_ROWS, 1), F32)
        for m in range(nb):
            off = (nb - 1 - m) * 128
            dc = dr_ref[:, off:off + 128] + dcc_ref[off:off + 128, :].T[0:HEAD_ROWS, :]
            dlf = _dot_exact(dc, lower, 3) + tail
            dz = dlf * jax.nn.sigmoid(-(z_ref[0:HEAD_ROWS, off:off + 128] + b_ref[...]))
            dz = jnp.where(real, dz, 0.0)
            dz_ref[off:off + 128, :] = _rows_to_cols(dz)
            tail = tail + jnp.sum(dc, axis=1, keepdims=True)
            dbias = dbias + jnp.sum(dz, axis=1, keepdims=True)
        db_ref[...] = jnp.broadcast_to(dbias, db_ref.shape)

    return pl.pallas_call(
        body, name="forget_bwd",
        out_shape=[jax.ShapeDtypeStruct((s, 128), F32), jax.ShapeDtypeStruct((HEAD_ROWS, 128), F32)],
        compiler_params=_params())(dc_rows, dc_cols, z_t, b_col)


def _inproj_bwd(dz, gate_c, u, conv_w, dq, dk, dv, dzf, db, x, dx2, g_pre, w_t, tm):
    s = x.shape[0]
    nt = s // tm
    t = dq.shape[2]
    assert t % tm == 0 and dq.shape[:2] == (s // t, ATTN_W)
    per = t // tm

    def body(dz_ref, dzn_ref, c_ref, u_ref, cp_ref, up_ref, cw_ref, dq_ref, dk_ref, dv_ref, dzf_ref, db_ref,
             x_ref, dx2_ref, g_ref, w_ref, gx_ref, dp_ref, dg_ref, dcw_ref):
        i = pl.program_id(0)
        first = i == 0
        last = i == nt - 1
        @pl.when(first)
        def _():
            dcw_ref[...] = jnp.zeros_like(dcw_ref)

        for lo in range(0, CONV_W, 128):
            cols = slice(lo, lo + 128)
            dzv = dz_ref[:, cols]
            row = lax.broadcasted_iota(jnp.int32, dzv.shape, 0)
            n0 = jnp.where(last, 0.0, dzn_ref[0:1, cols])
            n1 = jnp.where(last, 0.0, dzn_ref[1:2, cols])
            dz1 = jnp.where(row == tm - 1, n0, pltpu.roll(dzv, tm - 1, 0))
            dz2 = jnp.where(row == tm - 1, n1, jnp.where(row == tm - 2, n0, pltpu.roll(dzv, tm - 2, 0)))
            dcu = cw_ref[2:3, cols] * dzv + cw_ref[1:2, cols] * dz1 + cw_ref[0:1, cols] * dz2
            cv = c_ref[:, cols].astype(F32)
            uv = u_ref[:, cols].astype(F32)
            cu = cv * uv
            cu1, cu2 = _shift_down(cu, (cp_ref, up_ref), first, cols)
            dcw_ref[0:1, cols] += jnp.sum(dzv * cu2, axis=0, keepdims=True)
            dcw_ref[1:2, cols] += jnp.sum(dzv * cu1, axis=0, keepdims=True)
            dcw_ref[2:3, cols] += jnp.sum(dzv * cu, axis=0, keepdims=True)
            dp_ref[:, OFF_C + lo:OFF_C + lo + 128] = (dcu * uv).astype(MXU_DTYPE)
            dp_ref[:, OFF_U + lo:OFF_U + lo + 128] = (dcu * cv).astype(MXU_DTYPE)

        dp_ref[:, 0:512] = (dq_ref[0].T * Q_SCALE).astype(MXU_DTYPE)
        dp_ref[:, 512:1024] = dk_ref[...].astype(MXU_DTYPE)
        dp_ref[:, 1024:OFF_F] = dv_ref[...].astype(MXU_DTYPE)
        dp_ref[:, OFF_F:OFF_B] = dzf_ref[...].astype(MXU_DTYPE)
        dp_ref[:, OFF_B:OFF_C] = db_ref[...].astype(MXU_DTYPE)
        dh = _dot(dp_ref[...], w_ref[...])
        xv = x_ref[...]
        r1 = _rms(xv)
        nx = xv * r1
        _acc_rows(dg_ref, first, dh * nx)
        gx_ref[...] = dx2_ref[...] + _norm_bwd(dh, nx, r1, g_ref[...])

    prev = _halo_before(tm)
    nxt = pl.BlockSpec((8, 512), lambda i: (jnp.minimum((i + 1) * (tm // 8), s // 8 - 1), 0))
    sd = jax.ShapeDtypeStruct
    return pl.pallas_call(
        body, name="inproj_bwd", grid=(nt,),
        in_specs=[_tok(tm, 512), nxt, _tok(tm, 512), _tok(tm, 512), prev, prev, _whole((3, 512)),
                  pl.BlockSpec((1, ATTN_W, tm), lambda i: (i // per, 0, i % per)), _tok(tm, 512), _tok(tm, 512),
                  _tok(tm, 128),
                  _tok(tm, 512),
                  _tok(tm, D_MODEL), _tok(tm, D_MODEL), _whole((1, D_MODEL)), _whole((IN_PAD, D_MODEL), single=True)],
        out_specs=[_tok(tm, D_MODEL), _tok(tm, IN_PAD), _whole((1, D_MODEL)), _whole((8, 512))],
        out_shape=[sd((s, D_MODEL), F32), sd((s, IN_PAD), MXU_DTYPE), sd((1, D_MODEL), F32), sd((8, 512), F32)],
        compiler_params=_params(("arbitrary",)),
    )(dz, dz, gate_c, u, gate_c, u, conv_w, dq, dk, dv, dzf, db, x, dx2, g_pre, w_t)


def _tile(s, want):
    return want if s % want == 0 else s


def _halves(a):
    return a.reshape(2, a.shape[0] // 2, a.shape[1])


def _device_step(x, target, w, mom1, mom2, w_in_t, m_in_t, v_in_t, c_idx, me_idx):
    s = x.shape[0]
    tm = _tile(s, 512)
    tf = _tile(s, 256)
    ta = _tile(s, 512)
    tkk = _tile(s, 2048)
    gidx = np.arange(512) // HEAD_DIM
    gmat = jnp.asarray(gidx[:, None] == gidx[None, :], MXU_DTYPE)
    sel = jnp.asarray(gidx[:, None] == np.arange(128)[None, :], MXU_DTYPE)
    g_mix_pre, g_mix_post, g_ffn_pre, g_ffn_post = w["g_mix_pre"], w["g_mix_post"], w["g_ffn_pre"], w["g_ffn_post"]
    g_attn, g_conv, b_forget = w["g_attn_out"], w["g_conv_out"], w["b_forget"]
    shard = {n: _halves(w[n][0].astype(MXU_DTYPE)) for n in BIG[1:]}
    piece_rows = IN_W // N_CHIPS

    g_in, conv_all = _gather_weights([w_in_t.reshape(piece_rows, D_MODEL).astype(MXU_DTYPE)], w["conv_w"][0])
    conv_w = jnp.transpose(conv_all, (1, 0, 2)).reshape(3, CONV_W)

    h1, qs, k, v, k_t, v_t, z_t, gate_b, gate_c, u, w_t = _inproj_fwd(x, g_mix_pre, g_in, tm)
    b_col = jnp.pad(jnp.transpose(b_forget), ((0, HEAD_ROWS - N_HEADS), (0, 0)))
    q_bias, k_bias = _forget_fwd(z_t, b_col)
    o_attn, lse, probs, probs_max, (g_out, g_gu, g_dn) = _attn_fwd(
        qs, k, v_t, q_bias, k_bias, ta, [shard["w_out"], shard["w_gate_up"], shard["w_down"]])
    w_out = g_out.reshape(D_MODEL, D_MODEL)
    w_gu = g_gu.reshape(N_CHIPS, D_MODEL, FF_PIECE)
    w_dn = g_dn.reshape(2, FF_PIECE, D_MODEL)
    x2, merged, y, z = _mixer_fwd(x, o_attn, gate_b, gate_c, u, conv_w, g_attn, g_conv, w_out, g_mix_post, gmat, tm)
    h2, g, up, a, ff, dout, loss_acc = _ffn_fwd(x2, target, g_ffn_pre, w_gu, w_dn, g_ffn_post, tf)

    dx2, dff, dgu, dg_ffn_post, dg_ffn_pre = _ffn_bwd(dout, ff, x2, g, up, g_ffn_post, g_ffn_pre, w_gu, w_dn, tf)
    dw_dn = _tn_matmul(a, dff, FF_PIECE, 1024, tkk, "dw_down").reshape(N_CHIPS, 2, D_FF // (2 * N_CHIPS), D_MODEL)
    dw_gu = _tn_matmul(h2, dgu, 1024, FF_PIECE, tkk, "dw_gate_up").reshape(2, D_MODEL // 2, 2 * D_FF)
    (dy, d_o, d_b, dz, delta, dg_mix_post, dg_attn, dg_conv), (a_gu, a_dn) = _mixer_bwd(
        dx2, y, o_attn, gate_b, z, g_mix_post, g_attn, g_conv, w_out, gmat, sel, tm, [dw_gu, dw_dn], ["cols", "rows"])
    dw_out = _tn_matmul(merged, dy, 1024, 1024, tkk, "dw_out").reshape(N_CHIPS, 2, D_MODEL // (2 * N_CHIPS), D_MODEL)
    place = jnp.concatenate([c_idx, me_idx])
    *sum_gu, a_out = _pair_sum(place, dw_gu, "cols", a_gu, "pair_sum_w_gate_up", [dw_out], ["rows"])
    sum_dn = _pair_sum(place, dw_dn, "rows", a_dn, "pair_sum_w_down")
    sum_out = _pair_sum(place, dw_out, "rows", a_out, "pair_sum_w_out")
    (dq_t, dk, dv, dc_cols, dcq), (r_gu, r_dn, r_out) = _attn_bwd(
        qs, k_t, v, d_o, probs, probs_max, lse, delta, ta, [sum_gu[1], sum_dn[1], sum_out[1]])
    dc_rows = jnp.transpose(dcq, (1, 0, 2)).reshape(HEAD_ROWS, s)
    dzf, db_f = _forget_bwd(dc_rows, dc_cols, z_t, b_col)
    grad_x, dproj, dg_mix_pre, dcw = _inproj_bwd(dz, gate_c, u, conv_w, dq_t, dk, dv, dzf, d_b,
                                                 x, dx2, g_mix_pre, w_t, tm)
    done = [_chip_sum(sb[0], r, "chip_sum_" + n)
            for n, sb, r in zip(BIG[1:], (sum_out, sum_gu, sum_dn), (r_out, r_gu, r_dn))]
    dw_in, done_theirs = _tn_matmul(dproj, h1, IN_PAD, 1024, _tile(s, 512), "dw_in", done, piece_rows)

    (a_in,) = _pair_exchange([dw_in], ["lanes"])
    sum_in = _pair_sum(place, dw_in, "lanes", a_in, "pair_sum_w_in")
    small = dict(b_forget=db_f[:N_HEADS, 0], g_attn_out=dg_attn, g_conv_out=dg_conv, g_mix_pre=dg_mix_pre,
                 g_mix_post=dg_mix_post, g_ffn_pre=dg_ffn_pre, g_ffn_post=dg_ffn_post)
    (r_in,), small_all = _chip_exchange([sum_in[1]], _pack_small(small, dcw[:3], loss_acc[0, 0]))
    t_in = _chip_sum(sum_in[0], r_in, "chip_sum_w_in")
    (s_in,) = _pair_share([t_in], "pair_share_w_in")
    new = {"w_in": _adamw_lanes(c_idx, w_in_t, t_in, s_in, m_in_t, v_in_t, "adamw_w_in")}
    for n, mine, theirs in zip(BIG[1:], done, done_theirs):
        new[n] = _adamw(c_idx, w[n][0], mine, theirs, mom1[n][0], mom2[n][0], 2, "adamw_" + n)
    return grad_x, new, small_all


BIG = ("w_in", "w_out", "w_gate_up", "w_down")
ANY = pl.BlockSpec(memory_space=pl.ANY)


def _place():
    x, y, c = lax.axis_index("x"), lax.axis_index("y"), lax.axis_index("c")
    others = [(1 - x, y), (x, 1 - y), (1 - x, 1 - y)]
    return x, y, c, 2 * x + y, others, [2 * px + py for px, py in others]


def _remote(src, dst, send, recv, dev):
    return pltpu.make_async_remote_copy(src_ref=src, dst_ref=dst, send_sem=send, recv_sem=recv,
                                        device_id=dev, device_id_type=MESH_ID)


def _gather_stages(sh, outs, send, recv):
    x, y, c, me, others, chips = _place()
    sib = (x, y, 1 - c)
    every = [(w, kk) for w in range(len(sh)) for kk in range(3)]

    def half_of(ref, half, piece=None):
        ref = ref if piece is None else ref.at[piece]
        if len(ref.shape) == 3:
            return ref.at[half]
        hc = ref.shape[1] // 2
        return ref.at[:, pl.ds(pl.multiple_of(half * hc, 128), hc)]

    def first(w, kk):
        return _remote(half_of(sh[w], c), half_of(outs[w], c, me), send.at[w, kk], recv.at[w, kk], (*others[kk], c))

    def landed(w, kk):
        r = half_of(outs[w], c, chips[kk])
        return _remote(r, r, send.at[w, kk], recv.at[w, kk], (*others[kk], c))

    def onward(w, kk, half):
        r = half_of(outs[w], half, chips[kk])
        return _remote(r, r, send.at[w, 3 + kk], recv.at[w, 3 + kk], sib)

    def start():
        for w, kk in every:
            first(w, kk).start()

    def forward():
        for w, kk in every:
            landed(w, kk).wait_recv()
            onward(w, kk, c).start()

    def finish():
        for w, kk in every:
            onward(w, kk, 1 - c).wait_recv()
        for w, kk in every:
            first(w, kk).wait_send()
            onward(w, kk, c).wait_send()

    return start, forward, finish


def _pair_piece(ref, kind, p, half):
    if kind == "rows":
        return ref.at[p, half]
    if kind == "lanes":
        hc = ref.shape[2] // 2
        return ref.at[p, :, pl.ds(pl.multiple_of(half * hc, 128), hc)]
    cols = ref.shape[2] // N_CHIPS
    return ref.at[half, :, pl.ds(p * cols, cols)]


def _pair_stages(g, kinds, a, send, recv):
    x, y, c, _, _, _ = _place()
    copies = [_remote(_pair_piece(g[w], kinds[w], p, 1 - c), a[w].at[p], send.at[w, p], recv.at[w, p], (x, y, 1 - c))
              for w in range(len(g)) for p in range(N_CHIPS)]

    def start():
        for cp in copies:
            cp.start()

    def finish():
        for cp in copies:
            cp.wait()

    return start, finish


def _chip_stages(pb, rcv, send, recv):
    x, y, c, _, others, chips = _place()
    copies = [_remote(pb[w].at[chips[kk]], rcv[w].at[kk], send.at[w, kk], recv.at[w, kk], (*others[kk], c))
              for w in range(len(pb)) for kk in range(3)]

    def start():
        for cp in copies:
            cp.start()

    def finish():
        for cp in copies:
            cp.wait()

    return start, finish


def _gather_weights(shards, conv_w):
    n = len(shards)

    def body(*refs):
        sh, cw, outs, cwo = refs[:n], refs[n], refs[n + 1:2 * n + 1], refs[2 * n + 1]
        send, recv = refs[2 * n + 2:]
        x, y, c, me, others, chips = _place()
        start, forward, finish = _gather_stages(sh, outs, send, recv)
        start()
        small = [_remote(cw, cwo.at[me], send.at[n, kk], recv.at[n, kk], (*others[kk], c)) for kk in range(3)]
        for cp in small:
            cp.start()
        forward()
        for kk in range(3):
            _remote(cw, cwo.at[chips[kk]], send.at[n, kk], recv.at[n, kk], (*others[kk], c)).wait_recv()
        finish()
        for cp in small:
            cp.wait_send()

    out_shape = [jax.ShapeDtypeStruct((N_CHIPS,) + s.shape, s.dtype) for s in shards]
    out_shape.append(jax.ShapeDtypeStruct((N_CHIPS,) + conv_w.shape, conv_w.dtype))
    got = pl.pallas_call(
        body, name="gather_weights", in_specs=[ANY] * (n + 1), out_specs=[ANY] * (n + 1), out_shape=out_shape,
        scratch_shapes=[pltpu.SemaphoreType.DMA((n + 1, 6)), pltpu.SemaphoreType.DMA((n + 1, 6))],
    )(*shards, conv_w)
    me = 2 * lax.axis_index("x") + lax.axis_index("y")
    return [lax.dynamic_update_index_in_dim(g, own, me, 0) for g, own in zip(got, list(shards) + [conv_w])]


def _taken_shape(g, kind):
    if kind == "rows":
        return (N_CHIPS,) + g.shape[2:]
    if kind == "lanes":
        return g.shape[:2] + (g.shape[2] // 2,)
    return (N_CHIPS, g.shape[1], g.shape[2] // N_CHIPS)


def _pair_sum(place, g, kind, a, name, ready=(), ready_kinds=()):
    _, half, cols = a.shape
    nw = len(ready)
    if kind == "rows":
        mine = pl.BlockSpec((1, 1, half, cols), lambda p, pr: (p, pr[0], 0, 0))
    elif kind == "lanes":
        mine = pl.BlockSpec((1, half, cols), lambda p, pr: (p, 0, pr[0]))
    else:
        mine = pl.BlockSpec((1, half, cols), lambda p, pr: (pr[0], 0, p))

    def body(place_ref, g_ref, a_ref, *rest):
        grads, (own_ref, pb_ref), taken = rest[:nw], rest[nw:nw + 2], rest[nw + 2:2 * nw + 2]
        if nw:
            pair_start, pair_finish = _pair_stages(grads, ready_kinds, taken, *rest[2 * nw + 2:])
            pl.when(pl.program_id(0) == 0)(pair_start)
        tot = (g_ref[0, 0] if kind == "rows" else g_ref[0]) + a_ref[0]
        pb_ref[0] = tot.astype(BF16)

        @pl.when(pl.program_id(0) == place_ref[1])
        def _():
            own_ref[...] = tot

        if nw:
            pl.when(pl.program_id(0) == N_CHIPS - 1)(pair_finish)

    sems = [pltpu.SemaphoreType.DMA((nw, N_CHIPS)), pltpu.SemaphoreType.DMA((nw, N_CHIPS))] if nw else []
    gs = pltpu.PrefetchScalarGridSpec(
        num_scalar_prefetch=1, grid=(N_CHIPS,),
        in_specs=[mine, pl.BlockSpec((1, half, cols), lambda p, pr: (p, 0, 0))] + [ANY] * nw,
        out_specs=[pl.BlockSpec((half, cols), lambda p, pr: (0, 0)),
                   pl.BlockSpec((1, half, cols), lambda p, pr: (p, 0, 0))] + [ANY] * nw,
        scratch_shapes=sems)
    out = pl.pallas_call(
        body, name=name, grid_spec=gs,
        out_shape=[jax.ShapeDtypeStruct((half, cols), F32), jax.ShapeDtypeStruct((N_CHIPS, half, cols), BF16)]
        + [jax.ShapeDtypeStruct(_taken_shape(r, kd), r.dtype) for r, kd in zip(ready, ready_kinds)],
        compiler_params=_params(("arbitrary",)),
    )(place, g, a, *ready)
    return list(out)


def _chip_sum(own, rcv, name):
    half, cols = own.shape

    def body(o_ref, r_ref, t_ref):
        t_ref[...] = ((o_ref[...] + r_ref[0].astype(F32)) + r_ref[1].astype(F32)) + r_ref[2].astype(F32)

    return pl.pallas_call(
        body, name=name, grid=(1,),
        in_specs=[pl.BlockSpec((half, cols), lambda i: (0, 0)), pl.BlockSpec((3, half, cols), lambda i: (0, 0, 0))],
        out_specs=pl.BlockSpec((half, cols), lambda i: (0, 0)),
        out_shape=jax.ShapeDtypeStruct((half, cols), F32), compiler_params=_params(("arbitrary",)),
    )(own, rcv)


def _small_stages(sm, smg, send, recv):
    x, y, c, _, _, _ = _place()

    def peer(r):
        return (1 - x if r & 4 else x, 1 - y if r & 2 else y, 1 - c if r & 1 else c)

    mine = 4 * x + 2 * y + c
    copies = [_remote(sm, smg.at[mine], send.at[r - 1], recv.at[r - 1], peer(r)) for r in range(1, 8)]

    def start():
        for cp in copies:
            cp.start()

    def finish():
        for r in range(1, 8):
            px, py, pc = peer(r)
            _remote(sm, smg.at[4 * px + 2 * py + pc], send.at[r - 1], recv.at[r - 1], (px, py, pc)).wait_recv()
        for cp in copies:
            cp.wait_send()

    return start, finish


def _pair_exchange(grads, kinds):
    n = len(grads)

    def body(*refs):
        start, finish = _pair_stages(refs[:n], kinds, refs[n:2 * n], *refs[2 * n:])
        start()
        finish()

    return pl.pallas_call(
        body, name="pair_exchange", in_specs=[ANY] * n, out_specs=[ANY] * n,
        out_shape=[jax.ShapeDtypeStruct(_taken_shape(g, kd), g.dtype) for g, kd in zip(grads, kinds)],
        scratch_shapes=[pltpu.SemaphoreType.DMA((n, N_CHIPS)), pltpu.SemaphoreType.DMA((n, N_CHIPS))],
    )(*grads)


def _chip_exchange(parts, small):
    n = len(parts)

    def body(*refs):
        pb, sm, rcv, smg = refs[:n], refs[n], refs[n + 1:2 * n + 1], refs[2 * n + 1]
        send, recv, ssend, srecv = refs[2 * n + 2:]
        chip_start, chip_finish = _chip_stages(pb, rcv, send, recv)
        small_start, small_finish = _small_stages(sm, smg, ssend, srecv)
        chip_start()
        small_start()
        chip_finish()
        small_finish()

    out_shape = [jax.ShapeDtypeStruct((3,) + p.shape[1:], p.dtype) for p in parts]
    out_shape.append(jax.ShapeDtypeStruct((8,) + small.shape, small.dtype))
    *arrived, small_land = pl.pallas_call(
        body, name="chip_exchange", in_specs=[ANY] * (n + 1), out_specs=[ANY] * (n + 1), out_shape=out_shape,
        scratch_shapes=[pltpu.SemaphoreType.DMA((n, 3)), pltpu.SemaphoreType.DMA((n, 3)),
                        pltpu.SemaphoreType.DMA((7,)), pltpu.SemaphoreType.DMA((7,))],
    )(*parts, small)
    mine = 4 * lax.axis_index("x") + 2 * lax.axis_index("y") + lax.axis_index("c")
    return arrived, lax.dynamic_update_index_in_dim(small_land, small, mine, 0)


def _share_stages(t, g, send, recv):
    x, y, c, _, _, _ = _place()
    copies = [_remote(t[w], g[w], send.at[w], recv.at[w], (x, y, 1 - c)) for w in range(len(t))]

    def start():
        for cp in copies:
            cp.start()

    def finish():
        for cp in copies:
            cp.wait()

    return start, finish


def _pair_share(totals, name):
    n = len(totals)

    def body(*refs):
        start, finish = _share_stages(refs[:n], refs[n:2 * n], *refs[2 * n:])
        start()
        finish()

    return pl.pallas_call(
        body, name=name, in_specs=[ANY] * n, out_specs=[ANY] * n,
        out_shape=[jax.ShapeDtypeStruct(t.shape, t.dtype) for t in totals],
        scratch_shapes=[pltpu.SemaphoreType.DMA((n,)), pltpu.SemaphoreType.DMA((n,))],
    )(*totals)


def _adamw_math(w, g, m, v):
    m = ADAM_B1 * m + (1.0 - ADAM_B1) * g
    v = ADAM_B2 * v + (1.0 - ADAM_B2) * (g * g)
    m_hat = m / (1.0 - ADAM_B1 ** ADAM_STEP)
    v_hat = v / (1.0 - ADAM_B2 ** ADAM_STEP)
    delta = -ADAM_LR * (m_hat / (jnp.sqrt(v_hat) + ADAM_EPS) + ADAM_WD * w)
    return delta, m, v


def _adamw(c_idx, w, mine, theirs, m, v, nb, name):
    rows, cols = w.shape
    tr = rows // (2 * nb)

    def body(c_ref, w_ref, a_ref, b_ref, m_ref, v_ref, g_ref, d_ref, nm_ref, nv_ref):
        g = jnp.where(pl.program_id(0) == c_ref[0], a_ref[...], b_ref[...])
        g_ref[...] = g
        d_ref[...], nm_ref[...], nv_ref[...] = _adamw_math(w_ref[...], g, m_ref[...], v_ref[...])

    full = pl.BlockSpec((tr, cols), lambda hh, i, cr: (hh * nb + i, 0))
    half = pl.BlockSpec((tr, cols), lambda hh, i, cr: (i, 0))
    gs = pltpu.PrefetchScalarGridSpec(num_scalar_prefetch=1, grid=(2, nb), in_specs=[full, half, half, full, full],
                                      out_specs=[full] * 4)
    return pl.pallas_call(
        body, name=name, grid_spec=gs, out_shape=[jax.ShapeDtypeStruct((rows, cols), F32)] * 4,
        compiler_params=_params(("arbitrary", "arbitrary")),
    )(c_idx, w, mine, theirs, m, v)


def _adamw_lanes(c_idx, w, mine, theirs, m, v, name):
    rows, _, cols = w.shape
    hc = cols // 2

    def body(c_ref, w_ref, a_ref, b_ref, m_ref, v_ref, g_ref, d_ref, nm_ref, nv_ref):
        g = jnp.where(pl.program_id(0) == c_ref[0], a_ref[...], b_ref[...])
        g_ref[:, 0, :] = g
        d_ref[:, 0, :], nm_ref[:, 0, :], nv_ref[:, 0, :] = _adamw_math(w_ref[:, 0, :], g, m_ref[:, 0, :], v_ref[:, 0, :])

    full = pl.BlockSpec((rows, 1, hc), lambda hh, cr: (0, 0, hh))
    half = pl.BlockSpec((rows, hc), lambda hh, cr: (0, 0))
    gs = pltpu.PrefetchScalarGridSpec(num_scalar_prefetch=1, grid=(2,), in_specs=[full, half, half, full, full],
                                      out_specs=[full] * 4)
    return pl.pallas_call(
        body, name=name, grid_spec=gs, out_shape=[jax.ShapeDtypeStruct((rows, 1, cols), F32)] * 4,
        compiler_params=_params(("arbitrary",)),
    )(c_idx, w, mine, theirs, m, v)


SMALL = ("g_mix_pre", "g_mix_post", "g_ffn_pre", "g_ffn_post")
SMALL_ALL = SMALL + ("g_attn_out", "g_conv_out", "conv_w", "b_forget")
SMALL_AT = {"g_mix_pre": (0, 0, 1024), "g_mix_post": (1, 0, 1024), "g_ffn_pre": (2, 0, 1024),
            "g_ffn_post": (3, 0, 1024), "g_attn_out": (4, 0, 512), "g_conv_out": (4, 512, 512),
            "b_forget": (7, 0, N_HEADS)}
CONV_AT = ((5, 0), (5, 512), (6, 0))
LOSS_AT = (6, 512)


def _pack_small(t, conv_full, loss_sum):
    conv = jnp.concatenate([conv_full.reshape(1, 3 * CONV_W), loss_sum.reshape(1, 1),
                            jnp.zeros((1, 2048 - 3 * CONV_W - 1), F32)], axis=1).reshape(2, 1024)
    return jnp.concatenate([t[n].reshape(1, 1024) for n in SMALL]
                           + [jnp.concatenate([t["g_attn_out"].reshape(1, 512), t["g_conv_out"].reshape(1, 512)], axis=1),
                              conv, jnp.pad(t["b_forget"].reshape(1, N_HEADS), ((0, 0), (0, 1024 - N_HEADS)))], axis=0)


def _small_update(me_idx, gathered, w, m, v):
    def body(me_ref, gg_ref, *refs):
        k = len(SMALL_ALL)
        w_refs, m_refs, v_refs = refs[:k], refs[k:2 * k], refs[2 * k:3 * k]
        loss_ref = refs[3 * k]
        outs = refs[3 * k + 1:3 * k + 1 + 4 * k]
        sums = refs[-1]
        g = gg_ref[0]
        for dev in range(1, 8):
            g = g + gg_ref[dev]
        sums[...] = g
        loss_ref[...] = sums[LOSS_AT[0]:LOSS_AT[0] + 1, LOSS_AT[1]:LOSS_AT[1] + 1]
        mine = pl.multiple_of(me_ref[0] * 128, 128)
        for idx, name in enumerate(SMALL_ALL):
            g_ref, d_ref, nm_ref, nv_ref = outs[4 * idx:4 * idx + 4]
            if name == "conv_w":
                for r, (row, lo) in enumerate(CONV_AT):
                    gr = sums[row:row + 1, pl.ds(lo + mine, 128)]
                    g_ref[r] = gr
                    d_ref[r], nm_ref[r], nv_ref[r] = _adamw_math(w_refs[idx][r], gr, m_refs[idx][r], v_refs[idx][r])
            else:
                row, lo, n = SMALL_AT[name]
                gr = sums[row:row + 1, lo:lo + n]
                g_ref[...] = gr
                d_ref[...], nm_ref[...], nv_ref[...] = _adamw_math(w_refs[idx][...], gr, m_refs[idx][...],
                                                                    v_refs[idx][...])

    def whole(a):
        nd = a.ndim
        return pl.BlockSpec(a.shape, lambda i, mr: (0,) * nd)

    rows_first = lambda n, a: jnp.transpose(a, (1, 0, 2)) if n == "conv_w" else a
    ins = [rows_first(n, t[n]) for t in (w, m, v) for n in SMALL_ALL]
    out_shape = [jax.ShapeDtypeStruct((1, 1), F32)]
    for n in SMALL_ALL:
        out_shape += [jax.ShapeDtypeStruct(rows_first(n, w[n]).shape, F32)] * 4
    gs = pltpu.PrefetchScalarGridSpec(
        num_scalar_prefetch=1, grid=(1,), in_specs=[whole(gathered)] + [whole(a) for a in ins],
        out_specs=[whole(o) for o in out_shape], scratch_shapes=[pltpu.VMEM((8, 1024), F32)])
    out = pl.pallas_call(body, name="small_update", grid_spec=gs, out_shape=out_shape,
                         compiler_params=_params(("arbitrary",)))(me_idx, gathered, *ins)
    return out[0], {n: [rows_first(n, r) for r in out[1 + 4 * i:5 + 4 * i]] for i, n in enumerate(SMALL_ALL)}


def kernel(x, w_in, b_forget, conv_w, g_attn_out, g_conv_out, w_out, g_mix_pre, g_mix_post, w_gate_up, w_down, g_ffn_pre, g_ffn_post, loss_target, m_w_in, m_b_forget, m_conv_w, m_g_attn_out, m_g_conv_out, m_w_out, m_g_mix_pre, m_g_mix_post, m_w_gate_up, m_w_down, m_g_ffn_pre, m_g_ffn_post, v_w_in, v_b_forget, v_conv_w, v_g_attn_out, v_g_conv_out, v_w_out, v_g_mix_pre, v_g_mix_post, v_w_gate_up, v_w_down, v_g_ffn_pre, v_g_ffn_post):
    w = dict(w_in=w_in, b_forget=b_forget, conv_w=conv_w, g_attn_out=g_attn_out, g_conv_out=g_conv_out, w_out=w_out,
             g_mix_pre=g_mix_pre, g_mix_post=g_mix_post, w_gate_up=w_gate_up, w_down=w_down, g_ffn_pre=g_ffn_pre,
             g_ffn_post=g_ffn_post)
    m = dict(w_in=m_w_in, b_forget=m_b_forget, conv_w=m_conv_w, g_attn_out=m_g_attn_out, g_conv_out=m_g_conv_out,
             w_out=m_w_out, g_mix_pre=m_g_mix_pre, g_mix_post=m_g_mix_post, w_gate_up=m_w_gate_up, w_down=m_w_down,
             g_ffn_pre=m_g_ffn_pre, g_ffn_post=m_g_ffn_post)
    v = dict(w_in=v_w_in, b_forget=v_b_forget, conv_w=v_conv_w, g_attn_out=v_g_attn_out, g_conv_out=v_g_conv_out,
             w_out=v_w_out, g_mix_pre=v_g_mix_pre, g_mix_post=v_g_mix_post, w_gate_up=v_w_gate_up, w_down=v_w_down,
             g_ffn_pre=v_g_ffn_pre, g_ffn_post=v_g_ffn_post)
    cx, cy, cc = lax.axis_index("x"), lax.axis_index("y"), lax.axis_index("c")
    me = 2 * cx + cy
    c_idx = cc.astype(jnp.int32).reshape(1)
    me_idx = me.astype(jnp.int32).reshape(1)

    stored = lambda a: jnp.transpose(a, (2, 0, 1))
    grad_x, big, small_all = _device_step(x[0], loss_target[0], w, m, v, stored(w_in), stored(m_w_in),
                                          stored(v_w_in), c_idx, me_idx)
    gsum, delta, new_m, new_v = {}, {}, {}, {}
    for n in BIG:
        back = (lambda r: jnp.transpose(r, (1, 2, 0))) if n == "w_in" else (lambda r: r[None])
        gsum[n], delta[n], new_m[n], new_v[n] = [back(r) for r in big[n]]
    loss_sum, small_new = _small_update(me_idx, small_all, w, m, v)
    for n in SMALL_ALL:
        gsum[n], delta[n], new_m[n], new_v[n] = small_new[n]
    loss = 0.5 * loss_sum[0, 0]

    order = ("w_in", "b_forget", "conv_w", "g_attn_out", "g_conv_out", "w_out", "g_mix_pre", "g_mix_post",
             "w_gate_up", "w_down", "g_ffn_pre", "g_ffn_post")
    return (loss, grad_x[None], *[gsum[n] for n in order], *[delta[n] for n in order],
            *[new_m[n] for n in order], *[new_v[n] for n in order])
```

```python
import jax
import jax.numpy as jnp
import numpy as np
from jax import lax
from jax.experimental import pallas as pl
from jax.experimental.pallas import tpu as pltpu

F32 = jnp.float32
BF16 = jnp.bfloat16
MXU_DTYPE = jnp.bfloat16

D_MODEL = 1024
HEAD_DIM = 64
N_HEADS = 8
ATTN_W = 512
CONV_W = 512
D_FF = 2816
FF_PIECE = 1408
EPS = 1e-6
Q_SCALE = HEAD_DIM ** -0.5

OFF_F = 1536
OFF_B = 1664
OFF_C = 2176
OFF_U = 2688
IN_PAD = 3200
IN_W = 3080
N_CHIPS = 4

ADAM_LR = 0.001
ADAM_B1 = 0.9
ADAM_B2 = 0.999
ADAM_EPS = 1e-08
ADAM_WD = 0.01
ADAM_STEP = 10

VMEM_LIMIT_V7X = 56 * 1024 * 1024
MESH_ID = pl.DeviceIdType.MESH


def _params(sem=None, vmem=VMEM_LIMIT_V7X):
    kw = {"vmem_limit_bytes": vmem}
    if sem is not None:
        kw["dimension_semantics"] = sem
    return pltpu.CompilerParams(**kw)


def _dot(a, b):
    return jnp.dot(a, b, preferred_element_type=F32)


def _dot_nt(a, b):
    return lax.dot_general(a, b, (((1,), (1,)), ((), ())), preferred_element_type=F32)


def _dot_exact(x, ones, parts):
    if ones.dtype == F32:
        return _dot(x, ones)
    acc = None
    rem = x
    for _ in range(parts):
        piece = rem.astype(BF16)
        rem = rem - piece.astype(F32)
        term = _dot(piece, ones)
        acc = term if acc is None else acc + term
    return acc


def _rms(v):
    return lax.rsqrt(jnp.mean(v * v, axis=-1, keepdims=True) + EPS)


def _tok(tm, w):
    return pl.BlockSpec((tm, w), lambda i: (i, 0))


def _whole(shape, single=False):
    nd = len(shape)
    if single:
        return pl.BlockSpec(shape, lambda i: (0,) * nd, pipeline_mode=pl.Buffered(1))
    return pl.BlockSpec(shape, lambda i: (0,) * nd)


def _feat(rows, tm):
    return pl.BlockSpec((rows, tm), lambda i: (0, i))


def _piece_windows(rows, first, last):
    gap_at = OFF_F + N_HEADS
    runs = []
    for p in range(N_CHIPS):
        lo, hi = p * rows, (p + 1) * rows
        for a, b, shift in ((lo, min(hi, gap_at), 0), (max(lo, gap_at), hi, OFF_B - gap_at)):
            a, b = max(a + shift, first), min(b + shift, last)
            if a < b:
                runs.append((p, a - shift - lo, a, b - a))
    return runs


def _padded_rows(pieces_ref, w_ref):
    for p, at, dst, count in _piece_windows(pieces_ref.shape[1], 0, IN_PAD):
        w_ref[dst:dst + count, :] = pieces_ref[p, at:at + count, :]
    w_ref[OFF_F + N_HEADS:OFF_B, :] = jnp.zeros((OFF_B - OFF_F - N_HEADS, w_ref.shape[1]), w_ref.dtype)


def _inproj_fwd(x, g_pre, pieces, tm):
    s = x.shape[0]

    def body(x_ref, g_ref, pieces_ref, h_ref, q_ref, k_ref, v_ref, kt_ref, vt_ref, zt_ref, b_ref, c_ref, u_ref,
             w_ref):
        @pl.when(pl.program_id(0) == 0)
        def _():
            _padded_rows(pieces_ref, w_ref)

        xv = x_ref[...]
        h = ((xv * _rms(xv)) * g_ref[...]).astype(MXU_DTYPE)
        h_ref[...] = h

        def proj(lo, hi):
            return _dot_nt(h, w_ref[lo:hi, :])

        q_ref[...] = (proj(0, 512) * Q_SCALE).astype(MXU_DTYPE)
        kt = _dot_nt(w_ref[512:1024, :], h)
        vt = _dot_nt(w_ref[1024:OFF_F, :], h)
        kt_ref[...] = kt.astype(MXU_DTYPE)
        vt_ref[...] = vt.astype(MXU_DTYPE)
        k_ref[...] = kt.T.astype(MXU_DTYPE)
        v_ref[...] = vt.T.astype(MXU_DTYPE)
        zt_ref[...] = _dot_nt(w_ref[OFF_F:OFF_B, :], h)
        b_ref[...] = proj(OFF_B, OFF_C).astype(MXU_DTYPE)
        c_ref[...] = proj(OFF_C, OFF_U).astype(MXU_DTYPE)
        u_ref[...] = proj(OFF_U, IN_PAD).astype(MXU_DTYPE)

    sd = jax.ShapeDtypeStruct
    return pl.pallas_call(
        body, name="inproj_fwd", grid=(s // tm,),
        in_specs=[_tok(tm, D_MODEL), _whole((1, D_MODEL)), _whole(pieces.shape, single=True)],
        out_specs=[_tok(tm, D_MODEL), _tok(tm, 512), _tok(tm, 512), _tok(tm, 512), _feat(512, tm), _feat(512, tm),
                   _feat(128, tm), _tok(tm, 512), _tok(tm, 512), _tok(tm, 512), _whole((IN_PAD, D_MODEL))],
        out_shape=[sd((s, D_MODEL), MXU_DTYPE), sd((s, 512), MXU_DTYPE), sd((s, 512), MXU_DTYPE),
                   sd((s, 512), MXU_DTYPE), sd((512, s), MXU_DTYPE), sd((512, s), MXU_DTYPE), sd((128, s), F32),
                   sd((s, 512), MXU_DTYPE), sd((s, 512), MXU_DTYPE), sd((s, 512), MXU_DTYPE),
                   sd((IN_PAD, D_MODEL), MXU_DTYPE)],
        compiler_params=_params(("arbitrary",)),
    )(x, g_pre, pieces)


def _tri(n, upper):
    r = lax.broadcasted_iota(jnp.int32, (n, n), 0)
    c = lax.broadcasted_iota(jnp.int32, (n, n), 1)
    return ((r <= c) if upper else (r >= c)).astype(MXU_DTYPE)


HEAD_ROWS = 16


def _rows_to_cols(v):
    return jnp.concatenate([v, jnp.zeros((128 - HEAD_ROWS, 128), F32)], axis=0).T


BIAS_PARTS = 3


def _bias_placement():
    place_q = np.zeros((BIAS_PARTS, 128, ATTN_W), np.float32)
    place_k = np.zeros((BIAS_PARTS, 128, ATTN_W), np.float32)
    ones_q = np.zeros((1, ATTN_W), np.float32)
    ones_k = np.zeros((1, ATTN_W), np.float32)
    for h in range(N_HEADS):
        base = 2 * HEAD_DIM * (h // 2) + HEAD_DIM * (1 - h % 2)
        for part in range(BIAS_PARTS):
            place_q[part, h, base + part] = 1.0
            place_k[part, h, base + BIAS_PARTS + part] = -1.0
        ones_q[0, base + BIAS_PARTS:base + 2 * BIAS_PARTS] = 1.0
        ones_k[0, base:base + BIAS_PARTS] = 1.0
    return (jnp.asarray(place_q, MXU_DTYPE), jnp.asarray(place_k, MXU_DTYPE), jnp.asarray(ones_q), jnp.asarray(ones_k))


def _forget_fwd(z_t, b_col):
    s = z_t.shape[1]
    nb = s // 128

    def body(z_ref, b_ref, pq_ref, pk_ref, oq_ref, ok_ref, qa_ref, ka_ref, cc_ref):
        upper = _tri(128, True)

        carry = jnp.zeros((HEAD_ROWS, 1), F32)
        for n in range(nb):
            off = n * 128
            lf = jax.nn.log_sigmoid(z_ref[0:HEAD_ROWS, off:off + 128] + b_ref[...])
            cc_ref[off:off + 128, :] = _rows_to_cols(_dot_exact(lf, upper, 3) + carry)
            carry = carry + jnp.sum(lf, axis=1, keepdims=True)

        rb = min(s, 512)
        for off in range(0, s, rb):
            qa = jnp.broadcast_to(oq_ref[...], (rb, ATTN_W))
            ka = jnp.broadcast_to(ok_ref[...], (rb, ATTN_W))
            rem = cc_ref[off:off + rb, :]
            for part in range(BIAS_PARTS):
                piece = rem.astype(MXU_DTYPE)
                rem = rem - piece.astype(F32)
                qa = qa + _dot(piece, pq_ref[part])
                ka = ka + _dot(piece, pk_ref[part])
            qa_ref[off:off + rb, :] = qa.astype(MXU_DTYPE)
            ka_ref[off:off + rb, :] = ka.astype(MXU_DTYPE)

    sd = jax.ShapeDtypeStruct
    return pl.pallas_call(body, name="forget_fwd",
                          out_shape=[sd((s, ATTN_W), MXU_DTYPE), sd((s, ATTN_W), MXU_DTYPE)],
                          scratch_shapes=[pltpu.VMEM((s, 128), F32)],
                          compiler_params=_params())(z_t, b_col, *_bias_placement())


def _aligned(start, size):
    return pl.ds(start if isinstance(start, int) else pl.multiple_of(start, size), size)


def _pair_lanes(pp):
    return _aligned(pp * 2 * HEAD_DIM, 2 * HEAD_DIM)


def _head_rows(h):
    return _aligned(h * HEAD_DIM, HEAD_DIM)


def _only_head(block, hb):
    lane = lax.broadcasted_iota(jnp.int32, block.shape, 1)
    return jnp.where((lane >= HEAD_DIM) if hb else (lane < HEAD_DIM), block, jnp.zeros_like(block))


def _other_head(block, other, hb):
    lane = lax.broadcasted_iota(jnp.int32, block.shape, 1)
    return jnp.where((lane >= HEAD_DIM) if hb else (lane < HEAD_DIM), block, other)


def _attn_fwd(qs, k, v_t, q_bias, k_bias, t, shards):
    s = qs.shape[0]
    n = s // t
    pairs = [(i, j) for i in range(n) for j in range(i + 1)]
    it = jnp.asarray(np.array([p[0] for p in pairs], np.int32))
    jt = jnp.asarray(np.array([p[1] for p in pairs], np.int32))
    nw = len(shards)
    last = len(pairs) - 1
    mid = (2 * len(pairs)) // 3

    def body(it_ref, jt_ref, q_ref, k_ref, vt_ref, qb_ref, kb_ref, *rest):
        sh, (o_ref, lse_ref, p_ref, pm_ref), got = rest[:nw], rest[nw:nw + 4], rest[nw + 4:2 * nw + 4]
        m_sc, l_sc, acc_sc, send, recv = rest[2 * nw + 4:]
        p = pl.program_id(0)
        i = it_ref[p]
        j = jt_ref[p]
        gather_start, gather_forward, gather_finish = _gather_stages(sh, got, send, recv)
        pl.when(p == 0)(gather_start)
        if mid < last:
            pl.when(p == mid)(gather_forward)

        @pl.when(j == 0)
        def _():
            m_sc[...] = jnp.full_like(m_sc, -1e30)
            l_sc[...] = jnp.ones_like(l_sc)
            acc_sc[...] = jnp.zeros_like(acc_sc)

        pm_ref[...] = jnp.zeros_like(pm_ref)

        def pair_step(pp, diagonal):
            lanes = _pair_lanes(pp)
            kp = k_ref[:, lanes]
            qp = q_ref[:, lanes]
            kb = kb_ref[:, lanes]
            qb = qb_ref[:, lanes]
            for hb in range(2):
                h = 2 * pp + hb
                row = pl.ds(h, 1)
                rows = _head_rows(h)
                st = _dot_nt(_other_head(kp, kb, hb), _other_head(qp, qb, hb))
                if diagonal:
                    kpos = lax.broadcasted_iota(jnp.int32, (t, t), 0)
                    qpos = lax.broadcasted_iota(jnp.int32, (t, t), 1)
                    st = jnp.where(kpos <= qpos, st, -1e30)
                m_prev = m_sc[row, :]
                m_new = jnp.maximum(m_prev, jnp.max(st, axis=0, keepdims=True))
                alpha = jnp.exp(m_prev - m_new)
                pt = jnp.exp(st - m_new)
                l_sc[row, :] = alpha * l_sc[row, :] + jnp.sum(pt, axis=0, keepdims=True)
                ptb = pt.astype(MXU_DTYPE)
                acc_sc[rows, :] = acc_sc[rows, :] * alpha + _dot(vt_ref[rows, :], ptb)
                m_sc[row, :] = m_new
                p_ref[0, h] = ptb
                pm_ref[0, row, :] = m_new

        @pl.when(j < i)
        def _():
            for pp in range(N_HEADS // 2):
                pair_step(pp, False)

        @pl.when(j == i)
        def _():
            for pp in range(N_HEADS // 2):
                pair_step(pp, True)
                sub = lax.broadcasted_iota(jnp.int32, (2 * HEAD_DIM, t), 0)
                l_pair = jnp.where(sub < HEAD_DIM, l_sc[pl.ds(2 * pp, 1), :], l_sc[pl.ds(2 * pp + 1, 1), :])
                o_t = acc_sc[_aligned(pp * 2 * HEAD_DIM, 2 * HEAD_DIM), :] / l_pair
                o_ref[:, _pair_lanes(pp)] = o_t.T.astype(MXU_DTYPE)

            lse_ref[...] = m_sc[...] + jnp.log(l_sc[...])

        @pl.when(p == last)
        def _():
            if mid >= last:
                gather_forward()
            gather_finish()

    gs = pltpu.PrefetchScalarGridSpec(
        num_scalar_prefetch=2, grid=(len(pairs),),
        in_specs=[pl.BlockSpec((t, ATTN_W), lambda p, it_, jt_: (it_[p], 0)),
                  pl.BlockSpec((t, ATTN_W), lambda p, it_, jt_: (jt_[p], 0)),
                  pl.BlockSpec((ATTN_W, t), lambda p, it_, jt_: (0, jt_[p])),
                  pl.BlockSpec((t, ATTN_W), lambda p, it_, jt_: (it_[p], 0)),
                  pl.BlockSpec((t, ATTN_W), lambda p, it_, jt_: (jt_[p], 0))] + [ANY] * nw,
        out_specs=[pl.BlockSpec((t, ATTN_W), lambda p, it_, jt_: (it_[p], 0)),
                   pl.BlockSpec((HEAD_ROWS, t), lambda p, it_, jt_: (0, it_[p])),
                   pl.BlockSpec((1, N_HEADS, t, t), lambda p, it_, jt_: (p, 0, 0, 0)),
                   pl.BlockSpec((1, HEAD_ROWS, t), lambda p, it_, jt_: (p, 0, 0))] + [ANY] * nw,
        scratch_shapes=[pltpu.VMEM((HEAD_ROWS, t), F32), pltpu.VMEM((HEAD_ROWS, t), F32), pltpu.VMEM((ATTN_W, t), F32),
                        pltpu.SemaphoreType.DMA((nw, 6)), pltpu.SemaphoreType.DMA((nw, 6))])
    sd = jax.ShapeDtypeStruct
    o, lse, probs, probs_max, *got = pl.pallas_call(
        body, name="attn_fwd", grid_spec=gs,
        out_shape=[sd((s, ATTN_W), MXU_DTYPE), sd((HEAD_ROWS, s), F32), sd((len(pairs), N_HEADS, t, t), MXU_DTYPE),
                   sd((len(pairs), HEAD_ROWS, t), F32)]
        + [sd((N_CHIPS,) + a.shape, a.dtype) for a in shards],
        compiler_params=_params(("arbitrary",)),
    )(it, jt, qs, k, v_t, q_bias, k_bias, *shards)
    me = 2 * lax.axis_index("x") + lax.axis_index("y")
    return o, lse, probs, probs_max, [lax.dynamic_update_index_in_dim(g, own, me, 0) for g, own in zip(got, shards)]


HALO = 16


def _halo_before(tm):
    return pl.BlockSpec((HALO, CONV_W), lambda i: (jnp.maximum(i * (tm // HALO) - 1, 0), 0))


def _shift_down(cur, prev_ref, first, cols=slice(None)):
    row = lax.broadcasted_iota(jnp.int32, cur.shape, 0)

    def before(r):
        prod = prev_ref[0][r:r + 1, cols].astype(F32) * prev_ref[1][r:r + 1, cols].astype(F32)
        return jnp.where(first, 0.0, prod)

    p7, p6 = before(HALO - 1), before(HALO - 2)
    s1 = jnp.where(row == 0, p7, pltpu.roll(cur, 1, 0))
    s2 = jnp.where(row == 0, p6, jnp.where(row == 1, p7, pltpu.roll(cur, 2, 0)))
    return s1, s2


def _group_ms(v, gmat):
    return _dot_exact(v, gmat, 1) * (1.0 / HEAD_DIM)


def _mixer_fwd(x, o_attn, gate_b, gate_c, u, conv_w, g_attn, g_conv, w_out, g_post, gmat, tm):
    s = x.shape[0]

    def body(x_ref, o_ref, b_ref, c_ref, u_ref, cp_ref, up_ref, cw_ref, ga_ref, gc_ref, wo_ref, gp_ref, gm_ref,
             x2_ref, mg_ref, y_ref, z_ref):
        i = pl.program_id(0)
        gm = gm_ref[0:128, 0:128]
        for lo in range(0, ATTN_W, 128):
            cols = slice(lo, lo + 128)
            cu = c_ref[:, cols].astype(F32) * u_ref[:, cols].astype(F32)
            cu1, cu2 = _shift_down(cu, (cp_ref, up_ref), i == 0, cols)
            z = cw_ref[0:1, cols] * cu2 + cw_ref[1:2, cols] * cu1 + cw_ref[2:3, cols] * cu
            z_ref[:, cols] = z.astype(MXU_DTYPE)
            cv = b_ref[:, cols].astype(F32) * z
            ov = o_ref[:, cols].astype(F32)
            mg_ref[:, cols] = ((ov * lax.rsqrt(_group_ms(ov * ov, gm) + EPS)) * ga_ref[:, cols]).astype(MXU_DTYPE)
            mg_ref[:, ATTN_W + lo:ATTN_W + lo + 128] = (
                (cv * lax.rsqrt(_group_ms(cv * cv, gm) + EPS)) * gc_ref[:, cols]).astype(MXU_DTYPE)
        y = _dot(mg_ref[...], wo_ref[...])
        y_ref[...] = y
        x2_ref[...] = x_ref[...] + (y * _rms(y)) * gp_ref[...]

    halo = _halo_before(tm)
    sd = jax.ShapeDtypeStruct
    return pl.pallas_call(
        body, name="mixer_fwd", grid=(s // tm,),
        in_specs=[_tok(tm, D_MODEL), _tok(tm, 512), _tok(tm, 512), _tok(tm, 512), _tok(tm, 512), halo, halo,
                  _whole((3, 512)), _whole((1, 512)), _whole((1, 512)), _whole((D_MODEL, D_MODEL), single=True),
                  _whole((1, D_MODEL)), _whole((512, 512))],
        out_specs=[_tok(tm, D_MODEL), _tok(tm, D_MODEL), _tok(tm, D_MODEL), _tok(tm, 512)],
        out_shape=[sd((s, D_MODEL), F32), sd((s, D_MODEL), MXU_DTYPE), sd((s, D_MODEL), F32), sd((s, 512), MXU_DTYPE)],
        compiler_params=_params(("arbitrary",)),
    )(x, o_attn, gate_b, gate_c, u, gate_c, u, conv_w, g_attn, g_conv, w_out, g_post, gmat)


def _ffn_fwd(x2, target, g_pre, w_gu, w_dn, g_post, tm):
    s = x2.shape[0]

    def body(x_ref, t_ref, gpre_ref, wgu_ref, wdn_ref, gpost_ref,
             h_ref, g_ref, up_ref, a_ref, ff_ref, dout_ref, loss_ref):
        xv = x_ref[...]
        h = ((xv * _rms(xv)) * gpre_ref[...]).astype(MXU_DTYPE)
        h_ref[...] = h
        ff = jnp.zeros((tm, D_MODEL), F32)
        for j in range(2):
            cols = slice(j * FF_PIECE, (j + 1) * FF_PIECE)
            g = _dot(h, wgu_ref[j])
            up = _dot(h, wgu_ref[2 + j])
            a = ((g * jax.nn.sigmoid(g)) * up).astype(MXU_DTYPE)
            g_ref[:, cols] = g.astype(MXU_DTYPE)
            up_ref[:, cols] = up.astype(MXU_DTYPE)
            a_ref[:, cols] = a
            ff = ff + _dot(a, wdn_ref[j])
        ff_ref[...] = ff
        err = (xv + (ff * _rms(ff)) * gpost_ref[...]) - t_ref[...]
        dout_ref[...] = err * (1.0 / D_MODEL)
        part = jnp.sum(jnp.mean(err * err, axis=-1, keepdims=True), axis=0, keepdims=True)

        @pl.when(pl.program_id(0) == 0)
        def _():
            loss_ref[...] = jnp.zeros_like(loss_ref)

        loss_ref[...] += part

    sd = jax.ShapeDtypeStruct
    return pl.pallas_call(
        body, name="ffn_fwd", grid=(s // tm,),
        in_specs=[_tok(tm, D_MODEL), _tok(tm, D_MODEL), _whole((1, D_MODEL)),
                  _whole((4, D_MODEL, FF_PIECE), single=True), _whole((2, FF_PIECE, D_MODEL), single=True),
                  _whole((1, D_MODEL))],
        out_specs=[_tok(tm, D_MODEL), _tok(tm, D_FF), _tok(tm, D_FF), _tok(tm, D_FF), _tok(tm, D_MODEL),
                   _tok(tm, D_MODEL), _whole((8, 128))],
        out_shape=[sd((s, D_MODEL), MXU_DTYPE), sd((s, D_FF), MXU_DTYPE), sd((s, D_FF), MXU_DTYPE),
                   sd((s, D_FF), MXU_DTYPE), sd((s, D_MODEL), F32), sd((s, D_MODEL), F32), sd((8, 128), F32)],
        compiler_params=_params(("arbitrary",)),
    )(x2, target, g_pre, w_gu, w_dn, g_post)


def _norm_bwd(dy, normed, rinv, gain):
    t = dy * gain
    return rinv * (t - normed * jnp.mean(t * normed, axis=-1, keepdims=True))


def _acc_rows(ref, first, val):
    @pl.when(first)
    def _():
        ref[...] = jnp.zeros_like(ref)

    ref[...] += jnp.sum(val, axis=0, keepdims=True)


def _ffn_bwd(dout, ff, x2, g, up, g_post, g_pre, w_gu, w_dn, tm):
    s = x2.shape[0]

    def body(do_ref, ff_ref, x_ref, g_ref, up_ref, gpost_ref, gpre_ref, wgu_ref, wdn_ref,
             dx_ref, dff_ref, dgu_ref, dgpost_ref, dgpre_ref):
        first = pl.program_id(0) == 0
        ffv = ff_ref[...]
        rf = _rms(ffv)
        n = ffv * rf
        do = do_ref[...]
        _acc_rows(dgpost_ref, first, do * n)
        dff = _norm_bwd(do, n, rf, gpost_ref[...]).astype(MXU_DTYPE)
        dff_ref[...] = dff
        dh = jnp.zeros((tm, D_MODEL), F32)
        for j in range(2):
            cols = slice(j * FF_PIECE, (j + 1) * FF_PIECE)
            da = _dot_nt(dff, wdn_ref[j])
            gv = g_ref[:, cols].astype(F32)
            sg = jax.nn.sigmoid(gv)
            dg = (da * up_ref[:, cols].astype(F32) * (sg * (1.0 + gv * (1.0 - sg)))).astype(MXU_DTYPE)
            du = (da * (gv * sg)).astype(MXU_DTYPE)
            dgu_ref[:, cols] = dg
            dgu_ref[:, D_FF + j * FF_PIECE:D_FF + (j + 1) * FF_PIECE] = du
            dh = dh + _dot_nt(dg, wgu_ref[j]) + _dot_nt(du, wgu_ref[2 + j])
        xv = x_ref[...]
        r2 = _rms(xv)
        nx = xv * r2
        _acc_rows(dgpre_ref, first, dh * nx)
        dx_ref[...] = do + _norm_bwd(dh, nx, r2, gpre_ref[...])

    sd = jax.ShapeDtypeStruct
    return pl.pallas_call(
        body, name="ffn_bwd", grid=(s // tm,),
        in_specs=[_tok(tm, D_MODEL), _tok(tm, D_MODEL), _tok(tm, D_MODEL), _tok(tm, D_FF), _tok(tm, D_FF),
                  _whole((1, D_MODEL)), _whole((1, D_MODEL)),
                  _whole((4, D_MODEL, FF_PIECE), single=True), _whole((2, FF_PIECE, D_MODEL), single=True)],
        out_specs=[_tok(tm, D_MODEL), _tok(tm, D_MODEL), _tok(tm, 2 * D_FF), _whole((1, D_MODEL)),
                   _whole((1, D_MODEL))],
        out_shape=[sd((s, D_MODEL), F32), sd((s, D_MODEL), MXU_DTYPE), sd((s, 2 * D_FF), MXU_DTYPE),
                   sd((1, D_MODEL), F32), sd((1, D_MODEL), F32)],
        compiler_params=_params(("arbitrary",)),
    )(dout, ff, x2, g, up, g_post, g_pre, w_gu, w_dn)


def _tn_matmul(a, b, tm, tn, tk, name, totals=(), piece_rows=None):
    s, m = a.shape
    n = b.shape[1]
    nw = len(totals)
    grid = (m // tm, n // tn, s // tk)
    assert piece_rows is None or tn == n

    def body(a_ref, b_ref, *rest):
        o_ref = rest[nw]
        acc_ref = o_ref if piece_rows is None else rest[-1]
        if nw:
            step = (pl.program_id(0) * grid[1] + pl.program_id(1)) * grid[2] + pl.program_id(2)
            share_start, share_finish = _share_stages(rest[:nw], rest[nw + 1:2 * nw + 1], *rest[2 * nw + 1:2 * nw + 3])
            pl.when(step == 0)(share_start)

        @pl.when(pl.program_id(2) == 0)
        def _():
            acc_ref[...] = jnp.zeros_like(acc_ref)

        acc_ref[...] += lax.dot_general(a_ref[...], b_ref[...], (((0,), (0,)), ((), ())), preferred_element_type=F32)
        if piece_rows is not None:
            for r in range(grid[0]):
                @pl.when((pl.program_id(0) == r) & (pl.program_id(2) == grid[2] - 1))
                def _():
                    for p, at, src, count in _piece_windows(piece_rows, r * tm, (r + 1) * tm):
                        o_ref[p, at:at + count, :] = acc_ref[src - r * tm:src - r * tm + count, :]
        if nw:
            pl.when(step == grid[0] * grid[1] * grid[2] - 1)(share_finish)

    if piece_rows is None:
        o_spec, o_shape = pl.BlockSpec((tm, tn), lambda i, j, kk: (i, j)), (m, n)
    else:
        o_spec, o_shape = pl.BlockSpec((N_CHIPS, piece_rows, n), lambda i, j, kk: (0, 0, 0)), (N_CHIPS, piece_rows, n)
    out = pl.pallas_call(
        body, name=name, grid=grid,
        in_specs=[pl.BlockSpec((tk, tm), lambda i, j, kk: (kk, i)), pl.BlockSpec((tk, tn), lambda i, j, kk: (kk, j))]
        + [ANY] * nw,
        out_specs=[o_spec] + [ANY] * nw,
        out_shape=[jax.ShapeDtypeStruct(o_shape, F32)] + [jax.ShapeDtypeStruct(t.shape, t.dtype) for t in totals],
        scratch_shapes=([pltpu.SemaphoreType.DMA((nw,)), pltpu.SemaphoreType.DMA((nw,))] if nw else [])
        + ([] if piece_rows is None else [pltpu.VMEM((tm, tn), F32)]),
        compiler_params=_params(("arbitrary", "arbitrary", "arbitrary")),
    )(a, b, *totals)
    return (out[0], out[1:]) if nw else out[0]


def _mixer_bwd(dx2, y, o_attn, gate_b, z, g_post, g_attn, g_conv, w_out, gmat, sel, tm, ready, kinds):
    s = dx2.shape[0]
    nw = len(ready)
    nt = s // tm

    def body(d_ref, y_ref, o_ref, b_ref, z_ref, gp_ref, ga_ref, gc_ref, wo_ref, gm_ref, sel_ref, *rest):
        grads = rest[:nw]
        dy_ref, do_ref, db_ref, dz_ref, delta_ref, dgp_ref, dga_ref, dgc_ref = rest[nw:nw + 8]
        taken = rest[nw + 8:2 * nw + 8]
        send, recv = rest[2 * nw + 8:]
        first = pl.program_id(0) == 0
        pair_start, pair_finish = _pair_stages(grads, kinds, taken, send, recv)
        pl.when(first)(pair_start)
        yv = y_ref[...]
        ry = _rms(yv)
        ny = yv * ry
        d = d_ref[...]
        _acc_rows(dgp_ref, first, d * ny)
        dy = _norm_bwd(d, ny, ry, gp_ref[...]).astype(MXU_DTYPE)
        dy_ref[...] = dy
        dm = _dot_nt(dy, wo_ref[...])
        gm = gm_ref[0:128, 0:128]

        @pl.when(first)
        def _():
            dga_ref[...] = jnp.zeros_like(dga_ref)
            dgc_ref[...] = jnp.zeros_like(dgc_ref)

        def group_bwd(val, dmv, gain_ref, dg_ref, cols):
            rg = lax.rsqrt(_group_ms(val * val, gm) + EPS)
            nv = val * rg
            dg_ref[:, cols] += jnp.sum(dmv * nv, axis=0, keepdims=True)
            t = dmv * gain_ref[:, cols]
            return rg * (t - nv * _group_ms(t * nv, gm))

        delta = jnp.zeros((tm, 128), F32)
        for lo in range(0, ATTN_W, 128):
            cols = slice(lo, lo + 128)
            ov = o_ref[:, cols].astype(F32)
            d_o = group_bwd(ov, dm[:, cols], ga_ref, dga_ref, cols)
            do_ref[:, cols] = d_o.astype(MXU_DTYPE)
            delta = delta + _dot_exact(d_o * ov, sel_ref[cols, :], 2)
            zv = z_ref[:, cols].astype(F32)
            bv = b_ref[:, cols].astype(F32)
            d_cv = group_bwd(bv * zv, dm[:, ATTN_W + lo:ATTN_W + lo + 128], gc_ref, dgc_ref, cols)
            db_ref[:, cols] = (d_cv * zv).astype(MXU_DTYPE)
            dz_ref[:, cols] = d_cv * bv
        delta_ref[...] = delta.T[0:HEAD_ROWS, :]
        pl.when(pl.program_id(0) == nt - 1)(pair_finish)

    sd = jax.ShapeDtypeStruct
    taken_shape = [sd((N_CHIPS, g.shape[-2], g.shape[-1] if kd == "rows" else g.shape[-1] // N_CHIPS), F32)
                   for g, kd in zip(ready, kinds)]
    out = pl.pallas_call(
        body, name="mixer_bwd", grid=(nt,),
        in_specs=[_tok(tm, D_MODEL), _tok(tm, D_MODEL), _tok(tm, 512), _tok(tm, 512), _tok(tm, 512),
                  _whole((1, D_MODEL)), _whole((1, 512)), _whole((1, 512)),
                  _whole((D_MODEL, D_MODEL), single=True), _whole((512, 512)), _whole((512, 128))] + [ANY] * nw,
        out_specs=[_tok(tm, D_MODEL), _tok(tm, 512), _tok(tm, 512), _tok(tm, 512), _feat(HEAD_ROWS, tm),
                   _whole((1, D_MODEL)), _whole((1, 512)), _whole((1, 512))] + [ANY] * nw,
        out_shape=[sd((s, D_MODEL), MXU_DTYPE), sd((s, 512), MXU_DTYPE), sd((s, 512), MXU_DTYPE), sd((s, 512), F32),
                   sd((HEAD_ROWS, s), F32), sd((1, D_MODEL), F32), sd((1, 512), F32), sd((1, 512), F32)] + taken_shape,
        scratch_shapes=[pltpu.SemaphoreType.DMA((nw, N_CHIPS)), pltpu.SemaphoreType.DMA((nw, N_CHIPS))],
        compiler_params=_params(("arbitrary",)),
    )(dx2, y, o_attn, gate_b, z, g_post, g_attn, g_conv, w_out, gmat, sel, *ready)
    return out[:8], out[8:]


def _attn_bwd(qs, k_t, v, do, probs, probs_max, lse, delta, t, parts):
    s = qs.shape[0]
    n = s // t
    pairs = [(i, j) for j in range(n) for i in range(j, n)]
    it = jnp.asarray(np.array([p[0] for p in pairs], np.int32))
    jt = jnp.asarray(np.array([p[1] for p in pairs], np.int32))
    ft = jnp.asarray(np.array([p[0] * (p[0] + 1) // 2 + p[1] for p in pairs], np.int32))

    nw = len(parts)

    def body(it_ref, jt_ref, ft_ref, q_ref, kt_ref, v_ref, do_ref, p_ref, pm_ref, lse_ref, dl_ref, *rest):
        pb = rest[:nw]
        dq_ref, dk_ref, dv_ref, dc_ref, dcq_ref = rest[nw:nw + 5]
        rcv = rest[nw + 5:2 * nw + 5]
        dk_sc, dv_sc, dc_sc, send, recv = rest[2 * nw + 5:]
        p = pl.program_id(0)
        i = it_ref[p]
        j = jt_ref[p]
        chip_start, chip_finish = _chip_stages(pb, rcv, send, recv)

        @pl.when(p == 0)
        def _():
            chip_start()
            dq_ref[...] = jnp.zeros_like(dq_ref)
            dcq_ref[...] = jnp.zeros_like(dcq_ref)

        @pl.when(i == j)
        def _():
            dk_sc[...] = jnp.zeros_like(dk_sc)
            dv_sc[...] = jnp.zeros_like(dv_sc)
            dc_sc[...] = jnp.zeros_like(dc_sc)

        def pair_step(pp):
            lanes = _pair_lanes(pp)
            qp = q_ref[:, lanes]
            vp = v_ref[:, lanes]
            dop = do_ref[:, lanes]
            lane = lax.broadcasted_iota(jnp.int32, (t, 128), 1)
            for hb in range(2):
                h = 2 * pp + hb
                row = pl.ds(h, 1)
                pt = p_ref[0, h].astype(F32) * jnp.exp(pm_ref[0, row, :] - lse_ref[row, :])
                dv_sc[:, lanes] += _dot(pt.astype(MXU_DTYPE), _only_head(dop, hb))
                dst = pt * (_dot_nt(_only_head(vp, hb), dop) - dl_ref[row, :])
                dc_sc[...] -= jnp.where(lane == h, jnp.sum(dst, axis=1, keepdims=True), 0.0)
                dcq_ref[i, row, :] += jnp.sum(dst, axis=0, keepdims=True)
                dsb = dst.astype(MXU_DTYPE)
                dk_sc[:, lanes] += _dot(dsb, _only_head(qp, hb))
                rows = _head_rows(h)
                dq_ref[i, rows, :] += _dot(kt_ref[rows, :], dsb)

        for pp in range(N_HEADS // 2):
            pair_step(pp)

        @pl.when(i == n - 1)
        def _():
            dk_ref[...] = dk_sc[...].astype(MXU_DTYPE)
            dv_ref[...] = dv_sc[...].astype(MXU_DTYPE)
            dc_ref[...] = dc_sc[...]

        pl.when(p == len(pairs) - 1)(chip_finish)

    qi = lambda p, it_, jt_, ft_: (it_[p], 0)
    kj = lambda p, it_, jt_, ft_: (jt_[p], 0)
    row_i = lambda p, it_, jt_, ft_: (0, it_[p])
    gs = pltpu.PrefetchScalarGridSpec(
        num_scalar_prefetch=3, grid=(len(pairs),),
        in_specs=[pl.BlockSpec((t, ATTN_W), qi),
                  pl.BlockSpec((ATTN_W, t), lambda p, it_, jt_, ft_: (0, jt_[p])),
                  pl.BlockSpec((t, ATTN_W), kj), pl.BlockSpec((t, ATTN_W), qi),
                  pl.BlockSpec((1, N_HEADS, t, t), lambda p, it_, jt_, ft_: (ft_[p], 0, 0, 0)),
                  pl.BlockSpec((1, HEAD_ROWS, t), lambda p, it_, jt_, ft_: (ft_[p], 0, 0)),
                  pl.BlockSpec((HEAD_ROWS, t), row_i), pl.BlockSpec((HEAD_ROWS, t), row_i)] + [ANY] * nw,
        out_specs=[pl.BlockSpec((n, ATTN_W, t), lambda p, it_, jt_, ft_: (0, 0, 0)),
                   pl.BlockSpec((t, ATTN_W), kj), pl.BlockSpec((t, ATTN_W), kj),
                   pl.BlockSpec((t, 128), kj),
                   pl.BlockSpec((n, HEAD_ROWS, t), lambda p, it_, jt_, ft_: (0, 0, 0))] + [ANY] * nw,
        scratch_shapes=[pltpu.VMEM((t, ATTN_W), F32), pltpu.VMEM((t, ATTN_W), F32),
                        pltpu.VMEM((t, 128), F32), pltpu.SemaphoreType.DMA((nw, 3)), pltpu.SemaphoreType.DMA((nw, 3))])
    sd = jax.ShapeDtypeStruct
    out = pl.pallas_call(
        body, name="attn_bwd", grid_spec=gs,
        out_shape=[sd((n, ATTN_W, t), F32), sd((s, ATTN_W), MXU_DTYPE), sd((s, ATTN_W), MXU_DTYPE),
                   sd((s, 128), F32), sd((n, HEAD_ROWS, t), F32)] + [sd((3,) + a.shape[1:], a.dtype) for a in parts],
        compiler_params=_params(("arbitrary",)),
    )(it, jt, ft, qs, k_t, v, do, probs, probs_max, lse, delta, *parts)
    return out[:5], out[5:]


def _forget_bwd(dc_rows, dc_cols, z_t, b_col):
    s = z_t.shape[1]
    nb = s // 128

    def body(dr_ref, dcc_ref, z_ref, b_ref, dz_ref, db_ref):
        lower = _tri(128, False)
        real = lax.broadcasted_iota(jnp.int32, (HEAD_ROWS, 128), 0) < N_HEADS

        tail = jnp.zeros((HEAD_ROWS, 1), F32)
        dbias = jnp.zeros((HEAD_ROWS, 1), F32)
        for m in range(nb):
            off = (nb - 1 - m) * 128
            dc = dr_ref[:, off:off + 128] + dcc_ref[off:off + 128, :].T[0:HEAD_ROWS, :]
            dlf = _dot_exact(dc, lower, 3) + tail
            dz = dlf * jax.nn.sigmoid(-(z_ref[0:HEAD_ROWS, off:off + 128] + b_ref[...]))
            dz = jnp.where(real, dz, 0.0)
            dz_ref[off:off + 128, :] = _rows_to_cols(dz)
            tail = tail + jnp.sum(dc, axis=1, keepdims=True)
            dbias = dbias + jnp.sum(dz, axis=1, keepdims=True)
        db_ref[...] = jnp.broadcast_to(dbias, db_ref.shape)

    return pl.pallas_call(
        body, name="forget_bwd",
        out_shape=[jax.ShapeDtypeStruct((s, 128), F32), jax.ShapeDtypeStruct((HEAD_ROWS, 128), F32)],
        compiler_params=_params())(dc_rows, dc_cols, z_t, b_col)


def _inproj_bwd(dz, gate_c, u, conv_w, dq, dk, dv, dzf, db, x, dx2, g_pre, w_t, tm):
    s = x.shape[0]
    nt = s // tm
    t = dq.shape[2]
    assert t % tm == 0 and dq.shape[:2] == (s // t, ATTN_W)
    per = t // tm

    def body(dz_ref, dzn_ref, c_ref, u_ref, cp_ref, up_ref, cw_ref, dq_ref, dk_ref, dv_ref, dzf_ref, db_ref,
             x_ref, dx2_ref, g_ref, w_ref, gx_ref, dp_ref, dg_ref, dcw_ref):
        i = pl.program_id(0)
        first = i == 0
        last = i == nt - 1
        @pl.when(first)
        def _():
            dcw_ref[...] = jnp.zeros_like(dcw_ref)

        for lo in range(0, CONV_W, 128):
            cols = slice(lo, lo + 128)
            dzv = dz_ref[:, cols]
            row = lax.broadcasted_iota(jnp.int32, dzv.shape, 0)
            n0 = jnp.where(last, 0.0, dzn_ref[0:1, cols])
            n1 = jnp.where(last, 0.0, dzn_ref[1:2, cols])
            dz1 = jnp.where(row == tm - 1, n0, pltpu.roll(dzv, tm - 1, 0))
            dz2 = jnp.where(row == tm - 1, n1, jnp.where(row == tm - 2, n0, pltpu.roll(dzv, tm - 2, 0)))
            dcu = cw_ref[2:3, cols] * dzv + cw_ref[1:2, cols] * dz1 + cw_ref[0:1, cols] * dz2
            cv = c_ref[:, cols].astype(F32)
            uv = u_ref[:, cols].astype(F32)
            cu = cv * uv
            cu1, cu2 = _shift_down(cu, (cp_ref, up_ref), first, cols)
            dcw_ref[0:1, cols] += jnp.sum(dzv * cu2, axis=0, keepdims=True)
            dcw_ref[1:2, cols] += jnp.sum(dzv * cu1, axis=0, keepdims=True)
            dcw_ref[2:3, cols] += jnp.sum(dzv * cu, axis=0, keepdims=True)
            dp_ref[:, OFF_C + lo:OFF_C + lo + 128] = (dcu * uv).astype(MXU_DTYPE)
            dp_ref[:, OFF_U + lo:OFF_U + lo + 128] = (dcu * cv).astype(MXU_DTYPE)

        dp_ref[:, 0:512] = (dq_ref[0].T * Q_SCALE).astype(MXU_DTYPE)
        dp_ref[:, 512:1024] = dk_ref[...].astype(MXU_DTYPE)
        dp_ref[:, 1024:OFF_F] = dv_ref[...].astype(MXU_DTYPE)
        dp_ref[:, OFF_F:OFF_B] = dzf_ref[...].astype(MXU_DTYPE)
        dp_ref[:, OFF_B:OFF_C] = db_ref[...].astype(MXU_DTYPE)
        dh = _dot(dp_ref[...], w_ref[...])
        xv = x_ref[...]
        r1 = _rms(xv)
        nx = xv * r1
        _acc_rows(dg_ref, first, dh * nx)
        gx_ref[...] = dx2_ref[...] + _norm_bwd(dh, nx, r1, g_ref[...])

    prev = _halo_before(tm)
    nxt = pl.BlockSpec((8, 512), lambda i: (jnp.minimum((i + 1) * (tm // 8), s // 8 - 1), 0))
    sd = jax.ShapeDtypeStruct
    return pl.pallas_call(
        body, name="inproj_bwd", grid=(nt,),
        in_specs=[_tok(tm, 512), nxt, _tok(tm, 512), _tok(tm, 512), prev, prev, _whole((3, 512)),
                  pl.BlockSpec((1, ATTN_W, tm), lambda i: (i // per, 0, i % per)), _tok(tm, 512), _tok(tm, 512),
                  _tok(tm, 128),
                  _tok(tm, 512),
                  _tok(tm, D_MODEL), _tok(tm, D_MODEL), _whole((1, D_MODEL)), _whole((IN_PAD, D_MODEL), single=True)],
        out_specs=[_tok(tm, D_MODEL), _tok(tm, IN_PAD), _whole((1, D_MODEL)), _whole((8, 512))],
        out_shape=[sd((s, D_MODEL), F32), sd((s, IN_PAD), MXU_DTYPE), sd((1, D_MODEL), F32), sd((8, 512), F32)],
        compiler_params=_params(("arbitrary",)),
    )(dz, dz, gate_c, u, gate_c, u, conv_w, dq, dk, dv, dzf, db, x, dx2, g_pre, w_t)


def _tile(s, want):
    return want if s % want == 0 else s


def _halves(a):
    return a.reshape(2, a.shape[0] // 2, a.shape[1])


def _device_step(x, target, w, mom1, mom2, w_in_t, m_in_t, v_in_t, c_idx, me_idx):
    s = x.shape[0]
    tm = _tile(s, 512)
    tf = _tile(s, 256)
    ta = _tile(s, 512)
    tkk = _tile(s, 1024)
    gidx = np.arange(512) // HEAD_DIM
    gmat = jnp.asarray(gidx[:, None] == gidx[None, :], MXU_DTYPE)
    sel = jnp.asarray(gidx[:, None] == np.arange(128)[None, :], MXU_DTYPE)
    g_mix_pre, g_mix_post, g_ffn_pre, g_ffn_post = w["g_mix_pre"], w["g_mix_post"], w["g_ffn_pre"], w["g_ffn_post"]
    g_attn, g_conv, b_forget = w["g_attn_out"], w["g_conv_out"], w["b_forget"]
    shard = {n: _halves(w[n][0].astype(MXU_DTYPE)) for n in BIG[1:]}
    piece_rows = IN_W // N_CHIPS

    g_in, conv_all = _gather_weights([w_in_t.reshape(piece_rows, D_MODEL).astype(MXU_DTYPE)], w["conv_w"][0])
    conv_w = jnp.transpose(conv_all, (1, 0, 2)).reshape(3, CONV_W)

    h1, qs, k, v, k_t, v_t, z_t, gate_b, gate_c, u, w_t = _inproj_fwd(x, g_mix_pre, g_in, tm)
    b_col = jnp.pad(jnp.transpose(b_forget), ((0, HEAD_ROWS - N_HEADS), (0, 0)))
    q_bias, k_bias = _forget_fwd(z_t, b_col)
    o_attn, lse, probs, probs_max, (g_out, g_gu, g_dn) = _attn_fwd(
        qs, k, v_t, q_bias, k_bias, ta, [shard["w_out"], shard["w_gate_up"], shard["w_down"]])
    w_out = g_out.reshape(D_MODEL, D_MODEL)
    w_gu = g_gu.reshape(N_CHIPS, D_MODEL, FF_PIECE)
    w_dn = g_dn.reshape(2, FF_PIECE, D_MODEL)
    x2, merged, y, z = _mixer_fwd(x, o_attn, gate_b, gate_c, u, conv_w, g_attn, g_conv, w_out, g_mix_post, gmat, tm)
    h2, g, up, a, ff, dout, loss_acc = _ffn_fwd(x2, target, g_ffn_pre, w_gu, w_dn, g_ffn_post, tf)

    dx2, dff, dgu, dg_ffn_post, dg_ffn_pre = _ffn_bwd(dout, ff, x2, g, up, g_ffn_post, g_ffn_pre, w_gu, w_dn, tf)
    dw_dn = _tn_matmul(a, dff, FF_PIECE, 1024, tkk, "dw_down").reshape(N_CHIPS, 2, D_FF // (2 * N_CHIPS), D_MODEL)
    dw_gu = _tn_matmul(h2, dgu, 1024, FF_PIECE, tkk, "dw_gate_up").reshape(2, D_MODEL // 2, 2 * D_FF)
    (dy, d_o, d_b, dz, delta, dg_mix_post, dg_attn, dg_conv), (a_gu, a_dn) = _mixer_bwd(
        dx2, y, o_attn, gate_b, z, g_mix_post, g_attn, g_conv, w_out, gmat, sel, tm, [dw_gu, dw_dn], ["cols", "rows"])
    dw_out = _tn_matmul(merged, dy, 1024, 1024, tkk, "dw_out").reshape(N_CHIPS, 2, D_MODEL // (2 * N_CHIPS), D_MODEL)
    place = jnp.concatenate([c_idx, me_idx])
    *sum_gu, a_out = _pair_sum(place, dw_gu, "cols", a_gu, "pair_sum_w_gate_up", [dw_out], ["rows"])
    sum_dn = _pair_sum(place, dw_dn, "rows", a_dn, "pair_sum_w_down")
    sum_out = _pair_sum(place, dw_out, "rows", a_out, "pair_sum_w_out")
    (dq_t, dk, dv, dc_cols, dcq), (r_gu, r_dn, r_out) = _attn_bwd(
        qs, k_t, v, d_o, probs, probs_max, lse, delta, ta, [sum_gu[1], sum_dn[1], sum_out[1]])
    dc_rows = jnp.transpose(dcq, (1, 0, 2)).reshape(HEAD_ROWS, s)
    dzf, db_f = _forget_bwd(dc_rows, dc_cols, z_t, b_col)
    grad_x, dproj, dg_mix_pre, dcw = _inproj_bwd(dz, gate_c, u, conv_w, dq_t, dk, dv, dzf, d_b,
                                                 x, dx2, g_mix_pre, w_t, tm)
    done = [_chip_sum(sb[0], r, "chip_sum_" + n)
            for n, sb, r in zip(BIG[1:], (sum_out, sum_gu, sum_dn), (r_out, r_gu, r_dn))]
    dw_in, done_theirs = _tn_matmul(dproj, h1, IN_PAD, 1024, _tile(s, 512), "dw_in", done, piece_rows)

    (a_in,) = _pair_exchange([dw_in], ["lanes"])
    sum_in = _pair_sum(place, dw_in, "lanes", a_in, "pair_sum_w_in")
    small = dict(b_forget=db_f[:N_HEADS, 0], g_attn_out=dg_attn, g_conv_out=dg_conv, g_mix_pre=dg_mix_pre,
                 g_mix_post=dg_mix_post, g_ffn_pre=dg_ffn_pre, g_ffn_post=dg_ffn_post)
    (r_in,), small_all = _chip_exchange([sum_in[1]], _pack_small(small, dcw[:3], loss_acc[0, 0]))
    t_in = _chip_sum(sum_in[0], r_in, "chip_sum_w_in")
    (s_in,) = _pair_share([t_in], "pair_share_w_in")
    new = {"w_in": _adamw_lanes(c_idx, w_in_t, t_in, s_in, m_in_t, v_in_t, "adamw_w_in")}
    for n, mine, theirs in zip(BIG[1:], done, done_theirs):
        new[n] = _adamw(c_idx, w[n][0], mine, theirs, mom1[n][0], mom2[n][0], 2, "adamw_" + n)
    return grad_x, new, small_all


BIG = ("w_in", "w_out", "w_gate_up", "w_down")
ANY = pl.BlockSpec(memory_space=pl.ANY)


def _place():
    x, y, c = lax.axis_index("x"), lax.axis_index("y"), lax.axis_index("c")
    others = [(1 - x, y), (x, 1 - y), (1 - x, 1 - y)]
    return x, y, c, 2 * x + y, others, [2 * px + py for px, py in others]


def _remote(src, dst, send, recv, dev):
    return pltpu.make_async_remote_copy(src_ref=src, dst_ref=dst, send_sem=send, recv_sem=recv,
                                        device_id=dev, device_id_type=MESH_ID)


def _gather_stages(sh, outs, send, recv):
    x, y, c, me, others, chips = _place()
    sib = (x, y, 1 - c)
    every = [(w, kk) for w in range(len(sh)) for kk in range(3)]

    def half_of(ref, half, piece=None):
        ref = ref if piece is None else ref.at[piece]
        if len(ref.shape) == 3:
            return ref.at[half]
        hc = ref.shape[1] // 2
        return ref.at[:, pl.ds(pl.multiple_of(half * hc, 128), hc)]

    def first(w, kk):
        return _remote(half_of(sh[w], c), half_of(outs[w], c, me), send.at[w, kk], recv.at[w, kk], (*others[kk], c))

    def landed(w, kk):
        r = half_of(outs[w], c, chips[kk])
        return _remote(r, r, send.at[w, kk], recv.at[w, kk], (*others[kk], c))

    def onward(w, kk, half):
        r = half_of(outs[w], half, chips[kk])
        return _remote(r, r, send.at[w, 3 + kk], recv.at[w, 3 + kk], sib)

    def start():
        for w, kk in every:
            first(w, kk).start()

    def forward():
        for w, kk in every:
            landed(w, kk).wait_recv()
            onward(w, kk, c).start()

    def finish():
        for w, kk in every:
            onward(w, kk, 1 - c).wait_recv()
        for w, kk in every:
            first(w, kk).wait_send()
            onward(w, kk, c).wait_send()

    return start, forward, finish


def _pair_piece(ref, kind, p, half):
    if kind == "rows":
        return ref.at[p, half]
    if kind == "lanes":
        hc = ref.shape[2] // 2
        return ref.at[p, :, pl.ds(pl.multiple_of(half * hc, 128), hc)]
    cols = ref.shape[2] // N_CHIPS
    return ref.at[half, :, pl.ds(p * cols, cols)]


def _pair_stages(g, kinds, a, send, recv):
    x, y, c, _, _, _ = _place()
    copies = [_remote(_pair_piece(g[w], kinds[w], p, 1 - c), a[w].at[p], send.at[w, p], recv.at[w, p], (x, y, 1 - c))
              for w in range(len(g)) for p in range(N_CHIPS)]

    def start():
        for cp in copies:
            cp.start()

    def finish():
        for cp in copies:
            cp.wait()

    return start, finish


def _chip_stages(pb, rcv, send, recv):
    x, y, c, _, others, chips = _place()
    copies = [_remote(pb[w].at[chips[kk]], rcv[w].at[kk], send.at[w, kk], recv.at[w, kk], (*others[kk], c))
              for w in range(len(pb)) for kk in range(3)]

    def start():
        for cp in copies:
            cp.start()

    def finish():
        for cp in copies:
            cp.wait()

    return start, finish


def _gather_weights(shards, conv_w):
    n = len(shards)

    def body(*refs):
        sh, cw, outs, cwo = refs[:n], refs[n], refs[n + 1:2 * n + 1], refs[2 * n + 1]
        send, recv = refs[2 * n + 2:]
        x, y, c, me, others, chips = _place()
        start, forward, finish = _gather_stages(sh, outs, send, recv)
        start()
        small = [_remote(cw, cwo.at[me], send.at[n, kk], recv.at[n, kk], (*others[kk], c)) for kk in range(3)]
        for cp in small:
            cp.start()
        forward()
        for kk in range(3):
            _remote(cw, cwo.at[chips[kk]], send.at[n, kk], recv.at[n, kk], (*others[kk], c)).wait_recv()
        finish()
        for cp in small:
            cp.wait_send()

    out_shape = [jax.ShapeDtypeStruct((N_CHIPS,) + s.shape, s.dtype) for s in shards]
    out_shape.append(jax.ShapeDtypeStruct((N_CHIPS,) + conv_w.shape, conv_w.dtype))
    got = pl.pallas_call(
        body, name="gather_weights", in_specs=[ANY] * (n + 1), out_specs=[ANY] * (n + 1), out_shape=out_shape,
        scratch_shapes=[pltpu.SemaphoreType.DMA((n + 1, 6)), pltpu.SemaphoreType.DMA((n + 1, 6))],
    )(*shards, conv_w)
    me = 2 * lax.axis_index("x") + lax.axis_index("y")
    return [lax.dynamic_update_index_in_dim(g, own, me, 0) for g, own in zip(got, list(shards) + [conv_w])]


def _taken_shape(g, kind):
    if kind == "rows":
        return (N_CHIPS,) + g.shape[2:]
    if kind == "lanes":
        return g.shape[:2] + (g.shape[2] // 2,)
    return (N_CHIPS, g.shape[1], g.shape[2] // N_CHIPS)


def _pair_sum(place, g, kind, a, name, ready=(), ready_kinds=()):
    _, half, cols = a.shape
    nw = len(ready)
    if kind == "rows":
        mine = pl.BlockSpec((1, 1, half, cols), lambda p, pr: (p, pr[0], 0, 0))
    elif kind == "lanes":
        mine = pl.BlockSpec((1, half, cols), lambda p, pr: (p, 0, pr[0]))
    else:
        mine = pl.BlockSpec((1, half, cols), lambda p, pr: (pr[0], 0, p))

    def body(place_ref, g_ref, a_ref, *rest):
        grads, (own_ref, pb_ref), taken = rest[:nw], rest[nw:nw + 2], rest[nw + 2:2 * nw + 2]
        if nw:
            pair_start, pair_finish = _pair_stages(grads, ready_kinds, taken, *rest[2 * nw + 2:])
            pl.when(pl.program_id(0) == 0)(pair_start)
        tot = (g_ref[0, 0] if kind == "rows" else g_ref[0]) + a_ref[0]
        pb_ref[0] = tot.astype(BF16)

        @pl.when(pl.program_id(0) == place_ref[1])
        def _():
            own_ref[...] = tot

        if nw:
            pl.when(pl.program_id(0) == N_CHIPS - 1)(pair_finish)

    sems = [pltpu.SemaphoreType.DMA((nw, N_CHIPS)), pltpu.SemaphoreType.DMA((nw, N_CHIPS))] if nw else []
    gs = pltpu.PrefetchScalarGridSpec(
        num_scalar_prefetch=1, grid=(N_CHIPS,),
        in_specs=[mine, pl.BlockSpec((1, half, cols), lambda p, pr: (p, 0, 0))] + [ANY] * nw,
        out_specs=[pl.BlockSpec((half, cols), lambda p, pr: (0, 0)),
                   pl.BlockSpec((1, half, cols), lambda p, pr: (p, 0, 0))] + [ANY] * nw,
        scratch_shapes=sems)
    out = pl.pallas_call(
        body, name=name, grid_spec=gs,
        out_shape=[jax.ShapeDtypeStruct((half, cols), F32), jax.ShapeDtypeStruct((N_CHIPS, half, cols), BF16)]
        + [jax.ShapeDtypeStruct(_taken_shape(r, kd), r.dtype) for r, kd in zip(ready, ready_kinds)],
        compiler_params=_params(("arbitrary",)),
    )(place, g, a, *ready)
    return list(out)


def _chip_sum(own, rcv, name):
    half, cols = own.shape

    def body(o_ref, r_ref, t_ref):
        t_ref[...] = ((o_ref[...] + r_ref[0].astype(F32)) + r_ref[1].astype(F32)) + r_ref[2].astype(F32)

    return pl.pallas_call(
        body, name=name, grid=(1,),
        in_specs=[pl.BlockSpec((half, cols), lambda i: (0, 0)), pl.BlockSpec((3, half, cols), lambda i: (0, 0, 0))],
        out_specs=pl.BlockSpec((half, cols), lambda i: (0, 0)),
        out_shape=jax.ShapeDtypeStruct((half, cols), F32), compiler_params=_params(("arbitrary",)),
    )(own, rcv)


def _small_stages(sm, smg, send, recv):
    x, y, c, _, _, _ = _place()

    def peer(r):
        return (1 - x if r & 4 else x, 1 - y if r & 2 else y, 1 - c if r & 1 else c)

    mine = 4 * x + 2 * y + c
    copies = [_remote(sm, smg.at[mine], send.at[r - 1], recv.at[r - 1], peer(r)) for r in range(1, 8)]

    def start():
        for cp in copies:
            cp.start()

    def finish():
        for r in range(1, 8):
            px, py, pc = peer(r)
            _remote(sm, smg.at[4 * px + 2 * py + pc], send.at[r - 1], recv.at[r - 1], (px, py, pc)).wait_recv()
        for cp in copies:
            cp.wait_send()

    return start, finish


def _pair_exchange(grads, kinds):
    n = len(grads)

    def body(*refs):
        start, finish = _pair_stages(refs[:n], kinds, refs[n:2 * n], *refs[2 * n:])
        start()
        finish()

    return pl.pallas_call(
        body, name="pair_exchange", in_specs=[ANY] * n, out_specs=[ANY] * n,
        out_shape=[jax.ShapeDtypeStruct(_taken_shape(g, kd), g.dtype) for g, kd in zip(grads, kinds)],
        scratch_shapes=[pltpu.SemaphoreType.DMA((n, N_CHIPS)), pltpu.SemaphoreType.DMA((n, N_CHIPS))],
    )(*grads)


def _chip_exchange(parts, small):
    n = len(parts)

    def body(*refs):
        pb, sm, rcv, smg = refs[:n], refs[n], refs[n + 1:2 * n + 1], refs[2 * n + 1]
        send, recv, ssend, srecv = refs[2 * n + 2:]
        chip_start, chip_finish = _chip_stages(pb, rcv, send, recv)
        small_start, small_finish = _small_stages(sm, smg, ssend, srecv)
        chip_start()
        small_start()
        chip_finish()
        small_finish()

    out_shape = [jax.ShapeDtypeStruct((3,) + p.shape[1:], p.dtype) for p in parts]
    out_shape.append(jax.ShapeDtypeStruct((8,) + small.shape, small.dtype))
    *arrived, small_land = pl.pallas_call(
        body, name="chip_exchange", in_specs=[ANY] * (n + 1), out_specs=[ANY] * (n + 1), out_shape=out_shape,
        scratch_shapes=[pltpu.SemaphoreType.DMA((n, 3)), pltpu.SemaphoreType.DMA((n, 3)),
                        pltpu.SemaphoreType.DMA((7,)), pltpu.SemaphoreType.DMA((7,))],
    )(*parts, small)
    mine = 4 * lax.axis_index("x") + 2 * lax.axis_index("y") + lax.axis_index("c")
    return arrived, lax.dynamic_update_index_in_dim(small_land, small, mine, 0)


def _share_stages(t, g, send, recv):
    x, y, c, _, _, _ = _place()
    copies = [_remote(t[w], g[w], send.at[w], recv.at[w], (x, y, 1 - c)) for w in range(len(t))]

    def start():
        for cp in copies:
            cp.start()

    def finish():
        for cp in copies:
            cp.wait()

    return start, finish


def _pair_share(totals, name):
    n = len(totals)

    def body(*refs):
        start, finish = _share_stages(refs[:n], refs[n:2 * n], *refs[2 * n:])
        start()
        finish()

    return pl.pallas_call(
        body, name=name, in_specs=[ANY] * n, out_specs=[ANY] * n,
        out_shape=[jax.ShapeDtypeStruct(t.shape, t.dtype) for t in totals],
        scratch_shapes=[pltpu.SemaphoreType.DMA((n,)), pltpu.SemaphoreType.DMA((n,))],
    )(*totals)


def _adamw_math(w, g, m, v):
    m = ADAM_B1 * m + (1.0 - ADAM_B1) * g
    v = ADAM_B2 * v + (1.0 - ADAM_B2) * (g * g)
    m_hat = m / (1.0 - ADAM_B1 ** ADAM_STEP)
    v_hat = v / (1.0 - ADAM_B2 ** ADAM_STEP)
    delta = -ADAM_LR * (m_hat / (jnp.sqrt(v_hat) + ADAM_EPS) + ADAM_WD * w)
    return delta, m, v


def _adamw(c_idx, w, mine, theirs, m, v, nb, name):
    rows, cols = w.shape
    tr = rows // (2 * nb)

    def body(c_ref, w_ref, a_ref, b_ref, m_ref, v_ref, g_ref, d_ref, nm_ref, nv_ref):
        g = jnp.where(pl.program_id(0) == c_ref[0], a_ref[...], b_ref[...])
        g_ref[...] = g
        d_ref[...], nm_ref[...], nv_ref[...] = _adamw_math(w_ref[...], g, m_ref[...], v_ref[...])

    full = pl.BlockSpec((tr, cols), lambda hh, i, cr: (hh * nb + i, 0))
    half = pl.BlockSpec((tr, cols), lambda hh, i, cr: (i, 0))
    gs = pltpu.PrefetchScalarGridSpec(num_scalar_prefetch=1, grid=(2, nb), in_specs=[full, half, half, full, full],
                                      out_specs=[full] * 4)
    return pl.pallas_call(
        body, name=name, grid_spec=gs, out_shape=[jax.ShapeDtypeStruct((rows, cols), F32)] * 4,
        compiler_params=_params(("arbitrary", "arbitrary")),
    )(c_idx, w, mine, theirs, m, v)


def _adamw_lanes(c_idx, w, mine, theirs, m, v, name):
    rows, _, cols = w.shape
    hc = cols // 2

    def body(c_ref, w_ref, a_ref, b_ref, m_ref, v_ref, g_ref, d_ref, nm_ref, nv_ref):
        g = jnp.where(pl.program_id(0) == c_ref[0], a_ref[...], b_ref[...])
        g_ref[:, 0, :] = g
        d_ref[:, 0, :], nm_ref[:, 0, :], nv_ref[:, 0, :] = _adamw_math(w_ref[:, 0, :], g, m_ref[:, 0, :], v_ref[:, 0, :])

    full = pl.BlockSpec((rows, 1, hc), lambda hh, cr: (0, 0, hh))
    half = pl.BlockSpec((rows, hc), lambda hh, cr: (0, 0))
    gs = pltpu.PrefetchScalarGridSpec(num_scalar_prefetch=1, grid=(2,), in_specs=[full, half, half, full, full],
                                      out_specs=[full] * 4)
    return pl.pallas_call(
        body, name=name, grid_spec=gs, out_shape=[jax.ShapeDtypeStruct((rows, 1, cols), F32)] * 4,
        compiler_params=_params(("arbitrary",)),
    )(c_idx, w, mine, theirs, m, v)


SMALL = ("g_mix_pre", "g_mix_post", "g_ffn_pre", "g_ffn_post")
SMALL_ALL = SMALL + ("g_attn_out", "g_conv_out", "conv_w", "b_forget")
SMALL_AT = {"g_mix_pre": (0, 0, 1024), "g_mix_post": (1, 0, 1024), "g_ffn_pre": (2, 0, 1024),
            "g_ffn_post": (3, 0, 1024), "g_attn_out": (4, 0, 512), "g_conv_out": (4, 512, 512),
            "b_forget": (7, 0, N_HEADS)}
CONV_AT = ((5, 0), (5, 512), (6, 0))
LOSS_AT = (6, 512)


def _pack_small(t, conv_full, loss_sum):
    conv = jnp.concatenate([conv_full.reshape(1, 3 * CONV_W), loss_sum.reshape(1, 1),
                            jnp.zeros((1, 2048 - 3 * CONV_W - 1), F32)], axis=1).reshape(2, 1024)
    return jnp.concatenate([t[n].reshape(1, 1024) for n in SMALL]
                           + [jnp.concatenate([t["g_attn_out"].reshape(1, 512), t["g_conv_out"].reshape(1, 512)], axis=1),
                              conv, jnp.pad(t["b_forget"].reshape(1, N_HEADS), ((0, 0), (0, 1024 - N_HEADS)))], axis=0)


def _small_update(me_idx, gathered, w, m, v):
    def body(me_ref, gg_ref, *refs):
        k = len(SMALL_ALL)
        w_refs, m_refs, v_refs = refs[:k], refs[k:2 * k], refs[2 * k:3 * k]
        loss_ref = refs[3 * k]
        outs = refs[3 * k + 1:3 * k + 1 + 4 * k]
        sums = refs[-1]
        g = gg_ref[0]
        for dev in range(1, 8):
            g = g + gg_ref[dev]
        sums[...] = g
        loss_ref[...] = sums[LOSS_AT[0]:LOSS_AT[0] + 1, LOSS_AT[1]:LOSS_AT[1] + 1]
        mine = pl.multiple_of(me_ref[0] * 128, 128)
        for idx, name in enumerate(SMALL_ALL):
            g_ref, d_ref, nm_ref, nv_ref = outs[4 * idx:4 * idx + 4]
            if name == "conv_w":
                for r, (row, lo) in enumerate(CONV_AT):
                    gr = sums[row:row + 1, pl.ds(lo + mine, 128)]
                    g_ref[r] = gr
                    d_ref[r], nm_ref[r], nv_ref[r] = _adamw_math(w_refs[idx][r], gr, m_refs[idx][r], v_refs[idx][r])
            else:
                row, lo, n = SMALL_AT[name]
                gr = sums[row:row + 1, lo:lo + n]
                g_ref[...] = gr
                d_ref[...], nm_ref[...], nv_ref[...] = _adamw_math(w_refs[idx][...], gr, m_refs[idx][...],
                                                                    v_refs[idx][...])

    def whole(a):
        nd = a.ndim
        return pl.BlockSpec(a.shape, lambda i, mr: (0,) * nd)

    rows_first = lambda n, a: jnp.transpose(a, (1, 0, 2)) if n == "conv_w" else a
    ins = [rows_first(n, t[n]) for t in (w, m, v) for n in SMALL_ALL]
    out_shape = [jax.ShapeDtypeStruct((1, 1), F32)]
    for n in SMALL_ALL:
        out_shape += [jax.ShapeDtypeStruct(rows_first(n, w[n]).shape, F32)] * 4
    gs = pltpu.PrefetchScalarGridSpec(
        num_scalar_prefetch=1, grid=(1,), in_specs=[whole(gathered)] + [whole(a) for a in ins],
        out_specs=[whole(o) for o in out_shape], scratch_shapes=[pltpu.VMEM((8, 1024), F32)])
    out = pl.pallas_call(body, name="small_update", grid_spec=gs, out_shape=out_shape,
                         compiler_params=_params(("arbitrary",)))(me_idx, gathered, *ins)
    return out[0], {n: [rows_first(n, r) for r in out[1 + 4 * i:5 + 4 * i]] for i, n in enumerate(SMALL_ALL)}


def kernel(x, w_in, b_forget, conv_w, g_attn_out, g_conv_out, w_out, g_mix_pre, g_mix_post, w_gate_up, w_down, g_ffn_pre, g_ffn_post, loss_target, m_w_in, m_b_forget, m_conv_w, m_g_attn_out, m_g_conv_out, m_w_out, m_g_mix_pre, m_g_mix_post, m_w_gate_up, m_w_down, m_g_ffn_pre, m_g_ffn_post, v_w_in, v_b_forget, v_conv_w, v_g_attn_out, v_g_conv_out, v_w_out, v_g_mix_pre, v_g_mix_post, v_w_gate_up, v_w_down, v_g_ffn_pre, v_g_ffn_post):
    w = dict(w_in=w_in, b_forget=b_forget, conv_w=conv_w, g_attn_out=g_attn_out, g_conv_out=g_conv_out, w_out=w_out,
             g_mix_pre=g_mix_pre, g_mix_post=g_mix_post, w_gate_up=w_gate_up, w_down=w_down, g_ffn_pre=g_ffn_pre,
             g_ffn_post=g_ffn_post)
    m = dict(w_in=m_w_in, b_forget=m_b_forget, conv_w=m_conv_w, g_attn_out=m_g_attn_out, g_conv_out=m_g_conv_out,
             w_out=m_w_out, g_mix_pre=m_g_mix_pre, g_mix_post=m_g_mix_post, w_gate_up=m_w_gate_up, w_down=m_w_down,
             g_ffn_pre=m_g_ffn_pre, g_ffn_post=m_g_ffn_post)
    v = dict(w_in=v_w_in, b_forget=v_b_forget, conv_w=v_conv_w, g_attn_out=v_g_attn_out, g_conv_out=v_g_conv_out,
             w_out=v_w_out, g_mix_pre=v_g_mix_pre, g_mix_post=v_g_mix_post, w_gate_up=v_w_gate_up, w_down=v_w_down,
             g_ffn_pre=v_g_ffn_pre, g_ffn_post=v_g_ffn_post)
    cx, cy, cc = lax.axis_index("x"), lax.axis_index("y"), lax.axis_index("c")
    me = 2 * cx + cy
    c_idx = cc.astype(jnp.int32).reshape(1)
    me_idx = me.astype(jnp.int32).reshape(1)

    stored = lambda a: jnp.transpose(a, (2, 0, 1))
    grad_x, big, small_all = _device_step(x[0], loss_target[0], w, m, v, stored(w_in), stored(m_w_in),
                                          stored(v_w_in), c_idx, me_idx)
    gsum, delta, new_m, new_v = {}, {}, {}, {}
    for n in BIG:
        back = (lambda r: jnp.transpose(r, (1, 2, 0))) if n == "w_in" else (lambda r: r[None])
        gsum[n], delta[n], new_m[n], new_v[n] = [back(r) for r in big[n]]
    loss_sum, small_new = _small_update(me_idx, small_all, w, m, v)
    for n in SMALL_ALL:
        gsum[n], delta[n], new_m[n], new_v[n] = small_new[n]
    loss = 0.5 * loss_sum[0, 0]

    order = ("w_in", "b_forget", "conv_w", "g_attn_out", "g_conv_out", "w_out", "g_mix_pre", "g_mix_post",
             "w_gate_up", "w_down", "g_ffn_pre", "g_ffn_post")
    return (loss, grad_x[None], *[gsum[n] for n in order], *[delta[n] for n in order],
            *[new_m[n] for n in order], *[new_v[n] for n in order])
```
